```python
import math
import jax, jax.numpy as jnp
from jax import lax
import numpy as np

D_MODEL = 1024
BATCH = 16
SEQ = 2048
DEPTH = 4

CHUNK = 64

MIX_WIDTH = D_MODEL
POOL_WIDTH = MIX_WIDTH // 2
SSM_WIDTH = MIX_WIDTH - POOL_WIDTH
POOL_WINDOWS = (2, 4, 8, 16)
N_POOL_GROUPS = len(POOL_WINDOWS)
POOL_GC = POOL_WIDTH // N_POOL_GROUPS
SSM_GC = 16
SSM_GROUPS = SSM_WIDTH // SSM_GC
SSM_STATE = 64
NORM_EPS = 1e-5
DT_MIN = 1e-3
DT_MAX = 1e-1

kernel_name = "hybrid_pool_s5_parallel_groups"


def rmsnorm(x, g):
    xf = x.astype(jnp.float32)
    y = xf * lax.rsqrt(jnp.mean(xf * xf, axis=-1, keepdims=True) + NORM_EPS)
    return (y * g.astype(jnp.float32)).astype(x.dtype)


def pool_branch(u, pool_w, pool_scale):
    bsz, seq, _ = u.shape
    uf = u.astype(jnp.float32).reshape(bsz, seq, N_POOL_GROUPS, POOL_GC)
    cs = jnp.cumsum(uf, axis=1)
    cs0 = jnp.concatenate([jnp.zeros_like(cs[:, :1]), cs], axis=1)
    ks = jnp.array(POOL_WINDOWS, dtype=jnp.int32)
    t1 = jnp.arange(seq, dtype=jnp.int32)[:, None] + 1
    lag_idx = jnp.maximum(t1 - ks[None, :], 0)
    count = jnp.minimum(t1, ks[None, :]).astype(jnp.float32)
    g_idx = jnp.arange(N_POOL_GROUPS)[None, :]
    lagged = cs0[:, lag_idx, g_idx, :]
    mean = (cs0[:, 1:] - lagged) / count[None, :, :, None]
    pooled = (mean - uf).astype(u.dtype)
    y = jnp.einsum('blgc,gcd->blgd', pooled, pool_w)
    return y.reshape(bsz, seq, POOL_WIDTH) * pool_scale


def _complex_affine_combine(e1, e2):
    a1r, a1i, b1r, b1i = e1
    a2r, a2i, b2r, b2i = e2
    ar = a2r * a1r - a2i * a1i
    ai = a2r * a1i + a2i * a1r
    br = a2r * b1r - a2i * b1i + b2r
    bi = a2r * b1i + a2i * b1r + b2i
    return (ar, ai, br, bi)


def ssm_branch(u, a_re, a_im, log_dt, b_re, b_im, c_re, c_im, d_skip, glu_w, glu_b):
    bsz, seq, _ = u.shape
    f32 = jnp.float32
    uf = u.astype(f32).reshape(bsz, seq, SSM_GROUPS, SSM_GC)
    ar = a_re.astype(f32)
    ai = a_im.astype(f32)
    dt = jnp.exp(log_dt.astype(f32))[:, None]
    mag = jnp.exp(ar * dt)
    ang = ai * dt
    lb_re = mag * jnp.cos(ang)
    lb_im = mag * jnp.sin(ang)
    den = ar * ar + ai * ai
    n_re = lb_re - 1.0
    n_im = lb_im
    f_re = (n_re * ar + n_im * ai) / den
    f_im = (n_im * ar - n_re * ai) / den
    br = b_re.astype(f32)
    bi = b_im.astype(f32)
    bb_re = f_re[..., None] * br - f_im[..., None] * bi
    bb_im = f_re[..., None] * bi + f_im[..., None] * br
    bu_re = jnp.einsum('blgc,gpc->blgp', uf, bb_re)
    bu_im = jnp.einsum('blgc,gpc->blgp', uf, bb_im)
    a_full_re = jnp.broadcast_to(lb_re, bu_re.shape)
    a_full_im = jnp.broadcast_to(lb_im, bu_im.shape)
    _, _, s_re, s_im = lax.associative_scan(
        _complex_affine_combine, (a_full_re, a_full_im, bu_re, bu_im), axis=1)
    y = (jnp.einsum('blgp,gcp->blgc', s_re, c_re.astype(f32))
         - jnp.einsum('blgp,gcp->blgc', s_im, c_im.astype(f32)))
    y = y.reshape(bsz, seq, SSM_WIDTH) + d_skip.astype(f32) * uf.reshape(bsz, seq, SSM_WIDTH)
    y = jax.nn.gelu(y).astype(u.dtype)
    return y * jax.nn.sigmoid(y @ glu_w + glu_b)


def _fwd_setup_inputs(seed: int = 0) -> dict:
    key = jax.random.key(seed)
    ks = jax.random.split(key, 20)
    f32 = jnp.float32
    E, D = MIX_WIDTH, D_MODEL
    G, P, C = SSM_GROUPS, SSM_STATE, SSM_GC
    x = jax.random.normal(ks[0], (BATCH, SEQ, D), f32)
    norm_g = 1.0 + 0.02 * jax.random.normal(ks[1], (DEPTH, D), f32)
    w_in = jax.random.normal(ks[2], (DEPTH, D, 2 * E), f32) * D ** -0.5
    pool_w = jax.random.normal(ks[3], (DEPTH, N_POOL_GROUPS, POOL_GC, POOL_GC), f32) * POOL_GC ** -0.5
    pool_scale = 1.0 + 0.02 * jax.random.normal(ks[4], (DEPTH, POOL_WIDTH), f32)
    a_re = -0.5 + 0.01 * jax.random.normal(ks[5], (DEPTH, G, P), f32)
    a_im = (math.pi * jnp.arange(P, dtype=f32))[None, None, :] + 0.01 * jax.random.normal(ks[6], (DEPTH, G, P), f32)
    log_dt = jax.random.uniform(ks[7], (DEPTH, G), f32, math.log(DT_MIN), math.log(DT_MAX))
    b_re = jax.random.normal(ks[8], (DEPTH, G, P, C), f32) * (2.0 * C) ** -0.5
    b_im = jax.random.normal(ks[9], (DEPTH, G, P, C), f32) * (2.0 * C) ** -0.5
    c_re = jax.random.normal(ks[10], (DEPTH, G, C, P), f32) * (2.0 * P) ** -0.5
    c_im = jax.random.normal(ks[11], (DEPTH, G, C, P), f32) * (2.0 * P) ** -0.5
    d_skip = jax.random.normal(ks[12], (DEPTH, SSM_WIDTH), f32)
    glu_w = jax.random.normal(ks[13], (DEPTH, SSM_WIDTH, SSM_WIDTH), f32) * SSM_WIDTH ** -0.5
    glu_b = 0.01 * jax.random.normal(ks[14], (DEPTH, SSM_WIDTH), f32)
    w_out = jax.random.normal(ks[15], (DEPTH, E, D), f32) * E ** -0.5
    final_g = 1.0 + 0.02 * jax.random.normal(ks[16], (D,), f32)
    return {"x": x, "norm_g": norm_g, "w_in": w_in, "pool_w": pool_w, "pool_scale": pool_scale,
            "a_re": a_re, "a_im": a_im, "log_dt": log_dt, "b_re": b_re, "b_im": b_im,
            "c_re": c_re, "c_im": c_im, "d_skip": d_skip, "glu_w": glu_w, "glu_b": glu_b,
            "w_out": w_out, "final_g": final_g}


def _fwd_reference(x, norm_g, w_in, pool_w, pool_scale, a_re, a_im, log_dt, b_re, b_im,
              c_re, c_im, d_skip, glu_w, glu_b, w_out, final_g):
    for l in range(DEPTH):
        h = rmsnorm(x, norm_g[l])
        z = h @ w_in[l]
        u_pool = z[..., :POOL_WIDTH]
        u_ssm = z[..., POOL_WIDTH:MIX_WIDTH]
        gate = jax.nn.silu(z[..., MIX_WIDTH:])
        y_pool = pool_branch(u_pool, pool_w[l], pool_scale[l])
        y_ssm = ssm_branch(u_ssm, a_re[l], a_im[l], log_dt[l], b_re[l], b_im[l],
                           c_re[l], c_im[l], d_skip[l], glu_w[l], glu_b[l])
        y = jnp.concatenate([y_pool, y_ssm.astype(y_pool.dtype)], axis=-1) * gate
        x = x + (y @ w_out[l]).astype(x.dtype)
    return rmsnorm(x, final_g)


import jax as _jax
import jax.numpy as _jnp

TWIN_FORMAT = 'train_step'
FWD_PARAMS = ['x', 'norm_g', 'w_in', 'pool_w', 'pool_scale', 'a_re', 'a_im', 'log_dt', 'b_re', 'b_im', 'c_re', 'c_im', 'd_skip', 'glu_w', 'glu_b', 'w_out', 'final_g']
TWIN_WEIGHTS = ['norm_g', 'w_in', 'pool_w', 'pool_scale', 'a_re', 'a_im', 'log_dt', 'b_re', 'b_im', 'c_re', 'c_im', 'd_skip', 'glu_w', 'glu_b', 'w_out', 'final_g']
TWIN_DIFF_INPUT = 'x'
TWIN_INPUTS = ['x', 'norm_g', 'w_in', 'pool_w', 'pool_scale', 'a_re', 'a_im', 'log_dt', 'b_re', 'b_im', 'c_re', 'c_im', 'd_skip', 'glu_w', 'glu_b', 'w_out', 'final_g', 'loss_target', 'm_norm_g', 'm_w_in', 'm_pool_w', 'm_pool_scale', 'm_a_re', 'm_a_im', 'm_log_dt', 'm_b_re', 'm_b_im', 'm_c_re', 'm_c_im', 'm_d_skip', 'm_glu_w', 'm_glu_b', 'm_w_out', 'm_final_g', 'v_norm_g', 'v_w_in', 'v_pool_w', 'v_pool_scale', 'v_a_re', 'v_a_im', 'v_log_dt', 'v_b_re', 'v_b_im', 'v_c_re', 'v_c_im', 'v_d_skip', 'v_glu_w', 'v_glu_b', 'v_w_out', 'v_final_g']
TWIN_OUTPUTS = ['loss', 'grad_x', 'grad_norm_g', 'grad_w_in', 'grad_pool_w', 'grad_pool_scale', 'grad_a_re', 'grad_a_im', 'grad_log_dt', 'grad_b_re', 'grad_b_im', 'grad_c_re', 'grad_c_im', 'grad_d_skip', 'grad_glu_w', 'grad_glu_b', 'grad_w_out', 'grad_final_g', 'delta_norm_g', 'delta_w_in', 'delta_pool_w', 'delta_pool_scale', 'delta_a_re', 'delta_a_im', 'delta_log_dt', 'delta_b_re', 'delta_b_im', 'delta_c_re', 'delta_c_im', 'delta_d_skip', 'delta_glu_w', 'delta_glu_b', 'delta_w_out', 'delta_final_g', 'new_m_norm_g', 'new_m_w_in', 'new_m_pool_w', 'new_m_pool_scale', 'new_m_a_re', 'new_m_a_im', 'new_m_log_dt', 'new_m_b_re', 'new_m_b_im', 'new_m_c_re', 'new_m_c_im', 'new_m_d_skip', 'new_m_glu_w', 'new_m_glu_b', 'new_m_w_out', 'new_m_final_g', 'new_v_norm_g', 'new_v_w_in', 'new_v_pool_w', 'new_v_pool_scale', 'new_v_a_re', 'new_v_a_im', 'new_v_log_dt', 'new_v_b_re', 'new_v_b_im', 'new_v_c_re', 'new_v_c_im', 'new_v_d_skip', 'new_v_glu_w', 'new_v_glu_b', 'new_v_w_out', 'new_v_final_g']
TWIN_LEAF_KINDS = {'loss': 'loss', 'grad_x': 'grad_x', 'grad_norm_g': 'grad_w', 'grad_w_in': 'grad_w', 'grad_pool_w': 'grad_w', 'grad_pool_scale': 'grad_w', 'grad_a_re': 'grad_w', 'grad_a_im': 'grad_w', 'grad_log_dt': 'grad_w', 'grad_b_re': 'grad_w', 'grad_b_im': 'grad_w', 'grad_c_re': 'grad_w', 'grad_c_im': 'grad_w', 'grad_d_skip': 'grad_w', 'grad_glu_w': 'grad_w', 'grad_glu_b': 'grad_w', 'grad_w_out': 'grad_w', 'grad_final_g': 'grad_w', 'delta_norm_g': 'delta_w', 'delta_w_in': 'delta_w', 'delta_pool_w': 'delta_w', 'delta_pool_scale': 'delta_w', 'delta_a_re': 'delta_w', 'delta_a_im': 'delta_w', 'delta_log_dt': 'delta_w', 'delta_b_re': 'delta_w', 'delta_b_im': 'delta_w', 'delta_c_re': 'delta_w', 'delta_c_im': 'delta_w', 'delta_d_skip': 'delta_w', 'delta_glu_w': 'delta_w', 'delta_glu_b': 'delta_w', 'delta_w_out': 'delta_w', 'delta_final_g': 'delta_w', 'new_m_norm_g': 'new_m', 'new_m_w_in': 'new_m', 'new_m_pool_w': 'new_m', 'new_m_pool_scale': 'new_m', 'new_m_a_re': 'new_m', 'new_m_a_im': 'new_m', 'new_m_log_dt': 'new_m', 'new_m_b_re': 'new_m', 'new_m_b_im': 'new_m', 'new_m_c_re': 'new_m', 'new_m_c_im': 'new_m', 'new_m_d_skip': 'new_m', 'new_m_glu_w': 'new_m', 'new_m_glu_b': 'new_m', 'new_m_w_out': 'new_m', 'new_m_final_g': 'new_m', 'new_v_norm_g': 'new_v', 'new_v_w_in': 'new_v', 'new_v_pool_w': 'new_v', 'new_v_pool_scale': 'new_v', 'new_v_a_re': 'new_v', 'new_v_a_im': 'new_v', 'new_v_log_dt': 'new_v', 'new_v_b_re': 'new_v', 'new_v_b_im': 'new_v', 'new_v_c_re': 'new_v', 'new_v_c_im': 'new_v', 'new_v_d_skip': 'new_v', 'new_v_glu_w': 'new_v', 'new_v_glu_b': 'new_v', 'new_v_w_out': 'new_v', 'new_v_final_g': 'new_v'}


def _forward(args):
    return _fwd_reference(*[args[k] for k in FWD_PARAMS])


def _output_shape():
    out = _jax.eval_shape(lambda: _forward(_fwd_setup_inputs(0)))
    return out.shape, out.dtype

N_MICROBATCH = 1
ADAM_LR = 0.001
ADAM_B1 = 0.9
ADAM_B2 = 0.999
ADAM_EPS = 1e-08
ADAM_WD = 0.01
ADAM_STEP = 10
PER_EXAMPLE_BATCH_AXIS = {'x': 0, 'loss_target': 0}
SHARED_INPUTS = []
_WEIGHT_DTYPES = {'norm_g': _jnp.float32, 'w_in': _jnp.float32, 'pool_w': _jnp.float32, 'pool_scale': _jnp.float32, 'a_re': _jnp.float32, 'a_im': _jnp.float32, 'log_dt': _jnp.float32, 'b_re': _jnp.float32, 'b_im': _jnp.float32, 'c_re': _jnp.float32, 'c_im': _jnp.float32, 'd_skip': _jnp.float32, 'glu_w': _jnp.float32, 'glu_b': _jnp.float32, 'w_out': _jnp.float32, 'final_g': _jnp.float32}
MOMENT_SCALE = {'norm_g': 9.323210e-02, 'w_in': 6.722047e-02, 'pool_w': 8.669141e-02, 'pool_scale': 8.585003e-02, 'a_re': 2.155602e-03, 'a_im': 2.044595e-03, 'log_dt': 8.334635e-01, 'b_re': 1.189861e-03, 'b_im': 1.203819e-03, 'c_re': 2.434764e-03, 'c_im': 2.389917e-03, 'd_skip': 3.817995e-02, 'glu_w': 9.880805e-03, 'glu_b': 1.492556e-02, 'w_out': 6.553040e-02, 'final_g': 3.194128e+01}


def _to_microbatches(a, axis):
    t = _jnp.moveaxis(a, axis, 0)
    t = t.reshape((N_MICROBATCH, t.shape[0] // N_MICROBATCH) + t.shape[1:])
    return _jnp.moveaxis(t, 1, axis + 1)


def setup_inputs(seed: int = 0) -> dict:
    inp = _fwd_setup_inputs(seed)
    key = _jax.random.fold_in(_jax.random.key(seed), 7919)
    shape, _ = _output_shape()
    out = dict(inp)
    out["loss_target"] = _jax.random.normal(_jax.random.fold_in(key, 0), shape, _jnp.float32)
    for i, name in enumerate(TWIN_WEIGHTS):
        w = inp[name].astype(_jnp.float32)
        if MOMENT_SCALE is None:
            s = _jnp.sqrt(_jnp.mean(_jnp.square(w)) + 1e-30)
        else:
            s = MOMENT_SCALE[name]
        km, kv = _jax.random.split(_jax.random.fold_in(key, i + 1))
        out[name] = w
        out["m_" + name] = s * _jax.random.normal(km, w.shape, _jnp.float32)
        out["v_" + name] = (s * s) * _jax.random.uniform(kv, w.shape, _jnp.float32, 0.5, 1.5)
    if N_MICROBATCH > 1:
        for name, axis in PER_EXAMPLE_BATCH_AXIS.items():
            out[name] = _to_microbatches(out[name], axis)
    return {'x': out['x'], 'norm_g': out['norm_g'], 'w_in': out['w_in'], 'pool_w': out['pool_w'], 'pool_scale': out['pool_scale'], 'a_re': out['a_re'], 'a_im': out['a_im'], 'log_dt': out['log_dt'], 'b_re': out['b_re'], 'b_im': out['b_im'], 'c_re': out['c_re'], 'c_im': out['c_im'], 'd_skip': out['d_skip'], 'glu_w': out['glu_w'], 'glu_b': out['glu_b'], 'w_out': out['w_out'], 'final_g': out['final_g'], 'loss_target': out['loss_target'], 'm_norm_g': out['m_norm_g'], 'm_w_in': out['m_w_in'], 'm_pool_w': out['m_pool_w'], 'm_pool_scale': out['m_pool_scale'], 'm_a_re': out['m_a_re'], 'm_a_im': out['m_a_im'], 'm_log_dt': out['m_log_dt'], 'm_b_re': out['m_b_re'], 'm_b_im': out['m_b_im'], 'm_c_re': out['m_c_re'], 'm_c_im': out['m_c_im'], 'm_d_skip': out['m_d_skip'], 'm_glu_w': out['m_glu_w'], 'm_glu_b': out['m_glu_b'], 'm_w_out': out['m_w_out'], 'm_final_g': out['m_final_g'], 'v_norm_g': out['v_norm_g'], 'v_w_in': out['v_w_in'], 'v_pool_w': out['v_pool_w'], 'v_pool_scale': out['v_pool_scale'], 'v_a_re': out['v_a_re'], 'v_a_im': out['v_a_im'], 'v_log_dt': out['v_log_dt'], 'v_b_re': out['v_b_re'], 'v_b_im': out['v_b_im'], 'v_c_re': out['v_c_re'], 'v_c_im': out['v_c_im'], 'v_d_skip': out['v_d_skip'], 'v_glu_w': out['v_glu_w'], 'v_glu_b': out['v_glu_b'], 'v_w_out': out['v_w_out'], 'v_final_g': out['v_final_g']}


def _loss(weights, diff, rest, loss_target):
    with _jax.named_scope("forward"):
        args = {**rest, TWIN_DIFF_INPUT: diff, **{k: w.astype(_WEIGHT_DTYPES[k]) for k, w in weights.items()}}
        y = _forward(args)
    with _jax.named_scope("loss_head"):
        err = _jnp.square(y.astype(_jnp.float32) - loss_target)
        return 0.5 * _jnp.sum(_jnp.mean(err, axis=-1)) if err.ndim else 0.5 * err


def _adamw(w, g, m, v):
    m = ADAM_B1 * m + (1.0 - ADAM_B1) * g
    v = ADAM_B2 * v + (1.0 - ADAM_B2) * _jnp.square(g)
    m_hat = m / (1.0 - ADAM_B1 ** ADAM_STEP)
    v_hat = v / (1.0 - ADAM_B2 ** ADAM_STEP)
    delta = -ADAM_LR * (m_hat / (_jnp.sqrt(v_hat) + ADAM_EPS) + ADAM_WD * w)
    return delta, m, v


def reference(x, norm_g, w_in, pool_w, pool_scale, a_re, a_im, log_dt, b_re, b_im, c_re, c_im, d_skip, glu_w, glu_b, w_out, final_g, loss_target, m_norm_g, m_w_in, m_pool_w, m_pool_scale, m_a_re, m_a_im, m_log_dt, m_b_re, m_b_im, m_c_re, m_c_im, m_d_skip, m_glu_w, m_glu_b, m_w_out, m_final_g, v_norm_g, v_w_in, v_pool_w, v_pool_scale, v_a_re, v_a_im, v_log_dt, v_b_re, v_b_im, v_c_re, v_c_im, v_d_skip, v_glu_w, v_glu_b, v_w_out, v_final_g):
    given = dict(x=x, norm_g=norm_g, w_in=w_in, pool_w=pool_w, pool_scale=pool_scale, a_re=a_re, a_im=a_im, log_dt=log_dt, b_re=b_re, b_im=b_im, c_re=c_re, c_im=c_im, d_skip=d_skip, glu_w=glu_w, glu_b=glu_b, w_out=w_out, final_g=final_g, loss_target=loss_target, m_norm_g=m_norm_g, m_w_in=m_w_in, m_pool_w=m_pool_w, m_pool_scale=m_pool_scale, m_a_re=m_a_re, m_a_im=m_a_im, m_log_dt=m_log_dt, m_b_re=m_b_re, m_b_im=m_b_im, m_c_re=m_c_re, m_c_im=m_c_im, m_d_skip=m_d_skip, m_glu_w=m_glu_w, m_glu_b=m_glu_b, m_w_out=m_w_out, m_final_g=m_final_g, v_norm_g=v_norm_g, v_w_in=v_w_in, v_pool_w=v_pool_w, v_pool_scale=v_pool_scale, v_a_re=v_a_re, v_a_im=v_a_im, v_log_dt=v_log_dt, v_b_re=v_b_re, v_b_im=v_b_im, v_c_re=v_c_re, v_c_im=v_c_im, v_d_skip=v_d_skip, v_glu_w=v_glu_w, v_glu_b=v_glu_b, v_w_out=v_w_out, v_final_g=v_final_g)
    weights = {n: given[n] for n in TWIN_WEIGHTS}
    shared = {n: given[n] for n in SHARED_INPUTS}
    per_example = {n: given[n] for n in ['x']}
    grad_fn = _jax.value_and_grad(_loss, argnums=(0, 1))

    def one_microbatch(ex, loss_target):
        ex = dict(ex)
        diff = ex.pop(TWIN_DIFF_INPUT)
        return grad_fn(weights, diff, {**shared, **ex}, loss_target)

    if N_MICROBATCH == 1:
        loss, (grad_w, grad_x) = one_microbatch(per_example, given["loss_target"])
    else:
        def body(carry, xs):
            loss_sum, grad_sum = carry
            l_k, (gw_k, gx_k) = one_microbatch(xs[0], xs[1])
            with _jax.named_scope("update"):
                return (loss_sum + l_k, _jax.tree.map(_jnp.add, grad_sum, gw_k)), gx_k

        init = (_jnp.zeros((), _jnp.float32), _jax.tree.map(_jnp.zeros_like, weights))
        (loss, grad_w), grad_x = _jax.lax.scan(body, init, (per_example, given["loss_target"]))
    with _jax.named_scope("update"):
        delta_w, new_m, new_v = {}, {}, {}
        for n in TWIN_WEIGHTS:
            delta_w[n], new_m[n], new_v[n] = _adamw(weights[n], grad_w[n], given["m_" + n], given["v_" + n])
    return (loss, grad_x, *[grad_w[n] for n in TWIN_WEIGHTS], *[delta_w[n] for n in TWIN_WEIGHTS],
            *[new_m[n] for n in TWIN_WEIGHTS], *[new_v[n] for n in TWIN_WEIGHTS])
```

```python
import functools
import math

import jax
import jax.numpy as jnp
from jax import lax
from jax.experimental import pallas as pl
from jax.experimental.pallas import tpu as pltpu

F32 = jnp.float32
MXU_DTYPE = jnp.bfloat16

D_MODEL = 1024
MIX = 1024
POOL_W = 512
SSM_W = 512
N_POOL_G = 4
POOL_GC = 128
SSM_G = 32
SSM_C = 16
SSM_P = 64
DEPTH = 4
NORM_EPS = 1e-5
N_DEV = 8

ADAM_LR = 0.001
ADAM_B1 = 0.9
ADAM_B2 = 0.999
ADAM_EPS = 1e-08
ADAM_WD = 0.01
ADAM_STEP = 10

SUBLANES = 8
LANES = 128
HALO = 16
STATE_ROWS = 8
STATE_COLS = 256
T_BLK = 256
TM_FWD = 512
TM_BWD = 256
VMEM_LIMIT = 56 * 1024 * 1024

MESH = pl.DeviceIdType.MESH
VMEM_SPEC = pl.BlockSpec(memory_space=pltpu.VMEM)
ANY_SPEC = pl.BlockSpec(memory_space=pl.ANY)


def _mm(a, b):
    return jnp.dot(a, b, preferred_element_type=F32)


def _mm_tn(a, b):
    return lax.dot_general(a, b, (((0,), (0,)), ((), ())), preferred_element_type=F32)


def _mm_nt(a, b):
    return lax.dot_general(a, b, (((1,), (1,)), ((), ())), preferred_element_type=F32)


def _mx(a):
    return a.astype(MXU_DTYPE)


def _sigmoid(v):
    return 1.0 / (1.0 + jnp.exp(-v))


_GELU_C = math.sqrt(2.0 / math.pi)
_GELU_A = 0.044715


def _gelu_and_grad(y):
    th = jnp.tanh(_GELU_C * (y + _GELU_A * y * y * y))
    val = 0.5 * y * (1.0 + th)
    grad = 0.5 * (1.0 + th) + 0.5 * y * (1.0 - th * th) * (_GELU_C * (1.0 + 3.0 * _GELU_A * y * y))
    return val, grad


def _params(**kw):
    return pltpu.CompilerParams(vmem_limit_bytes=VMEM_LIMIT, **kw)


def _ssm_pack(a_re, a_im, log_dt, b_re, b_im, c_re, c_im):
    dt = jnp.exp(log_dt)[:, None]
    mag = jnp.exp(a_re * dt)
    ang = a_im * dt
    lb_re = mag * jnp.cos(ang)
    lb_im = mag * jnp.sin(ang)
    den = a_re * a_re + a_im * a_im
    n_re = lb_re - 1.0
    n_im = lb_im
    f_re = (n_re * a_re + n_im * a_im) / den
    f_im = (n_im * a_re - n_re * a_im) / den
    bb_re = f_re[..., None] * b_re - f_im[..., None] * b_im
    bb_im = f_re[..., None] * b_im + f_im[..., None] * b_re

    eye4 = jnp.eye(4, dtype=F32)
    parity = (jnp.arange(8)[:, None] % 2 == jnp.arange(2)[None, :]).astype(F32)
    core_b = jnp.stack([bb_re, bb_im], axis=0).reshape(2, 8, 4, SSM_P, SSM_C).transpose(1, 2, 4, 0, 3)
    wb = (core_b[:, None, :, :, :, None, :]
          * eye4[None, None, :, None, None, :, None]
          * parity[:, :, None, None, None, None, None])
    wb = wb.reshape(8, 128, 512)
    core_c = jnp.stack([c_re, -c_im], axis=0).reshape(2, 8, 4, SSM_C, SSM_P).transpose(1, 0, 2, 4, 3)
    wc = (core_c[:, :, :, :, None, None, :]
          * eye4[None, None, :, None, None, :, None]
          * parity[:, None, None, None, :, None, None])
    wc = wc.reshape(8, 512, 128)
    return (lb_re.reshape(STATE_ROWS, STATE_COLS), lb_im.reshape(STATE_ROWS, STATE_COLS), wb, wc)


def _inproj_fwd(x2, g_row, w_all):
    n = x2.shape[0]
    tm = TM_FWD

    def body(x_ref, g_ref, w_ref, z_ref, h_ref):
        x = x_ref[...]
        r = lax.rsqrt(jnp.mean(x * x, axis=-1, keepdims=True) + NORM_EPS)
        h = _mx(x * r * g_ref[...])
        h_ref[...] = h
        for d in range(N_DEV):
            z_ref[:, d * 256:(d + 1) * 256] = _mm(h, w_ref[d])

    return pl.pallas_call(
        body, name="inproj_fwd",
        grid=(n // tm,),
        in_specs=[pl.BlockSpec((tm, D_MODEL), lambda i: (i, 0)),
                  pl.BlockSpec((1, D_MODEL), lambda i: (0, 0)),
                  pl.BlockSpec((N_DEV, D_MODEL, 256), lambda i: (0, 0, 0))],
        out_specs=[pl.BlockSpec((tm, 2 * MIX), lambda i: (i, 0)),
                   pl.BlockSpec((tm, D_MODEL), lambda i: (i, 0))],
        out_shape=[jax.ShapeDtypeStruct((n, 2 * MIX), F32),
                   jax.ShapeDtypeStruct((n, D_MODEL), MXU_DTYPE)],
        compiler_params=_params(dimension_semantics=("arbitrary",)),
    )(x2, g_row, w_all)


def _outproj_fwd(x2, yg, w_out):
    n = x2.shape[0]
    tm = TM_FWD

    def body(x_ref, y_ref, w_ref, o_ref):
        o_ref[...] = x_ref[...] + _mm(y_ref[...], w_ref[...])

    return pl.pallas_call(
        body, name="outproj_fwd",
        grid=(n // tm,),
        in_specs=[pl.BlockSpec((tm, D_MODEL), lambda i: (i, 0)),
                  pl.BlockSpec((tm, MIX), lambda i: (i, 0)),
                  pl.BlockSpec((MIX, D_MODEL), lambda i: (0, 0))],
        out_specs=pl.BlockSpec((tm, D_MODEL), lambda i: (i, 0)),
        out_shape=jax.ShapeDtypeStruct((n, D_MODEL), F32),
        compiler_params=_params(dimension_semantics=("arbitrary",)),
    )(x2, yg, w_out)


def _loss_head(x2, tgt2, g_row):
    n = x2.shape[0]
    tm = TM_FWD

    def body(x_ref, t_ref, g_ref, dx_ref, loss_ref, dg_ref):
        @pl.when(pl.program_id(0) == 0)
        def _():
            loss_ref[...] = jnp.zeros_like(loss_ref)
            dg_ref[...] = jnp.zeros_like(dg_ref)

        x = x_ref[...]
        g = g_ref[...]
        r = lax.rsqrt(jnp.mean(x * x, axis=-1, keepdims=True) + NORM_EPS)
        xh = x * r
        e = xh * g - t_ref[...]
        loss_ref[...] += jnp.sum(jnp.sum(e * e, axis=-1, keepdims=True), axis=0, keepdims=True) * (0.5 / D_MODEL)
        dout = e * (1.0 / D_MODEL)
        dg_ref[...] += jnp.sum(dout * xh, axis=0, keepdims=True)
        gdy = dout * g
        dx_ref[...] = r * (gdy - xh * jnp.mean(xh * gdy, axis=-1, keepdims=True))

    return pl.pallas_call(
        body, name="loss_head",
        grid=(n // tm,),
        in_specs=[pl.BlockSpec((tm, D_MODEL), lambda i: (i, 0)),
                  pl.BlockSpec((tm, D_MODEL), lambda i: (i, 0)),
                  pl.BlockSpec((1, D_MODEL), lambda i: (0, 0))],
        out_specs=[pl.BlockSpec((tm, D_MODEL), lambda i: (i, 0)),
                   pl.BlockSpec((1, 1), lambda i: (0, 0)),
                   pl.BlockSpec((1, D_MODEL), lambda i: (0, 0))],
        out_shape=[jax.ShapeDtypeStruct((n, D_MODEL), F32),
                   jax.ShapeDtypeStruct((1, 1), F32),
                   jax.ShapeDtypeStruct((1, D_MODEL), F32)],
        compiler_params=_params(dimension_semantics=("arbitrary",)),
    )(x2, tgt2, g_row)


def _outproj_bwd(dx2, yg, w_out):
    n = dx2.shape[0]
    tm = TM_BWD

    def body(dx_ref, y_ref, w_ref, dy_ref, dw_ref):
        @pl.when(pl.program_id(0) == 0)
        def _():
            dw_ref[...] = jnp.zeros_like(dw_ref)

        dxb = _mx(dx_ref[...])
        dy_ref[...] = _mm_nt(dxb, w_ref[...])
        dw_ref[...] += _mm_tn(y_ref[...], dxb)

    return pl.pallas_call(
        body, name="outproj_bwd",
        grid=(n // tm,),
        in_specs=[pl.BlockSpec((tm, D_MODEL), lambda i: (i, 0)),
                  pl.BlockSpec((tm, MIX), lambda i: (i, 0)),
                  pl.BlockSpec((MIX, D_MODEL), lambda i: (0, 0))],
        out_specs=[pl.BlockSpec((tm, MIX), lambda i: (i, 0)),
                   pl.BlockSpec((MIX, D_MODEL), lambda i: (0, 0))],
        out_shape=[jax.ShapeDtypeStruct((n, MIX), F32),
                   jax.ShapeDtypeStruct((MIX, D_MODEL), F32)],
        compiler_params=_params(dimension_semantics=("arbitrary",)),
    )(dx2, yg, w_out)


def _inproj_bwd(dz, h, x2, dx_in, g_row, w_all):
    n = x2.shape[0]
    tm = TM_BWD

    def body(dz_ref, h_ref, x_ref, dxi_ref, g_ref, w_ref, dxo_ref, dw_ref, dg_ref):
        @pl.when(pl.program_id(0) == 0)
        def _():
            dw_ref[...] = jnp.zeros_like(dw_ref)
            dg_ref[...] = jnp.zeros_like(dg_ref)

        hb = h_ref[...]
        dh = jnp.zeros((tm, D_MODEL), F32)
        for d in range(N_DEV):
            dzd = dz_ref[:, d * 256:(d + 1) * 256]
            dw_ref[d] += _mm_tn(hb, dzd)
            dh = dh + _mm_nt(dzd, w_ref[d])
        x = x_ref[...]
        r = lax.rsqrt(jnp.mean(x * x, axis=-1, keepdims=True) + NORM_EPS)
        xh = x * r
        dg_ref[...] += jnp.sum(dh * xh, axis=0, keepdims=True)
        gdy = dh * g_ref[...]
        dxo_ref[...] = dxi_ref[...] + r * (gdy - xh * jnp.mean(xh * gdy, axis=-1, keepdims=True))

    return pl.pallas_call(
        body, name="inproj_bwd",
        grid=(n // tm,),
        in_specs=[pl.BlockSpec((tm, 2 * MIX), lambda i: (i, 0)),
                  pl.BlockSpec((tm, D_MODEL), lambda i: (i, 0)),
                  pl.BlockSpec((tm, D_MODEL), lambda i: (i, 0)),
                  pl.BlockSpec((tm, D_MODEL), lambda i: (i, 0)),
                  pl.BlockSpec((1, D_MODEL), lambda i: (0, 0)),
                  pl.BlockSpec((N_DEV, D_MODEL, 256), lambda i: (0, 0, 0))],
        out_specs=[pl.BlockSpec((tm, D_MODEL), lambda i: (i, 0)),
                   pl.BlockSpec((N_DEV, D_MODEL, 256), lambda i: (0, 0, 0)),
                   pl.BlockSpec((1, D_MODEL), lambda i: (0, 0))],
        out_shape=[jax.ShapeDtypeStruct((n, D_MODEL), F32),
                   jax.ShapeDtypeStruct((N_DEV, D_MODEL, 256), F32),
                   jax.ShapeDtypeStruct((1, D_MODEL), F32)],
        compiler_params=_params(dimension_semantics=("arbitrary",)),
    )(dz, h, x2, dx_in, g_row, w_all)


def _row_pos(t0, rows):
    return t0 + lax.broadcasted_iota(jnp.int32, (rows, LANES), 0)


def _pool_window_mean(upad, g, t0, t_blk):
    k = 2 << g
    w = upad
    sh = 1
    while sh < k:
        w = w + pltpu.roll(w, sh, 0)
        sh *= 2
    count = jnp.minimum(_row_pos(t0, t_blk) + 1, k).astype(F32)
    return w[HALO:] / count - upad[HALO:]


def _pool_window_bwd(qpad, g, t_blk):
    k = 2 << g
    n = t_blk + HALO
    w = qpad
    sh = 1
    while sh < k:
        w = w + pltpu.roll(w, n - sh, 0)
        sh *= 2
    return w[:t_blk]


class _StateBuf:
    def __init__(self, refs, t_blk):
        self.refs = refs
        self.t_blk = t_blk

    def put_chunk(self, b, j, val):
        for c in range(4):
            self.refs[4 * b + c][pl.ds(j, self.t_blk, stride=STATE_ROWS), :] = val[:, c * LANES:(c + 1) * LANES]

    def get_chunk(self, b, j):
        return jnp.concatenate(
            [self.refs[4 * b + c][pl.ds(j, self.t_blk, stride=STATE_ROWS), :] for c in range(4)], axis=-1)

    def load(self, b, r, part):
        return jnp.concatenate(
            [self.refs[4 * b + 2 * part + h][pl.ds(r, STATE_ROWS), :] for h in range(2)], axis=-1)

    def store(self, b, r, part, val):
        for h in range(2):
            self.refs[4 * b + 2 * part + h][pl.ds(r, STATE_ROWS), :] = val[:, h * LANES:(h + 1) * LANES]


def _state_scratch(nb, t_blk):
    return [pltpu.VMEM((t_blk * STATE_ROWS, LANES), F32) for _ in range(4 * nb)]


def _ssm_project_in(u_ssm, wb_ref, buf, b):
    ub = _mx(u_ssm)
    for j in range(STATE_ROWS):
        m = j // 2
        buf.put_chunk(b, j, _mm(ub[:, m * LANES:(m + 1) * LANES], wb_ref[j]))


def _scan_forward(buf, lbr, lbi, init, nb):
    def body(t, carry):
        r = pl.multiple_of(t * STATE_ROWS, STATE_ROWS)
        out = []
        for b in range(nb):
            sr, si = carry[2 * b], carry[2 * b + 1]
            nr = lbr * sr - lbi * si + buf.load(b, r, 0)
            ni = lbr * si + lbi * sr + buf.load(b, r, 1)
            buf.store(b, r, 0, nr)
            buf.store(b, r, 1, ni)
            out += [nr, ni]
        return tuple(out)

    return lax.fori_loop(0, buf.t_blk, body, init, unroll=4)


def _ssm_project_out(buf, wc_ref, b):
    tiles = []
    for m in range(4):
        acc = None
        for j in (2 * m, 2 * m + 1):
            part = _mm(_mx(buf.get_chunk(b, j)), wc_ref[j])
            acc = part if acc is None else acc + part
        tiles.append(acc)
    return jnp.concatenate(tiles, axis=-1)


def _mixer_fwd(z3, pool_w, pool_scale, lbr, lbi, wb, wc, d_skip, glu_w, glu_b):
    nb, seq, _ = z3.shape
    t_blk = min(T_BLK, seq)
    n_t = seq // t_blk
    halo_per_blk = t_blk // HALO

    def body(z_ref, zh_ref, pw_ref, ps_ref, lbr_ref, lbi_ref, wb_ref, wc_ref, dsk_ref, gw_ref, gb_ref,
             yg_ref, sb_ref, carry_ref, *s_refs):
        i = pl.program_id(0)
        t0 = i * t_blk
        buf = _StateBuf(s_refs, t_blk)

        @pl.when(i == 0)
        def _():
            carry_ref[...] = jnp.zeros_like(carry_ref)

        sb_ref[0] = carry_ref[...]
        for b in range(nb):
            _ssm_project_in(z_ref[b, :, POOL_W:MIX], wb_ref, buf, b)
        init = tuple(carry_ref[b, :, h * STATE_COLS:(h + 1) * STATE_COLS] for b in range(nb) for h in range(2))
        fin = _scan_forward(buf, lbr_ref[...], lbi_ref[...], init, nb)
        for b in range(nb):
            carry_ref[b, :, 0:STATE_COLS] = fin[2 * b]
            carry_ref[b, :, STATE_COLS:2 * STATE_COLS] = fin[2 * b + 1]

        first = (i == 0)
        for b in range(nb):
            u_ssm = z_ref[b, :, POOL_W:MIX]
            y = _ssm_project_out(buf, wc_ref, b) + dsk_ref[...] * u_ssm
            yg, _ = _gelu_and_grad(y)
            v = _mm(_mx(yg), gw_ref[...]) + gb_ref[...]
            o_ssm = yg * _sigmoid(v)
            gp = z_ref[b, :, MIX + POOL_W:2 * MIX]
            yg_ref[b, :, POOL_W:MIX] = _mx(o_ssm * (gp * _sigmoid(gp)))
            for g in range(N_POOL_G):
                cols = slice(g * POOL_GC, (g + 1) * POOL_GC)
                halo = jnp.where(first, 0.0, zh_ref[b, :, cols])
                upad = jnp.concatenate([halo, z_ref[b, :, cols]], axis=0)
                pooled = _pool_window_mean(upad, g, t0, t_blk)
                yp = _mm(_mx(pooled), pw_ref[g]) * ps_ref[:, cols]
                gpp = z_ref[b, :, MIX + g * POOL_GC:MIX + (g + 1) * POOL_GC]
                yg_ref[b, :, cols] = _mx(yp * (gpp * _sigmoid(gpp)))

    const = lambda *shape: pl.BlockSpec(shape, lambda i: (0,) * len(shape))
    return pl.pallas_call(
        body, name="mixer_fwd",
        grid=(n_t,),
        in_specs=[pl.BlockSpec((nb, t_blk, 2 * MIX), lambda i: (0, i, 0)),
                  pl.BlockSpec((nb, HALO, POOL_W), lambda i: (0, jnp.maximum(i * halo_per_blk - 1, 0), 0)),
                  const(N_POOL_G, POOL_GC, POOL_GC), const(1, POOL_W),
                  const(STATE_ROWS, STATE_COLS), const(STATE_ROWS, STATE_COLS),
                  const(STATE_ROWS, LANES, 2 * STATE_COLS), const(STATE_ROWS, 2 * STATE_COLS, LANES),
                  const(1, SSM_W), const(SSM_W, SSM_W), const(1, SSM_W)],
        out_specs=[pl.BlockSpec((nb, t_blk, MIX), lambda i: (0, i, 0)),
                   pl.BlockSpec((1, nb, STATE_ROWS, 2 * STATE_COLS), lambda i: (i, 0, 0, 0))],
        out_shape=[jax.ShapeDtypeStruct((nb, seq, MIX), MXU_DTYPE),
                   jax.ShapeDtypeStruct((n_t, nb, STATE_ROWS, 2 * STATE_COLS), F32)],
        scratch_shapes=[pltpu.VMEM((nb, STATE_ROWS, 2 * STATE_COLS), F32)] + _state_scratch(nb, t_blk),
        compiler_params=_params(dimension_semantics=("arbitrary",)),
    )(z3, z3, pool_w, pool_scale, lbr, lbi, wb, wc, d_skip, glu_w, glu_b)


def _mixer_bwd(z3, dy3, sbound, pool_w, pool_scale, lbr, lbi, wb, wc, d_skip, glu_w, glu_b):
    nb, seq, _ = z3.shape
    t_blk = min(T_BLK, seq)
    n_t = seq // t_blk
    halo_per_blk = t_blk // HALO

    def body(z_ref, zh_ref, dy_ref, sb_ref, pw_ref, ps_ref, lbr_ref, lbi_ref, wb_ref, wc_ref, dsk_ref,
             gw_ref, gb_ref,
             dz_ref, dpw_ref, dps_ref, dlbr_ref, dlbi_ref, dwb_ref, dwc_ref, ddsk_ref, dgw_ref, dgb_ref,
             gcarry_ref, qcarry_ref, du_ref, *sg_refs):
        i = pl.program_id(0)
        blk = n_t - 1 - i
        t0 = blk * t_blk
        sbuf = _StateBuf(sg_refs[:4 * nb], t_blk)
        gbuf = _StateBuf(sg_refs[4 * nb:], t_blk)

        @pl.when(i == 0)
        def _():
            gcarry_ref[...] = jnp.zeros_like(gcarry_ref)
            qcarry_ref[...] = jnp.zeros_like(qcarry_ref)
            for ref in (dpw_ref, dps_ref, dlbr_ref, dlbi_ref, dwb_ref, dwc_ref, ddsk_ref, dgw_ref, dgb_ref):
                ref[...] = jnp.zeros_like(ref)

        lbr_v = lbr_ref[...]
        lbi_v = lbi_ref[...]

        for b in range(nb):
            _ssm_project_in(z_ref[b, :, POOL_W:MIX], wb_ref, sbuf, b)
        init = tuple(sb_ref[0, b, :, h * STATE_COLS:(h + 1) * STATE_COLS] for b in range(nb) for h in range(2))
        _scan_forward(sbuf, lbr_v, lbi_v, init, nb)

        first = (blk == 0)
        for b in range(nb):
            u_ssm = z_ref[b, :, POOL_W:MIX]
            y = _ssm_project_out(sbuf, wc_ref, b) + dsk_ref[...] * u_ssm
            yg, dgelu = _gelu_and_grad(y)
            ygb = _mx(yg)
            sg = _sigmoid(_mm(ygb, gw_ref[...]) + gb_ref[...])
            o_ssm = yg * sg
            gp = z_ref[b, :, MIX + POOL_W:2 * MIX]
            sgm = _sigmoid(gp)
            dyv = dy_ref[b, :, POOL_W:MIX]
            dz_ref[b, :, MIX + POOL_W:2 * MIX] = _mx(dyv * o_ssm * (sgm * (1.0 + gp * (1.0 - sgm))))
            do = dyv * (gp * sgm)
            dv = do * yg * (sg * (1.0 - sg))
            dvb = _mx(dv)
            dgb_ref[...] += jnp.sum(dv, axis=0, keepdims=True)
            dgw_ref[...] += _mm_tn(ygb, dvb)
            dyp = (do * sg + _mm_nt(dvb, gw_ref[...])) * dgelu
            ddsk_ref[...] += jnp.sum(dyp * u_ssm, axis=0, keepdims=True)
            dypb = _mx(dyp)
            for j in range(STATE_ROWS):
                m = j // 2
                dyt = dypb[:, m * LANES:(m + 1) * LANES]
                gbuf.put_chunk(b, j, _mm_nt(dyt, wc_ref[j]))
                sj = _mx(sbuf.get_chunk(b, j))
                dwc_ref[j] += _mm_tn(sj, dyt)
            du_ref[b] = dsk_ref[...] * dyp

            for g in range(N_POOL_G):
                cols = slice(g * POOL_GC, (g + 1) * POOL_GC)
                halo = jnp.where(first, 0.0, zh_ref[b, :, cols])
                u_g = z_ref[b, :, cols]
                pooled = _pool_window_mean(jnp.concatenate([halo, u_g], axis=0), g, t0, t_blk)
                pb = _mx(pooled)
                ypre = _mm(pb, pw_ref[g])
                gpp = z_ref[b, :, MIX + g * POOL_GC:MIX + (g + 1) * POOL_GC]
                sgp = _sigmoid(gpp)
                dyg = dy_ref[b, :, cols]
                scale = ps_ref[:, cols]
                dz_ref[b, :, MIX + g * POOL_GC:MIX + (g + 1) * POOL_GC] = _mx(
                    dyg * (ypre * scale) * (sgp * (1.0 + gpp * (1.0 - sgp))))
                dyc = dyg * (gpp * sgp)
                dps_ref[:, cols] += jnp.sum(dyc * ypre, axis=0, keepdims=True)
                dypre = _mx(dyc * scale)
                dpw_ref[g] += _mm_tn(pb, dypre)
                dpooled = _mm_nt(dypre, pw_ref[g])
                count = jnp.minimum(_row_pos(t0, t_blk) + 1, 2 << g).astype(F32)
                q = dpooled / count
                qpad = jnp.concatenate([q, qcarry_ref[b, :, cols]], axis=0)
                qcarry_ref[b, :, cols] = q[:HALO]
                dz_ref[b, :, cols] = _mx(_pool_window_bwd(qpad, g, t_blk) - dpooled)

        def rev_step(r, carry, prev):
            dlr, dli = carry[2 * nb], carry[2 * nb + 1]
            out = []
            for b in range(nb):
                gr, gi = carry[2 * b], carry[2 * b + 1]
                ngr = lbr_v * gr + lbi_v * gi + gbuf.load(b, r, 0)
                ngi = lbr_v * gi - lbi_v * gr + gbuf.load(b, r, 1)
                gbuf.store(b, r, 0, ngr)
                gbuf.store(b, r, 1, ngi)
                spr, spi = prev(b)
                dlr = dlr + ngr * spr + ngi * spi
                dli = dli + ngi * spr - ngr * spi
                out += [ngr, ngi]
            return tuple(out) + (dlr, dli)

        def loop_body(k, carry):
            t = t_blk - 1 - k
            rp = pl.multiple_of((t - 1) * STATE_ROWS, STATE_ROWS)
            prev = lambda b: (sbuf.load(b, rp, 0), sbuf.load(b, rp, 1))
            return rev_step(pl.multiple_of(t * STATE_ROWS, STATE_ROWS), carry, prev)

        zero = jnp.zeros((STATE_ROWS, STATE_COLS), F32)
        init_g = tuple(gcarry_ref[b, :, h * STATE_COLS:(h + 1) * STATE_COLS] for b in range(nb) for h in range(2))
        carry = lax.fori_loop(0, t_blk - 1, loop_body, init_g + (zero, zero), unroll=4)
        prev0 = lambda b: (sb_ref[0, b, :, 0:STATE_COLS], sb_ref[0, b, :, STATE_COLS:2 * STATE_COLS])
        carry = rev_step(0, carry, prev0)
        for b in range(nb):
            gcarry_ref[b, :, 0:STATE_COLS] = carry[2 * b]
            gcarry_ref[b, :, STATE_COLS:2 * STATE_COLS] = carry[2 * b + 1]
        dlbr_ref[...] += carry[2 * nb]
        dlbi_ref[...] += carry[2 * nb + 1]

        for b in range(nb):
            ub = _mx(z_ref[b, :, POOL_W:MIX])
            for m in range(4):
                acc = du_ref[b, :, m * LANES:(m + 1) * LANES]
                for j in (2 * m, 2 * m + 1):
                    gj = _mx(gbuf.get_chunk(b, j))
                    acc = acc + _mm_nt(gj, wb_ref[j])
                    dwb_ref[j] += _mm_tn(ub[:, m * LANES:(m + 1) * LANES], gj)
                dz_ref[b, :, POOL_W + m * LANES:POOL_W + (m + 1) * LANES] = _mx(acc)

    const = lambda *shape: pl.BlockSpec(shape, lambda i: (0,) * len(shape))
    rev = lambda i: n_t - 1 - i
    out_shape = [jax.ShapeDtypeStruct((nb, seq, 2 * MIX), MXU_DTYPE),
                 jax.ShapeDtypeStruct((N_POOL_G, POOL_GC, POOL_GC), F32),
                 jax.ShapeDtypeStruct((1, POOL_W), F32),
                 jax.ShapeDtypeStruct((STATE_ROWS, STATE_COLS), F32),
                 jax.ShapeDtypeStruct((STATE_ROWS, STATE_COLS), F32),
                 jax.ShapeDtypeStruct((STATE_ROWS, LANES, 2 * STATE_COLS), F32),
                 jax.ShapeDtypeStruct((STATE_ROWS, 2 * STATE_COLS, LANES), F32),
                 jax.ShapeDtypeStruct((1, SSM_W), F32),
                 jax.ShapeDtypeStruct((SSM_W, SSM_W), F32),
                 jax.ShapeDtypeStruct((1, SSM_W), F32)]
    return pl.pallas_call(
        body, name="mixer_bwd",
        grid=(n_t,),
        in_specs=[pl.BlockSpec((nb, t_blk, 2 * MIX), lambda i: (0, rev(i), 0)),
                  pl.BlockSpec((nb, HALO, POOL_W), lambda i: (0, jnp.maximum(rev(i) * halo_per_blk - 1, 0), 0)),
                  pl.BlockSpec((nb, t_blk, MIX), lambda i: (0, rev(i), 0)),
                  pl.BlockSpec((1, nb, STATE_ROWS, 2 * STATE_COLS), lambda i: (rev(i), 0, 0, 0)),
                  const(N_POOL_G, POOL_GC, POOL_GC), const(1, POOL_W),
                  const(STATE_ROWS, STATE_COLS), const(STATE_ROWS, STATE_COLS),
                  const(STATE_ROWS, LANES, 2 * STATE_COLS), const(STATE_ROWS, 2 * STATE_COLS, LANES),
                  const(1, SSM_W), const(SSM_W, SSM_W), const(1, SSM_W)],
        out_specs=[pl.BlockSpec((nb, t_blk, 2 * MIX), lambda i: (0, rev(i), 0))]
                  + [const(*s.shape) for s in out_shape[1:]],
        out_shape=out_shape,
        scratch_shapes=[pltpu.VMEM((nb, STATE_ROWS, 2 * STATE_COLS), F32),
                        pltpu.VMEM((nb, HALO, POOL_W), F32),
                        pltpu.VMEM((nb, t_blk, SSM_W), F32)]
                       + _state_scratch(nb, t_blk) + _state_scratch(nb, t_blk),
        compiler_params=_params(dimension_semantics=("arbitrary",)),
    )(z3, z3, dy3, sbound, pool_w, pool_scale, lbr, lbi, wb, wc, d_skip, glu_w, glu_b)


def _mesh_place():
    x, y, c = lax.axis_index("x"), lax.axis_index("y"), lax.axis_index("c")
    return x, y, c


def _flip(place, k):
    x, y, c = place
    return (1 - x if k & 4 else x, 1 - y if k & 2 else y, 1 - c if k & 1 else c)


def _index(place):
    x, y, c = place
    return 4 * x + 2 * y + c


def _allgather_weights(w_in, glu_w, w_out):
    shards = (w_in, glu_w, w_out)
    n_kind = len(shards)
    per_kind = 7

    def body(win_ref, glu_ref, wout_ref, o_win, o_glu, o_wout, s_win, s_glu, s_wout, send_sems, recv_sems, local_sems):
        me = _mesh_place()
        sibling = _flip(me, 1)
        chips = [2, 4, 6]
        ins = (win_ref, glu_ref, wout_ref)
        outs = (o_win, o_glu, o_wout)
        stage = (s_win, s_glu, s_wout)

        def slot(ref, place):
            return ref.at[:, pl.ds(_index(place), 1)]

        def copy(kind, k, block, to, src=None):
            return pltpu.make_async_remote_copy(
                src_ref=slot(outs[kind], block) if src is None else src,
                dst_ref=slot(outs[kind], block),
                send_sem=send_sems.at[kind * per_kind + k],
                recv_sem=recv_sems.at[kind * per_kind + k],
                device_id=to, device_id_type=MESH)

        started = []
        local = []
        for kind in range(n_kind):
            stage[kind][:, 0] = _mx(ins[kind][...])
            mine = pltpu.make_async_copy(stage[kind], slot(outs[kind], me), local_sems.at[kind])
            mine.start()
            local.append(mine)
            first = [copy(kind, 0, me, sibling, src=stage[kind])]
            first += [copy(kind, 1 + n, me, _flip(me, k), src=stage[kind]) for n, k in enumerate(chips)]
            for cp in first:
                cp.start()
            started += first
        for kind in range(n_kind):
            for n, k in enumerate(chips):
                copy(kind, 1 + n, _flip(me, k), me).wait_recv()
                fwd = copy(kind, 4 + n, _flip(me, k), sibling)
                fwd.start()
                started.append(fwd)
        for kind in range(n_kind):
            copy(kind, 0, sibling, me).wait_recv()
            for n, k in enumerate(chips):
                copy(kind, 4 + n, _flip(sibling, k), me).wait_recv()
        for cp in started:
            cp.wait_send()
        for cp in local:
            cp.wait()

    out_shape = [jax.ShapeDtypeStruct((s.shape[0], N_DEV) + s.shape[1:], MXU_DTYPE) for s in shards]
    return pl.pallas_call(
        body, name="comm_allgather_weights",
        in_specs=[VMEM_SPEC] * n_kind,
        out_specs=[ANY_SPEC] * n_kind,
        out_shape=out_shape,
        scratch_shapes=[pltpu.VMEM((s.shape[0], 1) + s.shape[1:], MXU_DTYPE) for s in shards]
                       + [pltpu.SemaphoreType.DMA((n_kind * per_kind,)),
                          pltpu.SemaphoreType.DMA((n_kind * per_kind,)),
                          pltpu.SemaphoreType.DMA((n_kind,))],
        compiler_params=_params(),
    )(*shards)


def _exchange_shard_grads(d_win, d_glu, d_wout):
    grads = (d_win, d_glu, d_wout)
    n_kind = len(grads)
    per_kind = 7

    def body(i_win, i_glu, i_wout, o_win, o_glu, o_wout, send_sems, recv_sems, local_sems):
        me = _mesh_place()
        ins = (i_win, i_glu, i_wout)
        outs = (o_win, o_glu, o_wout)
        mine_idx = _index(me)
        copies = []
        for kind in range(n_kind):
            own = pltpu.make_async_copy(ins[kind].at[pl.ds(mine_idx, 1)], outs[kind].at[pl.ds(mine_idx, 1)],
                                        local_sems.at[kind])
            own.start()
            copies.append(own)
            for k in range(1, N_DEV):
                peer = _flip(me, k)
                cp = pltpu.make_async_remote_copy(
                    src_ref=ins[kind].at[pl.ds(_index(peer), 1)],
                    dst_ref=outs[kind].at[pl.ds(mine_idx, 1)],
                    send_sem=send_sems.at[kind * per_kind + k - 1],
                    recv_sem=recv_sems.at[kind * per_kind + k - 1],
                    device_id=peer, device_id_type=MESH)
                cp.start()
                copies.append(cp)
        for cp in copies:
            cp.wait()

    return pl.pallas_call(
        body, name="comm_exchange_shard_grads",
        in_specs=[ANY_SPEC] * n_kind,
        out_specs=[ANY_SPEC] * n_kind,
        out_shape=[jax.ShapeDtypeStruct(g.shape, g.dtype) for g in grads],
        scratch_shapes=[pltpu.SemaphoreType.DMA((n_kind * per_kind,)),
                        pltpu.SemaphoreType.DMA((n_kind * per_kind,)),
                        pltpu.SemaphoreType.DMA((n_kind,))],
        compiler_params=_params(),
    )(*grads)


def _allreduce_packed(p):
    rows = p.shape[0]
    chunk = rows // N_DEV

    def body(p_ref, o_ref, recv_ref, send_sems, recv_sems):
        me = _mesh_place()
        mine = pl.multiple_of(_index(me) * chunk, SUBLANES)
        scatter = []
        for k in range(1, N_DEV):
            peer = _flip(me, k)
            cp = pltpu.make_async_remote_copy(
                src_ref=p_ref.at[pl.ds(pl.multiple_of(_index(peer) * chunk, SUBLANES), chunk)],
                dst_ref=recv_ref.at[k - 1],
                send_sem=send_sems.at[k - 1], recv_sem=recv_sems.at[k - 1],
                device_id=peer, device_id_type=MESH)
            cp.start()
            scatter.append(cp)
        total = p_ref[pl.ds(mine, chunk), :]
        for k in range(1, N_DEV):
            scatter[k - 1].wait()
            total = total + recv_ref[k - 1]
        o_ref[pl.ds(mine, chunk), :] = total
        gather = []
        for k in range(1, N_DEV):
            cp = pltpu.make_async_remote_copy(
                src_ref=o_ref.at[pl.ds(mine, chunk)],
                dst_ref=o_ref.at[pl.ds(mine, chunk)],
                send_sem=send_sems.at[6 + k], recv_sem=recv_sems.at[6 + k],
                device_id=_flip(me, k), device_id_type=MESH)
            cp.start()
            gather.append(cp)
        for k in range(1, N_DEV):
            theirs = pl.multiple_of(_index(_flip(me, k)) * chunk, SUBLANES)
            recv = pltpu.make_async_remote_copy(
                src_ref=o_ref.at[pl.ds(theirs, chunk)], dst_ref=o_ref.at[pl.ds(theirs, chunk)],
                send_sem=send_sems.at[6 + k], recv_sem=recv_sems.at[6 + k],
                device_id=_flip(me, k), device_id_type=MESH)
            recv.wait_recv()
        for cp in gather:
            cp.wait_send()

    return pl.pallas_call(
        body, name="comm_allreduce_packed",
        in_specs=[VMEM_SPEC],
        out_specs=VMEM_SPEC,
        out_shape=jax.ShapeDtypeStruct(p.shape, F32),
        scratch_shapes=[pltpu.VMEM((N_DEV - 1, chunk, LANES), F32),
                        pltpu.SemaphoreType.DMA((2 * (N_DEV - 1),)),
                        pltpu.SemaphoreType.DMA((2 * (N_DEV - 1),))],
        compiler_params=_params(),
    )(p)


def _adamw_math(w, g, m, v):
    m = ADAM_B1 * m + (1.0 - ADAM_B1) * g
    v = ADAM_B2 * v + (1.0 - ADAM_B2) * (g * g)
    m_hat = m / (1.0 - ADAM_B1 ** ADAM_STEP)
    v_hat = v / (1.0 - ADAM_B2 ** ADAM_STEP)
    delta = -ADAM_LR * (m_hat / (jnp.sqrt(v_hat) + ADAM_EPS) + ADAM_WD * w)
    return delta, m, v


def _adamw_summed(parts, w, m, v, name):
    _, r, c = parts.shape
    tr = min(r, 256)

    def body(p_ref, w_ref, m_ref, v_ref, g_ref, d_ref, nm_ref, nv_ref):
        g = p_ref[0]
        for q in range(1, N_DEV):
            g = g + p_ref[q]
        g_ref[...] = g
        d_ref[...], nm_ref[...], nv_ref[...] = _adamw_math(w_ref[...], g, m_ref[...], v_ref[...])

    blk = pl.BlockSpec((tr, c), lambda i: (i, 0))
    return pl.pallas_call(
        body, name=name,
        grid=(r // tr,),
        in_specs=[pl.BlockSpec((N_DEV, tr, c), lambda i: (0, i, 0)), blk, blk, blk],
        out_specs=[blk] * 4,
        out_shape=[jax.ShapeDtypeStruct((r, c), F32)] * 4,
        compiler_params=_params(dimension_semantics=("arbitrary",)),
    )(parts, w, m, v)


def _adamw_packed(w, g, m, v):
    r, c = w.shape
    tr = r // N_DEV

    def body(w_ref, g_ref, m_ref, v_ref, d_ref, nm_ref, nv_ref):
        d_ref[...], nm_ref[...], nv_ref[...] = _adamw_math(w_ref[...], g_ref[...], m_ref[...], v_ref[...])

    blk = pl.BlockSpec((tr, c), lambda i: (i, 0))
    return pl.pallas_call(
        body, name="adamw_packed",
        grid=(r // tr,),
        in_specs=[blk] * 4,
        out_specs=[blk] * 3,
        out_shape=[jax.ShapeDtypeStruct((r, c), F32)] * 3,
        compiler_params=_params(dimension_semantics=("arbitrary",)),
    )(w, g, m, v)


_PACK_ROWS = SUBLANES * N_DEV


def _pack(arrays):
    flat = jnp.concatenate([a.reshape(-1) for a in arrays])
    per = _PACK_ROWS * LANES
    total = -(-flat.shape[0] // per) * per
    flat = jnp.pad(flat, (0, total - flat.shape[0]))
    return flat.reshape(total // LANES, LANES)


def _unpack(packed, like):
    flat = packed.reshape(-1)
    out = []
    off = 0
    for a in like:
        out.append(flat[off:off + a.size].reshape(a.shape))
        off += a.size
    return out


def kernel(x, norm_g, w_in, pool_w, pool_scale, a_re, a_im, log_dt, b_re, b_im, c_re, c_im, d_skip, glu_w, glu_b, w_out, final_g, loss_target, m_norm_g, m_w_in, m_pool_w, m_pool_scale, m_a_re, m_a_im, m_log_dt, m_b_re, m_b_im, m_c_re, m_c_im, m_d_skip, m_glu_w, m_glu_b, m_w_out, m_final_g, v_norm_g, v_w_in, v_pool_w, v_pool_scale, v_a_re, v_a_im, v_log_dt, v_b_re, v_b_im, v_c_re, v_c_im, v_d_skip, v_glu_w, v_glu_b, v_w_out, v_final_g):
    nb, seq, _ = x.shape
    n_tok = nb * seq
    depth = norm_g.shape[0]

    win_all, glu_all, wout_all = _allgather_weights(w_in, glu_w, w_out)

    pack_all = jax.vmap(_ssm_pack)
    (lbr, lbi, wb, wc), pack_vjp = jax.vjp(pack_all, a_re, a_im, log_dt, b_re, b_im, c_re, c_im)
    wb_m, wc_m = _mx(wb), _mx(wc)
    pool_w_m = _mx(pool_w)

    def layer_params(l):
        return (pool_w_m[l], pool_scale[l][None], lbr[l], lbi[l], wb_m[l], wc_m[l], d_skip[l][None],
                glu_all[l].reshape(SSM_W, SSM_W), glu_b[l][None])

    xs = [x.reshape(n_tok, D_MODEL)]
    saved = []
    for l in range(depth):
        z, h = _inproj_fwd(xs[-1], norm_g[l][None], win_all[l])
        z3 = z.reshape(nb, seq, 2 * MIX)
        yg, sbound = _mixer_fwd(z3, *layer_params(l))
        yg2 = yg.reshape(n_tok, MIX)
        xs.append(_outproj_fwd(xs[-1], yg2, wout_all[l].reshape(MIX, D_MODEL)))
        saved.append((z3, h, yg2, sbound))

    dx, loss_part, d_final_g = _loss_head(xs[-1], loss_target.reshape(n_tok, D_MODEL), final_g[None])
    loss = lax.psum(loss_part[0, 0], ("x", "y", "c"))

    small = {k: [None] * depth for k in
             ("norm_g", "pool_w", "pool_scale", "lbr", "lbi", "wb", "wc", "d_skip", "glu_b")}
    shard_out = {k: [None] * depth for k in ("w_in", "glu_w", "w_out")}
    for l in reversed(range(depth)):
        z3, h, yg2, sbound = saved[l]
        dy, d_wout = _outproj_bwd(dx, yg2, wout_all[l].reshape(MIX, D_MODEL))
        (dz, d_pw, d_ps, d_lbr, d_lbi, d_wb, d_wc, d_dsk, d_gw, d_gb) = _mixer_bwd(
            z3, dy.reshape(nb, seq, MIX), sbound, *layer_params(l))
        dx, d_win, d_ng = _inproj_bwd(dz.reshape(n_tok, 2 * MIX), h, xs[l], dx, norm_g[l][None], win_all[l])
        for k, val in (("norm_g", d_ng[0]), ("pool_w", d_pw), ("pool_scale", d_ps[0]), ("lbr", d_lbr),
                       ("lbi", d_lbi), ("wb", d_wb), ("wc", d_wc), ("d_skip", d_dsk[0]), ("glu_b", d_gb[0])):
            small[k][l] = val
        r_win, r_glu, r_wout = _exchange_shard_grads(
            d_win, d_gw.reshape(N_DEV, SSM_W // N_DEV, SSM_W), d_wout.reshape(N_DEV, MIX // N_DEV, D_MODEL))
        shard_out["w_in"][l] = _adamw_summed(r_win, w_in[l], m_w_in[l], v_w_in[l], "adamw_w_in")
        shard_out["glu_w"][l] = _adamw_summed(r_glu, glu_w[l], m_glu_w[l], v_glu_w[l], "adamw_glu_w")
        shard_out["w_out"][l] = _adamw_summed(r_wout, w_out[l], m_w_out[l], v_w_out[l], "adamw_w_out")

    stack = lambda k: jnp.stack(small[k])
    d_a_re, d_a_im, d_log_dt, d_b_re, d_b_im, d_c_re, d_c_im = pack_vjp(
        (stack("lbr"), stack("lbi"), stack("wb"), stack("wc")))
    local_grads = [stack("norm_g"), stack("pool_w"), stack("pool_scale"), d_a_re, d_a_im, d_log_dt,
                   d_b_re, d_b_im, d_c_re, d_c_im, stack("d_skip"), stack("glu_b"), d_final_g[0]]
    small_w = [norm_g, pool_w, pool_scale, a_re, a_im, log_dt, b_re, b_im, c_re, c_im, d_skip, glu_b, final_g]
    small_m = [m_norm_g, m_pool_w, m_pool_scale, m_a_re, m_a_im, m_log_dt, m_b_re, m_b_im, m_c_re, m_c_im,
               m_d_skip, m_glu_b, m_final_g]
    small_v = [v_norm_g, v_pool_w, v_pool_scale, v_a_re, v_a_im, v_log_dt, v_b_re, v_b_im, v_c_re, v_c_im,
               v_d_skip, v_glu_b, v_final_g]
    g_packed = _allreduce_packed(_pack(local_grads))
    d_packed, m_packed, v_packed = _adamw_packed(_pack(small_w), g_packed, _pack(small_m), _pack(small_v))
    names = ["norm_g", "pool_w", "pool_scale", "a_re", "a_im", "log_dt", "b_re", "b_im", "c_re", "c_im",
             "d_skip", "glu_b", "final_g"]
    res = {}
    for kind, packed in (("grad", g_packed), ("delta", d_packed), ("m", m_packed), ("v", v_packed)):
        for n, a in zip(names, _unpack(packed, small_w)):
            res[kind, n] = a
    for n in ("w_in", "glu_w", "w_out"):
        for pos, kind in enumerate(("grad", "delta", "m", "v")):
            res[kind, n] = jnp.stack([shard_out[n][l][pos] for l in range(depth)])

    order = ["norm_g", "w_in", "pool_w", "pool_scale", "a_re", "a_im", "log_dt", "b_re", "b_im", "c_re", "c_im",
             "d_skip", "glu_w", "glu_b", "w_out", "final_g"]
    outs = [loss, dx.reshape(nb, seq, D_MODEL)]
    for kind in ("grad", "delta", "m", "v"):
        outs += [res[kind, n] for n in order]
    return tuple(outs)
```

```python
import functools
import math

import jax
import jax.numpy as jnp
from jax import lax
from jax.experimental import pallas as pl
from jax.experimental.pallas import tpu as pltpu

F32 = jnp.float32
MXU_DTYPE = jnp.bfloat16

D_MODEL = 1024
MIX = 1024
POOL_W = 512
SSM_W = 512
N_POOL_G = 4
POOL_GC = 128
SSM_G = 32
SSM_C = 16
SSM_P = 64
DEPTH = 4
NORM_EPS = 1e-5
N_DEV = 8

ADAM_LR = 0.001
ADAM_B1 = 0.9
ADAM_B2 = 0.999
ADAM_EPS = 1e-08
ADAM_WD = 0.01
ADAM_STEP = 10

SUBLANES = 8
LANES = 128
HALO = 16
STATE_ROWS = 8
STATE_COLS = 256
T_BLK = 256
TM_FWD = 512
TM_BWD = 256
VMEM_LIMIT = 56 * 1024 * 1024

MESH = pl.DeviceIdType.MESH
VMEM_SPEC = pl.BlockSpec(memory_space=pltpu.VMEM)
ANY_SPEC = pl.BlockSpec(memory_space=pl.ANY)


def _mm(a, b):
    return jnp.dot(a, b, preferred_element_type=F32)


def _mm_tn(a, b):
    return lax.dot_general(a, b, (((0,), (0,)), ((), ())), preferred_element_type=F32)


def _mm_nt(a, b):
    return lax.dot_general(a, b, (((1,), (1,)), ((), ())), preferred_element_type=F32)


def _mx(a):
    return a.astype(MXU_DTYPE)


def _sigmoid(v):
    return 1.0 / (1.0 + jnp.exp(-v))


_GELU_C = math.sqrt(2.0 / math.pi)
_GELU_A = 0.044715


def _gelu_and_grad(y):
    th = jnp.tanh(_GELU_C * (y + _GELU_A * y * y * y))
    val = 0.5 * y * (1.0 + th)
    grad = 0.5 * (1.0 + th) + 0.5 * y * (1.0 - th * th) * (_GELU_C * (1.0 + 3.0 * _GELU_A * y * y))
    return val, grad


def _params(**kw):
    return pltpu.CompilerParams(vmem_limit_bytes=VMEM_LIMIT, **kw)


def _ssm_pack(a_re, a_im, log_dt, b_re, b_im, c_re, c_im):
    dt = jnp.exp(log_dt)[:, None]
    mag = jnp.exp(a_re * dt)
    ang = a_im * dt
    lb_re = mag * jnp.cos(ang)
    lb_im = mag * jnp.sin(ang)
    den = a_re * a_re + a_im * a_im
    n_re = lb_re - 1.0
    n_im = lb_im
    f_re = (n_re * a_re + n_im * a_im) / den
    f_im = (n_im * a_re - n_re * a_im) / den
    bb_re = f_re[..., None] * b_re - f_im[..., None] * b_im
    bb_im = f_re[..., None] * b_im + f_im[..., None] * b_re

    row_group = jnp.arange(64) // SSM_C
    col_group = (jnp.arange(512) // SSM_P) % 4
    own_group = (row_group[:, None] == col_group[None, :]).astype(F32)
    even = (jnp.arange(8) % 2 == 0).astype(F32)[:, None, None]

    def chunked(per_channel):
        half = jnp.tile(per_channel, (1, 4, 1)) * own_group
        return jnp.concatenate([half * even, half * (1.0 - even)], axis=1)

    bb = jnp.stack([bb_re, bb_im], axis=0).reshape(2, 8, 4, SSM_P, SSM_C)
    wb = chunked(bb.transpose(1, 4, 0, 2, 3).reshape(8, SSM_C, 512))
    cc = jnp.stack([c_re, -c_im], axis=0).reshape(2, 8, 4, SSM_C, SSM_P)
    wct = chunked(cc.transpose(1, 3, 0, 2, 4).reshape(8, SSM_C, 512))
    return (lb_re.reshape(STATE_ROWS, STATE_COLS), lb_im.reshape(STATE_ROWS, STATE_COLS), wb, wct)


def _inproj_fwd(x2, g_row, w_all, dep):
    n = x2.shape[0]
    tm = TM_FWD

    def body(x_ref, g_ref, w_ref, dep_ref, z_ref, h_ref):
        x = x_ref[...]
        r = lax.rsqrt(jnp.mean(x * x, axis=-1, keepdims=True) + NORM_EPS)
        h = _mx(x * r * g_ref[...])
        h_ref[...] = h
        for d in range(N_DEV):
            z_ref[:, d * 256:(d + 1) * 256] = _mm(h, w_ref[d])

    return pl.pallas_call(
        body, name="inproj_fwd",
        grid=(n // tm,),
        in_specs=[pl.BlockSpec((tm, D_MODEL), lambda i: (i, 0)),
                  pl.BlockSpec((1, D_MODEL), lambda i: (0, 0)),
                  pl.BlockSpec((N_DEV, D_MODEL, 256), lambda i: (0, 0, 0)),
                  ANY_SPEC],
        out_specs=[pl.BlockSpec((tm, 2 * MIX), lambda i: (i, 0)),
                   pl.BlockSpec((tm, D_MODEL), lambda i: (i, 0))],
        out_shape=[jax.ShapeDtypeStruct((n, 2 * MIX), F32),
                   jax.ShapeDtypeStruct((n, D_MODEL), MXU_DTYPE)],
        compiler_params=_params(dimension_semantics=("arbitrary",)),
    )(x2, g_row, w_all, dep)


def _outproj_fwd(x2, yg, w_out):
    n = x2.shape[0]
    tm = TM_FWD

    def body(x_ref, y_ref, w_ref, o_ref):
        o_ref[...] = x_ref[...] + _mm(y_ref[...], w_ref[...])

    return pl.pallas_call(
        body, name="outproj_fwd",
        grid=(n // tm,),
        in_specs=[pl.BlockSpec((tm, D_MODEL), lambda i: (i, 0)),
                  pl.BlockSpec((tm, MIX), lambda i: (i, 0)),
                  pl.BlockSpec((MIX, D_MODEL), lambda i: (0, 0))],
        out_specs=pl.BlockSpec((tm, D_MODEL), lambda i: (i, 0)),
        out_shape=jax.ShapeDtypeStruct((n, D_MODEL), F32),
        compiler_params=_params(dimension_semantics=("arbitrary",)),
    )(x2, yg, w_out)


def _loss_head(x2, tgt2, g_row):
    n = x2.shape[0]
    tm = TM_FWD

    def body(x_ref, t_ref, g_ref, dx_ref, loss_ref, dg_ref):
        @pl.when(pl.program_id(0) == 0)
        def _():
            loss_ref[...] = jnp.zeros_like(loss_ref)
            dg_ref[...] = jnp.zeros_like(dg_ref)

        x = x_ref[...]
        g = g_ref[...]
        r = lax.rsqrt(jnp.mean(x * x, axis=-1, keepdims=True) + NORM_EPS)
        xh = x * r
        e = xh * g - t_ref[...]
        loss_ref[...] += jnp.sum(jnp.sum(e * e, axis=-1, keepdims=True), axis=0, keepdims=True) * (0.5 / D_MODEL)
        dout = e * (1.0 / D_MODEL)
        dg_ref[...] += jnp.sum(dout * xh, axis=0, keepdims=True)
        gdy = dout * g
        dx_ref[...] = r * (gdy - xh * jnp.mean(xh * gdy, axis=-1, keepdims=True))

    return pl.pallas_call(
        body, name="loss_head",
        grid=(n // tm,),
        in_specs=[pl.BlockSpec((tm, D_MODEL), lambda i: (i, 0)),
                  pl.BlockSpec((tm, D_MODEL), lambda i: (i, 0)),
                  pl.BlockSpec((1, D_MODEL), lambda i: (0, 0))],
        out_specs=[pl.BlockSpec((tm, D_MODEL), lambda i: (i, 0)),
                   pl.BlockSpec((1, 1), lambda i: (0, 0)),
                   pl.BlockSpec((1, D_MODEL), lambda i: (0, 0))],
        out_shape=[jax.ShapeDtypeStruct((n, D_MODEL), F32),
                   jax.ShapeDtypeStruct((1, 1), F32),
                   jax.ShapeDtypeStruct((1, D_MODEL), F32)],
        compiler_params=_params(dimension_semantics=("arbitrary",)),
    )(x2, tgt2, g_row)


def _outproj_bwd(dx2, yg, w_out, dep):
    n = dx2.shape[0]
    tm = TM_BWD
    n_steps = n // tm

    def body(dx_ref, y_ref, w_ref, dep_ref, dy_ref, dw_ref, acc_ref):
        i = pl.program_id(0)

        @pl.when(i == 0)
        def _():
            acc_ref[...] = jnp.zeros_like(acc_ref)

        dxb = _mx(dx_ref[...])
        dy_ref[...] = _mm_nt(dxb, w_ref[...])
        acc_ref[...] += _mm_tn(y_ref[...], dxb)

        @pl.when(i == n_steps - 1)
        def _():
            dw_ref[...] = _mx(acc_ref[...])

    return pl.pallas_call(
        body, name="outproj_bwd",
        grid=(n_steps,),
        in_specs=[pl.BlockSpec((tm, D_MODEL), lambda i: (i, 0)),
                  pl.BlockSpec((tm, MIX), lambda i: (i, 0)),
                  pl.BlockSpec((MIX, D_MODEL), lambda i: (0, 0)),
                  ANY_SPEC],
        out_specs=[pl.BlockSpec((tm, MIX), lambda i: (i, 0)),
                   pl.BlockSpec((MIX, D_MODEL), lambda i: (0, 0))],
        out_shape=[jax.ShapeDtypeStruct((n, MIX), F32),
                   jax.ShapeDtypeStruct((MIX, D_MODEL), MXU_DTYPE)],
        scratch_shapes=[pltpu.VMEM((MIX, D_MODEL), F32)],
        compiler_params=_params(dimension_semantics=("arbitrary",)),
    )(dx2, yg, w_out, dep)


def _inproj_bwd(dz, h, x2, dx_in, g_row, w_all):
    n = x2.shape[0]
    tm = TM_BWD
    n_steps = n // tm

    def body(dz_ref, h_ref, x_ref, dxi_ref, g_ref, w_ref, dxo_ref, dw_ref, dg_ref, acc_ref):
        i = pl.program_id(0)

        @pl.when(i == 0)
        def _():
            acc_ref[...] = jnp.zeros_like(acc_ref)
            dg_ref[...] = jnp.zeros_like(dg_ref)

        hb = h_ref[...]
        dh = jnp.zeros((tm, D_MODEL), F32)
        for d in range(N_DEV):
            dzd = dz_ref[:, d * 256:(d + 1) * 256]
            acc_ref[d] += _mm_tn(hb, dzd)
            dh = dh + _mm_nt(dzd, w_ref[d])
        x = x_ref[...]
        r = lax.rsqrt(jnp.mean(x * x, axis=-1, keepdims=True) + NORM_EPS)
        xh = x * r
        dg_ref[...] += jnp.sum(dh * xh, axis=0, keepdims=True)
        gdy = dh * g_ref[...]
        dxo_ref[...] = dxi_ref[...] + r * (gdy - xh * jnp.mean(xh * gdy, axis=-1, keepdims=True))

        @pl.when(i == n_steps - 1)
        def _():
            dw_ref[...] = _mx(acc_ref[...])

    return pl.pallas_call(
        body, name="inproj_bwd",
        grid=(n_steps,),
        in_specs=[pl.BlockSpec((tm, 2 * MIX), lambda i: (i, 0)),
                  pl.BlockSpec((tm, D_MODEL), lambda i: (i, 0)),
                  pl.BlockSpec((tm, D_MODEL), lambda i: (i, 0)),
                  pl.BlockSpec((tm, D_MODEL), lambda i: (i, 0)),
                  pl.BlockSpec((1, D_MODEL), lambda i: (0, 0)),
                  pl.BlockSpec((N_DEV, D_MODEL, 256), lambda i: (0, 0, 0))],
        out_specs=[pl.BlockSpec((tm, D_MODEL), lambda i: (i, 0)),
                   pl.BlockSpec((N_DEV, D_MODEL, 256), lambda i: (0, 0, 0)),
                   pl.BlockSpec((1, D_MODEL), lambda i: (0, 0))],
        out_shape=[jax.ShapeDtypeStruct((n, D_MODEL), F32),
                   jax.ShapeDtypeStruct((N_DEV, D_MODEL, 256), MXU_DTYPE),
                   jax.ShapeDtypeStruct((1, D_MODEL), F32)],
        scratch_shapes=[pltpu.VMEM((N_DEV, D_MODEL, 256), F32)],
        compiler_params=_params(dimension_semantics=("arbitrary",)),
    )(dz, h, x2, dx_in, g_row, w_all)


def _row_pos(t0, rows):
    return t0 + lax.broadcasted_iota(jnp.int32, (rows, LANES), 0)


def _pool_window_mean(upad, g, t0, t_blk):
    k = 2 << g
    w = upad
    sh = 1
    while sh < k:
        w = w + pltpu.roll(w, sh, 0)
        sh *= 2
    count = jnp.minimum(_row_pos(t0, t_blk) + 1, k).astype(F32)
    return w[HALO:] / count - upad[HALO:]


def _pool_window_bwd(qpad, g, t_blk):
    k = 2 << g
    n = t_blk + HALO
    w = qpad
    sh = 1
    while sh < k:
        w = w + pltpu.roll(w, n - sh, 0)
        sh *= 2
    return w[:t_blk]


class _StateBuf:
    def __init__(self, refs, t_blk):
        self.refs = refs
        self.t_blk = t_blk

    def put_chunk(self, b, j, val):
        for c in range(4):
            self.refs[4 * b + c][pl.ds(j, self.t_blk, stride=STATE_ROWS), :] = val[:, c * LANES:(c + 1) * LANES]

    def get_chunk(self, b, j):
        return jnp.concatenate(
            [self.refs[4 * b + c][pl.ds(j, self.t_blk, stride=STATE_ROWS), :] for c in range(4)], axis=-1)

    def load(self, b, r, part):
        return jnp.concatenate(
            [self.refs[4 * b + 2 * part + h][pl.ds(r, STATE_ROWS), :] for h in range(2)], axis=-1)

    def store(self, b, r, part, val):
        for h in range(2):
            self.refs[4 * b + 2 * part + h][pl.ds(r, STATE_ROWS), :] = val[:, h * LANES:(h + 1) * LANES]


def _state_scratch(nb, t_blk):
    return [pltpu.VMEM((t_blk * STATE_ROWS, LANES), F32) for _ in range(4 * nb)]


def _ssm_project_in(u_ssm, wb_ref, buf, b):
    ub = _mx(u_ssm)
    for j in range(STATE_ROWS):
        m = j // 2
        buf.put_chunk(b, j, _mm(ub[:, m * LANES:(m + 1) * LANES], wb_ref[j]))


def _scan_forward(buf, lbr, lbi, init, nb):
    def body(t, carry):
        r = pl.multiple_of(t * STATE_ROWS, STATE_ROWS)
        out = []
        for b in range(nb):
            sr, si = carry[2 * b], carry[2 * b + 1]
            nr = lbr * sr - lbi * si + buf.load(b, r, 0)
            ni = lbr * si + lbi * sr + buf.load(b, r, 1)
            buf.store(b, r, 0, nr)
            buf.store(b, r, 1, ni)
            out += [nr, ni]
        return tuple(out)

    return lax.fori_loop(0, buf.t_blk, body, init, unroll=4)


def _ssm_project_out(buf, wc_ref, b):
    tiles = []
    for m in range(4):
        acc = None
        for j in (2 * m, 2 * m + 1):
            part = _mm_nt(_mx(buf.get_chunk(b, j)), wc_ref[j])
            acc = part if acc is None else acc + part
        tiles.append(acc)
    return jnp.concatenate(tiles, axis=-1)


def _mixer_fwd(z3, pool_w, pool_scale, lbr, lbi, wb, wc, d_skip, glu_w, glu_b):
    nb, seq, _ = z3.shape
    t_blk = min(T_BLK, seq)
    n_t = seq // t_blk
    halo_per_blk = t_blk // HALO

    def body(z_ref, zh_ref, pw_ref, ps_ref, lbr_ref, lbi_ref, wb_ref, wc_ref, dsk_ref, gw_ref, gb_ref,
             yg_ref, sb_ref, carry_ref, *s_refs):
        i = pl.program_id(0)
        t0 = i * t_blk
        buf = _StateBuf(s_refs, t_blk)

        @pl.when(i == 0)
        def _():
            carry_ref[...] = jnp.zeros_like(carry_ref)

        sb_ref[0] = carry_ref[...]
        for b in range(nb):
            _ssm_project_in(z_ref[b, :, POOL_W:MIX], wb_ref, buf, b)
        init = tuple(carry_ref[b, :, h * STATE_COLS:(h + 1) * STATE_COLS] for b in range(nb) for h in range(2))
        fin = _scan_forward(buf, lbr_ref[...], lbi_ref[...], init, nb)
        for b in range(nb):
            carry_ref[b, :, 0:STATE_COLS] = fin[2 * b]
            carry_ref[b, :, STATE_COLS:2 * STATE_COLS] = fin[2 * b + 1]

        first = (i == 0)
        for b in range(nb):
            u_ssm = z_ref[b, :, POOL_W:MIX]
            y = _ssm_project_out(buf, wc_ref, b) + dsk_ref[...] * u_ssm
            yg, _ = _gelu_and_grad(y)
            v = _mm(_mx(yg), gw_ref[...]) + gb_ref[...]
            o_ssm = yg * _sigmoid(v)
            gp = z_ref[b, :, MIX + POOL_W:2 * MIX]
            yg_ref[b, :, POOL_W:MIX] = _mx(o_ssm * (gp * _sigmoid(gp)))
            for g in range(N_POOL_G):
                cols = slice(g * POOL_GC, (g + 1) * POOL_GC)
                halo = jnp.where(first, 0.0, zh_ref[b, :, cols])
                upad = jnp.concatenate([halo, z_ref[b, :, cols]], axis=0)
                pooled = _pool_window_mean(upad, g, t0, t_blk)
                yp = _mm(_mx(pooled), pw_ref[g]) * ps_ref[:, cols]
                gpp = z_ref[b, :, MIX + g * POOL_GC:MIX + (g + 1) * POOL_GC]
                yg_ref[b, :, cols] = _mx(yp * (gpp * _sigmoid(gpp)))

    const = lambda *shape: pl.BlockSpec(shape, lambda i: (0,) * len(shape))
    return pl.pallas_call(
        body, name="mixer_fwd",
        grid=(n_t,),
        in_specs=[pl.BlockSpec((nb, t_blk, 2 * MIX), lambda i: (0, i, 0)),
                  pl.BlockSpec((nb, HALO, POOL_W), lambda i: (0, jnp.maximum(i * halo_per_blk - 1, 0), 0)),
                  const(N_POOL_G, POOL_GC, POOL_GC), const(1, POOL_W),
                  const(STATE_ROWS, STATE_COLS), const(STATE_ROWS, STATE_COLS),
                  const(STATE_ROWS, LANES, 2 * STATE_COLS), const(STATE_ROWS, LANES, 2 * STATE_COLS),
                  const(1, SSM_W), const(SSM_W, SSM_W), const(1, SSM_W)],
        out_specs=[pl.BlockSpec((nb, t_blk, MIX), lambda i: (0, i, 0)),
                   pl.BlockSpec((1, nb, STATE_ROWS, 2 * STATE_COLS), lambda i: (i, 0, 0, 0))],
        out_shape=[jax.ShapeDtypeStruct((nb, seq, MIX), MXU_DTYPE),
                   jax.ShapeDtypeStruct((n_t, nb, STATE_ROWS, 2 * STATE_COLS), F32)],
        scratch_shapes=[pltpu.VMEM((nb, STATE_ROWS, 2 * STATE_COLS), F32)] + _state_scratch(nb, t_blk),
        compiler_params=_params(dimension_semantics=("arbitrary",)),
    )(z3, z3, pool_w, pool_scale, lbr, lbi, wb, wc, d_skip, glu_w, glu_b)


def _mixer_bwd(z3, dy3, sbound, pool_w, pool_scale, lbr, lbi, wb, wc, d_skip, glu_w, glu_b):
    nb, seq, _ = z3.shape
    t_blk = min(T_BLK, seq)
    n_t = seq // t_blk
    halo_per_blk = t_blk // HALO

    def body(z_ref, zh_ref, dy_ref, sb_ref, pw_ref, ps_ref, lbr_ref, lbi_ref, wb_ref, wc_ref, dsk_ref,
             gw_ref, gb_ref,
             dz_ref, dpw_ref, dps_ref, dlbr_ref, dlbi_ref, dwb_ref, dwc_ref, ddsk_ref, dgw_ref, dgb_ref,
             gcarry_ref, qcarry_ref, du_ref, dgw_acc, *sg_refs):
        i = pl.program_id(0)
        blk = n_t - 1 - i
        t0 = blk * t_blk
        sbuf = _StateBuf(sg_refs[:4 * nb], t_blk)
        gbuf = _StateBuf(sg_refs[4 * nb:], t_blk)

        @pl.when(i == 0)
        def _():
            gcarry_ref[...] = jnp.zeros_like(gcarry_ref)
            qcarry_ref[...] = jnp.zeros_like(qcarry_ref)
            for ref in (dpw_ref, dps_ref, dlbr_ref, dlbi_ref, dwb_ref, dwc_ref, ddsk_ref, dgw_acc, dgb_ref):
                ref[...] = jnp.zeros_like(ref)

        lbr_v = lbr_ref[...]
        lbi_v = lbi_ref[...]

        for b in range(nb):
            _ssm_project_in(z_ref[b, :, POOL_W:MIX], wb_ref, sbuf, b)
        init = tuple(sb_ref[0, b, :, h * STATE_COLS:(h + 1) * STATE_COLS] for b in range(nb) for h in range(2))
        _scan_forward(sbuf, lbr_v, lbi_v, init, nb)

        first = (blk == 0)
        for b in range(nb):
            u_ssm = z_ref[b, :, POOL_W:MIX]
            y = _ssm_project_out(sbuf, wc_ref, b) + dsk_ref[...] * u_ssm
            yg, dgelu = _gelu_and_grad(y)
            ygb = _mx(yg)
            sg = _sigmoid(_mm(ygb, gw_ref[...]) + gb_ref[...])
            o_ssm = yg * sg
            gp = z_ref[b, :, MIX + POOL_W:2 * MIX]
            sgm = _sigmoid(gp)
            dyv = dy_ref[b, :, POOL_W:MIX]
            dz_ref[b, :, MIX + POOL_W:2 * MIX] = _mx(dyv * o_ssm * (sgm * (1.0 + gp * (1.0 - sgm))))
            do = dyv * (gp * sgm)
            dv = do * yg * (sg * (1.0 - sg))
            dvb = _mx(dv)
            dgb_ref[...] += jnp.sum(dv, axis=0, keepdims=True)
            dgw_acc[...] += _mm_tn(ygb, dvb)
            dyp = (do * sg + _mm_nt(dvb, gw_ref[...])) * dgelu
            ddsk_ref[...] += jnp.sum(dyp * u_ssm, axis=0, keepdims=True)
            dypb = _mx(dyp)
            for j in range(STATE_ROWS):
                m = j // 2
                dyt = dypb[:, m * LANES:(m + 1) * LANES]
                gbuf.put_chunk(b, j, _mm(dyt, wc_ref[j]))
                sj = _mx(sbuf.get_chunk(b, j))
                dwc_ref[j] += _mm_tn(dyt, sj)
            du_ref[b] = dsk_ref[...] * dyp

            for g in range(N_POOL_G):
                cols = slice(g * POOL_GC, (g + 1) * POOL_GC)
                halo = jnp.where(first, 0.0, zh_ref[b, :, cols])
                u_g = z_ref[b, :, cols]
                pooled = _pool_window_mean(jnp.concatenate([halo, u_g], axis=0), g, t0, t_blk)
                pb = _mx(pooled)
                ypre = _mm(pb, pw_ref[g])
                gpp = z_ref[b, :, MIX + g * POOL_GC:MIX + (g + 1) * POOL_GC]
                sgp = _sigmoid(gpp)
                dyg = dy_ref[b, :, cols]
                scale = ps_ref[:, cols]
                dz_ref[b, :, MIX + g * POOL_GC:MIX + (g + 1) * POOL_GC] = _mx(
                    dyg * (ypre * scale) * (sgp * (1.0 + gpp * (1.0 - sgp))))
                dyc = dyg * (gpp * sgp)
                dps_ref[:, cols] += jnp.sum(dyc * ypre, axis=0, keepdims=True)
                dypre = _mx(dyc * scale)
                dpw_ref[g] += _mm_tn(pb, dypre)
                dpooled = _mm_nt(dypre, pw_ref[g])
                count = jnp.minimum(_row_pos(t0, t_blk) + 1, 2 << g).astype(F32)
                q = dpooled / count
                qpad = jnp.concatenate([q, qcarry_ref[b, :, cols]], axis=0)
                qcarry_ref[b, :, cols] = q[:HALO]
                dz_ref[b, :, cols] = _mx(_pool_window_bwd(qpad, g, t_blk) - dpooled)

        def rev_step(r, carry, prev):
            dlr, dli = carry[2 * nb], carry[2 * nb + 1]
            out = []
            for b in range(nb):
                gr, gi = carry[2 * b], carry[2 * b + 1]
                ngr = lbr_v * gr + lbi_v * gi + gbuf.load(b, r, 0)
                ngi = lbr_v * gi - lbi_v * gr + gbuf.load(b, r, 1)
                gbuf.store(b, r, 0, ngr)
                gbuf.store(b, r, 1, ngi)
                spr, spi = prev(b)
                dlr = dlr + ngr * spr + ngi * spi
                dli = dli + ngi * spr - ngr * spi
                out += [ngr, ngi]
            return tuple(out) + (dlr, dli)

        def loop_body(k, carry):
            t = t_blk - 1 - k
            rp = pl.multiple_of((t - 1) * STATE_ROWS, STATE_ROWS)
            prev = lambda b: (sbuf.load(b, rp, 0), sbuf.load(b, rp, 1))
            return rev_step(pl.multiple_of(t * STATE_ROWS, STATE_ROWS), carry, prev)

        zero = jnp.zeros((STATE_ROWS, STATE_COLS), F32)
        init_g = tuple(gcarry_ref[b, :, h * STATE_COLS:(h + 1) * STATE_COLS] for b in range(nb) for h in range(2))
        carry = lax.fori_loop(0, t_blk - 1, loop_body, init_g + (zero, zero), unroll=4)
        prev0 = lambda b: (sb_ref[0, b, :, 0:STATE_COLS], sb_ref[0, b, :, STATE_COLS:2 * STATE_COLS])
        carry = rev_step(0, carry, prev0)
        for b in range(nb):
            gcarry_ref[b, :, 0:STATE_COLS] = carry[2 * b]
            gcarry_ref[b, :, STATE_COLS:2 * STATE_COLS] = carry[2 * b + 1]
        dlbr_ref[...] += carry[2 * nb]
        dlbi_ref[...] += carry[2 * nb + 1]

        for b in range(nb):
            ub = _mx(z_ref[b, :, POOL_W:MIX])
            for m in range(4):
                acc = du_ref[b, :, m * LANES:(m + 1) * LANES]
                for j in (2 * m, 2 * m + 1):
                    gj = _mx(gbuf.get_chunk(b, j))
                    acc = acc + _mm_nt(gj, wb_ref[j])
                    dwb_ref[j] += _mm_tn(ub[:, m * LANES:(m + 1) * LANES], gj)
                dz_ref[b, :, POOL_W + m * LANES:POOL_W + (m + 1) * LANES] = _mx(acc)

        @pl.when(i == n_t - 1)
        def _():
            dgw_ref[...] = _mx(dgw_acc[...])

    const = lambda *shape: pl.BlockSpec(shape, lambda i: (0,) * len(shape))
    rev = lambda i: n_t - 1 - i
    out_shape = [jax.ShapeDtypeStruct((nb, seq, 2 * MIX), MXU_DTYPE),
                 jax.ShapeDtypeStruct((N_POOL_G, POOL_GC, POOL_GC), F32),
                 jax.ShapeDtypeStruct((1, POOL_W), F32),
                 jax.ShapeDtypeStruct((STATE_ROWS, STATE_COLS), F32),
                 jax.ShapeDtypeStruct((STATE_ROWS, STATE_COLS), F32),
                 jax.ShapeDtypeStruct((STATE_ROWS, LANES, 2 * STATE_COLS), F32),
                 jax.ShapeDtypeStruct((STATE_ROWS, LANES, 2 * STATE_COLS), F32),
                 jax.ShapeDtypeStruct((1, SSM_W), F32),
                 jax.ShapeDtypeStruct((SSM_W, SSM_W), MXU_DTYPE),
                 jax.ShapeDtypeStruct((1, SSM_W), F32)]
    return pl.pallas_call(
        body, name="mixer_bwd",
        grid=(n_t,),
        in_specs=[pl.BlockSpec((nb, t_blk, 2 * MIX), lambda i: (0, rev(i), 0)),
                  pl.BlockSpec((nb, HALO, POOL_W), lambda i: (0, jnp.maximum(rev(i) * halo_per_blk - 1, 0), 0)),
                  pl.BlockSpec((nb, t_blk, MIX), lambda i: (0, rev(i), 0)),
                  pl.BlockSpec((1, nb, STATE_ROWS, 2 * STATE_COLS), lambda i: (rev(i), 0, 0, 0)),
                  const(N_POOL_G, POOL_GC, POOL_GC), const(1, POOL_W),
                  const(STATE_ROWS, STATE_COLS), const(STATE_ROWS, STATE_COLS),
                  const(STATE_ROWS, LANES, 2 * STATE_COLS), const(STATE_ROWS, LANES, 2 * STATE_COLS),
                  const(1, SSM_W), const(SSM_W, SSM_W), const(1, SSM_W)],
        out_specs=[pl.BlockSpec((nb, t_blk, 2 * MIX), lambda i: (0, rev(i), 0))]
                  + [const(*s.shape) for s in out_shape[1:]],
        out_shape=out_shape,
        scratch_shapes=[pltpu.VMEM((nb, STATE_ROWS, 2 * STATE_COLS), F32),
                        pltpu.VMEM((nb, HALO, POOL_W), F32),
                        pltpu.VMEM((nb, t_blk, SSM_W), F32),
                        pltpu.VMEM((SSM_W, SSM_W), F32)]
                       + _state_scratch(nb, t_blk) + _state_scratch(nb, t_blk),
        compiler_params=_params(dimension_semantics=("arbitrary",)),
    )(z3, z3, dy3, sbound, pool_w, pool_scale, lbr, lbi, wb, wc, d_skip, glu_w, glu_b)


def _mesh_place():
    x, y, c = lax.axis_index("x"), lax.axis_index("y"), lax.axis_index("c")
    return x, y, c


def _flip(place, k):
    x, y, c = place
    return (1 - x if k & 4 else x, 1 - y if k & 2 else y, 1 - c if k & 1 else c)


def _index(place):
    x, y, c = place
    return 4 * x + 2 * y + c


HBM_SPEC = pl.BlockSpec(memory_space=pltpu.HBM)
SEM_SPEC = pl.BlockSpec(memory_space=pltpu.SEMAPHORE)
_EFFECT = pltpu.SideEffectType.DATAFLOW_SIDE_EFFECTING
N_PEERS = N_DEV - 1


def _exchange_copies(src_refs, land_refs, send_sems, recv_sems, gather):
    me = _mesh_place()
    mine = _index(me)
    out = []
    for a, (src_ref, land_ref) in enumerate(zip(src_refs, land_refs)):
        for k in range(1, N_DEV):
            peer = _flip(me, k)
            theirs = _index(peer)
            n = a * N_PEERS + k - 1
            src = src_ref if gather else src_ref.at[theirs]
            send = pltpu.make_async_remote_copy(
                src_ref=src, dst_ref=land_ref.at[mine], send_sem=send_sems.at[n], recv_sem=recv_sems.at[n],
                device_id=peer, device_id_type=MESH)
            recv = pltpu.make_async_remote_copy(
                src_ref=src, dst_ref=land_ref.at[theirs], send_sem=send_sems.at[n], recv_sem=recv_sems.at[n],
                device_id=peer, device_id_type=MESH)
            out.append((send, recv))
    return out


def _exchange_start(srcs, lands, after, name, gather):
    n = len(srcs)

    def body(*refs):
        src_refs, land_refs = refs[:n], refs[n:2 * n]
        send_sems, recv_sems = refs[2 * n + 1], refs[2 * n + 2]
        token = refs[-1]
        for send, _ in _exchange_copies(src_refs, land_refs, send_sems, recv_sems, gather):
            send.start()
        token[...] = jnp.zeros_like(token)

    thru = [pltpu.HBM(a.shape, a.dtype) for a in tuple(srcs) + tuple(lands)]
    res = pl.pallas_call(
        body, name=name,
        in_specs=[HBM_SPEC] * (2 * n) + [ANY_SPEC],
        out_specs=[SEM_SPEC, SEM_SPEC] + [HBM_SPEC] * (2 * n) + [VMEM_SPEC],
        out_shape=[pltpu.SemaphoreType.DMA((n * N_PEERS,)), pltpu.SemaphoreType.DMA((n * N_PEERS,))]
                  + thru + [jax.ShapeDtypeStruct((SUBLANES, LANES), F32)],
        input_output_aliases={i: 2 + i for i in range(2 * n)},
        compiler_params=pltpu.CompilerParams(has_side_effects=_EFFECT),
    )(*[pltpu.with_memory_space_constraint(a, pltpu.HBM) for a in tuple(srcs) + tuple(lands)], after)
    return tuple(res[:-1]), res[-1]


def _exchange_wait(handle, after, name, gather):
    send_sems, recv_sems = handle[0], handle[1]
    arrays = handle[2:]
    n = len(arrays) // 2

    def body(*refs):
        src_refs, land_refs = refs[:n], refs[n:2 * n]
        for send, recv in _exchange_copies(src_refs, land_refs, refs[2 * n], refs[2 * n + 1], gather):
            send.wait_send()
            recv.wait_recv()

    res = pl.pallas_call(
        body, name=name,
        in_specs=[HBM_SPEC] * (2 * n) + [SEM_SPEC, SEM_SPEC, ANY_SPEC],
        out_specs=[HBM_SPEC] * (2 * n),
        out_shape=[pltpu.HBM(a.shape, a.dtype) for a in arrays],
        input_output_aliases={i: i for i in range(2 * n)},
        compiler_params=pltpu.CompilerParams(has_side_effects=_EFFECT),
    )(*arrays, send_sems, recv_sems, after)
    return tuple(res[n:])


def _with_own_slot(block, idx):
    zone = jnp.zeros((N_DEV,) + block.shape, block.dtype)
    return lax.dynamic_update_slice(zone, block[None], (idx,) + (0,) * block.ndim)


def _allreduce_packed(p):
    rows = p.shape[0]
    chunk = rows // N_DEV

    def body(p_ref, o_ref, recv_ref, send_sems, recv_sems):
        me = _mesh_place()
        mine = pl.multiple_of(_index(me) * chunk, SUBLANES)
        scatter = []
        for k in range(1, N_DEV):
            peer = _flip(me, k)
            cp = pltpu.make_async_remote_copy(
                src_ref=p_ref.at[pl.ds(pl.multiple_of(_index(peer) * chunk, SUBLANES), chunk)],
                dst_ref=recv_ref.at[k - 1],
                send_sem=send_sems.at[k - 1], recv_sem=recv_sems.at[k - 1],
                device_id=peer, device_id_type=MESH)
            cp.start()
            scatter.append(cp)
        total = p_ref[pl.ds(mine, chunk), :]
        for k in range(1, N_DEV):
            scatter[k - 1].wait()
            total = total + recv_ref[k - 1]
        o_ref[pl.ds(mine, chunk), :] = total
        gather = []
        for k in range(1, N_DEV):
            cp = pltpu.make_async_remote_copy(
                src_ref=o_ref.at[pl.ds(mine, chunk)],
                dst_ref=o_ref.at[pl.ds(mine, chunk)],
                send_sem=send_sems.at[6 + k], recv_sem=recv_sems.at[6 + k],
                device_id=_flip(me, k), device_id_type=MESH)
            cp.start()
            gather.append(cp)
        for k in range(1, N_DEV):
            theirs = pl.multiple_of(_index(_flip(me, k)) * chunk, SUBLANES)
            recv = pltpu.make_async_remote_copy(
                src_ref=o_ref.at[pl.ds(theirs, chunk)], dst_ref=o_ref.at[pl.ds(theirs, chunk)],
                send_sem=send_sems.at[6 + k], recv_sem=recv_sems.at[6 + k],
                device_id=_flip(me, k), device_id_type=MESH)
            recv.wait_recv()
        for cp in gather:
            cp.wait_send()

    return pl.pallas_call(
        body, name="comm_allreduce_packed",
        in_specs=[VMEM_SPEC],
        out_specs=VMEM_SPEC,
        out_shape=jax.ShapeDtypeStruct(p.shape, F32),
        scratch_shapes=[pltpu.VMEM((N_DEV - 1, chunk, LANES), F32),
                        pltpu.SemaphoreType.DMA((2 * (N_DEV - 1),)),
                        pltpu.SemaphoreType.DMA((2 * (N_DEV - 1),))],
        compiler_params=_params(),
    )(p)


def _adamw_math(w, g, m, v):
    m = ADAM_B1 * m + (1.0 - ADAM_B1) * g
    v = ADAM_B2 * v + (1.0 - ADAM_B2) * (g * g)
    m_hat = m / (1.0 - ADAM_B1 ** ADAM_STEP)
    v_hat = v / (1.0 - ADAM_B2 ** ADAM_STEP)
    delta = -ADAM_LR * (m_hat / (jnp.sqrt(v_hat) + ADAM_EPS) + ADAM_WD * w)
    return delta, m, v


def _adamw_summed(parts, w, m, v, name):
    depth, r, c = w.shape
    tr = min(r, 128)

    def body(*refs):
        p_refs = refs[:depth]
        w_ref, m_ref, v_ref, g_ref, d_ref, nm_ref, nv_ref = refs[depth:]
        for l in range(depth):
            g = p_refs[l][0].astype(F32)
            for q in range(1, N_DEV):
                g = g + p_refs[l][q].astype(F32)
            g_ref[l] = g
            d_ref[l], nm_ref[l], nv_ref[l] = _adamw_math(w_ref[l], g, m_ref[l], v_ref[l])

    blk = pl.BlockSpec((depth, tr, c), lambda i: (0, i, 0))
    return pl.pallas_call(
        body, name=name,
        grid=(r // tr,),
        in_specs=[pl.BlockSpec((N_DEV, tr, c), lambda i: (0, i, 0))] * depth + [blk, blk, blk],
        out_specs=[blk] * 4,
        out_shape=[jax.ShapeDtypeStruct((depth, r, c), F32)] * 4,
        compiler_params=_params(dimension_semantics=("arbitrary",)),
    )(*parts, w, m, v)


def _adamw_packed(w, g, m, v):
    r, c = w.shape
    tr = r // N_DEV

    def body(w_ref, g_ref, m_ref, v_ref, d_ref, nm_ref, nv_ref):
        d_ref[...], nm_ref[...], nv_ref[...] = _adamw_math(w_ref[...], g_ref[...], m_ref[...], v_ref[...])

    blk = pl.BlockSpec((tr, c), lambda i: (i, 0))
    return pl.pallas_call(
        body, name="adamw_packed",
        grid=(r // tr,),
        in_specs=[blk] * 4,
        out_specs=[blk] * 3,
        out_shape=[jax.ShapeDtypeStruct((r, c), F32)] * 3,
        compiler_params=_params(dimension_semantics=("arbitrary",)),
    )(w, g, m, v)


_PACK_ROWS = SUBLANES * N_DEV


def _pack(arrays):
    flat = jnp.concatenate([a.reshape(-1) for a in arrays])
    per = _PACK_ROWS * LANES
    total = -(-flat.shape[0] // per) * per
    flat = jnp.pad(flat, (0, total - flat.shape[0]))
    return flat.reshape(total // LANES, LANES)


def _unpack(packed, like):
    flat = packed.reshape(-1)
    out = []
    off = 0
    for a in like:
        out.append(flat[off:off + a.size].reshape(a.shape))
        off += a.size
    return out


def kernel(x, norm_g, w_in, pool_w, pool_scale, a_re, a_im, log_dt, b_re, b_im, c_re, c_im, d_skip, glu_w, glu_b, w_out, final_g, loss_target, m_norm_g, m_w_in, m_pool_w, m_pool_scale, m_a_re, m_a_im, m_log_dt, m_b_re, m_b_im, m_c_re, m_c_im, m_d_skip, m_glu_w, m_glu_b, m_w_out, m_final_g, v_norm_g, v_w_in, v_pool_w, v_pool_scale, v_a_re, v_a_im, v_log_dt, v_b_re, v_b_im, v_c_re, v_c_im, v_d_skip, v_glu_w, v_glu_b, v_w_out, v_final_g):
    nb, seq, _ = x.shape
    n_tok = nb * seq
    depth = norm_g.shape[0]

    my_idx = _index(_mesh_place())

    shards_m = (_mx(w_in), _mx(glu_w), _mx(w_out))

    def gather_start(l, after):
        srcs = tuple(s[l] for s in shards_m)
        lands = tuple(_with_own_slot(s, my_idx) for s in srcs)
        return _exchange_start(srcs, lands, after, f"comm_gather_start_{l}", gather=True)

    def gather_wait(handle, after, l):
        win, glu, wout = _exchange_wait(handle, after, f"comm_gather_wait_{l}", gather=True)
        return win, glu.reshape(SSM_W, SSM_W), wout.reshape(MIX, D_MODEL)

    pack_all = jax.vmap(_ssm_pack)
    (lbr, lbi, wb, wct), pack_vjp = jax.vjp(pack_all, a_re, a_im, log_dt, b_re, b_im, c_re, c_im)
    wb_m, wct_m = _mx(wb), _mx(wct)
    pool_w_m = _mx(pool_w)

    def layer_params(l):
        return (pool_w_m[l], pool_scale[l][None], lbr[l], lbi[l], wb_m[l], wct_m[l], d_skip[l][None],
                weights[l][1], glu_b[l][None])

    xs = [x.reshape(n_tok, D_MODEL)]
    saved = []
    weights = []
    handle, dep = gather_start(0, xs[0])
    for l in range(depth):
        weights.append(gather_wait(handle, xs[-1], l))
        if l + 1 < depth:
            handle, dep = gather_start(l + 1, weights[l][0])
        z, h = _inproj_fwd(xs[-1], norm_g[l][None], weights[l][0], dep)
        z3 = z.reshape(nb, seq, 2 * MIX)
        yg, sbound = _mixer_fwd(z3, *layer_params(l))
        yg2 = yg.reshape(n_tok, MIX)
        xs.append(_outproj_fwd(xs[-1], yg2, weights[l][2]))
        saved.append((z3, h, yg2, sbound))

    dx, loss_part, d_final_g = _loss_head(xs[-1], loss_target.reshape(n_tok, D_MODEL), final_g[None])
    loss = lax.psum(loss_part[0, 0], ("x", "y", "c"))

    small = {k: [None] * depth for k in
             ("norm_g", "pool_w", "pool_scale", "lbr", "lbi", "wb", "wct", "d_skip", "glu_b")}
    received = [None] * depth
    pending = None
    for l in reversed(range(depth)):
        z3, h, yg2, sbound = saved[l]
        dy, d_wout = _outproj_bwd(dx, yg2, weights[l][2], dep)
        (dz, d_pw, d_ps, d_lbr, d_lbi, d_wb, d_wct, d_dsk, d_gw, d_gb) = _mixer_bwd(
            z3, dy.reshape(nb, seq, MIX), sbound, *layer_params(l))
        dx, d_win, d_ng = _inproj_bwd(dz.reshape(n_tok, 2 * MIX), h, xs[l], dx, norm_g[l][None], weights[l][0])
        for k, val in (("norm_g", d_ng[0]), ("pool_w", d_pw), ("pool_scale", d_ps[0]), ("lbr", d_lbr),
                       ("lbi", d_lbi), ("wb", d_wb), ("wct", d_wct), ("d_skip", d_dsk[0]), ("glu_b", d_gb[0])):
            small[k][l] = val
        if pending is not None:
            received[l + 1] = _exchange_wait(pending, dx, f"comm_grads_wait_{l + 1}", gather=False)
        srcs = (d_win, d_gw.reshape(N_DEV, SSM_W // N_DEV, SSM_W), d_wout.reshape(N_DEV, MIX // N_DEV, D_MODEL))
        lands = tuple(_with_own_slot(lax.dynamic_index_in_dim(s, my_idx, 0, keepdims=False), my_idx) for s in srcs)
        pending, dep = _exchange_start(srcs, lands, dx, f"comm_grads_start_{l}", gather=False)
    received[0] = _exchange_wait(pending, dx, "comm_grads_wait_0", gather=False)

    shard_res = {}
    for pos, (n, w, m, v) in enumerate((("w_in", w_in, m_w_in, v_w_in), ("glu_w", glu_w, m_glu_w, v_glu_w),
                                        ("w_out", w_out, m_w_out, v_w_out))):
        shard_res[n] = _adamw_summed([received[l][pos] for l in range(depth)], w, m, v, "adamw_" + n)

    stack = lambda k: jnp.stack(small[k])
    d_a_re, d_a_im, d_log_dt, d_b_re, d_b_im, d_c_re, d_c_im = pack_vjp(
        (stack("lbr"), stack("lbi"), stack("wb"), stack("wct")))
    local_grads = [stack("norm_g"), stack("pool_w"), stack("pool_scale"), d_a_re, d_a_im, d_log_dt,
                   d_b_re, d_b_im, d_c_re, d_c_im, stack("d_skip"), stack("glu_b"), d_final_g[0]]
    small_w = [norm_g, pool_w, pool_scale, a_re, a_im, log_dt, b_re, b_im, c_re, c_im, d_skip, glu_b, final_g]
    small_m = [m_norm_g, m_pool_w, m_pool_scale, m_a_re, m_a_im, m_log_dt, m_b_re, m_b_im, m_c_re, m_c_im,
               m_d_skip, m_glu_b, m_final_g]
    small_v = [v_norm_g, v_pool_w, v_pool_scale, v_a_re, v_a_im, v_log_dt, v_b_re, v_b_im, v_c_re, v_c_im,
               v_d_skip, v_glu_b, v_final_g]
    g_packed = _allreduce_packed(_pack(local_grads))
    d_packed, m_packed, v_packed = _adamw_packed(_pack(small_w), g_packed, _pack(small_m), _pack(small_v))
    names = ["norm_g", "pool_w", "pool_scale", "a_re", "a_im", "log_dt", "b_re", "b_im", "c_re", "c_im",
             "d_skip", "glu_b", "final_g"]
    res = {}
    for kind, packed in (("grad", g_packed), ("delta", d_packed), ("m", m_packed), ("v", v_packed)):
        for n, a in zip(names, _unpack(packed, small_w)):
            res[kind, n] = a
    for n in ("w_in", "glu_w", "w_out"):
        for pos, kind in enumerate(("grad", "delta", "m", "v")):
            res[kind, n] = shard_res[n][pos]

    order = ["norm_g", "w_in", "pool_w", "pool_scale", "a_re", "a_im", "log_dt", "b_re", "b_im", "c_re", "c_im",
             "d_skip", "glu_w", "glu_b", "w_out", "final_g"]
    outs = [loss, dx.reshape(nb, seq, D_MODEL)]
    for kind in ("grad", "delta", "m", "v"):
        outs += [res[kind, n] for n in order]
    return tuple(outs)
```

```python
import functools
import math

import jax
import jax.numpy as jnp
from jax import lax
from jax.experimental import pallas as pl
from jax.experimental.pallas import tpu as pltpu

F32 = jnp.float32
MXU_DTYPE = jnp.bfloat16

D_MODEL = 1024
MIX = 1024
POOL_W = 512
SSM_W = 512
N_POOL_G = 4
POOL_GC = 128
SSM_G = 32
SSM_C = 16
SSM_P = 64
DEPTH = 4
NORM_EPS = 1e-5
N_DEV = 8

ADAM_LR = 0.001
ADAM_B1 = 0.9
ADAM_B2 = 0.999
ADAM_EPS = 1e-08
ADAM_WD = 0.01
ADAM_STEP = 10

SUBLANES = 8
LANES = 128
HALO = 16
STATE_ROWS = 8
STATE_COLS = 256
T_BLK = 256
TM_FWD = 512
TM_BWD = 256
VMEM_LIMIT = 56 * 1024 * 1024

MESH = pl.DeviceIdType.MESH
VMEM_SPEC = pl.BlockSpec(memory_space=pltpu.VMEM)
ANY_SPEC = pl.BlockSpec(memory_space=pl.ANY)


def _mm(a, b):
    return jnp.dot(a, b, preferred_element_type=F32)


def _mm_tn(a, b):
    return lax.dot_general(a, b, (((0,), (0,)), ((), ())), preferred_element_type=F32)


def _mm_nt(a, b):
    return lax.dot_general(a, b, (((1,), (1,)), ((), ())), preferred_element_type=F32)


def _mx(a):
    return a.astype(MXU_DTYPE)


def _sigmoid(v):
    return 1.0 / (1.0 + jnp.exp(-v))


_GELU_C = math.sqrt(2.0 / math.pi)
_GELU_A = 0.044715


def _gelu_and_grad(y):
    th = jnp.tanh(_GELU_C * (y + _GELU_A * y * y * y))
    val = 0.5 * y * (1.0 + th)
    grad = 0.5 * (1.0 + th) + 0.5 * y * (1.0 - th * th) * (_GELU_C * (1.0 + 3.0 * _GELU_A * y * y))
    return val, grad


def _params(**kw):
    return pltpu.CompilerParams(vmem_limit_bytes=VMEM_LIMIT, **kw)


def _ssm_pack(a_re, a_im, log_dt, b_re, b_im, c_re, c_im):
    dt = jnp.exp(log_dt)[:, None]
    mag = jnp.exp(a_re * dt)
    ang = a_im * dt
    lb_re = mag * jnp.cos(ang)
    lb_im = mag * jnp.sin(ang)
    den = a_re * a_re + a_im * a_im
    n_re = lb_re - 1.0
    n_im = lb_im
    f_re = (n_re * a_re + n_im * a_im) / den
    f_im = (n_im * a_re - n_re * a_im) / den
    bb_re = f_re[..., None] * b_re - f_im[..., None] * b_im
    bb_im = f_re[..., None] * b_im + f_im[..., None] * b_re

    row_group = jnp.arange(64) // SSM_C
    col_group = (jnp.arange(512) // SSM_P) % 4
    own_group = (row_group[:, None] == col_group[None, :]).astype(F32)
    even = (jnp.arange(8) % 2 == 0).astype(F32)[:, None, None]

    def chunked(per_channel):
        half = jnp.tile(per_channel, (1, 4, 1)) * own_group
        return jnp.concatenate([half * even, half * (1.0 - even)], axis=1)

    bb = jnp.stack([bb_re, bb_im], axis=0).reshape(2, 8, 4, SSM_P, SSM_C)
    wb = chunked(bb.transpose(1, 4, 0, 2, 3).reshape(8, SSM_C, 512))
    cc = jnp.stack([c_re, -c_im], axis=0).reshape(2, 8, 4, SSM_C, SSM_P)
    wct = chunked(cc.transpose(1, 3, 0, 2, 4).reshape(8, SSM_C, 512))
    return (lb_re.reshape(STATE_ROWS, STATE_COLS), lb_im.reshape(STATE_ROWS, STATE_COLS), wb, wct)


def _inproj_fwd(x2, g_row, w_all, dep):
    n = x2.shape[0]
    tm = TM_FWD

    def body(x_ref, g_ref, w_ref, dep_ref, z_ref, h_ref):
        x = x_ref[...]
        r = lax.rsqrt(jnp.mean(x * x, axis=-1, keepdims=True) + NORM_EPS)
        h = _mx(x * r * g_ref[...])
        h_ref[...] = h
        for d in range(N_DEV):
            z_ref[:, d * 256:(d + 1) * 256] = _mm(h, w_ref[d])

    return pl.pallas_call(
        body, name="inproj_fwd",
        grid=(n // tm,),
        in_specs=[pl.BlockSpec((tm, D_MODEL), lambda i: (i, 0)),
                  pl.BlockSpec((1, D_MODEL), lambda i: (0, 0)),
                  pl.BlockSpec((N_DEV, D_MODEL, 256), lambda i: (0, 0, 0)),
                  ANY_SPEC],
        out_specs=[pl.BlockSpec((tm, 2 * MIX), lambda i: (i, 0)),
                   pl.BlockSpec((tm, D_MODEL), lambda i: (i, 0))],
        out_shape=[jax.ShapeDtypeStruct((n, 2 * MIX), F32),
                   jax.ShapeDtypeStruct((n, D_MODEL), MXU_DTYPE)],
        compiler_params=_params(dimension_semantics=("arbitrary",)),
    )(x2, g_row, w_all, dep)


def _outproj_fwd(x2, yg, w_out):
    n = x2.shape[0]
    tm = TM_FWD

    def body(x_ref, y_ref, w_ref, o_ref):
        o_ref[...] = x_ref[...] + _mm(y_ref[...], w_ref[...])

    return pl.pallas_call(
        body, name="outproj_fwd",
        grid=(n // tm,),
        in_specs=[pl.BlockSpec((tm, D_MODEL), lambda i: (i, 0)),
                  pl.BlockSpec((tm, MIX), lambda i: (i, 0)),
                  pl.BlockSpec((MIX, D_MODEL), lambda i: (0, 0))],
        out_specs=pl.BlockSpec((tm, D_MODEL), lambda i: (i, 0)),
        out_shape=jax.ShapeDtypeStruct((n, D_MODEL), F32),
        compiler_params=_params(dimension_semantics=("arbitrary",)),
    )(x2, yg, w_out)


def _loss_head(x2, tgt2, g_row):
    n = x2.shape[0]
    tm = TM_FWD

    def body(x_ref, t_ref, g_ref, dx_ref, loss_ref, dg_ref):
        @pl.when(pl.program_id(0) == 0)
        def _():
            loss_ref[...] = jnp.zeros_like(loss_ref)
            dg_ref[...] = jnp.zeros_like(dg_ref)

        x = x_ref[...]
        g = g_ref[...]
        r = lax.rsqrt(jnp.mean(x * x, axis=-1, keepdims=True) + NORM_EPS)
        xh = x * r
        e = xh * g - t_ref[...]
        loss_ref[...] += jnp.sum(jnp.sum(e * e, axis=-1, keepdims=True), axis=0, keepdims=True) * (0.5 / D_MODEL)
        dout = e * (1.0 / D_MODEL)
        dg_ref[...] += jnp.sum(dout * xh, axis=0, keepdims=True)
        gdy = dout * g
        dx_ref[...] = r * (gdy - xh * jnp.mean(xh * gdy, axis=-1, keepdims=True))

    return pl.pallas_call(
        body, name="loss_head",
        grid=(n // tm,),
        in_specs=[pl.BlockSpec((tm, D_MODEL), lambda i: (i, 0)),
                  pl.BlockSpec((tm, D_MODEL), lambda i: (i, 0)),
                  pl.BlockSpec((1, D_MODEL), lambda i: (0, 0))],
        out_specs=[pl.BlockSpec((tm, D_MODEL), lambda i: (i, 0)),
                   pl.BlockSpec((1, 1), lambda i: (0, 0)),
                   pl.BlockSpec((1, D_MODEL), lambda i: (0, 0))],
        out_shape=[jax.ShapeDtypeStruct((n, D_MODEL), F32),
                   jax.ShapeDtypeStruct((1, 1), F32),
                   jax.ShapeDtypeStruct((1, D_MODEL), F32)],
        compiler_params=_params(dimension_semantics=("arbitrary",)),
    )(x2, tgt2, g_row)


def _outproj_bwd(dx2, yg, w_out, dep):
    n = dx2.shape[0]
    tm = TM_BWD
    n_steps = n // tm

    def body(dx_ref, y_ref, w_ref, dep_ref, dy_ref, dw_ref, acc_ref):
        i = pl.program_id(0)

        @pl.when(i == 0)
        def _():
            acc_ref[...] = jnp.zeros_like(acc_ref)

        dxb = _mx(dx_ref[...])
        dy_ref[...] = _mm_nt(dxb, w_ref[...])
        acc_ref[...] += _mm_tn(y_ref[...], dxb)

        @pl.when(i == n_steps - 1)
        def _():
            dw_ref[...] = _mx(acc_ref[...])

    return pl.pallas_call(
        body, name="outproj_bwd",
        grid=(n_steps,),
        in_specs=[pl.BlockSpec((tm, D_MODEL), lambda i: (i, 0)),
                  pl.BlockSpec((tm, MIX), lambda i: (i, 0)),
                  pl.BlockSpec((MIX, D_MODEL), lambda i: (0, 0)),
                  ANY_SPEC],
        out_specs=[pl.BlockSpec((tm, MIX), lambda i: (i, 0)),
                   pl.BlockSpec((MIX, D_MODEL), lambda i: (0, 0))],
        out_shape=[jax.ShapeDtypeStruct((n, MIX), F32),
                   jax.ShapeDtypeStruct((MIX, D_MODEL), MXU_DTYPE)],
        scratch_shapes=[pltpu.VMEM((MIX, D_MODEL), F32)],
        compiler_params=_params(dimension_semantics=("arbitrary",)),
    )(dx2, yg, w_out, dep)


def _inproj_bwd(dz, h, x2, dx_in, g_row, w_all):
    n = x2.shape[0]
    tm = TM_BWD
    n_steps = n // tm

    def body(dz_ref, h_ref, x_ref, dxi_ref, g_ref, w_ref, dxo_ref, dw_ref, dg_ref, acc_ref):
        i = pl.program_id(0)

        @pl.when(i == 0)
        def _():
            acc_ref[...] = jnp.zeros_like(acc_ref)
            dg_ref[...] = jnp.zeros_like(dg_ref)

        hb = h_ref[...]
        dh = jnp.zeros((tm, D_MODEL), F32)
        for d in range(N_DEV):
            dzd = dz_ref[:, d * 256:(d + 1) * 256]
            acc_ref[d] += _mm_tn(hb, dzd)
            dh = dh + _mm_nt(dzd, w_ref[d])
        x = x_ref[...]
        r = lax.rsqrt(jnp.mean(x * x, axis=-1, keepdims=True) + NORM_EPS)
        xh = x * r
        dg_ref[...] += jnp.sum(dh * xh, axis=0, keepdims=True)
        gdy = dh * g_ref[...]
        dxo_ref[...] = dxi_ref[...] + r * (gdy - xh * jnp.mean(xh * gdy, axis=-1, keepdims=True))

        @pl.when(i == n_steps - 1)
        def _():
            dw_ref[...] = _mx(acc_ref[...])

    return pl.pallas_call(
        body, name="inproj_bwd",
        grid=(n_steps,),
        in_specs=[pl.BlockSpec((tm, 2 * MIX), lambda i: (i, 0)),
                  pl.BlockSpec((tm, D_MODEL), lambda i: (i, 0)),
                  pl.BlockSpec((tm, D_MODEL), lambda i: (i, 0)),
                  pl.BlockSpec((tm, D_MODEL), lambda i: (i, 0)),
                  pl.BlockSpec((1, D_MODEL), lambda i: (0, 0)),
                  pl.BlockSpec((N_DEV, D_MODEL, 256), lambda i: (0, 0, 0))],
        out_specs=[pl.BlockSpec((tm, D_MODEL), lambda i: (i, 0)),
                   pl.BlockSpec((N_DEV, D_MODEL, 256), lambda i: (0, 0, 0)),
                   pl.BlockSpec((1, D_MODEL), lambda i: (0, 0))],
        out_shape=[jax.ShapeDtypeStruct((n, D_MODEL), F32),
                   jax.ShapeDtypeStruct((N_DEV, D_MODEL, 256), MXU_DTYPE),
                   jax.ShapeDtypeStruct((1, D_MODEL), F32)],
        scratch_shapes=[pltpu.VMEM((N_DEV, D_MODEL, 256), F32)],
        compiler_params=_params(dimension_semantics=("arbitrary",)),
    )(dz, h, x2, dx_in, g_row, w_all)


def _row_pos(t0, rows):
    return t0 + lax.broadcasted_iota(jnp.int32, (rows, LANES), 0)


def _pool_window_mean(upad, g, t0, t_blk):
    k = 2 << g
    w = upad
    sh = 1
    while sh < k:
        w = w + pltpu.roll(w, sh, 0)
        sh *= 2
    count = jnp.minimum(_row_pos(t0, t_blk) + 1, k).astype(F32)
    return w[HALO:] / count - upad[HALO:]


def _pool_window_bwd(qpad, g, t_blk):
    k = 2 << g
    n = t_blk + HALO
    w = qpad
    sh = 1
    while sh < k:
        w = w + pltpu.roll(w, n - sh, 0)
        sh *= 2
    return w[:t_blk]


class _StateBuf:
    def __init__(self, refs, t_blk):
        self.refs = refs
        self.t_blk = t_blk

    def put_chunk(self, b, j, val):
        for c in range(4):
            self.refs[4 * b + c][pl.ds(j, self.t_blk, stride=STATE_ROWS), :] = val[:, c * LANES:(c + 1) * LANES]

    def get_chunk(self, b, j):
        return jnp.concatenate(
            [self.refs[4 * b + c][pl.ds(j, self.t_blk, stride=STATE_ROWS), :] for c in range(4)], axis=-1)

    def load(self, b, r, part):
        return jnp.concatenate(
            [self.refs[4 * b + 2 * part + h][pl.ds(r, STATE_ROWS), :] for h in range(2)], axis=-1)

    def store(self, b, r, part, val):
        for h in range(2):
            self.refs[4 * b + 2 * part + h][pl.ds(r, STATE_ROWS), :] = val[:, h * LANES:(h + 1) * LANES]


def _state_scratch(nb, t_blk):
    return [pltpu.VMEM((t_blk * STATE_ROWS, LANES), F32) for _ in range(4 * nb)]


def _ssm_project_in(u_ssm, wb_ref, buf, b):
    ub = _mx(u_ssm)
    for j in range(STATE_ROWS):
        m = j // 2
        buf.put_chunk(b, j, _mm(ub[:, m * LANES:(m + 1) * LANES], wb_ref[j]))


def _scan_forward(buf, lbr, lbi, init, nb):
    def body(t, carry):
        r = pl.multiple_of(t * STATE_ROWS, STATE_ROWS)
        out = []
        for b in range(nb):
            sr, si = carry[2 * b], carry[2 * b + 1]
            nr = lbr * sr - lbi * si + buf.load(b, r, 0)
            ni = lbr * si + lbi * sr + buf.load(b, r, 1)
            buf.store(b, r, 0, nr)
            buf.store(b, r, 1, ni)
            out += [nr, ni]
        return tuple(out)

    return lax.fori_loop(0, buf.t_blk, body, init, unroll=4)


def _ssm_project_out(chunk, wc_ref):
    tiles = []
    for m in range(4):
        acc = None
        for j in (2 * m, 2 * m + 1):
            part = _mm_nt(chunk(j), wc_ref[j])
            acc = part if acc is None else acc + part
        tiles.append(acc)
    return jnp.concatenate(tiles, axis=-1)


def _mixer_fwd(z3, pool_w, pool_scale, lbr, lbi, wb, wc, d_skip, glu_w, glu_b):
    nb, seq, _ = z3.shape
    t_blk = min(T_BLK, seq)
    n_t = seq // t_blk
    halo_per_blk = t_blk // HALO

    def body(z_ref, zh_ref, pw_ref, ps_ref, lbr_ref, lbi_ref, wb_ref, wc_ref, dsk_ref, gw_ref, gb_ref,
             yg_ref, sc_ref, carry_ref, *s_refs):
        i = pl.program_id(0)
        t0 = i * t_blk
        buf = _StateBuf(s_refs, t_blk)

        @pl.when(i == 0)
        def _():
            carry_ref[...] = jnp.zeros_like(carry_ref)

        for b in range(nb):
            _ssm_project_in(z_ref[b, :, POOL_W:MIX], wb_ref, buf, b)
        init = tuple(carry_ref[b, :, h * STATE_COLS:(h + 1) * STATE_COLS] for b in range(nb) for h in range(2))
        fin = _scan_forward(buf, lbr_ref[...], lbi_ref[...], init, nb)
        for b in range(nb):
            carry_ref[b, :, 0:STATE_COLS] = fin[2 * b]
            carry_ref[b, :, STATE_COLS:2 * STATE_COLS] = fin[2 * b + 1]

        first = (i == 0)
        for b in range(nb):
            u_ssm = z_ref[b, :, POOL_W:MIX]

            def chunk(j, b=b):
                states = _mx(buf.get_chunk(b, j))
                sc_ref[b, j] = states
                return states

            y = _ssm_project_out(chunk, wc_ref) + dsk_ref[...] * u_ssm
            yg, _ = _gelu_and_grad(y)
            v = _mm(_mx(yg), gw_ref[...]) + gb_ref[...]
            o_ssm = yg * _sigmoid(v)
            gp = z_ref[b, :, MIX + POOL_W:2 * MIX]
            yg_ref[b, :, POOL_W:MIX] = _mx(o_ssm * (gp * _sigmoid(gp)))
            for g in range(N_POOL_G):
                cols = slice(g * POOL_GC, (g + 1) * POOL_GC)
                halo = jnp.where(first, 0.0, zh_ref[b, :, cols])
                upad = jnp.concatenate([halo, z_ref[b, :, cols]], axis=0)
                pooled = _pool_window_mean(upad, g, t0, t_blk)
                yp = _mm(_mx(pooled), pw_ref[g]) * ps_ref[:, cols]
                gpp = z_ref[b, :, MIX + g * POOL_GC:MIX + (g + 1) * POOL_GC]
                yg_ref[b, :, cols] = _mx(yp * (gpp * _sigmoid(gpp)))

    const = lambda *shape: pl.BlockSpec(shape, lambda i: (0,) * len(shape))
    return pl.pallas_call(
        body, name="mixer_fwd",
        grid=(n_t,),
        in_specs=[pl.BlockSpec((nb, t_blk, 2 * MIX), lambda i: (0, i, 0)),
                  pl.BlockSpec((nb, HALO, POOL_W), lambda i: (0, jnp.maximum(i * halo_per_blk - 1, 0), 0)),
                  const(N_POOL_G, POOL_GC, POOL_GC), const(1, POOL_W),
                  const(STATE_ROWS, STATE_COLS), const(STATE_ROWS, STATE_COLS),
                  const(STATE_ROWS, LANES, 2 * STATE_COLS), const(STATE_ROWS, LANES, 2 * STATE_COLS),
                  const(1, SSM_W), const(SSM_W, SSM_W), const(1, SSM_W)],
        out_specs=[pl.BlockSpec((nb, t_blk, MIX), lambda i: (0, i, 0)),
                   pl.BlockSpec((nb, STATE_ROWS, t_blk, 2 * STATE_COLS), lambda i: (0, 0, i, 0))],
        out_shape=[jax.ShapeDtypeStruct((nb, seq, MIX), MXU_DTYPE),
                   jax.ShapeDtypeStruct((nb, STATE_ROWS, seq, 2 * STATE_COLS), MXU_DTYPE)],
        scratch_shapes=[pltpu.VMEM((nb, STATE_ROWS, 2 * STATE_COLS), F32)] + _state_scratch(nb, t_blk),
        compiler_params=_params(dimension_semantics=("arbitrary",)),
    )(z3, z3, pool_w, pool_scale, lbr, lbi, wb, wc, d_skip, glu_w, glu_b)


def _mixer_bwd(z3, dy3, states, pool_w, pool_scale, lbr, lbi, wb, wc, d_skip, glu_w, glu_b):
    nb, seq, _ = z3.shape
    t_blk = min(T_BLK, seq)
    n_t = seq // t_blk
    halo_per_blk = t_blk // HALO

    def body(z_ref, zh_ref, dy_ref, sc_ref, sch_ref, pw_ref, ps_ref, lbr_ref, lbi_ref, wb_ref, wc_ref, dsk_ref,
             gw_ref, gb_ref,
             dz_ref, dpw_ref, dps_ref, dlbr_ref, dlbi_ref, dwb_ref, dwc_ref, ddsk_ref, dgw_ref, dgb_ref,
             gcarry_ref, qcarry_ref, du_ref, dgw_acc, *g_refs):
        i = pl.program_id(0)
        blk = n_t - 1 - i
        t0 = blk * t_blk
        gbuf = _StateBuf(g_refs, t_blk)

        @pl.when(i == 0)
        def _():
            gcarry_ref[...] = jnp.zeros_like(gcarry_ref)
            qcarry_ref[...] = jnp.zeros_like(qcarry_ref)
            for ref in (dpw_ref, dps_ref, dlbr_ref, dlbi_ref, dwb_ref, dwc_ref, ddsk_ref, dgw_acc, dgb_ref):
                ref[...] = jnp.zeros_like(ref)

        lbr_v = lbr_ref[...]
        lbi_v = lbi_ref[...]

        first = (blk == 0)
        for b in range(nb):
            u_ssm = z_ref[b, :, POOL_W:MIX]
            y = _ssm_project_out(lambda j, b=b: sc_ref[b, j], wc_ref) + dsk_ref[...] * u_ssm
            yg, dgelu = _gelu_and_grad(y)
            ygb = _mx(yg)
            sg = _sigmoid(_mm(ygb, gw_ref[...]) + gb_ref[...])
            o_ssm = yg * sg
            gp = z_ref[b, :, MIX + POOL_W:2 * MIX]
            sgm = _sigmoid(gp)
            dyv = dy_ref[b, :, POOL_W:MIX]
            dz_ref[b, :, MIX + POOL_W:2 * MIX] = _mx(dyv * o_ssm * (sgm * (1.0 + gp * (1.0 - sgm))))
            do = dyv * (gp * sgm)
            dv = do * yg * (sg * (1.0 - sg))
            dvb = _mx(dv)
            dgb_ref[...] += jnp.sum(dv, axis=0, keepdims=True)
            dgw_acc[...] += _mm_tn(ygb, dvb)
            dyp = (do * sg + _mm_nt(dvb, gw_ref[...])) * dgelu
            ddsk_ref[...] += jnp.sum(dyp * u_ssm, axis=0, keepdims=True)
            dypb = _mx(dyp)
            for j in range(STATE_ROWS):
                m = j // 2
                dyt = dypb[:, m * LANES:(m + 1) * LANES]
                gbuf.put_chunk(b, j, _mm(dyt, wc_ref[j]))
                dwc_ref[j] += _mm_tn(dyt, sc_ref[b, j])
            du_ref[b] = dsk_ref[...] * dyp

            for g in range(N_POOL_G):
                cols = slice(g * POOL_GC, (g + 1) * POOL_GC)
                halo = jnp.where(first, 0.0, zh_ref[b, :, cols])
                u_g = z_ref[b, :, cols]
                pooled = _pool_window_mean(jnp.concatenate([halo, u_g], axis=0), g, t0, t_blk)
                pb = _mx(pooled)
                ypre = _mm(pb, pw_ref[g])
                gpp = z_ref[b, :, MIX + g * POOL_GC:MIX + (g + 1) * POOL_GC]
                sgp = _sigmoid(gpp)
                dyg = dy_ref[b, :, cols]
                scale = ps_ref[:, cols]
                dz_ref[b, :, MIX + g * POOL_GC:MIX + (g + 1) * POOL_GC] = _mx(
                    dyg * (ypre * scale) * (sgp * (1.0 + gpp * (1.0 - sgp))))
                dyc = dyg * (gpp * sgp)
                dps_ref[:, cols] += jnp.sum(dyc * ypre, axis=0, keepdims=True)
                dypre = _mx(dyc * scale)
                dpw_ref[g] += _mm_tn(pb, dypre)
                dpooled = _mm_nt(dypre, pw_ref[g])
                count = jnp.minimum(_row_pos(t0, t_blk) + 1, 2 << g).astype(F32)
                q = dpooled / count
                qpad = jnp.concatenate([q, qcarry_ref[b, :, cols]], axis=0)
                qcarry_ref[b, :, cols] = q[:HALO]
                dz_ref[b, :, cols] = _mx(_pool_window_bwd(qpad, g, t_blk) - dpooled)

        def rev_body(k, carry):
            r = pl.multiple_of((t_blk - 1 - k) * STATE_ROWS, STATE_ROWS)
            out = []
            for b in range(nb):
                gr, gi = carry[2 * b], carry[2 * b + 1]
                ngr = lbr_v * gr + lbi_v * gi + gbuf.load(b, r, 0)
                ngi = lbr_v * gi - lbi_v * gr + gbuf.load(b, r, 1)
                gbuf.store(b, r, 0, ngr)
                gbuf.store(b, r, 1, ngi)
                out += [ngr, ngi]
            return tuple(out)

        init_g = tuple(gcarry_ref[b, :, h * STATE_COLS:(h + 1) * STATE_COLS] for b in range(nb) for h in range(2))
        fin = lax.fori_loop(0, t_blk, rev_body, init_g, unroll=4)
        for b in range(nb):
            gcarry_ref[b, :, 0:STATE_COLS] = fin[2 * b]
            gcarry_ref[b, :, STATE_COLS:2 * STATE_COLS] = fin[2 * b + 1]

        for b in range(nb):
            ub = _mx(z_ref[b, :, POOL_W:MIX])
            for m in range(4):
                acc = du_ref[b, :, m * LANES:(m + 1) * LANES]
                for j in (2 * m, 2 * m + 1):
                    g = gbuf.get_chunk(b, j)
                    gj = _mx(g)
                    acc = acc + _mm_nt(gj, wb_ref[j])
                    dwb_ref[j] += _mm_tn(ub[:, m * LANES:(m + 1) * LANES], gj)
                    before = jnp.where(first, 0.0, sch_ref[b, j].astype(F32))
                    spad = jnp.concatenate([before, sc_ref[b, j].astype(F32)], axis=0)
                    s_prev = pltpu.roll(spad, 1, 0)[HALO:]
                    g_re, g_im = g[:, :STATE_COLS], g[:, STATE_COLS:]
                    p_re, p_im = s_prev[:, :STATE_COLS], s_prev[:, STATE_COLS:]
                    dlbr_ref[j:j + 1, :] += jnp.sum(g_re * p_re + g_im * p_im, axis=0, keepdims=True)
                    dlbi_ref[j:j + 1, :] += jnp.sum(g_im * p_re - g_re * p_im, axis=0, keepdims=True)
                dz_ref[b, :, POOL_W + m * LANES:POOL_W + (m + 1) * LANES] = _mx(acc)

        @pl.when(i == n_t - 1)
        def _():
            dgw_ref[...] = _mx(dgw_acc[...])

    const = lambda *shape: pl.BlockSpec(shape, lambda i: (0,) * len(shape))
    rev = lambda i: n_t - 1 - i
    out_shape = [jax.ShapeDtypeStruct((nb, seq, 2 * MIX), MXU_DTYPE),
                 jax.ShapeDtypeStruct((N_POOL_G, POOL_GC, POOL_GC), F32),
                 jax.ShapeDtypeStruct((1, POOL_W), F32),
                 jax.ShapeDtypeStruct((STATE_ROWS, STATE_COLS), F32),
                 jax.ShapeDtypeStruct((STATE_ROWS, STATE_COLS), F32),
                 jax.ShapeDtypeStruct((STATE_ROWS, LANES, 2 * STATE_COLS), F32),
                 jax.ShapeDtypeStruct((STATE_ROWS, LANES, 2 * STATE_COLS), F32),
                 jax.ShapeDtypeStruct((1, SSM_W), F32),
                 jax.ShapeDtypeStruct((SSM_W, SSM_W), MXU_DTYPE),
                 jax.ShapeDtypeStruct((1, SSM_W), F32)]
    return pl.pallas_call(
        body, name="mixer_bwd",
        grid=(n_t,),
        in_specs=[pl.BlockSpec((nb, t_blk, 2 * MIX), lambda i: (0, rev(i), 0)),
                  pl.BlockSpec((nb, HALO, POOL_W), lambda i: (0, jnp.maximum(rev(i) * halo_per_blk - 1, 0), 0)),
                  pl.BlockSpec((nb, t_blk, MIX), lambda i: (0, rev(i), 0)),
                  pl.BlockSpec((nb, STATE_ROWS, t_blk, 2 * STATE_COLS), lambda i: (0, 0, rev(i), 0)),
                  pl.BlockSpec((nb, STATE_ROWS, HALO, 2 * STATE_COLS),
                               lambda i: (0, 0, jnp.maximum(rev(i) * halo_per_blk - 1, 0), 0)),
                  const(N_POOL_G, POOL_GC, POOL_GC), const(1, POOL_W),
                  const(STATE_ROWS, STATE_COLS), const(STATE_ROWS, STATE_COLS),
                  const(STATE_ROWS, LANES, 2 * STATE_COLS), const(STATE_ROWS, LANES, 2 * STATE_COLS),
                  const(1, SSM_W), const(SSM_W, SSM_W), const(1, SSM_W)],
        out_specs=[pl.BlockSpec((nb, t_blk, 2 * MIX), lambda i: (0, rev(i), 0))]
                  + [const(*s.shape) for s in out_shape[1:]],
        out_shape=out_shape,
        scratch_shapes=[pltpu.VMEM((nb, STATE_ROWS, 2 * STATE_COLS), F32),
                        pltpu.VMEM((nb, HALO, POOL_W), F32),
                        pltpu.VMEM((nb, t_blk, SSM_W), F32),
                        pltpu.VMEM((SSM_W, SSM_W), F32)]
                       + _state_scratch(nb, t_blk),
        compiler_params=_params(dimension_semantics=("arbitrary",)),
    )(z3, z3, dy3, states, states, pool_w, pool_scale, lbr, lbi, wb, wc, d_skip, glu_w, glu_b)


def _mesh_place():
    x, y, c = lax.axis_index("x"), lax.axis_index("y"), lax.axis_index("c")
    return x, y, c


def _flip(place, k):
    x, y, c = place
    return (1 - x if k & 4 else x, 1 - y if k & 2 else y, 1 - c if k & 1 else c)


def _index(place):
    x, y, c = place
    return 4 * x + 2 * y + c


HBM_SPEC = pl.BlockSpec(memory_space=pltpu.HBM)
SEM_SPEC = pl.BlockSpec(memory_space=pltpu.SEMAPHORE)
_EFFECT = pltpu.SideEffectType.DATAFLOW_SIDE_EFFECTING
N_PEERS = N_DEV - 1


def _exchange_copies(src_refs, land_refs, send_sems, recv_sems, gather):
    me = _mesh_place()
    mine = _index(me)
    out = []
    for a, (src_ref, land_ref) in enumerate(zip(src_refs, land_refs)):
        for k in range(1, N_DEV):
            peer = _flip(me, k)
            theirs = _index(peer)
            n = a * N_PEERS + k - 1
            src = src_ref if gather else src_ref.at[theirs]
            send = pltpu.make_async_remote_copy(
                src_ref=src, dst_ref=land_ref.at[mine], send_sem=send_sems.at[n], recv_sem=recv_sems.at[n],
                device_id=peer, device_id_type=MESH)
            recv = pltpu.make_async_remote_copy(
                src_ref=src, dst_ref=land_ref.at[theirs], send_sem=send_sems.at[n], recv_sem=recv_sems.at[n],
                device_id=peer, device_id_type=MESH)
            out.append((send, recv))
    return out


def _exchange_start(srcs, lands, after, name, gather):
    n = len(srcs)

    def body(*refs):
        src_refs, land_refs = refs[:n], refs[n:2 * n]
        send_sems, recv_sems = refs[2 * n + 1], refs[2 * n + 2]
        token = refs[-1]
        for send, _ in _exchange_copies(src_refs, land_refs, send_sems, recv_sems, gather):
            send.start()
        token[...] = jnp.zeros_like(token)

    thru = [pltpu.HBM(a.shape, a.dtype) for a in tuple(srcs) + tuple(lands)]
    res = pl.pallas_call(
        body, name=name,
        in_specs=[HBM_SPEC] * (2 * n) + [ANY_SPEC],
        out_specs=[SEM_SPEC, SEM_SPEC] + [HBM_SPEC] * (2 * n) + [VMEM_SPEC],
        out_shape=[pltpu.SemaphoreType.DMA((n * N_PEERS,)), pltpu.SemaphoreType.DMA((n * N_PEERS,))]
                  + thru + [jax.ShapeDtypeStruct((SUBLANES, LANES), F32)],
        input_output_aliases={i: 2 + i for i in range(2 * n)},
        compiler_params=pltpu.CompilerParams(has_side_effects=_EFFECT),
    )(*[pltpu.with_memory_space_constraint(a, pltpu.HBM) for a in tuple(srcs) + tuple(lands)], after)
    return tuple(res[:-1]), res[-1]


def _exchange_wait(handle, after, name, gather):
    send_sems, recv_sems = handle[0], handle[1]
    arrays = handle[2:]
    n = len(arrays) // 2

    def body(*refs):
        src_refs, land_refs = refs[:n], refs[n:2 * n]
        for send, recv in _exchange_copies(src_refs, land_refs, refs[2 * n], refs[2 * n + 1], gather):
            send.wait_send()
            recv.wait_recv()

    res = pl.pallas_call(
        body, name=name,
        in_specs=[HBM_SPEC] * (2 * n) + [SEM_SPEC, SEM_SPEC, ANY_SPEC],
        out_specs=[HBM_SPEC] * (2 * n),
        out_shape=[pltpu.HBM(a.shape, a.dtype) for a in arrays],
        input_output_aliases={i: i for i in range(2 * n)},
        compiler_params=pltpu.CompilerParams(has_side_effects=_EFFECT),
    )(*arrays, send_sems, recv_sems, after)
    return tuple(res[n:])


def _with_own_slot(block, idx):
    zone = jnp.zeros((N_DEV,) + block.shape, block.dtype)
    return lax.dynamic_update_slice(zone, block[None], (idx,) + (0,) * block.ndim)


def _allreduce_packed(p):
    rows = p.shape[0]
    chunk = rows // N_DEV

    def body(p_ref, o_ref, recv_ref, send_sems, recv_sems):
        me = _mesh_place()
        mine = pl.multiple_of(_index(me) * chunk, SUBLANES)
        scatter = []
        for k in range(1, N_DEV):
            peer = _flip(me, k)
            cp = pltpu.make_async_remote_copy(
                src_ref=p_ref.at[pl.ds(pl.multiple_of(_index(peer) * chunk, SUBLANES), chunk)],
                dst_ref=recv_ref.at[k - 1],
                send_sem=send_sems.at[k - 1], recv_sem=recv_sems.at[k - 1],
                device_id=peer, device_id_type=MESH)
            cp.start()
            scatter.append(cp)
        total = p_ref[pl.ds(mine, chunk), :]
        for k in range(1, N_DEV):
            scatter[k - 1].wait()
            total = total + recv_ref[k - 1]
        o_ref[pl.ds(mine, chunk), :] = total
        gather = []
        for k in range(1, N_DEV):
            cp = pltpu.make_async_remote_copy(
                src_ref=o_ref.at[pl.ds(mine, chunk)],
                dst_ref=o_ref.at[pl.ds(mine, chunk)],
                send_sem=send_sems.at[6 + k], recv_sem=recv_sems.at[6 + k],
                device_id=_flip(me, k), device_id_type=MESH)
            cp.start()
            gather.append(cp)
        for k in range(1, N_DEV):
            theirs = pl.multiple_of(_index(_flip(me, k)) * chunk, SUBLANES)
            recv = pltpu.make_async_remote_copy(
                src_ref=o_ref.at[pl.ds(theirs, chunk)], dst_ref=o_ref.at[pl.ds(theirs, chunk)],
                send_sem=send_sems.at[6 + k], recv_sem=recv_sems.at[6 + k],
                device_id=_flip(me, k), device_id_type=MESH)
            recv.wait_recv()
        for cp in gather:
            cp.wait_send()

    return pl.pallas_call(
        body, name="comm_allreduce_packed",
        in_specs=[VMEM_SPEC],
        out_specs=VMEM_SPEC,
        out_shape=jax.ShapeDtypeStruct(p.shape, F32),
        scratch_shapes=[pltpu.VMEM((N_DEV - 1, chunk, LANES), F32),
                        pltpu.SemaphoreType.DMA((2 * (N_DEV - 1),)),
                        pltpu.SemaphoreType.DMA((2 * (N_DEV - 1),))],
        compiler_params=_params(),
    )(p)


def _adamw_math(w, g, m, v):
    m = ADAM_B1 * m + (1.0 - ADAM_B1) * g
    v = ADAM_B2 * v + (1.0 - ADAM_B2) * (g * g)
    m_hat = m / (1.0 - ADAM_B1 ** ADAM_STEP)
    v_hat = v / (1.0 - ADAM_B2 ** ADAM_STEP)
    delta = -ADAM_LR * (m_hat / (jnp.sqrt(v_hat) + ADAM_EPS) + ADAM_WD * w)
    return delta, m, v


def _adamw_summed(parts, w, m, v, name):
    depth, r, c = w.shape
    tr = min(r, 128)

    def body(*refs):
        p_refs = refs[:depth]
        w_ref, m_ref, v_ref, g_ref, d_ref, nm_ref, nv_ref = refs[depth:]
        for l in range(depth):
            g = p_refs[l][0].astype(F32)
            for q in range(1, N_DEV):
                g = g + p_refs[l][q].astype(F32)
            g_ref[l] = g
            d_ref[l], nm_ref[l], nv_ref[l] = _adamw_math(w_ref[l], g, m_ref[l], v_ref[l])

    blk = pl.BlockSpec((depth, tr, c), lambda i: (0, i, 0))
    return pl.pallas_call(
        body, name=name,
        grid=(r // tr,),
        in_specs=[pl.BlockSpec((N_DEV, tr, c), lambda i: (0, i, 0))] * depth + [blk, blk, blk],
        out_specs=[blk] * 4,
        out_shape=[jax.ShapeDtypeStruct((depth, r, c), F32)] * 4,
        compiler_params=_params(dimension_semantics=("arbitrary",)),
    )(*parts, w, m, v)


def _adamw_packed(w, g, m, v):
    r, c = w.shape
    tr = r // N_DEV

    def body(w_ref, g_ref, m_ref, v_ref, d_ref, nm_ref, nv_ref):
        d_ref[...], nm_ref[...], nv_ref[...] = _adamw_math(w_ref[...], g_ref[...], m_ref[...], v_ref[...])

    blk = pl.BlockSpec((tr, c), lambda i: (i, 0))
    return pl.pallas_call(
        body, name="adamw_packed",
        grid=(r // tr,),
        in_specs=[blk] * 4,
        out_specs=[blk] * 3,
        out_shape=[jax.ShapeDtypeStruct((r, c), F32)] * 3,
        compiler_params=_params(dimension_semantics=("arbitrary",)),
    )(w, g, m, v)


_PACK_ROWS = SUBLANES * N_DEV


def _pack(arrays):
    flat = jnp.concatenate([a.reshape(-1) for a in arrays])
    per = _PACK_ROWS * LANES
    total = -(-flat.shape[0] // per) * per
    flat = jnp.pad(flat, (0, total - flat.shape[0]))
    return flat.reshape(total // LANES, LANES)


def _unpack(packed, like):
    flat = packed.reshape(-1)
    out = []
    off = 0
    for a in like:
        out.append(flat[off:off + a.size].reshape(a.shape))
        off += a.size
    return out


def kernel(x, norm_g, w_in, pool_w, pool_scale, a_re, a_im, log_dt, b_re, b_im, c_re, c_im, d_skip, glu_w, glu_b, w_out, final_g, loss_target, m_norm_g, m_w_in, m_pool_w, m_pool_scale, m_a_re, m_a_im, m_log_dt, m_b_re, m_b_im, m_c_re, m_c_im, m_d_skip, m_glu_w, m_glu_b, m_w_out, m_final_g, v_norm_g, v_w_in, v_pool_w, v_pool_scale, v_a_re, v_a_im, v_log_dt, v_b_re, v_b_im, v_c_re, v_c_im, v_d_skip, v_glu_w, v_glu_b, v_w_out, v_final_g):
    nb, seq, _ = x.shape
    n_tok = nb * seq
    depth = norm_g.shape[0]

    my_idx = _index(_mesh_place())

    shards_m = (_mx(w_in), _mx(glu_w), _mx(w_out))

    def gather_start(l, after):
        srcs = tuple(s[l] for s in shards_m)
        lands = tuple(_with_own_slot(s, my_idx) for s in srcs)
        return _exchange_start(srcs, lands, after, f"comm_gather_start_{l}", gather=True)

    def gather_wait(handle, after, l):
        win, glu, wout = _exchange_wait(handle, after, f"comm_gather_wait_{l}", gather=True)
        return win, glu.reshape(SSM_W, SSM_W), wout.reshape(MIX, D_MODEL)

    pack_all = jax.vmap(_ssm_pack)
    (lbr, lbi, wb, wct), pack_vjp = jax.vjp(pack_all, a_re, a_im, log_dt, b_re, b_im, c_re, c_im)
    wb_m, wct_m = _mx(wb), _mx(wct)
    pool_w_m = _mx(pool_w)

    def layer_params(l):
        return (pool_w_m[l], pool_scale[l][None], lbr[l], lbi[l], wb_m[l], wct_m[l], d_skip[l][None],
                weights[l][1], glu_b[l][None])

    xs = [x.reshape(n_tok, D_MODEL)]
    saved = []
    weights = []
    handle, dep = gather_start(0, xs[0])
    for l in range(depth):
        weights.append(gather_wait(handle, xs[-1], l))
        if l + 1 < depth:
            handle, dep = gather_start(l + 1, weights[l][0])
        z, h = _inproj_fwd(xs[-1], norm_g[l][None], weights[l][0], dep)
        z3 = z.reshape(nb, seq, 2 * MIX)
        yg, states = _mixer_fwd(z3, *layer_params(l))
        yg2 = yg.reshape(n_tok, MIX)
        xs.append(_outproj_fwd(xs[-1], yg2, weights[l][2]))
        saved.append((z3, h, yg2, states))

    dx, loss_part, d_final_g = _loss_head(xs[-1], loss_target.reshape(n_tok, D_MODEL), final_g[None])
    loss = lax.psum(loss_part[0, 0], ("x", "y", "c"))

    small = {k: [None] * depth for k in
             ("norm_g", "pool_w", "pool_scale", "lbr", "lbi", "wb", "wct", "d_skip", "glu_b")}
    received = [None] * depth
    pending = None
    for l in reversed(range(depth)):
        z3, h, yg2, states = saved[l]
        dy, d_wout = _outproj_bwd(dx, yg2, weights[l][2], dep)
        (dz, d_pw, d_ps, d_lbr, d_lbi, d_wb, d_wct, d_dsk, d_gw, d_gb) = _mixer_bwd(
            z3, dy.reshape(nb, seq, MIX), states, *layer_params(l))
        dx, d_win, d_ng = _inproj_bwd(dz.reshape(n_tok, 2 * MIX), h, xs[l], dx, norm_g[l][None], weights[l][0])
        for k, val in (("norm_g", d_ng[0]), ("pool_w", d_pw), ("pool_scale", d_ps[0]), ("lbr", d_lbr),
                       ("lbi", d_lbi), ("wb", d_wb), ("wct", d_wct), ("d_skip", d_dsk[0]), ("glu_b", d_gb[0])):
            small[k][l] = val
        if pending is not None:
            received[l + 1] = _exchange_wait(pending, dx, f"comm_grads_wait_{l + 1}", gather=False)
        srcs = (d_win, d_gw.reshape(N_DEV, SSM_W // N_DEV, SSM_W), d_wout.reshape(N_DEV, MIX // N_DEV, D_MODEL))
        lands = tuple(_with_own_slot(lax.dynamic_index_in_dim(s, my_idx, 0, keepdims=False), my_idx) for s in srcs)
        pending, dep = _exchange_start(srcs, lands, dx, f"comm_grads_start_{l}", gather=False)
    received[0] = _exchange_wait(pending, dx, "comm_grads_wait_0", gather=False)

    shard_res = {}
    for pos, (n, w, m, v) in enumerate((("w_in", w_in, m_w_in, v_w_in), ("glu_w", glu_w, m_glu_w, v_glu_w),
                                        ("w_out", w_out, m_w_out, v_w_out))):
        shard_res[n] = _adamw_summed([received[l][pos] for l in range(depth)], w, m, v, "adamw_" + n)

    stack = lambda k: jnp.stack(small[k])
    d_a_re, d_a_im, d_log_dt, d_b_re, d_b_im, d_c_re, d_c_im = pack_vjp(
        (stack("lbr"), stack("lbi"), stack("wb"), stack("wct")))
    local_grads = [stack("norm_g"), stack("pool_w"), stack("pool_scale"), d_a_re, d_a_im, d_log_dt,
                   d_b_re, d_b_im, d_c_re, d_c_im, stack("d_skip"), stack("glu_b"), d_final_g[0]]
    small_w = [norm_g, pool_w, pool_scale, a_re, a_im, log_dt, b_re, b_im, c_re, c_im, d_skip, glu_b, final_g]
    small_m = [m_norm_g, m_pool_w, m_pool_scale, m_a_re, m_a_im, m_log_dt, m_b_re, m_b_im, m_c_re, m_c_im,
               m_d_skip, m_glu_b, m_final_g]
    small_v = [v_norm_g, v_pool_w, v_pool_scale, v_a_re, v_a_im, v_log_dt, v_b_re, v_b_im, v_c_re, v_c_im,
               v_d_skip, v_glu_b, v_final_g]
    g_packed = _allreduce_packed(_pack(local_grads))
    d_packed, m_packed, v_packed = _adamw_packed(_pack(small_w), g_packed, _pack(small_m), _pack(small_v))
    names = ["norm_g", "pool_w", "pool_scale", "a_re", "a_im", "log_dt", "b_re", "b_im", "c_re", "c_im",
             "d_skip", "glu_b", "final_g"]
    res = {}
    for kind, packed in (("grad", g_packed), ("delta", d_packed), ("m", m_packed), ("v", v_packed)):
        for n, a in zip(names, _unpack(packed, small_w)):
            res[kind, n] = a
    for n in ("w_in", "glu_w", "w_out"):
        for pos, kind in enumerate(("grad", "delta", "m", "v")):
            res[kind, n] = shard_res[n][pos]

    order = ["norm_g", "w_in", "pool_w", "pool_scale", "a_re", "a_im", "log_dt", "b_re", "b_im", "c_re", "c_im",
             "d_skip", "glu_w", "glu_b", "w_out", "final_g"]
    outs = [loss, dx.reshape(nb, seq, D_MODEL)]
    for kind in ("grad", "delta", "m", "v"):
        outs += [res[kind, n] for n in order]
    return tuple(outs)
```

```python
import functools
import math

import jax
import jax.numpy as jnp
from jax import lax
from jax.experimental import pallas as pl
from jax.experimental.pallas import tpu as pltpu

F32 = jnp.float32
MXU_DTYPE = jnp.bfloat16

D_MODEL = 1024
MIX = 1024
POOL_W = 512
SSM_W = 512
N_POOL_G = 4
POOL_GC = 128
SSM_G = 32
SSM_C = 16
SSM_P = 64
DEPTH = 4
NORM_EPS = 1e-5
N_DEV = 8

ADAM_LR = 0.001
ADAM_B1 = 0.9
ADAM_B2 = 0.999
ADAM_EPS = 1e-08
ADAM_WD = 0.01
ADAM_STEP = 10

SUBLANES = 8
LANES = 128
HALO = 16
STATE_ROWS = 8
STATE_COLS = 256
T_BLK = 256
TM_FWD = 512
TM_BWD = 256
VMEM_LIMIT = 56 * 1024 * 1024

MESH = pl.DeviceIdType.MESH
VMEM_SPEC = pl.BlockSpec(memory_space=pltpu.VMEM)
ANY_SPEC = pl.BlockSpec(memory_space=pl.ANY)


def _mm(a, b):
    return jnp.dot(a, b, preferred_element_type=F32)


def _mm_tn(a, b):
    return lax.dot_general(a, b, (((0,), (0,)), ((), ())), preferred_element_type=F32)


def _mm_nt(a, b):
    return lax.dot_general(a, b, (((1,), (1,)), ((), ())), preferred_element_type=F32)


def _mx(a):
    return a.astype(MXU_DTYPE)


def _sigmoid(v):
    return 1.0 / (1.0 + jnp.exp(-v))


_GELU_C = math.sqrt(2.0 / math.pi)
_GELU_A = 0.044715


def _gelu_and_grad(y):
    th = jnp.tanh(_GELU_C * (y + _GELU_A * y * y * y))
    val = 0.5 * y * (1.0 + th)
    grad = 0.5 * (1.0 + th) + 0.5 * y * (1.0 - th * th) * (_GELU_C * (1.0 + 3.0 * _GELU_A * y * y))
    return val, grad


def _params(**kw):
    return pltpu.CompilerParams(vmem_limit_bytes=VMEM_LIMIT, **kw)


def _ssm_dense(a_re, a_im, log_dt, b_re, b_im, c_re, c_im):
    dt = jnp.exp(log_dt)[:, None]
    mag = jnp.exp(a_re * dt)
    ang = a_im * dt
    lb_re = mag * jnp.cos(ang)
    lb_im = mag * jnp.sin(ang)
    den = a_re * a_re + a_im * a_im
    n_re = lb_re - 1.0
    n_im = lb_im
    f_re = (n_re * a_re + n_im * a_im) / den
    f_im = (n_im * a_re - n_re * a_im) / den
    bb_re = f_re[..., None] * b_re - f_im[..., None] * b_im
    bb_im = f_re[..., None] * b_im + f_im[..., None] * b_re

    bb = jnp.stack([bb_re, bb_im], axis=0).reshape(2, 8, 4, SSM_P, SSM_C)
    rb = bb.transpose(1, 4, 0, 2, 3).reshape(8, SSM_C, 512)
    cc = jnp.stack([c_re, -c_im], axis=0).reshape(2, 8, 4, SSM_C, SSM_P)
    rc = cc.transpose(1, 3, 0, 2, 4).reshape(8, SSM_C, 512)
    return (lb_re.reshape(STATE_ROWS, STATE_COLS), lb_im.reshape(STATE_ROWS, STATE_COLS), rb, rc)


def _ssm_chunked(per_channel):
    row_group = jnp.arange(64) // SSM_C
    col_group = (jnp.arange(512) // SSM_P) % 4
    own_group = (row_group[:, None] == col_group[None, :]).astype(F32)
    even = (jnp.arange(8) % 2 == 0).astype(F32)[:, None, None]
    half = jnp.tile(per_channel, (1, 4, 1)) * own_group
    return jnp.concatenate([half * even, half * (1.0 - even)], axis=1)


def _inproj_fwd(x2, g_row, w_all, dep):
    n = x2.shape[0]
    tm = TM_FWD

    def body(x_ref, g_ref, w_ref, dep_ref, z_ref, h_ref):
        x = x_ref[...]
        r = lax.rsqrt(jnp.mean(x * x, axis=-1, keepdims=True) + NORM_EPS)
        h = _mx(x * r * g_ref[...])
        h_ref[...] = h
        for d in range(N_DEV):
            z_ref[:, d * 256:(d + 1) * 256] = _mm(h, w_ref[d])

    return pl.pallas_call(
        body, name="inproj_fwd",
        grid=(n // tm,),
        in_specs=[pl.BlockSpec((tm, D_MODEL), lambda i: (i, 0)),
                  pl.BlockSpec((1, D_MODEL), lambda i: (0, 0)),
                  pl.BlockSpec((N_DEV, D_MODEL, 256), lambda i: (0, 0, 0)),
                  ANY_SPEC],
        out_specs=[pl.BlockSpec((tm, 2 * MIX), lambda i: (i, 0)),
                   pl.BlockSpec((tm, D_MODEL), lambda i: (i, 0))],
        out_shape=[jax.ShapeDtypeStruct((n, 2 * MIX), F32),
                   jax.ShapeDtypeStruct((n, D_MODEL), MXU_DTYPE)],
        compiler_params=_params(dimension_semantics=("arbitrary",)),
    )(x2, g_row, w_all, dep)


def _outproj_fwd(x2, yg, w_out):
    n = x2.shape[0]
    tm = TM_FWD

    def body(x_ref, y_ref, w_ref, o_ref):
        o_ref[...] = x_ref[...] + _mm(y_ref[...], w_ref[...])

    return pl.pallas_call(
        body, name="outproj_fwd",
        grid=(n // tm,),
        in_specs=[pl.BlockSpec((tm, D_MODEL), lambda i: (i, 0)),
                  pl.BlockSpec((tm, MIX), lambda i: (i, 0)),
                  pl.BlockSpec((MIX, D_MODEL), lambda i: (0, 0))],
        out_specs=pl.BlockSpec((tm, D_MODEL), lambda i: (i, 0)),
        out_shape=jax.ShapeDtypeStruct((n, D_MODEL), F32),
        compiler_params=_params(dimension_semantics=("arbitrary",)),
    )(x2, yg, w_out)


def _loss_head(x2, tgt2, g_row):
    n = x2.shape[0]
    tm = TM_FWD

    def body(x_ref, t_ref, g_ref, dx_ref, loss_ref, dg_ref):
        @pl.when(pl.program_id(0) == 0)
        def _():
            loss_ref[...] = jnp.zeros_like(loss_ref)
            dg_ref[...] = jnp.zeros_like(dg_ref)

        x = x_ref[...]
        g = g_ref[...]
        r = lax.rsqrt(jnp.mean(x * x, axis=-1, keepdims=True) + NORM_EPS)
        xh = x * r
        e = xh * g - t_ref[...]
        loss_ref[...] += jnp.sum(jnp.sum(e * e, axis=-1, keepdims=True), axis=0, keepdims=True) * (0.5 / D_MODEL)
        dout = e * (1.0 / D_MODEL)
        dg_ref[...] += jnp.sum(dout * xh, axis=0, keepdims=True)
        gdy = dout * g
        dx_ref[...] = r * (gdy - xh * jnp.mean(xh * gdy, axis=-1, keepdims=True))

    return pl.pallas_call(
        body, name="loss_head",
        grid=(n // tm,),
        in_specs=[pl.BlockSpec((tm, D_MODEL), lambda i: (i, 0)),
                  pl.BlockSpec((tm, D_MODEL), lambda i: (i, 0)),
                  pl.BlockSpec((1, D_MODEL), lambda i: (0, 0))],
        out_specs=[pl.BlockSpec((tm, D_MODEL), lambda i: (i, 0)),
                   pl.BlockSpec((1, 1), lambda i: (0, 0)),
                   pl.BlockSpec((1, D_MODEL), lambda i: (0, 0))],
        out_shape=[jax.ShapeDtypeStruct((n, D_MODEL), F32),
                   jax.ShapeDtypeStruct((1, 1), F32),
                   jax.ShapeDtypeStruct((1, D_MODEL), F32)],
        compiler_params=_params(dimension_semantics=("arbitrary",)),
    )(x2, tgt2, g_row)


def _outproj_bwd(dx2, yg, w_out, dep):
    n = dx2.shape[0]
    tm = TM_BWD
    n_steps = n // tm

    def body(dx_ref, y_ref, w_ref, dep_ref, dy_ref, dw_ref, acc_ref):
        i = pl.program_id(0)

        @pl.when(i == 0)
        def _():
            acc_ref[...] = jnp.zeros_like(acc_ref)

        dxb = _mx(dx_ref[...])
        dy_ref[...] = _mm_nt(dxb, w_ref[...])
        acc_ref[...] += _mm_tn(y_ref[...], dxb)

        @pl.when(i == n_steps - 1)
        def _():
            dw_ref[...] = _mx(acc_ref[...])

    return pl.pallas_call(
        body, name="outproj_bwd",
        grid=(n_steps,),
        in_specs=[pl.BlockSpec((tm, D_MODEL), lambda i: (i, 0)),
                  pl.BlockSpec((tm, MIX), lambda i: (i, 0)),
                  pl.BlockSpec((MIX, D_MODEL), lambda i: (0, 0)),
                  ANY_SPEC],
        out_specs=[pl.BlockSpec((tm, MIX), lambda i: (i, 0)),
                   pl.BlockSpec((MIX, D_MODEL), lambda i: (0, 0))],
        out_shape=[jax.ShapeDtypeStruct((n, MIX), F32),
                   jax.ShapeDtypeStruct((MIX, D_MODEL), MXU_DTYPE)],
        scratch_shapes=[pltpu.VMEM((MIX, D_MODEL), F32)],
        compiler_params=_params(dimension_semantics=("arbitrary",)),
    )(dx2, yg, w_out, dep)


def _inproj_bwd(dz, h, x2, dx_in, g_row, w_all):
    n = x2.shape[0]
    tm = TM_BWD
    n_steps = n // tm

    def body(dz_ref, h_ref, x_ref, dxi_ref, g_ref, w_ref, dxo_ref, dw_ref, dg_ref, acc_ref):
        i = pl.program_id(0)

        @pl.when(i == 0)
        def _():
            acc_ref[...] = jnp.zeros_like(acc_ref)
            dg_ref[...] = jnp.zeros_like(dg_ref)

        hb = h_ref[...]
        dh = jnp.zeros((tm, D_MODEL), F32)
        for d in range(N_DEV):
            dzd = dz_ref[:, d * 256:(d + 1) * 256]
            acc_ref[d] += _mm_tn(hb, dzd)
            dh = dh + _mm_nt(dzd, w_ref[d])
        x = x_ref[...]
        r = lax.rsqrt(jnp.mean(x * x, axis=-1, keepdims=True) + NORM_EPS)
        xh = x * r
        dg_ref[...] += jnp.sum(dh * xh, axis=0, keepdims=True)
        gdy = dh * g_ref[...]
        dxo_ref[...] = dxi_ref[...] + r * (gdy - xh * jnp.mean(xh * gdy, axis=-1, keepdims=True))

        @pl.when(i == n_steps - 1)
        def _():
            dw_ref[...] = _mx(acc_ref[...])

    return pl.pallas_call(
        body, name="inproj_bwd",
        grid=(n_steps,),
        in_specs=[pl.BlockSpec((tm, 2 * MIX), lambda i: (i, 0)),
                  pl.BlockSpec((tm, D_MODEL), lambda i: (i, 0)),
                  pl.BlockSpec((tm, D_MODEL), lambda i: (i, 0)),
                  pl.BlockSpec((tm, D_MODEL), lambda i: (i, 0)),
                  pl.BlockSpec((1, D_MODEL), lambda i: (0, 0)),
                  pl.BlockSpec((N_DEV, D_MODEL, 256), lambda i: (0, 0, 0))],
        out_specs=[pl.BlockSpec((tm, D_MODEL), lambda i: (i, 0)),
                   pl.BlockSpec((N_DEV, D_MODEL, 256), lambda i: (0, 0, 0)),
                   pl.BlockSpec((1, D_MODEL), lambda i: (0, 0))],
        out_shape=[jax.ShapeDtypeStruct((n, D_MODEL), F32),
                   jax.ShapeDtypeStruct((N_DEV, D_MODEL, 256), MXU_DTYPE),
                   jax.ShapeDtypeStruct((1, D_MODEL), F32)],
        scratch_shapes=[pltpu.VMEM((N_DEV, D_MODEL, 256), F32)],
        compiler_params=_params(dimension_semantics=("arbitrary",)),
    )(dz, h, x2, dx_in, g_row, w_all)


def _row_pos(t0, rows):
    return t0 + lax.broadcasted_iota(jnp.int32, (rows, LANES), 0)


def _pool_window_mean(upad, g, t0, t_blk):
    k = 2 << g
    w = upad
    sh = 1
    while sh < k:
        w = w + pltpu.roll(w, sh, 0)
        sh *= 2
    count = jnp.minimum(_row_pos(t0, t_blk) + 1, k).astype(F32)
    return w[HALO:] / count - upad[HALO:]


def _pool_window_bwd(qpad, g, t_blk):
    k = 2 << g
    n = t_blk + HALO
    w = qpad
    sh = 1
    while sh < k:
        w = w + pltpu.roll(w, n - sh, 0)
        sh *= 2
    return w[:t_blk]


class _StateBuf:
    def __init__(self, refs, t_blk):
        self.refs = refs
        self.t_blk = t_blk

    def put_chunk(self, b, j, val):
        for c in range(4):
            self.refs[4 * b + c][pl.ds(j, self.t_blk, stride=STATE_ROWS), :] = val[:, c * LANES:(c + 1) * LANES]

    def get_chunk(self, b, j):
        return jnp.concatenate(
            [self.refs[4 * b + c][pl.ds(j, self.t_blk, stride=STATE_ROWS), :] for c in range(4)], axis=-1)

    def load(self, b, r, part):
        return jnp.concatenate(
            [self.refs[4 * b + 2 * part + h][pl.ds(r, STATE_ROWS), :] for h in range(2)], axis=-1)

    def store(self, b, r, part, val):
        for h in range(2):
            self.refs[4 * b + 2 * part + h][pl.ds(r, STATE_ROWS), :] = val[:, h * LANES:(h + 1) * LANES]


def _state_scratch(nb, t_blk):
    return [pltpu.VMEM((t_blk * STATE_ROWS, LANES), F32) for _ in range(4 * nb)]


def _ssm_project_in(u_ssm, wb_ref, buf, b):
    ub = _mx(u_ssm)
    for j in range(STATE_ROWS):
        m = j // 2
        buf.put_chunk(b, j, _mm(ub[:, m * LANES:(m + 1) * LANES], wb_ref[j]))


def _scan_forward(buf, lbr, lbi, init, nb):
    def body(t, carry):
        r = pl.multiple_of(t * STATE_ROWS, STATE_ROWS)
        out = []
        for b in range(nb):
            sr, si = carry[2 * b], carry[2 * b + 1]
            nr = lbr * sr - lbi * si + buf.load(b, r, 0)
            ni = lbr * si + lbi * sr + buf.load(b, r, 1)
            buf.store(b, r, 0, nr)
            buf.store(b, r, 1, ni)
            out += [nr, ni]
        return tuple(out)

    return lax.fori_loop(0, buf.t_blk, body, init, unroll=4)


def _ssm_project_out(chunk, wc_ref):
    tiles = []
    for m in range(4):
        acc = None
        for j in (2 * m, 2 * m + 1):
            part = _mm_nt(chunk(j), wc_ref[j])
            acc = part if acc is None else acc + part
        tiles.append(acc)
    return jnp.concatenate(tiles, axis=-1)


def _mixer_fwd(z3, pool_w, pool_scale, lbr, lbi, wb, wc, d_skip, glu_w, glu_b):
    nb, seq, _ = z3.shape
    t_blk = min(T_BLK, seq)
    n_t = seq // t_blk
    halo_per_blk = t_blk // HALO

    def body(z_ref, zh_ref, pw_ref, ps_ref, lbr_ref, lbi_ref, wb_ref, wc_ref, dsk_ref, gw_ref, gb_ref,
             yg_ref, sc_ref, carry_ref, *s_refs):
        i = pl.program_id(0)
        t0 = i * t_blk
        buf = _StateBuf(s_refs, t_blk)

        @pl.when(i == 0)
        def _():
            carry_ref[...] = jnp.zeros_like(carry_ref)

        for b in range(nb):
            _ssm_project_in(z_ref[b, :, POOL_W:MIX], wb_ref, buf, b)
        init = tuple(carry_ref[b, :, h * STATE_COLS:(h + 1) * STATE_COLS] for b in range(nb) for h in range(2))
        fin = _scan_forward(buf, lbr_ref[...], lbi_ref[...], init, nb)
        for b in range(nb):
            carry_ref[b, :, 0:STATE_COLS] = fin[2 * b]
            carry_ref[b, :, STATE_COLS:2 * STATE_COLS] = fin[2 * b + 1]

        first = (i == 0)
        for b in range(nb):
            u_ssm = z_ref[b, :, POOL_W:MIX]

            def chunk(j, b=b):
                states = _mx(buf.get_chunk(b, j))
                sc_ref[b, j] = states
                return states

            y = _ssm_project_out(chunk, wc_ref) + dsk_ref[...] * u_ssm
            yg, _ = _gelu_and_grad(y)
            v = _mm(_mx(yg), gw_ref[...]) + gb_ref[...]
            o_ssm = yg * _sigmoid(v)
            gp = z_ref[b, :, MIX + POOL_W:2 * MIX]
            yg_ref[b, :, POOL_W:MIX] = _mx(o_ssm * (gp * _sigmoid(gp)))
            for g in range(N_POOL_G):
                cols = slice(g * POOL_GC, (g + 1) * POOL_GC)
                halo = jnp.where(first, 0.0, zh_ref[b, :, cols])
                upad = jnp.concatenate([halo, z_ref[b, :, cols]], axis=0)
                pooled = _pool_window_mean(upad, g, t0, t_blk)
                yp = _mm(_mx(pooled), pw_ref[g]) * ps_ref[:, cols]
                gpp = z_ref[b, :, MIX + g * POOL_GC:MIX + (g + 1) * POOL_GC]
                yg_ref[b, :, cols] = _mx(yp * (gpp * _sigmoid(gpp)))

    const = lambda *shape: pl.BlockSpec(shape, lambda i: (0,) * len(shape))
    return pl.pallas_call(
        body, name="mixer_fwd",
        grid=(n_t,),
        in_specs=[pl.BlockSpec((nb, t_blk, 2 * MIX), lambda i: (0, i, 0)),
                  pl.BlockSpec((nb, HALO, POOL_W), lambda i: (0, jnp.maximum(i * halo_per_blk - 1, 0), 0)),
                  const(N_POOL_G, POOL_GC, POOL_GC), const(1, POOL_W),
                  const(STATE_ROWS, STATE_COLS), const(STATE_ROWS, STATE_COLS),
                  const(STATE_ROWS, LANES, 2 * STATE_COLS), const(STATE_ROWS, LANES, 2 * STATE_COLS),
                  const(1, SSM_W), const(SSM_W, SSM_W), const(1, SSM_W)],
        out_specs=[pl.BlockSpec((nb, t_blk, MIX), lambda i: (0, i, 0)),
                   pl.BlockSpec((nb, STATE_ROWS, t_blk, 2 * STATE_COLS), lambda i: (0, 0, i, 0))],
        out_shape=[jax.ShapeDtypeStruct((nb, seq, MIX), MXU_DTYPE),
                   jax.ShapeDtypeStruct((nb, STATE_ROWS, seq, 2 * STATE_COLS), MXU_DTYPE)],
        scratch_shapes=[pltpu.VMEM((nb, STATE_ROWS, 2 * STATE_COLS), F32)] + _state_scratch(nb, t_blk),
        compiler_params=_params(dimension_semantics=("arbitrary",)),
    )(z3, z3, pool_w, pool_scale, lbr, lbi, wb, wc, d_skip, glu_w, glu_b)


def _mixer_bwd(z3, dy3, states, pool_w, pool_scale, lbr, lbi, wb, wc, d_skip, glu_w, glu_b):
    nb, seq, _ = z3.shape
    t_blk = min(T_BLK, seq)
    n_t = seq // t_blk
    halo_per_blk = t_blk // HALO

    def body(z_ref, zh_ref, dy_ref, sc_ref, sch_ref, pw_ref, ps_ref, lbr_ref, lbi_ref, wb_ref, wc_ref, dsk_ref,
             gw_ref, gb_ref,
             dz_ref, dpw_ref, dps_ref, dlbr_ref, dlbi_ref, dwb_ref, dwc_ref, ddsk_ref, dgw_ref, dgb_ref,
             gcarry_ref, qcarry_ref, du_ref, dgw_acc, *g_refs):
        i = pl.program_id(0)
        blk = n_t - 1 - i
        t0 = blk * t_blk
        gbuf = _StateBuf(g_refs, t_blk)

        @pl.when(i == 0)
        def _():
            gcarry_ref[...] = jnp.zeros_like(gcarry_ref)
            qcarry_ref[...] = jnp.zeros_like(qcarry_ref)
            for ref in (dpw_ref, dps_ref, dlbr_ref, dlbi_ref, dwb_ref, dwc_ref, ddsk_ref, dgw_acc, dgb_ref):
                ref[...] = jnp.zeros_like(ref)

        lbr_v = lbr_ref[...]
        lbi_v = lbi_ref[...]

        first = (blk == 0)
        for b in range(nb):
            u_ssm = z_ref[b, :, POOL_W:MIX]
            y = _ssm_project_out(lambda j, b=b: sc_ref[b, j], wc_ref) + dsk_ref[...] * u_ssm
            yg, dgelu = _gelu_and_grad(y)
            ygb = _mx(yg)
            sg = _sigmoid(_mm(ygb, gw_ref[...]) + gb_ref[...])
            o_ssm = yg * sg
            gp = z_ref[b, :, MIX + POOL_W:2 * MIX]
            sgm = _sigmoid(gp)
            dyv = dy_ref[b, :, POOL_W:MIX]
            dz_ref[b, :, MIX + POOL_W:2 * MIX] = _mx(dyv * o_ssm * (sgm * (1.0 + gp * (1.0 - sgm))))
            do = dyv * (gp * sgm)
            dv = do * yg * (sg * (1.0 - sg))
            dvb = _mx(dv)
            dgb_ref[...] += jnp.sum(dv, axis=0, keepdims=True)
            dgw_acc[...] += _mm_tn(ygb, dvb)
            dyp = (do * sg + _mm_nt(dvb, gw_ref[...])) * dgelu
            ddsk_ref[...] += jnp.sum(dyp * u_ssm, axis=0, keepdims=True)
            dypb = _mx(dyp)
            for j in range(STATE_ROWS):
                m = j // 2
                dyt = dypb[:, m * LANES:(m + 1) * LANES]
                gbuf.put_chunk(b, j, _mm(dyt, wc_ref[j]))
                dwc_ref[j] += _mm_tn(dyt, sc_ref[b, j])
            du_ref[b] = dsk_ref[...] * dyp

            for g in range(N_POOL_G):
                cols = slice(g * POOL_GC, (g + 1) * POOL_GC)
                halo = jnp.where(first, 0.0, zh_ref[b, :, cols])
                u_g = z_ref[b, :, cols]
                pooled = _pool_window_mean(jnp.concatenate([halo, u_g], axis=0), g, t0, t_blk)
                pb = _mx(pooled)
                ypre = _mm(pb, pw_ref[g])
                gpp = z_ref[b, :, MIX + g * POOL_GC:MIX + (g + 1) * POOL_GC]
                sgp = _sigmoid(gpp)
                dyg = dy_ref[b, :, cols]
                scale = ps_ref[:, cols]
                dz_ref[b, :, MIX + g * POOL_GC:MIX + (g + 1) * POOL_GC] = _mx(
                    dyg * (ypre * scale) * (sgp * (1.0 + gpp * (1.0 - sgp))))
                dyc = dyg * (gpp * sgp)
                dps_ref[:, cols] += jnp.sum(dyc * ypre, axis=0, keepdims=True)
                dypre = _mx(dyc * scale)
                dpw_ref[g] += _mm_tn(pb, dypre)
                dpooled = _mm_nt(dypre, pw_ref[g])
                count = jnp.minimum(_row_pos(t0, t_blk) + 1, 2 << g).astype(F32)
                q = dpooled / count
                qpad = jnp.concatenate([q, qcarry_ref[b, :, cols]], axis=0)
                qcarry_ref[b, :, cols] = q[:HALO]
                dz_ref[b, :, cols] = _mx(_pool_window_bwd(qpad, g, t_blk) - dpooled)

        def rev_body(k, carry):
            r = pl.multiple_of((t_blk - 1 - k) * STATE_ROWS, STATE_ROWS)
            out = []
            for b in range(nb):
                gr, gi = carry[2 * b], carry[2 * b + 1]
                ngr = lbr_v * gr + lbi_v * gi + gbuf.load(b, r, 0)
                ngi = lbr_v * gi - lbi_v * gr + gbuf.load(b, r, 1)
                gbuf.store(b, r, 0, ngr)
                gbuf.store(b, r, 1, ngi)
                out += [ngr, ngi]
            return tuple(out)

        init_g = tuple(gcarry_ref[b, :, h * STATE_COLS:(h + 1) * STATE_COLS] for b in range(nb) for h in range(2))
        fin = lax.fori_loop(0, t_blk, rev_body, init_g, unroll=4)
        for b in range(nb):
            gcarry_ref[b, :, 0:STATE_COLS] = fin[2 * b]
            gcarry_ref[b, :, STATE_COLS:2 * STATE_COLS] = fin[2 * b + 1]

        for b in range(nb):
            ub = _mx(z_ref[b, :, POOL_W:MIX])
            for m in range(4):
                acc = du_ref[b, :, m * LANES:(m + 1) * LANES]
                for j in (2 * m, 2 * m + 1):
                    g = gbuf.get_chunk(b, j)
                    gj = _mx(g)
                    acc = acc + _mm_nt(gj, wb_ref[j])
                    dwb_ref[j] += _mm_tn(ub[:, m * LANES:(m + 1) * LANES], gj)
                    before = jnp.where(first, 0.0, sch_ref[b, j].astype(F32))
                    spad = jnp.concatenate([before, sc_ref[b, j].astype(F32)], axis=0)
                    s_prev = pltpu.roll(spad, 1, 0)[HALO:]
                    g_re, g_im = g[:, :STATE_COLS], g[:, STATE_COLS:]
                    p_re, p_im = s_prev[:, :STATE_COLS], s_prev[:, STATE_COLS:]
                    dlbr_ref[j:j + 1, :] += jnp.sum(g_re * p_re + g_im * p_im, axis=0, keepdims=True)
                    dlbi_ref[j:j + 1, :] += jnp.sum(g_im * p_re - g_re * p_im, axis=0, keepdims=True)
                dz_ref[b, :, POOL_W + m * LANES:POOL_W + (m + 1) * LANES] = _mx(acc)

        @pl.when(i == n_t - 1)
        def _():
            dgw_ref[...] = _mx(dgw_acc[...])

    const = lambda *shape: pl.BlockSpec(shape, lambda i: (0,) * len(shape))
    rev = lambda i: n_t - 1 - i
    out_shape = [jax.ShapeDtypeStruct((nb, seq, 2 * MIX), MXU_DTYPE),
                 jax.ShapeDtypeStruct((N_POOL_G, POOL_GC, POOL_GC), F32),
                 jax.ShapeDtypeStruct((1, POOL_W), F32),
                 jax.ShapeDtypeStruct((STATE_ROWS, STATE_COLS), F32),
                 jax.ShapeDtypeStruct((STATE_ROWS, STATE_COLS), F32),
                 jax.ShapeDtypeStruct((STATE_ROWS, LANES, 2 * STATE_COLS), F32),
                 jax.ShapeDtypeStruct((STATE_ROWS, LANES, 2 * STATE_COLS), F32),
                 jax.ShapeDtypeStruct((1, SSM_W), F32),
                 jax.ShapeDtypeStruct((SSM_W, SSM_W), MXU_DTYPE),
                 jax.ShapeDtypeStruct((1, SSM_W), F32)]
    return pl.pallas_call(
        body, name="mixer_bwd",
        grid=(n_t,),
        in_specs=[pl.BlockSpec((nb, t_blk, 2 * MIX), lambda i: (0, rev(i), 0)),
                  pl.BlockSpec((nb, HALO, POOL_W), lambda i: (0, jnp.maximum(rev(i) * halo_per_blk - 1, 0), 0)),
                  pl.BlockSpec((nb, t_blk, MIX), lambda i: (0, rev(i), 0)),
                  pl.BlockSpec((nb, STATE_ROWS, t_blk, 2 * STATE_COLS), lambda i: (0, 0, rev(i), 0)),
                  pl.BlockSpec((nb, STATE_ROWS, HALO, 2 * STATE_COLS),
                               lambda i: (0, 0, jnp.maximum(rev(i) * halo_per_blk - 1, 0), 0)),
                  const(N_POOL_G, POOL_GC, POOL_GC), const(1, POOL_W),
                  const(STATE_ROWS, STATE_COLS), const(STATE_ROWS, STATE_COLS),
                  const(STATE_ROWS, LANES, 2 * STATE_COLS), const(STATE_ROWS, LANES, 2 * STATE_COLS),
                  const(1, SSM_W), const(SSM_W, SSM_W), const(1, SSM_W)],
        out_specs=[pl.BlockSpec((nb, t_blk, 2 * MIX), lambda i: (0, rev(i), 0))]
                  + [const(*s.shape) for s in out_shape[1:]],
        out_shape=out_shape,
        scratch_shapes=[pltpu.VMEM((nb, STATE_ROWS, 2 * STATE_COLS), F32),
                        pltpu.VMEM((nb, HALO, POOL_W), F32),
                        pltpu.VMEM((nb, t_blk, SSM_W), F32),
                        pltpu.VMEM((SSM_W, SSM_W), F32)]
                       + _state_scratch(nb, t_blk),
        compiler_params=_params(dimension_semantics=("arbitrary",)),
    )(z3, z3, dy3, states, states, pool_w, pool_scale, lbr, lbi, wb, wc, d_skip, glu_w, glu_b)


def _mesh_place():
    x, y, c = lax.axis_index("x"), lax.axis_index("y"), lax.axis_index("c")
    return x, y, c


def _flip(place, k):
    x, y, c = place
    return (1 - x if k & 4 else x, 1 - y if k & 2 else y, 1 - c if k & 1 else c)


def _index(place):
    x, y, c = place
    return 4 * x + 2 * y + c


HBM_SPEC = pl.BlockSpec(memory_space=pltpu.HBM)
SEM_SPEC = pl.BlockSpec(memory_space=pltpu.SEMAPHORE)
_EFFECT = pltpu.SideEffectType.DATAFLOW_SIDE_EFFECTING
N_PEERS = N_DEV - 1


def _exchange_copies(src_refs, land_refs, send_sems, recv_sems):
    me = _mesh_place()
    mine = _index(me)
    out = []
    for a, land_ref in enumerate(land_refs):
        for k in range(1, N_DEV):
            peer = _flip(me, k)
            theirs = _index(peer)
            n = a * N_PEERS + k - 1
            src = src_refs[a].at[theirs] if src_refs else land_ref.at[mine]
            send = pltpu.make_async_remote_copy(
                src_ref=src, dst_ref=land_ref.at[mine], send_sem=send_sems.at[n], recv_sem=recv_sems.at[n],
                device_id=peer, device_id_type=MESH)
            recv = pltpu.make_async_remote_copy(
                src_ref=src, dst_ref=land_ref.at[theirs], send_sem=send_sems.at[n], recv_sem=recv_sems.at[n],
                device_id=peer, device_id_type=MESH)
            out.append((send, recv))
    return out


def _exchange_start(srcs, lands, after, name):
    arrays = tuple(srcs) + tuple(lands)
    n_src, n_all = len(srcs), len(arrays)
    n_copies = len(lands) * N_PEERS

    def body(*refs):
        send_sems, recv_sems = refs[n_all + 1], refs[n_all + 2]
        token = refs[-1]
        for send, _ in _exchange_copies(refs[:n_src], refs[n_src:n_all], send_sems, recv_sems):
            send.start()
        token[...] = jnp.zeros_like(token)

    res = pl.pallas_call(
        body, name=name,
        in_specs=[HBM_SPEC] * n_all + [ANY_SPEC],
        out_specs=[SEM_SPEC, SEM_SPEC] + [HBM_SPEC] * n_all + [VMEM_SPEC],
        out_shape=[pltpu.SemaphoreType.DMA((n_copies,)), pltpu.SemaphoreType.DMA((n_copies,))]
                  + [pltpu.HBM(a.shape, a.dtype) for a in arrays] + [jax.ShapeDtypeStruct((SUBLANES, LANES), F32)],
        input_output_aliases={i: 2 + i for i in range(n_all)},
        compiler_params=pltpu.CompilerParams(has_side_effects=_EFFECT),
    )(*[pltpu.with_memory_space_constraint(a, pltpu.HBM) for a in arrays], after)
    return tuple(res[:-1]), res[-1]


def _exchange_wait(handle, n_lands, after, name):
    send_sems, recv_sems = handle[0], handle[1]
    arrays = handle[2:]
    n_all = len(arrays)
    n_src = n_all - n_lands

    def body(*refs):
        for send, recv in _exchange_copies(refs[:n_src], refs[n_src:n_all], refs[n_all], refs[n_all + 1]):
            send.wait_send()
            recv.wait_recv()

    res = pl.pallas_call(
        body, name=name,
        in_specs=[HBM_SPEC] * n_all + [SEM_SPEC, SEM_SPEC, ANY_SPEC],
        out_specs=[HBM_SPEC] * n_all,
        out_shape=[pltpu.HBM(a.shape, a.dtype) for a in arrays],
        input_output_aliases={i: i for i in range(n_all)},
        compiler_params=pltpu.CompilerParams(has_side_effects=_EFFECT),
    )(*arrays, send_sems, recv_sems, after)
    return tuple(res[n_src:])


def _weight_zones(w_in, glu_w, w_out, my_idx):
    shards = (w_in, glu_w, w_out)
    depth = w_in.shape[0]

    def body(idx_ref, *refs):
        ins, zones = refs[:len(shards)], refs[len(shards):]
        for l in range(depth):
            for a, src in enumerate(ins):
                zones[l * len(shards) + a][0] = _mx(src[l])

    whole = lambda s: pl.BlockSpec(s.shape, lambda i, idx: (0,) * s.ndim)
    return pl.pallas_call(
        body, name="weight_zones",
        grid_spec=pltpu.PrefetchScalarGridSpec(
            num_scalar_prefetch=1, grid=(1,),
            in_specs=[whole(s) for s in shards],
            out_specs=[pl.BlockSpec((1,) + s.shape[1:], lambda i, idx: (idx[0], 0, 0))
                       for _ in range(depth) for s in shards]),
        out_shape=[jax.ShapeDtypeStruct((N_DEV,) + s.shape[1:], MXU_DTYPE) for _ in range(depth) for s in shards],
        compiler_params=_params(dimension_semantics=("arbitrary",)),
    )(my_idx.reshape(1).astype(jnp.int32), *shards)


def _allreduce_packed(p):
    rows = p.shape[0]
    chunk = rows // N_DEV

    def body(p_ref, o_ref, recv_ref, send_sems, recv_sems):
        me = _mesh_place()
        mine = pl.multiple_of(_index(me) * chunk, SUBLANES)
        scatter = []
        for k in range(1, N_DEV):
            peer = _flip(me, k)
            cp = pltpu.make_async_remote_copy(
                src_ref=p_ref.at[pl.ds(pl.multiple_of(_index(peer) * chunk, SUBLANES), chunk)],
                dst_ref=recv_ref.at[k - 1],
                send_sem=send_sems.at[k - 1], recv_sem=recv_sems.at[k - 1],
                device_id=peer, device_id_type=MESH)
            cp.start()
            scatter.append(cp)
        total = p_ref[pl.ds(mine, chunk), :]
        for k in range(1, N_DEV):
            scatter[k - 1].wait()
            total = total + recv_ref[k - 1]
        o_ref[pl.ds(mine, chunk), :] = total
        gather = []
        for k in range(1, N_DEV):
            cp = pltpu.make_async_remote_copy(
                src_ref=o_ref.at[pl.ds(mine, chunk)],
                dst_ref=o_ref.at[pl.ds(mine, chunk)],
                send_sem=send_sems.at[6 + k], recv_sem=recv_sems.at[6 + k],
                device_id=_flip(me, k), device_id_type=MESH)
            cp.start()
            gather.append(cp)
        for k in range(1, N_DEV):
            theirs = pl.multiple_of(_index(_flip(me, k)) * chunk, SUBLANES)
            recv = pltpu.make_async_remote_copy(
                src_ref=o_ref.at[pl.ds(theirs, chunk)], dst_ref=o_ref.at[pl.ds(theirs, chunk)],
                send_sem=send_sems.at[6 + k], recv_sem=recv_sems.at[6 + k],
                device_id=_flip(me, k), device_id_type=MESH)
            recv.wait_recv()
        for cp in gather:
            cp.wait_send()

    return pl.pallas_call(
        body, name="comm_allreduce_packed",
        in_specs=[VMEM_SPEC],
        out_specs=VMEM_SPEC,
        out_shape=jax.ShapeDtypeStruct(p.shape, F32),
        scratch_shapes=[pltpu.VMEM((N_DEV - 1, chunk, LANES), F32),
                        pltpu.SemaphoreType.DMA((2 * (N_DEV - 1),)),
                        pltpu.SemaphoreType.DMA((2 * (N_DEV - 1),))],
        compiler_params=_params(),
    )(p)


def _adamw_math(w, g, m, v):
    m = ADAM_B1 * m + (1.0 - ADAM_B1) * g
    v = ADAM_B2 * v + (1.0 - ADAM_B2) * (g * g)
    m_hat = m / (1.0 - ADAM_B1 ** ADAM_STEP)
    v_hat = v / (1.0 - ADAM_B2 ** ADAM_STEP)
    delta = -ADAM_LR * (m_hat / (jnp.sqrt(v_hat) + ADAM_EPS) + ADAM_WD * w)
    return delta, m, v


def _adamw_summed(received, own, my_idx, w, m, v, name):
    depth, r, c = w.shape
    tr = min(r, 128)

    def body(idx_ref, *refs):
        r_refs, o_refs = refs[:depth], refs[depth:2 * depth]
        w_ref, m_ref, v_ref, g_ref, d_ref, nm_ref, nv_ref = refs[2 * depth:]
        me = idx_ref[0]
        for l in range(depth):
            g = jnp.zeros((tr, c), F32)
            for q in range(N_DEV):
                g = g + jnp.where(q == me, o_refs[l][0], r_refs[l][q]).astype(F32)
            g_ref[l] = g
            d_ref[l], nm_ref[l], nv_ref[l] = _adamw_math(w_ref[l], g, m_ref[l], v_ref[l])

    blk = pl.BlockSpec((depth, tr, c), lambda i, idx: (0, i, 0))
    return pl.pallas_call(
        body, name=name,
        grid_spec=pltpu.PrefetchScalarGridSpec(
            num_scalar_prefetch=1, grid=(r // tr,),
            in_specs=[pl.BlockSpec((N_DEV, tr, c), lambda i, idx: (0, i, 0))] * depth
                     + [pl.BlockSpec((1, tr, c), lambda i, idx: (idx[0], i, 0))] * depth
                     + [blk, blk, blk],
            out_specs=[blk] * 4),
        out_shape=[jax.ShapeDtypeStruct((depth, r, c), F32)] * 4,
        compiler_params=_params(dimension_semantics=("arbitrary",)),
    )(my_idx.reshape(1).astype(jnp.int32), *received, *own, w, m, v)


def _adamw_small(ws, gs, ms, vs):
    n = len(ws)
    depth = ws[0].shape[0]
    quarters = 4

    def spec(a):
        per_layer = a.shape[0] == depth
        split = a.ndim >= 3 and a.shape[1] % quarters == 0 and a.shape[1] >= quarters
        block = (1, a.shape[1] // quarters if split else a.shape[1]) + a.shape[2:]
        rest = (0,) * (a.ndim - 2)
        return pl.BlockSpec(block, lambda l, s: ((l if per_layer else 0), (s if split else 0)) + rest)

    def body(*refs):
        w_refs, g_refs, m_refs, v_refs = (refs[k * n:(k + 1) * n] for k in range(4))
        d_refs, nm_refs, nv_refs = (refs[(4 + k) * n:(5 + k) * n] for k in range(3))
        for k in range(n):
            d_refs[k][...], nm_refs[k][...], nv_refs[k][...] = _adamw_math(
                w_refs[k][...], g_refs[k][...], m_refs[k][...], v_refs[k][...])

    specs = [spec(a) for a in ws]
    shapes = [jax.ShapeDtypeStruct(a.shape, F32) for a in ws]
    res = pl.pallas_call(
        body, name="adamw_small",
        grid=(depth, quarters),
        in_specs=specs * 4,
        out_specs=specs * 3,
        out_shape=shapes * 3,
        compiler_params=_params(dimension_semantics=("arbitrary", "arbitrary")),
    )(*ws, *gs, *ms, *vs)
    return res[:n], res[n:2 * n], res[2 * n:]


_PACK_ROWS = SUBLANES * N_DEV


def _pack(arrays):
    flat = jnp.concatenate([a.reshape(-1) for a in arrays])
    per = _PACK_ROWS * LANES
    total = -(-flat.shape[0] // per) * per
    flat = jnp.pad(flat, (0, total - flat.shape[0]))
    return flat.reshape(total // LANES, LANES)


def _unpack(packed, like):
    flat = packed.reshape(-1)
    out = []
    off = 0
    for a in like:
        out.append(flat[off:off + a.size].reshape(a.shape))
        off += a.size
    return out


def kernel(x, norm_g, w_in, pool_w, pool_scale, a_re, a_im, log_dt, b_re, b_im, c_re, c_im, d_skip, glu_w, glu_b, w_out, final_g, loss_target, m_norm_g, m_w_in, m_pool_w, m_pool_scale, m_a_re, m_a_im, m_log_dt, m_b_re, m_b_im, m_c_re, m_c_im, m_d_skip, m_glu_w, m_glu_b, m_w_out, m_final_g, v_norm_g, v_w_in, v_pool_w, v_pool_scale, v_a_re, v_a_im, v_log_dt, v_b_re, v_b_im, v_c_re, v_c_im, v_d_skip, v_glu_w, v_glu_b, v_w_out, v_final_g):
    nb, seq, _ = x.shape
    n_tok = nb * seq
    depth = norm_g.shape[0]

    my_idx = _index(_mesh_place())

    zones = _weight_zones(w_in, glu_w, w_out, my_idx)

    def gather_start(l, after):
        return _exchange_start((), zones[3 * l:3 * l + 3], after, f"comm_gather_start_{l}")

    def gather_wait(handle, after, l):
        win, glu, wout = _exchange_wait(handle, 3, after, f"comm_gather_wait_{l}")
        return win, glu.reshape(SSM_W, SSM_W), wout.reshape(MIX, D_MODEL)

    (lbr, lbi, rb, rc), dense_vjp = jax.vjp(jax.vmap(_ssm_dense), a_re, a_im, log_dt, b_re, b_im, c_re, c_im)
    chunk_all = jax.vmap(_ssm_chunked)
    (wb, wct), chunk_vjp = jax.vjp(lambda p, q: (chunk_all(p), chunk_all(q)), rb, rc)
    wb_m, wct_m = _mx(wb), _mx(wct)
    pool_w_m = _mx(pool_w)

    def layer_params(l):
        return (pool_w_m[l], pool_scale[l][None], lbr[l], lbi[l], wb_m[l], wct_m[l], d_skip[l][None],
                weights[l][1], glu_b[l][None])

    xs = [x.reshape(n_tok, D_MODEL)]
    saved = []
    weights = []
    handle, dep = gather_start(0, xs[0])
    for l in range(depth):
        weights.append(gather_wait(handle, xs[-1], l))
        if l + 1 < depth:
            handle, dep = gather_start(l + 1, weights[l][0])
        z, h = _inproj_fwd(xs[-1], norm_g[l][None], weights[l][0], dep)
        z3 = z.reshape(nb, seq, 2 * MIX)
        yg, states = _mixer_fwd(z3, *layer_params(l))
        yg2 = yg.reshape(n_tok, MIX)
        xs.append(_outproj_fwd(xs[-1], yg2, weights[l][2]))
        saved.append((z3, h, yg2, states))

    dx, loss_part, d_final_g = _loss_head(xs[-1], loss_target.reshape(n_tok, D_MODEL), final_g[None])
    loss = lax.psum(loss_part[0, 0], ("x", "y", "c"))

    small = {k: [None] * depth for k in
             ("norm_g", "pool_w", "pool_scale", "lbr", "lbi", "wb", "wct", "d_skip", "glu_b")}
    received = [None] * depth
    sent = [None] * depth
    pending = None
    for l in reversed(range(depth)):
        z3, h, yg2, states = saved[l]
        dy, d_wout = _outproj_bwd(dx, yg2, weights[l][2], dep)
        (dz, d_pw, d_ps, d_lbr, d_lbi, d_wb, d_wct, d_dsk, d_gw, d_gb) = _mixer_bwd(
            z3, dy.reshape(nb, seq, MIX), states, *layer_params(l))
        dx, d_win, d_ng = _inproj_bwd(dz.reshape(n_tok, 2 * MIX), h, xs[l], dx, norm_g[l][None], weights[l][0])
        for k, val in (("norm_g", d_ng[0]), ("pool_w", d_pw), ("pool_scale", d_ps[0]), ("lbr", d_lbr),
                       ("lbi", d_lbi), ("wb", d_wb), ("wct", d_wct), ("d_skip", d_dsk[0]), ("glu_b", d_gb[0])):
            small[k][l] = val
        if pending is not None:
            received[l + 1] = _exchange_wait(pending, 3, dx, f"comm_grads_wait_{l + 1}")
        sent[l] = (d_win, d_gw.reshape(N_DEV, SSM_W // N_DEV, SSM_W), d_wout.reshape(N_DEV, MIX // N_DEV, D_MODEL))
        lands = tuple(lax.empty(s.shape, s.dtype) for s in sent[l])
        pending, dep = _exchange_start(sent[l], lands, dx, f"comm_grads_start_{l}")
    received[0] = _exchange_wait(pending, 3, dx, "comm_grads_wait_0")

    shard_res = {}
    for pos, (n, w, m, v) in enumerate((("w_in", w_in, m_w_in, v_w_in), ("glu_w", glu_w, m_glu_w, v_glu_w),
                                        ("w_out", w_out, m_w_out, v_w_out))):
        shard_res[n] = _adamw_summed([received[l][pos] for l in range(depth)], [sent[l][pos] for l in range(depth)],
                                     my_idx, w, m, v, "adamw_" + n)

    stack = lambda k: jnp.stack(small[k])
    d_rb, d_rc = chunk_vjp((stack("wb"), stack("wct")))
    local = [stack("norm_g"), stack("pool_w"), stack("pool_scale"), stack("lbr"), stack("lbi"), d_rb, d_rc,
             stack("d_skip"), stack("glu_b"), d_final_g[0]]
    (g_norm_g, g_pool_w, g_pool_scale, g_lbr, g_lbi, g_rb, g_rc, g_d_skip, g_glu_b, g_final_g) = _unpack(
        _allreduce_packed(_pack(local)), local)
    g_a_re, g_a_im, g_log_dt, g_b_re, g_b_im, g_c_re, g_c_im = dense_vjp((g_lbr, g_lbi, g_rb, g_rc))

    names = ["norm_g", "pool_w", "pool_scale", "a_re", "a_im", "log_dt", "b_re", "b_im", "c_re", "c_im",
             "d_skip", "glu_b", "final_g"]
    rows = {"norm_g", "pool_scale", "log_dt", "d_skip", "glu_b"}
    small_w = [norm_g, pool_w, pool_scale, a_re, a_im, log_dt, b_re, b_im, c_re, c_im, d_skip, glu_b, final_g]
    small_g = [g_norm_g, g_pool_w, g_pool_scale, g_a_re, g_a_im, g_log_dt, g_b_re, g_b_im, g_c_re, g_c_im,
               g_d_skip, g_glu_b, g_final_g]
    small_m = [m_norm_g, m_pool_w, m_pool_scale, m_a_re, m_a_im, m_log_dt, m_b_re, m_b_im, m_c_re, m_c_im,
               m_d_skip, m_glu_b, m_final_g]
    small_v = [v_norm_g, v_pool_w, v_pool_scale, v_a_re, v_a_im, v_log_dt, v_b_re, v_b_im, v_c_re, v_c_im,
               v_d_skip, v_glu_b, v_final_g]

    def blocked(arrays):
        return [a.reshape(1, 1, -1) if n == "final_g" else a[:, None, :] if n in rows else a
                for n, a in zip(names, arrays)]

    small_d, small_nm, small_nv = _adamw_small(blocked(small_w), blocked(small_g), blocked(small_m), blocked(small_v))
    res = {}
    for kind, arrays in (("grad", small_g), ("delta", small_d), ("m", small_nm), ("v", small_nv)):
        for n, a, like in zip(names, arrays, small_w):
            res[kind, n] = a.reshape(like.shape)
    for n in ("w_in", "glu_w", "w_out"):
        for pos, kind in enumerate(("grad", "delta", "m", "v")):
            res[kind, n] = shard_res[n][pos]

    order = ["norm_g", "w_in", "pool_w", "pool_scale", "a_re", "a_im", "log_dt", "b_re", "b_im", "c_re", "c_im",
             "d_skip", "glu_w", "glu_b", "w_out", "final_g"]
    outs = [loss, dx.reshape(nb, seq, D_MODEL)]
    for kind in ("grad", "delta", "m", "v"):
        outs += [res[kind, n] for n in order]
    return tuple(outs)
```

```python
import functools
import math

import jax
import jax.numpy as jnp
from jax import lax
from jax.experimental import pallas as pl
from jax.experimental.pallas import tpu as pltpu

F32 = jnp.float32
MXU_DTYPE = jnp.bfloat16

D_MODEL = 1024
MIX = 1024
POOL_W = 512
SSM_W = 512
N_POOL_G = 4
POOL_GC = 128
SSM_G = 32
SSM_C = 16
SSM_P = 64
DEPTH = 4
NORM_EPS = 1e-5
N_DEV = 8

ADAM_LR = 0.001
ADAM_B1 = 0.9
ADAM_B2 = 0.999
ADAM_EPS = 1e-08
ADAM_WD = 0.01
ADAM_STEP = 10

SUBLANES = 8
LANES = 128
HALO = 16
STATE_ROWS = 8
STATE_COLS = 256
T_BLK = 256
TM_FWD = 512
TM_BWD = 512
VMEM_LIMIT = 56 * 1024 * 1024

MESH = pl.DeviceIdType.MESH
VMEM_SPEC = pl.BlockSpec(memory_space=pltpu.VMEM)
ANY_SPEC = pl.BlockSpec(memory_space=pl.ANY)


def _mm(a, b):
    return jnp.dot(a, b, preferred_element_type=F32)


def _mm_tn(a, b):
    return lax.dot_general(a, b, (((0,), (0,)), ((), ())), preferred_element_type=F32)


def _mm_nt(a, b):
    return lax.dot_general(a, b, (((1,), (1,)), ((), ())), preferred_element_type=F32)


def _mx(a):
    return a.astype(MXU_DTYPE)


def _sigmoid(v):
    return 1.0 / (1.0 + jnp.exp(-v))


_GELU_C = math.sqrt(2.0 / math.pi)
_GELU_A = 0.044715


def _gelu_and_grad(y):
    th = jnp.tanh(_GELU_C * (y + _GELU_A * y * y * y))
    val = 0.5 * y * (1.0 + th)
    grad = 0.5 * (1.0 + th) + 0.5 * y * (1.0 - th * th) * (_GELU_C * (1.0 + 3.0 * _GELU_A * y * y))
    return val, grad


def _params(**kw):
    return pltpu.CompilerParams(vmem_limit_bytes=VMEM_LIMIT, **kw)


def _ssm_dense(a_re, a_im, log_dt, b_re, b_im, c_re, c_im):
    dt = jnp.exp(log_dt)[:, None]
    mag = jnp.exp(a_re * dt)
    ang = a_im * dt
    lb_re = mag * jnp.cos(ang)
    lb_im = mag * jnp.sin(ang)
    den = a_re * a_re + a_im * a_im
    n_re = lb_re - 1.0
    n_im = lb_im
    f_re = (n_re * a_re + n_im * a_im) / den
    f_im = (n_im * a_re - n_re * a_im) / den
    bb_re = f_re[..., None] * b_re - f_im[..., None] * b_im
    bb_im = f_re[..., None] * b_im + f_im[..., None] * b_re

    bb = jnp.stack([bb_re, bb_im], axis=0).reshape(2, 8, 4, SSM_P, SSM_C)
    rb = bb.transpose(1, 4, 0, 2, 3).reshape(8, SSM_C, 512)
    cc = jnp.stack([c_re, -c_im], axis=0).reshape(2, 8, 4, SSM_C, SSM_P)
    rc = cc.transpose(1, 3, 0, 2, 4).reshape(8, SSM_C, 512)
    return (lb_re.reshape(STATE_ROWS, STATE_COLS), lb_im.reshape(STATE_ROWS, STATE_COLS), rb, rc)


def _ssm_chunked(per_channel):
    row_group = jnp.arange(64) // SSM_C
    col_group = (jnp.arange(512) // SSM_P) % 4
    own_group = (row_group[:, None] == col_group[None, :]).astype(F32)
    even = (jnp.arange(8) % 2 == 0).astype(F32)[:, None, None]
    half = jnp.tile(per_channel, (1, 4, 1)) * own_group
    return jnp.concatenate([half * even, half * (1.0 - even)], axis=1)


def _inproj_fwd(x2, g_row, w_all, dep):
    n = x2.shape[0]
    tm = TM_FWD

    def body(x_ref, g_ref, w_ref, dep_ref, z_ref, h_ref):
        x = x_ref[...]
        r = lax.rsqrt(jnp.mean(x * x, axis=-1, keepdims=True) + NORM_EPS)
        h = _mx(x * r * g_ref[...])
        h_ref[...] = h
        for d in range(N_DEV):
            z_ref[:, d * 256:(d + 1) * 256] = _mm(h, w_ref[d])

    return pl.pallas_call(
        body, name="inproj_fwd",
        grid=(n // tm,),
        in_specs=[pl.BlockSpec((tm, D_MODEL), lambda i: (i, 0)),
                  pl.BlockSpec((1, D_MODEL), lambda i: (0, 0)),
                  pl.BlockSpec((N_DEV, D_MODEL, 256), lambda i: (0, 0, 0)),
                  ANY_SPEC],
        out_specs=[pl.BlockSpec((tm, 2 * MIX), lambda i: (i, 0)),
                   pl.BlockSpec((tm, D_MODEL), lambda i: (i, 0))],
        out_shape=[jax.ShapeDtypeStruct((n, 2 * MIX), F32),
                   jax.ShapeDtypeStruct((n, D_MODEL), MXU_DTYPE)],
        compiler_params=_params(dimension_semantics=("arbitrary",)),
    )(x2, g_row, w_all, dep)


def _outproj_fwd(x2, yg, w_out):
    n = x2.shape[0]
    tm = TM_FWD

    def body(x_ref, y_ref, w_ref, o_ref):
        o_ref[...] = x_ref[...] + _mm(y_ref[...], w_ref[...])

    return pl.pallas_call(
        body, name="outproj_fwd",
        grid=(n // tm,),
        in_specs=[pl.BlockSpec((tm, D_MODEL), lambda i: (i, 0)),
                  pl.BlockSpec((tm, MIX), lambda i: (i, 0)),
                  pl.BlockSpec((MIX, D_MODEL), lambda i: (0, 0))],
        out_specs=pl.BlockSpec((tm, D_MODEL), lambda i: (i, 0)),
        out_shape=jax.ShapeDtypeStruct((n, D_MODEL), F32),
        compiler_params=_params(dimension_semantics=("arbitrary",)),
    )(x2, yg, w_out)


def _loss_head(x2, tgt2, g_row):
    n = x2.shape[0]
    tm = TM_FWD

    def body(x_ref, t_ref, g_ref, dx_ref, loss_ref, dg_ref):
        @pl.when(pl.program_id(0) == 0)
        def _():
            loss_ref[...] = jnp.zeros_like(loss_ref)
            dg_ref[...] = jnp.zeros_like(dg_ref)

        x = x_ref[...]
        g = g_ref[...]
        r = lax.rsqrt(jnp.mean(x * x, axis=-1, keepdims=True) + NORM_EPS)
        xh = x * r
        e = xh * g - t_ref[...]
        loss_ref[...] += jnp.sum(jnp.sum(e * e, axis=-1, keepdims=True), axis=0, keepdims=True) * (0.5 / D_MODEL)
        dout = e * (1.0 / D_MODEL)
        dg_ref[...] += jnp.sum(dout * xh, axis=0, keepdims=True)
        gdy = dout * g
        dx_ref[...] = r * (gdy - xh * jnp.mean(xh * gdy, axis=-1, keepdims=True))

    return pl.pallas_call(
        body, name="loss_head",
        grid=(n // tm,),
        in_specs=[pl.BlockSpec((tm, D_MODEL), lambda i: (i, 0)),
                  pl.BlockSpec((tm, D_MODEL), lambda i: (i, 0)),
                  pl.BlockSpec((1, D_MODEL), lambda i: (0, 0))],
        out_specs=[pl.BlockSpec((tm, D_MODEL), lambda i: (i, 0)),
                   pl.BlockSpec((1, 1), lambda i: (0, 0)),
                   pl.BlockSpec((1, D_MODEL), lambda i: (0, 0))],
        out_shape=[jax.ShapeDtypeStruct((n, D_MODEL), F32),
                   jax.ShapeDtypeStruct((1, 1), F32),
                   jax.ShapeDtypeStruct((1, D_MODEL), F32)],
        compiler_params=_params(dimension_semantics=("arbitrary",)),
    )(x2, tgt2, g_row)


def _outproj_bwd(dx2, yg, w_out, dep):
    n = dx2.shape[0]
    tm = TM_BWD
    n_steps = n // tm

    def body(dx_ref, y_ref, w_ref, dep_ref, dy_ref, dw_ref, acc_ref):
        i = pl.program_id(0)

        @pl.when(i == 0)
        def _():
            acc_ref[...] = jnp.zeros_like(acc_ref)

        dxb = _mx(dx_ref[...])
        dy_ref[...] = _mm_nt(dxb, w_ref[...])
        acc_ref[...] += _mm_tn(y_ref[...], dxb)

        @pl.when(i == n_steps - 1)
        def _():
            dw_ref[...] = _mx(acc_ref[...])

    return pl.pallas_call(
        body, name="outproj_bwd",
        grid=(n_steps,),
        in_specs=[pl.BlockSpec((tm, D_MODEL), lambda i: (i, 0)),
                  pl.BlockSpec((tm, MIX), lambda i: (i, 0)),
                  pl.BlockSpec((MIX, D_MODEL), lambda i: (0, 0)),
                  ANY_SPEC],
        out_specs=[pl.BlockSpec((tm, MIX), lambda i: (i, 0)),
                   pl.BlockSpec((MIX, D_MODEL), lambda i: (0, 0))],
        out_shape=[jax.ShapeDtypeStruct((n, MIX), F32),
                   jax.ShapeDtypeStruct((MIX, D_MODEL), MXU_DTYPE)],
        scratch_shapes=[pltpu.VMEM((MIX, D_MODEL), F32)],
        compiler_params=_params(dimension_semantics=("arbitrary",)),
    )(dx2, yg, w_out, dep)


def _inproj_bwd(dz, h, x2, dx_in, g_row, w_all):
    n = x2.shape[0]
    tm = TM_BWD
    n_steps = n // tm

    def body(dz_ref, h_ref, x_ref, dxi_ref, g_ref, w_ref, dxo_ref, dw_ref, dg_ref, acc_ref, wcat_ref):
        i = pl.program_id(0)

        @pl.when(i == 0)
        def _():
            acc_ref[...] = jnp.zeros_like(acc_ref)
            dg_ref[...] = jnp.zeros_like(dg_ref)
            for d in range(N_DEV):
                wcat_ref[:, d * 256:(d + 1) * 256] = w_ref[d]

        hb = h_ref[...]
        for d in range(N_DEV):
            acc_ref[d] += _mm_tn(hb, dz_ref[:, d * 256:(d + 1) * 256])
        dh = _mm_nt(dz_ref[...], wcat_ref[...])
        x = x_ref[...]
        r = lax.rsqrt(jnp.mean(x * x, axis=-1, keepdims=True) + NORM_EPS)
        xh = x * r
        dg_ref[...] += jnp.sum(dh * xh, axis=0, keepdims=True)
        gdy = dh * g_ref[...]
        dxo_ref[...] = dxi_ref[...] + r * (gdy - xh * jnp.mean(xh * gdy, axis=-1, keepdims=True))

        @pl.when(i == n_steps - 1)
        def _():
            dw_ref[...] = _mx(acc_ref[...])

    return pl.pallas_call(
        body, name="inproj_bwd",
        grid=(n_steps,),
        in_specs=[pl.BlockSpec((tm, 2 * MIX), lambda i: (i, 0)),
                  pl.BlockSpec((tm, D_MODEL), lambda i: (i, 0)),
                  pl.BlockSpec((tm, D_MODEL), lambda i: (i, 0)),
                  pl.BlockSpec((tm, D_MODEL), lambda i: (i, 0)),
                  pl.BlockSpec((1, D_MODEL), lambda i: (0, 0)),
                  pl.BlockSpec((N_DEV, D_MODEL, 256), lambda i: (0, 0, 0))],
        out_specs=[pl.BlockSpec((tm, D_MODEL), lambda i: (i, 0)),
                   pl.BlockSpec((N_DEV, D_MODEL, 256), lambda i: (0, 0, 0)),
                   pl.BlockSpec((1, D_MODEL), lambda i: (0, 0))],
        out_shape=[jax.ShapeDtypeStruct((n, D_MODEL), F32),
                   jax.ShapeDtypeStruct((N_DEV, D_MODEL, 256), MXU_DTYPE),
                   jax.ShapeDtypeStruct((1, D_MODEL), F32)],
        scratch_shapes=[pltpu.VMEM((N_DEV, D_MODEL, 256), F32),
                        pltpu.VMEM((D_MODEL, 2 * MIX), MXU_DTYPE)],
        compiler_params=_params(dimension_semantics=("arbitrary",)),
    )(dz, h, x2, dx_in, g_row, w_all)


def _row_pos(t0, rows):
    return t0 + lax.broadcasted_iota(jnp.int32, (rows, LANES), 0)


def _pool_window_mean(upad, g, t0, t_blk):
    k = 2 << g
    w = upad
    sh = 1
    while sh < k:
        w = w + pltpu.roll(w, sh, 0)
        sh *= 2
    count = jnp.minimum(_row_pos(t0, t_blk) + 1, k).astype(F32)
    return w[HALO:] / count - upad[HALO:]


def _pool_window_bwd(qpad, g, t_blk):
    k = 2 << g
    n = t_blk + HALO
    w = qpad
    sh = 1
    while sh < k:
        w = w + pltpu.roll(w, n - sh, 0)
        sh *= 2
    return w[:t_blk]


class _StateBuf:
    def __init__(self, refs, t_blk):
        self.refs = refs
        self.t_blk = t_blk

    def put_chunk(self, b, j, val):
        for c in range(4):
            self.refs[4 * b + c][pl.ds(j, self.t_blk, stride=STATE_ROWS), :] = val[:, c * LANES:(c + 1) * LANES]

    def get_chunk(self, b, j):
        return jnp.concatenate(
            [self.refs[4 * b + c][pl.ds(j, self.t_blk, stride=STATE_ROWS), :] for c in range(4)], axis=-1)

    def load(self, b, r, part):
        return jnp.concatenate(
            [self.refs[4 * b + 2 * part + h][pl.ds(r, STATE_ROWS), :] for h in range(2)], axis=-1)

    def store(self, b, r, part, val):
        for h in range(2):
            self.refs[4 * b + 2 * part + h][pl.ds(r, STATE_ROWS), :] = val[:, h * LANES:(h + 1) * LANES]


def _state_scratch(nb, t_blk):
    return [pltpu.VMEM((t_blk * STATE_ROWS, LANES), F32) for _ in range(4 * nb)]


def _ssm_project_in(u_ssm, wb_ref, buf, b):
    ub = _mx(u_ssm)
    for j in range(STATE_ROWS):
        m = j // 2
        buf.put_chunk(b, j, _mm(ub[:, m * LANES:(m + 1) * LANES], wb_ref[j]))


def _scan_forward(buf, lbr, lbi, init, nb):
    def body(t, carry):
        r = pl.multiple_of(t * STATE_ROWS, STATE_ROWS)
        out = []
        for b in range(nb):
            sr, si = carry[2 * b], carry[2 * b + 1]
            nr = lbr * sr - lbi * si + buf.load(b, r, 0)
            ni = lbr * si + lbi * sr + buf.load(b, r, 1)
            buf.store(b, r, 0, nr)
            buf.store(b, r, 1, ni)
            out += [nr, ni]
        return tuple(out)

    return lax.fori_loop(0, buf.t_blk, body, init, unroll=4)


def _ssm_project_out(chunk, wc_ref):
    tiles = []
    for m in range(4):
        acc = None
        for j in (2 * m, 2 * m + 1):
            part = _mm_nt(chunk(j), wc_ref[j])
            acc = part if acc is None else acc + part
        tiles.append(acc)
    return jnp.concatenate(tiles, axis=-1)


def _mixer_fwd(z3, pool_w, pool_scale, lbr, lbi, wb, wc, d_skip, glu_w, glu_b):
    nb, seq, _ = z3.shape
    t_blk = min(T_BLK, seq)
    n_t = seq // t_blk
    halo_per_blk = t_blk // HALO

    def body(z_ref, zh_ref, pw_ref, ps_ref, lbr_ref, lbi_ref, wb_ref, wc_ref, dsk_ref, gw_ref, gb_ref,
             yg_ref, sc_ref, carry_ref, *s_refs):
        i = pl.program_id(0)
        t0 = i * t_blk
        buf = _StateBuf(s_refs, t_blk)

        @pl.when(i == 0)
        def _():
            carry_ref[...] = jnp.zeros_like(carry_ref)

        for b in range(nb):
            _ssm_project_in(z_ref[b, :, POOL_W:MIX], wb_ref, buf, b)
        init = tuple(carry_ref[b, :, h * STATE_COLS:(h + 1) * STATE_COLS] for b in range(nb) for h in range(2))
        fin = _scan_forward(buf, lbr_ref[...], lbi_ref[...], init, nb)
        for b in range(nb):
            carry_ref[b, :, 0:STATE_COLS] = fin[2 * b]
            carry_ref[b, :, STATE_COLS:2 * STATE_COLS] = fin[2 * b + 1]

        first = (i == 0)
        for b in range(nb):
            u_ssm = z_ref[b, :, POOL_W:MIX]

            def chunk(j, b=b):
                states = _mx(buf.get_chunk(b, j))
                sc_ref[b, j] = states
                return states

            y = _ssm_project_out(chunk, wc_ref) + dsk_ref[...] * u_ssm
            yg, _ = _gelu_and_grad(y)
            v = _mm(_mx(yg), gw_ref[...]) + gb_ref[...]
            o_ssm = yg * _sigmoid(v)
            gp = z_ref[b, :, MIX + POOL_W:2 * MIX]
            yg_ref[b, :, POOL_W:MIX] = _mx(o_ssm * (gp * _sigmoid(gp)))
            for g in range(N_POOL_G):
                cols = slice(g * POOL_GC, (g + 1) * POOL_GC)
                halo = jnp.where(first, 0.0, zh_ref[b, :, cols])
                upad = jnp.concatenate([halo, z_ref[b, :, cols]], axis=0)
                pooled = _pool_window_mean(upad, g, t0, t_blk)
                yp = _mm(_mx(pooled), pw_ref[g]) * ps_ref[:, cols]
                gpp = z_ref[b, :, MIX + g * POOL_GC:MIX + (g + 1) * POOL_GC]
                yg_ref[b, :, cols] = _mx(yp * (gpp * _sigmoid(gpp)))

    const = lambda *shape: pl.BlockSpec(shape, lambda i: (0,) * len(shape))
    return pl.pallas_call(
        body, name="mixer_fwd",
        grid=(n_t,),
        in_specs=[pl.BlockSpec((nb, t_blk, 2 * MIX), lambda i: (0, i, 0)),
                  pl.BlockSpec((nb, HALO, POOL_W), lambda i: (0, jnp.maximum(i * halo_per_blk - 1, 0), 0)),
                  const(N_POOL_G, POOL_GC, POOL_GC), const(1, POOL_W),
                  const(STATE_ROWS, STATE_COLS), const(STATE_ROWS, STATE_COLS),
                  const(STATE_ROWS, LANES, 2 * STATE_COLS), const(STATE_ROWS, LANES, 2 * STATE_COLS),
                  const(1, SSM_W), const(SSM_W, SSM_W), const(1, SSM_W)],
        out_specs=[pl.BlockSpec((nb, t_blk, MIX), lambda i: (0, i, 0)),
                   pl.BlockSpec((nb, STATE_ROWS, t_blk, 2 * STATE_COLS), lambda i: (0, 0, i, 0))],
        out_shape=[jax.ShapeDtypeStruct((nb, seq, MIX), MXU_DTYPE),
                   jax.ShapeDtypeStruct((nb, STATE_ROWS, seq, 2 * STATE_COLS), MXU_DTYPE)],
        scratch_shapes=[pltpu.VMEM((nb, STATE_ROWS, 2 * STATE_COLS), F32)] + _state_scratch(nb, t_blk),
        compiler_params=_params(dimension_semantics=("arbitrary",)),
    )(z3, z3, pool_w, pool_scale, lbr, lbi, wb, wc, d_skip, glu_w, glu_b)


def _mixer_bwd(z3, dy3, states, pool_w, pool_scale, lbr, lbi, wb, wc, d_skip, glu_w, glu_b):
    nb, seq, _ = z3.shape
    t_blk = min(T_BLK, seq)
    n_t = seq // t_blk
    halo_per_blk = t_blk // HALO

    def body(z_ref, zh_ref, dy_ref, sc_ref, sch_ref, pw_ref, ps_ref, lbr_ref, lbi_ref, wb_ref, wc_ref, dsk_ref,
             gw_ref, gb_ref,
             dz_ref, dpw_ref, dps_ref, dlbr_ref, dlbi_ref, dwb_ref, dwc_ref, ddsk_ref, dgw_ref, dgb_ref,
             gcarry_ref, qcarry_ref, du_ref, dgw_acc, *g_refs):
        i = pl.program_id(0)
        blk = n_t - 1 - i
        t0 = blk * t_blk
        gbuf = _StateBuf(g_refs, t_blk)

        @pl.when(i == 0)
        def _():
            gcarry_ref[...] = jnp.zeros_like(gcarry_ref)
            qcarry_ref[...] = jnp.zeros_like(qcarry_ref)
            for ref in (dpw_ref, dps_ref, dlbr_ref, dlbi_ref, dwb_ref, dwc_ref, ddsk_ref, dgw_acc, dgb_ref):
                ref[...] = jnp.zeros_like(ref)

        lbr_v = lbr_ref[...]
        lbi_v = lbi_ref[...]

        first = (blk == 0)
        for b in range(nb):
            u_ssm = z_ref[b, :, POOL_W:MIX]
            y = _ssm_project_out(lambda j, b=b: sc_ref[b, j], wc_ref) + dsk_ref[...] * u_ssm
            yg, dgelu = _gelu_and_grad(y)
            ygb = _mx(yg)
            sg = _sigmoid(_mm(ygb, gw_ref[...]) + gb_ref[...])
            o_ssm = yg * sg
            gp = z_ref[b, :, MIX + POOL_W:2 * MIX]
            sgm = _sigmoid(gp)
            dyv = dy_ref[b, :, POOL_W:MIX]
            dz_ref[b, :, MIX + POOL_W:2 * MIX] = _mx(dyv * o_ssm * (sgm * (1.0 + gp * (1.0 - sgm))))
            do = dyv * (gp * sgm)
            dv = do * yg * (sg * (1.0 - sg))
            dvb = _mx(dv)
            dgb_ref[...] += jnp.sum(dv, axis=0, keepdims=True)
            dgw_acc[...] += _mm_tn(ygb, dvb)
            dyp = (do * sg + _mm_nt(dvb, gw_ref[...])) * dgelu
            ddsk_ref[...] += jnp.sum(dyp * u_ssm, axis=0, keepdims=True)
            dypb = _mx(dyp)
            for j in range(STATE_ROWS):
                m = j // 2
                dyt = dypb[:, m * LANES:(m + 1) * LANES]
                gbuf.put_chunk(b, j, _mm(dyt, wc_ref[j]))
                dwc_ref[j] += _mm_tn(dyt, sc_ref[b, j])
            du_ref[b] = dsk_ref[...] * dyp

            for g in range(N_POOL_G):
                cols = slice(g * POOL_GC, (g + 1) * POOL_GC)
                halo = jnp.where(first, 0.0, zh_ref[b, :, cols])
                u_g = z_ref[b, :, cols]
                pooled = _pool_window_mean(jnp.concatenate([halo, u_g], axis=0), g, t0, t_blk)
                pb = _mx(pooled)
                ypre = _mm(pb, pw_ref[g])
                gpp = z_ref[b, :, MIX + g * POOL_GC:MIX + (g + 1) * POOL_GC]
                sgp = _sigmoid(gpp)
                dyg = dy_ref[b, :, cols]
                scale = ps_ref[:, cols]
                dz_ref[b, :, MIX + g * POOL_GC:MIX + (g + 1) * POOL_GC] = _mx(
                    dyg * (ypre * scale) * (sgp * (1.0 + gpp * (1.0 - sgp))))
                dyc = dyg * (gpp * sgp)
                dps_ref[:, cols] += jnp.sum(dyc * ypre, axis=0, keepdims=True)
                dypre = _mx(dyc * scale)
                dpw_ref[g] += _mm_tn(pb, dypre)
                dpooled = _mm_nt(dypre, pw_ref[g])
                count = jnp.minimum(_row_pos(t0, t_blk) + 1, 2 << g).astype(F32)
                q = dpooled / count
                qpad = jnp.concatenate([q, qcarry_ref[b, :, cols]], axis=0)
                qcarry_ref[b, :, cols] = q[:HALO]
                dz_ref[b, :, cols] = _mx(_pool_window_bwd(qpad, g, t_blk) - dpooled)

        def rev_body(k, carry):
            r = pl.multiple_of((t_blk - 1 - k) * STATE_ROWS, STATE_ROWS)
            out = []
            for b in range(nb):
                gr, gi = carry[2 * b], carry[2 * b + 1]
                ngr = lbr_v * gr + lbi_v * gi + gbuf.load(b, r, 0)
                ngi = lbr_v * gi - lbi_v * gr + gbuf.load(b, r, 1)
                gbuf.store(b, r, 0, ngr)
                gbuf.store(b, r, 1, ngi)
                out += [ngr, ngi]
            return tuple(out)

        init_g = tuple(gcarry_ref[b, :, h * STATE_COLS:(h + 1) * STATE_COLS] for b in range(nb) for h in range(2))
        fin = lax.fori_loop(0, t_blk, rev_body, init_g, unroll=4)
        for b in range(nb):
            gcarry_ref[b, :, 0:STATE_COLS] = fin[2 * b]
            gcarry_ref[b, :, STATE_COLS:2 * STATE_COLS] = fin[2 * b + 1]

        for b in range(nb):
            ub = _mx(z_ref[b, :, POOL_W:MIX])
            for m in range(4):
                acc = du_ref[b, :, m * LANES:(m + 1) * LANES]
                for j in (2 * m, 2 * m + 1):
                    g = gbuf.get_chunk(b, j)
                    gj = _mx(g)
                    acc = acc + _mm_nt(gj, wb_ref[j])
                    dwb_ref[j] += _mm_tn(ub[:, m * LANES:(m + 1) * LANES], gj)
                    before = jnp.where(first, 0.0, sch_ref[b, j].astype(F32))
                    spad = jnp.concatenate([before, sc_ref[b, j].astype(F32)], axis=0)
                    s_prev = pltpu.roll(spad, 1, 0)[HALO:]
                    g_re, g_im = g[:, :STATE_COLS], g[:, STATE_COLS:]
                    p_re, p_im = s_prev[:, :STATE_COLS], s_prev[:, STATE_COLS:]
                    dlbr_ref[j:j + 1, :] += jnp.sum(g_re * p_re + g_im * p_im, axis=0, keepdims=True)
                    dlbi_ref[j:j + 1, :] += jnp.sum(g_im * p_re - g_re * p_im, axis=0, keepdims=True)
                dz_ref[b, :, POOL_W + m * LANES:POOL_W + (m + 1) * LANES] = _mx(acc)

        @pl.when(i == n_t - 1)
        def _():
            dgw_ref[...] = _mx(dgw_acc[...])

    const = lambda *shape: pl.BlockSpec(shape, lambda i: (0,) * len(shape))
    rev = lambda i: n_t - 1 - i
    out_shape = [jax.ShapeDtypeStruct((nb, seq, 2 * MIX), MXU_DTYPE),
                 jax.ShapeDtypeStruct((N_POOL_G, POOL_GC, POOL_GC), F32),
                 jax.ShapeDtypeStruct((1, POOL_W), F32),
                 jax.ShapeDtypeStruct((STATE_ROWS, STATE_COLS), F32),
                 jax.ShapeDtypeStruct((STATE_ROWS, STATE_COLS), F32),
                 jax.ShapeDtypeStruct((STATE_ROWS, LANES, 2 * STATE_COLS), F32),
                 jax.ShapeDtypeStruct((STATE_ROWS, LANES, 2 * STATE_COLS), F32),
                 jax.ShapeDtypeStruct((1, SSM_W), F32),
                 jax.ShapeDtypeStruct((SSM_W, SSM_W), MXU_DTYPE),
                 jax.ShapeDtypeStruct((1, SSM_W), F32)]
    return pl.pallas_call(
        body, name="mixer_bwd",
        grid=(n_t,),
        in_specs=[pl.BlockSpec((nb, t_blk, 2 * MIX), lambda i: (0, rev(i), 0)),
                  pl.BlockSpec((nb, HALO, POOL_W), lambda i: (0, jnp.maximum(rev(i) * halo_per_blk - 1, 0), 0)),
                  pl.BlockSpec((nb, t_blk, MIX), lambda i: (0, rev(i), 0)),
                  pl.BlockSpec((nb, STATE_ROWS, t_blk, 2 * STATE_COLS), lambda i: (0, 0, rev(i), 0)),
                  pl.BlockSpec((nb, STATE_ROWS, HALO, 2 * STATE_COLS),
                               lambda i: (0, 0, jnp.maximum(rev(i) * halo_per_blk - 1, 0), 0)),
                  const(N_POOL_G, POOL_GC, POOL_GC), const(1, POOL_W),
                  const(STATE_ROWS, STATE_COLS), const(STATE_ROWS, STATE_COLS),
                  const(STATE_ROWS, LANES, 2 * STATE_COLS), const(STATE_ROWS, LANES, 2 * STATE_COLS),
                  const(1, SSM_W), const(SSM_W, SSM_W), const(1, SSM_W)],
        out_specs=[pl.BlockSpec((nb, t_blk, 2 * MIX), lambda i: (0, rev(i), 0))]
                  + [const(*s.shape) for s in out_shape[1:]],
        out_shape=out_shape,
        scratch_shapes=[pltpu.VMEM((nb, STATE_ROWS, 2 * STATE_COLS), F32),
                        pltpu.VMEM((nb, HALO, POOL_W), F32),
                        pltpu.VMEM((nb, t_blk, SSM_W), F32),
                        pltpu.VMEM((SSM_W, SSM_W), F32)]
                       + _state_scratch(nb, t_blk),
        compiler_params=_params(dimension_semantics=("arbitrary",)),
    )(z3, z3, dy3, states, states, pool_w, pool_scale, lbr, lbi, wb, wc, d_skip, glu_w, glu_b)


def _mesh_place():
    x, y, c = lax.axis_index("x"), lax.axis_index("y"), lax.axis_index("c")
    return x, y, c


def _flip(place, k):
    x, y, c = place
    return (1 - x if k & 4 else x, 1 - y if k & 2 else y, 1 - c if k & 1 else c)


def _index(place):
    x, y, c = place
    return 4 * x + 2 * y + c


HBM_SPEC = pl.BlockSpec(memory_space=pltpu.HBM)
SEM_SPEC = pl.BlockSpec(memory_space=pltpu.SEMAPHORE)
_EFFECT = pltpu.SideEffectType.DATAFLOW_SIDE_EFFECTING
N_PEERS = N_DEV - 1


def _exchange_copies(src_refs, land_refs, send_sems, recv_sems):
    me = _mesh_place()
    mine = _index(me)
    out = []
    for a, land_ref in enumerate(land_refs):
        for k in range(1, N_DEV):
            peer = _flip(me, k)
            theirs = _index(peer)
            n = a * N_PEERS + k - 1
            src = src_refs[a].at[theirs] if src_refs else land_ref.at[mine]
            send = pltpu.make_async_remote_copy(
                src_ref=src, dst_ref=land_ref.at[mine], send_sem=send_sems.at[n], recv_sem=recv_sems.at[n],
                device_id=peer, device_id_type=MESH)
            recv = pltpu.make_async_remote_copy(
                src_ref=src, dst_ref=land_ref.at[theirs], send_sem=send_sems.at[n], recv_sem=recv_sems.at[n],
                device_id=peer, device_id_type=MESH)
            out.append((send, recv))
    return out


def _exchange_start(srcs, lands, after, name):
    arrays = tuple(srcs) + tuple(lands)
    n_src, n_all = len(srcs), len(arrays)
    n_copies = len(lands) * N_PEERS

    def body(*refs):
        send_sems, recv_sems = refs[n_all + 1], refs[n_all + 2]
        token = refs[-1]
        for send, _ in _exchange_copies(refs[:n_src], refs[n_src:n_all], send_sems, recv_sems):
            send.start()
        token[...] = jnp.zeros_like(token)

    res = pl.pallas_call(
        body, name=name,
        in_specs=[HBM_SPEC] * n_all + [ANY_SPEC],
        out_specs=[SEM_SPEC, SEM_SPEC] + [HBM_SPEC] * n_all + [VMEM_SPEC],
        out_shape=[pltpu.SemaphoreType.DMA((n_copies,)), pltpu.SemaphoreType.DMA((n_copies,))]
                  + [pltpu.HBM(a.shape, a.dtype) for a in arrays] + [jax.ShapeDtypeStruct((SUBLANES, LANES), F32)],
        input_output_aliases={i: 2 + i for i in range(n_all)},
        compiler_params=pltpu.CompilerParams(has_side_effects=_EFFECT),
    )(*[pltpu.with_memory_space_constraint(a, pltpu.HBM) for a in arrays], after)
    return tuple(res[:-1]), res[-1]


def _exchange_wait(handle, n_lands, after, name):
    send_sems, recv_sems = handle[0], handle[1]
    arrays = handle[2:]
    n_all = len(arrays)
    n_src = n_all - n_lands

    def body(*refs):
        for send, recv in _exchange_copies(refs[:n_src], refs[n_src:n_all], refs[n_all], refs[n_all + 1]):
            send.wait_send()
            recv.wait_recv()

    res = pl.pallas_call(
        body, name=name,
        in_specs=[HBM_SPEC] * n_all + [SEM_SPEC, SEM_SPEC, ANY_SPEC],
        out_specs=[HBM_SPEC] * n_all,
        out_shape=[pltpu.HBM(a.shape, a.dtype) for a in arrays],
        input_output_aliases={i: i for i in range(n_all)},
        compiler_params=pltpu.CompilerParams(has_side_effects=_EFFECT),
    )(*arrays, send_sems, recv_sems, after)
    return tuple(res[:n_src]), tuple(res[n_src:])


def _weight_zones(w_in, glu_w, w_out, my_idx):
    shards = (w_in, glu_w, w_out)
    depth = w_in.shape[0]

    def body(idx_ref, *refs):
        ins, zones = refs[:len(shards)], refs[len(shards):]
        for l in range(depth):
            for a, src in enumerate(ins):
                zones[l * len(shards) + a][0] = _mx(src[l])

    whole = lambda s: pl.BlockSpec(s.shape, lambda i, idx: (0,) * s.ndim)
    return pl.pallas_call(
        body, name="weight_zones",
        grid_spec=pltpu.PrefetchScalarGridSpec(
            num_scalar_prefetch=1, grid=(1,),
            in_specs=[whole(s) for s in shards],
            out_specs=[pl.BlockSpec((1,) + s.shape[1:], lambda i, idx: (idx[0], 0, 0))
                       for _ in range(depth) for s in shards]),
        out_shape=[jax.ShapeDtypeStruct((N_DEV,) + s.shape[1:], MXU_DTYPE) for _ in range(depth) for s in shards],
        compiler_params=_params(dimension_semantics=("arbitrary",)),
    )(my_idx.reshape(1).astype(jnp.int32), *shards)


def _allreduce_packed(p):
    rows = p.shape[0]
    chunk = rows // N_DEV

    def body(p_ref, o_ref, recv_ref, send_sems, recv_sems):
        me = _mesh_place()
        mine = pl.multiple_of(_index(me) * chunk, SUBLANES)
        scatter = []
        for k in range(1, N_DEV):
            peer = _flip(me, k)
            cp = pltpu.make_async_remote_copy(
                src_ref=p_ref.at[pl.ds(pl.multiple_of(_index(peer) * chunk, SUBLANES), chunk)],
                dst_ref=recv_ref.at[k - 1],
                send_sem=send_sems.at[k - 1], recv_sem=recv_sems.at[k - 1],
                device_id=peer, device_id_type=MESH)
            cp.start()
            scatter.append(cp)
        total = p_ref[pl.ds(mine, chunk), :]
        for k in range(1, N_DEV):
            scatter[k - 1].wait()
            total = total + recv_ref[k - 1]
        o_ref[pl.ds(mine, chunk), :] = total
        gather = []
        for k in range(1, N_DEV):
            cp = pltpu.make_async_remote_copy(
                src_ref=o_ref.at[pl.ds(mine, chunk)],
                dst_ref=o_ref.at[pl.ds(mine, chunk)],
                send_sem=send_sems.at[6 + k], recv_sem=recv_sems.at[6 + k],
                device_id=_flip(me, k), device_id_type=MESH)
            cp.start()
            gather.append(cp)
        for k in range(1, N_DEV):
            theirs = pl.multiple_of(_index(_flip(me, k)) * chunk, SUBLANES)
            recv = pltpu.make_async_remote_copy(
                src_ref=o_ref.at[pl.ds(theirs, chunk)], dst_ref=o_ref.at[pl.ds(theirs, chunk)],
                send_sem=send_sems.at[6 + k], recv_sem=recv_sems.at[6 + k],
                device_id=_flip(me, k), device_id_type=MESH)
            recv.wait_recv()
        for cp in gather:
            cp.wait_send()

    return pl.pallas_call(
        body, name="comm_allreduce_packed",
        in_specs=[VMEM_SPEC],
        out_specs=VMEM_SPEC,
        out_shape=jax.ShapeDtypeStruct(p.shape, F32),
        scratch_shapes=[pltpu.VMEM((N_DEV - 1, chunk, LANES), F32),
                        pltpu.SemaphoreType.DMA((2 * (N_DEV - 1),)),
                        pltpu.SemaphoreType.DMA((2 * (N_DEV - 1),))],
        compiler_params=_params(),
    )(p)


def _adamw_math(w, g, m, v):
    m = ADAM_B1 * m + (1.0 - ADAM_B1) * g
    v = ADAM_B2 * v + (1.0 - ADAM_B2) * (g * g)
    m_hat = m / (1.0 - ADAM_B1 ** ADAM_STEP)
    v_hat = v / (1.0 - ADAM_B2 ** ADAM_STEP)
    delta = -ADAM_LR * (m_hat / (jnp.sqrt(v_hat) + ADAM_EPS) + ADAM_WD * w)
    return delta, m, v


def _adamw_summed(received, own, my_idx, w, m, v, name):
    depth, r, c = w.shape
    tr = min(r, 128)

    def body(idx_ref, *refs):
        r_refs, o_refs = refs[:depth], refs[depth:2 * depth]
        w_ref, m_ref, v_ref, g_ref, d_ref, nm_ref, nv_ref = refs[2 * depth:]
        me = idx_ref[0]
        for l in range(depth):
            g = jnp.zeros((tr, c), F32)
            for q in range(N_DEV):
                g = g + jnp.where(q == me, o_refs[l][0], r_refs[l][q]).astype(F32)
            g_ref[l] = g
            d_ref[l], nm_ref[l], nv_ref[l] = _adamw_math(w_ref[l], g, m_ref[l], v_ref[l])

    blk = pl.BlockSpec((depth, tr, c), lambda i, idx: (0, i, 0))
    return pl.pallas_call(
        body, name=name,
        grid_spec=pltpu.PrefetchScalarGridSpec(
            num_scalar_prefetch=1, grid=(r // tr,),
            in_specs=[pl.BlockSpec((N_DEV, tr, c), lambda i, idx: (0, i, 0))] * depth
                     + [pl.BlockSpec((1, tr, c), lambda i, idx: (idx[0], i, 0))] * depth
                     + [blk, blk, blk],
            out_specs=[blk] * 4),
        out_shape=[jax.ShapeDtypeStruct((depth, r, c), F32)] * 4,
        compiler_params=_params(dimension_semantics=("arbitrary",)),
    )(my_idx.reshape(1).astype(jnp.int32), *received, *own, w, m, v)


def _adamw_small(ws, gs, ms, vs):
    n = len(ws)
    depth = ws[0].shape[0]
    quarters = 4

    def spec(a):
        per_layer = a.shape[0] == depth
        split = a.ndim >= 3 and a.shape[1] % quarters == 0 and a.shape[1] >= quarters
        block = (1, a.shape[1] // quarters if split else a.shape[1]) + a.shape[2:]
        rest = (0,) * (a.ndim - 2)
        return pl.BlockSpec(block, lambda l, s: ((l if per_layer else 0), (s if split else 0)) + rest)

    def body(*refs):
        w_refs, g_refs, m_refs, v_refs = (refs[k * n:(k + 1) * n] for k in range(4))
        d_refs, nm_refs, nv_refs = (refs[(4 + k) * n:(5 + k) * n] for k in range(3))
        for k in range(n):
            d_refs[k][...], nm_refs[k][...], nv_refs[k][...] = _adamw_math(
                w_refs[k][...], g_refs[k][...], m_refs[k][...], v_refs[k][...])

    specs = [spec(a) for a in ws]
    shapes = [jax.ShapeDtypeStruct(a.shape, F32) for a in ws]
    res = pl.pallas_call(
        body, name="adamw_small",
        grid=(depth, quarters),
        in_specs=specs * 4,
        out_specs=specs * 3,
        out_shape=shapes * 3,
        compiler_params=_params(dimension_semantics=("arbitrary", "arbitrary")),
    )(*ws, *gs, *ms, *vs)
    return res[:n], res[n:2 * n], res[2 * n:]


_PACK_ROWS = SUBLANES * N_DEV


def _pack(arrays):
    flat = jnp.concatenate([a.reshape(-1) for a in arrays])
    per = _PACK_ROWS * LANES
    total = -(-flat.shape[0] // per) * per
    flat = jnp.pad(flat, (0, total - flat.shape[0]))
    return flat.reshape(total // LANES, LANES)


def _unpack(packed, like):
    flat = packed.reshape(-1)
    out = []
    off = 0
    for a in like:
        out.append(flat[off:off + a.size].reshape(a.shape))
        off += a.size
    return out


def kernel(x, norm_g, w_in, pool_w, pool_scale, a_re, a_im, log_dt, b_re, b_im, c_re, c_im, d_skip, glu_w, glu_b, w_out, final_g, loss_target, m_norm_g, m_w_in, m_pool_w, m_pool_scale, m_a_re, m_a_im, m_log_dt, m_b_re, m_b_im, m_c_re, m_c_im, m_d_skip, m_glu_w, m_glu_b, m_w_out, m_final_g, v_norm_g, v_w_in, v_pool_w, v_pool_scale, v_a_re, v_a_im, v_log_dt, v_b_re, v_b_im, v_c_re, v_c_im, v_d_skip, v_glu_w, v_glu_b, v_w_out, v_final_g):
    nb, seq, _ = x.shape
    n_tok = nb * seq
    depth = norm_g.shape[0]

    my_idx = _index(_mesh_place())

    zones = _weight_zones(w_in, glu_w, w_out, my_idx)

    def gather_start(l, after):
        return _exchange_start((), zones[3 * l:3 * l + 3], after, f"comm_gather_start_{l}")

    def gather_wait(handle, after, l):
        _, (win, glu, wout) = _exchange_wait(handle, 3, after, f"comm_gather_wait_{l}")
        return win, glu.reshape(SSM_W, SSM_W), wout.reshape(MIX, D_MODEL)

    xs = [x.reshape(n_tok, D_MODEL)]
    handle, dep = gather_start(0, xs[0])

    (lbr, lbi, rb, rc), dense_vjp = jax.vjp(jax.vmap(_ssm_dense), a_re, a_im, log_dt + dep[0, 0], b_re, b_im, c_re, c_im)
    chunk_all = jax.vmap(_ssm_chunked)
    (wb, wct), chunk_vjp = jax.vjp(lambda p, q: (chunk_all(p), chunk_all(q)), rb, rc)
    wb_m, wct_m = _mx(wb), _mx(wct)
    pool_w_m = _mx(pool_w)

    def layer_params(l):
        return (pool_w_m[l], pool_scale[l][None], lbr[l], lbi[l], wb_m[l], wct_m[l], d_skip[l][None],
                weights[l][1], glu_b[l][None])

    saved = []
    weights = []
    for l in range(depth):
        weights.append(gather_wait(handle, wct_m if l == 0 else xs[-1], l))
        if l + 1 < depth:
            handle, dep = gather_start(l + 1, weights[l][0])
        z, h = _inproj_fwd(xs[-1], norm_g[l][None], weights[l][0], dep)
        z3 = z.reshape(nb, seq, 2 * MIX)
        yg, states = _mixer_fwd(z3, *layer_params(l))
        yg2 = yg.reshape(n_tok, MIX)
        xs.append(_outproj_fwd(xs[-1], yg2, weights[l][2]))
        saved.append((z3, h, yg2, states))

    dx, loss_part, d_final_g = _loss_head(xs[-1], loss_target.reshape(n_tok, D_MODEL), final_g[None])
    loss = lax.psum(loss_part[0, 0], ("x", "y", "c"))

    small = {k: [None] * depth for k in
             ("norm_g", "pool_w", "pool_scale", "lbr", "lbi", "wb", "wct", "d_skip", "glu_b")}
    received = [None] * depth
    sent = [None] * depth
    pending = None
    for l in reversed(range(depth)):
        z3, h, yg2, states = saved[l]
        dy, d_wout = _outproj_bwd(dx, yg2, weights[l][2], dep)
        (dz, d_pw, d_ps, d_lbr, d_lbi, d_wb, d_wct, d_dsk, d_gw, d_gb) = _mixer_bwd(
            z3, dy.reshape(nb, seq, MIX), states, *layer_params(l))
        dx, d_win, d_ng = _inproj_bwd(dz.reshape(n_tok, 2 * MIX), h, xs[l], dx, norm_g[l][None], weights[l][0])
        for k, val in (("norm_g", d_ng[0]), ("pool_w", d_pw), ("pool_scale", d_ps[0]), ("lbr", d_lbr),
                       ("lbi", d_lbi), ("wb", d_wb), ("wct", d_wct), ("d_skip", d_dsk[0]), ("glu_b", d_gb[0])):
            small[k][l] = val
        if pending is not None:
            sent[l + 1], received[l + 1] = _exchange_wait(pending, 3, dx, f"comm_grads_wait_{l + 1}")
        sent[l] = (d_win, d_gw.reshape(N_DEV, SSM_W // N_DEV, SSM_W), d_wout.reshape(N_DEV, MIX // N_DEV, D_MODEL))
        lands = tuple(lax.empty(s.shape, s.dtype) for s in sent[l])
        pending, dep = _exchange_start(sent[l], lands, dx, f"comm_grads_start_{l}")
    sent[0], received[0] = _exchange_wait(pending, 3, dx, "comm_grads_wait_0")

    shard_res = {}
    for pos, (n, w, m, v) in enumerate((("w_in", w_in, m_w_in, v_w_in), ("glu_w", glu_w, m_glu_w, v_glu_w),
                                        ("w_out", w_out, m_w_out, v_w_out))):
        shard_res[n] = _adamw_summed([received[l][pos] for l in range(depth)], [sent[l][pos] for l in range(depth)],
                                     my_idx, w, m, v, "adamw_" + n)

    stack = lambda k: jnp.stack(small[k])
    d_rb, d_rc = chunk_vjp((stack("wb"), stack("wct")))
    local = [stack("norm_g"), stack("pool_w"), stack("pool_scale"), stack("lbr"), stack("lbi"), d_rb, d_rc,
             stack("d_skip"), stack("glu_b"), d_final_g[0]]
    (g_norm_g, g_pool_w, g_pool_scale, g_lbr, g_lbi, g_rb, g_rc, g_d_skip, g_glu_b, g_final_g) = _unpack(
        _allreduce_packed(_pack(local)), local)
    g_a_re, g_a_im, g_log_dt, g_b_re, g_b_im, g_c_re, g_c_im = dense_vjp((g_lbr, g_lbi, g_rb, g_rc))

    names = ["norm_g", "pool_w", "pool_scale", "a_re", "a_im", "log_dt", "b_re", "b_im", "c_re", "c_im",
             "d_skip", "glu_b", "final_g"]
    rows = {"norm_g", "pool_scale", "log_dt", "d_skip", "glu_b"}
    small_w = [norm_g, pool_w, pool_scale, a_re, a_im, log_dt, b_re, b_im, c_re, c_im, d_skip, glu_b, final_g]
    small_g = [g_norm_g, g_pool_w, g_pool_scale, g_a_re, g_a_im, g_log_dt, g_b_re, g_b_im, g_c_re, g_c_im,
               g_d_skip, g_glu_b, g_final_g]
    small_m = [m_norm_g, m_pool_w, m_pool_scale, m_a_re, m_a_im, m_log_dt, m_b_re, m_b_im, m_c_re, m_c_im,
               m_d_skip, m_glu_b, m_final_g]
    small_v = [v_norm_g, v_pool_w, v_pool_scale, v_a_re, v_a_im, v_log_dt, v_b_re, v_b_im, v_c_re, v_c_im,
               v_d_skip, v_glu_b, v_final_g]

    wide_last = {"b_re", "b_im"}

    def blocked(arrays):
        return [a.reshape(1, 1, -1) if n == "final_g" else a[:, None, :] if n in rows
                else a.swapaxes(2, 3) if n in wide_last else a for n, a in zip(names, arrays)]

    small_d, small_nm, small_nv = _adamw_small(blocked(small_w), blocked(small_g), blocked(small_m), blocked(small_v))
    res = {}
    for kind, arrays in (("grad", small_g), ("delta", small_d), ("m", small_nm), ("v", small_nv)):
        for n, a, like in zip(names, arrays, small_w):
            if kind != "grad" and n in wide_last:
                a = a.swapaxes(2, 3)
            res[kind, n] = a.reshape(like.shape)
    for n in ("w_in", "glu_w", "w_out"):
        for pos, kind in enumerate(("grad", "delta", "m", "v")):
            res[kind, n] = shard_res[n][pos]

    order = ["norm_g", "w_in", "pool_w", "pool_scale", "a_re", "a_im", "log_dt", "b_re", "b_im", "c_re", "c_im",
             "d_skip", "glu_w", "glu_b", "w_out", "final_g"]
    outs = [loss, dx.reshape(nb, seq, D_MODEL)]
    for kind in ("grad", "delta", "m", "v"):
        outs += [res[kind, n] for n in order]
    return tuple(outs)
```

```python
import functools
import math

import jax
import jax.numpy as jnp
from jax import lax
from jax.experimental import pallas as pl
from jax.experimental.pallas import tpu as pltpu

F32 = jnp.float32
MXU_DTYPE = jnp.bfloat16

D_MODEL = 1024
MIX = 1024
POOL_W = 512
SSM_W = 512
N_POOL_G = 4
POOL_GC = 128
SSM_G = 32
SSM_C = 16
SSM_P = 64
DEPTH = 4
NORM_EPS = 1e-5
N_DEV = 8

ADAM_LR = 0.001
ADAM_B1 = 0.9
ADAM_B2 = 0.999
ADAM_EPS = 1e-08
ADAM_WD = 0.01
ADAM_STEP = 10

SUBLANES = 8
LANES = 128
HALO = 16
STATE_ROWS = 8
STATE_COLS = 256
T_BLK = 256
TM_FWD = 512
TM_BWD = 512
VMEM_LIMIT = 56 * 1024 * 1024

MESH = pl.DeviceIdType.MESH
VMEM_SPEC = pl.BlockSpec(memory_space=pltpu.VMEM)
ANY_SPEC = pl.BlockSpec(memory_space=pl.ANY)


def _mm(a, b):
    return jnp.dot(a, b, preferred_element_type=F32)


def _mm_tn(a, b):
    return lax.dot_general(a, b, (((0,), (0,)), ((), ())), preferred_element_type=F32)


def _mm_nt(a, b):
    return lax.dot_general(a, b, (((1,), (1,)), ((), ())), preferred_element_type=F32)


def _mx(a):
    return a.astype(MXU_DTYPE)


def _sigmoid(v):
    return 1.0 / (1.0 + jnp.exp(-v))


_GELU_C = math.sqrt(2.0 / math.pi)
_GELU_A = 0.044715


def _gelu_and_grad(y):
    th = jnp.tanh(_GELU_C * (y + _GELU_A * y * y * y))
    val = 0.5 * y * (1.0 + th)
    grad = 0.5 * (1.0 + th) + 0.5 * y * (1.0 - th * th) * (_GELU_C * (1.0 + 3.0 * _GELU_A * y * y))
    return val, grad


def _params(**kw):
    return pltpu.CompilerParams(vmem_limit_bytes=VMEM_LIMIT, **kw)


def _ssm_dense(a_re, a_im, log_dt, b_re, b_im, c_re, c_im):
    dt = jnp.exp(log_dt)[:, None]
    mag = jnp.exp(a_re * dt)
    ang = a_im * dt
    lb_re = mag * jnp.cos(ang)
    lb_im = mag * jnp.sin(ang)
    den = a_re * a_re + a_im * a_im
    n_re = lb_re - 1.0
    n_im = lb_im
    f_re = (n_re * a_re + n_im * a_im) / den
    f_im = (n_im * a_re - n_re * a_im) / den
    bb_re = f_re[..., None] * b_re - f_im[..., None] * b_im
    bb_im = f_re[..., None] * b_im + f_im[..., None] * b_re

    bb = jnp.stack([bb_re, bb_im], axis=0).reshape(2, 8, 4, SSM_P, SSM_C)
    rb = bb.transpose(1, 4, 0, 2, 3).reshape(8, SSM_C, 512)
    cc = jnp.stack([c_re, -c_im], axis=0).reshape(2, 8, 4, SSM_C, SSM_P)
    rc = cc.transpose(1, 3, 0, 2, 4).reshape(8, SSM_C, 512)
    return (lb_re.reshape(STATE_ROWS, STATE_COLS), lb_im.reshape(STATE_ROWS, STATE_COLS), rb, rc)


def _ssm_chunked(per_channel):
    row_group = jnp.arange(64) // SSM_C
    col_group = (jnp.arange(512) // SSM_P) % 4
    own_group = (row_group[:, None] == col_group[None, :]).astype(F32)
    even = (jnp.arange(8) % 2 == 0).astype(F32)[:, None, None]
    half = jnp.tile(per_channel, (1, 4, 1)) * own_group
    return jnp.concatenate([half * even, half * (1.0 - even)], axis=1)


def _inproj_fwd(x2, g_row, w_all, dep):
    n = x2.shape[0]
    tm = TM_FWD

    def body(x_ref, g_ref, w_ref, dep_ref, z_ref, h_ref):
        x = x_ref[...]
        r = lax.rsqrt(jnp.mean(x * x, axis=-1, keepdims=True) + NORM_EPS)
        h = _mx(x * r * g_ref[...])
        h_ref[...] = h
        for d in range(N_DEV):
            z_ref[:, d * 256:(d + 1) * 256] = _mm(h, w_ref[d])

    return pl.pallas_call(
        body, name="inproj_fwd",
        grid=(n // tm,),
        in_specs=[pl.BlockSpec((tm, D_MODEL), lambda i: (i, 0)),
                  pl.BlockSpec((1, D_MODEL), lambda i: (0, 0)),
                  pl.BlockSpec((N_DEV, D_MODEL, 256), lambda i: (0, 0, 0)),
                  ANY_SPEC],
        out_specs=[pl.BlockSpec((tm, 2 * MIX), lambda i: (i, 0)),
                   pl.BlockSpec((tm, D_MODEL), lambda i: (i, 0))],
        out_shape=[jax.ShapeDtypeStruct((n, 2 * MIX), F32),
                   jax.ShapeDtypeStruct((n, D_MODEL), MXU_DTYPE)],
        compiler_params=_params(dimension_semantics=("arbitrary",)),
    )(x2, g_row, w_all, dep)


def _outproj_fwd(x2, yg, w_out):
    n = x2.shape[0]
    tm = TM_FWD

    def body(x_ref, y_ref, w_ref, o_ref):
        o_ref[...] = x_ref[...] + _mm(y_ref[...], w_ref[...])

    return pl.pallas_call(
        body, name="outproj_fwd",
        grid=(n // tm,),
        in_specs=[pl.BlockSpec((tm, D_MODEL), lambda i: (i, 0)),
                  pl.BlockSpec((tm, MIX), lambda i: (i, 0)),
                  pl.BlockSpec((MIX, D_MODEL), lambda i: (0, 0))],
        out_specs=pl.BlockSpec((tm, D_MODEL), lambda i: (i, 0)),
        out_shape=jax.ShapeDtypeStruct((n, D_MODEL), F32),
        compiler_params=_params(dimension_semantics=("arbitrary",)),
    )(x2, yg, w_out)


def _loss_head(x2, tgt2, g_row):
    n = x2.shape[0]
    tm = TM_FWD

    def body(x_ref, t_ref, g_ref, dx_ref, loss_ref, dg_ref):
        @pl.when(pl.program_id(0) == 0)
        def _():
            loss_ref[...] = jnp.zeros_like(loss_ref)
            dg_ref[...] = jnp.zeros_like(dg_ref)

        x = x_ref[...]
        g = g_ref[...]
        r = lax.rsqrt(jnp.mean(x * x, axis=-1, keepdims=True) + NORM_EPS)
        xh = x * r
        e = xh * g - t_ref[...]
        loss_ref[...] += jnp.sum(jnp.sum(e * e, axis=-1, keepdims=True), axis=0, keepdims=True) * (0.5 / D_MODEL)
        dout = e * (1.0 / D_MODEL)
        dg_ref[...] += jnp.sum(dout * xh, axis=0, keepdims=True)
        gdy = dout * g
        dx_ref[...] = r * (gdy - xh * jnp.mean(xh * gdy, axis=-1, keepdims=True))

    return pl.pallas_call(
        body, name="loss_head",
        grid=(n // tm,),
        in_specs=[pl.BlockSpec((tm, D_MODEL), lambda i: (i, 0)),
                  pl.BlockSpec((tm, D_MODEL), lambda i: (i, 0)),
                  pl.BlockSpec((1, D_MODEL), lambda i: (0, 0))],
        out_specs=[pl.BlockSpec((tm, D_MODEL), lambda i: (i, 0)),
                   pl.BlockSpec((1, 1), lambda i: (0, 0)),
                   pl.BlockSpec((1, D_MODEL), lambda i: (0, 0))],
        out_shape=[jax.ShapeDtypeStruct((n, D_MODEL), F32),
                   jax.ShapeDtypeStruct((1, 1), F32),
                   jax.ShapeDtypeStruct((1, D_MODEL), F32)],
        compiler_params=_params(dimension_semantics=("arbitrary",)),
    )(x2, tgt2, g_row)


def _outproj_bwd(dx2, yg, w_out, dep):
    n = dx2.shape[0]
    tm = TM_BWD
    n_steps = n // tm

    def body(dx_ref, y_ref, w_ref, dep_ref, dy_ref, dw_ref, acc_ref):
        i = pl.program_id(0)

        @pl.when(i == 0)
        def _():
            acc_ref[...] = jnp.zeros_like(acc_ref)

        dxb = _mx(dx_ref[...])
        dy_ref[...] = _mm_nt(dxb, w_ref[...])
        acc_ref[...] += _mm_tn(y_ref[...], dxb)

        @pl.when(i == n_steps - 1)
        def _():
            dw_ref[...] = _mx(acc_ref[...])

    return pl.pallas_call(
        body, name="outproj_bwd",
        grid=(n_steps,),
        in_specs=[pl.BlockSpec((tm, D_MODEL), lambda i: (i, 0)),
                  pl.BlockSpec((tm, MIX), lambda i: (i, 0)),
                  pl.BlockSpec((MIX, D_MODEL), lambda i: (0, 0)),
                  ANY_SPEC],
        out_specs=[pl.BlockSpec((tm, MIX), lambda i: (i, 0)),
                   pl.BlockSpec((MIX, D_MODEL), lambda i: (0, 0))],
        out_shape=[jax.ShapeDtypeStruct((n, MIX), F32),
                   jax.ShapeDtypeStruct((MIX, D_MODEL), MXU_DTYPE)],
        scratch_shapes=[pltpu.VMEM((MIX, D_MODEL), F32)],
        compiler_params=_params(dimension_semantics=("arbitrary",)),
    )(dx2, yg, w_out, dep)


def _inproj_bwd(dz, h, x2, dx_in, g_row, w_all):
    n = x2.shape[0]
    tm = TM_BWD
    n_steps = n // tm

    def body(dz_ref, h_ref, x_ref, dxi_ref, g_ref, w_ref, dxo_ref, dw_ref, dg_ref, acc_ref, wcat_ref):
        i = pl.program_id(0)

        @pl.when(i == 0)
        def _():
            acc_ref[...] = jnp.zeros_like(acc_ref)
            dg_ref[...] = jnp.zeros_like(dg_ref)
            for d in range(N_DEV):
                wcat_ref[:, d * 256:(d + 1) * 256] = w_ref[d]

        hb = h_ref[...]
        for d in range(N_DEV):
            acc_ref[d] += _mm_tn(hb, dz_ref[:, d * 256:(d + 1) * 256])
        dh = _mm_nt(dz_ref[...], wcat_ref[...])
        x = x_ref[...]
        r = lax.rsqrt(jnp.mean(x * x, axis=-1, keepdims=True) + NORM_EPS)
        xh = x * r
        dg_ref[...] += jnp.sum(dh * xh, axis=0, keepdims=True)
        gdy = dh * g_ref[...]
        dxo_ref[...] = dxi_ref[...] + r * (gdy - xh * jnp.mean(xh * gdy, axis=-1, keepdims=True))

        @pl.when(i == n_steps - 1)
        def _():
            dw_ref[...] = _mx(acc_ref[...])

    return pl.pallas_call(
        body, name="inproj_bwd",
        grid=(n_steps,),
        in_specs=[pl.BlockSpec((tm, 2 * MIX), lambda i: (i, 0)),
                  pl.BlockSpec((tm, D_MODEL), lambda i: (i, 0)),
                  pl.BlockSpec((tm, D_MODEL), lambda i: (i, 0)),
                  pl.BlockSpec((tm, D_MODEL), lambda i: (i, 0)),
                  pl.BlockSpec((1, D_MODEL), lambda i: (0, 0)),
                  pl.BlockSpec((N_DEV, D_MODEL, 256), lambda i: (0, 0, 0))],
        out_specs=[pl.BlockSpec((tm, D_MODEL), lambda i: (i, 0)),
                   pl.BlockSpec((N_DEV, D_MODEL, 256), lambda i: (0, 0, 0)),
                   pl.BlockSpec((1, D_MODEL), lambda i: (0, 0))],
        out_shape=[jax.ShapeDtypeStruct((n, D_MODEL), F32),
                   jax.ShapeDtypeStruct((N_DEV, D_MODEL, 256), MXU_DTYPE),
                   jax.ShapeDtypeStruct((1, D_MODEL), F32)],
        scratch_shapes=[pltpu.VMEM((N_DEV, D_MODEL, 256), F32),
                        pltpu.VMEM((D_MODEL, 2 * MIX), MXU_DTYPE)],
        compiler_params=_params(dimension_semantics=("arbitrary",)),
    )(dz, h, x2, dx_in, g_row, w_all)


def _row_pos(t0, rows):
    return t0 + lax.broadcasted_iota(jnp.int32, (rows, LANES), 0)


def _pool_window_mean(upad, g, t0, t_blk):
    k = 2 << g
    w = upad
    sh = 1
    while sh < k:
        w = w + pltpu.roll(w, sh, 0)
        sh *= 2
    count = jnp.minimum(_row_pos(t0, t_blk) + 1, k).astype(F32)
    return w[HALO:] / count - upad[HALO:]


def _pool_window_bwd(qpad, g, t_blk):
    k = 2 << g
    n = t_blk + HALO
    w = qpad
    sh = 1
    while sh < k:
        w = w + pltpu.roll(w, n - sh, 0)
        sh *= 2
    return w[:t_blk]


class _StateBuf:
    def __init__(self, refs, t_blk):
        self.refs = refs
        self.t_blk = t_blk

    def put_chunk(self, b, j, val):
        for c in range(4):
            self.refs[4 * b + c][pl.ds(j, self.t_blk, stride=STATE_ROWS), :] = val[:, c * LANES:(c + 1) * LANES]

    def get_chunk(self, b, j):
        return jnp.concatenate(
            [self.refs[4 * b + c][pl.ds(j, self.t_blk, stride=STATE_ROWS), :] for c in range(4)], axis=-1)

    def load(self, b, r, part):
        return jnp.concatenate(
            [self.refs[4 * b + 2 * part + h][pl.ds(r, STATE_ROWS), :] for h in range(2)], axis=-1)

    def store(self, b, r, part, val):
        for h in range(2):
            self.refs[4 * b + 2 * part + h][pl.ds(r, STATE_ROWS), :] = val[:, h * LANES:(h + 1) * LANES]


def _state_scratch(nb, t_blk):
    return [pltpu.VMEM((t_blk * STATE_ROWS, LANES), F32) for _ in range(4 * nb)]


def _ssm_project_in(u_ssm, wb_ref, buf, b):
    ub = _mx(u_ssm)
    for j in range(STATE_ROWS):
        m = j // 2
        buf.put_chunk(b, j, _mm(ub[:, m * LANES:(m + 1) * LANES], wb_ref[j]))


def _scan_forward(buf, lbr, lbi, init, nb):
    def body(t, carry):
        r = pl.multiple_of(t * STATE_ROWS, STATE_ROWS)
        out = []
        for b in range(nb):
            sr, si = carry[2 * b], carry[2 * b + 1]
            nr = lbr * sr - lbi * si + buf.load(b, r, 0)
            ni = lbr * si + lbi * sr + buf.load(b, r, 1)
            buf.store(b, r, 0, nr)
            buf.store(b, r, 1, ni)
            out += [nr, ni]
        return tuple(out)

    return lax.fori_loop(0, buf.t_blk, body, init, unroll=4)


def _ssm_project_out(chunk, wc_ref):
    tiles = []
    for m in range(4):
        acc = None
        for j in (2 * m, 2 * m + 1):
            part = _mm_nt(chunk(j), wc_ref[j])
            acc = part if acc is None else acc + part
        tiles.append(acc)
    return jnp.concatenate(tiles, axis=-1)


def _mixer_fwd(z3, pool_w, pool_scale, lbr, lbi, wb, wc, d_skip, glu_w, glu_b):
    nb, seq, _ = z3.shape
    t_blk = min(T_BLK, seq)
    n_t = seq // t_blk
    halo_per_blk = t_blk // HALO

    def body(z_ref, zh_ref, pw_ref, ps_ref, lbr_ref, lbi_ref, wb_ref, wc_ref, dsk_ref, gw_ref, gb_ref,
             yg_ref, sc_ref, carry_ref, *s_refs):
        i = pl.program_id(0)
        t0 = i * t_blk
        buf = _StateBuf(s_refs, t_blk)

        @pl.when(i == 0)
        def _():
            carry_ref[...] = jnp.zeros_like(carry_ref)

        for b in range(nb):
            _ssm_project_in(z_ref[b, :, POOL_W:MIX], wb_ref, buf, b)
        init = tuple(carry_ref[b, :, h * STATE_COLS:(h + 1) * STATE_COLS] for b in range(nb) for h in range(2))
        fin = _scan_forward(buf, lbr_ref[...], lbi_ref[...], init, nb)
        for b in range(nb):
            carry_ref[b, :, 0:STATE_COLS] = fin[2 * b]
            carry_ref[b, :, STATE_COLS:2 * STATE_COLS] = fin[2 * b + 1]

        first = (i == 0)
        for b in range(nb):
            u_ssm = z_ref[b, :, POOL_W:MIX]

            def chunk(j, b=b):
                states = _mx(buf.get_chunk(b, j))
                sc_ref[b, j] = states
                return states

            y = _ssm_project_out(chunk, wc_ref) + dsk_ref[...] * u_ssm
            yg, _ = _gelu_and_grad(y)
            v = _mm(_mx(yg), gw_ref[...]) + gb_ref[...]
            o_ssm = yg * _sigmoid(v)
            gp = z_ref[b, :, MIX + POOL_W:2 * MIX]
            yg_ref[b, :, POOL_W:MIX] = _mx(o_ssm * (gp * _sigmoid(gp)))
            for g in range(N_POOL_G):
                cols = slice(g * POOL_GC, (g + 1) * POOL_GC)
                halo = jnp.where(first, 0.0, zh_ref[b, :, cols])
                upad = jnp.concatenate([halo, z_ref[b, :, cols]], axis=0)
                pooled = _pool_window_mean(upad, g, t0, t_blk)
                yp = _mm(_mx(pooled), pw_ref[g]) * ps_ref[:, cols]
                gpp = z_ref[b, :, MIX + g * POOL_GC:MIX + (g + 1) * POOL_GC]
                yg_ref[b, :, cols] = _mx(yp * (gpp * _sigmoid(gpp)))

    const = lambda *shape: pl.BlockSpec(shape, lambda i: (0,) * len(shape))
    return pl.pallas_call(
        body, name="mixer_fwd",
        grid=(n_t,),
        in_specs=[pl.BlockSpec((nb, t_blk, 2 * MIX), lambda i: (0, i, 0)),
                  pl.BlockSpec((nb, HALO, POOL_W), lambda i: (0, jnp.maximum(i * halo_per_blk - 1, 0), 0)),
                  const(N_POOL_G, POOL_GC, POOL_GC), const(1, POOL_W),
                  const(STATE_ROWS, STATE_COLS), const(STATE_ROWS, STATE_COLS),
                  const(STATE_ROWS, LANES, 2 * STATE_COLS), const(STATE_ROWS, LANES, 2 * STATE_COLS),
                  const(1, SSM_W), const(SSM_W, SSM_W), const(1, SSM_W)],
        out_specs=[pl.BlockSpec((nb, t_blk, MIX), lambda i: (0, i, 0)),
                   pl.BlockSpec((nb, STATE_ROWS, t_blk, 2 * STATE_COLS), lambda i: (0, 0, i, 0))],
        out_shape=[jax.ShapeDtypeStruct((nb, seq, MIX), MXU_DTYPE),
                   jax.ShapeDtypeStruct((nb, STATE_ROWS, seq, 2 * STATE_COLS), MXU_DTYPE)],
        scratch_shapes=[pltpu.VMEM((nb, STATE_ROWS, 2 * STATE_COLS), F32)] + _state_scratch(nb, t_blk),
        compiler_params=_params(dimension_semantics=("arbitrary",)),
    )(z3, z3, pool_w, pool_scale, lbr, lbi, wb, wc, d_skip, glu_w, glu_b)


def _mixer_bwd(z3, dy3, states, pool_w, pool_scale, lbr, lbi, wb, wc, d_skip, glu_w, glu_b):
    nb, seq, _ = z3.shape
    t_blk = min(T_BLK, seq)
    n_t = seq // t_blk
    halo_per_blk = t_blk // HALO

    def body(z_ref, zh_ref, dy_ref, sc_ref, sch_ref, pw_ref, ps_ref, lbr_ref, lbi_ref, wb_ref, wc_ref, dsk_ref,
             gw_ref, gb_ref,
             dz_ref, dpw_ref, dps_ref, dlbr_ref, dlbi_ref, dwb_ref, dwc_ref, ddsk_ref, dgw_ref, dgb_ref,
             gcarry_ref, qcarry_ref, du_ref, dgw_acc, *g_refs):
        i = pl.program_id(0)
        blk = n_t - 1 - i
        t0 = blk * t_blk
        gbuf = _StateBuf(g_refs, t_blk)

        @pl.when(i == 0)
        def _():
            gcarry_ref[...] = jnp.zeros_like(gcarry_ref)
            qcarry_ref[...] = jnp.zeros_like(qcarry_ref)
            for ref in (dpw_ref, dps_ref, dlbr_ref, dlbi_ref, dwb_ref, dwc_ref, ddsk_ref, dgw_acc, dgb_ref):
                ref[...] = jnp.zeros_like(ref)

        lbr_v = lbr_ref[...]
        lbi_v = lbi_ref[...]

        first = (blk == 0)
        for b in range(nb):
            u_ssm = z_ref[b, :, POOL_W:MIX]
            y = _ssm_project_out(lambda j, b=b: sc_ref[b, j], wc_ref) + dsk_ref[...] * u_ssm
            yg, dgelu = _gelu_and_grad(y)
            ygb = _mx(yg)
            sg = _sigmoid(_mm(ygb, gw_ref[...]) + gb_ref[...])
            o_ssm = yg * sg
            gp = z_ref[b, :, MIX + POOL_W:2 * MIX]
            sgm = _sigmoid(gp)
            dyv = dy_ref[b, :, POOL_W:MIX]
            dz_ref[b, :, MIX + POOL_W:2 * MIX] = _mx(dyv * o_ssm * (sgm * (1.0 + gp * (1.0 - sgm))))
            do = dyv * (gp * sgm)
            dv = do * yg * (sg * (1.0 - sg))
            dvb = _mx(dv)
            dgb_ref[...] += jnp.sum(dv, axis=0, keepdims=True)
            dgw_acc[...] += _mm_tn(ygb, dvb)
            dyp = (do * sg + _mm_nt(dvb, gw_ref[...])) * dgelu
            ddsk_ref[...] += jnp.sum(dyp * u_ssm, axis=0, keepdims=True)
            dypb = _mx(dyp)
            for j in range(STATE_ROWS):
                m = j // 2
                dyt = dypb[:, m * LANES:(m + 1) * LANES]
                gbuf.put_chunk(b, j, _mm(dyt, wc_ref[j]))
                dwc_ref[j] += _mm_tn(dyt, sc_ref[b, j])
            du_ref[b] = dsk_ref[...] * dyp

            for g in range(N_POOL_G):
                cols = slice(g * POOL_GC, (g + 1) * POOL_GC)
                halo = jnp.where(first, 0.0, zh_ref[b, :, cols])
                u_g = z_ref[b, :, cols]
                pooled = _pool_window_mean(jnp.concatenate([halo, u_g], axis=0), g, t0, t_blk)
                pb = _mx(pooled)
                ypre = _mm(pb, pw_ref[g])
                gpp = z_ref[b, :, MIX + g * POOL_GC:MIX + (g + 1) * POOL_GC]
                sgp = _sigmoid(gpp)
                dyg = dy_ref[b, :, cols]
                scale = ps_ref[:, cols]
                dz_ref[b, :, MIX + g * POOL_GC:MIX + (g + 1) * POOL_GC] = _mx(
                    dyg * (ypre * scale) * (sgp * (1.0 + gpp * (1.0 - sgp))))
                dyc = dyg * (gpp * sgp)
                dps_ref[:, cols] += jnp.sum(dyc * ypre, axis=0, keepdims=True)
                dypre = _mx(dyc * scale)
                dpw_ref[g] += _mm_tn(pb, dypre)
                dpooled = _mm_nt(dypre, pw_ref[g])
                count = jnp.minimum(_row_pos(t0, t_blk) + 1, 2 << g).astype(F32)
                q = dpooled / count
                qpad = jnp.concatenate([q, qcarry_ref[b, :, cols]], axis=0)
                qcarry_ref[b, :, cols] = q[:HALO]
                dz_ref[b, :, cols] = _mx(_pool_window_bwd(qpad, g, t_blk) - dpooled)

        def rev_body(k, carry):
            r = pl.multiple_of((t_blk - 1 - k) * STATE_ROWS, STATE_ROWS)
            out = []
            for b in range(nb):
                gr, gi = carry[2 * b], carry[2 * b + 1]
                ngr = lbr_v * gr + lbi_v * gi + gbuf.load(b, r, 0)
                ngi = lbr_v * gi - lbi_v * gr + gbuf.load(b, r, 1)
                gbuf.store(b, r, 0, ngr)
                gbuf.store(b, r, 1, ngi)
                out += [ngr, ngi]
            return tuple(out)

        init_g = tuple(gcarry_ref[b, :, h * STATE_COLS:(h + 1) * STATE_COLS] for b in range(nb) for h in range(2))
        fin = lax.fori_loop(0, t_blk, rev_body, init_g, unroll=4)
        for b in range(nb):
            gcarry_ref[b, :, 0:STATE_COLS] = fin[2 * b]
            gcarry_ref[b, :, STATE_COLS:2 * STATE_COLS] = fin[2 * b + 1]

        for b in range(nb):
            ub = _mx(z_ref[b, :, POOL_W:MIX])
            for m in range(4):
                acc = du_ref[b, :, m * LANES:(m + 1) * LANES]
                for j in (2 * m, 2 * m + 1):
                    g = gbuf.get_chunk(b, j)
                    gj = _mx(g)
                    acc = acc + _mm_nt(gj, wb_ref[j])
                    dwb_ref[j] += _mm_tn(ub[:, m * LANES:(m + 1) * LANES], gj)
                    before = jnp.where(first, 0.0, sch_ref[b, j].astype(F32))
                    spad = jnp.concatenate([before, sc_ref[b, j].astype(F32)], axis=0)
                    s_prev = pltpu.roll(spad, 1, 0)[HALO:]
                    g_re, g_im = g[:, :STATE_COLS], g[:, STATE_COLS:]
                    p_re, p_im = s_prev[:, :STATE_COLS], s_prev[:, STATE_COLS:]
                    dlbr_ref[j:j + 1, :] += jnp.sum(g_re * p_re + g_im * p_im, axis=0, keepdims=True)
                    dlbi_ref[j:j + 1, :] += jnp.sum(g_im * p_re - g_re * p_im, axis=0, keepdims=True)
                dz_ref[b, :, POOL_W + m * LANES:POOL_W + (m + 1) * LANES] = _mx(acc)

        @pl.when(i == n_t - 1)
        def _():
            dgw_ref[...] = _mx(dgw_acc[...])

    const = lambda *shape: pl.BlockSpec(shape, lambda i: (0,) * len(shape))
    rev = lambda i: n_t - 1 - i
    out_shape = [jax.ShapeDtypeStruct((nb, seq, 2 * MIX), MXU_DTYPE),
                 jax.ShapeDtypeStruct((N_POOL_G, POOL_GC, POOL_GC), F32),
                 jax.ShapeDtypeStruct((1, POOL_W), F32),
                 jax.ShapeDtypeStruct((STATE_ROWS, STATE_COLS), F32),
                 jax.ShapeDtypeStruct((STATE_ROWS, STATE_COLS), F32),
                 jax.ShapeDtypeStruct((STATE_ROWS, LANES, 2 * STATE_COLS), F32),
                 jax.ShapeDtypeStruct((STATE_ROWS, LANES, 2 * STATE_COLS), F32),
                 jax.ShapeDtypeStruct((1, SSM_W), F32),
                 jax.ShapeDtypeStruct((SSM_W, SSM_W), MXU_DTYPE),
                 jax.ShapeDtypeStruct((1, SSM_W), F32)]
    return pl.pallas_call(
        body, name="mixer_bwd",
        grid=(n_t,),
        in_specs=[pl.BlockSpec((nb, t_blk, 2 * MIX), lambda i: (0, rev(i), 0)),
                  pl.BlockSpec((nb, HALO, POOL_W), lambda i: (0, jnp.maximum(rev(i) * halo_per_blk - 1, 0), 0)),
                  pl.BlockSpec((nb, t_blk, MIX), lambda i: (0, rev(i), 0)),
                  pl.BlockSpec((nb, STATE_ROWS, t_blk, 2 * STATE_COLS), lambda i: (0, 0, rev(i), 0)),
                  pl.BlockSpec((nb, STATE_ROWS, HALO, 2 * STATE_COLS),
                               lambda i: (0, 0, jnp.maximum(rev(i) * halo_per_blk - 1, 0), 0)),
                  const(N_POOL_G, POOL_GC, POOL_GC), const(1, POOL_W),
                  const(STATE_ROWS, STATE_COLS), const(STATE_ROWS, STATE_COLS),
                  const(STATE_ROWS, LANES, 2 * STATE_COLS), const(STATE_ROWS, LANES, 2 * STATE_COLS),
                  const(1, SSM_W), const(SSM_W, SSM_W), const(1, SSM_W)],
        out_specs=[pl.BlockSpec((nb, t_blk, 2 * MIX), lambda i: (0, rev(i), 0))]
                  + [const(*s.shape) for s in out_shape[1:]],
        out_shape=out_shape,
        scratch_shapes=[pltpu.VMEM((nb, STATE_ROWS, 2 * STATE_COLS), F32),
                        pltpu.VMEM((nb, HALO, POOL_W), F32),
                        pltpu.VMEM((nb, t_blk, SSM_W), F32),
                        pltpu.VMEM((SSM_W, SSM_W), F32)]
                       + _state_scratch(nb, t_blk),
        compiler_params=_params(dimension_semantics=("arbitrary",)),
    )(z3, z3, dy3, states, states, pool_w, pool_scale, lbr, lbi, wb, wc, d_skip, glu_w, glu_b)


def _mesh_place():
    x, y, c = lax.axis_index("x"), lax.axis_index("y"), lax.axis_index("c")
    return x, y, c


def _flip(place, k):
    x, y, c = place
    return (1 - x if k & 4 else x, 1 - y if k & 2 else y, 1 - c if k & 1 else c)


def _index(place):
    x, y, c = place
    return 4 * x + 2 * y + c


HBM_SPEC = pl.BlockSpec(memory_space=pltpu.HBM)
SEM_SPEC = pl.BlockSpec(memory_space=pltpu.SEMAPHORE)
_EFFECT = pltpu.SideEffectType.DATAFLOW_SIDE_EFFECTING
N_PEERS = N_DEV - 1


def _exchange_copies(src_refs, land_refs, send_sems, recv_sems):
    me = _mesh_place()
    mine = _index(me)
    out = []
    for a, land_ref in enumerate(land_refs):
        for k in range(1, N_DEV):
            peer = _flip(me, k)
            theirs = _index(peer)
            n = a * N_PEERS + k - 1
            src = src_refs[a].at[theirs] if src_refs else land_ref.at[mine]
            send = pltpu.make_async_remote_copy(
                src_ref=src, dst_ref=land_ref.at[mine], send_sem=send_sems.at[n], recv_sem=recv_sems.at[n],
                device_id=peer, device_id_type=MESH)
            recv = pltpu.make_async_remote_copy(
                src_ref=src, dst_ref=land_ref.at[theirs], send_sem=send_sems.at[n], recv_sem=recv_sems.at[n],
                device_id=peer, device_id_type=MESH)
            out.append((send, recv))
    return out


def _exchange_start(srcs, lands, after, name):
    arrays = tuple(srcs) + tuple(lands)
    n_src, n_all = len(srcs), len(arrays)
    n_copies = len(lands) * N_PEERS

    def body(*refs):
        send_sems, recv_sems = refs[n_all + 1], refs[n_all + 2]
        token = refs[-1]
        for send, _ in _exchange_copies(refs[:n_src], refs[n_src:n_all], send_sems, recv_sems):
            send.start()
        token[...] = jnp.zeros_like(token)

    res = pl.pallas_call(
        body, name=name,
        in_specs=[HBM_SPEC] * n_all + [ANY_SPEC],
        out_specs=[SEM_SPEC, SEM_SPEC] + [HBM_SPEC] * n_all + [VMEM_SPEC],
        out_shape=[pltpu.SemaphoreType.DMA((n_copies,)), pltpu.SemaphoreType.DMA((n_copies,))]
                  + [pltpu.HBM(a.shape, a.dtype) for a in arrays] + [jax.ShapeDtypeStruct((SUBLANES, LANES), F32)],
        input_output_aliases={i: 2 + i for i in range(n_all)},
        compiler_params=pltpu.CompilerParams(has_side_effects=_EFFECT),
    )(*[pltpu.with_memory_space_constraint(a, pltpu.HBM) for a in arrays], after)
    return tuple(res[:-1]), res[-1]


def _exchange_wait(handle, n_lands, after, name):
    send_sems, recv_sems = handle[0], handle[1]
    arrays = handle[2:]
    n_all = len(arrays)
    n_src = n_all - n_lands

    def body(*refs):
        for send, recv in _exchange_copies(refs[:n_src], refs[n_src:n_all], refs[n_all], refs[n_all + 1]):
            send.wait_send()
            recv.wait_recv()

    res = pl.pallas_call(
        body, name=name,
        in_specs=[HBM_SPEC] * n_all + [SEM_SPEC, SEM_SPEC, ANY_SPEC],
        out_specs=[HBM_SPEC] * n_all,
        out_shape=[pltpu.HBM(a.shape, a.dtype) for a in arrays],
        input_output_aliases={i: i for i in range(n_all)},
        compiler_params=pltpu.CompilerParams(has_side_effects=_EFFECT),
    )(*arrays, send_sems, recv_sems, after)
    return tuple(res[:n_src]), tuple(res[n_src:])


def _weight_zones(w_in, glu_w, w_out, my_idx):
    shards = (w_in, glu_w, w_out)
    depth = w_in.shape[0]

    def body(idx_ref, *refs):
        ins, zones = refs[:len(shards)], refs[len(shards):]
        for l in range(depth):
            for a, src in enumerate(ins):
                zones[l * len(shards) + a][0] = _mx(src[l])

    whole = lambda s: pl.BlockSpec(s.shape, lambda i, idx: (0,) * s.ndim)
    return pl.pallas_call(
        body, name="weight_zones",
        grid_spec=pltpu.PrefetchScalarGridSpec(
            num_scalar_prefetch=1, grid=(1,),
            in_specs=[whole(s) for s in shards],
            out_specs=[pl.BlockSpec((1,) + s.shape[1:], lambda i, idx: (idx[0], 0, 0))
                       for _ in range(depth) for s in shards]),
        out_shape=[jax.ShapeDtypeStruct((N_DEV,) + s.shape[1:], MXU_DTYPE) for _ in range(depth) for s in shards],
        compiler_params=_params(dimension_semantics=("arbitrary",)),
    )(my_idx.reshape(1).astype(jnp.int32), *shards)


def _allreduce_packed(p):
    rows = p.shape[0]
    half = rows // 2
    quarter = half // 4

    def body(p_ref, o_ref, part_ref, sib_ref, got_ref, send_sems, recv_sems):
        x, y, c = _mesh_place()
        sibling = (x, y, 1 - c)
        chip = 2 * x + y
        chips = [(k, (1 - x if k & 2 else x, 1 - y if k & 1 else y, c), chip ^ k) for k in (1, 2, 3)]
        my_half = pl.multiple_of(c * half, SUBLANES)
        other_half = pl.multiple_of((1 - c) * half, SUBLANES)

        def copy(n, src, dst, to):
            return pltpu.make_async_remote_copy(src_ref=src, dst_ref=dst, send_sem=send_sems.at[n],
                                                recv_sem=recv_sems.at[n], device_id=to, device_id_type=MESH)

        def quarter_of(ref, base, q):
            return ref.at[pl.ds(pl.multiple_of(base + q * quarter, SUBLANES), quarter)]

        swap = copy(0, p_ref.at[pl.ds(other_half, half)], sib_ref, sibling)
        swap.start()
        swap.wait()
        part_ref[...] = p_ref[pl.ds(my_half, half), :] + sib_ref[...]

        scatter = [copy(k, quarter_of(part_ref, 0, q), got_ref.at[k - 1], to) for k, to, q in chips]
        for cp in scatter:
            cp.start()
        total = part_ref[pl.ds(pl.multiple_of(chip * quarter, SUBLANES), quarter), :]
        for cp, (k, _, _) in zip(scatter, chips):
            cp.wait()
            total = total + got_ref[k - 1]
        mine = pl.multiple_of(my_half + chip * quarter, SUBLANES)
        o_ref[pl.ds(mine, quarter), :] = total

        gather = [copy(3 + k, o_ref.at[pl.ds(mine, quarter)], o_ref.at[pl.ds(mine, quarter)], to) for k, to, _ in chips]
        for cp in gather:
            cp.start()
        for k, to, q in chips:
            theirs = quarter_of(o_ref, my_half, q)
            copy(3 + k, theirs, theirs, to).wait_recv()
        for cp in gather:
            cp.wait_send()

        back = copy(7, o_ref.at[pl.ds(my_half, half)], o_ref.at[pl.ds(my_half, half)], sibling)
        back.start()
        copy(7, o_ref.at[pl.ds(other_half, half)], o_ref.at[pl.ds(other_half, half)], sibling).wait_recv()
        back.wait_send()

    return pl.pallas_call(
        body, name="comm_allreduce_packed",
        in_specs=[VMEM_SPEC],
        out_specs=VMEM_SPEC,
        out_shape=jax.ShapeDtypeStruct(p.shape, F32),
        scratch_shapes=[pltpu.VMEM((half, LANES), F32),
                        pltpu.VMEM((half, LANES), F32),
                        pltpu.VMEM((3, quarter, LANES), F32),
                        pltpu.SemaphoreType.DMA((8,)),
                        pltpu.SemaphoreType.DMA((8,))],
        compiler_params=_params(),
    )(p)


def _adamw_math(w, g, m, v):
    m = ADAM_B1 * m + (1.0 - ADAM_B1) * g
    v = ADAM_B2 * v + (1.0 - ADAM_B2) * (g * g)
    m_hat = m / (1.0 - ADAM_B1 ** ADAM_STEP)
    v_hat = v / (1.0 - ADAM_B2 ** ADAM_STEP)
    delta = -ADAM_LR * (m_hat / (jnp.sqrt(v_hat) + ADAM_EPS) + ADAM_WD * w)
    return delta, m, v


def _adamw_summed(received, own, my_idx, w, m, v, name):
    depth, r, c = w.shape
    tr = min(r, 128)

    def body(idx_ref, *refs):
        r_refs, o_refs = refs[:depth], refs[depth:2 * depth]
        w_ref, m_ref, v_ref, g_ref, d_ref, nm_ref, nv_ref = refs[2 * depth:]
        me = idx_ref[0]
        for l in range(depth):
            g = jnp.zeros((tr, c), F32)
            for q in range(N_DEV):
                g = g + jnp.where(q == me, o_refs[l][0], r_refs[l][q]).astype(F32)
            g_ref[l] = g
            d_ref[l], nm_ref[l], nv_ref[l] = _adamw_math(w_ref[l], g, m_ref[l], v_ref[l])

    blk = pl.BlockSpec((depth, tr, c), lambda i, idx: (0, i, 0))
    return pl.pallas_call(
        body, name=name,
        grid_spec=pltpu.PrefetchScalarGridSpec(
            num_scalar_prefetch=1, grid=(r // tr,),
            in_specs=[pl.BlockSpec((N_DEV, tr, c), lambda i, idx: (0, i, 0))] * depth
                     + [pl.BlockSpec((1, tr, c), lambda i, idx: (idx[0], i, 0))] * depth
                     + [blk, blk, blk],
            out_specs=[blk] * 4),
        out_shape=[jax.ShapeDtypeStruct((depth, r, c), F32)] * 4,
        compiler_params=_params(dimension_semantics=("arbitrary",)),
    )(my_idx.reshape(1).astype(jnp.int32), *received, *own, w, m, v)


def _adamw_small(ws, gs, ms, vs):
    n = len(ws)
    depth = ws[0].shape[0]
    quarters = 4

    def spec(a):
        per_layer = a.shape[0] == depth
        split = a.ndim >= 3 and a.shape[1] % quarters == 0 and a.shape[1] >= quarters
        block = (1, a.shape[1] // quarters if split else a.shape[1]) + a.shape[2:]
        rest = (0,) * (a.ndim - 2)
        return pl.BlockSpec(block, lambda l, s: ((l if per_layer else 0), (s if split else 0)) + rest)

    def body(*refs):
        w_refs, g_refs, m_refs, v_refs = (refs[k * n:(k + 1) * n] for k in range(4))
        d_refs, nm_refs, nv_refs = (refs[(4 + k) * n:(5 + k) * n] for k in range(3))
        for k in range(n):
            d_refs[k][...], nm_refs[k][...], nv_refs[k][...] = _adamw_math(
                w_refs[k][...], g_refs[k][...], m_refs[k][...], v_refs[k][...])

    specs = [spec(a) for a in ws]
    shapes = [jax.ShapeDtypeStruct(a.shape, F32) for a in ws]
    res = pl.pallas_call(
        body, name="adamw_small",
        grid=(depth, quarters),
        in_specs=specs * 4,
        out_specs=specs * 3,
        out_shape=shapes * 3,
        compiler_params=_params(dimension_semantics=("arbitrary", "arbitrary")),
    )(*ws, *gs, *ms, *vs)
    return res[:n], res[n:2 * n], res[2 * n:]


_PACK_ROWS = SUBLANES * N_DEV


def _pack(arrays):
    flat = jnp.concatenate([a.reshape(-1) for a in arrays])
    per = _PACK_ROWS * LANES
    total = -(-flat.shape[0] // per) * per
    flat = jnp.pad(flat, (0, total - flat.shape[0]))
    return flat.reshape(total // LANES, LANES)


def _unpack(packed, like):
    flat = packed.reshape(-1)
    out = []
    off = 0
    for a in like:
        out.append(flat[off:off + a.size].reshape(a.shape))
        off += a.size
    return out


def kernel(x, norm_g, w_in, pool_w, pool_scale, a_re, a_im, log_dt, b_re, b_im, c_re, c_im, d_skip, glu_w, glu_b, w_out, final_g, loss_target, m_norm_g, m_w_in, m_pool_w, m_pool_scale, m_a_re, m_a_im, m_log_dt, m_b_re, m_b_im, m_c_re, m_c_im, m_d_skip, m_glu_w, m_glu_b, m_w_out, m_final_g, v_norm_g, v_w_in, v_pool_w, v_pool_scale, v_a_re, v_a_im, v_log_dt, v_b_re, v_b_im, v_c_re, v_c_im, v_d_skip, v_glu_w, v_glu_b, v_w_out, v_final_g):
    nb, seq, _ = x.shape
    n_tok = nb * seq
    depth = norm_g.shape[0]

    my_idx = _index(_mesh_place())

    zones = _weight_zones(w_in, glu_w, w_out, my_idx)

    def gather_start(l, after):
        return _exchange_start((), zones[3 * l:3 * l + 3], after, f"comm_gather_start_{l}")

    def gather_wait(handle, after, l):
        _, (win, glu, wout) = _exchange_wait(handle, 3, after, f"comm_gather_wait_{l}")
        return win, glu.reshape(SSM_W, SSM_W), wout.reshape(MIX, D_MODEL)

    xs = [x.reshape(n_tok, D_MODEL)]
    handle, dep = gather_start(0, xs[0])

    (lbr, lbi, rb, rc), dense_vjp = jax.vjp(jax.vmap(_ssm_dense), a_re, a_im, log_dt + dep[0, 0], b_re, b_im, c_re, c_im)
    chunk_all = jax.vmap(_ssm_chunked)
    (wb, wct), chunk_vjp = jax.vjp(lambda p, q: (chunk_all(p), chunk_all(q)), rb, rc)
    wb_m, wct_m = _mx(wb), _mx(wct)
    pool_w_m = _mx(pool_w)

    def layer_params(l):
        return (pool_w_m[l], pool_scale[l][None], lbr[l], lbi[l], wb_m[l], wct_m[l], d_skip[l][None],
                weights[l][1], glu_b[l][None])

    saved = []
    weights = []
    for l in range(depth):
        weights.append(gather_wait(handle, wct_m if l == 0 else xs[-1], l))
        if l + 1 < depth:
            handle, dep = gather_start(l + 1, weights[l][0])
        z, h = _inproj_fwd(xs[-1], norm_g[l][None], weights[l][0], dep)
        z3 = z.reshape(nb, seq, 2 * MIX)
        yg, states = _mixer_fwd(z3, *layer_params(l))
        yg2 = yg.reshape(n_tok, MIX)
        xs.append(_outproj_fwd(xs[-1], yg2, weights[l][2]))
        saved.append((z3, h, yg2, states))

    dx, loss_part, d_final_g = _loss_head(xs[-1], loss_target.reshape(n_tok, D_MODEL), final_g[None])
    loss = lax.psum(loss_part[0, 0], ("x", "y", "c"))

    small = {k: [None] * depth for k in
             ("norm_g", "pool_w", "pool_scale", "lbr", "lbi", "wb", "wct", "d_skip", "glu_b")}
    received = [None] * depth
    sent = [None] * depth
    pending = None
    for l in reversed(range(depth)):
        z3, h, yg2, states = saved[l]
        dy, d_wout = _outproj_bwd(dx, yg2, weights[l][2], dep)
        (dz, d_pw, d_ps, d_lbr, d_lbi, d_wb, d_wct, d_dsk, d_gw, d_gb) = _mixer_bwd(
            z3, dy.reshape(nb, seq, MIX), states, *layer_params(l))
        dx, d_win, d_ng = _inproj_bwd(dz.reshape(n_tok, 2 * MIX), h, xs[l], dx, norm_g[l][None], weights[l][0])
        for k, val in (("norm_g", d_ng[0]), ("pool_w", d_pw), ("pool_scale", d_ps[0]), ("lbr", d_lbr),
                       ("lbi", d_lbi), ("wb", d_wb), ("wct", d_wct), ("d_skip", d_dsk[0]), ("glu_b", d_gb[0])):
            small[k][l] = val
        if pending is not None:
            sent[l + 1], received[l + 1] = _exchange_wait(pending, 3, dx, f"comm_grads_wait_{l + 1}")
        sent[l] = (d_win, d_gw.reshape(N_DEV, SSM_W // N_DEV, SSM_W), d_wout.reshape(N_DEV, MIX // N_DEV, D_MODEL))
        lands = tuple(lax.empty(s.shape, s.dtype) for s in sent[l])
        pending, dep = _exchange_start(sent[l], lands, dx, f"comm_grads_start_{l}")
    stack = lambda k: jnp.stack(small[k])
    d_rb, d_rc = chunk_vjp((stack("wb"), stack("wct")))
    local = [stack("norm_g"), stack("pool_w"), stack("pool_scale"), stack("lbr"), stack("lbi"), d_rb, d_rc,
             stack("d_skip"), stack("glu_b"), d_final_g[0] + dep[0, 0]]
    (g_norm_g, g_pool_w, g_pool_scale, g_lbr, g_lbi, g_rb, g_rc, g_d_skip, g_glu_b, g_final_g) = _unpack(
        _allreduce_packed(_pack(local)), local)
    g_a_re, g_a_im, g_log_dt, g_b_re, g_b_im, g_c_re, g_c_im = dense_vjp((g_lbr, g_lbi, g_rb, g_rc))

    names = ["norm_g", "pool_w", "pool_scale", "a_re", "a_im", "log_dt", "b_re", "b_im", "c_re", "c_im",
             "d_skip", "glu_b", "final_g"]
    rows = {"norm_g", "pool_scale", "log_dt", "d_skip", "glu_b"}
    small_w = [norm_g, pool_w, pool_scale, a_re, a_im, log_dt, b_re, b_im, c_re, c_im, d_skip, glu_b, final_g]
    small_g = [g_norm_g, g_pool_w, g_pool_scale, g_a_re, g_a_im, g_log_dt, g_b_re, g_b_im, g_c_re, g_c_im,
               g_d_skip, g_glu_b, g_final_g]
    small_m = [m_norm_g, m_pool_w, m_pool_scale, m_a_re, m_a_im, m_log_dt, m_b_re, m_b_im, m_c_re, m_c_im,
               m_d_skip, m_glu_b, m_final_g]
    small_v = [v_norm_g, v_pool_w, v_pool_scale, v_a_re, v_a_im, v_log_dt, v_b_re, v_b_im, v_c_re, v_c_im,
               v_d_skip, v_glu_b, v_final_g]

    wide_last = {"b_re", "b_im"}

    def blocked(arrays):
        return [a.reshape(1, 1, -1) if n == "final_g" else a[:, None, :] if n in rows
                else a.swapaxes(2, 3) if n in wide_last else a for n, a in zip(names, arrays)]

    small_d, small_nm, small_nv = _adamw_small(blocked(small_w), blocked(small_g), blocked(small_m), blocked(small_v))
    res = {}
    for kind, arrays in (("grad", small_g), ("delta", small_d), ("m", small_nm), ("v", small_nv)):
        for n, a, like in zip(names, arrays, small_w):
            if kind != "grad" and n in wide_last:
                a = a.swapaxes(2, 3)
            res[kind, n] = a.reshape(like.shape)

    sent[0], received[0] = _exchange_wait(pending, 3, small_d[0], "comm_grads_wait_0")
    shard_res = {}
    for pos, (n, w, m, v) in enumerate((("w_in", w_in, m_w_in, v_w_in), ("glu_w", glu_w, m_glu_w, v_glu_w),
                                        ("w_out", w_out, m_w_out, v_w_out))):
        shard_res[n] = _adamw_summed([received[l][pos] for l in range(depth)], [sent[l][pos] for l in range(depth)],
                                     my_idx, w, m, v, "adamw_" + n)
    for n in ("w_in", "glu_w", "w_out"):
        for pos, kind in enumerate(("grad", "delta", "m", "v")):
            res[kind, n] = shard_res[n][pos]

    order = ["norm_g", "w_in", "pool_w", "pool_scale", "a_re", "a_im", "log_dt", "b_re", "b_im", "c_re", "c_im",
             "d_skip", "glu_w", "glu_b", "w_out", "final_g"]
    outs = [loss, dx.reshape(nb, seq, D_MODEL)]
    for kind in ("grad", "delta", "m", "v"):
        outs += [res[kind, n] for n in order]
    return tuple(outs)
```

```python
import functools
import math

import jax
import jax.numpy as jnp
from jax import lax
from jax.experimental import pallas as pl
from jax.experimental.pallas import tpu as pltpu

F32 = jnp.float32
MXU_DTYPE = jnp.bfloat16

D_MODEL = 1024
MIX = 1024
POOL_W = 512
SSM_W = 512
N_POOL_G = 4
POOL_GC = 128
SSM_G = 32
SSM_C = 16
SSM_P = 64
DEPTH = 4
NORM_EPS = 1e-5
N_DEV = 8

ADAM_LR = 0.001
ADAM_B1 = 0.9
ADAM_B2 = 0.999
ADAM_EPS = 1e-08
ADAM_WD = 0.01
ADAM_STEP = 10

SUBLANES = 8
LANES = 128
HALO = 16
STATE_ROWS = 8
STATE_COLS = 256
T_BLK = 256
TM_FWD = 512
TM_BWD = 512
VMEM_LIMIT = 56 * 1024 * 1024

MESH = pl.DeviceIdType.MESH
VMEM_SPEC = pl.BlockSpec(memory_space=pltpu.VMEM)
ANY_SPEC = pl.BlockSpec(memory_space=pl.ANY)


def _mm(a, b):
    return jnp.dot(a, b, preferred_element_type=F32)


def _mm_tn(a, b):
    return lax.dot_general(a, b, (((0,), (0,)), ((), ())), preferred_element_type=F32)


def _mm_nt(a, b):
    return lax.dot_general(a, b, (((1,), (1,)), ((), ())), preferred_element_type=F32)


def _mx(a):
    return a.astype(MXU_DTYPE)


def _sigmoid(v):
    return 1.0 / (1.0 + jnp.exp(-v))


_GELU_C = math.sqrt(2.0 / math.pi)
_GELU_A = 0.044715


def _gelu_and_grad(y):
    th = jnp.tanh(_GELU_C * (y + _GELU_A * y * y * y))
    val = 0.5 * y * (1.0 + th)
    grad = 0.5 * (1.0 + th) + 0.5 * y * (1.0 - th * th) * (_GELU_C * (1.0 + 3.0 * _GELU_A * y * y))
    return val, grad


def _params(**kw):
    return pltpu.CompilerParams(vmem_limit_bytes=VMEM_LIMIT, **kw)


def _ssm_dense(a_re, a_im, log_dt, b_re, b_im, c_re, c_im):
    dt = jnp.exp(log_dt)[:, None]
    mag = jnp.exp(a_re * dt)
    ang = a_im * dt
    lb_re = mag * jnp.cos(ang)
    lb_im = mag * jnp.sin(ang)
    den = a_re * a_re + a_im * a_im
    n_re = lb_re - 1.0
    n_im = lb_im
    f_re = (n_re * a_re + n_im * a_im) / den
    f_im = (n_im * a_re - n_re * a_im) / den
    bb_re = f_re[..., None] * b_re - f_im[..., None] * b_im
    bb_im = f_re[..., None] * b_im + f_im[..., None] * b_re

    bb = jnp.stack([bb_re, bb_im], axis=0).reshape(2, 8, 4, SSM_P, SSM_C)
    rb = bb.transpose(1, 4, 0, 2, 3).reshape(8, SSM_C, 512)
    cc = jnp.stack([c_re, -c_im], axis=0).reshape(2, 8, 4, SSM_C, SSM_P)
    rc = cc.transpose(1, 3, 0, 2, 4).reshape(8, SSM_C, 512)
    return (lb_re.reshape(STATE_ROWS, STATE_COLS), lb_im.reshape(STATE_ROWS, STATE_COLS), rb, rc)


def _ssm_chunked(per_channel):
    row_group = jnp.arange(64) // SSM_C
    col_group = (jnp.arange(512) // SSM_P) % 4
    own_group = (row_group[:, None] == col_group[None, :]).astype(F32)
    even = (jnp.arange(8) % 2 == 0).astype(F32)[:, None, None]
    half = jnp.tile(per_channel, (1, 4, 1)) * own_group
    return jnp.concatenate([half * even, half * (1.0 - even)], axis=1)


def _inproj_fwd(x2, g_row, w_all, dep):
    n = x2.shape[0]
    tm = TM_FWD

    def body(x_ref, g_ref, w_ref, dep_ref, z_ref, h_ref):
        x = x_ref[...]
        r = lax.rsqrt(jnp.mean(x * x, axis=-1, keepdims=True) + NORM_EPS)
        h = _mx(x * r * g_ref[...])
        h_ref[...] = h
        for d in range(N_DEV):
            z_ref[:, d * 256:(d + 1) * 256] = _mm(h, w_ref[d])

    return pl.pallas_call(
        body, name="inproj_fwd",
        grid=(n // tm,),
        in_specs=[pl.BlockSpec((tm, D_MODEL), lambda i: (i, 0)),
                  pl.BlockSpec((1, D_MODEL), lambda i: (0, 0)),
                  pl.BlockSpec((N_DEV, D_MODEL, 256), lambda i: (0, 0, 0)),
                  ANY_SPEC],
        out_specs=[pl.BlockSpec((tm, 2 * MIX), lambda i: (i, 0)),
                   pl.BlockSpec((tm, D_MODEL), lambda i: (i, 0))],
        out_shape=[jax.ShapeDtypeStruct((n, 2 * MIX), F32),
                   jax.ShapeDtypeStruct((n, D_MODEL), MXU_DTYPE)],
        compiler_params=_params(dimension_semantics=("arbitrary",)),
    )(x2, g_row, w_all, dep)


def _loss_head(x2, tgt2, g_row):
    n = x2.shape[0]
    tm = TM_FWD

    def body(x_ref, t_ref, g_ref, dx_ref, loss_ref, dg_ref):
        @pl.when(pl.program_id(0) == 0)
        def _():
            loss_ref[...] = jnp.zeros_like(loss_ref)
            dg_ref[...] = jnp.zeros_like(dg_ref)

        x = x_ref[...]
        g = g_ref[...]
        r = lax.rsqrt(jnp.mean(x * x, axis=-1, keepdims=True) + NORM_EPS)
        xh = x * r
        e = xh * g - t_ref[...]
        loss_ref[...] += jnp.sum(jnp.sum(e * e, axis=-1, keepdims=True), axis=0, keepdims=True) * (0.5 / D_MODEL)
        dout = e * (1.0 / D_MODEL)
        dg_ref[...] += jnp.sum(dout * xh, axis=0, keepdims=True)
        gdy = dout * g
        dx_ref[...] = r * (gdy - xh * jnp.mean(xh * gdy, axis=-1, keepdims=True))

    return pl.pallas_call(
        body, name="loss_head",
        grid=(n // tm,),
        in_specs=[pl.BlockSpec((tm, D_MODEL), lambda i: (i, 0)),
                  pl.BlockSpec((tm, D_MODEL), lambda i: (i, 0)),
                  pl.BlockSpec((1, D_MODEL), lambda i: (0, 0))],
        out_specs=[pl.BlockSpec((tm, D_MODEL), lambda i: (i, 0)),
                   pl.BlockSpec((1, 1), lambda i: (0, 0)),
                   pl.BlockSpec((1, D_MODEL), lambda i: (0, 0))],
        out_shape=[jax.ShapeDtypeStruct((n, D_MODEL), F32),
                   jax.ShapeDtypeStruct((1, 1), F32),
                   jax.ShapeDtypeStruct((1, D_MODEL), F32)],
        compiler_params=_params(dimension_semantics=("arbitrary",)),
    )(x2, tgt2, g_row)


def _outproj_bwd(dx2, yg, w_out, dep):
    n = dx2.shape[0]
    tm = TM_BWD
    n_steps = n // tm

    def body(dx_ref, y_ref, w_ref, dep_ref, dy_ref, dw_ref, acc_ref):
        i = pl.program_id(0)

        @pl.when(i == 0)
        def _():
            acc_ref[...] = jnp.zeros_like(acc_ref)

        dxb = _mx(dx_ref[...])
        dy_ref[...] = _mm_nt(dxb, w_ref[...])
        acc_ref[...] += _mm_tn(y_ref[...], dxb)

        @pl.when(i == n_steps - 1)
        def _():
            dw_ref[...] = _mx(acc_ref[...])

    return pl.pallas_call(
        body, name="outproj_bwd",
        grid=(n_steps,),
        in_specs=[pl.BlockSpec((tm, D_MODEL), lambda i: (i, 0)),
                  pl.BlockSpec((tm, MIX), lambda i: (i, 0)),
                  pl.BlockSpec((MIX, D_MODEL), lambda i: (0, 0)),
                  ANY_SPEC],
        out_specs=[pl.BlockSpec((tm, MIX), lambda i: (i, 0)),
                   pl.BlockSpec((MIX, D_MODEL), lambda i: (0, 0))],
        out_shape=[jax.ShapeDtypeStruct((n, MIX), F32),
                   jax.ShapeDtypeStruct((MIX, D_MODEL), MXU_DTYPE)],
        scratch_shapes=[pltpu.VMEM((MIX, D_MODEL), F32)],
        compiler_params=_params(dimension_semantics=("arbitrary",)),
    )(dx2, yg, w_out, dep)


def _inproj_bwd(dz, h, x2, dx_in, g_row, w_all):
    n = x2.shape[0]
    tm = TM_BWD
    n_steps = n // tm

    def body(dz_ref, h_ref, x_ref, dxi_ref, g_ref, w_ref, dxo_ref, dw_ref, dg_ref, acc_ref, wcat_ref):
        i = pl.program_id(0)

        @pl.when(i == 0)
        def _():
            acc_ref[...] = jnp.zeros_like(acc_ref)
            dg_ref[...] = jnp.zeros_like(dg_ref)
            for d in range(N_DEV):
                wcat_ref[:, d * 256:(d + 1) * 256] = w_ref[d]

        hb = h_ref[...]
        for d in range(N_DEV):
            acc_ref[d] += _mm_tn(hb, dz_ref[:, d * 256:(d + 1) * 256])
        dh = _mm_nt(dz_ref[...], wcat_ref[...])
        x = x_ref[...]
        r = lax.rsqrt(jnp.mean(x * x, axis=-1, keepdims=True) + NORM_EPS)
        xh = x * r
        dg_ref[...] += jnp.sum(dh * xh, axis=0, keepdims=True)
        gdy = dh * g_ref[...]
        dxo_ref[...] = dxi_ref[...] + r * (gdy - xh * jnp.mean(xh * gdy, axis=-1, keepdims=True))

        @pl.when(i == n_steps - 1)
        def _():
            dw_ref[...] = _mx(acc_ref[...])

    return pl.pallas_call(
        body, name="inproj_bwd",
        grid=(n_steps,),
        in_specs=[pl.BlockSpec((tm, 2 * MIX), lambda i: (i, 0)),
                  pl.BlockSpec((tm, D_MODEL), lambda i: (i, 0)),
                  pl.BlockSpec((tm, D_MODEL), lambda i: (i, 0)),
                  pl.BlockSpec((tm, D_MODEL), lambda i: (i, 0)),
                  pl.BlockSpec((1, D_MODEL), lambda i: (0, 0)),
                  pl.BlockSpec((N_DEV, D_MODEL, 256), lambda i: (0, 0, 0))],
        out_specs=[pl.BlockSpec((tm, D_MODEL), lambda i: (i, 0)),
                   pl.BlockSpec((N_DEV, D_MODEL, 256), lambda i: (0, 0, 0)),
                   pl.BlockSpec((1, D_MODEL), lambda i: (0, 0))],
        out_shape=[jax.ShapeDtypeStruct((n, D_MODEL), F32),
                   jax.ShapeDtypeStruct((N_DEV, D_MODEL, 256), MXU_DTYPE),
                   jax.ShapeDtypeStruct((1, D_MODEL), F32)],
        scratch_shapes=[pltpu.VMEM((N_DEV, D_MODEL, 256), F32),
                        pltpu.VMEM((D_MODEL, 2 * MIX), MXU_DTYPE)],
        compiler_params=_params(dimension_semantics=("arbitrary",)),
    )(dz, h, x2, dx_in, g_row, w_all)


def _row_pos(t0, rows):
    return t0 + lax.broadcasted_iota(jnp.int32, (rows, LANES), 0)


def _pool_window_mean(upad, g, t0, t_blk):
    k = 2 << g
    w = upad
    sh = 1
    while sh < k:
        w = w + pltpu.roll(w, sh, 0)
        sh *= 2
    count = jnp.minimum(_row_pos(t0, t_blk) + 1, k).astype(F32)
    return w[HALO:] / count - upad[HALO:]


def _pool_window_bwd(qpad, g, t_blk):
    k = 2 << g
    n = t_blk + HALO
    w = qpad
    sh = 1
    while sh < k:
        w = w + pltpu.roll(w, n - sh, 0)
        sh *= 2
    return w[:t_blk]


class _StateBuf:
    def __init__(self, refs, t_blk):
        self.refs = refs
        self.t_blk = t_blk

    def put_chunk(self, b, j, val):
        for c in range(4):
            self.refs[4 * b + c][pl.ds(j, self.t_blk, stride=STATE_ROWS), :] = val[:, c * LANES:(c + 1) * LANES]

    def get_chunk(self, b, j):
        return jnp.concatenate(
            [self.refs[4 * b + c][pl.ds(j, self.t_blk, stride=STATE_ROWS), :] for c in range(4)], axis=-1)

    def load(self, b, r, part):
        return jnp.concatenate(
            [self.refs[4 * b + 2 * part + h][pl.ds(r, STATE_ROWS), :] for h in range(2)], axis=-1)

    def store(self, b, r, part, val):
        for h in range(2):
            self.refs[4 * b + 2 * part + h][pl.ds(r, STATE_ROWS), :] = val[:, h * LANES:(h + 1) * LANES]


def _state_scratch(nb, t_blk):
    return [pltpu.VMEM((t_blk * STATE_ROWS, LANES), F32) for _ in range(4 * nb)]


def _ssm_project_in(u_ssm, wb_ref, buf, nb):
    t_blk = u_ssm.shape[0] // nb
    ub = _mx(u_ssm)
    for j in range(STATE_ROWS):
        m = j // 2
        bu = _mm(ub[:, m * LANES:(m + 1) * LANES], wb_ref[j])
        for b in range(nb):
            buf.put_chunk(b, j, bu[b * t_blk:(b + 1) * t_blk])


def _scan_forward(buf, lbr, lbi, init, nb):
    def body(t, carry):
        r = pl.multiple_of(t * STATE_ROWS, STATE_ROWS)
        out = []
        for b in range(nb):
            sr, si = carry[2 * b], carry[2 * b + 1]
            nr = lbr * sr - lbi * si + buf.load(b, r, 0)
            ni = lbr * si + lbi * sr + buf.load(b, r, 1)
            buf.store(b, r, 0, nr)
            buf.store(b, r, 1, ni)
            out += [nr, ni]
        return tuple(out)

    return lax.fori_loop(0, buf.t_blk, body, init, unroll=4)


def _ssm_project_out(chunk, wc_ref):
    tiles = []
    for m in range(4):
        acc = None
        for j in (2 * m, 2 * m + 1):
            part = _mm_nt(chunk(j), wc_ref[j])
            acc = part if acc is None else acc + part
        tiles.append(acc)
    return jnp.concatenate(tiles, axis=-1)


def _mixer_fwd(z3, x3, pool_w, pool_scale, lbr, lbi, wb, wc, d_skip, glu_w, glu_b, w_out):
    nb, seq, _ = z3.shape
    t_blk = min(T_BLK, seq)
    n_t = seq // t_blk
    halo_per_blk = t_blk // HALO
    rows = nb * t_blk

    def body(z_ref, zh_ref, x_ref, pw_ref, ps_ref, lbr_ref, lbi_ref, wb_ref, wc_ref, dsk_ref, gw_ref, gb_ref, wo_ref,
             yg_ref, sc_ref, xo_ref, carry_ref, *s_refs):
        i = pl.program_id(0)
        t0 = i * t_blk
        buf = _StateBuf(s_refs, t_blk)
        both = lambda lo, hi: z_ref[:, :, lo:hi].reshape(rows, hi - lo)

        @pl.when(i == 0)
        def _():
            carry_ref[...] = jnp.zeros_like(carry_ref)

        u_ssm = both(POOL_W, MIX)
        _ssm_project_in(u_ssm, wb_ref, buf, nb)
        init = tuple(carry_ref[b, :, h * STATE_COLS:(h + 1) * STATE_COLS] for b in range(nb) for h in range(2))
        fin = _scan_forward(buf, lbr_ref[...], lbi_ref[...], init, nb)
        for b in range(nb):
            carry_ref[b, :, 0:STATE_COLS] = fin[2 * b]
            carry_ref[b, :, STATE_COLS:2 * STATE_COLS] = fin[2 * b + 1]

        def chunk(j):
            states = _mx(jnp.concatenate([buf.get_chunk(b, j) for b in range(nb)], axis=0))
            sc_ref[:, j] = states.reshape(nb, t_blk, 2 * STATE_COLS)
            return states

        y = _ssm_project_out(chunk, wc_ref) + dsk_ref[...] * u_ssm
        yg, _ = _gelu_and_grad(y)
        o_ssm = yg * _sigmoid(_mm(_mx(yg), gw_ref[...]) + gb_ref[...])
        gp = both(MIX + POOL_W, 2 * MIX)
        parts = []
        first = (i == 0)
        for g in range(N_POOL_G):
            cols = slice(g * POOL_GC, (g + 1) * POOL_GC)
            pooled = []
            for b in range(nb):
                halo = jnp.where(first, 0.0, zh_ref[b, :, cols])
                pooled.append(_pool_window_mean(jnp.concatenate([halo, z_ref[b, :, cols]], axis=0), g, t0, t_blk))
            yp = _mm(_mx(jnp.concatenate(pooled, axis=0)), pw_ref[g]) * ps_ref[:, cols]
            gpp = both(MIX + g * POOL_GC, MIX + (g + 1) * POOL_GC)
            parts.append(_mx(yp * (gpp * _sigmoid(gpp))))
        parts.append(_mx(o_ssm * (gp * _sigmoid(gp))))
        gated = jnp.concatenate(parts, axis=-1)
        yg_ref[...] = gated.reshape(nb, t_blk, MIX)
        xo_ref[...] = x_ref[...] + _mm(gated, wo_ref[...]).reshape(nb, t_blk, D_MODEL)

    const = lambda *shape: pl.BlockSpec(shape, lambda i: (0,) * len(shape))
    tokens = lambda width: pl.BlockSpec((nb, t_blk, width), lambda i: (0, i, 0))
    return pl.pallas_call(
        body, name="mixer_fwd",
        grid=(n_t,),
        in_specs=[tokens(2 * MIX),
                  pl.BlockSpec((nb, HALO, POOL_W), lambda i: (0, jnp.maximum(i * halo_per_blk - 1, 0), 0)),
                  tokens(D_MODEL),
                  const(N_POOL_G, POOL_GC, POOL_GC), const(1, POOL_W),
                  const(STATE_ROWS, STATE_COLS), const(STATE_ROWS, STATE_COLS),
                  const(STATE_ROWS, LANES, 2 * STATE_COLS), const(STATE_ROWS, LANES, 2 * STATE_COLS),
                  const(1, SSM_W), const(SSM_W, SSM_W), const(1, SSM_W), const(MIX, D_MODEL)],
        out_specs=[tokens(MIX),
                   pl.BlockSpec((nb, STATE_ROWS, t_blk, 2 * STATE_COLS), lambda i: (0, 0, i, 0)),
                   tokens(D_MODEL)],
        out_shape=[jax.ShapeDtypeStruct((nb, seq, MIX), MXU_DTYPE),
                   jax.ShapeDtypeStruct((nb, STATE_ROWS, seq, 2 * STATE_COLS), MXU_DTYPE),
                   jax.ShapeDtypeStruct((nb, seq, D_MODEL), F32)],
        scratch_shapes=[pltpu.VMEM((nb, STATE_ROWS, 2 * STATE_COLS), F32)] + _state_scratch(nb, t_blk),
        compiler_params=_params(dimension_semantics=("arbitrary",)),
    )(z3, z3, x3, pool_w, pool_scale, lbr, lbi, wb, wc, d_skip, glu_w, glu_b, w_out)


def _mixer_bwd(z3, dy3, states, pool_w, pool_scale, lbr, lbi, wb, wc, d_skip, glu_w, glu_b):
    nb, seq, _ = z3.shape
    t_blk = min(T_BLK, seq)
    n_t = seq // t_blk
    halo_per_blk = t_blk // HALO
    rows = nb * t_blk

    def body(z_ref, zh_ref, dy_ref, sc_ref, sch_ref, pw_ref, ps_ref, lbr_ref, lbi_ref, wb_ref, wc_ref, dsk_ref,
             gw_ref, gb_ref,
             dz_ref, dpw_ref, dps_ref, dlbr_ref, dlbi_ref, dwb_ref, dwc_ref, ddsk_ref, dgw_ref, dgb_ref,
             gcarry_ref, qcarry_ref, du_ref, dgw_acc, *g_refs):
        i = pl.program_id(0)
        blk = n_t - 1 - i
        t0 = blk * t_blk
        gbuf = _StateBuf(g_refs, t_blk)

        @pl.when(i == 0)
        def _():
            gcarry_ref[...] = jnp.zeros_like(gcarry_ref)
            qcarry_ref[...] = jnp.zeros_like(qcarry_ref)
            for ref in (dpw_ref, dps_ref, dlbr_ref, dlbi_ref, dwb_ref, dwc_ref, ddsk_ref, dgw_acc, dgb_ref):
                ref[...] = jnp.zeros_like(ref)

        lbr_v = lbr_ref[...]
        lbi_v = lbi_ref[...]

        both = lambda ref, lo, hi: ref[:, :, lo:hi].reshape(rows, hi - lo)
        split = lambda val: val.reshape(nb, t_blk, val.shape[-1])
        states = lambda j: sc_ref[:, j].reshape(rows, 2 * STATE_COLS)
        first = (blk == 0)

        u_ssm = both(z_ref, POOL_W, MIX)
        y = _ssm_project_out(states, wc_ref) + dsk_ref[...] * u_ssm
        yg, dgelu = _gelu_and_grad(y)
        ygb = _mx(yg)
        sg = _sigmoid(_mm(ygb, gw_ref[...]) + gb_ref[...])
        o_ssm = yg * sg
        gp = both(z_ref, MIX + POOL_W, 2 * MIX)
        sgm = _sigmoid(gp)
        dyv = both(dy_ref, POOL_W, MIX)
        dz_ref[:, :, MIX + POOL_W:2 * MIX] = split(_mx(dyv * o_ssm * (sgm * (1.0 + gp * (1.0 - sgm)))))
        do = dyv * (gp * sgm)
        dv = do * yg * (sg * (1.0 - sg))
        dvb = _mx(dv)
        dgb_ref[...] += jnp.sum(dv, axis=0, keepdims=True)
        dgw_acc[...] += _mm_tn(ygb, dvb)
        dyp = (do * sg + _mm_nt(dvb, gw_ref[...])) * dgelu
        ddsk_ref[...] += jnp.sum(dyp * u_ssm, axis=0, keepdims=True)
        dypb = _mx(dyp)
        for j in range(STATE_ROWS):
            m = j // 2
            dyt = dypb[:, m * LANES:(m + 1) * LANES]
            ds = _mm(dyt, wc_ref[j])
            for b in range(nb):
                gbuf.put_chunk(b, j, ds[b * t_blk:(b + 1) * t_blk])
            dwc_ref[j] += _mm_tn(dyt, states(j))
        du_ref[...] = split(dsk_ref[...] * dyp)

        for g in range(N_POOL_G):
            cols = slice(g * POOL_GC, (g + 1) * POOL_GC)
            pooled = []
            for b in range(nb):
                halo = jnp.where(first, 0.0, zh_ref[b, :, cols])
                pooled.append(_pool_window_mean(jnp.concatenate([halo, z_ref[b, :, cols]], axis=0), g, t0, t_blk))
            pb = _mx(jnp.concatenate(pooled, axis=0))
            ypre = _mm(pb, pw_ref[g])
            gpp = both(z_ref, MIX + g * POOL_GC, MIX + (g + 1) * POOL_GC)
            sgp = _sigmoid(gpp)
            dyg = both(dy_ref, g * POOL_GC, (g + 1) * POOL_GC)
            scale = ps_ref[:, cols]
            dz_ref[:, :, MIX + g * POOL_GC:MIX + (g + 1) * POOL_GC] = split(_mx(
                dyg * (ypre * scale) * (sgp * (1.0 + gpp * (1.0 - sgp)))))
            dyc = dyg * (gpp * sgp)
            dps_ref[:, cols] += jnp.sum(dyc * ypre, axis=0, keepdims=True)
            dypre = _mx(dyc * scale)
            dpw_ref[g] += _mm_tn(pb, dypre)
            dpooled = _mm_nt(dypre, pw_ref[g])
            count = jnp.minimum(_row_pos(t0, t_blk) + 1, 2 << g).astype(F32)
            for b in range(nb):
                dp = dpooled[b * t_blk:(b + 1) * t_blk]
                q = dp / count
                qpad = jnp.concatenate([q, qcarry_ref[b, :, cols]], axis=0)
                qcarry_ref[b, :, cols] = q[:HALO]
                dz_ref[b, :, cols] = _mx(_pool_window_bwd(qpad, g, t_blk) - dp)

        def rev_body(k, carry):
            r = pl.multiple_of((t_blk - 1 - k) * STATE_ROWS, STATE_ROWS)
            out = []
            for b in range(nb):
                gr, gi = carry[2 * b], carry[2 * b + 1]
                ngr = lbr_v * gr + lbi_v * gi + gbuf.load(b, r, 0)
                ngi = lbr_v * gi - lbi_v * gr + gbuf.load(b, r, 1)
                gbuf.store(b, r, 0, ngr)
                gbuf.store(b, r, 1, ngi)
                out += [ngr, ngi]
            return tuple(out)

        init_g = tuple(gcarry_ref[b, :, h * STATE_COLS:(h + 1) * STATE_COLS] for b in range(nb) for h in range(2))
        fin = lax.fori_loop(0, t_blk, rev_body, init_g, unroll=4)
        for b in range(nb):
            gcarry_ref[b, :, 0:STATE_COLS] = fin[2 * b]
            gcarry_ref[b, :, STATE_COLS:2 * STATE_COLS] = fin[2 * b + 1]

        ub = _mx(u_ssm)
        for m in range(4):
            acc = both(du_ref, m * LANES, (m + 1) * LANES)
            for j in (2 * m, 2 * m + 1):
                g = jnp.concatenate([gbuf.get_chunk(b, j) for b in range(nb)], axis=0)
                gj = _mx(g)
                acc = acc + _mm_nt(gj, wb_ref[j])
                dwb_ref[j] += _mm_tn(ub[:, m * LANES:(m + 1) * LANES], gj)
                shifted = []
                for b in range(nb):
                    before = jnp.where(first, 0.0, sch_ref[b, j].astype(F32))
                    spad = jnp.concatenate([before, sc_ref[b, j].astype(F32)], axis=0)
                    shifted.append(pltpu.roll(spad, 1, 0)[HALO:])
                s_prev = jnp.concatenate(shifted, axis=0)
                g_re, g_im = g[:, :STATE_COLS], g[:, STATE_COLS:]
                p_re, p_im = s_prev[:, :STATE_COLS], s_prev[:, STATE_COLS:]
                dlbr_ref[j:j + 1, :] += jnp.sum(g_re * p_re + g_im * p_im, axis=0, keepdims=True)
                dlbi_ref[j:j + 1, :] += jnp.sum(g_im * p_re - g_re * p_im, axis=0, keepdims=True)
            dz_ref[:, :, POOL_W + m * LANES:POOL_W + (m + 1) * LANES] = split(_mx(acc))

        @pl.when(i == n_t - 1)
        def _():
            dgw_ref[...] = _mx(dgw_acc[...])

    const = lambda *shape: pl.BlockSpec(shape, lambda i: (0,) * len(shape))
    rev = lambda i: n_t - 1 - i
    out_shape = [jax.ShapeDtypeStruct((nb, seq, 2 * MIX), MXU_DTYPE),
                 jax.ShapeDtypeStruct((N_POOL_G, POOL_GC, POOL_GC), F32),
                 jax.ShapeDtypeStruct((1, POOL_W), F32),
                 jax.ShapeDtypeStruct((STATE_ROWS, STATE_COLS), F32),
                 jax.ShapeDtypeStruct((STATE_ROWS, STATE_COLS), F32),
                 jax.ShapeDtypeStruct((STATE_ROWS, LANES, 2 * STATE_COLS), F32),
                 jax.ShapeDtypeStruct((STATE_ROWS, LANES, 2 * STATE_COLS), F32),
                 jax.ShapeDtypeStruct((1, SSM_W), F32),
                 jax.ShapeDtypeStruct((SSM_W, SSM_W), MXU_DTYPE),
                 jax.ShapeDtypeStruct((1, SSM_W), F32)]
    return pl.pallas_call(
        body, name="mixer_bwd",
        grid=(n_t,),
        in_specs=[pl.BlockSpec((nb, t_blk, 2 * MIX), lambda i: (0, rev(i), 0)),
                  pl.BlockSpec((nb, HALO, POOL_W), lambda i: (0, jnp.maximum(rev(i) * halo_per_blk - 1, 0), 0)),
                  pl.BlockSpec((nb, t_blk, MIX), lambda i: (0, rev(i), 0)),
                  pl.BlockSpec((nb, STATE_ROWS, t_blk, 2 * STATE_COLS), lambda i: (0, 0, rev(i), 0)),
                  pl.BlockSpec((nb, STATE_ROWS, HALO, 2 * STATE_COLS),
                               lambda i: (0, 0, jnp.maximum(rev(i) * halo_per_blk - 1, 0), 0)),
                  const(N_POOL_G, POOL_GC, POOL_GC), const(1, POOL_W),
                  const(STATE_ROWS, STATE_COLS), const(STATE_ROWS, STATE_COLS),
                  const(STATE_ROWS, LANES, 2 * STATE_COLS), const(STATE_ROWS, LANES, 2 * STATE_COLS),
                  const(1, SSM_W), const(SSM_W, SSM_W), const(1, SSM_W)],
        out_specs=[pl.BlockSpec((nb, t_blk, 2 * MIX), lambda i: (0, rev(i), 0))]
                  + [const(*s.shape) for s in out_shape[1:]],
        out_shape=out_shape,
        scratch_shapes=[pltpu.VMEM((nb, STATE_ROWS, 2 * STATE_COLS), F32),
                        pltpu.VMEM((nb, HALO, POOL_W), F32),
                        pltpu.VMEM((nb, t_blk, SSM_W), F32),
                        pltpu.VMEM((SSM_W, SSM_W), F32)]
                       + _state_scratch(nb, t_blk),
        compiler_params=_params(dimension_semantics=("arbitrary",)),
    )(z3, z3, dy3, states, states, pool_w, pool_scale, lbr, lbi, wb, wc, d_skip, glu_w, glu_b)


def _mesh_place():
    x, y, c = lax.axis_index("x"), lax.axis_index("y"), lax.axis_index("c")
    return x, y, c


def _flip(place, k):
    x, y, c = place
    return (1 - x if k & 4 else x, 1 - y if k & 2 else y, 1 - c if k & 1 else c)


def _index(place):
    x, y, c = place
    return 4 * x + 2 * y + c


HBM_SPEC = pl.BlockSpec(memory_space=pltpu.HBM)
SEM_SPEC = pl.BlockSpec(memory_space=pltpu.SEMAPHORE)
_EFFECT = pltpu.SideEffectType.DATAFLOW_SIDE_EFFECTING
N_PEERS = N_DEV - 1


def _exchange_copies(src_refs, land_refs, send_sems, recv_sems):
    me = _mesh_place()
    mine = _index(me)
    out = []
    for a, land_ref in enumerate(land_refs):
        for k in range(1, N_DEV):
            peer = _flip(me, k)
            theirs = _index(peer)
            n = a * N_PEERS + k - 1
            src = src_refs[a].at[theirs] if src_refs else land_ref.at[mine]
            send = pltpu.make_async_remote_copy(
                src_ref=src, dst_ref=land_ref.at[mine], send_sem=send_sems.at[n], recv_sem=recv_sems.at[n],
                device_id=peer, device_id_type=MESH)
            recv = pltpu.make_async_remote_copy(
                src_ref=src, dst_ref=land_ref.at[theirs], send_sem=send_sems.at[n], recv_sem=recv_sems.at[n],
                device_id=peer, device_id_type=MESH)
            out.append((send, recv))
    return out


def _exchange_start(srcs, lands, after, name):
    arrays = tuple(srcs) + tuple(lands)
    n_src, n_all = len(srcs), len(arrays)
    n_copies = len(lands) * N_PEERS

    def body(*refs):
        send_sems, recv_sems = refs[n_all + 1], refs[n_all + 2]
        token = refs[-1]
        for send, _ in _exchange_copies(refs[:n_src], refs[n_src:n_all], send_sems, recv_sems):
            send.start()
        token[...] = jnp.zeros_like(token)

    res = pl.pallas_call(
        body, name=name,
        in_specs=[HBM_SPEC] * n_all + [ANY_SPEC],
        out_specs=[SEM_SPEC, SEM_SPEC] + [HBM_SPEC] * n_all + [VMEM_SPEC],
        out_shape=[pltpu.SemaphoreType.DMA((n_copies,)), pltpu.SemaphoreType.DMA((n_copies,))]
                  + [pltpu.HBM(a.shape, a.dtype) for a in arrays] + [jax.ShapeDtypeStruct((SUBLANES, LANES), F32)],
        input_output_aliases={i: 2 + i for i in range(n_all)},
        compiler_params=pltpu.CompilerParams(has_side_effects=_EFFECT),
    )(*[pltpu.with_memory_space_constraint(a, pltpu.HBM) for a in arrays], after)
    return tuple(res[:-1]), res[-1]


def _exchange_wait(handle, n_lands, after, name):
    send_sems, recv_sems = handle[0], handle[1]
    arrays = handle[2:]
    n_all = len(arrays)
    n_src = n_all - n_lands

    def body(*refs):
        for send, recv in _exchange_copies(refs[:n_src], refs[n_src:n_all], refs[n_all], refs[n_all + 1]):
            send.wait_send()
            recv.wait_recv()

    res = pl.pallas_call(
        body, name=name,
        in_specs=[HBM_SPEC] * n_all + [SEM_SPEC, SEM_SPEC, ANY_SPEC],
        out_specs=[HBM_SPEC] * n_all,
        out_shape=[pltpu.HBM(a.shape, a.dtype) for a in arrays],
        input_output_aliases={i: i for i in range(n_all)},
        compiler_params=pltpu.CompilerParams(has_side_effects=_EFFECT),
    )(*arrays, send_sems, recv_sems, after)
    return tuple(res[:n_src]), tuple(res[n_src:])


def _weight_zones(w_in, glu_w, w_out, my_idx):
    shards = (w_in, glu_w, w_out)
    depth = w_in.shape[0]

    def body(idx_ref, *refs):
        ins, zones = refs[:len(shards)], refs[len(shards):]
        for l in range(depth):
            for a, src in enumerate(ins):
                zones[l * len(shards) + a][0] = _mx(src[l])

    whole = lambda s: pl.BlockSpec(s.shape, lambda i, idx: (0,) * s.ndim)
    return pl.pallas_call(
        body, name="weight_zones",
        grid_spec=pltpu.PrefetchScalarGridSpec(
            num_scalar_prefetch=1, grid=(1,),
            in_specs=[whole(s) for s in shards],
            out_specs=[pl.BlockSpec((1,) + s.shape[1:], lambda i, idx: (idx[0], 0, 0))
                       for _ in range(depth) for s in shards]),
        out_shape=[jax.ShapeDtypeStruct((N_DEV,) + s.shape[1:], MXU_DTYPE) for _ in range(depth) for s in shards],
        compiler_params=_params(dimension_semantics=("arbitrary",)),
    )(my_idx.reshape(1).astype(jnp.int32), *shards)


def _allreduce_packed(p):
    rows = p.shape[0]
    half = rows // 2
    quarter = half // 4

    def body(p_ref, o_ref, part_ref, sib_ref, got_ref, send_sems, recv_sems):
        x, y, c = _mesh_place()
        sibling = (x, y, 1 - c)
        chip = 2 * x + y
        chips = [(k, (1 - x if k & 2 else x, 1 - y if k & 1 else y, c), chip ^ k) for k in (1, 2, 3)]
        my_half = pl.multiple_of(c * half, SUBLANES)
        other_half = pl.multiple_of((1 - c) * half, SUBLANES)

        def copy(n, src, dst, to):
            return pltpu.make_async_remote_copy(src_ref=src, dst_ref=dst, send_sem=send_sems.at[n],
                                                recv_sem=recv_sems.at[n], device_id=to, device_id_type=MESH)

        def quarter_of(ref, base, q):
            return ref.at[pl.ds(pl.multiple_of(base + q * quarter, SUBLANES), quarter)]

        swap = copy(0, p_ref.at[pl.ds(other_half, half)], sib_ref, sibling)
        swap.start()
        swap.wait()
        part_ref[...] = p_ref[pl.ds(my_half, half), :] + sib_ref[...]

        scatter = [copy(k, quarter_of(part_ref, 0, q), got_ref.at[k - 1], to) for k, to, q in chips]
        for cp in scatter:
            cp.start()
        total = part_ref[pl.ds(pl.multiple_of(chip * quarter, SUBLANES), quarter), :]
        for cp, (k, _, _) in zip(scatter, chips):
            cp.wait()
            total = total + got_ref[k - 1]
        mine = pl.multiple_of(my_half + chip * quarter, SUBLANES)
        o_ref[pl.ds(mine, quarter), :] = total

        gather = [copy(3 + k, o_ref.at[pl.ds(mine, quarter)], o_ref.at[pl.ds(mine, quarter)], to) for k, to, _ in chips]
        for cp in gather:
            cp.start()
        for k, to, q in chips:
            theirs = quarter_of(o_ref, my_half, q)
            copy(3 + k, theirs, theirs, to).wait_recv()
        for cp in gather:
            cp.wait_send()

        back = copy(7, o_ref.at[pl.ds(my_half, half)], o_ref.at[pl.ds(my_half, half)], sibling)
        back.start()
        copy(7, o_ref.at[pl.ds(other_half, half)], o_ref.at[pl.ds(other_half, half)], sibling).wait_recv()
        back.wait_send()

    return pl.pallas_call(
        body, name="comm_allreduce_packed",
        in_specs=[VMEM_SPEC],
        out_specs=VMEM_SPEC,
        out_shape=jax.ShapeDtypeStruct(p.shape, F32),
        scratch_shapes=[pltpu.VMEM((half, LANES), F32),
                        pltpu.VMEM((half, LANES), F32),
                        pltpu.VMEM((3, quarter, LANES), F32),
                        pltpu.SemaphoreType.DMA((8,)),
                        pltpu.SemaphoreType.DMA((8,))],
        compiler_params=_params(),
    )(p)


def _adamw_math(w, g, m, v):
    m = ADAM_B1 * m + (1.0 - ADAM_B1) * g
    v = ADAM_B2 * v + (1.0 - ADAM_B2) * (g * g)
    m_hat = m / (1.0 - ADAM_B1 ** ADAM_STEP)
    v_hat = v / (1.0 - ADAM_B2 ** ADAM_STEP)
    delta = -ADAM_LR * (m_hat / (jnp.sqrt(v_hat) + ADAM_EPS) + ADAM_WD * w)
    return delta, m, v


def _adamw_summed(received, own, my_idx, w, m, v, name):
    depth, r, c = w.shape
    tr = min(r, 128)

    def body(idx_ref, *refs):
        r_refs, o_refs = refs[:depth], refs[depth:2 * depth]
        w_ref, m_ref, v_ref, g_ref, d_ref, nm_ref, nv_ref = refs[2 * depth:]
        me = idx_ref[0]
        for l in range(depth):
            g = jnp.zeros((tr, c), F32)
            for q in range(N_DEV):
                g = g + jnp.where(q == me, o_refs[l][0], r_refs[l][q]).astype(F32)
            g_ref[l] = g
            d_ref[l], nm_ref[l], nv_ref[l] = _adamw_math(w_ref[l], g, m_ref[l], v_ref[l])

    blk = pl.BlockSpec((depth, tr, c), lambda i, idx: (0, i, 0))
    return pl.pallas_call(
        body, name=name,
        grid_spec=pltpu.PrefetchScalarGridSpec(
            num_scalar_prefetch=1, grid=(r // tr,),
            in_specs=[pl.BlockSpec((N_DEV, tr, c), lambda i, idx: (0, i, 0))] * depth
                     + [pl.BlockSpec((1, tr, c), lambda i, idx: (idx[0], i, 0))] * depth
                     + [blk, blk, blk],
            out_specs=[blk] * 4),
        out_shape=[jax.ShapeDtypeStruct((depth, r, c), F32)] * 4,
        compiler_params=_params(dimension_semantics=("arbitrary",)),
    )(my_idx.reshape(1).astype(jnp.int32), *received, *own, w, m, v)


def _adamw_small(ws, gs, ms, vs):
    n = len(ws)
    depth = ws[0].shape[0]
    quarters = 4

    def spec(a):
        per_layer = a.shape[0] == depth
        split = a.ndim >= 3 and a.shape[1] % quarters == 0 and a.shape[1] >= quarters
        block = (1, a.shape[1] // quarters if split else a.shape[1]) + a.shape[2:]
        rest = (0,) * (a.ndim - 2)
        return pl.BlockSpec(block, lambda l, s: ((l if per_layer else 0), (s if split else 0)) + rest)

    def body(*refs):
        w_refs, g_refs, m_refs, v_refs = (refs[k * n:(k + 1) * n] for k in range(4))
        d_refs, nm_refs, nv_refs = (refs[(4 + k) * n:(5 + k) * n] for k in range(3))
        for k in range(n):
            d_refs[k][...], nm_refs[k][...], nv_refs[k][...] = _adamw_math(
                w_refs[k][...], g_refs[k][...], m_refs[k][...], v_refs[k][...])

    specs = [spec(a) for a in ws]
    shapes = [jax.ShapeDtypeStruct(a.shape, F32) for a in ws]
    res = pl.pallas_call(
        body, name="adamw_small",
        grid=(depth, quarters),
        in_specs=specs * 4,
        out_specs=specs * 3,
        out_shape=shapes * 3,
        compiler_params=_params(dimension_semantics=("arbitrary", "arbitrary")),
    )(*ws, *gs, *ms, *vs)
    return res[:n], res[n:2 * n], res[2 * n:]


_PACK_ROWS = SUBLANES * N_DEV


def _pack(arrays):
    flat = jnp.concatenate([a.reshape(-1) for a in arrays])
    per = _PACK_ROWS * LANES
    total = -(-flat.shape[0] // per) * per
    flat = jnp.pad(flat, (0, total - flat.shape[0]))
    return flat.reshape(total // LANES, LANES)


def _unpack(packed, like):
    flat = packed.reshape(-1)
    out = []
    off = 0
    for a in like:
        out.append(flat[off:off + a.size].reshape(a.shape))
        off += a.size
    return out


def kernel(x, norm_g, w_in, pool_w, pool_scale, a_re, a_im, log_dt, b_re, b_im, c_re, c_im, d_skip, glu_w, glu_b, w_out, final_g, loss_target, m_norm_g, m_w_in, m_pool_w, m_pool_scale, m_a_re, m_a_im, m_log_dt, m_b_re, m_b_im, m_c_re, m_c_im, m_d_skip, m_glu_w, m_glu_b, m_w_out, m_final_g, v_norm_g, v_w_in, v_pool_w, v_pool_scale, v_a_re, v_a_im, v_log_dt, v_b_re, v_b_im, v_c_re, v_c_im, v_d_skip, v_glu_w, v_glu_b, v_w_out, v_final_g):
    nb, seq, _ = x.shape
    n_tok = nb * seq
    depth = norm_g.shape[0]

    my_idx = _index(_mesh_place())

    zones = _weight_zones(w_in, glu_w, w_out, my_idx)

    def gather_start(l, after):
        return _exchange_start((), zones[3 * l:3 * l + 3], after, f"comm_gather_start_{l}")

    def gather_wait(handle, after, l):
        _, (win, glu, wout) = _exchange_wait(handle, 3, after, f"comm_gather_wait_{l}")
        return win, glu.reshape(SSM_W, SSM_W), wout.reshape(MIX, D_MODEL)

    xs = [x.reshape(n_tok, D_MODEL)]
    handle, dep = gather_start(0, xs[0])

    (lbr, lbi, rb, rc), dense_vjp = jax.vjp(jax.vmap(_ssm_dense), a_re, a_im, log_dt + dep[0, 0], b_re, b_im, c_re, c_im)
    chunk_all = jax.vmap(_ssm_chunked)
    (wb, wct), chunk_vjp = jax.vjp(lambda p, q: (chunk_all(p), chunk_all(q)), rb, rc)
    wb_m, wct_m = _mx(wb), _mx(wct)
    pool_w_m = _mx(pool_w)

    def layer_params(l):
        return (pool_w_m[l], pool_scale[l][None], lbr[l], lbi[l], wb_m[l], wct_m[l], d_skip[l][None],
                weights[l][1], glu_b[l][None])

    saved = []
    weights = []
    for l in range(depth):
        weights.append(gather_wait(handle, wct_m if l == 0 else xs[-1], l))
        if l + 1 < depth:
            handle, dep = gather_start(l + 1, weights[l][0])
        z, h = _inproj_fwd(xs[-1], norm_g[l][None], weights[l][0], dep)
        z3 = z.reshape(nb, seq, 2 * MIX)
        yg, states, x_next = _mixer_fwd(z3, xs[-1].reshape(nb, seq, D_MODEL), *layer_params(l), weights[l][2])
        xs.append(x_next.reshape(n_tok, D_MODEL))
        saved.append((z3, h, yg.reshape(n_tok, MIX), states))

    dx, loss_part, d_final_g = _loss_head(xs[-1], loss_target.reshape(n_tok, D_MODEL), final_g[None])
    loss = lax.psum(loss_part[0, 0], ("x", "y", "c"))

    small = {k: [None] * depth for k in
             ("norm_g", "pool_w", "pool_scale", "lbr", "lbi", "wb", "wct", "d_skip", "glu_b")}
    received = [None] * depth
    sent = [None] * depth
    pending = None
    for l in reversed(range(depth)):
        z3, h, yg2, states = saved[l]
        dy, d_wout = _outproj_bwd(dx, yg2, weights[l][2], dep)
        (dz, d_pw, d_ps, d_lbr, d_lbi, d_wb, d_wct, d_dsk, d_gw, d_gb) = _mixer_bwd(
            z3, dy.reshape(nb, seq, MIX), states, *layer_params(l))
        dx, d_win, d_ng = _inproj_bwd(dz.reshape(n_tok, 2 * MIX), h, xs[l], dx, norm_g[l][None], weights[l][0])
        for k, val in (("norm_g", d_ng[0]), ("pool_w", d_pw), ("pool_scale", d_ps[0]), ("lbr", d_lbr),
                       ("lbi", d_lbi), ("wb", d_wb), ("wct", d_wct), ("d_skip", d_dsk[0]), ("glu_b", d_gb[0])):
            small[k][l] = val
        if pending is not None:
            sent[l + 1], received[l + 1] = _exchange_wait(pending, 3, dx, f"comm_grads_wait_{l + 1}")
        sent[l] = (d_win, d_gw.reshape(N_DEV, SSM_W // N_DEV, SSM_W), d_wout.reshape(N_DEV, MIX // N_DEV, D_MODEL))
        lands = tuple(lax.empty(s.shape, s.dtype) for s in sent[l])
        pending, dep = _exchange_start(sent[l], lands, dx, f"comm_grads_start_{l}")
    stack = lambda k: jnp.stack(small[k])
    d_rb, d_rc = chunk_vjp((stack("wb"), stack("wct")))
    local = [stack("norm_g"), stack("pool_w"), stack("pool_scale"), stack("lbr"), stack("lbi"), d_rb, d_rc,
             stack("d_skip"), stack("glu_b"), d_final_g[0] + dep[0, 0]]
    (g_norm_g, g_pool_w, g_pool_scale, g_lbr, g_lbi, g_rb, g_rc, g_d_skip, g_glu_b, g_final_g) = _unpack(
        _allreduce_packed(_pack(local)), local)
    g_a_re, g_a_im, g_log_dt, g_b_re, g_b_im, g_c_re, g_c_im = dense_vjp((g_lbr, g_lbi, g_rb, g_rc))

    names = ["norm_g", "pool_w", "pool_scale", "a_re", "a_im", "log_dt", "b_re", "b_im", "c_re", "c_im",
             "d_skip", "glu_b", "final_g"]
    rows = {"norm_g", "pool_scale", "log_dt", "d_skip", "glu_b"}
    small_w = [norm_g, pool_w, pool_scale, a_re, a_im, log_dt, b_re, b_im, c_re, c_im, d_skip, glu_b, final_g]
    small_g = [g_norm_g, g_pool_w, g_pool_scale, g_a_re, g_a_im, g_log_dt, g_b_re, g_b_im, g_c_re, g_c_im,
               g_d_skip, g_glu_b, g_final_g]
    small_m = [m_norm_g, m_pool_w, m_pool_scale, m_a_re, m_a_im, m_log_dt, m_b_re, m_b_im, m_c_re, m_c_im,
               m_d_skip, m_glu_b, m_final_g]
    small_v = [v_norm_g, v_pool_w, v_pool_scale, v_a_re, v_a_im, v_log_dt, v_b_re, v_b_im, v_c_re, v_c_im,
               v_d_skip, v_glu_b, v_final_g]

    wide_last = {"b_re", "b_im"}

    def blocked(arrays):
        return [a.reshape(1, 1, -1) if n == "final_g" else a[:, None, :] if n in rows
                else a.swapaxes(2, 3) if n in wide_last else a for n, a in zip(names, arrays)]

    small_d, small_nm, small_nv = _adamw_small(blocked(small_w), blocked(small_g), blocked(small_m), blocked(small_v))
    res = {}
    for kind, arrays in (("grad", small_g), ("delta", small_d), ("m", small_nm), ("v", small_nv)):
        for n, a, like in zip(names, arrays, small_w):
            if kind != "grad" and n in wide_last:
                a = a.swapaxes(2, 3)
            res[kind, n] = a.reshape(like.shape)

    sent[0], received[0] = _exchange_wait(pending, 3, small_d[0], "comm_grads_wait_0")
    shard_res = {}
    for pos, (n, w, m, v) in enumerate((("w_in", w_in, m_w_in, v_w_in), ("glu_w", glu_w, m_glu_w, v_glu_w),
                                        ("w_out", w_out, m_w_out, v_w_out))):
        shard_res[n] = _adamw_summed([received[l][pos] for l in range(depth)], [sent[l][pos] for l in range(depth)],
                                     my_idx, w, m, v, "adamw_" + n)
    for n in ("w_in", "glu_w", "w_out"):
        for pos, kind in enumerate(("grad", "delta", "m", "v")):
            res[kind, n] = shard_res[n][pos]

    order = ["norm_g", "w_in", "pool_w", "pool_scale", "a_re", "a_im", "log_dt", "b_re", "b_im", "c_re", "c_im",
             "d_skip", "glu_w", "glu_b", "w_out", "final_g"]
    outs = [loss, dx.reshape(nb, seq, D_MODEL)]
    for kind in ("grad", "delta", "m", "v"):
        outs += [res[kind, n] for n in order]
    return tuple(outs)
```

```python
import functools
import math

import jax
import jax.numpy as jnp
from jax import lax
from jax.experimental import pallas as pl
from jax.experimental.pallas import tpu as pltpu

F32 = jnp.float32
MXU_DTYPE = jnp.bfloat16

D_MODEL = 1024
MIX = 1024
POOL_W = 512
SSM_W = 512
N_POOL_G = 4
POOL_GC = 128
SSM_G = 32
SSM_C = 16
SSM_P = 64
DEPTH = 4
NORM_EPS = 1e-5
N_DEV = 8

ADAM_LR = 0.001
ADAM_B1 = 0.9
ADAM_B2 = 0.999
ADAM_EPS = 1e-08
ADAM_WD = 0.01
ADAM_STEP = 10

SUBLANES = 8
LANES = 128
HALO = 16
STATE_ROWS = 8
STATE_COLS = 256
T_BLK = 256
TM_FWD = 512
TM_BWD = 512
VMEM_LIMIT = 56 * 1024 * 1024

MESH = pl.DeviceIdType.MESH
VMEM_SPEC = pl.BlockSpec(memory_space=pltpu.VMEM)
ANY_SPEC = pl.BlockSpec(memory_space=pl.ANY)


def _mm(a, b):
    return jnp.dot(a, b, preferred_element_type=F32)


def _mm_tn(a, b):
    return lax.dot_general(a, b, (((0,), (0,)), ((), ())), preferred_element_type=F32)


def _mm_nt(a, b):
    return lax.dot_general(a, b, (((1,), (1,)), ((), ())), preferred_element_type=F32)


def _mx(a):
    return a.astype(MXU_DTYPE)


def _sigmoid(v):
    return 1.0 / (1.0 + jnp.exp(-v))


_GELU_C = math.sqrt(2.0 / math.pi)
_GELU_A = 0.044715


def _gelu_and_grad(y):
    th = jnp.tanh(_GELU_C * (y + _GELU_A * y * y * y))
    val = 0.5 * y * (1.0 + th)
    grad = 0.5 * (1.0 + th) + 0.5 * y * (1.0 - th * th) * (_GELU_C * (1.0 + 3.0 * _GELU_A * y * y))
    return val, grad


def _params(**kw):
    return pltpu.CompilerParams(vmem_limit_bytes=VMEM_LIMIT, **kw)


def _ssm_dense(a_re, a_im, log_dt, b_re, b_im, c_re, c_im):
    dt = jnp.exp(log_dt)[:, None]
    mag = jnp.exp(a_re * dt)
    ang = a_im * dt
    lb_re = mag * jnp.cos(ang)
    lb_im = mag * jnp.sin(ang)
    den = a_re * a_re + a_im * a_im
    n_re = lb_re - 1.0
    n_im = lb_im
    f_re = (n_re * a_re + n_im * a_im) / den
    f_im = (n_im * a_re - n_re * a_im) / den
    bb_re = f_re[..., None] * b_re - f_im[..., None] * b_im
    bb_im = f_re[..., None] * b_im + f_im[..., None] * b_re

    bb = jnp.stack([bb_re, bb_im], axis=0).reshape(2, 8, 4, SSM_P, SSM_C)
    rb = bb.transpose(1, 4, 0, 2, 3).reshape(8, SSM_C, 512)
    cc = jnp.stack([c_re, -c_im], axis=0).reshape(2, 8, 4, SSM_C, SSM_P)
    rc = cc.transpose(1, 3, 0, 2, 4).reshape(8, SSM_C, 512)
    return (lb_re.reshape(STATE_ROWS, STATE_COLS), lb_im.reshape(STATE_ROWS, STATE_COLS), rb, rc)


def _ssm_chunked(per_channel):
    row_group = jnp.arange(64) // SSM_C
    col_group = (jnp.arange(512) // SSM_P) % 4
    own_group = (row_group[:, None] == col_group[None, :]).astype(F32)
    even = (jnp.arange(8) % 2 == 0).astype(F32)[:, None, None]
    half = jnp.tile(per_channel, (1, 4, 1)) * own_group
    return jnp.concatenate([half * even, half * (1.0 - even)], axis=1)


def _inproj_fwd(x2, g_row, w_all, dep):
    n = x2.shape[0]
    tm = TM_FWD

    def body(x_ref, g_ref, w_ref, dep_ref, z_ref, h_ref):
        x = x_ref[...]
        r = lax.rsqrt(jnp.mean(x * x, axis=-1, keepdims=True) + NORM_EPS)
        h = _mx(x * r * g_ref[...])
        h_ref[...] = h
        for d in range(N_DEV):
            z_ref[:, d * 256:(d + 1) * 256] = _mm(h, w_ref[d])

    return pl.pallas_call(
        body, name="inproj_fwd",
        grid=(n // tm,),
        in_specs=[pl.BlockSpec((tm, D_MODEL), lambda i: (i, 0)),
                  pl.BlockSpec((1, D_MODEL), lambda i: (0, 0)),
                  pl.BlockSpec((N_DEV, D_MODEL, 256), lambda i: (0, 0, 0)),
                  ANY_SPEC],
        out_specs=[pl.BlockSpec((tm, 2 * MIX), lambda i: (i, 0)),
                   pl.BlockSpec((tm, D_MODEL), lambda i: (i, 0))],
        out_shape=[jax.ShapeDtypeStruct((n, 2 * MIX), F32),
                   jax.ShapeDtypeStruct((n, D_MODEL), MXU_DTYPE)],
        compiler_params=_params(dimension_semantics=("arbitrary",)),
    )(x2, g_row, w_all, dep)


def _loss_head(x2, tgt2, g_row):
    n = x2.shape[0]
    tm = TM_FWD

    def body(x_ref, t_ref, g_ref, dx_ref, loss_ref, dg_ref):
        @pl.when(pl.program_id(0) == 0)
        def _():
            loss_ref[...] = jnp.zeros_like(loss_ref)
            dg_ref[...] = jnp.zeros_like(dg_ref)

        x = x_ref[...]
        g = g_ref[...]
        r = lax.rsqrt(jnp.mean(x * x, axis=-1, keepdims=True) + NORM_EPS)
        xh = x * r
        e = xh * g - t_ref[...]
        loss_ref[...] += jnp.sum(jnp.sum(e * e, axis=-1, keepdims=True), axis=0, keepdims=True) * (0.5 / D_MODEL)
        dout = e * (1.0 / D_MODEL)
        dg_ref[...] += jnp.sum(dout * xh, axis=0, keepdims=True)
        gdy = dout * g
        dx_ref[...] = r * (gdy - xh * jnp.mean(xh * gdy, axis=-1, keepdims=True))

    return pl.pallas_call(
        body, name="loss_head",
        grid=(n // tm,),
        in_specs=[pl.BlockSpec((tm, D_MODEL), lambda i: (i, 0)),
                  pl.BlockSpec((tm, D_MODEL), lambda i: (i, 0)),
                  pl.BlockSpec((1, D_MODEL), lambda i: (0, 0))],
        out_specs=[pl.BlockSpec((tm, D_MODEL), lambda i: (i, 0)),
                   pl.BlockSpec((1, 1), lambda i: (0, 0)),
                   pl.BlockSpec((1, D_MODEL), lambda i: (0, 0))],
        out_shape=[jax.ShapeDtypeStruct((n, D_MODEL), F32),
                   jax.ShapeDtypeStruct((1, 1), F32),
                   jax.ShapeDtypeStruct((1, D_MODEL), F32)],
        compiler_params=_params(dimension_semantics=("arbitrary",)),
    )(x2, tgt2, g_row)


def _outproj_bwd(dx2, yg, w_out, dep):
    n = dx2.shape[0]
    tm = TM_BWD
    n_steps = n // tm

    def body(dx_ref, y_ref, w_ref, dep_ref, dy_ref, dw_ref, acc_ref):
        i = pl.program_id(0)

        @pl.when(i == 0)
        def _():
            acc_ref[...] = jnp.zeros_like(acc_ref)

        dxb = _mx(dx_ref[...])
        dy_ref[...] = _mm_nt(dxb, w_ref[...])
        acc_ref[...] += _mm_tn(y_ref[...], dxb)

        @pl.when(i == n_steps - 1)
        def _():
            dw_ref[...] = _mx(acc_ref[...])

    return pl.pallas_call(
        body, name="outproj_bwd",
        grid=(n_steps,),
        in_specs=[pl.BlockSpec((tm, D_MODEL), lambda i: (i, 0)),
                  pl.BlockSpec((tm, MIX), lambda i: (i, 0)),
                  pl.BlockSpec((MIX, D_MODEL), lambda i: (0, 0)),
                  ANY_SPEC],
        out_specs=[pl.BlockSpec((tm, MIX), lambda i: (i, 0)),
                   pl.BlockSpec((MIX, D_MODEL), lambda i: (0, 0))],
        out_shape=[jax.ShapeDtypeStruct((n, MIX), F32),
                   jax.ShapeDtypeStruct((MIX, D_MODEL), MXU_DTYPE)],
        scratch_shapes=[pltpu.VMEM((MIX, D_MODEL), F32)],
        compiler_params=_params(dimension_semantics=("arbitrary",)),
    )(dx2, yg, w_out, dep)


def _inproj_bwd(dz, h, x2, dx_in, g_row, w_all, dep):
    n = x2.shape[0]
    tm = TM_BWD
    n_steps = n // tm

    def body(dz_ref, h_ref, x_ref, dxi_ref, g_ref, w_ref, dep_ref, dxo_ref, dw_ref, dg_ref, acc_ref, wcat_ref):
        i = pl.program_id(0)

        @pl.when(i == 0)
        def _():
            acc_ref[...] = jnp.zeros_like(acc_ref)
            dg_ref[...] = jnp.zeros_like(dg_ref)
            for d in range(N_DEV):
                wcat_ref[:, d * 256:(d + 1) * 256] = w_ref[d]

        hb = h_ref[...]
        for d in range(N_DEV):
            acc_ref[d] += _mm_tn(hb, dz_ref[:, d * 256:(d + 1) * 256])
        dh = _mm_nt(dz_ref[...], wcat_ref[...])
        x = x_ref[...]
        r = lax.rsqrt(jnp.mean(x * x, axis=-1, keepdims=True) + NORM_EPS)
        xh = x * r
        dg_ref[...] += jnp.sum(dh * xh, axis=0, keepdims=True)
        gdy = dh * g_ref[...]
        dxo_ref[...] = dxi_ref[...] + r * (gdy - xh * jnp.mean(xh * gdy, axis=-1, keepdims=True))

        @pl.when(i == n_steps - 1)
        def _():
            dw_ref[...] = _mx(acc_ref[...])

    return pl.pallas_call(
        body, name="inproj_bwd",
        grid=(n_steps,),
        in_specs=[pl.BlockSpec((tm, 2 * MIX), lambda i: (i, 0)),
                  pl.BlockSpec((tm, D_MODEL), lambda i: (i, 0)),
                  pl.BlockSpec((tm, D_MODEL), lambda i: (i, 0)),
                  pl.BlockSpec((tm, D_MODEL), lambda i: (i, 0)),
                  pl.BlockSpec((1, D_MODEL), lambda i: (0, 0)),
                  pl.BlockSpec((N_DEV, D_MODEL, 256), lambda i: (0, 0, 0)),
                  ANY_SPEC],
        out_specs=[pl.BlockSpec((tm, D_MODEL), lambda i: (i, 0)),
                   pl.BlockSpec((N_DEV, D_MODEL, 256), lambda i: (0, 0, 0)),
                   pl.BlockSpec((1, D_MODEL), lambda i: (0, 0))],
        out_shape=[jax.ShapeDtypeStruct((n, D_MODEL), F32),
                   jax.ShapeDtypeStruct((N_DEV, D_MODEL, 256), MXU_DTYPE),
                   jax.ShapeDtypeStruct((1, D_MODEL), F32)],
        scratch_shapes=[pltpu.VMEM((N_DEV, D_MODEL, 256), F32),
                        pltpu.VMEM((D_MODEL, 2 * MIX), MXU_DTYPE)],
        compiler_params=_params(dimension_semantics=("arbitrary",)),
    )(dz, h, x2, dx_in, g_row, w_all, dep)


def _row_pos(t0, rows):
    return t0 + lax.broadcasted_iota(jnp.int32, (rows, LANES), 0)


def _pool_window_mean(upad, g, t0, t_blk):
    k = 2 << g
    w = upad
    sh = 1
    while sh < k:
        w = w + pltpu.roll(w, sh, 0)
        sh *= 2
    count = jnp.minimum(_row_pos(t0, t_blk) + 1, k).astype(F32)
    return w[HALO:] / count - upad[HALO:]


def _pool_window_bwd(qpad, g, t_blk):
    k = 2 << g
    n = t_blk + HALO
    w = qpad
    sh = 1
    while sh < k:
        w = w + pltpu.roll(w, n - sh, 0)
        sh *= 2
    return w[:t_blk]


class _StateBuf:
    def __init__(self, refs, t_blk):
        self.refs = refs
        self.t_blk = t_blk

    def put_chunk(self, b, j, val):
        for c in range(4):
            self.refs[4 * b + c][pl.ds(j, self.t_blk, stride=STATE_ROWS), :] = val[:, c * LANES:(c + 1) * LANES]

    def get_chunk(self, b, j):
        return jnp.concatenate(
            [self.refs[4 * b + c][pl.ds(j, self.t_blk, stride=STATE_ROWS), :] for c in range(4)], axis=-1)

    def load(self, b, r, part):
        return jnp.concatenate(
            [self.refs[4 * b + 2 * part + h][pl.ds(r, STATE_ROWS), :] for h in range(2)], axis=-1)

    def store(self, b, r, part, val):
        for h in range(2):
            self.refs[4 * b + 2 * part + h][pl.ds(r, STATE_ROWS), :] = val[:, h * LANES:(h + 1) * LANES]


def _state_scratch(nb, t_blk):
    return [pltpu.VMEM((t_blk * STATE_ROWS, LANES), F32) for _ in range(4 * nb)]


def _ssm_project_in(u_ssm, wb_ref, buf, nb):
    t_blk = u_ssm.shape[0] // nb
    ub = _mx(u_ssm)
    for j in range(STATE_ROWS):
        m = j // 2
        bu = _mm(ub[:, m * LANES:(m + 1) * LANES], wb_ref[j])
        for b in range(nb):
            buf.put_chunk(b, j, bu[b * t_blk:(b + 1) * t_blk])


def _scan_forward(buf, lbr, lbi, init, nb):
    def body(t, carry):
        r = pl.multiple_of(t * STATE_ROWS, STATE_ROWS)
        out = []
        for b in range(nb):
            sr, si = carry[2 * b], carry[2 * b + 1]
            nr = lbr * sr - lbi * si + buf.load(b, r, 0)
            ni = lbr * si + lbi * sr + buf.load(b, r, 1)
            buf.store(b, r, 0, nr)
            buf.store(b, r, 1, ni)
            out += [nr, ni]
        return tuple(out)

    return lax.fori_loop(0, buf.t_blk, body, init, unroll=4)


def _ssm_project_out(chunk, wc_ref):
    tiles = []
    for m in range(4):
        acc = None
        for j in (2 * m, 2 * m + 1):
            part = _mm_nt(chunk(j), wc_ref[j])
            acc = part if acc is None else acc + part
        tiles.append(acc)
    return jnp.concatenate(tiles, axis=-1)


def _mixer_fwd(z3, x3, pool_w, pool_scale, lbr, lbi, wb, wc, d_skip, glu_w, glu_b, w_out, dep):
    nb, seq, _ = z3.shape
    t_blk = min(T_BLK, seq)
    n_t = seq // t_blk
    halo_per_blk = t_blk // HALO
    rows = nb * t_blk

    def body(z_ref, zh_ref, x_ref, pw_ref, ps_ref, lbr_ref, lbi_ref, wb_ref, wc_ref, dsk_ref, gw_ref, gb_ref, wo_ref,
             dep_ref, yg_ref, sc_ref, xo_ref, carry_ref, *s_refs):
        i = pl.program_id(0)
        t0 = i * t_blk
        buf = _StateBuf(s_refs, t_blk)
        both = lambda lo, hi: z_ref[:, :, lo:hi].reshape(rows, hi - lo)

        @pl.when(i == 0)
        def _():
            carry_ref[...] = jnp.zeros_like(carry_ref)

        u_ssm = both(POOL_W, MIX)
        _ssm_project_in(u_ssm, wb_ref, buf, nb)
        init = tuple(carry_ref[b, :, h * STATE_COLS:(h + 1) * STATE_COLS] for b in range(nb) for h in range(2))
        fin = _scan_forward(buf, lbr_ref[...], lbi_ref[...], init, nb)
        for b in range(nb):
            carry_ref[b, :, 0:STATE_COLS] = fin[2 * b]
            carry_ref[b, :, STATE_COLS:2 * STATE_COLS] = fin[2 * b + 1]

        def chunk(j):
            states = _mx(jnp.concatenate([buf.get_chunk(b, j) for b in range(nb)], axis=0))
            sc_ref[:, j] = states.reshape(nb, t_blk, 2 * STATE_COLS)
            return states

        y = _ssm_project_out(chunk, wc_ref) + dsk_ref[...] * u_ssm
        yg, _ = _gelu_and_grad(y)
        o_ssm = yg * _sigmoid(_mm(_mx(yg), gw_ref[...]) + gb_ref[...])
        gp = both(MIX + POOL_W, 2 * MIX)
        parts = []
        first = (i == 0)
        for g in range(N_POOL_G):
            cols = slice(g * POOL_GC, (g + 1) * POOL_GC)
            pooled = []
            for b in range(nb):
                halo = jnp.where(first, 0.0, zh_ref[b, :, cols])
                pooled.append(_pool_window_mean(jnp.concatenate([halo, z_ref[b, :, cols]], axis=0), g, t0, t_blk))
            yp = _mm(_mx(jnp.concatenate(pooled, axis=0)), pw_ref[g]) * ps_ref[:, cols]
            gpp = both(MIX + g * POOL_GC, MIX + (g + 1) * POOL_GC)
            parts.append(_mx(yp * (gpp * _sigmoid(gpp))))
        parts.append(_mx(o_ssm * (gp * _sigmoid(gp))))
        gated = jnp.concatenate(parts, axis=-1)
        yg_ref[...] = gated.reshape(nb, t_blk, MIX)
        xo_ref[...] = x_ref[...] + _mm(gated, wo_ref[...]).reshape(nb, t_blk, D_MODEL)

    const = lambda *shape: pl.BlockSpec(shape, lambda i: (0,) * len(shape))
    tokens = lambda width: pl.BlockSpec((nb, t_blk, width), lambda i: (0, i, 0))
    return pl.pallas_call(
        body, name="mixer_fwd",
        grid=(n_t,),
        in_specs=[tokens(2 * MIX),
                  pl.BlockSpec((nb, HALO, POOL_W), lambda i: (0, jnp.maximum(i * halo_per_blk - 1, 0), 0)),
                  tokens(D_MODEL),
                  const(N_POOL_G, POOL_GC, POOL_GC), const(1, POOL_W),
                  const(STATE_ROWS, STATE_COLS), const(STATE_ROWS, STATE_COLS),
                  const(STATE_ROWS, LANES, 2 * STATE_COLS), const(STATE_ROWS, LANES, 2 * STATE_COLS),
                  const(1, SSM_W), const(SSM_W, SSM_W), const(1, SSM_W), const(MIX, D_MODEL), ANY_SPEC],
        out_specs=[tokens(MIX),
                   pl.BlockSpec((nb, STATE_ROWS, t_blk, 2 * STATE_COLS), lambda i: (0, 0, i, 0)),
                   tokens(D_MODEL)],
        out_shape=[jax.ShapeDtypeStruct((nb, seq, MIX), MXU_DTYPE),
                   jax.ShapeDtypeStruct((nb, STATE_ROWS, seq, 2 * STATE_COLS), MXU_DTYPE),
                   jax.ShapeDtypeStruct((nb, seq, D_MODEL), F32)],
        scratch_shapes=[pltpu.VMEM((nb, STATE_ROWS, 2 * STATE_COLS), F32)] + _state_scratch(nb, t_blk),
        compiler_params=_params(dimension_semantics=("arbitrary",)),
    )(z3, z3, x3, pool_w, pool_scale, lbr, lbi, wb, wc, d_skip, glu_w, glu_b, w_out, dep)


def _mixer_bwd(z3, dy3, states, pool_w, pool_scale, lbr, lbi, wb, wc, d_skip, glu_w, glu_b):
    nb, seq, _ = z3.shape
    t_blk = min(T_BLK, seq)
    n_t = seq // t_blk
    halo_per_blk = t_blk // HALO
    rows = nb * t_blk

    def body(z_ref, zh_ref, dy_ref, sc_ref, sch_ref, pw_ref, ps_ref, lbr_ref, lbi_ref, wb_ref, wc_ref, dsk_ref,
             gw_ref, gb_ref,
             dz_ref, dpw_ref, dps_ref, dlbr_ref, dlbi_ref, dwb_ref, dwc_ref, ddsk_ref, dgw_ref, dgb_ref,
             gcarry_ref, qcarry_ref, du_ref, dgw_acc, *g_refs):
        i = pl.program_id(0)
        blk = n_t - 1 - i
        t0 = blk * t_blk
        gbuf = _StateBuf(g_refs, t_blk)

        @pl.when(i == 0)
        def _():
            gcarry_ref[...] = jnp.zeros_like(gcarry_ref)
            qcarry_ref[...] = jnp.zeros_like(qcarry_ref)
            for ref in (dpw_ref, dps_ref, dlbr_ref, dlbi_ref, dwb_ref, dwc_ref, ddsk_ref, dgw_acc, dgb_ref):
                ref[...] = jnp.zeros_like(ref)

        lbr_v = lbr_ref[...]
        lbi_v = lbi_ref[...]

        both = lambda ref, lo, hi: ref[:, :, lo:hi].reshape(rows, hi - lo)
        split = lambda val: val.reshape(nb, t_blk, val.shape[-1])
        states = lambda j: sc_ref[:, j].reshape(rows, 2 * STATE_COLS)
        first = (blk == 0)

        u_ssm = both(z_ref, POOL_W, MIX)
        y = _ssm_project_out(states, wc_ref) + dsk_ref[...] * u_ssm
        yg, dgelu = _gelu_and_grad(y)
        ygb = _mx(yg)
        sg = _sigmoid(_mm(ygb, gw_ref[...]) + gb_ref[...])
        o_ssm = yg * sg
        gp = both(z_ref, MIX + POOL_W, 2 * MIX)
        sgm = _sigmoid(gp)
        dyv = both(dy_ref, POOL_W, MIX)
        dz_ref[:, :, MIX + POOL_W:2 * MIX] = split(_mx(dyv * o_ssm * (sgm * (1.0 + gp * (1.0 - sgm)))))
        do = dyv * (gp * sgm)
        dv = do * yg * (sg * (1.0 - sg))
        dvb = _mx(dv)
        dgb_ref[...] += jnp.sum(dv, axis=0, keepdims=True)
        dgw_acc[...] += _mm_tn(ygb, dvb)
        dyp = (do * sg + _mm_nt(dvb, gw_ref[...])) * dgelu
        ddsk_ref[...] += jnp.sum(dyp * u_ssm, axis=0, keepdims=True)
        dypb = _mx(dyp)
        for j in range(STATE_ROWS):
            m = j // 2
            dyt = dypb[:, m * LANES:(m + 1) * LANES]
            ds = _mm(dyt, wc_ref[j])
            for b in range(nb):
                gbuf.put_chunk(b, j, ds[b * t_blk:(b + 1) * t_blk])
            dwc_ref[j] += _mm_tn(dyt, states(j))
        du_ref[...] = split(dsk_ref[...] * dyp)

        for g in range(N_POOL_G):
            cols = slice(g * POOL_GC, (g + 1) * POOL_GC)
            pooled = []
            for b in range(nb):
                halo = jnp.where(first, 0.0, zh_ref[b, :, cols])
                pooled.append(_pool_window_mean(jnp.concatenate([halo, z_ref[b, :, cols]], axis=0), g, t0, t_blk))
            pb = _mx(jnp.concatenate(pooled, axis=0))
            ypre = _mm(pb, pw_ref[g])
            gpp = both(z_ref, MIX + g * POOL_GC, MIX + (g + 1) * POOL_GC)
            sgp = _sigmoid(gpp)
            dyg = both(dy_ref, g * POOL_GC, (g + 1) * POOL_GC)
            scale = ps_ref[:, cols]
            dz_ref[:, :, MIX + g * POOL_GC:MIX + (g + 1) * POOL_GC] = split(_mx(
                dyg * (ypre * scale) * (sgp * (1.0 + gpp * (1.0 - sgp)))))
            dyc = dyg * (gpp * sgp)
            dps_ref[:, cols] += jnp.sum(dyc * ypre, axis=0, keepdims=True)
            dypre = _mx(dyc * scale)
            dpw_ref[g] += _mm_tn(pb, dypre)
            dpooled = _mm_nt(dypre, pw_ref[g])
            count = jnp.minimum(_row_pos(t0, t_blk) + 1, 2 << g).astype(F32)
            for b in range(nb):
                dp = dpooled[b * t_blk:(b + 1) * t_blk]
                q = dp / count
                qpad = jnp.concatenate([q, qcarry_ref[b, :, cols]], axis=0)
                qcarry_ref[b, :, cols] = q[:HALO]
                dz_ref[b, :, cols] = _mx(_pool_window_bwd(qpad, g, t_blk) - dp)

        def rev_body(k, carry):
            r = pl.multiple_of((t_blk - 1 - k) * STATE_ROWS, STATE_ROWS)
            out = []
            for b in range(nb):
                gr, gi = carry[2 * b], carry[2 * b + 1]
                ngr = lbr_v * gr + lbi_v * gi + gbuf.load(b, r, 0)
                ngi = lbr_v * gi - lbi_v * gr + gbuf.load(b, r, 1)
                gbuf.store(b, r, 0, ngr)
                gbuf.store(b, r, 1, ngi)
                out += [ngr, ngi]
            return tuple(out)

        init_g = tuple(gcarry_ref[b, :, h * STATE_COLS:(h + 1) * STATE_COLS] for b in range(nb) for h in range(2))
        fin = lax.fori_loop(0, t_blk, rev_body, init_g, unroll=4)
        for b in range(nb):
            gcarry_ref[b, :, 0:STATE_COLS] = fin[2 * b]
            gcarry_ref[b, :, STATE_COLS:2 * STATE_COLS] = fin[2 * b + 1]

        ub = _mx(u_ssm)
        for m in range(4):
            acc = both(du_ref, m * LANES, (m + 1) * LANES)
            for j in (2 * m, 2 * m + 1):
                g = jnp.concatenate([gbuf.get_chunk(b, j) for b in range(nb)], axis=0)
                gj = _mx(g)
                acc = acc + _mm_nt(gj, wb_ref[j])
                dwb_ref[j] += _mm_tn(ub[:, m * LANES:(m + 1) * LANES], gj)
                shifted = []
                for b in range(nb):
                    before = jnp.where(first, 0.0, sch_ref[b, j].astype(F32))
                    spad = jnp.concatenate([before, sc_ref[b, j].astype(F32)], axis=0)
                    shifted.append(pltpu.roll(spad, 1, 0)[HALO:])
                s_prev = jnp.concatenate(shifted, axis=0)
                g_re, g_im = g[:, :STATE_COLS], g[:, STATE_COLS:]
                p_re, p_im = s_prev[:, :STATE_COLS], s_prev[:, STATE_COLS:]
                dlbr_ref[j:j + 1, :] += jnp.sum(g_re * p_re + g_im * p_im, axis=0, keepdims=True)
                dlbi_ref[j:j + 1, :] += jnp.sum(g_im * p_re - g_re * p_im, axis=0, keepdims=True)
            dz_ref[:, :, POOL_W + m * LANES:POOL_W + (m + 1) * LANES] = split(_mx(acc))

        @pl.when(i == n_t - 1)
        def _():
            dgw_ref[...] = _mx(dgw_acc[...])

    const = lambda *shape: pl.BlockSpec(shape, lambda i: (0,) * len(shape))
    rev = lambda i: n_t - 1 - i
    out_shape = [jax.ShapeDtypeStruct((nb, seq, 2 * MIX), MXU_DTYPE),
                 jax.ShapeDtypeStruct((N_POOL_G, POOL_GC, POOL_GC), F32),
                 jax.ShapeDtypeStruct((1, POOL_W), F32),
                 jax.ShapeDtypeStruct((STATE_ROWS, STATE_COLS), F32),
                 jax.ShapeDtypeStruct((STATE_ROWS, STATE_COLS), F32),
                 jax.ShapeDtypeStruct((STATE_ROWS, LANES, 2 * STATE_COLS), F32),
                 jax.ShapeDtypeStruct((STATE_ROWS, LANES, 2 * STATE_COLS), F32),
                 jax.ShapeDtypeStruct((1, SSM_W), F32),
                 jax.ShapeDtypeStruct((SSM_W, SSM_W), MXU_DTYPE),
                 jax.ShapeDtypeStruct((1, SSM_W), F32)]
    return pl.pallas_call(
        body, name="mixer_bwd",
        grid=(n_t,),
        in_specs=[pl.BlockSpec((nb, t_blk, 2 * MIX), lambda i: (0, rev(i), 0)),
                  pl.BlockSpec((nb, HALO, POOL_W), lambda i: (0, jnp.maximum(rev(i) * halo_per_blk - 1, 0), 0)),
                  pl.BlockSpec((nb, t_blk, MIX), lambda i: (0, rev(i), 0)),
                  pl.BlockSpec((nb, STATE_ROWS, t_blk, 2 * STATE_COLS), lambda i: (0, 0, rev(i), 0)),
                  pl.BlockSpec((nb, STATE_ROWS, HALO, 2 * STATE_COLS),
                               lambda i: (0, 0, jnp.maximum(rev(i) * halo_per_blk - 1, 0), 0)),
                  const(N_POOL_G, POOL_GC, POOL_GC), const(1, POOL_W),
                  const(STATE_ROWS, STATE_COLS), const(STATE_ROWS, STATE_COLS),
                  const(STATE_ROWS, LANES, 2 * STATE_COLS), const(STATE_ROWS, LANES, 2 * STATE_COLS),
                  const(1, SSM_W), const(SSM_W, SSM_W), const(1, SSM_W)],
        out_specs=[pl.BlockSpec((nb, t_blk, 2 * MIX), lambda i: (0, rev(i), 0))]
                  + [const(*s.shape) for s in out_shape[1:]],
        out_shape=out_shape,
        scratch_shapes=[pltpu.VMEM((nb, STATE_ROWS, 2 * STATE_COLS), F32),
                        pltpu.VMEM((nb, HALO, POOL_W), F32),
                        pltpu.VMEM((nb, t_blk, SSM_W), F32),
                        pltpu.VMEM((SSM_W, SSM_W), F32)]
                       + _state_scratch(nb, t_blk),
        compiler_params=_params(dimension_semantics=("arbitrary",)),
    )(z3, z3, dy3, states, states, pool_w, pool_scale, lbr, lbi, wb, wc, d_skip, glu_w, glu_b)


def _mesh_place():
    x, y, c = lax.axis_index("x"), lax.axis_index("y"), lax.axis_index("c")
    return x, y, c


def _flip(place, k):
    x, y, c = place
    return (1 - x if k & 4 else x, 1 - y if k & 2 else y, 1 - c if k & 1 else c)


def _index(place):
    x, y, c = place
    return 4 * x + 2 * y + c


HBM_SPEC = pl.BlockSpec(memory_space=pltpu.HBM)
SEM_SPEC = pl.BlockSpec(memory_space=pltpu.SEMAPHORE)
_EFFECT = pltpu.SideEffectType.DATAFLOW_SIDE_EFFECTING
N_PEERS = N_DEV - 1


def _exchange_copies(src_refs, land_refs, send_sems, recv_sems):
    me = _mesh_place()
    mine = _index(me)
    out = []
    for a, land_ref in enumerate(land_refs):
        for k in range(1, N_DEV):
            peer = _flip(me, k)
            theirs = _index(peer)
            n = a * N_PEERS + k - 1
            src = src_refs[a].at[theirs] if src_refs else land_ref.at[mine]
            send = pltpu.make_async_remote_copy(
                src_ref=src, dst_ref=land_ref.at[mine], send_sem=send_sems.at[n], recv_sem=recv_sems.at[n],
                device_id=peer, device_id_type=MESH)
            recv = pltpu.make_async_remote_copy(
                src_ref=src, dst_ref=land_ref.at[theirs], send_sem=send_sems.at[n], recv_sem=recv_sems.at[n],
                device_id=peer, device_id_type=MESH)
            out.append((send, recv))
    return out


def _exchange_start(srcs, lands, after, name):
    arrays = tuple(srcs) + tuple(lands)
    n_src, n_all = len(srcs), len(arrays)
    n_copies = len(lands) * N_PEERS

    def body(*refs):
        send_sems, recv_sems = refs[n_all + 1], refs[n_all + 2]
        token = refs[-1]
        for send, _ in _exchange_copies(refs[:n_src], refs[n_src:n_all], send_sems, recv_sems):
            send.start()
        token[...] = jnp.zeros_like(token)

    res = pl.pallas_call(
        body, name=name,
        in_specs=[HBM_SPEC] * n_all + [ANY_SPEC],
        out_specs=[SEM_SPEC, SEM_SPEC] + [HBM_SPEC] * n_all + [VMEM_SPEC],
        out_shape=[pltpu.SemaphoreType.DMA((n_copies,)), pltpu.SemaphoreType.DMA((n_copies,))]
                  + [pltpu.HBM(a.shape, a.dtype) for a in arrays] + [jax.ShapeDtypeStruct((SUBLANES, LANES), F32)],
        input_output_aliases={i: 2 + i for i in range(n_all)},
        compiler_params=pltpu.CompilerParams(has_side_effects=_EFFECT),
    )(*[pltpu.with_memory_space_constraint(a, pltpu.HBM) for a in arrays], after)
    return tuple(res[:-1]), res[-1]


def _exchange_wait(handle, n_lands, after, name):
    send_sems, recv_sems = handle[0], handle[1]
    arrays = handle[2:]
    n_all = len(arrays)
    n_src = n_all - n_lands

    def body(*refs):
        for send, recv in _exchange_copies(refs[:n_src], refs[n_src:n_all], refs[n_all], refs[n_all + 1]):
            send.wait_send()
            recv.wait_recv()

    res = pl.pallas_call(
        body, name=name,
        in_specs=[HBM_SPEC] * n_all + [SEM_SPEC, SEM_SPEC, ANY_SPEC],
        out_specs=[HBM_SPEC] * n_all,
        out_shape=[pltpu.HBM(a.shape, a.dtype) for a in arrays],
        input_output_aliases={i: i for i in range(n_all)},
        compiler_params=pltpu.CompilerParams(has_side_effects=_EFFECT),
    )(*arrays, send_sems, recv_sems, after)
    return tuple(res[:n_src]), tuple(res[n_src:])


def _weight_zones(w_in, glu_w, w_out, my_idx):
    shards = (w_in, glu_w, w_out)
    depth = w_in.shape[0]

    def body(idx_ref, *refs):
        ins, zones = refs[:len(shards)], refs[len(shards):]
        for l in range(depth):
            for a, src in enumerate(ins):
                zones[l * len(shards) + a][0] = _mx(src[l])

    whole = lambda s: pl.BlockSpec(s.shape, lambda i, idx: (0,) * s.ndim)
    return pl.pallas_call(
        body, name="weight_zones",
        grid_spec=pltpu.PrefetchScalarGridSpec(
            num_scalar_prefetch=1, grid=(1,),
            in_specs=[whole(s) for s in shards],
            out_specs=[pl.BlockSpec((1,) + s.shape[1:], lambda i, idx: (idx[0], 0, 0))
                       for _ in range(depth) for s in shards]),
        out_shape=[jax.ShapeDtypeStruct((N_DEV,) + s.shape[1:], MXU_DTYPE) for _ in range(depth) for s in shards],
        compiler_params=_params(dimension_semantics=("arbitrary",)),
    )(my_idx.reshape(1).astype(jnp.int32), *shards)


def _allreduce_packed(p):
    rows = p.shape[0]
    half = rows // 2
    quarter = half // 4

    def body(p_ref, o_ref, part_ref, sib_ref, got_ref, send_sems, recv_sems):
        x, y, c = _mesh_place()
        sibling = (x, y, 1 - c)
        chip = 2 * x + y
        chips = [(k, (1 - x if k & 2 else x, 1 - y if k & 1 else y, c), chip ^ k) for k in (1, 2, 3)]
        my_half = pl.multiple_of(c * half, SUBLANES)
        other_half = pl.multiple_of((1 - c) * half, SUBLANES)

        def copy(n, src, dst, to):
            return pltpu.make_async_remote_copy(src_ref=src, dst_ref=dst, send_sem=send_sems.at[n],
                                                recv_sem=recv_sems.at[n], device_id=to, device_id_type=MESH)

        def quarter_of(ref, base, q):
            return ref.at[pl.ds(pl.multiple_of(base + q * quarter, SUBLANES), quarter)]

        swap = copy(0, p_ref.at[pl.ds(other_half, half)], sib_ref, sibling)
        swap.start()
        swap.wait()
        part_ref[...] = p_ref[pl.ds(my_half, half), :] + sib_ref[...]

        scatter = [copy(k, quarter_of(part_ref, 0, q), got_ref.at[k - 1], to) for k, to, q in chips]
        for cp in scatter:
            cp.start()
        total = part_ref[pl.ds(pl.multiple_of(chip * quarter, SUBLANES), quarter), :]
        for cp, (k, _, _) in zip(scatter, chips):
            cp.wait()
            total = total + got_ref[k - 1]
        mine = pl.multiple_of(my_half + chip * quarter, SUBLANES)
        o_ref[pl.ds(mine, quarter), :] = total

        gather = [copy(3 + k, o_ref.at[pl.ds(mine, quarter)], o_ref.at[pl.ds(mine, quarter)], to) for k, to, _ in chips]
        for cp in gather:
            cp.start()
        for k, to, q in chips:
            theirs = quarter_of(o_ref, my_half, q)
            copy(3 + k, theirs, theirs, to).wait_recv()
        for cp in gather:
            cp.wait_send()

        back = copy(7, o_ref.at[pl.ds(my_half, half)], o_ref.at[pl.ds(my_half, half)], sibling)
        back.start()
        copy(7, o_ref.at[pl.ds(other_half, half)], o_ref.at[pl.ds(other_half, half)], sibling).wait_recv()
        back.wait_send()

    return pl.pallas_call(
        body, name="comm_allreduce_packed",
        in_specs=[VMEM_SPEC],
        out_specs=VMEM_SPEC,
        out_shape=jax.ShapeDtypeStruct(p.shape, F32),
        scratch_shapes=[pltpu.VMEM((half, LANES), F32),
                        pltpu.VMEM((half, LANES), F32),
                        pltpu.VMEM((3, quarter, LANES), F32),
                        pltpu.SemaphoreType.DMA((8,)),
                        pltpu.SemaphoreType.DMA((8,))],
        compiler_params=_params(),
    )(p)


def _adamw_math(w, g, m, v):
    m = ADAM_B1 * m + (1.0 - ADAM_B1) * g
    v = ADAM_B2 * v + (1.0 - ADAM_B2) * (g * g)
    m_hat = m / (1.0 - ADAM_B1 ** ADAM_STEP)
    v_hat = v / (1.0 - ADAM_B2 ** ADAM_STEP)
    delta = -ADAM_LR * (m_hat / (jnp.sqrt(v_hat) + ADAM_EPS) + ADAM_WD * w)
    return delta, m, v


def _adamw_summed(received, own, my_idx, w, m, v, name):
    depth, r, c = w.shape
    tr = min(r, 128)

    def body(idx_ref, *refs):
        r_refs, o_refs = refs[:depth], refs[depth:2 * depth]
        w_ref, m_ref, v_ref, g_ref, d_ref, nm_ref, nv_ref = refs[2 * depth:]
        me = idx_ref[0]
        for l in range(depth):
            g = jnp.zeros((tr, c), F32)
            for q in range(N_DEV):
                g = g + jnp.where(q == me, o_refs[l][0], r_refs[l][q]).astype(F32)
            g_ref[l] = g
            d_ref[l], nm_ref[l], nv_ref[l] = _adamw_math(w_ref[l], g, m_ref[l], v_ref[l])

    blk = pl.BlockSpec((depth, tr, c), lambda i, idx: (0, i, 0))
    return pl.pallas_call(
        body, name=name,
        grid_spec=pltpu.PrefetchScalarGridSpec(
            num_scalar_prefetch=1, grid=(r // tr,),
            in_specs=[pl.BlockSpec((N_DEV, tr, c), lambda i, idx: (0, i, 0))] * depth
                     + [pl.BlockSpec((1, tr, c), lambda i, idx: (idx[0], i, 0))] * depth
                     + [blk, blk, blk],
            out_specs=[blk] * 4),
        out_shape=[jax.ShapeDtypeStruct((depth, r, c), F32)] * 4,
        compiler_params=_params(dimension_semantics=("arbitrary",)),
    )(my_idx.reshape(1).astype(jnp.int32), *received, *own, w, m, v)


def _adamw_small(ws, gs, ms, vs):
    n = len(ws)
    depth = ws[0].shape[0]
    quarters = 4

    def spec(a):
        per_layer = a.shape[0] == depth
        split = a.ndim >= 3 and a.shape[1] % quarters == 0 and a.shape[1] >= quarters
        block = (1, a.shape[1] // quarters if split else a.shape[1]) + a.shape[2:]
        rest = (0,) * (a.ndim - 2)
        return pl.BlockSpec(block, lambda l, s: ((l if per_layer else 0), (s if split else 0)) + rest)

    def body(*refs):
        w_refs, g_refs, m_refs, v_refs = (refs[k * n:(k + 1) * n] for k in range(4))
        d_refs, nm_refs, nv_refs = (refs[(4 + k) * n:(5 + k) * n] for k in range(3))
        for k in range(n):
            d_refs[k][...], nm_refs[k][...], nv_refs[k][...] = _adamw_math(
                w_refs[k][...], g_refs[k][...], m_refs[k][...], v_refs[k][...])

    specs = [spec(a) for a in ws]
    shapes = [jax.ShapeDtypeStruct(a.shape, F32) for a in ws]
    res = pl.pallas_call(
        body, name="adamw_small",
        grid=(depth, quarters),
        in_specs=specs * 4,
        out_specs=specs * 3,
        out_shape=shapes * 3,
        compiler_params=_params(dimension_semantics=("arbitrary", "arbitrary")),
    )(*ws, *gs, *ms, *vs)
    return res[:n], res[n:2 * n], res[2 * n:]


_PACK_ROWS = SUBLANES * N_DEV


def _pack(arrays):
    flat = jnp.concatenate([a.reshape(-1) for a in arrays])
    per = _PACK_ROWS * LANES
    total = -(-flat.shape[0] // per) * per
    flat = jnp.pad(flat, (0, total - flat.shape[0]))
    return flat.reshape(total // LANES, LANES)


def _unpack(packed, like):
    flat = packed.reshape(-1)
    out = []
    off = 0
    for a in like:
        out.append(flat[off:off + a.size].reshape(a.shape))
        off += a.size
    return out


def kernel(x, norm_g, w_in, pool_w, pool_scale, a_re, a_im, log_dt, b_re, b_im, c_re, c_im, d_skip, glu_w, glu_b, w_out, final_g, loss_target, m_norm_g, m_w_in, m_pool_w, m_pool_scale, m_a_re, m_a_im, m_log_dt, m_b_re, m_b_im, m_c_re, m_c_im, m_d_skip, m_glu_w, m_glu_b, m_w_out, m_final_g, v_norm_g, v_w_in, v_pool_w, v_pool_scale, v_a_re, v_a_im, v_log_dt, v_b_re, v_b_im, v_c_re, v_c_im, v_d_skip, v_glu_w, v_glu_b, v_w_out, v_final_g):
    nb, seq, _ = x.shape
    n_tok = nb * seq
    depth = norm_g.shape[0]

    my_idx = _index(_mesh_place())

    zones = _weight_zones(w_in, glu_w, w_out, my_idx)

    def gather_start(l, after):
        return _exchange_start((), zones[3 * l:3 * l + 3], after, f"comm_gather_start_{l}")

    def gather_wait(handle, after, l):
        _, (win, glu, wout) = _exchange_wait(handle, 3, after, f"comm_gather_wait_{l}")
        return win, glu.reshape(SSM_W, SSM_W), wout.reshape(MIX, D_MODEL)

    xs = [x.reshape(n_tok, D_MODEL)]
    first_w_in, dep = _exchange_start((), zones[0:1], xs[0], "comm_gather_start_0_w_in")

    (lbr, lbi, rb, rc), dense_vjp = jax.vjp(jax.vmap(_ssm_dense), a_re, a_im, log_dt + dep[0, 0], b_re, b_im, c_re, c_im)
    chunk_all = jax.vmap(_ssm_chunked)
    (wb, wct), chunk_vjp = jax.vjp(lambda p, q: (chunk_all(p), chunk_all(q)), rb, rc)
    wb_m, wct_m = _mx(wb), _mx(wct)
    pool_w_m = _mx(pool_w)

    def layer_params(l):
        return (pool_w_m[l], pool_scale[l][None], lbr[l], lbi[l], wb_m[l], wct_m[l], d_skip[l][None],
                weights[l][1], glu_b[l][None])

    saved = []
    weights = []
    for l in range(depth):
        if l == 0:
            _, (win,) = _exchange_wait(first_w_in, 1, wct_m, "comm_gather_wait_0_w_in")
            rest, dep = _exchange_start((), zones[1:3], win, "comm_gather_start_0_rest")
            z, h = _inproj_fwd(xs[-1], norm_g[l][None], win, dep)
            _, (glu, wout) = _exchange_wait(rest, 2, z, "comm_gather_wait_0_rest")
            weights.append((win, glu.reshape(SSM_W, SSM_W), wout.reshape(MIX, D_MODEL)))
            handle, dep = gather_start(1, weights[0][2])
        else:
            weights.append(gather_wait(handle, xs[-1], l))
            if l + 1 < depth:
                handle, dep = gather_start(l + 1, weights[l][0])
            z, h = _inproj_fwd(xs[-1], norm_g[l][None], weights[l][0], dep)
        z3 = z.reshape(nb, seq, 2 * MIX)
        yg, states, x_next = _mixer_fwd(z3, xs[-1].reshape(nb, seq, D_MODEL), *layer_params(l), weights[l][2], dep)
        xs.append(x_next.reshape(n_tok, D_MODEL))
        saved.append((z3, h, yg.reshape(n_tok, MIX), states))

    dx, loss_part, d_final_g = _loss_head(xs[-1], loss_target.reshape(n_tok, D_MODEL), final_g[None])
    loss = lax.psum(loss_part[0, 0], ("x", "y", "c"))

    small = {k: [None] * depth for k in
             ("norm_g", "pool_w", "pool_scale", "lbr", "lbi", "wb", "wct", "d_skip", "glu_b")}
    received = [None] * depth
    sent = [None] * depth
    pending = None
    early = None
    for l in reversed(range(depth)):
        z3, h, yg2, states = saved[l]
        dy, d_wout = _outproj_bwd(dx, yg2, weights[l][2], dep)
        (dz, d_pw, d_ps, d_lbr, d_lbi, d_wb, d_wct, d_dsk, d_gw, d_gb) = _mixer_bwd(
            z3, dy.reshape(nb, seq, MIX), states, *layer_params(l))
        rest = (d_gw.reshape(N_DEV, SSM_W // N_DEV, SSM_W), d_wout.reshape(N_DEV, MIX // N_DEV, D_MODEL))
        if l == 0:
            early, dep = _exchange_start(rest, tuple(lax.empty(s.shape, s.dtype) for s in rest), dz,
                                         "comm_grads_start_0_rest")
        dx, d_win, d_ng = _inproj_bwd(dz.reshape(n_tok, 2 * MIX), h, xs[l], dx, norm_g[l][None], weights[l][0], dep)
        for k, val in (("norm_g", d_ng[0]), ("pool_w", d_pw), ("pool_scale", d_ps[0]), ("lbr", d_lbr),
                       ("lbi", d_lbi), ("wb", d_wb), ("wct", d_wct), ("d_skip", d_dsk[0]), ("glu_b", d_gb[0])):
            small[k][l] = val
        if pending is not None:
            sent[l + 1], received[l + 1] = _exchange_wait(pending, 3, dx, f"comm_grads_wait_{l + 1}")
        srcs = (d_win,) if l == 0 else (d_win,) + rest
        lands = tuple(lax.empty(s.shape, s.dtype) for s in srcs)
        pending, dep = _exchange_start(srcs, lands, dx, f"comm_grads_start_{l}")
    stack = lambda k: jnp.stack(small[k])
    d_rb, d_rc = chunk_vjp((stack("wb"), stack("wct")))
    local = [stack("norm_g"), stack("pool_w"), stack("pool_scale"), stack("lbr"), stack("lbi"), d_rb, d_rc,
             stack("d_skip"), stack("glu_b"), d_final_g[0] + dep[0, 0]]
    (g_norm_g, g_pool_w, g_pool_scale, g_lbr, g_lbi, g_rb, g_rc, g_d_skip, g_glu_b, g_final_g) = _unpack(
        _allreduce_packed(_pack(local)), local)
    g_a_re, g_a_im, g_log_dt, g_b_re, g_b_im, g_c_re, g_c_im = dense_vjp((g_lbr, g_lbi, g_rb, g_rc))

    names = ["norm_g", "pool_w", "pool_scale", "a_re", "a_im", "log_dt", "b_re", "b_im", "c_re", "c_im",
             "d_skip", "glu_b", "final_g"]
    rows = {"norm_g", "pool_scale", "log_dt", "d_skip", "glu_b"}
    small_w = [norm_g, pool_w, pool_scale, a_re, a_im, log_dt, b_re, b_im, c_re, c_im, d_skip, glu_b, final_g]
    small_g = [g_norm_g, g_pool_w, g_pool_scale, g_a_re, g_a_im, g_log_dt, g_b_re, g_b_im, g_c_re, g_c_im,
               g_d_skip, g_glu_b, g_final_g]
    small_m = [m_norm_g, m_pool_w, m_pool_scale, m_a_re, m_a_im, m_log_dt, m_b_re, m_b_im, m_c_re, m_c_im,
               m_d_skip, m_glu_b, m_final_g]
    small_v = [v_norm_g, v_pool_w, v_pool_scale, v_a_re, v_a_im, v_log_dt, v_b_re, v_b_im, v_c_re, v_c_im,
               v_d_skip, v_glu_b, v_final_g]

    wide_last = {"b_re", "b_im"}

    def blocked(arrays):
        return [a.reshape(1, 1, -1) if n == "final_g" else a[:, None, :] if n in rows
                else a.swapaxes(2, 3) if n in wide_last else a for n, a in zip(names, arrays)]

    small_d, small_nm, small_nv = _adamw_small(blocked(small_w), blocked(small_g), blocked(small_m), blocked(small_v))
    res = {}
    for kind, arrays in (("grad", small_g), ("delta", small_d), ("m", small_nm), ("v", small_nv)):
        for n, a, like in zip(names, arrays, small_w):
            if kind != "grad" and n in wide_last:
                a = a.swapaxes(2, 3)
            res[kind, n] = a.reshape(like.shape)

    (s_win,), (r_win,) = _exchange_wait(pending, 1, small_d[0], "comm_grads_wait_0")
    (s_glu, s_wout), (r_glu, r_wout) = _exchange_wait(early, 2, small_d[0], "comm_grads_wait_0_rest")
    sent[0], received[0] = (s_win, s_glu, s_wout), (r_win, r_glu, r_wout)
    shard_res = {}
    for pos, (n, w, m, v) in enumerate((("w_in", w_in, m_w_in, v_w_in), ("glu_w", glu_w, m_glu_w, v_glu_w),
                                        ("w_out", w_out, m_w_out, v_w_out))):
        shard_res[n] = _adamw_summed([received[l][pos] for l in range(depth)], [sent[l][pos] for l in range(depth)],
                                     my_idx, w, m, v, "adamw_" + n)
    for n in ("w_in", "glu_w", "w_out"):
        for pos, kind in enumerate(("grad", "delta", "m", "v")):
            res[kind, n] = shard_res[n][pos]

    order = ["norm_g", "w_in", "pool_w", "pool_scale", "a_re", "a_im", "log_dt", "b_re", "b_im", "c_re", "c_im",
             "d_skip", "glu_w", "glu_b", "w_out", "final_g"]
    outs = [loss, dx.reshape(nb, seq, D_MODEL)]
    for kind in ("grad", "delta", "m", "v"):
        outs += [res[kind, n] for n in order]
    return tuple(outs)
```

```python
import functools
import math

import jax
import jax.numpy as jnp
from jax import lax
from jax.experimental import pallas as pl
from jax.experimental.pallas import tpu as pltpu

F32 = jnp.float32
MXU_DTYPE = jnp.bfloat16

D_MODEL = 1024
MIX = 1024
POOL_W = 512
SSM_W = 512
N_POOL_G = 4
POOL_GC = 128
SSM_G = 32
SSM_C = 16
SSM_P = 64
DEPTH = 4
NORM_EPS = 1e-5
N_DEV = 8

ADAM_LR = 0.001
ADAM_B1 = 0.9
ADAM_B2 = 0.999
ADAM_EPS = 1e-08
ADAM_WD = 0.01
ADAM_STEP = 10

SUBLANES = 8
LANES = 128
HALO = 16
STATE_ROWS = 8
STATE_COLS = 256
T_BLK = 256
TM_FWD = 512
TM_BWD = 512
VMEM_LIMIT = 56 * 1024 * 1024

MESH = pl.DeviceIdType.MESH
VMEM_SPEC = pl.BlockSpec(memory_space=pltpu.VMEM)
ANY_SPEC = pl.BlockSpec(memory_space=pl.ANY)


def _mm(a, b):
    return jnp.dot(a, b, preferred_element_type=F32)


def _mm_tn(a, b):
    return lax.dot_general(a, b, (((0,), (0,)), ((), ())), preferred_element_type=F32)


def _mm_nt(a, b):
    return lax.dot_general(a, b, (((1,), (1,)), ((), ())), preferred_element_type=F32)


def _mx(a):
    return a.astype(MXU_DTYPE)


def _sigmoid(v):
    return 1.0 / (1.0 + jnp.exp(-v))


_GELU_C = math.sqrt(2.0 / math.pi)
_GELU_A = 0.044715


def _gelu_and_grad(y):
    th = jnp.tanh(_GELU_C * (y + _GELU_A * y * y * y))
    val = 0.5 * y * (1.0 + th)
    grad = 0.5 * (1.0 + th) + 0.5 * y * (1.0 - th * th) * (_GELU_C * (1.0 + 3.0 * _GELU_A * y * y))
    return val, grad


def _params(**kw):
    return pltpu.CompilerParams(vmem_limit_bytes=VMEM_LIMIT, **kw)


def _ssm_dense(a_re, a_im, log_dt, b_re, b_im, c_re, c_im):
    dt = jnp.exp(log_dt)[:, None]
    mag = jnp.exp(a_re * dt)
    ang = a_im * dt
    lb_re = mag * jnp.cos(ang)
    lb_im = mag * jnp.sin(ang)
    den = a_re * a_re + a_im * a_im
    n_re = lb_re - 1.0
    n_im = lb_im
    f_re = (n_re * a_re + n_im * a_im) / den
    f_im = (n_im * a_re - n_re * a_im) / den
    bb_re = f_re[..., None] * b_re - f_im[..., None] * b_im
    bb_im = f_re[..., None] * b_im + f_im[..., None] * b_re

    bb = jnp.stack([bb_re, bb_im], axis=0).reshape(2, 8, 4, SSM_P, SSM_C)
    rb = bb.transpose(1, 4, 0, 2, 3).reshape(8, SSM_C, 512)
    cc = jnp.stack([c_re, -c_im], axis=0).reshape(2, 8, 4, SSM_C, SSM_P)
    rc = cc.transpose(1, 3, 0, 2, 4).reshape(8, SSM_C, 512)
    return (lb_re.reshape(STATE_ROWS, STATE_COLS), lb_im.reshape(STATE_ROWS, STATE_COLS), rb, rc)


def _ssm_chunked(per_channel):
    row_group = jnp.arange(64) // SSM_C
    col_group = (jnp.arange(512) // SSM_P) % 4
    own_group = (row_group[:, None] == col_group[None, :]).astype(F32)
    even = (jnp.arange(8) % 2 == 0).astype(F32)[:, None, None]
    half = jnp.tile(per_channel, (1, 4, 1)) * own_group
    return jnp.concatenate([half * even, half * (1.0 - even)], axis=1)


def _inproj_fwd(x2, g_row, w_all, dep):
    n = x2.shape[0]
    tm = TM_FWD

    def body(x_ref, g_ref, w_ref, dep_ref, z_ref, h_ref):
        x = x_ref[...]
        r = lax.rsqrt(jnp.mean(x * x, axis=-1, keepdims=True) + NORM_EPS)
        h = _mx(x * r * g_ref[...])
        h_ref[...] = h
        for d in range(N_DEV):
            z_ref[:, d * 256:(d + 1) * 256] = _mm(h, w_ref[d])

    return pl.pallas_call(
        body, name="inproj_fwd",
        grid=(n // tm,),
        in_specs=[pl.BlockSpec((tm, D_MODEL), lambda i: (i, 0)),
                  pl.BlockSpec((1, D_MODEL), lambda i: (0, 0)),
                  pl.BlockSpec((N_DEV, D_MODEL, 256), lambda i: (0, 0, 0)),
                  ANY_SPEC],
        out_specs=[pl.BlockSpec((tm, 2 * MIX), lambda i: (i, 0)),
                   pl.BlockSpec((tm, D_MODEL), lambda i: (i, 0))],
        out_shape=[jax.ShapeDtypeStruct((n, 2 * MIX), F32),
                   jax.ShapeDtypeStruct((n, D_MODEL), MXU_DTYPE)],
        compiler_params=_params(dimension_semantics=("arbitrary",)),
    )(x2, g_row, w_all, dep)


def _loss_head(x2, tgt2, g_row):
    n = x2.shape[0]
    tm = TM_FWD

    def body(x_ref, t_ref, g_ref, dx_ref, loss_ref, dg_ref):
        @pl.when(pl.program_id(0) == 0)
        def _():
            loss_ref[...] = jnp.zeros_like(loss_ref)
            dg_ref[...] = jnp.zeros_like(dg_ref)

        x = x_ref[...]
        g = g_ref[...]
        r = lax.rsqrt(jnp.mean(x * x, axis=-1, keepdims=True) + NORM_EPS)
        xh = x * r
        e = xh * g - t_ref[...]
        loss_ref[...] += jnp.sum(jnp.sum(e * e, axis=-1, keepdims=True), axis=0, keepdims=True) * (0.5 / D_MODEL)
        dout = e * (1.0 / D_MODEL)
        dg_ref[...] += jnp.sum(dout * xh, axis=0, keepdims=True)
        gdy = dout * g
        dx_ref[...] = r * (gdy - xh * jnp.mean(xh * gdy, axis=-1, keepdims=True))

    return pl.pallas_call(
        body, name="loss_head",
        grid=(n // tm,),
        in_specs=[pl.BlockSpec((tm, D_MODEL), lambda i: (i, 0)),
                  pl.BlockSpec((tm, D_MODEL), lambda i: (i, 0)),
                  pl.BlockSpec((1, D_MODEL), lambda i: (0, 0))],
        out_specs=[pl.BlockSpec((tm, D_MODEL), lambda i: (i, 0)),
                   pl.BlockSpec((1, 1), lambda i: (0, 0)),
                   pl.BlockSpec((1, D_MODEL), lambda i: (0, 0))],
        out_shape=[jax.ShapeDtypeStruct((n, D_MODEL), F32),
                   jax.ShapeDtypeStruct((1, 1), F32),
                   jax.ShapeDtypeStruct((1, D_MODEL), F32)],
        compiler_params=_params(dimension_semantics=("arbitrary",)),
    )(x2, tgt2, g_row)


def _outproj_bwd(dx2, yg, w_out, dep):
    n = dx2.shape[0]
    tm = TM_BWD
    n_steps = n // tm

    def body(dx_ref, y_ref, w_ref, dep_ref, dy_ref, dw_ref, acc_ref):
        i = pl.program_id(0)

        @pl.when(i == 0)
        def _():
            acc_ref[...] = jnp.zeros_like(acc_ref)

        dxb = _mx(dx_ref[...])
        dy_ref[...] = _mm_nt(dxb, w_ref[...])
        acc_ref[...] += _mm_tn(y_ref[...], dxb)

        @pl.when(i == n_steps - 1)
        def _():
            dw_ref[...] = _mx(acc_ref[...])

    return pl.pallas_call(
        body, name="outproj_bwd",
        grid=(n_steps,),
        in_specs=[pl.BlockSpec((tm, D_MODEL), lambda i: (i, 0)),
                  pl.BlockSpec((tm, MIX), lambda i: (i, 0)),
                  pl.BlockSpec((MIX, D_MODEL), lambda i: (0, 0)),
                  ANY_SPEC],
        out_specs=[pl.BlockSpec((tm, MIX), lambda i: (i, 0)),
                   pl.BlockSpec((MIX, D_MODEL), lambda i: (0, 0))],
        out_shape=[jax.ShapeDtypeStruct((n, MIX), F32),
                   jax.ShapeDtypeStruct((MIX, D_MODEL), MXU_DTYPE)],
        scratch_shapes=[pltpu.VMEM((MIX, D_MODEL), F32)],
        compiler_params=_params(dimension_semantics=("arbitrary",)),
    )(dx2, yg, w_out, dep)


def _inproj_bwd(dz, h, x2, dx_in, g_row, w_all, dep):
    n = x2.shape[0]
    tm = TM_BWD
    n_steps = n // tm

    def body(dz_ref, h_ref, x_ref, dxi_ref, g_ref, w_ref, dep_ref, dxo_ref, dw_ref, dg_ref, acc_ref, wcat_ref):
        i = pl.program_id(0)

        @pl.when(i == 0)
        def _():
            acc_ref[...] = jnp.zeros_like(acc_ref)
            dg_ref[...] = jnp.zeros_like(dg_ref)
            for d in range(N_DEV):
                wcat_ref[:, d * 256:(d + 1) * 256] = w_ref[d]

        hb = h_ref[...]
        for d in range(N_DEV):
            acc_ref[d] += _mm_tn(hb, dz_ref[:, d * 256:(d + 1) * 256])
        dh = _mm_nt(dz_ref[...], wcat_ref[...])
        x = x_ref[...]
        r = lax.rsqrt(jnp.mean(x * x, axis=-1, keepdims=True) + NORM_EPS)
        xh = x * r
        dg_ref[...] += jnp.sum(dh * xh, axis=0, keepdims=True)
        gdy = dh * g_ref[...]
        dxo_ref[...] = dxi_ref[...] + r * (gdy - xh * jnp.mean(xh * gdy, axis=-1, keepdims=True))

        @pl.when(i == n_steps - 1)
        def _():
            dw_ref[...] = _mx(acc_ref[...])

    return pl.pallas_call(
        body, name="inproj_bwd",
        grid=(n_steps,),
        in_specs=[pl.BlockSpec((tm, 2 * MIX), lambda i: (i, 0)),
                  pl.BlockSpec((tm, D_MODEL), lambda i: (i, 0)),
                  pl.BlockSpec((tm, D_MODEL), lambda i: (i, 0)),
                  pl.BlockSpec((tm, D_MODEL), lambda i: (i, 0)),
                  pl.BlockSpec((1, D_MODEL), lambda i: (0, 0)),
                  pl.BlockSpec((N_DEV, D_MODEL, 256), lambda i: (0, 0, 0)),
                  ANY_SPEC],
        out_specs=[pl.BlockSpec((tm, D_MODEL), lambda i: (i, 0)),
                   pl.BlockSpec((N_DEV, D_MODEL, 256), lambda i: (0, 0, 0)),
                   pl.BlockSpec((1, D_MODEL), lambda i: (0, 0))],
        out_shape=[jax.ShapeDtypeStruct((n, D_MODEL), F32),
                   jax.ShapeDtypeStruct((N_DEV, D_MODEL, 256), MXU_DTYPE),
                   jax.ShapeDtypeStruct((1, D_MODEL), F32)],
        scratch_shapes=[pltpu.VMEM((N_DEV, D_MODEL, 256), F32),
                        pltpu.VMEM((D_MODEL, 2 * MIX), MXU_DTYPE)],
        compiler_params=_params(dimension_semantics=("arbitrary",)),
    )(dz, h, x2, dx_in, g_row, w_all, dep)


def _row_pos(t0, rows):
    return t0 + lax.broadcasted_iota(jnp.int32, (rows, LANES), 0)


def _pool_window_mean(upad, g, t0, t_blk):
    k = 2 << g
    w = upad
    sh = 1
    while sh < k:
        w = w + pltpu.roll(w, sh, 0)
        sh *= 2
    count = jnp.minimum(_row_pos(t0, t_blk) + 1, k).astype(F32)
    return w[HALO:] / count - upad[HALO:]


def _pool_window_bwd(qpad, g, t_blk):
    k = 2 << g
    n = t_blk + HALO
    w = qpad
    sh = 1
    while sh < k:
        w = w + pltpu.roll(w, n - sh, 0)
        sh *= 2
    return w[:t_blk]


class _StateBuf:
    def __init__(self, refs, t_blk):
        self.refs = refs
        self.t_blk = t_blk

    def put_chunk(self, b, j, val):
        for c in range(4):
            self.refs[4 * b + c][pl.ds(j, self.t_blk, stride=STATE_ROWS), :] = val[:, c * LANES:(c + 1) * LANES]

    def get_chunk(self, b, j):
        return jnp.concatenate(
            [self.refs[4 * b + c][pl.ds(j, self.t_blk, stride=STATE_ROWS), :] for c in range(4)], axis=-1)

    def load(self, b, r, part):
        return jnp.concatenate(
            [self.refs[4 * b + 2 * part + h][pl.ds(r, STATE_ROWS), :] for h in range(2)], axis=-1)

    def store(self, b, r, part, val):
        for h in range(2):
            self.refs[4 * b + 2 * part + h][pl.ds(r, STATE_ROWS), :] = val[:, h * LANES:(h + 1) * LANES]


def _state_scratch(nb, t_blk):
    return [pltpu.VMEM((t_blk * STATE_ROWS, LANES), F32) for _ in range(4 * nb)]


def _ssm_project_in(u_ssm, wb_ref, buf, nb):
    t_blk = u_ssm.shape[0] // nb
    ub = _mx(u_ssm)
    for j in range(STATE_ROWS):
        m = j // 2
        bu = _mm(ub[:, m * LANES:(m + 1) * LANES], wb_ref[j])
        for b in range(nb):
            buf.put_chunk(b, j, bu[b * t_blk:(b + 1) * t_blk])


def _scan_forward(buf, lbr, lbi, init, nb):
    def body(t, carry):
        r = pl.multiple_of(t * STATE_ROWS, STATE_ROWS)
        out = []
        for b in range(nb):
            sr, si = carry[2 * b], carry[2 * b + 1]
            nr = lbr * sr - lbi * si + buf.load(b, r, 0)
            ni = lbr * si + lbi * sr + buf.load(b, r, 1)
            buf.store(b, r, 0, nr)
            buf.store(b, r, 1, ni)
            out += [nr, ni]
        return tuple(out)

    return lax.fori_loop(0, buf.t_blk, body, init, unroll=4)


def _ssm_project_out(chunk, wc_ref):
    tiles = []
    for m in range(4):
        acc = None
        for j in (2 * m, 2 * m + 1):
            part = _mm_nt(chunk(j), wc_ref[j])
            acc = part if acc is None else acc + part
        tiles.append(acc)
    return jnp.concatenate(tiles, axis=-1)


def _layer_fwd(x3, z3, g_row, w_in, pool_w, pool_scale, lbr, lbi, wb, wc, d_skip, glu_w, glu_b, w_out, dep):
    nb, seq, _ = x3.shape
    t_blk = min(T_BLK, seq)
    n_t = seq // t_blk
    halo_per_blk = t_blk // HALO
    rows = nb * t_blk
    fused = z3 is None

    def body(*refs):
        if fused:
            (x_ref, g_ref, wi_ref, pw_ref, ps_ref, lbr_ref, lbi_ref, wb_ref, wc_ref, dsk_ref, gw_ref, gb_ref, wo_ref,
             dep_ref, z_ref, h_ref, yg_ref, sc_ref, xo_ref, carry_ref, halo_ref, *s_refs) = refs
        else:
            (x_ref, z_ref, zh_ref, pw_ref, ps_ref, lbr_ref, lbi_ref, wb_ref, wc_ref, dsk_ref, gw_ref, gb_ref, wo_ref,
             dep_ref, yg_ref, sc_ref, xo_ref, carry_ref, *s_refs) = refs
        i = pl.program_id(0)
        t0 = i * t_blk
        buf = _StateBuf(s_refs, t_blk)
        both = lambda lo, hi: z_ref[:, :, lo:hi].reshape(rows, hi - lo)

        @pl.when(i == 0)
        def _():
            carry_ref[...] = jnp.zeros_like(carry_ref)
            if fused:
                halo_ref[...] = jnp.zeros_like(halo_ref)

        x = x_ref[...].reshape(rows, D_MODEL)
        if fused:
            r = lax.rsqrt(jnp.mean(x * x, axis=-1, keepdims=True) + NORM_EPS)
            h = _mx(x * r * g_ref[...])
            h_ref[...] = h.reshape(nb, t_blk, D_MODEL)
            for d in range(N_DEV):
                z_ref[:, :, d * 256:(d + 1) * 256] = _mm(h, wi_ref[d]).reshape(nb, t_blk, 256)

        u_ssm = both(POOL_W, MIX)
        _ssm_project_in(u_ssm, wb_ref, buf, nb)
        init = tuple(carry_ref[b, :, h * STATE_COLS:(h + 1) * STATE_COLS] for b in range(nb) for h in range(2))
        fin = _scan_forward(buf, lbr_ref[...], lbi_ref[...], init, nb)
        for b in range(nb):
            carry_ref[b, :, 0:STATE_COLS] = fin[2 * b]
            carry_ref[b, :, STATE_COLS:2 * STATE_COLS] = fin[2 * b + 1]

        def chunk(j):
            states = _mx(jnp.concatenate([buf.get_chunk(b, j) for b in range(nb)], axis=0))
            sc_ref[:, j] = states.reshape(nb, t_blk, 2 * STATE_COLS)
            return states

        y = _ssm_project_out(chunk, wc_ref) + dsk_ref[...] * u_ssm
        yg, _ = _gelu_and_grad(y)
        o_ssm = yg * _sigmoid(_mm(_mx(yg), gw_ref[...]) + gb_ref[...])
        gp = both(MIX + POOL_W, 2 * MIX)
        parts = []
        first = (i == 0)
        for g in range(N_POOL_G):
            cols = slice(g * POOL_GC, (g + 1) * POOL_GC)
            pooled = []
            for b in range(nb):
                halo = halo_ref[b, :, cols] if fused else jnp.where(first, 0.0, zh_ref[b, :, cols])
                pooled.append(_pool_window_mean(jnp.concatenate([halo, z_ref[b, :, cols]], axis=0), g, t0, t_blk))
            yp = _mm(_mx(jnp.concatenate(pooled, axis=0)), pw_ref[g]) * ps_ref[:, cols]
            gpp = both(MIX + g * POOL_GC, MIX + (g + 1) * POOL_GC)
            parts.append(_mx(yp * (gpp * _sigmoid(gpp))))
        parts.append(_mx(o_ssm * (gp * _sigmoid(gp))))
        gated = jnp.concatenate(parts, axis=-1)
        yg_ref[...] = gated.reshape(nb, t_blk, MIX)
        xo_ref[...] = (x + _mm(gated, wo_ref[...])).reshape(nb, t_blk, D_MODEL)
        if fused:
            halo_ref[...] = z_ref[:, t_blk - HALO:, 0:POOL_W]

    const = lambda *shape: pl.BlockSpec(shape, lambda i: (0,) * len(shape))
    tokens = lambda width: pl.BlockSpec((nb, t_blk, width), lambda i: (0, i, 0))
    mixer_specs = [const(N_POOL_G, POOL_GC, POOL_GC), const(1, POOL_W),
                   const(STATE_ROWS, STATE_COLS), const(STATE_ROWS, STATE_COLS),
                   const(STATE_ROWS, LANES, 2 * STATE_COLS), const(STATE_ROWS, LANES, 2 * STATE_COLS),
                   const(1, SSM_W), const(SSM_W, SSM_W), const(1, SSM_W), const(MIX, D_MODEL), ANY_SPEC]
    mixer_args = (pool_w, pool_scale, lbr, lbi, wb, wc, d_skip, glu_w, glu_b, w_out, dep)
    out_specs = [tokens(MIX), pl.BlockSpec((nb, STATE_ROWS, t_blk, 2 * STATE_COLS), lambda i: (0, 0, i, 0)),
                 tokens(D_MODEL)]
    out_shape = [jax.ShapeDtypeStruct((nb, seq, MIX), MXU_DTYPE),
                 jax.ShapeDtypeStruct((nb, STATE_ROWS, seq, 2 * STATE_COLS), MXU_DTYPE),
                 jax.ShapeDtypeStruct((nb, seq, D_MODEL), F32)]
    scratch = [pltpu.VMEM((nb, STATE_ROWS, 2 * STATE_COLS), F32)]
    if fused:
        in_specs = [tokens(D_MODEL), const(1, D_MODEL), const(N_DEV, D_MODEL, 256)] + mixer_specs
        args = (x3, g_row, w_in) + mixer_args
        out_specs = [tokens(2 * MIX), tokens(D_MODEL)] + out_specs
        out_shape = [jax.ShapeDtypeStruct((nb, seq, 2 * MIX), F32),
                     jax.ShapeDtypeStruct((nb, seq, D_MODEL), MXU_DTYPE)] + out_shape
        scratch = scratch + [pltpu.VMEM((nb, HALO, POOL_W), F32)]
    else:
        in_specs = [tokens(D_MODEL), tokens(2 * MIX),
                    pl.BlockSpec((nb, HALO, POOL_W), lambda i: (0, jnp.maximum(i * halo_per_blk - 1, 0), 0))] + mixer_specs
        args = (x3, z3, z3) + mixer_args
    return pl.pallas_call(
        body, name="layer_fwd" if fused else "mixer_fwd",
        grid=(n_t,),
        in_specs=in_specs, out_specs=out_specs, out_shape=out_shape,
        scratch_shapes=scratch + _state_scratch(nb, t_blk),
        compiler_params=_params(dimension_semantics=("arbitrary",)),
    )(*args)


def _mixer_bwd(z3, dy3, states, pool_w, pool_scale, lbr, lbi, wb, wc, d_skip, glu_w, glu_b):
    nb, seq, _ = z3.shape
    t_blk = min(T_BLK, seq)
    n_t = seq // t_blk
    halo_per_blk = t_blk // HALO
    rows = nb * t_blk

    def body(z_ref, zh_ref, dy_ref, sc_ref, sch_ref, pw_ref, ps_ref, lbr_ref, lbi_ref, wb_ref, wc_ref, dsk_ref,
             gw_ref, gb_ref,
             dz_ref, dpw_ref, dps_ref, dlbr_ref, dlbi_ref, dwb_ref, dwc_ref, ddsk_ref, dgw_ref, dgb_ref,
             gcarry_ref, qcarry_ref, du_ref, dgw_acc, *g_refs):
        i = pl.program_id(0)
        blk = n_t - 1 - i
        t0 = blk * t_blk
        gbuf = _StateBuf(g_refs, t_blk)

        @pl.when(i == 0)
        def _():
            gcarry_ref[...] = jnp.zeros_like(gcarry_ref)
            qcarry_ref[...] = jnp.zeros_like(qcarry_ref)
            for ref in (dpw_ref, dps_ref, dlbr_ref, dlbi_ref, dwb_ref, dwc_ref, ddsk_ref, dgw_acc, dgb_ref):
                ref[...] = jnp.zeros_like(ref)

        lbr_v = lbr_ref[...]
        lbi_v = lbi_ref[...]

        both = lambda ref, lo, hi: ref[:, :, lo:hi].reshape(rows, hi - lo)
        split = lambda val: val.reshape(nb, t_blk, val.shape[-1])
        states = lambda j: sc_ref[:, j].reshape(rows, 2 * STATE_COLS)
        first = (blk == 0)

        u_ssm = both(z_ref, POOL_W, MIX)
        y = _ssm_project_out(states, wc_ref) + dsk_ref[...] * u_ssm
        yg, dgelu = _gelu_and_grad(y)
        ygb = _mx(yg)
        sg = _sigmoid(_mm(ygb, gw_ref[...]) + gb_ref[...])
        o_ssm = yg * sg
        gp = both(z_ref, MIX + POOL_W, 2 * MIX)
        sgm = _sigmoid(gp)
        dyv = both(dy_ref, POOL_W, MIX)
        dz_ref[:, :, MIX + POOL_W:2 * MIX] = split(_mx(dyv * o_ssm * (sgm * (1.0 + gp * (1.0 - sgm)))))
        do = dyv * (gp * sgm)
        dv = do * yg * (sg * (1.0 - sg))
        dvb = _mx(dv)
        dgb_ref[...] += jnp.sum(dv, axis=0, keepdims=True)
        dgw_acc[...] += _mm_tn(ygb, dvb)
        dyp = (do * sg + _mm_nt(dvb, gw_ref[...])) * dgelu
        ddsk_ref[...] += jnp.sum(dyp * u_ssm, axis=0, keepdims=True)
        dypb = _mx(dyp)
        for j in range(STATE_ROWS):
            m = j // 2
            dyt = dypb[:, m * LANES:(m + 1) * LANES]
            ds = _mm(dyt, wc_ref[j])
            for b in range(nb):
                gbuf.put_chunk(b, j, ds[b * t_blk:(b + 1) * t_blk])
            dwc_ref[j] += _mm_tn(dyt, states(j))
        du_ref[...] = split(dsk_ref[...] * dyp)

        for g in range(N_POOL_G):
            cols = slice(g * POOL_GC, (g + 1) * POOL_GC)
            pooled = []
            for b in range(nb):
                halo = jnp.where(first, 0.0, zh_ref[b, :, cols])
                pooled.append(_pool_window_mean(jnp.concatenate([halo, z_ref[b, :, cols]], axis=0), g, t0, t_blk))
            pb = _mx(jnp.concatenate(pooled, axis=0))
            ypre = _mm(pb, pw_ref[g])
            gpp = both(z_ref, MIX + g * POOL_GC, MIX + (g + 1) * POOL_GC)
            sgp = _sigmoid(gpp)
            dyg = both(dy_ref, g * POOL_GC, (g + 1) * POOL_GC)
            scale = ps_ref[:, cols]
            dz_ref[:, :, MIX + g * POOL_GC:MIX + (g + 1) * POOL_GC] = split(_mx(
                dyg * (ypre * scale) * (sgp * (1.0 + gpp * (1.0 - sgp)))))
            dyc = dyg * (gpp * sgp)
            dps_ref[:, cols] += jnp.sum(dyc * ypre, axis=0, keepdims=True)
            dypre = _mx(dyc * scale)
            dpw_ref[g] += _mm_tn(pb, dypre)
            dpooled = _mm_nt(dypre, pw_ref[g])
            count = jnp.minimum(_row_pos(t0, t_blk) + 1, 2 << g).astype(F32)
            for b in range(nb):
                dp = dpooled[b * t_blk:(b + 1) * t_blk]
                q = dp / count
                qpad = jnp.concatenate([q, qcarry_ref[b, :, cols]], axis=0)
                qcarry_ref[b, :, cols] = q[:HALO]
                dz_ref[b, :, cols] = _mx(_pool_window_bwd(qpad, g, t_blk) - dp)

        def rev_body(k, carry):
            r = pl.multiple_of((t_blk - 1 - k) * STATE_ROWS, STATE_ROWS)
            out = []
            for b in range(nb):
                gr, gi = carry[2 * b], carry[2 * b + 1]
                ngr = lbr_v * gr + lbi_v * gi + gbuf.load(b, r, 0)
                ngi = lbr_v * gi - lbi_v * gr + gbuf.load(b, r, 1)
                gbuf.store(b, r, 0, ngr)
                gbuf.store(b, r, 1, ngi)
                out += [ngr, ngi]
            return tuple(out)

        init_g = tuple(gcarry_ref[b, :, h * STATE_COLS:(h + 1) * STATE_COLS] for b in range(nb) for h in range(2))
        fin = lax.fori_loop(0, t_blk, rev_body, init_g, unroll=4)
        for b in range(nb):
            gcarry_ref[b, :, 0:STATE_COLS] = fin[2 * b]
            gcarry_ref[b, :, STATE_COLS:2 * STATE_COLS] = fin[2 * b + 1]

        ub = _mx(u_ssm)
        for m in range(4):
            acc = both(du_ref, m * LANES, (m + 1) * LANES)
            for j in (2 * m, 2 * m + 1):
                g = jnp.concatenate([gbuf.get_chunk(b, j) for b in range(nb)], axis=0)
                gj = _mx(g)
                acc = acc + _mm_nt(gj, wb_ref[j])
                dwb_ref[j] += _mm_tn(ub[:, m * LANES:(m + 1) * LANES], gj)
                shifted = []
                for b in range(nb):
                    before = jnp.where(first, 0.0, sch_ref[b, j].astype(F32))
                    spad = jnp.concatenate([before, sc_ref[b, j].astype(F32)], axis=0)
                    shifted.append(pltpu.roll(spad, 1, 0)[HALO:])
                s_prev = jnp.concatenate(shifted, axis=0)
                g_re, g_im = g[:, :STATE_COLS], g[:, STATE_COLS:]
                p_re, p_im = s_prev[:, :STATE_COLS], s_prev[:, STATE_COLS:]
                dlbr_ref[j:j + 1, :] += jnp.sum(g_re * p_re + g_im * p_im, axis=0, keepdims=True)
                dlbi_ref[j:j + 1, :] += jnp.sum(g_im * p_re - g_re * p_im, axis=0, keepdims=True)
            dz_ref[:, :, POOL_W + m * LANES:POOL_W + (m + 1) * LANES] = split(_mx(acc))

        @pl.when(i == n_t - 1)
        def _():
            dgw_ref[...] = _mx(dgw_acc[...])

    const = lambda *shape: pl.BlockSpec(shape, lambda i: (0,) * len(shape))
    rev = lambda i: n_t - 1 - i
    out_shape = [jax.ShapeDtypeStruct((nb, seq, 2 * MIX), MXU_DTYPE),
                 jax.ShapeDtypeStruct((N_POOL_G, POOL_GC, POOL_GC), F32),
                 jax.ShapeDtypeStruct((1, POOL_W), F32),
                 jax.ShapeDtypeStruct((STATE_ROWS, STATE_COLS), F32),
                 jax.ShapeDtypeStruct((STATE_ROWS, STATE_COLS), F32),
                 jax.ShapeDtypeStruct((STATE_ROWS, LANES, 2 * STATE_COLS), F32),
                 jax.ShapeDtypeStruct((STATE_ROWS, LANES, 2 * STATE_COLS), F32),
                 jax.ShapeDtypeStruct((1, SSM_W), F32),
                 jax.ShapeDtypeStruct((SSM_W, SSM_W), MXU_DTYPE),
                 jax.ShapeDtypeStruct((1, SSM_W), F32)]
    return pl.pallas_call(
        body, name="mixer_bwd",
        grid=(n_t,),
        in_specs=[pl.BlockSpec((nb, t_blk, 2 * MIX), lambda i: (0, rev(i), 0)),
                  pl.BlockSpec((nb, HALO, POOL_W), lambda i: (0, jnp.maximum(rev(i) * halo_per_blk - 1, 0), 0)),
                  pl.BlockSpec((nb, t_blk, MIX), lambda i: (0, rev(i), 0)),
                  pl.BlockSpec((nb, STATE_ROWS, t_blk, 2 * STATE_COLS), lambda i: (0, 0, rev(i), 0)),
                  pl.BlockSpec((nb, STATE_ROWS, HALO, 2 * STATE_COLS),
                               lambda i: (0, 0, jnp.maximum(rev(i) * halo_per_blk - 1, 0), 0)),
                  const(N_POOL_G, POOL_GC, POOL_GC), const(1, POOL_W),
                  const(STATE_ROWS, STATE_COLS), const(STATE_ROWS, STATE_COLS),
                  const(STATE_ROWS, LANES, 2 * STATE_COLS), const(STATE_ROWS, LANES, 2 * STATE_COLS),
                  const(1, SSM_W), const(SSM_W, SSM_W), const(1, SSM_W)],
        out_specs=[pl.BlockSpec((nb, t_blk, 2 * MIX), lambda i: (0, rev(i), 0))]
                  + [const(*s.shape) for s in out_shape[1:]],
        out_shape=out_shape,
        scratch_shapes=[pltpu.VMEM((nb, STATE_ROWS, 2 * STATE_COLS), F32),
                        pltpu.VMEM((nb, HALO, POOL_W), F32),
                        pltpu.VMEM((nb, t_blk, SSM_W), F32),
                        pltpu.VMEM((SSM_W, SSM_W), F32)]
                       + _state_scratch(nb, t_blk),
        compiler_params=_params(dimension_semantics=("arbitrary",)),
    )(z3, z3, dy3, states, states, pool_w, pool_scale, lbr, lbi, wb, wc, d_skip, glu_w, glu_b)


def _mesh_place():
    x, y, c = lax.axis_index("x"), lax.axis_index("y"), lax.axis_index("c")
    return x, y, c


def _flip(place, k):
    x, y, c = place
    return (1 - x if k & 4 else x, 1 - y if k & 2 else y, 1 - c if k & 1 else c)


def _index(place):
    x, y, c = place
    return 4 * x + 2 * y + c


HBM_SPEC = pl.BlockSpec(memory_space=pltpu.HBM)
SEM_SPEC = pl.BlockSpec(memory_space=pltpu.SEMAPHORE)
_EFFECT = pltpu.SideEffectType.DATAFLOW_SIDE_EFFECTING
N_PEERS = N_DEV - 1


def _exchange_copies(src_refs, land_refs, send_sems, recv_sems):
    me = _mesh_place()
    mine = _index(me)
    out = []
    for a, land_ref in enumerate(land_refs):
        for k in range(1, N_DEV):
            peer = _flip(me, k)
            theirs = _index(peer)
            n = a * N_PEERS + k - 1
            src = src_refs[a].at[theirs] if src_refs else land_ref.at[mine]
            send = pltpu.make_async_remote_copy(
                src_ref=src, dst_ref=land_ref.at[mine], send_sem=send_sems.at[n], recv_sem=recv_sems.at[n],
                device_id=peer, device_id_type=MESH)
            recv = pltpu.make_async_remote_copy(
                src_ref=src, dst_ref=land_ref.at[theirs], send_sem=send_sems.at[n], recv_sem=recv_sems.at[n],
                device_id=peer, device_id_type=MESH)
            out.append((send, recv))
    return out


def _exchange_start(srcs, lands, after, name):
    arrays = tuple(srcs) + tuple(lands)
    n_src, n_all = len(srcs), len(arrays)
    n_copies = len(lands) * N_PEERS

    def body(*refs):
        send_sems, recv_sems = refs[n_all + 1], refs[n_all + 2]
        token = refs[-1]
        for send, _ in _exchange_copies(refs[:n_src], refs[n_src:n_all], send_sems, recv_sems):
            send.start()
        token[...] = jnp.zeros_like(token)

    res = pl.pallas_call(
        body, name=name,
        in_specs=[HBM_SPEC] * n_all + [ANY_SPEC],
        out_specs=[SEM_SPEC, SEM_SPEC] + [HBM_SPEC] * n_all + [VMEM_SPEC],
        out_shape=[pltpu.SemaphoreType.DMA((n_copies,)), pltpu.SemaphoreType.DMA((n_copies,))]
                  + [pltpu.HBM(a.shape, a.dtype) for a in arrays] + [jax.ShapeDtypeStruct((SUBLANES, LANES), F32)],
        input_output_aliases={i: 2 + i for i in range(n_all)},
        compiler_params=pltpu.CompilerParams(has_side_effects=_EFFECT),
    )(*[pltpu.with_memory_space_constraint(a, pltpu.HBM) for a in arrays], after)
    return tuple(res[:-1]), res[-1]


def _exchange_wait(handle, n_lands, after, name):
    send_sems, recv_sems = handle[0], handle[1]
    arrays = handle[2:]
    n_all = len(arrays)
    n_src = n_all - n_lands

    def body(*refs):
        for send, recv in _exchange_copies(refs[:n_src], refs[n_src:n_all], refs[n_all], refs[n_all + 1]):
            send.wait_send()
            recv.wait_recv()

    res = pl.pallas_call(
        body, name=name,
        in_specs=[HBM_SPEC] * n_all + [SEM_SPEC, SEM_SPEC, ANY_SPEC],
        out_specs=[HBM_SPEC] * n_all,
        out_shape=[pltpu.HBM(a.shape, a.dtype) for a in arrays],
        input_output_aliases={i: i for i in range(n_all)},
        compiler_params=pltpu.CompilerParams(has_side_effects=_EFFECT),
    )(*arrays, send_sems, recv_sems, after)
    return tuple(res[:n_src]), tuple(res[n_src:])


def _weight_zones(w_in, glu_w, w_out, my_idx):
    shards = (w_in, glu_w, w_out)
    depth = w_in.shape[0]

    def body(idx_ref, *refs):
        ins, zones = refs[:len(shards)], refs[len(shards):]
        for l in range(depth):
            for a, src in enumerate(ins):
                zones[l * len(shards) + a][0] = _mx(src[l])

    whole = lambda s: pl.BlockSpec(s.shape, lambda i, idx: (0,) * s.ndim)
    return pl.pallas_call(
        body, name="weight_zones",
        grid_spec=pltpu.PrefetchScalarGridSpec(
            num_scalar_prefetch=1, grid=(1,),
            in_specs=[whole(s) for s in shards],
            out_specs=[pl.BlockSpec((1,) + s.shape[1:], lambda i, idx: (idx[0], 0, 0))
                       for _ in range(depth) for s in shards]),
        out_shape=[jax.ShapeDtypeStruct((N_DEV,) + s.shape[1:], MXU_DTYPE) for _ in range(depth) for s in shards],
        compiler_params=_params(dimension_semantics=("arbitrary",)),
    )(my_idx.reshape(1).astype(jnp.int32), *shards)


def _allreduce_packed(p):
    rows = p.shape[0]
    half = rows // 2
    quarter = half // 4

    def body(p_ref, o_ref, part_ref, sib_ref, got_ref, send_sems, recv_sems):
        x, y, c = _mesh_place()
        sibling = (x, y, 1 - c)
        chip = 2 * x + y
        chips = [(k, (1 - x if k & 2 else x, 1 - y if k & 1 else y, c), chip ^ k) for k in (1, 2, 3)]
        my_half = pl.multiple_of(c * half, SUBLANES)
        other_half = pl.multiple_of((1 - c) * half, SUBLANES)

        def copy(n, src, dst, to):
            return pltpu.make_async_remote_copy(src_ref=src, dst_ref=dst, send_sem=send_sems.at[n],
                                                recv_sem=recv_sems.at[n], device_id=to, device_id_type=MESH)

        def quarter_of(ref, base, q):
            return ref.at[pl.ds(pl.multiple_of(base + q * quarter, SUBLANES), quarter)]

        swap = copy(0, p_ref.at[pl.ds(other_half, half)], sib_ref, sibling)
        swap.start()
        swap.wait()
        part_ref[...] = p_ref[pl.ds(my_half, half), :] + sib_ref[...]

        scatter = [copy(k, quarter_of(part_ref, 0, q), got_ref.at[k - 1], to) for k, to, q in chips]
        for cp in scatter:
            cp.start()
        total = part_ref[pl.ds(pl.multiple_of(chip * quarter, SUBLANES), quarter), :]
        for cp, (k, _, _) in zip(scatter, chips):
            cp.wait()
            total = total + got_ref[k - 1]
        mine = pl.multiple_of(my_half + chip * quarter, SUBLANES)
        o_ref[pl.ds(mine, quarter), :] = total

        gather = [copy(3 + k, o_ref.at[pl.ds(mine, quarter)], o_ref.at[pl.ds(mine, quarter)], to) for k, to, _ in chips]
        for cp in gather:
            cp.start()
        for k, to, q in chips:
            theirs = quarter_of(o_ref, my_half, q)
            copy(3 + k, theirs, theirs, to).wait_recv()
        for cp in gather:
            cp.wait_send()

        back = copy(7, o_ref.at[pl.ds(my_half, half)], o_ref.at[pl.ds(my_half, half)], sibling)
        back.start()
        copy(7, o_ref.at[pl.ds(other_half, half)], o_ref.at[pl.ds(other_half, half)], sibling).wait_recv()
        back.wait_send()

    return pl.pallas_call(
        body, name="comm_allreduce_packed",
        in_specs=[VMEM_SPEC],
        out_specs=VMEM_SPEC,
        out_shape=jax.ShapeDtypeStruct(p.shape, F32),
        scratch_shapes=[pltpu.VMEM((half, LANES), F32),
                        pltpu.VMEM((half, LANES), F32),
                        pltpu.VMEM((3, quarter, LANES), F32),
                        pltpu.SemaphoreType.DMA((8,)),
                        pltpu.SemaphoreType.DMA((8,))],
        compiler_params=_params(),
    )(p)


def _adamw_math(w, g, m, v):
    m = ADAM_B1 * m + (1.0 - ADAM_B1) * g
    v = ADAM_B2 * v + (1.0 - ADAM_B2) * (g * g)
    m_hat = m / (1.0 - ADAM_B1 ** ADAM_STEP)
    v_hat = v / (1.0 - ADAM_B2 ** ADAM_STEP)
    delta = -ADAM_LR * (m_hat / (jnp.sqrt(v_hat) + ADAM_EPS) + ADAM_WD * w)
    return delta, m, v


def _adamw_summed(received, own, my_idx, w, m, v, name):
    depth, r, c = w.shape
    tr = min(r, 128)

    def body(idx_ref, *refs):
        r_refs, o_refs = refs[:depth], refs[depth:2 * depth]
        w_ref, m_ref, v_ref, g_ref, d_ref, nm_ref, nv_ref = refs[2 * depth:]
        me = idx_ref[0]
        for l in range(depth):
            g = jnp.zeros((tr, c), F32)
            for q in range(N_DEV):
                g = g + jnp.where(q == me, o_refs[l][0], r_refs[l][q]).astype(F32)
            g_ref[l] = g
            d_ref[l], nm_ref[l], nv_ref[l] = _adamw_math(w_ref[l], g, m_ref[l], v_ref[l])

    blk = pl.BlockSpec((depth, tr, c), lambda i, idx: (0, i, 0))
    return pl.pallas_call(
        body, name=name,
        grid_spec=pltpu.PrefetchScalarGridSpec(
            num_scalar_prefetch=1, grid=(r // tr,),
            in_specs=[pl.BlockSpec((N_DEV, tr, c), lambda i, idx: (0, i, 0))] * depth
                     + [pl.BlockSpec((1, tr, c), lambda i, idx: (idx[0], i, 0))] * depth
                     + [blk, blk, blk],
            out_specs=[blk] * 4),
        out_shape=[jax.ShapeDtypeStruct((depth, r, c), F32)] * 4,
        compiler_params=_params(dimension_semantics=("arbitrary",)),
    )(my_idx.reshape(1).astype(jnp.int32), *received, *own, w, m, v)


def _adamw_small(ws, gs, ms, vs):
    n = len(ws)
    depth = ws[0].shape[0]
    quarters = 4

    def spec(a):
        per_layer = a.shape[0] == depth
        split = a.ndim >= 3 and a.shape[1] % quarters == 0 and a.shape[1] >= quarters
        block = (1, a.shape[1] // quarters if split else a.shape[1]) + a.shape[2:]
        rest = (0,) * (a.ndim - 2)
        return pl.BlockSpec(block, lambda l, s: ((l if per_layer else 0), (s if split else 0)) + rest)

    def body(*refs):
        w_refs, g_refs, m_refs, v_refs = (refs[k * n:(k + 1) * n] for k in range(4))
        d_refs, nm_refs, nv_refs = (refs[(4 + k) * n:(5 + k) * n] for k in range(3))
        for k in range(n):
            d_refs[k][...], nm_refs[k][...], nv_refs[k][...] = _adamw_math(
                w_refs[k][...], g_refs[k][...], m_refs[k][...], v_refs[k][...])

    specs = [spec(a) for a in ws]
    shapes = [jax.ShapeDtypeStruct(a.shape, F32) for a in ws]
    res = pl.pallas_call(
        body, name="adamw_small",
        grid=(depth, quarters),
        in_specs=specs * 4,
        out_specs=specs * 3,
        out_shape=shapes * 3,
        compiler_params=_params(dimension_semantics=("arbitrary", "arbitrary")),
    )(*ws, *gs, *ms, *vs)
    return res[:n], res[n:2 * n], res[2 * n:]


_PACK_ROWS = SUBLANES * N_DEV


def _pack(arrays):
    flat = jnp.concatenate([a.reshape(-1) for a in arrays])
    per = _PACK_ROWS * LANES
    total = -(-flat.shape[0] // per) * per
    flat = jnp.pad(flat, (0, total - flat.shape[0]))
    return flat.reshape(total // LANES, LANES)


def _unpack(packed, like):
    flat = packed.reshape(-1)
    out = []
    off = 0
    for a in like:
        out.append(flat[off:off + a.size].reshape(a.shape))
        off += a.size
    return out


def kernel(x, norm_g, w_in, pool_w, pool_scale, a_re, a_im, log_dt, b_re, b_im, c_re, c_im, d_skip, glu_w, glu_b, w_out, final_g, loss_target, m_norm_g, m_w_in, m_pool_w, m_pool_scale, m_a_re, m_a_im, m_log_dt, m_b_re, m_b_im, m_c_re, m_c_im, m_d_skip, m_glu_w, m_glu_b, m_w_out, m_final_g, v_norm_g, v_w_in, v_pool_w, v_pool_scale, v_a_re, v_a_im, v_log_dt, v_b_re, v_b_im, v_c_re, v_c_im, v_d_skip, v_glu_w, v_glu_b, v_w_out, v_final_g):
    nb, seq, _ = x.shape
    n_tok = nb * seq
    depth = norm_g.shape[0]

    my_idx = _index(_mesh_place())

    zones = _weight_zones(w_in, glu_w, w_out, my_idx)

    def gather_start(l, after):
        return _exchange_start((), zones[3 * l:3 * l + 3], after, f"comm_gather_start_{l}")

    def gather_wait(handle, after, l):
        _, (win, glu, wout) = _exchange_wait(handle, 3, after, f"comm_gather_wait_{l}")
        return win, glu.reshape(SSM_W, SSM_W), wout.reshape(MIX, D_MODEL)

    xs = [x.reshape(n_tok, D_MODEL)]
    first_w_in, dep = _exchange_start((), zones[0:1], xs[0], "comm_gather_start_0_w_in")

    (lbr, lbi, rb, rc), dense_vjp = jax.vjp(jax.vmap(_ssm_dense), a_re, a_im, log_dt + dep[0, 0], b_re, b_im, c_re, c_im)
    chunk_all = jax.vmap(_ssm_chunked)
    (wb, wct), chunk_vjp = jax.vjp(lambda p, q: (chunk_all(p), chunk_all(q)), rb, rc)
    wb_m, wct_m = _mx(wb), _mx(wct)
    pool_w_m = _mx(pool_w)

    def layer_params(l):
        return (pool_w_m[l], pool_scale[l][None], lbr[l], lbi[l], wb_m[l], wct_m[l], d_skip[l][None],
                weights[l][1], glu_b[l][None])

    saved = []
    weights = []
    for l in range(depth):
        if l == 0:
            _, (win,) = _exchange_wait(first_w_in, 1, wct_m, "comm_gather_wait_0_w_in")
            rest, dep = _exchange_start((), zones[1:3], win, "comm_gather_start_0_rest")
            z, h = _inproj_fwd(xs[-1], norm_g[l][None], win, dep)
            _, (glu, wout) = _exchange_wait(rest, 2, z, "comm_gather_wait_0_rest")
            weights.append((win, glu.reshape(SSM_W, SSM_W), wout.reshape(MIX, D_MODEL)))
            handle, dep = gather_start(1, weights[0][2])
            z3 = z.reshape(nb, seq, 2 * MIX)
            yg, states, x_next = _layer_fwd(xs[-1].reshape(nb, seq, D_MODEL), z3, None, None, *layer_params(l),
                                            weights[l][2], dep)
        else:
            weights.append(gather_wait(handle, xs[-1], l))
            if l + 1 < depth:
                handle, dep = gather_start(l + 1, weights[l][0])
            z3, h3, yg, states, x_next = _layer_fwd(xs[-1].reshape(nb, seq, D_MODEL), None, norm_g[l][None],
                                                    weights[l][0], *layer_params(l), weights[l][2], dep)
            h = h3.reshape(n_tok, D_MODEL)
        xs.append(x_next.reshape(n_tok, D_MODEL))
        saved.append((z3, h, yg.reshape(n_tok, MIX), states))

    dx, loss_part, d_final_g = _loss_head(xs[-1], loss_target.reshape(n_tok, D_MODEL), final_g[None])
    loss = lax.psum(loss_part[0, 0], ("x", "y", "c"))

    small = {k: [None] * depth for k in
             ("norm_g", "pool_w", "pool_scale", "lbr", "lbi", "wb", "wct", "d_skip", "glu_b")}
    received = [None] * depth
    sent = [None] * depth
    pending = None
    early = None
    for l in reversed(range(depth)):
        z3, h, yg2, states = saved[l]
        dy, d_wout = _outproj_bwd(dx, yg2, weights[l][2], dep)
        (dz, d_pw, d_ps, d_lbr, d_lbi, d_wb, d_wct, d_dsk, d_gw, d_gb) = _mixer_bwd(
            z3, dy.reshape(nb, seq, MIX), states, *layer_params(l))
        rest = (d_gw.reshape(N_DEV, SSM_W // N_DEV, SSM_W), d_wout.reshape(N_DEV, MIX // N_DEV, D_MODEL))
        if l == 0:
            early, dep = _exchange_start(rest, tuple(lax.empty(s.shape, s.dtype) for s in rest), dz,
                                         "comm_grads_start_0_rest")
        dx, d_win, d_ng = _inproj_bwd(dz.reshape(n_tok, 2 * MIX), h, xs[l], dx, norm_g[l][None], weights[l][0], dep)
        for k, val in (("norm_g", d_ng[0]), ("pool_w", d_pw), ("pool_scale", d_ps[0]), ("lbr", d_lbr),
                       ("lbi", d_lbi), ("wb", d_wb), ("wct", d_wct), ("d_skip", d_dsk[0]), ("glu_b", d_gb[0])):
            small[k][l] = val
        if pending is not None:
            sent[l + 1], received[l + 1] = _exchange_wait(pending, 3, dx, f"comm_grads_wait_{l + 1}")
        srcs = (d_win,) if l == 0 else (d_win,) + rest
        lands = tuple(lax.empty(s.shape, s.dtype) for s in srcs)
        pending, dep = _exchange_start(srcs, lands, dx, f"comm_grads_start_{l}")
    stack = lambda k: jnp.stack(small[k])
    d_rb, d_rc = chunk_vjp((stack("wb"), stack("wct")))
    local = [stack("norm_g"), stack("pool_w"), stack("pool_scale"), stack("lbr"), stack("lbi"), d_rb, d_rc,
             stack("d_skip"), stack("glu_b"), d_final_g[0] + dep[0, 0]]
    (g_norm_g, g_pool_w, g_pool_scale, g_lbr, g_lbi, g_rb, g_rc, g_d_skip, g_glu_b, g_final_g) = _unpack(
        _allreduce_packed(_pack(local)), local)
    g_a_re, g_a_im, g_log_dt, g_b_re, g_b_im, g_c_re, g_c_im = dense_vjp((g_lbr, g_lbi, g_rb, g_rc))

    names = ["norm_g", "pool_w", "pool_scale", "a_re", "a_im", "log_dt", "b_re", "b_im", "c_re", "c_im",
             "d_skip", "glu_b", "final_g"]
    rows = {"norm_g", "pool_scale", "log_dt", "d_skip", "glu_b"}
    small_w = [norm_g, pool_w, pool_scale, a_re, a_im, log_dt, b_re, b_im, c_re, c_im, d_skip, glu_b, final_g]
    small_g = [g_norm_g, g_pool_w, g_pool_scale, g_a_re, g_a_im, g_log_dt, g_b_re, g_b_im, g_c_re, g_c_im,
               g_d_skip, g_glu_b, g_final_g]
    small_m = [m_norm_g, m_pool_w, m_pool_scale, m_a_re, m_a_im, m_log_dt, m_b_re, m_b_im, m_c_re, m_c_im,
               m_d_skip, m_glu_b, m_final_g]
    small_v = [v_norm_g, v_pool_w, v_pool_scale, v_a_re, v_a_im, v_log_dt, v_b_re, v_b_im, v_c_re, v_c_im,
               v_d_skip, v_glu_b, v_final_g]

    wide_last = {"b_re", "b_im"}

    def blocked(arrays):
        return [a.reshape(1, 1, -1) if n == "final_g" else a[:, None, :] if n in rows
                else a.swapaxes(2, 3) if n in wide_last else a for n, a in zip(names, arrays)]

    small_d, small_nm, small_nv = _adamw_small(blocked(small_w), blocked(small_g), blocked(small_m), blocked(small_v))
    res = {}
    for kind, arrays in (("grad", small_g), ("delta", small_d), ("m", small_nm), ("v", small_nv)):
        for n, a, like in zip(names, arrays, small_w):
            if kind != "grad" and n in wide_last:
                a = a.swapaxes(2, 3)
            res[kind, n] = a.reshape(like.shape)

    (s_win,), (r_win,) = _exchange_wait(pending, 1, small_d[0], "comm_grads_wait_0")
    (s_glu, s_wout), (r_glu, r_wout) = _exchange_wait(early, 2, small_d[0], "comm_grads_wait_0_rest")
    sent[0], received[0] = (s_win, s_glu, s_wout), (r_win, r_glu, r_wout)
    shard_res = {}
    for pos, (n, w, m, v) in enumerate((("w_in", w_in, m_w_in, v_w_in), ("glu_w", glu_w, m_glu_w, v_glu_w),
                                        ("w_out", w_out, m_w_out, v_w_out))):
        shard_res[n] = _adamw_summed([received[l][pos] for l in range(depth)], [sent[l][pos] for l in range(depth)],
                                     my_idx, w, m, v, "adamw_" + n)
    for n in ("w_in", "glu_w", "w_out"):
        for pos, kind in enumerate(("grad", "delta", "m", "v")):
            res[kind, n] = shard_res[n][pos]

    order = ["norm_g", "w_in", "pool_w", "pool_scale", "a_re", "a_im", "log_dt", "b_re", "b_im", "c_re", "c_im",
             "d_skip", "glu_w", "glu_b", "w_out", "final_g"]
    outs = [loss, dx.reshape(nb, seq, D_MODEL)]
    for kind in ("grad", "delta", "m", "v"):
        outs += [res[kind, n] for n in order]
    return tuple(outs)
```

```python
import functools
import math

import jax
import jax.numpy as jnp
from jax import lax
from jax.experimental import pallas as pl
from jax.experimental.pallas import tpu as pltpu

F32 = jnp.float32
MXU_DTYPE = jnp.bfloat16

D_MODEL = 1024
MIX = 1024
POOL_W = 512
SSM_W = 512
N_POOL_G = 4
POOL_GC = 128
SSM_G = 32
SSM_C = 16
SSM_P = 64
DEPTH = 4
NORM_EPS = 1e-5
N_DEV = 8

ADAM_LR = 0.001
ADAM_B1 = 0.9
ADAM_B2 = 0.999
ADAM_EPS = 1e-08
ADAM_WD = 0.01
ADAM_STEP = 10

SUBLANES = 8
LANES = 128
HALO = 16
STATE_ROWS = 8
STATE_COLS = 256
T_BLK = 256
TM_FWD = 512
TM_BWD = 512
VMEM_LIMIT = 56 * 1024 * 1024

MESH = pl.DeviceIdType.MESH
VMEM_SPEC = pl.BlockSpec(memory_space=pltpu.VMEM)
ANY_SPEC = pl.BlockSpec(memory_space=pl.ANY)


def _mm(a, b):
    return jnp.dot(a, b, preferred_element_type=F32)


def _mm_tn(a, b):
    return lax.dot_general(a, b, (((0,), (0,)), ((), ())), preferred_element_type=F32)


def _mm_nt(a, b):
    return lax.dot_general(a, b, (((1,), (1,)), ((), ())), preferred_element_type=F32)


def _mx(a):
    return a.astype(MXU_DTYPE)


def _sigmoid(v):
    return 1.0 / (1.0 + jnp.exp(-v))


_GELU_C = math.sqrt(2.0 / math.pi)
_GELU_A = 0.044715


def _gelu_and_grad(y):
    th = jnp.tanh(_GELU_C * (y + _GELU_A * y * y * y))
    val = 0.5 * y * (1.0 + th)
    grad = 0.5 * (1.0 + th) + 0.5 * y * (1.0 - th * th) * (_GELU_C * (1.0 + 3.0 * _GELU_A * y * y))
    return val, grad


def _params(**kw):
    return pltpu.CompilerParams(vmem_limit_bytes=VMEM_LIMIT, **kw)


def _ssm_dense(a_re, a_im, log_dt, b_re, b_im, c_re, c_im):
    dt = jnp.exp(log_dt)[:, None]
    mag = jnp.exp(a_re * dt)
    ang = a_im * dt
    lb_re = mag * jnp.cos(ang)
    lb_im = mag * jnp.sin(ang)
    den = a_re * a_re + a_im * a_im
    n_re = lb_re - 1.0
    n_im = lb_im
    f_re = (n_re * a_re + n_im * a_im) / den
    f_im = (n_im * a_re - n_re * a_im) / den
    bb_re = f_re[..., None] * b_re - f_im[..., None] * b_im
    bb_im = f_re[..., None] * b_im + f_im[..., None] * b_re

    bb = jnp.stack([bb_re, bb_im], axis=0).reshape(2, 8, 4, SSM_P, SSM_C)
    rb = bb.transpose(1, 4, 0, 2, 3).reshape(8, SSM_C, 512)
    cc = jnp.stack([c_re, -c_im], axis=0).reshape(2, 8, 4, SSM_C, SSM_P)
    rc = cc.transpose(1, 3, 0, 2, 4).reshape(8, SSM_C, 512)
    return (lb_re.reshape(STATE_ROWS, STATE_COLS), lb_im.reshape(STATE_ROWS, STATE_COLS), rb, rc)


def _ssm_chunked(per_channel):
    row_group = jnp.arange(64) // SSM_C
    col_group = (jnp.arange(512) // SSM_P) % 4
    own_group = (row_group[:, None] == col_group[None, :]).astype(F32)
    even = (jnp.arange(8) % 2 == 0).astype(F32)[:, None, None]
    half = jnp.tile(per_channel, (1, 4, 1)) * own_group
    return jnp.concatenate([half * even, half * (1.0 - even)], axis=1)


def _inproj_fwd(x2, g_row, w_all, dep):
    n = x2.shape[0]
    tm = TM_FWD

    def body(x_ref, g_ref, w_ref, dep_ref, z_ref, h_ref):
        x = x_ref[...]
        r = lax.rsqrt(jnp.mean(x * x, axis=-1, keepdims=True) + NORM_EPS)
        h = _mx(x * r * g_ref[...])
        h_ref[...] = h
        for d in range(N_DEV):
            z_ref[:, d * 256:(d + 1) * 256] = _mm(h, w_ref[d])

    return pl.pallas_call(
        body, name="inproj_fwd",
        grid=(n // tm,),
        in_specs=[pl.BlockSpec((tm, D_MODEL), lambda i: (i, 0)),
                  pl.BlockSpec((1, D_MODEL), lambda i: (0, 0)),
                  pl.BlockSpec((N_DEV, D_MODEL, 256), lambda i: (0, 0, 0)),
                  ANY_SPEC],
        out_specs=[pl.BlockSpec((tm, 2 * MIX), lambda i: (i, 0)),
                   pl.BlockSpec((tm, D_MODEL), lambda i: (i, 0))],
        out_shape=[jax.ShapeDtypeStruct((n, 2 * MIX), F32),
                   jax.ShapeDtypeStruct((n, D_MODEL), MXU_DTYPE)],
        compiler_params=_params(dimension_semantics=("arbitrary",)),
    )(x2, g_row, w_all, dep)


def _loss_head(x2, tgt2, g_row):
    n = x2.shape[0]
    tm = TM_FWD

    def body(x_ref, t_ref, g_ref, dx_ref, loss_ref, dg_ref):
        @pl.when(pl.program_id(0) == 0)
        def _():
            loss_ref[...] = jnp.zeros_like(loss_ref)
            dg_ref[...] = jnp.zeros_like(dg_ref)

        x = x_ref[...]
        g = g_ref[...]
        r = lax.rsqrt(jnp.mean(x * x, axis=-1, keepdims=True) + NORM_EPS)
        xh = x * r
        e = xh * g - t_ref[...]
        loss_ref[...] += jnp.sum(jnp.sum(e * e, axis=-1, keepdims=True), axis=0, keepdims=True) * (0.5 / D_MODEL)
        dout = e * (1.0 / D_MODEL)
        dg_ref[...] += jnp.sum(dout * xh, axis=0, keepdims=True)
        gdy = dout * g
        dx_ref[...] = r * (gdy - xh * jnp.mean(xh * gdy, axis=-1, keepdims=True))

    return pl.pallas_call(
        body, name="loss_head",
        grid=(n // tm,),
        in_specs=[pl.BlockSpec((tm, D_MODEL), lambda i: (i, 0)),
                  pl.BlockSpec((tm, D_MODEL), lambda i: (i, 0)),
                  pl.BlockSpec((1, D_MODEL), lambda i: (0, 0))],
        out_specs=[pl.BlockSpec((tm, D_MODEL), lambda i: (i, 0)),
                   pl.BlockSpec((1, 1), lambda i: (0, 0)),
                   pl.BlockSpec((1, D_MODEL), lambda i: (0, 0))],
        out_shape=[jax.ShapeDtypeStruct((n, D_MODEL), F32),
                   jax.ShapeDtypeStruct((1, 1), F32),
                   jax.ShapeDtypeStruct((1, D_MODEL), F32)],
        compiler_params=_params(dimension_semantics=("arbitrary",)),
    )(x2, tgt2, g_row)


def _outproj_bwd(dx2, yg, w_out, dep):
    n = dx2.shape[0]
    tm = TM_BWD
    n_steps = n // tm

    def body(dx_ref, y_ref, w_ref, dep_ref, dy_ref, dw_ref, acc_ref):
        i = pl.program_id(0)

        @pl.when(i == 0)
        def _():
            acc_ref[...] = jnp.zeros_like(acc_ref)

        dxb = _mx(dx_ref[...])
        dy_ref[...] = _mm_nt(dxb, w_ref[...])
        acc_ref[...] += _mm_tn(y_ref[...], dxb)

        @pl.when(i == n_steps - 1)
        def _():
            dw_ref[...] = _mx(acc_ref[...])

    return pl.pallas_call(
        body, name="outproj_bwd",
        grid=(n_steps,),
        in_specs=[pl.BlockSpec((tm, D_MODEL), lambda i: (i, 0)),
                  pl.BlockSpec((tm, MIX), lambda i: (i, 0)),
                  pl.BlockSpec((MIX, D_MODEL), lambda i: (0, 0)),
                  ANY_SPEC],
        out_specs=[pl.BlockSpec((tm, MIX), lambda i: (i, 0)),
                   pl.BlockSpec((MIX, D_MODEL), lambda i: (0, 0))],
        out_shape=[jax.ShapeDtypeStruct((n, MIX), F32),
                   jax.ShapeDtypeStruct((MIX, D_MODEL), MXU_DTYPE)],
        scratch_shapes=[pltpu.VMEM((MIX, D_MODEL), F32)],
        compiler_params=_params(dimension_semantics=("arbitrary",)),
    )(dx2, yg, w_out, dep)


def _inproj_bwd(dz, h, x2, dx_in, g_row, w_all, dep):
    n = x2.shape[0]
    tm = TM_BWD
    n_steps = n // tm

    def body(dz_ref, h_ref, x_ref, dxi_ref, g_ref, w_ref, dep_ref, dxo_ref, dw_ref, dg_ref, acc_ref, wcat_ref):
        i = pl.program_id(0)

        @pl.when(i == 0)
        def _():
            acc_ref[...] = jnp.zeros_like(acc_ref)
            dg_ref[...] = jnp.zeros_like(dg_ref)
            for d in range(N_DEV):
                wcat_ref[:, d * 256:(d + 1) * 256] = w_ref[d]

        hb = h_ref[...]
        for d in range(N_DEV):
            acc_ref[d] += _mm_tn(hb, dz_ref[:, d * 256:(d + 1) * 256])
        dh = _mm_nt(dz_ref[...], wcat_ref[...])
        x = x_ref[...]
        r = lax.rsqrt(jnp.mean(x * x, axis=-1, keepdims=True) + NORM_EPS)
        xh = x * r
        dg_ref[...] += jnp.sum(dh * xh, axis=0, keepdims=True)
        gdy = dh * g_ref[...]
        dxo_ref[...] = dxi_ref[...] + r * (gdy - xh * jnp.mean(xh * gdy, axis=-1, keepdims=True))

        @pl.when(i == n_steps - 1)
        def _():
            dw_ref[...] = _mx(acc_ref[...])

    return pl.pallas_call(
        body, name="inproj_bwd",
        grid=(n_steps,),
        in_specs=[pl.BlockSpec((tm, 2 * MIX), lambda i: (i, 0)),
                  pl.BlockSpec((tm, D_MODEL), lambda i: (i, 0)),
                  pl.BlockSpec((tm, D_MODEL), lambda i: (i, 0)),
                  pl.BlockSpec((tm, D_MODEL), lambda i: (i, 0)),
                  pl.BlockSpec((1, D_MODEL), lambda i: (0, 0)),
                  pl.BlockSpec((N_DEV, D_MODEL, 256), lambda i: (0, 0, 0)),
                  ANY_SPEC],
        out_specs=[pl.BlockSpec((tm, D_MODEL), lambda i: (i, 0)),
                   pl.BlockSpec((N_DEV, D_MODEL, 256), lambda i: (0, 0, 0)),
                   pl.BlockSpec((1, D_MODEL), lambda i: (0, 0))],
        out_shape=[jax.ShapeDtypeStruct((n, D_MODEL), F32),
                   jax.ShapeDtypeStruct((N_DEV, D_MODEL, 256), MXU_DTYPE),
                   jax.ShapeDtypeStruct((1, D_MODEL), F32)],
        scratch_shapes=[pltpu.VMEM((N_DEV, D_MODEL, 256), F32),
                        pltpu.VMEM((D_MODEL, 2 * MIX), MXU_DTYPE)],
        compiler_params=_params(dimension_semantics=("arbitrary",)),
    )(dz, h, x2, dx_in, g_row, w_all, dep)


def _row_pos(t0, rows):
    return t0 + lax.broadcasted_iota(jnp.int32, (rows, LANES), 0)


def _pool_window_mean(upad, g, t0, t_blk):
    k = 2 << g
    w = upad
    sh = 1
    while sh < k:
        w = w + pltpu.roll(w, sh, 0)
        sh *= 2
    count = jnp.minimum(_row_pos(t0, t_blk) + 1, k).astype(F32)
    return w[HALO:] / count - upad[HALO:]


def _pool_window_bwd(qpad, g, t_blk):
    k = 2 << g
    n = t_blk + HALO
    w = qpad
    sh = 1
    while sh < k:
        w = w + pltpu.roll(w, n - sh, 0)
        sh *= 2
    return w[:t_blk]


class _StateBuf:
    def __init__(self, refs, t_blk):
        self.refs = refs
        self.t_blk = t_blk

    def put_chunk(self, b, j, val):
        for c in range(4):
            self.refs[4 * b + c][pl.ds(j, self.t_blk, stride=STATE_ROWS), :] = val[:, c * LANES:(c + 1) * LANES]

    def get_chunk(self, b, j):
        return jnp.concatenate(
            [self.refs[4 * b + c][pl.ds(j, self.t_blk, stride=STATE_ROWS), :] for c in range(4)], axis=-1)

    def load(self, b, r, part):
        return jnp.concatenate(
            [self.refs[4 * b + 2 * part + h][pl.ds(r, STATE_ROWS), :] for h in range(2)], axis=-1)

    def store(self, b, r, part, val):
        for h in range(2):
            self.refs[4 * b + 2 * part + h][pl.ds(r, STATE_ROWS), :] = val[:, h * LANES:(h + 1) * LANES]


def _state_scratch(nb, t_blk):
    return [pltpu.VMEM((t_blk * STATE_ROWS, LANES), F32) for _ in range(4 * nb)]


def _ssm_project_in(u_ssm, wb_ref, buf, nb):
    t_blk = u_ssm.shape[0] // nb
    ub = _mx(u_ssm)
    for j in range(STATE_ROWS):
        m = j // 2
        bu = _mm(ub[:, m * LANES:(m + 1) * LANES], wb_ref[j])
        for b in range(nb):
            buf.put_chunk(b, j, bu[b * t_blk:(b + 1) * t_blk])


def _scan_forward(buf, lbr, lbi, init, nb):
    def body(t, carry):
        r = pl.multiple_of(t * STATE_ROWS, STATE_ROWS)
        out = []
        for b in range(nb):
            sr, si = carry[2 * b], carry[2 * b + 1]
            nr = lbr * sr - lbi * si + buf.load(b, r, 0)
            ni = lbr * si + lbi * sr + buf.load(b, r, 1)
            buf.store(b, r, 0, nr)
            buf.store(b, r, 1, ni)
            out += [nr, ni]
        return tuple(out)

    return lax.fori_loop(0, buf.t_blk, body, init, unroll=4)


def _ssm_project_out(chunk, wc_ref):
    tiles = []
    for m in range(4):
        acc = None
        for j in (2 * m, 2 * m + 1):
            part = _mm_nt(chunk(j), wc_ref[j])
            acc = part if acc is None else acc + part
        tiles.append(acc)
    return jnp.concatenate(tiles, axis=-1)


def _layer_fwd(x3, z3, g_row, w_in, pool_w, pool_scale, lbr, lbi, wb, wc, d_skip, glu_w, glu_b, w_out, dep):
    nb, seq, _ = x3.shape
    t_blk = min(T_BLK, seq)
    n_t = seq // t_blk
    halo_per_blk = t_blk // HALO
    rows = nb * t_blk
    fused = z3 is None

    def body(*refs):
        if fused:
            (x_ref, g_ref, wi_ref, pw_ref, ps_ref, lbr_ref, lbi_ref, wb_ref, wc_ref, dsk_ref, gw_ref, gb_ref, wo_ref,
             dep_ref, z_ref, h_ref, yg_ref, sc_ref, xo_ref, carry_ref, halo_ref, *s_refs) = refs
        else:
            (x_ref, z_ref, zh_ref, pw_ref, ps_ref, lbr_ref, lbi_ref, wb_ref, wc_ref, dsk_ref, gw_ref, gb_ref, wo_ref,
             dep_ref, yg_ref, sc_ref, xo_ref, carry_ref, *s_refs) = refs
        i = pl.program_id(0)
        t0 = i * t_blk
        buf = _StateBuf(s_refs, t_blk)
        both = lambda lo, hi: z_ref[:, :, lo:hi].reshape(rows, hi - lo)

        @pl.when(i == 0)
        def _():
            carry_ref[...] = jnp.zeros_like(carry_ref)
            if fused:
                halo_ref[...] = jnp.zeros_like(halo_ref)

        x = x_ref[...].reshape(rows, D_MODEL)
        if fused:
            r = lax.rsqrt(jnp.mean(x * x, axis=-1, keepdims=True) + NORM_EPS)
            h = _mx(x * r * g_ref[...])
            h_ref[...] = h.reshape(nb, t_blk, D_MODEL)
            for d in range(N_DEV):
                z_ref[:, :, d * 256:(d + 1) * 256] = _mm(h, wi_ref[d]).reshape(nb, t_blk, 256)

        u_ssm = both(POOL_W, MIX)
        _ssm_project_in(u_ssm, wb_ref, buf, nb)
        init = tuple(carry_ref[b, :, h * STATE_COLS:(h + 1) * STATE_COLS] for b in range(nb) for h in range(2))
        fin = _scan_forward(buf, lbr_ref[...], lbi_ref[...], init, nb)
        for b in range(nb):
            carry_ref[b, :, 0:STATE_COLS] = fin[2 * b]
            carry_ref[b, :, STATE_COLS:2 * STATE_COLS] = fin[2 * b + 1]

        def chunk(j):
            states = _mx(jnp.concatenate([buf.get_chunk(b, j) for b in range(nb)], axis=0))
            sc_ref[:, j] = states.reshape(nb, t_blk, 2 * STATE_COLS)
            return states

        y = _ssm_project_out(chunk, wc_ref) + dsk_ref[...] * u_ssm
        yg, _ = _gelu_and_grad(y)
        o_ssm = yg * _sigmoid(_mm(_mx(yg), gw_ref[...]) + gb_ref[...])
        gp = both(MIX + POOL_W, 2 * MIX)
        parts = []
        first = (i == 0)
        for g in range(N_POOL_G):
            cols = slice(g * POOL_GC, (g + 1) * POOL_GC)
            pooled = []
            for b in range(nb):
                halo = halo_ref[b, :, cols] if fused else jnp.where(first, 0.0, zh_ref[b, :, cols])
                pooled.append(_pool_window_mean(jnp.concatenate([halo, z_ref[b, :, cols]], axis=0), g, t0, t_blk))
            yp = _mm(_mx(jnp.concatenate(pooled, axis=0)), pw_ref[g]) * ps_ref[:, cols]
            gpp = both(MIX + g * POOL_GC, MIX + (g + 1) * POOL_GC)
            parts.append(_mx(yp * (gpp * _sigmoid(gpp))))
        parts.append(_mx(o_ssm * (gp * _sigmoid(gp))))
        gated = jnp.concatenate(parts, axis=-1)
        yg_ref[...] = gated.reshape(nb, t_blk, MIX)
        xo_ref[...] = (x + _mm(gated, wo_ref[...])).reshape(nb, t_blk, D_MODEL)
        if fused:
            halo_ref[...] = z_ref[:, t_blk - HALO:, 0:POOL_W]

    const = lambda *shape: pl.BlockSpec(shape, lambda i: (0,) * len(shape))
    tokens = lambda width: pl.BlockSpec((nb, t_blk, width), lambda i: (0, i, 0))
    mixer_specs = [const(N_POOL_G, POOL_GC, POOL_GC), const(1, POOL_W),
                   const(STATE_ROWS, STATE_COLS), const(STATE_ROWS, STATE_COLS),
                   const(STATE_ROWS, LANES, 2 * STATE_COLS), const(STATE_ROWS, LANES, 2 * STATE_COLS),
                   const(1, SSM_W), const(SSM_W, SSM_W), const(1, SSM_W), const(MIX, D_MODEL), ANY_SPEC]
    mixer_args = (pool_w, pool_scale, lbr, lbi, wb, wc, d_skip, glu_w, glu_b, w_out, dep)
    out_specs = [tokens(MIX), pl.BlockSpec((nb, STATE_ROWS, t_blk, 2 * STATE_COLS), lambda i: (0, 0, i, 0)),
                 tokens(D_MODEL)]
    out_shape = [jax.ShapeDtypeStruct((nb, seq, MIX), MXU_DTYPE),
                 jax.ShapeDtypeStruct((nb, STATE_ROWS, seq, 2 * STATE_COLS), MXU_DTYPE),
                 jax.ShapeDtypeStruct((nb, seq, D_MODEL), F32)]
    scratch = [pltpu.VMEM((nb, STATE_ROWS, 2 * STATE_COLS), F32)]
    if fused:
        in_specs = [tokens(D_MODEL), const(1, D_MODEL), const(N_DEV, D_MODEL, 256)] + mixer_specs
        args = (x3, g_row, w_in) + mixer_args
        out_specs = [tokens(2 * MIX), tokens(D_MODEL)] + out_specs
        out_shape = [jax.ShapeDtypeStruct((nb, seq, 2 * MIX), F32),
                     jax.ShapeDtypeStruct((nb, seq, D_MODEL), MXU_DTYPE)] + out_shape
        scratch = scratch + [pltpu.VMEM((nb, HALO, POOL_W), F32)]
    else:
        in_specs = [tokens(D_MODEL), tokens(2 * MIX),
                    pl.BlockSpec((nb, HALO, POOL_W), lambda i: (0, jnp.maximum(i * halo_per_blk - 1, 0), 0))] + mixer_specs
        args = (x3, z3, z3) + mixer_args
    return pl.pallas_call(
        body, name="layer_fwd" if fused else "mixer_fwd",
        grid=(n_t,),
        in_specs=in_specs, out_specs=out_specs, out_shape=out_shape,
        scratch_shapes=scratch + _state_scratch(nb, t_blk),
        compiler_params=_params(dimension_semantics=("arbitrary",)),
    )(*args)


def _mixer_bwd(z3, dy3, states, pool_w, pool_scale, lbr, lbi, wb, wc, d_skip, glu_w, glu_b):
    nb, seq, _ = z3.shape
    t_blk = min(T_BLK, seq)
    n_t = seq // t_blk
    halo_per_blk = t_blk // HALO
    rows = nb * t_blk

    def body(z_ref, zh_ref, dy_ref, sc_ref, sch_ref, pw_ref, ps_ref, lbr_ref, lbi_ref, wb_ref, wc_ref, dsk_ref,
             gw_ref, gb_ref,
             dz_ref, dpw_ref, dps_ref, dlbr_ref, dlbi_ref, dwb_ref, dwc_ref, ddsk_ref, dgw_ref, dgb_ref,
             gcarry_ref, qcarry_ref, du_ref, dgw_acc, *g_refs):
        i = pl.program_id(0)
        blk = n_t - 1 - i
        t0 = blk * t_blk
        gbuf = _StateBuf(g_refs, t_blk)

        @pl.when(i == 0)
        def _():
            gcarry_ref[...] = jnp.zeros_like(gcarry_ref)
            qcarry_ref[...] = jnp.zeros_like(qcarry_ref)
            for ref in (dpw_ref, dps_ref, dlbr_ref, dlbi_ref, dwb_ref, dwc_ref, ddsk_ref, dgw_acc, dgb_ref):
                ref[...] = jnp.zeros_like(ref)

        lbr_v = lbr_ref[...]
        lbi_v = lbi_ref[...]

        both = lambda ref, lo, hi: ref[:, :, lo:hi].reshape(rows, hi - lo)
        split = lambda val: val.reshape(nb, t_blk, val.shape[-1])
        states = lambda j: sc_ref[:, j].reshape(rows, 2 * STATE_COLS)
        first = (blk == 0)

        u_ssm = both(z_ref, POOL_W, MIX)
        y = _ssm_project_out(states, wc_ref) + dsk_ref[...] * u_ssm
        yg, dgelu = _gelu_and_grad(y)
        ygb = _mx(yg)
        sg = _sigmoid(_mm(ygb, gw_ref[...]) + gb_ref[...])
        o_ssm = yg * sg
        gp = both(z_ref, MIX + POOL_W, 2 * MIX)
        sgm = _sigmoid(gp)
        dyv = both(dy_ref, POOL_W, MIX)
        dz_ref[:, :, MIX + POOL_W:2 * MIX] = split(_mx(dyv * o_ssm * (sgm * (1.0 + gp * (1.0 - sgm)))))
        do = dyv * (gp * sgm)
        dv = do * yg * (sg * (1.0 - sg))
        dvb = _mx(dv)
        dgb_ref[...] += jnp.sum(dv, axis=0, keepdims=True)
        dgw_acc[...] += _mm_tn(ygb, dvb)
        dyp = (do * sg + _mm_nt(dvb, gw_ref[...])) * dgelu
        ddsk_ref[...] += jnp.sum(dyp * u_ssm, axis=0, keepdims=True)
        dypb = _mx(dyp)
        for j in range(STATE_ROWS):
            m = j // 2
            dyt = dypb[:, m * LANES:(m + 1) * LANES]
            ds = _mm(dyt, wc_ref[j])
            for b in range(nb):
                gbuf.put_chunk(b, j, ds[b * t_blk:(b + 1) * t_blk])
            dwc_ref[j] += _mm_tn(dyt, states(j))
        du_ref[...] = split(dsk_ref[...] * dyp)

        for g in range(N_POOL_G):
            cols = slice(g * POOL_GC, (g + 1) * POOL_GC)
            pooled = []
            for b in range(nb):
                halo = jnp.where(first, 0.0, zh_ref[b, :, cols])
                pooled.append(_pool_window_mean(jnp.concatenate([halo, z_ref[b, :, cols]], axis=0), g, t0, t_blk))
            pb = _mx(jnp.concatenate(pooled, axis=0))
            ypre = _mm(pb, pw_ref[g])
            gpp = both(z_ref, MIX + g * POOL_GC, MIX + (g + 1) * POOL_GC)
            sgp = _sigmoid(gpp)
            dyg = both(dy_ref, g * POOL_GC, (g + 1) * POOL_GC)
            scale = ps_ref[:, cols]
            dz_ref[:, :, MIX + g * POOL_GC:MIX + (g + 1) * POOL_GC] = split(_mx(
                dyg * (ypre * scale) * (sgp * (1.0 + gpp * (1.0 - sgp)))))
            dyc = dyg * (gpp * sgp)
            dps_ref[:, cols] += jnp.sum(dyc * ypre, axis=0, keepdims=True)
            dypre = _mx(dyc * scale)
            dpw_ref[g] += _mm_tn(pb, dypre)
            dpooled = _mm_nt(dypre, pw_ref[g])
            count = jnp.minimum(_row_pos(t0, t_blk) + 1, 2 << g).astype(F32)
            for b in range(nb):
                dp = dpooled[b * t_blk:(b + 1) * t_blk]
                q = dp / count
                qpad = jnp.concatenate([q, qcarry_ref[b, :, cols]], axis=0)
                qcarry_ref[b, :, cols] = q[:HALO]
                dz_ref[b, :, cols] = _mx(_pool_window_bwd(qpad, g, t_blk) - dp)

        def rev_body(k, carry):
            r = pl.multiple_of((t_blk - 1 - k) * STATE_ROWS, STATE_ROWS)
            out = []
            for b in range(nb):
                gr, gi = carry[2 * b], carry[2 * b + 1]
                ngr = lbr_v * gr + lbi_v * gi + gbuf.load(b, r, 0)
                ngi = lbr_v * gi - lbi_v * gr + gbuf.load(b, r, 1)
                gbuf.store(b, r, 0, ngr)
                gbuf.store(b, r, 1, ngi)
                out += [ngr, ngi]
            return tuple(out)

        init_g = tuple(gcarry_ref[b, :, h * STATE_COLS:(h + 1) * STATE_COLS] for b in range(nb) for h in range(2))
        fin = lax.fori_loop(0, t_blk, rev_body, init_g, unroll=4)
        for b in range(nb):
            gcarry_ref[b, :, 0:STATE_COLS] = fin[2 * b]
            gcarry_ref[b, :, STATE_COLS:2 * STATE_COLS] = fin[2 * b + 1]

        ub = _mx(u_ssm)
        for m in range(4):
            acc = both(du_ref, m * LANES, (m + 1) * LANES)
            for j in (2 * m, 2 * m + 1):
                g = jnp.concatenate([gbuf.get_chunk(b, j) for b in range(nb)], axis=0)
                gj = _mx(g)
                acc = acc + _mm_nt(gj, wb_ref[j])
                dwb_ref[j] += _mm_tn(ub[:, m * LANES:(m + 1) * LANES], gj)
                shifted = []
                for b in range(nb):
                    before = jnp.where(first, 0.0, sch_ref[b, j].astype(F32))
                    spad = jnp.concatenate([before, sc_ref[b, j].astype(F32)], axis=0)
                    shifted.append(pltpu.roll(spad, 1, 0)[HALO:])
                s_prev = jnp.concatenate(shifted, axis=0)
                g_re, g_im = g[:, :STATE_COLS], g[:, STATE_COLS:]
                p_re, p_im = s_prev[:, :STATE_COLS], s_prev[:, STATE_COLS:]
                dlbr_ref[j:j + 1, :] += jnp.sum(g_re * p_re + g_im * p_im, axis=0, keepdims=True)
                dlbi_ref[j:j + 1, :] += jnp.sum(g_im * p_re - g_re * p_im, axis=0, keepdims=True)
            dz_ref[:, :, POOL_W + m * LANES:POOL_W + (m + 1) * LANES] = split(_mx(acc))

        @pl.when(i == n_t - 1)
        def _():
            dgw_ref[...] = _mx(dgw_acc[...])

    const = lambda *shape: pl.BlockSpec(shape, lambda i: (0,) * len(shape))
    rev = lambda i: n_t - 1 - i
    out_shape = [jax.ShapeDtypeStruct((nb, seq, 2 * MIX), MXU_DTYPE),
                 jax.ShapeDtypeStruct((N_POOL_G, POOL_GC, POOL_GC), F32),
                 jax.ShapeDtypeStruct((1, POOL_W), F32),
                 jax.ShapeDtypeStruct((STATE_ROWS, STATE_COLS), F32),
                 jax.ShapeDtypeStruct((STATE_ROWS, STATE_COLS), F32),
                 jax.ShapeDtypeStruct((STATE_ROWS, LANES, 2 * STATE_COLS), F32),
                 jax.ShapeDtypeStruct((STATE_ROWS, LANES, 2 * STATE_COLS), F32),
                 jax.ShapeDtypeStruct((1, SSM_W), F32),
                 jax.ShapeDtypeStruct((SSM_W, SSM_W), MXU_DTYPE),
                 jax.ShapeDtypeStruct((1, SSM_W), F32)]
    return pl.pallas_call(
        body, name="mixer_bwd",
        grid=(n_t,),
        in_specs=[pl.BlockSpec((nb, t_blk, 2 * MIX), lambda i: (0, rev(i), 0)),
                  pl.BlockSpec((nb, HALO, POOL_W), lambda i: (0, jnp.maximum(rev(i) * halo_per_blk - 1, 0), 0)),
                  pl.BlockSpec((nb, t_blk, MIX), lambda i: (0, rev(i), 0)),
                  pl.BlockSpec((nb, STATE_ROWS, t_blk, 2 * STATE_COLS), lambda i: (0, 0, rev(i), 0)),
                  pl.BlockSpec((nb, STATE_ROWS, HALO, 2 * STATE_COLS),
                               lambda i: (0, 0, jnp.maximum(rev(i) * halo_per_blk - 1, 0), 0)),
                  const(N_POOL_G, POOL_GC, POOL_GC), const(1, POOL_W),
                  const(STATE_ROWS, STATE_COLS), const(STATE_ROWS, STATE_COLS),
                  const(STATE_ROWS, LANES, 2 * STATE_COLS), const(STATE_ROWS, LANES, 2 * STATE_COLS),
                  const(1, SSM_W), const(SSM_W, SSM_W), const(1, SSM_W)],
        out_specs=[pl.BlockSpec((nb, t_blk, 2 * MIX), lambda i: (0, rev(i), 0))]
                  + [const(*s.shape) for s in out_shape[1:]],
        out_shape=out_shape,
        scratch_shapes=[pltpu.VMEM((nb, STATE_ROWS, 2 * STATE_COLS), F32),
                        pltpu.VMEM((nb, HALO, POOL_W), F32),
                        pltpu.VMEM((nb, t_blk, SSM_W), F32),
                        pltpu.VMEM((SSM_W, SSM_W), F32)]
                       + _state_scratch(nb, t_blk),
        compiler_params=_params(dimension_semantics=("arbitrary",)),
    )(z3, z3, dy3, states, states, pool_w, pool_scale, lbr, lbi, wb, wc, d_skip, glu_w, glu_b)


def _mesh_place():
    x, y, c = lax.axis_index("x"), lax.axis_index("y"), lax.axis_index("c")
    return x, y, c


def _flip(place, k):
    x, y, c = place
    return (1 - x if k & 4 else x, 1 - y if k & 2 else y, 1 - c if k & 1 else c)


def _index(place):
    x, y, c = place
    return 4 * x + 2 * y + c


HBM_SPEC = pl.BlockSpec(memory_space=pltpu.HBM)
SEM_SPEC = pl.BlockSpec(memory_space=pltpu.SEMAPHORE)
_EFFECT = pltpu.SideEffectType.DATAFLOW_SIDE_EFFECTING
N_PEERS = N_DEV - 1


def _exchange_copies(src_refs, land_refs, send_sems, recv_sems):
    me = _mesh_place()
    mine = _index(me)
    out = []
    for a, land_ref in enumerate(land_refs):
        for k in range(1, N_DEV):
            peer = _flip(me, k)
            theirs = _index(peer)
            n = a * N_PEERS + k - 1
            src = src_refs[a].at[theirs] if src_refs else land_ref.at[mine]
            send = pltpu.make_async_remote_copy(
                src_ref=src, dst_ref=land_ref.at[mine], send_sem=send_sems.at[n], recv_sem=recv_sems.at[n],
                device_id=peer, device_id_type=MESH)
            recv = pltpu.make_async_remote_copy(
                src_ref=src, dst_ref=land_ref.at[theirs], send_sem=send_sems.at[n], recv_sem=recv_sems.at[n],
                device_id=peer, device_id_type=MESH)
            out.append((send, recv))
    return out


def _exchange_start(srcs, lands, after, name):
    arrays = tuple(srcs) + tuple(lands)
    n_src, n_all = len(srcs), len(arrays)
    n_copies = len(lands) * N_PEERS

    def body(*refs):
        send_sems, recv_sems = refs[n_all + 1], refs[n_all + 2]
        token = refs[-1]
        for send, _ in _exchange_copies(refs[:n_src], refs[n_src:n_all], send_sems, recv_sems):
            send.start()
        token[...] = jnp.zeros_like(token)

    res = pl.pallas_call(
        body, name=name,
        in_specs=[HBM_SPEC] * n_all + [ANY_SPEC],
        out_specs=[SEM_SPEC, SEM_SPEC] + [HBM_SPEC] * n_all + [VMEM_SPEC],
        out_shape=[pltpu.SemaphoreType.DMA((n_copies,)), pltpu.SemaphoreType.DMA((n_copies,))]
                  + [pltpu.HBM(a.shape, a.dtype) for a in arrays] + [jax.ShapeDtypeStruct((SUBLANES, LANES), F32)],
        input_output_aliases={i: 2 + i for i in range(n_all)},
        compiler_params=pltpu.CompilerParams(has_side_effects=_EFFECT),
    )(*[pltpu.with_memory_space_constraint(a, pltpu.HBM) for a in arrays], after)
    return tuple(res[:-1]), res[-1]


def _exchange_wait(handle, n_lands, after, name):
    send_sems, recv_sems = handle[0], handle[1]
    arrays = handle[2:]
    n_all = len(arrays)
    n_src = n_all - n_lands

    def body(*refs):
        for send, recv in _exchange_copies(refs[:n_src], refs[n_src:n_all], refs[n_all], refs[n_all + 1]):
            send.wait_send()
            recv.wait_recv()

    res = pl.pallas_call(
        body, name=name,
        in_specs=[HBM_SPEC] * n_all + [SEM_SPEC, SEM_SPEC, ANY_SPEC],
        out_specs=[HBM_SPEC] * n_all,
        out_shape=[pltpu.HBM(a.shape, a.dtype) for a in arrays],
        input_output_aliases={i: i for i in range(n_all)},
        compiler_params=pltpu.CompilerParams(has_side_effects=_EFFECT),
    )(*arrays, send_sems, recv_sems, after)
    return tuple(res[:n_src]), tuple(res[n_src:])


def _weight_zones(w_in, glu_w, w_out, my_idx):
    shards = (w_in, glu_w, w_out)
    depth = w_in.shape[0]

    def body(idx_ref, *refs):
        ins, zones = refs[:len(shards)], refs[len(shards):]
        for l in range(depth):
            for a, src in enumerate(ins):
                zones[l * len(shards) + a][0] = _mx(src[l])

    whole = lambda s: pl.BlockSpec(s.shape, lambda i, idx: (0,) * s.ndim)
    return pl.pallas_call(
        body, name="weight_zones",
        grid_spec=pltpu.PrefetchScalarGridSpec(
            num_scalar_prefetch=1, grid=(1,),
            in_specs=[whole(s) for s in shards],
            out_specs=[pl.BlockSpec((1,) + s.shape[1:], lambda i, idx: (idx[0], 0, 0))
                       for _ in range(depth) for s in shards]),
        out_shape=[jax.ShapeDtypeStruct((N_DEV,) + s.shape[1:], MXU_DTYPE) for _ in range(depth) for s in shards],
        compiler_params=_params(dimension_semantics=("arbitrary",)),
    )(my_idx.reshape(1).astype(jnp.int32), *shards)


def _allreduce_packed(p):
    rows = p.shape[0]
    half = rows // 2
    quarter = half // 4

    def body(p_ref, o_ref, part_ref, sib_ref, got_ref, send_sems, recv_sems):
        x, y, c = _mesh_place()
        sibling = (x, y, 1 - c)
        chip = 2 * x + y
        chips = [(k, (1 - x if k & 2 else x, 1 - y if k & 1 else y, c), chip ^ k) for k in (1, 2, 3)]
        my_half = pl.multiple_of(c * half, SUBLANES)
        other_half = pl.multiple_of((1 - c) * half, SUBLANES)

        def copy(n, src, dst, to):
            return pltpu.make_async_remote_copy(src_ref=src, dst_ref=dst, send_sem=send_sems.at[n],
                                                recv_sem=recv_sems.at[n], device_id=to, device_id_type=MESH)

        def quarter_of(ref, base, q):
            return ref.at[pl.ds(pl.multiple_of(base + q * quarter, SUBLANES), quarter)]

        swap = copy(0, p_ref.at[pl.ds(other_half, half)], sib_ref, sibling)
        swap.start()
        swap.wait()
        part_ref[...] = p_ref[pl.ds(my_half, half), :] + sib_ref[...]

        scatter = [copy(k, quarter_of(part_ref, 0, q), got_ref.at[k - 1], to) for k, to, q in chips]
        for cp in scatter:
            cp.start()
        total = part_ref[pl.ds(pl.multiple_of(chip * quarter, SUBLANES), quarter), :]
        for cp, (k, _, _) in zip(scatter, chips):
            cp.wait()
            total = total + got_ref[k - 1]
        mine = pl.multiple_of(my_half + chip * quarter, SUBLANES)
        o_ref[pl.ds(mine, quarter), :] = total

        gather = [copy(3 + k, o_ref.at[pl.ds(mine, quarter)], o_ref.at[pl.ds(mine, quarter)], to) for k, to, _ in chips]
        for cp in gather:
            cp.start()
        for k, to, q in chips:
            theirs = quarter_of(o_ref, my_half, q)
            copy(3 + k, theirs, theirs, to).wait_recv()
        for cp in gather:
            cp.wait_send()

        back = copy(7, o_ref.at[pl.ds(my_half, half)], o_ref.at[pl.ds(my_half, half)], sibling)
        back.start()
        copy(7, o_ref.at[pl.ds(other_half, half)], o_ref.at[pl.ds(other_half, half)], sibling).wait_recv()
        back.wait_send()

    return pl.pallas_call(
        body, name="comm_allreduce_packed",
        in_specs=[VMEM_SPEC],
        out_specs=VMEM_SPEC,
        out_shape=jax.ShapeDtypeStruct(p.shape, F32),
        scratch_shapes=[pltpu.VMEM((half, LANES), F32),
                        pltpu.VMEM((half, LANES), F32),
                        pltpu.VMEM((3, quarter, LANES), F32),
                        pltpu.SemaphoreType.DMA((8,)),
                        pltpu.SemaphoreType.DMA((8,))],
        compiler_params=_params(),
    )(p)


def _adamw_math(w, g, m, v):
    m = ADAM_B1 * m + (1.0 - ADAM_B1) * g
    v = ADAM_B2 * v + (1.0 - ADAM_B2) * (g * g)
    m_hat = m / (1.0 - ADAM_B1 ** ADAM_STEP)
    v_hat = v / (1.0 - ADAM_B2 ** ADAM_STEP)
    delta = -ADAM_LR * (m_hat / (jnp.sqrt(v_hat) + ADAM_EPS) + ADAM_WD * w)
    return delta, m, v


def _adamw_summed(received, own, my_idx, w, m, v, name):
    depth, r, c = w.shape
    tr = min(r, 128)

    def body(idx_ref, *refs):
        r_refs, o_refs = refs[:depth], refs[depth:2 * depth]
        w_ref, m_ref, v_ref, g_ref, d_ref, nm_ref, nv_ref = refs[2 * depth:]
        me = idx_ref[0]
        for l in range(depth):
            g = jnp.zeros((tr, c), F32)
            for q in range(N_DEV):
                g = g + jnp.where(q == me, o_refs[l][0], r_refs[l][q]).astype(F32)
            g_ref[l] = g
            d_ref[l], nm_ref[l], nv_ref[l] = _adamw_math(w_ref[l], g, m_ref[l], v_ref[l])

    blk = pl.BlockSpec((depth, tr, c), lambda i, idx: (0, i, 0))
    return pl.pallas_call(
        body, name=name,
        grid_spec=pltpu.PrefetchScalarGridSpec(
            num_scalar_prefetch=1, grid=(r // tr,),
            in_specs=[pl.BlockSpec((N_DEV, tr, c), lambda i, idx: (0, i, 0))] * depth
                     + [pl.BlockSpec((1, tr, c), lambda i, idx: (idx[0], i, 0))] * depth
                     + [blk, blk, blk],
            out_specs=[blk] * 4),
        out_shape=[jax.ShapeDtypeStruct((depth, r, c), F32)] * 4,
        compiler_params=_params(dimension_semantics=("arbitrary",)),
    )(my_idx.reshape(1).astype(jnp.int32), *received, *own, w, m, v)


def _adamw_small(ws, gs, ms, vs):
    n = len(ws)
    depth = ws[0].shape[0]
    quarters = 4

    def spec(a):
        per_layer = a.shape[0] == depth
        split = a.ndim >= 3 and a.shape[1] % quarters == 0 and a.shape[1] >= quarters
        block = (1, a.shape[1] // quarters if split else a.shape[1]) + a.shape[2:]
        rest = (0,) * (a.ndim - 2)
        return pl.BlockSpec(block, lambda l, s: ((l if per_layer else 0), (s if split else 0)) + rest)

    def body(*refs):
        w_refs, g_refs, m_refs, v_refs = (refs[k * n:(k + 1) * n] for k in range(4))
        d_refs, nm_refs, nv_refs = (refs[(4 + k) * n:(5 + k) * n] for k in range(3))
        for k in range(n):
            d_refs[k][...], nm_refs[k][...], nv_refs[k][...] = _adamw_math(
                w_refs[k][...], g_refs[k][...], m_refs[k][...], v_refs[k][...])

    specs = [spec(a) for a in ws]
    shapes = [jax.ShapeDtypeStruct(a.shape, F32) for a in ws]
    res = pl.pallas_call(
        body, name="adamw_small",
        grid=(depth, quarters),
        in_specs=specs * 4,
        out_specs=specs * 3,
        out_shape=shapes * 3,
        compiler_params=_params(dimension_semantics=("arbitrary", "arbitrary")),
    )(*ws, *gs, *ms, *vs)
    return res[:n], res[n:2 * n], res[2 * n:]


_PACK_ROWS = SUBLANES * N_DEV


def _pack(arrays):
    flat = jnp.concatenate([a.reshape(-1) for a in arrays])
    per = _PACK_ROWS * LANES
    total = -(-flat.shape[0] // per) * per
    flat = jnp.pad(flat, (0, total - flat.shape[0]))
    return flat.reshape(total // LANES, LANES)


def _unpack(packed, like):
    flat = packed.reshape(-1)
    out = []
    off = 0
    for a in like:
        out.append(flat[off:off + a.size].reshape(a.shape))
        off += a.size
    return out


def kernel(x, norm_g, w_in, pool_w, pool_scale, a_re, a_im, log_dt, b_re, b_im, c_re, c_im, d_skip, glu_w, glu_b, w_out, final_g, loss_target, m_norm_g, m_w_in, m_pool_w, m_pool_scale, m_a_re, m_a_im, m_log_dt, m_b_re, m_b_im, m_c_re, m_c_im, m_d_skip, m_glu_w, m_glu_b, m_w_out, m_final_g, v_norm_g, v_w_in, v_pool_w, v_pool_scale, v_a_re, v_a_im, v_log_dt, v_b_re, v_b_im, v_c_re, v_c_im, v_d_skip, v_glu_w, v_glu_b, v_w_out, v_final_g):
    nb, seq, _ = x.shape
    n_tok = nb * seq
    depth = norm_g.shape[0]

    my_idx = _index(_mesh_place())

    zones = _weight_zones(w_in, glu_w, w_out, my_idx)

    def gather_start(l, after):
        return _exchange_start((), zones[3 * l:3 * l + 3], after, f"comm_gather_start_{l}")

    def gather_wait(handle, after, l):
        _, (win, glu, wout) = _exchange_wait(handle, 3, after, f"comm_gather_wait_{l}")
        return win, glu.reshape(SSM_W, SSM_W), wout.reshape(MIX, D_MODEL)

    xs = [x.reshape(n_tok, D_MODEL)]
    first_w_in, dep = _exchange_start((), zones[0:1], xs[0], "comm_gather_start_0_w_in")

    (lbr, lbi, rb, rc), dense_vjp = jax.vjp(jax.vmap(_ssm_dense), a_re, a_im, log_dt + dep[0, 0], b_re, b_im, c_re, c_im)
    chunk_all = jax.vmap(_ssm_chunked)
    (wb, wct), chunk_vjp = jax.vjp(lambda p, q: (chunk_all(p), chunk_all(q)), rb, rc)
    wb_m, wct_m = _mx(wb), _mx(wct)
    pool_w_m = _mx(pool_w)

    def layer_params(l):
        return (pool_w_m[l], pool_scale[l][None], lbr[l], lbi[l], wb_m[l], wct_m[l], d_skip[l][None],
                weights[l][1], glu_b[l][None])

    saved = []
    weights = []
    for l in range(depth):
        if l == 0:
            _, (win,) = _exchange_wait(first_w_in, 1, wct_m, "comm_gather_wait_0_w_in")
            rest, dep = _exchange_start((), zones[1:3], win, "comm_gather_start_0_rest")
            z, h = _inproj_fwd(xs[-1], norm_g[l][None], win, dep)
            _, (glu, wout) = _exchange_wait(rest, 2, z, "comm_gather_wait_0_rest")
            weights.append((win, glu.reshape(SSM_W, SSM_W), wout.reshape(MIX, D_MODEL)))
            handle, dep = gather_start(1, weights[0][2])
            z3 = z.reshape(nb, seq, 2 * MIX)
            yg, states, x_next = _layer_fwd(xs[-1].reshape(nb, seq, D_MODEL), z3, None, None, *layer_params(l),
                                            weights[l][2], dep)
        else:
            weights.append(gather_wait(handle, xs[-1], l))
            if l + 1 < depth:
                handle, dep = gather_start(l + 1, weights[l][0])
            z3, h3, yg, states, x_next = _layer_fwd(xs[-1].reshape(nb, seq, D_MODEL), None, norm_g[l][None],
                                                    weights[l][0], *layer_params(l), weights[l][2], dep)
            h = h3.reshape(n_tok, D_MODEL)
        xs.append(x_next.reshape(n_tok, D_MODEL))
        saved.append((z3, h, yg.reshape(n_tok, MIX), states))

    dx, loss_part, d_final_g = _loss_head(xs[-1], loss_target.reshape(n_tok, D_MODEL), final_g[None])

    small = {k: [None] * depth for k in
             ("norm_g", "pool_w", "pool_scale", "lbr", "lbi", "wb", "wct", "d_skip", "glu_b")}
    received = [None] * depth
    sent = [None] * depth
    pending = None
    early = None
    for l in reversed(range(depth)):
        z3, h, yg2, states = saved[l]
        dy, d_wout = _outproj_bwd(dx, yg2, weights[l][2], dep)
        (dz, d_pw, d_ps, d_lbr, d_lbi, d_wb, d_wct, d_dsk, d_gw, d_gb) = _mixer_bwd(
            z3, dy.reshape(nb, seq, MIX), states, *layer_params(l))
        rest = (d_gw.reshape(N_DEV, SSM_W // N_DEV, SSM_W), d_wout.reshape(N_DEV, MIX // N_DEV, D_MODEL))
        if l == 0:
            early, dep = _exchange_start(rest, tuple(lax.empty(s.shape, s.dtype) for s in rest), dz,
                                         "comm_grads_start_0_rest")
        dx, d_win, d_ng = _inproj_bwd(dz.reshape(n_tok, 2 * MIX), h, xs[l], dx, norm_g[l][None], weights[l][0], dep)
        for k, val in (("norm_g", d_ng[0]), ("pool_w", d_pw), ("pool_scale", d_ps[0]), ("lbr", d_lbr),
                       ("lbi", d_lbi), ("wb", d_wb), ("wct", d_wct), ("d_skip", d_dsk[0]), ("glu_b", d_gb[0])):
            small[k][l] = val
        if pending is not None:
            sent[l + 1], received[l + 1] = _exchange_wait(pending, 3, dx, f"comm_grads_wait_{l + 1}")
        srcs = (d_win,) if l == 0 else (d_win,) + rest
        lands = tuple(lax.empty(s.shape, s.dtype) for s in srcs)
        pending, dep = _exchange_start(srcs, lands, dx, f"comm_grads_start_{l}")
    stack = lambda k: jnp.stack(small[k])
    d_rb, d_rc = chunk_vjp((stack("wb"), stack("wct")))
    local = [stack("norm_g"), stack("pool_w"), stack("pool_scale"), stack("lbr"), stack("lbi"), d_rb, d_rc,
             stack("d_skip"), stack("glu_b"), d_final_g[0] + dep[0, 0], loss_part[0]]
    (g_norm_g, g_pool_w, g_pool_scale, g_lbr, g_lbi, g_rb, g_rc, g_d_skip, g_glu_b, g_final_g, loss) = _unpack(
        _allreduce_packed(_pack(local)), local)
    loss = loss[0]
    g_a_re, g_a_im, g_log_dt, g_b_re, g_b_im, g_c_re, g_c_im = dense_vjp((g_lbr, g_lbi, g_rb, g_rc))

    names = ["norm_g", "pool_w", "pool_scale", "a_re", "a_im", "log_dt", "b_re", "b_im", "c_re", "c_im",
             "d_skip", "glu_b", "final_g"]
    rows = {"norm_g", "pool_scale", "log_dt", "d_skip", "glu_b"}
    small_w = [norm_g, pool_w, pool_scale, a_re, a_im, log_dt, b_re, b_im, c_re, c_im, d_skip, glu_b, final_g]
    small_g = [g_norm_g, g_pool_w, g_pool_scale, g_a_re, g_a_im, g_log_dt, g_b_re, g_b_im, g_c_re, g_c_im,
               g_d_skip, g_glu_b, g_final_g]
    small_m = [m_norm_g, m_pool_w, m_pool_scale, m_a_re, m_a_im, m_log_dt, m_b_re, m_b_im, m_c_re, m_c_im,
               m_d_skip, m_glu_b, m_final_g]
    small_v = [v_norm_g, v_pool_w, v_pool_scale, v_a_re, v_a_im, v_log_dt, v_b_re, v_b_im, v_c_re, v_c_im,
               v_d_skip, v_glu_b, v_final_g]

    wide_last = {"b_re", "b_im"}

    def blocked(arrays):
        return [a.reshape(1, 1, -1) if n == "final_g" else a[:, None, :] if n in rows
                else a.swapaxes(2, 3) if n in wide_last else a for n, a in zip(names, arrays)]

    small_d, small_nm, small_nv = _adamw_small(blocked(small_w), blocked(small_g), blocked(small_m), blocked(small_v))
    res = {}
    for kind, arrays in (("grad", small_g), ("delta", small_d), ("m", small_nm), ("v", small_nv)):
        for n, a, like in zip(names, arrays, small_w):
            if kind != "grad" and n in wide_last:
                a = a.swapaxes(2, 3)
            res[kind, n] = a.reshape(like.shape)

    (s_win,), (r_win,) = _exchange_wait(pending, 1, small_d[0], "comm_grads_wait_0")
    (s_glu, s_wout), (r_glu, r_wout) = _exchange_wait(early, 2, small_d[0], "comm_grads_wait_0_rest")
    sent[0], received[0] = (s_win, s_glu, s_wout), (r_win, r_glu, r_wout)
    shard_res = {}
    for pos, (n, w, m, v) in enumerate((("w_in", w_in, m_w_in, v_w_in), ("glu_w", glu_w, m_glu_w, v_glu_w),
                                        ("w_out", w_out, m_w_out, v_w_out))):
        shard_res[n] = _adamw_summed([received[l][pos] for l in range(depth)], [sent[l][pos] for l in range(depth)],
                                     my_idx, w, m, v, "adamw_" + n)
    for n in ("w_in", "glu_w", "w_out"):
        for pos, kind in enumerate(("grad", "delta", "m", "v")):
            res[kind, n] = shard_res[n][pos]

    order = ["norm_g", "w_in", "pool_w", "pool_scale", "a_re", "a_im", "log_dt", "b_re", "b_im", "c_re", "c_im",
             "d_skip", "glu_w", "glu_b", "w_out", "final_g"]
    outs = [loss, dx.reshape(nb, seq, D_MODEL)]
    for kind in ("grad", "delta", "m", "v"):
        outs += [res[kind, n] for n in order]
    return tuple(outs)
```

```python
import math

import jax
import jax.numpy as jnp
from jax import lax
from jax.experimental import pallas as pl
from jax.experimental.pallas import tpu as pltpu

F32 = jnp.float32
MXU_DTYPE = jnp.bfloat16

D_MODEL = 1024
MIX = 1024
POOL_W = 512
SSM_W = 512
N_POOL_G = 4
POOL_GC = 128
SSM_C = 16
SSM_P = 64
NORM_EPS = 1e-5
N_DEV = 8
W_IN_COLS = 2 * MIX // N_DEV

ADAM_LR = 0.001
ADAM_B1 = 0.9
ADAM_B2 = 0.999
ADAM_EPS = 1e-08
ADAM_WD = 0.01
ADAM_STEP = 10

SUBLANES = 8
LANES = 128
HALO = 16
STATE_ROWS = 8
STATE_COLS = 256
CHUNK_GROUPS = STATE_COLS // SSM_P
CHUNK_CH = CHUNK_GROUPS * SSM_C
T_BLK = 256
TM_FWD = 512
TM_BWD = 512
VMEM_LIMIT = 56 * 1024 * 1024

MESH = pl.DeviceIdType.MESH
VMEM_SPEC = pl.BlockSpec(memory_space=pltpu.VMEM)
ANY_SPEC = pl.BlockSpec(memory_space=pl.ANY)


def _mm(a, b):
    return jnp.dot(a, b, preferred_element_type=F32)


def _mm_tn(a, b):
    return lax.dot_general(a, b, (((0,), (0,)), ((), ())), preferred_element_type=F32)


def _mm_nt(a, b):
    return lax.dot_general(a, b, (((1,), (1,)), ((), ())), preferred_element_type=F32)


def _mx(a):
    return a.astype(MXU_DTYPE)


def _sigmoid(v):
    return 1.0 / (1.0 + jnp.exp(-v))


_GELU_C = math.sqrt(2.0 / math.pi)
_GELU_A = 0.044715


def _gelu_and_grad(y):
    th = jnp.tanh(_GELU_C * (y + _GELU_A * y * y * y))
    val = 0.5 * y * (1.0 + th)
    grad = 0.5 * (1.0 + th) + 0.5 * y * (1.0 - th * th) * (_GELU_C * (1.0 + 3.0 * _GELU_A * y * y))
    return val, grad


def _params(**kw):
    return pltpu.CompilerParams(vmem_limit_bytes=VMEM_LIMIT, **kw)


def _ssm_dense(a_re, a_im, log_dt, b_re, b_im, c_re, c_im):
    dt = jnp.exp(log_dt)[:, None]
    mag = jnp.exp(a_re * dt)
    ang = a_im * dt
    lb_re = mag * jnp.cos(ang)
    lb_im = mag * jnp.sin(ang)
    den = a_re * a_re + a_im * a_im
    n_re = lb_re - 1.0
    n_im = lb_im
    f_re = (n_re * a_re + n_im * a_im) / den
    f_im = (n_im * a_re - n_re * a_im) / den
    bb_re = f_re[..., None] * b_re - f_im[..., None] * b_im
    bb_im = f_re[..., None] * b_im + f_im[..., None] * b_re

    bb = jnp.stack([bb_re, bb_im], axis=0).reshape(2, STATE_ROWS, CHUNK_GROUPS, SSM_P, SSM_C)
    rb = bb.transpose(1, 4, 0, 2, 3).reshape(STATE_ROWS, SSM_C, 2 * STATE_COLS)
    cc = jnp.stack([c_re, -c_im], axis=0).reshape(2, STATE_ROWS, CHUNK_GROUPS, SSM_C, SSM_P)
    rc = cc.transpose(1, 3, 0, 2, 4).reshape(STATE_ROWS, SSM_C, 2 * STATE_COLS)
    return (lb_re.reshape(STATE_ROWS, STATE_COLS), lb_im.reshape(STATE_ROWS, STATE_COLS), rb, rc)


def _ssm_chunked(per_channel):
    row_group = jnp.arange(CHUNK_CH) // SSM_C
    col_group = (jnp.arange(2 * STATE_COLS) // SSM_P) % CHUNK_GROUPS
    own_group = (row_group[:, None] == col_group[None, :]).astype(F32)
    even = (jnp.arange(STATE_ROWS) % 2 == 0).astype(F32)[:, None, None]
    half = jnp.tile(per_channel, (1, CHUNK_GROUPS, 1)) * own_group
    return jnp.concatenate([half * even, half * (1.0 - even)], axis=1)


def _inproj_fwd(x2, g_row, w_all, dep):
    n = x2.shape[0]
    tm = TM_FWD

    def body(x_ref, g_ref, w_ref, dep_ref, z_ref, h_ref):
        x = x_ref[...]
        r = lax.rsqrt(jnp.mean(x * x, axis=-1, keepdims=True) + NORM_EPS)
        h = _mx(x * r * g_ref[...])
        h_ref[...] = h
        for d in range(N_DEV):
            z_ref[:, d * W_IN_COLS:(d + 1) * W_IN_COLS] = _mm(h, w_ref[d])

    return pl.pallas_call(
        body, name="inproj_fwd",
        grid=(n // tm,),
        in_specs=[pl.BlockSpec((tm, D_MODEL), lambda i: (i, 0)),
                  pl.BlockSpec((1, D_MODEL), lambda i: (0, 0)),
                  pl.BlockSpec((N_DEV, D_MODEL, W_IN_COLS), lambda i: (0, 0, 0)),
                  ANY_SPEC],
        out_specs=[pl.BlockSpec((tm, 2 * MIX), lambda i: (i, 0)),
                   pl.BlockSpec((tm, D_MODEL), lambda i: (i, 0))],
        out_shape=[jax.ShapeDtypeStruct((n, 2 * MIX), F32),
                   jax.ShapeDtypeStruct((n, D_MODEL), MXU_DTYPE)],
        compiler_params=_params(dimension_semantics=("arbitrary",)),
    )(x2, g_row, w_all, dep)


def _loss_head(x2, tgt2, g_row):
    n = x2.shape[0]
    tm = TM_FWD

    def body(x_ref, t_ref, g_ref, dx_ref, loss_ref, dg_ref):
        @pl.when(pl.program_id(0) == 0)
        def _():
            loss_ref[...] = jnp.zeros_like(loss_ref)
            dg_ref[...] = jnp.zeros_like(dg_ref)

        x = x_ref[...]
        g = g_ref[...]
        r = lax.rsqrt(jnp.mean(x * x, axis=-1, keepdims=True) + NORM_EPS)
        xh = x * r
        e = xh * g - t_ref[...]
        loss_ref[...] += jnp.sum(jnp.sum(e * e, axis=-1, keepdims=True), axis=0, keepdims=True) * (0.5 / D_MODEL)
        dout = e * (1.0 / D_MODEL)
        dg_ref[...] += jnp.sum(dout * xh, axis=0, keepdims=True)
        gdy = dout * g
        dx_ref[...] = r * (gdy - xh * jnp.mean(xh * gdy, axis=-1, keepdims=True))

    return pl.pallas_call(
        body, name="loss_head",
        grid=(n // tm,),
        in_specs=[pl.BlockSpec((tm, D_MODEL), lambda i: (i, 0)),
                  pl.BlockSpec((tm, D_MODEL), lambda i: (i, 0)),
                  pl.BlockSpec((1, D_MODEL), lambda i: (0, 0))],
        out_specs=[pl.BlockSpec((tm, D_MODEL), lambda i: (i, 0)),
                   pl.BlockSpec((1, 1), lambda i: (0, 0)),
                   pl.BlockSpec((1, D_MODEL), lambda i: (0, 0))],
        out_shape=[jax.ShapeDtypeStruct((n, D_MODEL), F32),
                   jax.ShapeDtypeStruct((1, 1), F32),
                   jax.ShapeDtypeStruct((1, D_MODEL), F32)],
        compiler_params=_params(dimension_semantics=("arbitrary",)),
    )(x2, tgt2, g_row)


def _outproj_bwd(dx2, yg, w_out, dep):
    n = dx2.shape[0]
    tm = TM_BWD
    n_steps = n // tm

    def body(dx_ref, y_ref, w_ref, dep_ref, dy_ref, dw_ref, acc_ref):
        i = pl.program_id(0)

        @pl.when(i == 0)
        def _():
            acc_ref[...] = jnp.zeros_like(acc_ref)

        dxb = _mx(dx_ref[...])
        dy_ref[...] = _mm_nt(dxb, w_ref[...])
        acc_ref[...] += _mm_tn(y_ref[...], dxb)

        @pl.when(i == n_steps - 1)
        def _():
            dw_ref[...] = _mx(acc_ref[...])

    return pl.pallas_call(
        body, name="outproj_bwd",
        grid=(n_steps,),
        in_specs=[pl.BlockSpec((tm, D_MODEL), lambda i: (i, 0)),
                  pl.BlockSpec((tm, MIX), lambda i: (i, 0)),
                  pl.BlockSpec((MIX, D_MODEL), lambda i: (0, 0)),
                  ANY_SPEC],
        out_specs=[pl.BlockSpec((tm, MIX), lambda i: (i, 0)),
                   pl.BlockSpec((MIX, D_MODEL), lambda i: (0, 0))],
        out_shape=[jax.ShapeDtypeStruct((n, MIX), F32),
                   jax.ShapeDtypeStruct((MIX, D_MODEL), MXU_DTYPE)],
        scratch_shapes=[pltpu.VMEM((MIX, D_MODEL), F32)],
        compiler_params=_params(dimension_semantics=("arbitrary",)),
    )(dx2, yg, w_out, dep)


def _inproj_bwd(dz, h, x2, dx_in, g_row, w_all, dep):
    n = x2.shape[0]
    tm = TM_BWD
    n_steps = n // tm

    def body(dz_ref, h_ref, x_ref, dxi_ref, g_ref, w_ref, dep_ref, dxo_ref, dw_ref, dg_ref, acc_ref, wcat_ref):
        i = pl.program_id(0)

        @pl.when(i == 0)
        def _():
            acc_ref[...] = jnp.zeros_like(acc_ref)
            dg_ref[...] = jnp.zeros_like(dg_ref)
            for d in range(N_DEV):
                wcat_ref[:, d * W_IN_COLS:(d + 1) * W_IN_COLS] = w_ref[d]

        hb = h_ref[...]
        for d in range(N_DEV):
            acc_ref[d] += _mm_tn(hb, dz_ref[:, d * W_IN_COLS:(d + 1) * W_IN_COLS])
        dh = _mm_nt(dz_ref[...], wcat_ref[...])
        x = x_ref[...]
        r = lax.rsqrt(jnp.mean(x * x, axis=-1, keepdims=True) + NORM_EPS)
        xh = x * r
        dg_ref[...] += jnp.sum(dh * xh, axis=0, keepdims=True)
        gdy = dh * g_ref[...]
        dxo_ref[...] = dxi_ref[...] + r * (gdy - xh * jnp.mean(xh * gdy, axis=-1, keepdims=True))

        @pl.when(i == n_steps - 1)
        def _():
            dw_ref[...] = _mx(acc_ref[...])

    return pl.pallas_call(
        body, name="inproj_bwd",
        grid=(n_steps,),
        in_specs=[pl.BlockSpec((tm, 2 * MIX), lambda i: (i, 0)),
                  pl.BlockSpec((tm, D_MODEL), lambda i: (i, 0)),
                  pl.BlockSpec((tm, D_MODEL), lambda i: (i, 0)),
                  pl.BlockSpec((tm, D_MODEL), lambda i: (i, 0)),
                  pl.BlockSpec((1, D_MODEL), lambda i: (0, 0)),
                  pl.BlockSpec((N_DEV, D_MODEL, W_IN_COLS), lambda i: (0, 0, 0)),
                  ANY_SPEC],
        out_specs=[pl.BlockSpec((tm, D_MODEL), lambda i: (i, 0)),
                   pl.BlockSpec((N_DEV, D_MODEL, W_IN_COLS), lambda i: (0, 0, 0)),
                   pl.BlockSpec((1, D_MODEL), lambda i: (0, 0))],
        out_shape=[jax.ShapeDtypeStruct((n, D_MODEL), F32),
                   jax.ShapeDtypeStruct((N_DEV, D_MODEL, W_IN_COLS), MXU_DTYPE),
                   jax.ShapeDtypeStruct((1, D_MODEL), F32)],
        scratch_shapes=[pltpu.VMEM((N_DEV, D_MODEL, W_IN_COLS), F32),
                        pltpu.VMEM((D_MODEL, 2 * MIX), MXU_DTYPE)],
        compiler_params=_params(dimension_semantics=("arbitrary",)),
    )(dz, h, x2, dx_in, g_row, w_all, dep)


def _row_pos(t0, rows):
    return t0 + lax.broadcasted_iota(jnp.int32, (rows, LANES), 0)


def _pool_window_mean(upad, g, t0, t_blk):
    k = 2 << g
    w = upad
    sh = 1
    while sh < k:
        w = w + pltpu.roll(w, sh, 0)
        sh *= 2
    count = jnp.minimum(_row_pos(t0, t_blk) + 1, k).astype(F32)
    return w[HALO:] / count - upad[HALO:]


def _pool_window_bwd(qpad, g, t_blk):
    k = 2 << g
    n = t_blk + HALO
    w = qpad
    sh = 1
    while sh < k:
        w = w + pltpu.roll(w, n - sh, 0)
        sh *= 2
    return w[:t_blk]


class _StateBuf:
    def __init__(self, refs, t_blk):
        self.refs = refs
        self.t_blk = t_blk

    def put_chunk(self, b, j, val):
        for c in range(4):
            self.refs[4 * b + c][pl.ds(j, self.t_blk, stride=STATE_ROWS), :] = val[:, c * LANES:(c + 1) * LANES]

    def get_chunk(self, b, j):
        return jnp.concatenate(
            [self.refs[4 * b + c][pl.ds(j, self.t_blk, stride=STATE_ROWS), :] for c in range(4)], axis=-1)

    def load(self, b, r, part):
        return jnp.concatenate(
            [self.refs[4 * b + 2 * part + h][pl.ds(r, STATE_ROWS), :] for h in range(2)], axis=-1)

    def store(self, b, r, part, val):
        for h in range(2):
            self.refs[4 * b + 2 * part + h][pl.ds(r, STATE_ROWS), :] = val[:, h * LANES:(h + 1) * LANES]


def _state_scratch(nb, t_blk):
    return [pltpu.VMEM((t_blk * STATE_ROWS, LANES), F32) for _ in range(4 * nb)]


def _ssm_project_in(u_ssm, wb_ref, buf, nb):
    t_blk = u_ssm.shape[0] // nb
    ub = _mx(u_ssm)
    for j in range(STATE_ROWS):
        m = j // 2
        bu = _mm(ub[:, m * LANES:(m + 1) * LANES], wb_ref[j])
        for b in range(nb):
            buf.put_chunk(b, j, bu[b * t_blk:(b + 1) * t_blk])


def _scan_forward(buf, lbr, lbi, init, nb):
    def body(t, carry):
        r = pl.multiple_of(t * STATE_ROWS, STATE_ROWS)
        out = []
        for b in range(nb):
            sr, si = carry[2 * b], carry[2 * b + 1]
            nr = lbr * sr - lbi * si + buf.load(b, r, 0)
            ni = lbr * si + lbi * sr + buf.load(b, r, 1)
            buf.store(b, r, 0, nr)
            buf.store(b, r, 1, ni)
            out += [nr, ni]
        return tuple(out)

    return lax.fori_loop(0, buf.t_blk, body, init, unroll=4)


def _ssm_project_out(chunk, wc_ref):
    tiles = []
    for m in range(4):
        acc = None
        for j in (2 * m, 2 * m + 1):
            part = _mm_nt(chunk(j), wc_ref[j])
            acc = part if acc is None else acc + part
        tiles.append(acc)
    return jnp.concatenate(tiles, axis=-1)


def _layer_fwd(x3, z3, g_row, w_in, pool_w, pool_scale, lbr, lbi, wb, wc, d_skip, glu_w, glu_b, w_out, dep):
    nb, seq, _ = x3.shape
    t_blk = min(T_BLK, seq)
    n_t = seq // t_blk
    halo_per_blk = t_blk // HALO
    rows = nb * t_blk
    fused = z3 is None

    def body(*refs):
        if fused:
            (x_ref, g_ref, wi_ref, pw_ref, ps_ref, lbr_ref, lbi_ref, wb_ref, wc_ref, dsk_ref, gw_ref, gb_ref, wo_ref,
             dep_ref, z_ref, h_ref, yg_ref, sc_ref, xo_ref, carry_ref, halo_ref, *s_refs) = refs
        else:
            (x_ref, z_ref, zh_ref, pw_ref, ps_ref, lbr_ref, lbi_ref, wb_ref, wc_ref, dsk_ref, gw_ref, gb_ref, wo_ref,
             dep_ref, yg_ref, sc_ref, xo_ref, carry_ref, *s_refs) = refs
        i = pl.program_id(0)
        t0 = i * t_blk
        buf = _StateBuf(s_refs, t_blk)
        both = lambda lo, hi: z_ref[:, :, lo:hi].reshape(rows, hi - lo)

        @pl.when(i == 0)
        def _():
            carry_ref[...] = jnp.zeros_like(carry_ref)
            if fused:
                halo_ref[...] = jnp.zeros_like(halo_ref)

        x = x_ref[...].reshape(rows, D_MODEL)
        if fused:
            r = lax.rsqrt(jnp.mean(x * x, axis=-1, keepdims=True) + NORM_EPS)
            h = _mx(x * r * g_ref[...])
            h_ref[...] = h.reshape(nb, t_blk, D_MODEL)
            for d in range(N_DEV):
                z_ref[:, :, d * W_IN_COLS:(d + 1) * W_IN_COLS] = _mm(h, wi_ref[d]).reshape(nb, t_blk, W_IN_COLS)

        u_ssm = both(POOL_W, MIX)
        _ssm_project_in(u_ssm, wb_ref, buf, nb)
        init = tuple(carry_ref[b, :, h * STATE_COLS:(h + 1) * STATE_COLS] for b in range(nb) for h in range(2))
        fin = _scan_forward(buf, lbr_ref[...], lbi_ref[...], init, nb)
        for b in range(nb):
            carry_ref[b, :, 0:STATE_COLS] = fin[2 * b]
            carry_ref[b, :, STATE_COLS:2 * STATE_COLS] = fin[2 * b + 1]

        def chunk(j):
            states = _mx(jnp.concatenate([buf.get_chunk(b, j) for b in range(nb)], axis=0))
            sc_ref[:, j] = states.reshape(nb, t_blk, 2 * STATE_COLS)
            return states

        y = _ssm_project_out(chunk, wc_ref) + dsk_ref[...] * u_ssm
        yg, _ = _gelu_and_grad(y)
        o_ssm = yg * _sigmoid(_mm(_mx(yg), gw_ref[...]) + gb_ref[...])
        gp = both(MIX + POOL_W, 2 * MIX)
        parts = []
        first = (i == 0)
        for g in range(N_POOL_G):
            cols = slice(g * POOL_GC, (g + 1) * POOL_GC)
            pooled = []
            for b in range(nb):
                halo = halo_ref[b, :, cols] if fused else jnp.where(first, 0.0, zh_ref[b, :, cols])
                pooled.append(_pool_window_mean(jnp.concatenate([halo, z_ref[b, :, cols]], axis=0), g, t0, t_blk))
            yp = _mm(_mx(jnp.concatenate(pooled, axis=0)), pw_ref[g]) * ps_ref[:, cols]
            gpp = both(MIX + g * POOL_GC, MIX + (g + 1) * POOL_GC)
            parts.append(_mx(yp * (gpp * _sigmoid(gpp))))
        parts.append(_mx(o_ssm * (gp * _sigmoid(gp))))
        gated = jnp.concatenate(parts, axis=-1)
        yg_ref[...] = gated.reshape(nb, t_blk, MIX)
        xo_ref[...] = (x + _mm(gated, wo_ref[...])).reshape(nb, t_blk, D_MODEL)
        if fused:
            halo_ref[...] = z_ref[:, t_blk - HALO:, 0:POOL_W]

    const = lambda *shape: pl.BlockSpec(shape, lambda i: (0,) * len(shape))
    tokens = lambda width: pl.BlockSpec((nb, t_blk, width), lambda i: (0, i, 0))
    mixer_specs = [const(N_POOL_G, POOL_GC, POOL_GC), const(1, POOL_W),
                   const(STATE_ROWS, STATE_COLS), const(STATE_ROWS, STATE_COLS),
                   const(STATE_ROWS, LANES, 2 * STATE_COLS), const(STATE_ROWS, LANES, 2 * STATE_COLS),
                   const(1, SSM_W), const(SSM_W, SSM_W), const(1, SSM_W), const(MIX, D_MODEL), ANY_SPEC]
    mixer_args = (pool_w, pool_scale, lbr, lbi, wb, wc, d_skip, glu_w, glu_b, w_out, dep)
    out_specs = [tokens(MIX), pl.BlockSpec((nb, STATE_ROWS, t_blk, 2 * STATE_COLS), lambda i: (0, 0, i, 0)),
                 tokens(D_MODEL)]
    out_shape = [jax.ShapeDtypeStruct((nb, seq, MIX), MXU_DTYPE),
                 jax.ShapeDtypeStruct((nb, STATE_ROWS, seq, 2 * STATE_COLS), MXU_DTYPE),
                 jax.ShapeDtypeStruct((nb, seq, D_MODEL), F32)]
    scratch = [pltpu.VMEM((nb, STATE_ROWS, 2 * STATE_COLS), F32)]
    if fused:
        in_specs = [tokens(D_MODEL), const(1, D_MODEL), const(N_DEV, D_MODEL, W_IN_COLS)] + mixer_specs
        args = (x3, g_row, w_in) + mixer_args
        out_specs = [tokens(2 * MIX), tokens(D_MODEL)] + out_specs
        out_shape = [jax.ShapeDtypeStruct((nb, seq, 2 * MIX), F32),
                     jax.ShapeDtypeStruct((nb, seq, D_MODEL), MXU_DTYPE)] + out_shape
        scratch = scratch + [pltpu.VMEM((nb, HALO, POOL_W), F32)]
    else:
        in_specs = [tokens(D_MODEL), tokens(2 * MIX),
                    pl.BlockSpec((nb, HALO, POOL_W), lambda i: (0, jnp.maximum(i * halo_per_blk - 1, 0), 0))] + mixer_specs
        args = (x3, z3, z3) + mixer_args
    return pl.pallas_call(
        body, name="layer_fwd" if fused else "mixer_fwd",
        grid=(n_t,),
        in_specs=in_specs, out_specs=out_specs, out_shape=out_shape,
        scratch_shapes=scratch + _state_scratch(nb, t_blk),
        compiler_params=_params(dimension_semantics=("arbitrary",)),
    )(*args)


def _mixer_bwd(z3, dy3, states, pool_w, pool_scale, lbr, lbi, wb, wc, d_skip, glu_w, glu_b):
    nb, seq, _ = z3.shape
    t_blk = min(T_BLK, seq)
    n_t = seq // t_blk
    halo_per_blk = t_blk // HALO
    rows = nb * t_blk

    def body(z_ref, zh_ref, dy_ref, sc_ref, sch_ref, pw_ref, ps_ref, lbr_ref, lbi_ref, wb_ref, wc_ref, dsk_ref,
             gw_ref, gb_ref,
             dz_ref, dpw_ref, dps_ref, dlbr_ref, dlbi_ref, dwb_ref, dwc_ref, ddsk_ref, dgw_ref, dgb_ref,
             gcarry_ref, qcarry_ref, du_ref, dgw_acc, *g_refs):
        i = pl.program_id(0)
        blk = n_t - 1 - i
        t0 = blk * t_blk
        gbuf = _StateBuf(g_refs, t_blk)

        @pl.when(i == 0)
        def _():
            gcarry_ref[...] = jnp.zeros_like(gcarry_ref)
            qcarry_ref[...] = jnp.zeros_like(qcarry_ref)
            for ref in (dpw_ref, dps_ref, dlbr_ref, dlbi_ref, dwb_ref, dwc_ref, ddsk_ref, dgw_acc, dgb_ref):
                ref[...] = jnp.zeros_like(ref)

        lbr_v = lbr_ref[...]
        lbi_v = lbi_ref[...]

        both = lambda ref, lo, hi: ref[:, :, lo:hi].reshape(rows, hi - lo)
        split = lambda val: val.reshape(nb, t_blk, val.shape[-1])
        states = lambda j: sc_ref[:, j].reshape(rows, 2 * STATE_COLS)
        first = (blk == 0)

        u_ssm = both(z_ref, POOL_W, MIX)
        y = _ssm_project_out(states, wc_ref) + dsk_ref[...] * u_ssm
        yg, dgelu = _gelu_and_grad(y)
        ygb = _mx(yg)
        sg = _sigmoid(_mm(ygb, gw_ref[...]) + gb_ref[...])
        o_ssm = yg * sg
        gp = both(z_ref, MIX + POOL_W, 2 * MIX)
        sgm = _sigmoid(gp)
        dyv = both(dy_ref, POOL_W, MIX)
        dz_ref[:, :, MIX + POOL_W:2 * MIX] = split(_mx(dyv * o_ssm * (sgm * (1.0 + gp * (1.0 - sgm)))))
        do = dyv * (gp * sgm)
        dv = do * yg * (sg * (1.0 - sg))
        dvb = _mx(dv)
        dgb_ref[...] += jnp.sum(dv, axis=0, keepdims=True)
        dgw_acc[...] += _mm_tn(ygb, dvb)
        dyp = (do * sg + _mm_nt(dvb, gw_ref[...])) * dgelu
        ddsk_ref[...] += jnp.sum(dyp * u_ssm, axis=0, keepdims=True)
        dypb = _mx(dyp)
        for j in range(STATE_ROWS):
            m = j // 2
            dyt = dypb[:, m * LANES:(m + 1) * LANES]
            ds = _mm(dyt, wc_ref[j])
            for b in range(nb):
                gbuf.put_chunk(b, j, ds[b * t_blk:(b + 1) * t_blk])
            dwc_ref[j] += _mm_tn(dyt, states(j))
        du_ref[...] = split(dsk_ref[...] * dyp)

        for g in range(N_POOL_G):
            cols = slice(g * POOL_GC, (g + 1) * POOL_GC)
            pooled = []
            for b in range(nb):
                halo = jnp.where(first, 0.0, zh_ref[b, :, cols])
                pooled.append(_pool_window_mean(jnp.concatenate([halo, z_ref[b, :, cols]], axis=0), g, t0, t_blk))
            pb = _mx(jnp.concatenate(pooled, axis=0))
            ypre = _mm(pb, pw_ref[g])
            gpp = both(z_ref, MIX + g * POOL_GC, MIX + (g + 1) * POOL_GC)
            sgp = _sigmoid(gpp)
            dyg = both(dy_ref, g * POOL_GC, (g + 1) * POOL_GC)
            scale = ps_ref[:, cols]
            dz_ref[:, :, MIX + g * POOL_GC:MIX + (g + 1) * POOL_GC] = split(_mx(
                dyg * (ypre * scale) * (sgp * (1.0 + gpp * (1.0 - sgp)))))
            dyc = dyg * (gpp * sgp)
            dps_ref[:, cols] += jnp.sum(dyc * ypre, axis=0, keepdims=True)
            dypre = _mx(dyc * scale)
            dpw_ref[g] += _mm_tn(pb, dypre)
            dpooled = _mm_nt(dypre, pw_ref[g])
            count = jnp.minimum(_row_pos(t0, t_blk) + 1, 2 << g).astype(F32)
            for b in range(nb):
                dp = dpooled[b * t_blk:(b + 1) * t_blk]
                q = dp / count
                qpad = jnp.concatenate([q, qcarry_ref[b, :, cols]], axis=0)
                qcarry_ref[b, :, cols] = q[:HALO]
                dz_ref[b, :, cols] = _mx(_pool_window_bwd(qpad, g, t_blk) - dp)

        def rev_body(k, carry):
            r = pl.multiple_of((t_blk - 1 - k) * STATE_ROWS, STATE_ROWS)
            out = []
            for b in range(nb):
                gr, gi = carry[2 * b], carry[2 * b + 1]
                ngr = lbr_v * gr + lbi_v * gi + gbuf.load(b, r, 0)
                ngi = lbr_v * gi - lbi_v * gr + gbuf.load(b, r, 1)
                gbuf.store(b, r, 0, ngr)
                gbuf.store(b, r, 1, ngi)
                out += [ngr, ngi]
            return tuple(out)

        init_g = tuple(gcarry_ref[b, :, h * STATE_COLS:(h + 1) * STATE_COLS] for b in range(nb) for h in range(2))
        fin = lax.fori_loop(0, t_blk, rev_body, init_g, unroll=4)
        for b in range(nb):
            gcarry_ref[b, :, 0:STATE_COLS] = fin[2 * b]
            gcarry_ref[b, :, STATE_COLS:2 * STATE_COLS] = fin[2 * b + 1]

        ub = _mx(u_ssm)
        for m in range(4):
            acc = both(du_ref, m * LANES, (m + 1) * LANES)
            for j in (2 * m, 2 * m + 1):
                g = jnp.concatenate([gbuf.get_chunk(b, j) for b in range(nb)], axis=0)
                gj = _mx(g)
                acc = acc + _mm_nt(gj, wb_ref[j])
                dwb_ref[j] += _mm_tn(ub[:, m * LANES:(m + 1) * LANES], gj)
                shifted = []
                for b in range(nb):
                    before = jnp.where(first, 0.0, sch_ref[b, j].astype(F32))
                    spad = jnp.concatenate([before, sc_ref[b, j].astype(F32)], axis=0)
                    shifted.append(pltpu.roll(spad, 1, 0)[HALO:])
                s_prev = jnp.concatenate(shifted, axis=0)
                g_re, g_im = g[:, :STATE_COLS], g[:, STATE_COLS:]
                p_re, p_im = s_prev[:, :STATE_COLS], s_prev[:, STATE_COLS:]
                dlbr_ref[j:j + 1, :] += jnp.sum(g_re * p_re + g_im * p_im, axis=0, keepdims=True)
                dlbi_ref[j:j + 1, :] += jnp.sum(g_im * p_re - g_re * p_im, axis=0, keepdims=True)
            dz_ref[:, :, POOL_W + m * LANES:POOL_W + (m + 1) * LANES] = split(_mx(acc))

        @pl.when(i == n_t - 1)
        def _():
            dgw_ref[...] = _mx(dgw_acc[...])

    const = lambda *shape: pl.BlockSpec(shape, lambda i: (0,) * len(shape))
    rev = lambda i: n_t - 1 - i
    out_shape = [jax.ShapeDtypeStruct((nb, seq, 2 * MIX), MXU_DTYPE),
                 jax.ShapeDtypeStruct((N_POOL_G, POOL_GC, POOL_GC), F32),
                 jax.ShapeDtypeStruct((1, POOL_W), F32),
                 jax.ShapeDtypeStruct((STATE_ROWS, STATE_COLS), F32),
                 jax.ShapeDtypeStruct((STATE_ROWS, STATE_COLS), F32),
                 jax.ShapeDtypeStruct((STATE_ROWS, LANES, 2 * STATE_COLS), F32),
                 jax.ShapeDtypeStruct((STATE_ROWS, LANES, 2 * STATE_COLS), F32),
                 jax.ShapeDtypeStruct((1, SSM_W), F32),
                 jax.ShapeDtypeStruct((SSM_W, SSM_W), MXU_DTYPE),
                 jax.ShapeDtypeStruct((1, SSM_W), F32)]
    return pl.pallas_call(
        body, name="mixer_bwd",
        grid=(n_t,),
        in_specs=[pl.BlockSpec((nb, t_blk, 2 * MIX), lambda i: (0, rev(i), 0)),
                  pl.BlockSpec((nb, HALO, POOL_W), lambda i: (0, jnp.maximum(rev(i) * halo_per_blk - 1, 0), 0)),
                  pl.BlockSpec((nb, t_blk, MIX), lambda i: (0, rev(i), 0)),
                  pl.BlockSpec((nb, STATE_ROWS, t_blk, 2 * STATE_COLS), lambda i: (0, 0, rev(i), 0)),
                  pl.BlockSpec((nb, STATE_ROWS, HALO, 2 * STATE_COLS),
                               lambda i: (0, 0, jnp.maximum(rev(i) * halo_per_blk - 1, 0), 0)),
                  const(N_POOL_G, POOL_GC, POOL_GC), const(1, POOL_W),
                  const(STATE_ROWS, STATE_COLS), const(STATE_ROWS, STATE_COLS),
                  const(STATE_ROWS, LANES, 2 * STATE_COLS), const(STATE_ROWS, LANES, 2 * STATE_COLS),
                  const(1, SSM_W), const(SSM_W, SSM_W), const(1, SSM_W)],
        out_specs=[pl.BlockSpec((nb, t_blk, 2 * MIX), lambda i: (0, rev(i), 0))]
                  + [const(*s.shape) for s in out_shape[1:]],
        out_shape=out_shape,
        scratch_shapes=[pltpu.VMEM((nb, STATE_ROWS, 2 * STATE_COLS), F32),
                        pltpu.VMEM((nb, HALO, POOL_W), F32),
                        pltpu.VMEM((nb, t_blk, SSM_W), F32),
                        pltpu.VMEM((SSM_W, SSM_W), F32)]
                       + _state_scratch(nb, t_blk),
        compiler_params=_params(dimension_semantics=("arbitrary",)),
    )(z3, z3, dy3, states, states, pool_w, pool_scale, lbr, lbi, wb, wc, d_skip, glu_w, glu_b)


def _mesh_place():
    x, y, c = lax.axis_index("x"), lax.axis_index("y"), lax.axis_index("c")
    return x, y, c


def _flip(place, k):
    x, y, c = place
    return (1 - x if k & 4 else x, 1 - y if k & 2 else y, 1 - c if k & 1 else c)


def _index(place):
    x, y, c = place
    return 4 * x + 2 * y + c


HBM_SPEC = pl.BlockSpec(memory_space=pltpu.HBM)
SEM_SPEC = pl.BlockSpec(memory_space=pltpu.SEMAPHORE)
_EFFECT = pltpu.SideEffectType.DATAFLOW_SIDE_EFFECTING
N_PEERS = N_DEV - 1


def _exchange_copies(src_refs, land_refs, send_sems, recv_sems):
    me = _mesh_place()
    mine = _index(me)
    out = []
    for a, land_ref in enumerate(land_refs):
        for k in range(1, N_DEV):
            peer = _flip(me, k)
            theirs = _index(peer)
            n = a * N_PEERS + k - 1
            src = src_refs[a].at[theirs] if src_refs else land_ref.at[mine]
            send = pltpu.make_async_remote_copy(
                src_ref=src, dst_ref=land_ref.at[mine], send_sem=send_sems.at[n], recv_sem=recv_sems.at[n],
                device_id=peer, device_id_type=MESH)
            recv = pltpu.make_async_remote_copy(
                src_ref=src, dst_ref=land_ref.at[theirs], send_sem=send_sems.at[n], recv_sem=recv_sems.at[n],
                device_id=peer, device_id_type=MESH)
            out.append((send, recv))
    return out


def _exchange_start(srcs, lands, after, name):
    fresh = lands is None
    ins = tuple(srcs) if fresh else tuple(srcs) + tuple(lands)
    n_src, n_in = len(srcs), len(ins)
    n_lands = n_src if fresh else len(lands)
    n_copies = n_lands * N_PEERS

    def body(*refs):
        send_sems, recv_sems = refs[n_in + 1], refs[n_in + 2]
        first_out = n_in + 3
        land_refs = refs[first_out + n_in:first_out + n_in + n_lands] if fresh else refs[n_src:n_in]
        token = refs[-1]
        for send, _ in _exchange_copies(refs[:n_src], land_refs, send_sems, recv_sems):
            send.start()
        token[...] = jnp.zeros_like(token)

    made = [pltpu.HBM(a.shape, a.dtype) for a in (srcs if fresh else ())]
    res = pl.pallas_call(
        body, name=name,
        in_specs=[HBM_SPEC] * n_in + [ANY_SPEC],
        out_specs=[SEM_SPEC, SEM_SPEC] + [HBM_SPEC] * (n_in + len(made)) + [VMEM_SPEC],
        out_shape=[pltpu.SemaphoreType.DMA((n_copies,)), pltpu.SemaphoreType.DMA((n_copies,))]
                  + [pltpu.HBM(a.shape, a.dtype) for a in ins] + made + [jax.ShapeDtypeStruct((SUBLANES, LANES), F32)],
        input_output_aliases={i: 2 + i for i in range(n_in)},
        compiler_params=pltpu.CompilerParams(has_side_effects=_EFFECT),
    )(*[pltpu.with_memory_space_constraint(a, pltpu.HBM) for a in ins], after)
    return tuple(res[:-1]), res[-1]


def _exchange_wait(handle, n_lands, after, name):
    send_sems, recv_sems = handle[0], handle[1]
    arrays = handle[2:]
    n_all = len(arrays)
    n_src = n_all - n_lands

    def body(*refs):
        for send, recv in _exchange_copies(refs[:n_src], refs[n_src:n_all], refs[n_all], refs[n_all + 1]):
            send.wait_send()
            recv.wait_recv()

    res = pl.pallas_call(
        body, name=name,
        in_specs=[HBM_SPEC] * n_all + [SEM_SPEC, SEM_SPEC, ANY_SPEC],
        out_specs=[HBM_SPEC] * n_all,
        out_shape=[pltpu.HBM(a.shape, a.dtype) for a in arrays],
        input_output_aliases={i: i for i in range(n_all)},
        compiler_params=pltpu.CompilerParams(has_side_effects=_EFFECT),
    )(*arrays, send_sems, recv_sems, after)
    return tuple(res[:n_src]), tuple(res[n_src:])


def _weight_zones(w_in, glu_w, w_out, my_idx):
    shards = (w_in, glu_w, w_out)
    depth = w_in.shape[0]

    def body(idx_ref, *refs):
        ins, zones = refs[:len(shards)], refs[len(shards):]
        for l in range(depth):
            for a, src in enumerate(ins):
                zones[l * len(shards) + a][0] = _mx(src[l])

    whole = lambda s: pl.BlockSpec(s.shape, lambda i, idx: (0,) * s.ndim)
    return pl.pallas_call(
        body, name="weight_zones",
        grid_spec=pltpu.PrefetchScalarGridSpec(
            num_scalar_prefetch=1, grid=(1,),
            in_specs=[whole(s) for s in shards],
            out_specs=[pl.BlockSpec((1,) + s.shape[1:], lambda i, idx: (idx[0], 0, 0))
                       for _ in range(depth) for s in shards]),
        out_shape=[jax.ShapeDtypeStruct((N_DEV,) + s.shape[1:], MXU_DTYPE) for _ in range(depth) for s in shards],
        compiler_params=_params(dimension_semantics=("arbitrary",)),
    )(my_idx.reshape(1).astype(jnp.int32), *shards)


def _allreduce_packed(p):
    rows = p.shape[0]
    half = rows // 2
    quarter = half // 4

    def body(p_ref, o_ref, part_ref, sib_ref, got_ref, send_sems, recv_sems):
        x, y, c = _mesh_place()
        sibling = (x, y, 1 - c)
        chip = 2 * x + y
        chips = [(k, (1 - x if k & 2 else x, 1 - y if k & 1 else y, c), chip ^ k) for k in (1, 2, 3)]
        my_half = pl.multiple_of(c * half, SUBLANES)
        other_half = pl.multiple_of((1 - c) * half, SUBLANES)

        def copy(n, src, dst, to):
            return pltpu.make_async_remote_copy(src_ref=src, dst_ref=dst, send_sem=send_sems.at[n],
                                                recv_sem=recv_sems.at[n], device_id=to, device_id_type=MESH)

        def quarter_of(ref, base, q):
            return ref.at[pl.ds(pl.multiple_of(base + q * quarter, SUBLANES), quarter)]

        swap = copy(0, p_ref.at[pl.ds(other_half, half)], sib_ref, sibling)
        swap.start()
        swap.wait()
        part_ref[...] = p_ref[pl.ds(my_half, half), :] + sib_ref[...]

        scatter = [copy(k, quarter_of(part_ref, 0, q), got_ref.at[k - 1], to) for k, to, q in chips]
        for cp in scatter:
            cp.start()
        total = part_ref[pl.ds(pl.multiple_of(chip * quarter, SUBLANES), quarter), :]
        for cp, (k, _, _) in zip(scatter, chips):
            cp.wait()
            total = total + got_ref[k - 1]
        mine = pl.multiple_of(my_half + chip * quarter, SUBLANES)
        o_ref[pl.ds(mine, quarter), :] = total

        gather = [copy(3 + k, o_ref.at[pl.ds(mine, quarter)], o_ref.at[pl.ds(mine, quarter)], to) for k, to, _ in chips]
        for cp in gather:
            cp.start()
        for k, to, q in chips:
            theirs = quarter_of(o_ref, my_half, q)
            copy(3 + k, theirs, theirs, to).wait_recv()
        for cp in gather:
            cp.wait_send()

        back = copy(7, o_ref.at[pl.ds(my_half, half)], o_ref.at[pl.ds(my_half, half)], sibling)
        back.start()
        copy(7, o_ref.at[pl.ds(other_half, half)], o_ref.at[pl.ds(other_half, half)], sibling).wait_recv()
        back.wait_send()

    return pl.pallas_call(
        body, name="comm_allreduce_packed",
        in_specs=[VMEM_SPEC],
        out_specs=VMEM_SPEC,
        out_shape=jax.ShapeDtypeStruct(p.shape, F32),
        scratch_shapes=[pltpu.VMEM((half, LANES), F32),
                        pltpu.VMEM((half, LANES), F32),
                        pltpu.VMEM((3, quarter, LANES), F32),
                        pltpu.SemaphoreType.DMA((8,)),
                        pltpu.SemaphoreType.DMA((8,))],
        compiler_params=_params(),
    )(p)


def _adamw_math(w, g, m, v):
    m = ADAM_B1 * m + (1.0 - ADAM_B1) * g
    v = ADAM_B2 * v + (1.0 - ADAM_B2) * (g * g)
    m_hat = m / (1.0 - ADAM_B1 ** ADAM_STEP)
    v_hat = v / (1.0 - ADAM_B2 ** ADAM_STEP)
    delta = -ADAM_LR * (m_hat / (jnp.sqrt(v_hat) + ADAM_EPS) + ADAM_WD * w)
    return delta, m, v


def _adamw_summed(received, own, my_idx, w, m, v, name):
    depth, r, c = w.shape
    tr = min(r, 128)

    def body(idx_ref, *refs):
        r_refs, o_refs = refs[:depth], refs[depth:2 * depth]
        w_ref, m_ref, v_ref, g_ref, d_ref, nm_ref, nv_ref = refs[2 * depth:]
        me = idx_ref[0]
        for l in range(depth):
            g = jnp.zeros((tr, c), F32)
            for q in range(N_DEV):
                g = g + jnp.where(q == me, o_refs[l][0], r_refs[l][q]).astype(F32)
            g_ref[l] = g
            d_ref[l], nm_ref[l], nv_ref[l] = _adamw_math(w_ref[l], g, m_ref[l], v_ref[l])

    blk = pl.BlockSpec((depth, tr, c), lambda i, idx: (0, i, 0))
    return pl.pallas_call(
        body, name=name,
        grid_spec=pltpu.PrefetchScalarGridSpec(
            num_scalar_prefetch=1, grid=(r // tr,),
            in_specs=[pl.BlockSpec((N_DEV, tr, c), lambda i, idx: (0, i, 0))] * depth
                     + [pl.BlockSpec((1, tr, c), lambda i, idx: (idx[0], i, 0))] * depth
                     + [blk, blk, blk],
            out_specs=[blk] * 4),
        out_shape=[jax.ShapeDtypeStruct((depth, r, c), F32)] * 4,
        compiler_params=_params(dimension_semantics=("arbitrary",)),
    )(my_idx.reshape(1).astype(jnp.int32), *received, *own, w, m, v)


def _adamw_small(ws, gs, ms, vs):
    n = len(ws)
    depth = ws[0].shape[0]
    quarters = 4

    def spec(a):
        per_layer = a.shape[0] == depth
        split = a.ndim >= 3 and a.shape[1] % quarters == 0 and a.shape[1] >= quarters
        block = (1, a.shape[1] // quarters if split else a.shape[1]) + a.shape[2:]
        rest = (0,) * (a.ndim - 2)
        return pl.BlockSpec(block, lambda l, s: ((l if per_layer else 0), (s if split else 0)) + rest)

    def body(*refs):
        w_refs, g_refs, m_refs, v_refs = (refs[k * n:(k + 1) * n] for k in range(4))
        d_refs, nm_refs, nv_refs = (refs[(4 + k) * n:(5 + k) * n] for k in range(3))
        for k in range(n):
            d_refs[k][...], nm_refs[k][...], nv_refs[k][...] = _adamw_math(
                w_refs[k][...], g_refs[k][...], m_refs[k][...], v_refs[k][...])

    specs = [spec(a) for a in ws]
    shapes = [jax.ShapeDtypeStruct(a.shape, F32) for a in ws]
    res = pl.pallas_call(
        body, name="adamw_small",
        grid=(depth, quarters),
        in_specs=specs * 4,
        out_specs=specs * 3,
        out_shape=shapes * 3,
        compiler_params=_params(dimension_semantics=("arbitrary", "arbitrary")),
    )(*ws, *gs, *ms, *vs)
    return res[:n], res[n:2 * n], res[2 * n:]


_PACK_ROWS = SUBLANES * N_DEV


def _pack(arrays):
    flat = jnp.concatenate([a.reshape(-1) for a in arrays])
    per = _PACK_ROWS * LANES
    total = -(-flat.shape[0] // per) * per
    flat = jnp.pad(flat, (0, total - flat.shape[0]))
    return flat.reshape(total // LANES, LANES)


def _unpack(packed, like):
    flat = packed.reshape(-1)
    out = []
    off = 0
    for a in like:
        out.append(flat[off:off + a.size].reshape(a.shape))
        off += a.size
    return out


def kernel(x, norm_g, w_in, pool_w, pool_scale, a_re, a_im, log_dt, b_re, b_im, c_re, c_im, d_skip, glu_w, glu_b, w_out, final_g, loss_target, m_norm_g, m_w_in, m_pool_w, m_pool_scale, m_a_re, m_a_im, m_log_dt, m_b_re, m_b_im, m_c_re, m_c_im, m_d_skip, m_glu_w, m_glu_b, m_w_out, m_final_g, v_norm_g, v_w_in, v_pool_w, v_pool_scale, v_a_re, v_a_im, v_log_dt, v_b_re, v_b_im, v_c_re, v_c_im, v_d_skip, v_glu_w, v_glu_b, v_w_out, v_final_g):
    nb, seq, _ = x.shape
    n_tok = nb * seq
    depth = norm_g.shape[0]

    my_idx = _index(_mesh_place())

    zones = _weight_zones(w_in, glu_w, w_out, my_idx)

    def gather_start(l, after):
        return _exchange_start((), zones[3 * l:3 * l + 3], after, f"comm_gather_start_{l}")

    def gather_wait(handle, after, l):
        _, (win, glu, wout) = _exchange_wait(handle, 3, after, f"comm_gather_wait_{l}")
        return win, glu.reshape(SSM_W, SSM_W), wout.reshape(MIX, D_MODEL)

    xs = [x.reshape(n_tok, D_MODEL)]
    first_w_in, dep = _exchange_start((), zones[0:1], xs[0], "comm_gather_start_0_w_in")

    (lbr, lbi, rb, rc), dense_vjp = jax.vjp(jax.vmap(_ssm_dense), a_re, a_im, log_dt + dep[0, 0], b_re, b_im, c_re, c_im)
    chunk_all = jax.vmap(_ssm_chunked)
    (wb, wct), chunk_vjp = jax.vjp(lambda p, q: (chunk_all(p), chunk_all(q)), rb, rc)
    wb_m, wct_m = _mx(wb), _mx(wct)
    pool_w_m = _mx(pool_w)

    def layer_params(l):
        return (pool_w_m[l], pool_scale[l][None], lbr[l], lbi[l], wb_m[l], wct_m[l], d_skip[l][None],
                weights[l][1], glu_b[l][None])

    saved = []
    weights = []
    for l in range(depth):
        if l == 0:
            _, (win,) = _exchange_wait(first_w_in, 1, wct_m, "comm_gather_wait_0_w_in")
            rest, dep = _exchange_start((), zones[1:3], win, "comm_gather_start_0_rest")
            z, h = _inproj_fwd(xs[-1], norm_g[l][None], win, dep)
            _, (glu, wout) = _exchange_wait(rest, 2, z, "comm_gather_wait_0_rest")
            weights.append((win, glu.reshape(SSM_W, SSM_W), wout.reshape(MIX, D_MODEL)))
            handle, dep = gather_start(1, weights[0][2])
            z3 = z.reshape(nb, seq, 2 * MIX)
            yg, states, x_next = _layer_fwd(xs[-1].reshape(nb, seq, D_MODEL), z3, None, None, *layer_params(l),
                                            weights[l][2], dep)
        else:
            weights.append(gather_wait(handle, xs[-1], l))
            if l + 1 < depth:
                handle, dep = gather_start(l + 1, weights[l][0])
            z3, h3, yg, states, x_next = _layer_fwd(xs[-1].reshape(nb, seq, D_MODEL), None, norm_g[l][None],
                                                    weights[l][0], *layer_params(l), weights[l][2], dep)
            h = h3.reshape(n_tok, D_MODEL)
        xs.append(x_next.reshape(n_tok, D_MODEL))
        saved.append((z3, h, yg.reshape(n_tok, MIX), states))

    dx, loss_part, d_final_g = _loss_head(xs[-1], loss_target.reshape(n_tok, D_MODEL), final_g[None])

    small = {k: [None] * depth for k in
             ("norm_g", "pool_w", "pool_scale", "lbr", "lbi", "wb", "wct", "d_skip", "glu_b")}
    received = [None] * depth
    sent = [None] * depth
    pending = None
    early = None
    for l in reversed(range(depth)):
        z3, h, yg2, states = saved[l]
        dy, d_wout = _outproj_bwd(dx, yg2, weights[l][2], dep)
        (dz, d_pw, d_ps, d_lbr, d_lbi, d_wb, d_wct, d_dsk, d_gw, d_gb) = _mixer_bwd(
            z3, dy.reshape(nb, seq, MIX), states, *layer_params(l))
        rest = (d_gw.reshape(N_DEV, SSM_W // N_DEV, SSM_W), d_wout.reshape(N_DEV, MIX // N_DEV, D_MODEL))
        if l == 0:
            early, dep = _exchange_start(rest, None, dz, "comm_grads_start_0_rest")
        dx, d_win, d_ng = _inproj_bwd(dz.reshape(n_tok, 2 * MIX), h, xs[l], dx, norm_g[l][None], weights[l][0], dep)
        for k, val in (("norm_g", d_ng[0]), ("pool_w", d_pw), ("pool_scale", d_ps[0]), ("lbr", d_lbr),
                       ("lbi", d_lbi), ("wb", d_wb), ("wct", d_wct), ("d_skip", d_dsk[0]), ("glu_b", d_gb[0])):
            small[k][l] = val
        if pending is not None:
            sent[l + 1], received[l + 1] = _exchange_wait(pending, 3, dx, f"comm_grads_wait_{l + 1}")
        srcs = (d_win,) if l == 0 else (d_win,) + rest
        pending, dep = _exchange_start(srcs, None, dx, f"comm_grads_start_{l}")
    stack = lambda k: jnp.stack(small[k])
    d_rb, d_rc = chunk_vjp((stack("wb"), stack("wct")))
    local = [stack("norm_g"), stack("pool_w"), stack("pool_scale"), stack("lbr"), stack("lbi"), d_rb, d_rc,
             stack("d_skip"), stack("glu_b"), d_final_g[0] + dep[0, 0], loss_part[0]]
    (g_norm_g, g_pool_w, g_pool_scale, g_lbr, g_lbi, g_rb, g_rc, g_d_skip, g_glu_b, g_final_g, loss) = _unpack(
        _allreduce_packed(_pack(local)), local)
    loss = loss[0]
    g_a_re, g_a_im, g_log_dt, g_b_re, g_b_im, g_c_re, g_c_im = dense_vjp((g_lbr, g_lbi, g_rb, g_rc))

    names = ["norm_g", "pool_w", "pool_scale", "a_re", "a_im", "log_dt", "b_re", "b_im", "c_re", "c_im",
             "d_skip", "glu_b", "final_g"]
    rows = {"norm_g", "pool_scale", "log_dt", "d_skip", "glu_b"}
    small_w = [norm_g, pool_w, pool_scale, a_re, a_im, log_dt, b_re, b_im, c_re, c_im, d_skip, glu_b, final_g]
    small_g = [g_norm_g, g_pool_w, g_pool_scale, g_a_re, g_a_im, g_log_dt, g_b_re, g_b_im, g_c_re, g_c_im,
               g_d_skip, g_glu_b, g_final_g]
    small_m = [m_norm_g, m_pool_w, m_pool_scale, m_a_re, m_a_im, m_log_dt, m_b_re, m_b_im, m_c_re, m_c_im,
               m_d_skip, m_glu_b, m_final_g]
    small_v = [v_norm_g, v_pool_w, v_pool_scale, v_a_re, v_a_im, v_log_dt, v_b_re, v_b_im, v_c_re, v_c_im,
               v_d_skip, v_glu_b, v_final_g]

    wide_last = {"b_re", "b_im"}

    def blocked(arrays):
        return [a.reshape(1, 1, -1) if n == "final_g" else a[:, None, :] if n in rows
                else a.swapaxes(2, 3) if n in wide_last else a for n, a in zip(names, arrays)]

    small_d, small_nm, small_nv = _adamw_small(blocked(small_w), blocked(small_g), blocked(small_m), blocked(small_v))
    res = {}
    for kind, arrays in (("grad", small_g), ("delta", small_d), ("m", small_nm), ("v", small_nv)):
        for n, a, like in zip(names, arrays, small_w):
            if kind != "grad" and n in wide_last:
                a = a.swapaxes(2, 3)
            res[kind, n] = a.reshape(like.shape)

    (s_win,), (r_win,) = _exchange_wait(pending, 1, small_d[0], "comm_grads_wait_0")
    (s_glu, s_wout), (r_glu, r_wout) = _exchange_wait(early, 2, small_d[0], "comm_grads_wait_0_rest")
    sent[0], received[0] = (s_win, s_glu, s_wout), (r_win, r_glu, r_wout)
    shard_res = {}
    for pos, (n, w, m, v) in enumerate((("w_in", w_in, m_w_in, v_w_in), ("glu_w", glu_w, m_glu_w, v_glu_w),
                                        ("w_out", w_out, m_w_out, v_w_out))):
        shard_res[n] = _adamw_summed([received[l][pos] for l in range(depth)], [sent[l][pos] for l in range(depth)],
                                     my_idx, w, m, v, "adamw_" + n)
    for n in ("w_in", "glu_w", "w_out"):
        for pos, kind in enumerate(("grad", "delta", "m", "v")):
            res[kind, n] = shard_res[n][pos]

    order = ["norm_g", "w_in", "pool_w", "pool_scale", "a_re", "a_im", "log_dt", "b_re", "b_im", "c_re", "c_im",
             "d_skip", "glu_w", "glu_b", "w_out", "final_g"]
    outs = [loss, dx.reshape(nb, seq, D_MODEL)]
    for kind in ("grad", "delta", "m", "v"):
        outs += [res[kind, n] for n in order]
    return tuple(outs)
```

```python
import math

import jax
import jax.numpy as jnp
from jax import lax
from jax.experimental import pallas as pl
from jax.experimental.pallas import tpu as pltpu

F32 = jnp.float32
MXU_DTYPE = jnp.bfloat16

D_MODEL = 1024
MIX = 1024
POOL_W = 512
SSM_W = 512
N_POOL_G = 4
POOL_GC = 128
SSM_C = 16
SSM_P = 64
NORM_EPS = 1e-5
N_DEV = 8
W_IN_COLS = 2 * MIX // N_DEV

ADAM_LR = 0.001
ADAM_B1 = 0.9
ADAM_B2 = 0.999
ADAM_EPS = 1e-08
ADAM_WD = 0.01
ADAM_STEP = 10

SUBLANES = 8
LANES = 128
HALO = 16
STATE_ROWS = 8
STATE_COLS = 256
CHUNK_GROUPS = STATE_COLS // SSM_P
CHUNK_CH = CHUNK_GROUPS * SSM_C
T_BLK = 256
TM_FWD = 512
TM_BWD = 512
VMEM_LIMIT = 56 * 1024 * 1024

MESH = pl.DeviceIdType.MESH
VMEM_SPEC = pl.BlockSpec(memory_space=pltpu.VMEM)
ANY_SPEC = pl.BlockSpec(memory_space=pl.ANY)


def _mm(a, b):
    return jnp.dot(a, b, preferred_element_type=F32)


def _mm_tn(a, b):
    return lax.dot_general(a, b, (((0,), (0,)), ((), ())), preferred_element_type=F32)


def _mm_nt(a, b):
    return lax.dot_general(a, b, (((1,), (1,)), ((), ())), preferred_element_type=F32)


def _mx(a):
    return a.astype(MXU_DTYPE)


def _sigmoid(v):
    return 1.0 / (1.0 + jnp.exp(-v))


_GELU_C = math.sqrt(2.0 / math.pi)
_GELU_A = 0.044715


def _gelu_and_grad(y):
    th = jnp.tanh(_GELU_C * (y + _GELU_A * y * y * y))
    val = 0.5 * y * (1.0 + th)
    grad = 0.5 * (1.0 + th) + 0.5 * y * (1.0 - th * th) * (_GELU_C * (1.0 + 3.0 * _GELU_A * y * y))
    return val, grad


def _params(**kw):
    return pltpu.CompilerParams(vmem_limit_bytes=VMEM_LIMIT, **kw)


def _ssm_dense(a_re, a_im, log_dt, b_re, b_im, c_re, c_im):
    dt = jnp.exp(log_dt)[:, None]
    mag = jnp.exp(a_re * dt)
    ang = a_im * dt
    lb_re = mag * jnp.cos(ang)
    lb_im = mag * jnp.sin(ang)
    den = a_re * a_re + a_im * a_im
    n_re = lb_re - 1.0
    n_im = lb_im
    f_re = (n_re * a_re + n_im * a_im) / den
    f_im = (n_im * a_re - n_re * a_im) / den
    bb_re = f_re[..., None] * b_re - f_im[..., None] * b_im
    bb_im = f_re[..., None] * b_im + f_im[..., None] * b_re

    bb = jnp.stack([bb_re, bb_im], axis=0).reshape(2, STATE_ROWS, CHUNK_GROUPS, SSM_P, SSM_C)
    rb = bb.transpose(1, 4, 0, 2, 3).reshape(STATE_ROWS, SSM_C, 2 * STATE_COLS)
    cc = jnp.stack([c_re, -c_im], axis=0).reshape(2, STATE_ROWS, CHUNK_GROUPS, SSM_C, SSM_P)
    rc = cc.transpose(1, 3, 0, 2, 4).reshape(STATE_ROWS, SSM_C, 2 * STATE_COLS)
    return (lb_re.reshape(STATE_ROWS, STATE_COLS), lb_im.reshape(STATE_ROWS, STATE_COLS), rb, rc)


def _ssm_chunked(per_channel):
    row_group = jnp.arange(CHUNK_CH) // SSM_C
    col_group = (jnp.arange(2 * STATE_COLS) // SSM_P) % CHUNK_GROUPS
    own_group = (row_group[:, None] == col_group[None, :]).astype(F32)
    even = (jnp.arange(STATE_ROWS) % 2 == 0).astype(F32)[:, None, None]
    half = jnp.tile(per_channel, (1, CHUNK_GROUPS, 1)) * own_group
    return jnp.concatenate([half * even, half * (1.0 - even)], axis=1)


def _inproj_fwd(x2, g_row, w_all, dep):
    n = x2.shape[0]
    tm = TM_FWD

    def body(x_ref, g_ref, w_ref, dep_ref, z_ref, h_ref):
        x = x_ref[...]
        r = lax.rsqrt(jnp.mean(x * x, axis=-1, keepdims=True) + NORM_EPS)
        h = _mx(x * r * g_ref[...])
        h_ref[...] = h
        for d in range(N_DEV):
            z_ref[:, d * W_IN_COLS:(d + 1) * W_IN_COLS] = _mm(h, w_ref[d])

    return pl.pallas_call(
        body, name="inproj_fwd",
        grid=(n // tm,),
        in_specs=[pl.BlockSpec((tm, D_MODEL), lambda i: (i, 0)),
                  pl.BlockSpec((1, D_MODEL), lambda i: (0, 0)),
                  pl.BlockSpec((N_DEV, D_MODEL, W_IN_COLS), lambda i: (0, 0, 0)),
                  ANY_SPEC],
        out_specs=[pl.BlockSpec((tm, 2 * MIX), lambda i: (i, 0)),
                   pl.BlockSpec((tm, D_MODEL), lambda i: (i, 0))],
        out_shape=[jax.ShapeDtypeStruct((n, 2 * MIX), F32),
                   jax.ShapeDtypeStruct((n, D_MODEL), MXU_DTYPE)],
        compiler_params=_params(dimension_semantics=("arbitrary",)),
    )(x2, g_row, w_all, dep)


def _loss_head(x2, tgt2, g_row):
    n = x2.shape[0]
    tm = TM_FWD

    def body(x_ref, t_ref, g_ref, dx_ref, loss_ref, dg_ref):
        @pl.when(pl.program_id(0) == 0)
        def _():
            loss_ref[...] = jnp.zeros_like(loss_ref)
            dg_ref[...] = jnp.zeros_like(dg_ref)

        x = x_ref[...]
        g = g_ref[...]
        r = lax.rsqrt(jnp.mean(x * x, axis=-1, keepdims=True) + NORM_EPS)
        xh = x * r
        e = xh * g - t_ref[...]
        loss_ref[...] += jnp.sum(jnp.sum(e * e, axis=-1, keepdims=True), axis=0, keepdims=True) * (0.5 / D_MODEL)
        dout = e * (1.0 / D_MODEL)
        dg_ref[...] += jnp.sum(dout * xh, axis=0, keepdims=True)
        gdy = dout * g
        dx_ref[...] = r * (gdy - xh * jnp.mean(xh * gdy, axis=-1, keepdims=True))

    return pl.pallas_call(
        body, name="loss_head",
        grid=(n // tm,),
        in_specs=[pl.BlockSpec((tm, D_MODEL), lambda i: (i, 0)),
                  pl.BlockSpec((tm, D_MODEL), lambda i: (i, 0)),
                  pl.BlockSpec((1, D_MODEL), lambda i: (0, 0))],
        out_specs=[pl.BlockSpec((tm, D_MODEL), lambda i: (i, 0)),
                   pl.BlockSpec((1, 1), lambda i: (0, 0)),
                   pl.BlockSpec((1, D_MODEL), lambda i: (0, 0))],
        out_shape=[jax.ShapeDtypeStruct((n, D_MODEL), F32),
                   jax.ShapeDtypeStruct((1, 1), F32),
                   jax.ShapeDtypeStruct((1, D_MODEL), F32)],
        compiler_params=_params(dimension_semantics=("arbitrary",)),
    )(x2, tgt2, g_row)


def _outproj_bwd(dx2, yg, w_out, dep):
    n = dx2.shape[0]
    tm = TM_BWD
    n_steps = n // tm

    def body(dx_ref, y_ref, w_ref, dep_ref, dy_ref, dw_ref, acc_ref):
        i = pl.program_id(0)

        @pl.when(i == 0)
        def _():
            acc_ref[...] = jnp.zeros_like(acc_ref)

        dxb = _mx(dx_ref[...])
        dy_ref[...] = _mm_nt(dxb, w_ref[...])
        acc_ref[...] += _mm_tn(y_ref[...], dxb)

        @pl.when(i == n_steps - 1)
        def _():
            dw_ref[...] = _mx(acc_ref[...])

    return pl.pallas_call(
        body, name="outproj_bwd",
        grid=(n_steps,),
        in_specs=[pl.BlockSpec((tm, D_MODEL), lambda i: (i, 0)),
                  pl.BlockSpec((tm, MIX), lambda i: (i, 0)),
                  pl.BlockSpec((MIX, D_MODEL), lambda i: (0, 0)),
                  ANY_SPEC],
        out_specs=[pl.BlockSpec((tm, MIX), lambda i: (i, 0)),
                   pl.BlockSpec((MIX, D_MODEL), lambda i: (0, 0))],
        out_shape=[jax.ShapeDtypeStruct((n, MIX), F32),
                   jax.ShapeDtypeStruct((MIX, D_MODEL), MXU_DTYPE)],
        scratch_shapes=[pltpu.VMEM((MIX, D_MODEL), F32)],
        compiler_params=_params(dimension_semantics=("arbitrary",)),
    )(dx2, yg, w_out, dep)


def _inproj_bwd(dz, h, x2, dx_in, g_row, w_all, dep):
    n = x2.shape[0]
    tm = TM_BWD
    n_steps = n // tm

    def body(dz_ref, h_ref, x_ref, dxi_ref, g_ref, w_ref, dep_ref, dxo_ref, dw_ref, dg_ref, acc_ref, wcat_ref):
        i = pl.program_id(0)

        @pl.when(i == 0)
        def _():
            acc_ref[...] = jnp.zeros_like(acc_ref)
            dg_ref[...] = jnp.zeros_like(dg_ref)
            for d in range(N_DEV):
                wcat_ref[:, d * W_IN_COLS:(d + 1) * W_IN_COLS] = w_ref[d]

        hb = h_ref[...]
        for d in range(N_DEV):
            acc_ref[d] += _mm_tn(hb, dz_ref[:, d * W_IN_COLS:(d + 1) * W_IN_COLS])
        dh = _mm_nt(dz_ref[...], wcat_ref[...])
        x = x_ref[...]
        r = lax.rsqrt(jnp.mean(x * x, axis=-1, keepdims=True) + NORM_EPS)
        xh = x * r
        dg_ref[...] += jnp.sum(dh * xh, axis=0, keepdims=True)
        gdy = dh * g_ref[...]
        dxo_ref[...] = dxi_ref[...] + r * (gdy - xh * jnp.mean(xh * gdy, axis=-1, keepdims=True))

        @pl.when(i == n_steps - 1)
        def _():
            dw_ref[...] = _mx(acc_ref[...])

    return pl.pallas_call(
        body, name="inproj_bwd",
        grid=(n_steps,),
        in_specs=[pl.BlockSpec((tm, 2 * MIX), lambda i: (i, 0)),
                  pl.BlockSpec((tm, D_MODEL), lambda i: (i, 0)),
                  pl.BlockSpec((tm, D_MODEL), lambda i: (i, 0)),
                  pl.BlockSpec((tm, D_MODEL), lambda i: (i, 0)),
                  pl.BlockSpec((1, D_MODEL), lambda i: (0, 0)),
                  pl.BlockSpec((N_DEV, D_MODEL, W_IN_COLS), lambda i: (0, 0, 0)),
                  ANY_SPEC],
        out_specs=[pl.BlockSpec((tm, D_MODEL), lambda i: (i, 0)),
                   pl.BlockSpec((N_DEV, D_MODEL, W_IN_COLS), lambda i: (0, 0, 0)),
                   pl.BlockSpec((1, D_MODEL), lambda i: (0, 0))],
        out_shape=[jax.ShapeDtypeStruct((n, D_MODEL), F32),
                   jax.ShapeDtypeStruct((N_DEV, D_MODEL, W_IN_COLS), MXU_DTYPE),
                   jax.ShapeDtypeStruct((1, D_MODEL), F32)],
        scratch_shapes=[pltpu.VMEM((N_DEV, D_MODEL, W_IN_COLS), F32),
                        pltpu.VMEM((D_MODEL, 2 * MIX), MXU_DTYPE)],
        compiler_params=_params(dimension_semantics=("arbitrary",)),
    )(dz, h, x2, dx_in, g_row, w_all, dep)


def _row_pos(t0, rows):
    return t0 + lax.broadcasted_iota(jnp.int32, (rows, LANES), 0)


def _pool_window_mean(upad, g, t0, t_blk):
    k = 2 << g
    w = upad
    sh = 1
    while sh < k:
        w = w + pltpu.roll(w, sh, 0)
        sh *= 2
    count = jnp.minimum(_row_pos(t0, t_blk) + 1, k).astype(F32)
    return w[HALO:] / count - upad[HALO:]


def _pool_window_bwd(qpad, g, t_blk):
    k = 2 << g
    n = t_blk + HALO
    w = qpad
    sh = 1
    while sh < k:
        w = w + pltpu.roll(w, n - sh, 0)
        sh *= 2
    return w[:t_blk]


class _StateBuf:
    def __init__(self, refs, t_blk):
        self.refs = refs
        self.t_blk = t_blk

    def put_chunk(self, b, j, val):
        for c in range(4):
            self.refs[4 * b + c][pl.ds(j, self.t_blk, stride=STATE_ROWS), :] = val[:, c * LANES:(c + 1) * LANES]

    def get_chunk(self, b, j):
        return jnp.concatenate(
            [self.refs[4 * b + c][pl.ds(j, self.t_blk, stride=STATE_ROWS), :] for c in range(4)], axis=-1)

    def load(self, b, r, part):
        return jnp.concatenate(
            [self.refs[4 * b + 2 * part + h][pl.ds(r, STATE_ROWS), :] for h in range(2)], axis=-1)

    def store(self, b, r, part, val):
        for h in range(2):
            self.refs[4 * b + 2 * part + h][pl.ds(r, STATE_ROWS), :] = val[:, h * LANES:(h + 1) * LANES]


def _state_scratch(nb, t_blk):
    return [pltpu.VMEM((t_blk * STATE_ROWS, LANES), F32) for _ in range(4 * nb)]


def _ssm_project_in(u_ssm, wb_ref, buf, nb):
    t_blk = u_ssm.shape[0] // nb
    ub = _mx(u_ssm)
    for j in range(STATE_ROWS):
        m = j // 2
        bu = _mm(ub[:, m * LANES:(m + 1) * LANES], wb_ref[j])
        for b in range(nb):
            buf.put_chunk(b, j, bu[b * t_blk:(b + 1) * t_blk])


def _scan_forward(buf, lbr, lbi, init, nb):
    def body(t, carry):
        r = pl.multiple_of(t * STATE_ROWS, STATE_ROWS)
        out = []
        for b in range(nb):
            sr, si = carry[2 * b], carry[2 * b + 1]
            nr = lbr * sr - lbi * si + buf.load(b, r, 0)
            ni = lbr * si + lbi * sr + buf.load(b, r, 1)
            buf.store(b, r, 0, nr)
            buf.store(b, r, 1, ni)
            out += [nr, ni]
        return tuple(out)

    return lax.fori_loop(0, buf.t_blk, body, init, unroll=4)


def _ssm_project_out(chunk, wc_ref):
    tiles = []
    for m in range(4):
        acc = None
        for j in (2 * m, 2 * m + 1):
            part = _mm_nt(chunk(j), wc_ref[j])
            acc = part if acc is None else acc + part
        tiles.append(acc)
    return jnp.concatenate(tiles, axis=-1)


def _layer_fwd(x3, z3, g_row, w_in, pool_w, pool_scale, lbr, lbi, wb, wc, d_skip, glu_w, glu_b, w_out, dep):
    nb, seq, _ = x3.shape
    t_blk = min(T_BLK, seq)
    n_t = seq // t_blk
    halo_per_blk = t_blk // HALO
    rows = nb * t_blk
    fused = z3 is None

    def body(*refs):
        if fused:
            (x_ref, g_ref, wi_ref, pw_ref, ps_ref, lbr_ref, lbi_ref, wb_ref, wc_ref, dsk_ref, gw_ref, gb_ref, wo_ref,
             dep_ref, z_ref, h_ref, yg_ref, sc_ref, y_ref, xo_ref, carry_ref, halo_ref, *s_refs) = refs
        else:
            (x_ref, z_ref, zh_ref, pw_ref, ps_ref, lbr_ref, lbi_ref, wb_ref, wc_ref, dsk_ref, gw_ref, gb_ref, wo_ref,
             dep_ref, yg_ref, sc_ref, y_ref, xo_ref, carry_ref, *s_refs) = refs
        i = pl.program_id(0)
        t0 = i * t_blk
        buf = _StateBuf(s_refs, t_blk)
        both = lambda lo, hi: z_ref[:, :, lo:hi].reshape(rows, hi - lo)

        @pl.when(i == 0)
        def _():
            carry_ref[...] = jnp.zeros_like(carry_ref)
            if fused:
                halo_ref[...] = jnp.zeros_like(halo_ref)

        x = x_ref[...].reshape(rows, D_MODEL)
        if fused:
            r = lax.rsqrt(jnp.mean(x * x, axis=-1, keepdims=True) + NORM_EPS)
            h = _mx(x * r * g_ref[...])
            h_ref[...] = h.reshape(nb, t_blk, D_MODEL)
            for d in range(N_DEV):
                z_ref[:, :, d * W_IN_COLS:(d + 1) * W_IN_COLS] = _mm(h, wi_ref[d]).reshape(nb, t_blk, W_IN_COLS)

        u_ssm = both(POOL_W, MIX)
        _ssm_project_in(u_ssm, wb_ref, buf, nb)
        init = tuple(carry_ref[b, :, h * STATE_COLS:(h + 1) * STATE_COLS] for b in range(nb) for h in range(2))
        fin = _scan_forward(buf, lbr_ref[...], lbi_ref[...], init, nb)
        for b in range(nb):
            carry_ref[b, :, 0:STATE_COLS] = fin[2 * b]
            carry_ref[b, :, STATE_COLS:2 * STATE_COLS] = fin[2 * b + 1]

        def chunk(j):
            states = _mx(jnp.concatenate([buf.get_chunk(b, j) for b in range(nb)], axis=0))
            sc_ref[:, j] = states.reshape(nb, t_blk, 2 * STATE_COLS)
            return states

        y = _ssm_project_out(chunk, wc_ref) + dsk_ref[...] * u_ssm
        y_ref[...] = y.reshape(nb, t_blk, SSM_W)
        yg, _ = _gelu_and_grad(y)
        o_ssm = yg * _sigmoid(_mm(_mx(yg), gw_ref[...]) + gb_ref[...])
        gp = both(MIX + POOL_W, 2 * MIX)
        parts = []
        first = (i == 0)
        for g in range(N_POOL_G):
            cols = slice(g * POOL_GC, (g + 1) * POOL_GC)
            pooled = []
            for b in range(nb):
                halo = halo_ref[b, :, cols] if fused else jnp.where(first, 0.0, zh_ref[b, :, cols])
                pooled.append(_pool_window_mean(jnp.concatenate([halo, z_ref[b, :, cols]], axis=0), g, t0, t_blk))
            yp = _mm(_mx(jnp.concatenate(pooled, axis=0)), pw_ref[g]) * ps_ref[:, cols]
            gpp = both(MIX + g * POOL_GC, MIX + (g + 1) * POOL_GC)
            parts.append(_mx(yp * (gpp * _sigmoid(gpp))))
        parts.append(_mx(o_ssm * (gp * _sigmoid(gp))))
        gated = jnp.concatenate(parts, axis=-1)
        yg_ref[...] = gated.reshape(nb, t_blk, MIX)
        xo_ref[...] = (x + _mm(gated, wo_ref[...])).reshape(nb, t_blk, D_MODEL)
        if fused:
            halo_ref[...] = z_ref[:, t_blk - HALO:, 0:POOL_W]

    const = lambda *shape: pl.BlockSpec(shape, lambda i: (0,) * len(shape))
    tokens = lambda width: pl.BlockSpec((nb, t_blk, width), lambda i: (0, i, 0))
    mixer_specs = [const(N_POOL_G, POOL_GC, POOL_GC), const(1, POOL_W),
                   const(STATE_ROWS, STATE_COLS), const(STATE_ROWS, STATE_COLS),
                   const(STATE_ROWS, LANES, 2 * STATE_COLS), const(STATE_ROWS, LANES, 2 * STATE_COLS),
                   const(1, SSM_W), const(SSM_W, SSM_W), const(1, SSM_W), const(MIX, D_MODEL), ANY_SPEC]
    mixer_args = (pool_w, pool_scale, lbr, lbi, wb, wc, d_skip, glu_w, glu_b, w_out, dep)
    out_specs = [tokens(MIX), pl.BlockSpec((nb, STATE_ROWS, t_blk, 2 * STATE_COLS), lambda i: (0, 0, i, 0)),
                 tokens(SSM_W), tokens(D_MODEL)]
    out_shape = [jax.ShapeDtypeStruct((nb, seq, MIX), MXU_DTYPE),
                 jax.ShapeDtypeStruct((nb, STATE_ROWS, seq, 2 * STATE_COLS), MXU_DTYPE),
                 jax.ShapeDtypeStruct((nb, seq, SSM_W), F32),
                 jax.ShapeDtypeStruct((nb, seq, D_MODEL), F32)]
    scratch = [pltpu.VMEM((nb, STATE_ROWS, 2 * STATE_COLS), F32)]
    if fused:
        in_specs = [tokens(D_MODEL), const(1, D_MODEL), const(N_DEV, D_MODEL, W_IN_COLS)] + mixer_specs
        args = (x3, g_row, w_in) + mixer_args
        out_specs = [tokens(2 * MIX), tokens(D_MODEL)] + out_specs
        out_shape = [jax.ShapeDtypeStruct((nb, seq, 2 * MIX), F32),
                     jax.ShapeDtypeStruct((nb, seq, D_MODEL), MXU_DTYPE)] + out_shape
        scratch = scratch + [pltpu.VMEM((nb, HALO, POOL_W), F32)]
    else:
        in_specs = [tokens(D_MODEL), tokens(2 * MIX),
                    pl.BlockSpec((nb, HALO, POOL_W), lambda i: (0, jnp.maximum(i * halo_per_blk - 1, 0), 0))] + mixer_specs
        args = (x3, z3, z3) + mixer_args
    return pl.pallas_call(
        body, name="layer_fwd" if fused else "mixer_fwd",
        grid=(n_t,),
        in_specs=in_specs, out_specs=out_specs, out_shape=out_shape,
        scratch_shapes=scratch + _state_scratch(nb, t_blk),
        compiler_params=_params(dimension_semantics=("arbitrary",)),
    )(*args)


def _mixer_bwd(z3, dy3, states, y3, pool_w, pool_scale, lbr, lbi, wb, wc, d_skip, glu_w, glu_b):
    nb, seq, _ = z3.shape
    t_blk = min(T_BLK, seq)
    n_t = seq // t_blk
    halo_per_blk = t_blk // HALO
    rows = nb * t_blk

    def body(z_ref, zh_ref, dy_ref, sc_ref, sch_ref, y_ref, pw_ref, ps_ref, lbr_ref, lbi_ref, wb_ref, wc_ref, dsk_ref,
             gw_ref, gb_ref,
             dz_ref, dpw_ref, dps_ref, dlbr_ref, dlbi_ref, dwb_ref, dwc_ref, ddsk_ref, dgw_ref, dgb_ref,
             gcarry_ref, qcarry_ref, du_ref, dgw_acc, *g_refs):
        i = pl.program_id(0)
        blk = n_t - 1 - i
        t0 = blk * t_blk
        gbuf = _StateBuf(g_refs, t_blk)

        @pl.when(i == 0)
        def _():
            gcarry_ref[...] = jnp.zeros_like(gcarry_ref)
            qcarry_ref[...] = jnp.zeros_like(qcarry_ref)
            for ref in (dpw_ref, dps_ref, dlbr_ref, dlbi_ref, dwb_ref, dwc_ref, ddsk_ref, dgw_acc, dgb_ref):
                ref[...] = jnp.zeros_like(ref)

        lbr_v = lbr_ref[...]
        lbi_v = lbi_ref[...]

        both = lambda ref, lo, hi: ref[:, :, lo:hi].reshape(rows, hi - lo)
        split = lambda val: val.reshape(nb, t_blk, val.shape[-1])
        states = lambda j: sc_ref[:, j].reshape(rows, 2 * STATE_COLS)
        first = (blk == 0)

        u_ssm = both(z_ref, POOL_W, MIX)
        yg, dgelu = _gelu_and_grad(y_ref[...].reshape(rows, SSM_W))
        ygb = _mx(yg)
        sg = _sigmoid(_mm(ygb, gw_ref[...]) + gb_ref[...])
        o_ssm = yg * sg
        gp = both(z_ref, MIX + POOL_W, 2 * MIX)
        sgm = _sigmoid(gp)
        dyv = both(dy_ref, POOL_W, MIX)
        dz_ref[:, :, MIX + POOL_W:2 * MIX] = split(_mx(dyv * o_ssm * (sgm * (1.0 + gp * (1.0 - sgm)))))
        do = dyv * (gp * sgm)
        dv = do * yg * (sg * (1.0 - sg))
        dvb = _mx(dv)
        dgb_ref[...] += jnp.sum(dv, axis=0, keepdims=True)
        dgw_acc[...] += _mm_tn(ygb, dvb)
        dyp = (do * sg + _mm_nt(dvb, gw_ref[...])) * dgelu
        ddsk_ref[...] += jnp.sum(dyp * u_ssm, axis=0, keepdims=True)
        dypb = _mx(dyp)
        for j in range(STATE_ROWS):
            m = j // 2
            dyt = dypb[:, m * LANES:(m + 1) * LANES]
            ds = _mm(dyt, wc_ref[j])
            for b in range(nb):
                gbuf.put_chunk(b, j, ds[b * t_blk:(b + 1) * t_blk])
            dwc_ref[j] += _mm_tn(dyt, states(j))
        du_ref[...] = split(dsk_ref[...] * dyp)

        for g in range(N_POOL_G):
            cols = slice(g * POOL_GC, (g + 1) * POOL_GC)
            pooled = []
            for b in range(nb):
                halo = jnp.where(first, 0.0, zh_ref[b, :, cols])
                pooled.append(_pool_window_mean(jnp.concatenate([halo, z_ref[b, :, cols]], axis=0), g, t0, t_blk))
            pb = _mx(jnp.concatenate(pooled, axis=0))
            ypre = _mm(pb, pw_ref[g])
            gpp = both(z_ref, MIX + g * POOL_GC, MIX + (g + 1) * POOL_GC)
            sgp = _sigmoid(gpp)
            dyg = both(dy_ref, g * POOL_GC, (g + 1) * POOL_GC)
            scale = ps_ref[:, cols]
            dz_ref[:, :, MIX + g * POOL_GC:MIX + (g + 1) * POOL_GC] = split(_mx(
                dyg * (ypre * scale) * (sgp * (1.0 + gpp * (1.0 - sgp)))))
            dyc = dyg * (gpp * sgp)
            dps_ref[:, cols] += jnp.sum(dyc * ypre, axis=0, keepdims=True)
            dypre = _mx(dyc * scale)
            dpw_ref[g] += _mm_tn(pb, dypre)
            dpooled = _mm_nt(dypre, pw_ref[g])
            count = jnp.minimum(_row_pos(t0, t_blk) + 1, 2 << g).astype(F32)
            for b in range(nb):
                dp = dpooled[b * t_blk:(b + 1) * t_blk]
                q = dp / count
                qpad = jnp.concatenate([q, qcarry_ref[b, :, cols]], axis=0)
                qcarry_ref[b, :, cols] = q[:HALO]
                dz_ref[b, :, cols] = _mx(_pool_window_bwd(qpad, g, t_blk) - dp)

        def rev_body(k, carry):
            r = pl.multiple_of((t_blk - 1 - k) * STATE_ROWS, STATE_ROWS)
            out = []
            for b in range(nb):
                gr, gi = carry[2 * b], carry[2 * b + 1]
                ngr = lbr_v * gr + lbi_v * gi + gbuf.load(b, r, 0)
                ngi = lbr_v * gi - lbi_v * gr + gbuf.load(b, r, 1)
                gbuf.store(b, r, 0, ngr)
                gbuf.store(b, r, 1, ngi)
                out += [ngr, ngi]
            return tuple(out)

        init_g = tuple(gcarry_ref[b, :, h * STATE_COLS:(h + 1) * STATE_COLS] for b in range(nb) for h in range(2))
        fin = lax.fori_loop(0, t_blk, rev_body, init_g, unroll=4)
        for b in range(nb):
            gcarry_ref[b, :, 0:STATE_COLS] = fin[2 * b]
            gcarry_ref[b, :, STATE_COLS:2 * STATE_COLS] = fin[2 * b + 1]

        ub = _mx(u_ssm)
        for m in range(4):
            acc = both(du_ref, m * LANES, (m + 1) * LANES)
            for j in (2 * m, 2 * m + 1):
                g = jnp.concatenate([gbuf.get_chunk(b, j) for b in range(nb)], axis=0)
                gj = _mx(g)
                acc = acc + _mm_nt(gj, wb_ref[j])
                dwb_ref[j] += _mm_tn(ub[:, m * LANES:(m + 1) * LANES], gj)
                shifted = []
                for b in range(nb):
                    before = jnp.where(first, 0.0, sch_ref[b, j].astype(F32))
                    spad = jnp.concatenate([before, sc_ref[b, j].astype(F32)], axis=0)
                    shifted.append(pltpu.roll(spad, 1, 0)[HALO:])
                s_prev = jnp.concatenate(shifted, axis=0)
                g_re, g_im = g[:, :STATE_COLS], g[:, STATE_COLS:]
                p_re, p_im = s_prev[:, :STATE_COLS], s_prev[:, STATE_COLS:]
                dlbr_ref[j:j + 1, :] += jnp.sum(g_re * p_re + g_im * p_im, axis=0, keepdims=True)
                dlbi_ref[j:j + 1, :] += jnp.sum(g_im * p_re - g_re * p_im, axis=0, keepdims=True)
            dz_ref[:, :, POOL_W + m * LANES:POOL_W + (m + 1) * LANES] = split(_mx(acc))

        @pl.when(i == n_t - 1)
        def _():
            dgw_ref[...] = _mx(dgw_acc[...])

    const = lambda *shape: pl.BlockSpec(shape, lambda i: (0,) * len(shape))
    rev = lambda i: n_t - 1 - i
    out_shape = [jax.ShapeDtypeStruct((nb, seq, 2 * MIX), MXU_DTYPE),
                 jax.ShapeDtypeStruct((N_POOL_G, POOL_GC, POOL_GC), F32),
                 jax.ShapeDtypeStruct((1, POOL_W), F32),
                 jax.ShapeDtypeStruct((STATE_ROWS, STATE_COLS), F32),
                 jax.ShapeDtypeStruct((STATE_ROWS, STATE_COLS), F32),
                 jax.ShapeDtypeStruct((STATE_ROWS, LANES, 2 * STATE_COLS), F32),
                 jax.ShapeDtypeStruct((STATE_ROWS, LANES, 2 * STATE_COLS), F32),
                 jax.ShapeDtypeStruct((1, SSM_W), F32),
                 jax.ShapeDtypeStruct((SSM_W, SSM_W), MXU_DTYPE),
                 jax.ShapeDtypeStruct((1, SSM_W), F32)]
    return pl.pallas_call(
        body, name="mixer_bwd",
        grid=(n_t,),
        in_specs=[pl.BlockSpec((nb, t_blk, 2 * MIX), lambda i: (0, rev(i), 0)),
                  pl.BlockSpec((nb, HALO, POOL_W), lambda i: (0, jnp.maximum(rev(i) * halo_per_blk - 1, 0), 0)),
                  pl.BlockSpec((nb, t_blk, MIX), lambda i: (0, rev(i), 0)),
                  pl.BlockSpec((nb, STATE_ROWS, t_blk, 2 * STATE_COLS), lambda i: (0, 0, rev(i), 0)),
                  pl.BlockSpec((nb, STATE_ROWS, HALO, 2 * STATE_COLS),
                               lambda i: (0, 0, jnp.maximum(rev(i) * halo_per_blk - 1, 0), 0)),
                  pl.BlockSpec((nb, t_blk, SSM_W), lambda i: (0, rev(i), 0)),
                  const(N_POOL_G, POOL_GC, POOL_GC), const(1, POOL_W),
                  const(STATE_ROWS, STATE_COLS), const(STATE_ROWS, STATE_COLS),
                  const(STATE_ROWS, LANES, 2 * STATE_COLS), const(STATE_ROWS, LANES, 2 * STATE_COLS),
                  const(1, SSM_W), const(SSM_W, SSM_W), const(1, SSM_W)],
        out_specs=[pl.BlockSpec((nb, t_blk, 2 * MIX), lambda i: (0, rev(i), 0))]
                  + [const(*s.shape) for s in out_shape[1:]],
        out_shape=out_shape,
        scratch_shapes=[pltpu.VMEM((nb, STATE_ROWS, 2 * STATE_COLS), F32),
                        pltpu.VMEM((nb, HALO, POOL_W), F32),
                        pltpu.VMEM((nb, t_blk, SSM_W), F32),
                        pltpu.VMEM((SSM_W, SSM_W), F32)]
                       + _state_scratch(nb, t_blk),
        compiler_params=_params(dimension_semantics=("arbitrary",)),
    )(z3, z3, dy3, states, states, y3, pool_w, pool_scale, lbr, lbi, wb, wc, d_skip, glu_w, glu_b)


def _mesh_place():
    x, y, c = lax.axis_index("x"), lax.axis_index("y"), lax.axis_index("c")
    return x, y, c


def _flip(place, k):
    x, y, c = place
    return (1 - x if k & 4 else x, 1 - y if k & 2 else y, 1 - c if k & 1 else c)


def _index(place):
    x, y, c = place
    return 4 * x + 2 * y + c


HBM_SPEC = pl.BlockSpec(memory_space=pltpu.HBM)
SEM_SPEC = pl.BlockSpec(memory_space=pltpu.SEMAPHORE)
_EFFECT = pltpu.SideEffectType.DATAFLOW_SIDE_EFFECTING
N_PEERS = N_DEV - 1


def _exchange_copies(src_refs, land_refs, send_sems, recv_sems):
    me = _mesh_place()
    mine = _index(me)
    out = []
    for a, land_ref in enumerate(land_refs):
        for k in range(1, N_DEV):
            peer = _flip(me, k)
            theirs = _index(peer)
            n = a * N_PEERS + k - 1
            src = src_refs[a].at[theirs] if src_refs else land_ref.at[mine]
            send = pltpu.make_async_remote_copy(
                src_ref=src, dst_ref=land_ref.at[mine], send_sem=send_sems.at[n], recv_sem=recv_sems.at[n],
                device_id=peer, device_id_type=MESH)
            recv = pltpu.make_async_remote_copy(
                src_ref=src, dst_ref=land_ref.at[theirs], send_sem=send_sems.at[n], recv_sem=recv_sems.at[n],
                device_id=peer, device_id_type=MESH)
            out.append((send, recv))
    return out


def _exchange_start(srcs, lands, after, name):
    arrays = tuple(srcs) + tuple(lands)
    n_src, n_all = len(srcs), len(arrays)
    n_copies = len(lands) * N_PEERS

    def body(*refs):
        send_sems, recv_sems = refs[n_all + 1], refs[n_all + 2]
        token = refs[-1]
        for send, _ in _exchange_copies(refs[:n_src], refs[n_src:n_all], send_sems, recv_sems):
            send.start()
        token[...] = jnp.zeros_like(token)

    res = pl.pallas_call(
        body, name=name,
        in_specs=[HBM_SPEC] * n_all + [ANY_SPEC],
        out_specs=[SEM_SPEC, SEM_SPEC] + [HBM_SPEC] * n_all + [VMEM_SPEC],
        out_shape=[pltpu.SemaphoreType.DMA((n_copies,)), pltpu.SemaphoreType.DMA((n_copies,))]
                  + [pltpu.HBM(a.shape, a.dtype) for a in arrays] + [jax.ShapeDtypeStruct((SUBLANES, LANES), F32)],
        input_output_aliases={i: 2 + i for i in range(n_all)},
        compiler_params=pltpu.CompilerParams(has_side_effects=_EFFECT),
    )(*[pltpu.with_memory_space_constraint(a, pltpu.HBM) for a in arrays], after)
    return tuple(res[:-1]), res[-1]


def _exchange_wait(handle, n_lands, after, name):
    send_sems, recv_sems = handle[0], handle[1]
    arrays = handle[2:]
    n_all = len(arrays)
    n_src = n_all - n_lands

    def body(*refs):
        for send, recv in _exchange_copies(refs[:n_src], refs[n_src:n_all], refs[n_all], refs[n_all + 1]):
            send.wait_send()
            recv.wait_recv()

    res = pl.pallas_call(
        body, name=name,
        in_specs=[HBM_SPEC] * n_all + [SEM_SPEC, SEM_SPEC, ANY_SPEC],
        out_specs=[HBM_SPEC] * n_all,
        out_shape=[pltpu.HBM(a.shape, a.dtype) for a in arrays],
        input_output_aliases={i: i for i in range(n_all)},
        compiler_params=pltpu.CompilerParams(has_side_effects=_EFFECT),
    )(*arrays, send_sems, recv_sems, after)
    return tuple(res[:n_src]), tuple(res[n_src:])


def _weight_zones(w_in, glu_w, w_out, my_idx):
    shards = (w_in, glu_w, w_out)
    depth = w_in.shape[0]

    def body(idx_ref, *refs):
        ins, zones = refs[:len(shards)], refs[len(shards):]
        for l in range(depth):
            for a, src in enumerate(ins):
                zones[l * len(shards) + a][0] = _mx(src[l])

    whole = lambda s: pl.BlockSpec(s.shape, lambda i, idx: (0,) * s.ndim)
    return pl.pallas_call(
        body, name="weight_zones",
        grid_spec=pltpu.PrefetchScalarGridSpec(
            num_scalar_prefetch=1, grid=(1,),
            in_specs=[whole(s) for s in shards],
            out_specs=[pl.BlockSpec((1,) + s.shape[1:], lambda i, idx: (idx[0], 0, 0))
                       for _ in range(depth) for s in shards]),
        out_shape=[jax.ShapeDtypeStruct((N_DEV,) + s.shape[1:], MXU_DTYPE) for _ in range(depth) for s in shards],
        compiler_params=_params(dimension_semantics=("arbitrary",)),
    )(my_idx.reshape(1).astype(jnp.int32), *shards)


def _allreduce_packed(p):
    rows = p.shape[0]
    half = rows // 2
    quarter = half // 4

    def body(p_ref, o_ref, part_ref, sib_ref, got_ref, send_sems, recv_sems):
        x, y, c = _mesh_place()
        sibling = (x, y, 1 - c)
        chip = 2 * x + y
        chips = [(k, (1 - x if k & 2 else x, 1 - y if k & 1 else y, c), chip ^ k) for k in (1, 2, 3)]
        my_half = pl.multiple_of(c * half, SUBLANES)
        other_half = pl.multiple_of((1 - c) * half, SUBLANES)

        def copy(n, src, dst, to):
            return pltpu.make_async_remote_copy(src_ref=src, dst_ref=dst, send_sem=send_sems.at[n],
                                                recv_sem=recv_sems.at[n], device_id=to, device_id_type=MESH)

        def quarter_of(ref, base, q):
            return ref.at[pl.ds(pl.multiple_of(base + q * quarter, SUBLANES), quarter)]

        swap = copy(0, p_ref.at[pl.ds(other_half, half)], sib_ref, sibling)
        swap.start()
        swap.wait()
        part_ref[...] = p_ref[pl.ds(my_half, half), :] + sib_ref[...]

        scatter = [copy(k, quarter_of(part_ref, 0, q), got_ref.at[k - 1], to) for k, to, q in chips]
        for cp in scatter:
            cp.start()
        total = part_ref[pl.ds(pl.multiple_of(chip * quarter, SUBLANES), quarter), :]
        for cp, (k, _, _) in zip(scatter, chips):
            cp.wait()
            total = total + got_ref[k - 1]
        mine = pl.multiple_of(my_half + chip * quarter, SUBLANES)
        o_ref[pl.ds(mine, quarter), :] = total

        gather = [copy(3 + k, o_ref.at[pl.ds(mine, quarter)], o_ref.at[pl.ds(mine, quarter)], to) for k, to, _ in chips]
        for cp in gather:
            cp.start()
        for k, to, q in chips:
            theirs = quarter_of(o_ref, my_half, q)
            copy(3 + k, theirs, theirs, to).wait_recv()
        for cp in gather:
            cp.wait_send()

        back = copy(7, o_ref.at[pl.ds(my_half, half)], o_ref.at[pl.ds(my_half, half)], sibling)
        back.start()
        copy(7, o_ref.at[pl.ds(other_half, half)], o_ref.at[pl.ds(other_half, half)], sibling).wait_recv()
        back.wait_send()

    return pl.pallas_call(
        body, name="comm_allreduce_packed",
        in_specs=[VMEM_SPEC],
        out_specs=VMEM_SPEC,
        out_shape=jax.ShapeDtypeStruct(p.shape, F32),
        scratch_shapes=[pltpu.VMEM((half, LANES), F32),
                        pltpu.VMEM((half, LANES), F32),
                        pltpu.VMEM((3, quarter, LANES), F32),
                        pltpu.SemaphoreType.DMA((8,)),
                        pltpu.SemaphoreType.DMA((8,))],
        compiler_params=_params(),
    )(p)


def _adamw_math(w, g, m, v):
    m = ADAM_B1 * m + (1.0 - ADAM_B1) * g
    v = ADAM_B2 * v + (1.0 - ADAM_B2) * (g * g)
    m_hat = m / (1.0 - ADAM_B1 ** ADAM_STEP)
    v_hat = v / (1.0 - ADAM_B2 ** ADAM_STEP)
    delta = -ADAM_LR * (m_hat / (jnp.sqrt(v_hat) + ADAM_EPS) + ADAM_WD * w)
    return delta, m, v


def _adamw_summed(received, own, my_idx, w, m, v, name):
    depth, r, c = w.shape
    tr = min(r, 128)

    def body(idx_ref, *refs):
        r_refs, o_refs = refs[:depth], refs[depth:2 * depth]
        w_ref, m_ref, v_ref, g_ref, d_ref, nm_ref, nv_ref = refs[2 * depth:]
        me = idx_ref[0]
        for l in range(depth):
            g = jnp.zeros((tr, c), F32)
            for q in range(N_DEV):
                g = g + jnp.where(q == me, o_refs[l][0], r_refs[l][q]).astype(F32)
            g_ref[l] = g
            d_ref[l], nm_ref[l], nv_ref[l] = _adamw_math(w_ref[l], g, m_ref[l], v_ref[l])

    blk = pl.BlockSpec((depth, tr, c), lambda i, idx: (0, i, 0))
    return pl.pallas_call(
        body, name=name,
        grid_spec=pltpu.PrefetchScalarGridSpec(
            num_scalar_prefetch=1, grid=(r // tr,),
            in_specs=[pl.BlockSpec((N_DEV, tr, c), lambda i, idx: (0, i, 0))] * depth
                     + [pl.BlockSpec((1, tr, c), lambda i, idx: (idx[0], i, 0))] * depth
                     + [blk, blk, blk],
            out_specs=[blk] * 4),
        out_shape=[jax.ShapeDtypeStruct((depth, r, c), F32)] * 4,
        compiler_params=_params(dimension_semantics=("arbitrary",)),
    )(my_idx.reshape(1).astype(jnp.int32), *received, *own, w, m, v)


def _adamw_small(ws, gs, ms, vs):
    n = len(ws)
    depth = ws[0].shape[0]
    quarters = 4

    def spec(a):
        per_layer = a.shape[0] == depth
        split = a.ndim >= 3 and a.shape[1] % quarters == 0 and a.shape[1] >= quarters
        block = (1, a.shape[1] // quarters if split else a.shape[1]) + a.shape[2:]
        rest = (0,) * (a.ndim - 2)
        return pl.BlockSpec(block, lambda l, s: ((l if per_layer else 0), (s if split else 0)) + rest)

    def body(*refs):
        w_refs, g_refs, m_refs, v_refs = (refs[k * n:(k + 1) * n] for k in range(4))
        d_refs, nm_refs, nv_refs = (refs[(4 + k) * n:(5 + k) * n] for k in range(3))
        for k in range(n):
            d_refs[k][...], nm_refs[k][...], nv_refs[k][...] = _adamw_math(
                w_refs[k][...], g_refs[k][...], m_refs[k][...], v_refs[k][...])

    specs = [spec(a) for a in ws]
    shapes = [jax.ShapeDtypeStruct(a.shape, F32) for a in ws]
    res = pl.pallas_call(
        body, name="adamw_small",
        grid=(depth, quarters),
        in_specs=specs * 4,
        out_specs=specs * 3,
        out_shape=shapes * 3,
        compiler_params=_params(dimension_semantics=("arbitrary", "arbitrary")),
    )(*ws, *gs, *ms, *vs)
    return res[:n], res[n:2 * n], res[2 * n:]


_PACK_ROWS = SUBLANES * N_DEV


def _pack(arrays):
    flat = jnp.concatenate([a.reshape(-1) for a in arrays])
    per = _PACK_ROWS * LANES
    total = -(-flat.shape[0] // per) * per
    flat = jnp.pad(flat, (0, total - flat.shape[0]))
    return flat.reshape(total // LANES, LANES)


def _unpack(packed, like):
    flat = packed.reshape(-1)
    out = []
    off = 0
    for a in like:
        out.append(flat[off:off + a.size].reshape(a.shape))
        off += a.size
    return out


def kernel(x, norm_g, w_in, pool_w, pool_scale, a_re, a_im, log_dt, b_re, b_im, c_re, c_im, d_skip, glu_w, glu_b, w_out, final_g, loss_target, m_norm_g, m_w_in, m_pool_w, m_pool_scale, m_a_re, m_a_im, m_log_dt, m_b_re, m_b_im, m_c_re, m_c_im, m_d_skip, m_glu_w, m_glu_b, m_w_out, m_final_g, v_norm_g, v_w_in, v_pool_w, v_pool_scale, v_a_re, v_a_im, v_log_dt, v_b_re, v_b_im, v_c_re, v_c_im, v_d_skip, v_glu_w, v_glu_b, v_w_out, v_final_g):
    nb, seq, _ = x.shape
    n_tok = nb * seq
    depth = norm_g.shape[0]

    my_idx = _index(_mesh_place())

    zones = _weight_zones(w_in, glu_w, w_out, my_idx)

    def gather_start(l, after):
        return _exchange_start((), zones[3 * l:3 * l + 3], after, f"comm_gather_start_{l}")

    def gather_wait(handle, after, l):
        _, (win, glu, wout) = _exchange_wait(handle, 3, after, f"comm_gather_wait_{l}")
        return win, glu.reshape(SSM_W, SSM_W), wout.reshape(MIX, D_MODEL)

    xs = [x.reshape(n_tok, D_MODEL)]
    first_w_in, dep = _exchange_start((), zones[0:1], xs[0], "comm_gather_start_0_w_in")

    (lbr, lbi, rb, rc), dense_vjp = jax.vjp(jax.vmap(_ssm_dense), a_re, a_im, log_dt + dep[0, 0], b_re, b_im, c_re, c_im)
    chunk_all = jax.vmap(_ssm_chunked)
    (wb, wct), chunk_vjp = jax.vjp(lambda p, q: (chunk_all(p), chunk_all(q)), rb, rc)
    wb_m, wct_m = _mx(wb), _mx(wct)
    pool_w_m = _mx(pool_w)

    def layer_params(l):
        return (pool_w_m[l], pool_scale[l][None], lbr[l], lbi[l], wb_m[l], wct_m[l], d_skip[l][None],
                weights[l][1], glu_b[l][None])

    saved = []
    weights = []
    for l in range(depth):
        if l == 0:
            _, (win,) = _exchange_wait(first_w_in, 1, wct_m, "comm_gather_wait_0_w_in")
            rest, dep = _exchange_start((), zones[1:3], win, "comm_gather_start_0_rest")
            z, h = _inproj_fwd(xs[-1], norm_g[l][None], win, dep)
            _, (glu, wout) = _exchange_wait(rest, 2, z, "comm_gather_wait_0_rest")
            weights.append((win, glu.reshape(SSM_W, SSM_W), wout.reshape(MIX, D_MODEL)))
            handle, dep = gather_start(1, weights[0][2])
            z3 = z.reshape(nb, seq, 2 * MIX)
            yg, states, y3, x_next = _layer_fwd(xs[-1].reshape(nb, seq, D_MODEL), z3, None, None, *layer_params(l),
                                                weights[l][2], dep)
        else:
            weights.append(gather_wait(handle, xs[-1], l))
            if l + 1 < depth:
                handle, dep = gather_start(l + 1, weights[l][0])
            z3, h3, yg, states, y3, x_next = _layer_fwd(xs[-1].reshape(nb, seq, D_MODEL), None, norm_g[l][None],
                                                        weights[l][0], *layer_params(l), weights[l][2], dep)
            h = h3.reshape(n_tok, D_MODEL)
        xs.append(x_next.reshape(n_tok, D_MODEL))
        saved.append((z3, h, yg.reshape(n_tok, MIX), states, y3))

    dx, loss_part, d_final_g = _loss_head(xs[-1], loss_target.reshape(n_tok, D_MODEL), final_g[None])

    small = {k: [None] * depth for k in
             ("norm_g", "pool_w", "pool_scale", "lbr", "lbi", "wb", "wct", "d_skip", "glu_b")}
    received = [None] * depth
    sent = [None] * depth
    pending = None
    early = None
    for l in reversed(range(depth)):
        z3, h, yg2, states, y3 = saved[l]
        dy, d_wout = _outproj_bwd(dx, yg2, weights[l][2], dep)
        (dz, d_pw, d_ps, d_lbr, d_lbi, d_wb, d_wct, d_dsk, d_gw, d_gb) = _mixer_bwd(
            z3, dy.reshape(nb, seq, MIX), states, y3, *layer_params(l))
        rest = (d_gw.reshape(N_DEV, SSM_W // N_DEV, SSM_W), d_wout.reshape(N_DEV, MIX // N_DEV, D_MODEL))
        if l == 0:
            early, dep = _exchange_start(rest, tuple(lax.empty(s.shape, s.dtype) for s in rest), dz,
                                         "comm_grads_start_0_rest")
        dx, d_win, d_ng = _inproj_bwd(dz.reshape(n_tok, 2 * MIX), h, xs[l], dx, norm_g[l][None], weights[l][0], dep)
        for k, val in (("norm_g", d_ng[0]), ("pool_w", d_pw), ("pool_scale", d_ps[0]), ("lbr", d_lbr),
                       ("lbi", d_lbi), ("wb", d_wb), ("wct", d_wct), ("d_skip", d_dsk[0]), ("glu_b", d_gb[0])):
            small[k][l] = val
        if pending is not None:
            sent[l + 1], received[l + 1] = _exchange_wait(pending, 3, dx, f"comm_grads_wait_{l + 1}")
        srcs = (d_win,) if l == 0 else (d_win,) + rest
        lands = tuple(lax.empty(s.shape, s.dtype) for s in srcs)
        pending, dep = _exchange_start(srcs, lands, dx, f"comm_grads_start_{l}")
    stack = lambda k: jnp.stack(small[k])
    d_rb, d_rc = chunk_vjp((stack("wb"), stack("wct")))
    local = [stack("norm_g"), stack("pool_w"), stack("pool_scale"), stack("lbr"), stack("lbi"), d_rb, d_rc,
             stack("d_skip"), stack("glu_b"), d_final_g[0] + dep[0, 0], loss_part[0]]
    (g_norm_g, g_pool_w, g_pool_scale, g_lbr, g_lbi, g_rb, g_rc, g_d_skip, g_glu_b, g_final_g, loss) = _unpack(
        _allreduce_packed(_pack(local)), local)
    loss = loss[0]
    g_a_re, g_a_im, g_log_dt, g_b_re, g_b_im, g_c_re, g_c_im = dense_vjp((g_lbr, g_lbi, g_rb, g_rc))

    names = ["norm_g", "pool_w", "pool_scale", "a_re", "a_im", "log_dt", "b_re", "b_im", "c_re", "c_im",
             "d_skip", "glu_b", "final_g"]
    rows = {"norm_g", "pool_scale", "log_dt", "d_skip", "glu_b"}
    small_w = [norm_g, pool_w, pool_scale, a_re, a_im, log_dt, b_re, b_im, c_re, c_im, d_skip, glu_b, final_g]
    small_g = [g_norm_g, g_pool_w, g_pool_scale, g_a_re, g_a_im, g_log_dt, g_b_re, g_b_im, g_c_re, g_c_im,
               g_d_skip, g_glu_b, g_final_g]
    small_m = [m_norm_g, m_pool_w, m_pool_scale, m_a_re, m_a_im, m_log_dt, m_b_re, m_b_im, m_c_re, m_c_im,
               m_d_skip, m_glu_b, m_final_g]
    small_v = [v_norm_g, v_pool_w, v_pool_scale, v_a_re, v_a_im, v_log_dt, v_b_re, v_b_im, v_c_re, v_c_im,
               v_d_skip, v_glu_b, v_final_g]

    wide_last = {"b_re", "b_im"}

    def blocked(arrays):
        return [a.reshape(1, 1, -1) if n == "final_g" else a[:, None, :] if n in rows
                else a.swapaxes(2, 3) if n in wide_last else a for n, a in zip(names, arrays)]

    small_d, small_nm, small_nv = _adamw_small(blocked(small_w), blocked(small_g), blocked(small_m), blocked(small_v))
    res = {}
    for kind, arrays in (("grad", small_g), ("delta", small_d), ("m", small_nm), ("v", small_nv)):
        for n, a, like in zip(names, arrays, small_w):
            if kind != "grad" and n in wide_last:
                a = a.swapaxes(2, 3)
            res[kind, n] = a.reshape(like.shape)

    (s_win,), (r_win,) = _exchange_wait(pending, 1, small_d[0], "comm_grads_wait_0")
    (s_glu, s_wout), (r_glu, r_wout) = _exchange_wait(early, 2, small_d[0], "comm_grads_wait_0_rest")
    sent[0], received[0] = (s_win, s_glu, s_wout), (r_win, r_glu, r_wout)
    shard_res = {}
    for pos, (n, w, m, v) in enumerate((("w_in", w_in, m_w_in, v_w_in), ("glu_w", glu_w, m_glu_w, v_glu_w),
                                        ("w_out", w_out, m_w_out, v_w_out))):
        shard_res[n] = _adamw_summed([received[l][pos] for l in range(depth)], [sent[l][pos] for l in range(depth)],
                                     my_idx, w, m, v, "adamw_" + n)
    for n in ("w_in", "glu_w", "w_out"):
        for pos, kind in enumerate(("grad", "delta", "m", "v")):
            res[kind, n] = shard_res[n][pos]

    order = ["norm_g", "w_in", "pool_w", "pool_scale", "a_re", "a_im", "log_dt", "b_re", "b_im", "c_re", "c_im",
             "d_skip", "glu_w", "glu_b", "w_out", "final_g"]
    outs = [loss, dx.reshape(nb, seq, D_MODEL)]
    for kind in ("grad", "delta", "m", "v"):
        outs += [res[kind, n] for n in order]
    return tuple(outs)
```

```python
import math

import jax
import jax.numpy as jnp
from jax import lax
from jax.experimental import pallas as pl
from jax.experimental.pallas import tpu as pltpu

F32 = jnp.float32
MXU_DTYPE = jnp.bfloat16

D_MODEL = 1024
MIX = 1024
POOL_W = 512
SSM_W = 512
N_POOL_G = 4
POOL_GC = 128
SSM_C = 16
SSM_P = 64
NORM_EPS = 1e-5
N_DEV = 8
W_IN_COLS = 2 * MIX // N_DEV

ADAM_LR = 0.001
ADAM_B1 = 0.9
ADAM_B2 = 0.999
ADAM_EPS = 1e-08
ADAM_WD = 0.01
ADAM_STEP = 10

SUBLANES = 8
LANES = 128
HALO = 16
STATE_ROWS = 8
STATE_COLS = 256
CHUNK_GROUPS = STATE_COLS // SSM_P
CHUNK_CH = CHUNK_GROUPS * SSM_C
T_BLK = 256
TM_FWD = 512
TM_BWD = 512
VMEM_LIMIT = 56 * 1024 * 1024

MESH = pl.DeviceIdType.MESH
VMEM_SPEC = pl.BlockSpec(memory_space=pltpu.VMEM)
ANY_SPEC = pl.BlockSpec(memory_space=pl.ANY)


def _mm(a, b):
    return jnp.dot(a, b, preferred_element_type=F32)


def _mm_tn(a, b):
    return lax.dot_general(a, b, (((0,), (0,)), ((), ())), preferred_element_type=F32)


def _mm_nt(a, b):
    return lax.dot_general(a, b, (((1,), (1,)), ((), ())), preferred_element_type=F32)


def _mx(a):
    return a.astype(MXU_DTYPE)


def _sigmoid(v):
    return 1.0 / (1.0 + jnp.exp(-v))


_GELU_C = math.sqrt(2.0 / math.pi)
_GELU_A = 0.044715


def _gelu_and_grad(y):
    th = jnp.tanh(_GELU_C * (y + _GELU_A * y * y * y))
    val = 0.5 * y * (1.0 + th)
    grad = 0.5 * (1.0 + th) + 0.5 * y * (1.0 - th * th) * (_GELU_C * (1.0 + 3.0 * _GELU_A * y * y))
    return val, grad


def _params(**kw):
    return pltpu.CompilerParams(vmem_limit_bytes=VMEM_LIMIT, **kw)


def _ssm_dense(a_re, a_im, log_dt, b_re, b_im, c_re, c_im):
    dt = jnp.exp(log_dt)[:, None]
    mag = jnp.exp(a_re * dt)
    ang = a_im * dt
    lb_re = mag * jnp.cos(ang)
    lb_im = mag * jnp.sin(ang)
    den = a_re * a_re + a_im * a_im
    n_re = lb_re - 1.0
    n_im = lb_im
    f_re = (n_re * a_re + n_im * a_im) / den
    f_im = (n_im * a_re - n_re * a_im) / den
    bb_re = f_re[..., None] * b_re - f_im[..., None] * b_im
    bb_im = f_re[..., None] * b_im + f_im[..., None] * b_re

    bb = jnp.stack([bb_re, bb_im], axis=0).reshape(2, STATE_ROWS, CHUNK_GROUPS, SSM_P, SSM_C)
    rb = bb.transpose(1, 4, 0, 2, 3).reshape(STATE_ROWS, SSM_C, 2 * STATE_COLS)
    cc = jnp.stack([c_re, -c_im], axis=0).reshape(2, STATE_ROWS, CHUNK_GROUPS, SSM_C, SSM_P)
    rc = cc.transpose(1, 3, 0, 2, 4).reshape(STATE_ROWS, SSM_C, 2 * STATE_COLS)
    return (lb_re.reshape(STATE_ROWS, STATE_COLS), lb_im.reshape(STATE_ROWS, STATE_COLS), rb, rc)


def _ssm_chunked(per_channel):
    row_group = jnp.arange(CHUNK_CH) // SSM_C
    col_group = (jnp.arange(2 * STATE_COLS) // SSM_P) % CHUNK_GROUPS
    own_group = (row_group[:, None] == col_group[None, :]).astype(F32)
    even = (jnp.arange(STATE_ROWS) % 2 == 0).astype(F32)[:, None, None]
    half = jnp.tile(per_channel, (1, CHUNK_GROUPS, 1)) * own_group
    return jnp.concatenate([half * even, half * (1.0 - even)], axis=1)


def _inproj_fwd(x2, g_row, w_all, dep):
    n = x2.shape[0]
    tm = TM_FWD

    def body(x_ref, g_ref, w_ref, dep_ref, z_ref, h_ref):
        x = x_ref[...]
        r = lax.rsqrt(jnp.mean(x * x, axis=-1, keepdims=True) + NORM_EPS)
        h = _mx(x * r * g_ref[...])
        h_ref[...] = h
        for d in range(N_DEV):
            z_ref[:, d * W_IN_COLS:(d + 1) * W_IN_COLS] = _mm(h, w_ref[d])

    return pl.pallas_call(
        body, name="inproj_fwd",
        grid=(n // tm,),
        in_specs=[pl.BlockSpec((tm, D_MODEL), lambda i: (i, 0)),
                  pl.BlockSpec((1, D_MODEL), lambda i: (0, 0)),
                  pl.BlockSpec((N_DEV, D_MODEL, W_IN_COLS), lambda i: (0, 0, 0)),
                  ANY_SPEC],
        out_specs=[pl.BlockSpec((tm, 2 * MIX), lambda i: (i, 0)),
                   pl.BlockSpec((tm, D_MODEL), lambda i: (i, 0))],
        out_shape=[jax.ShapeDtypeStruct((n, 2 * MIX), F32),
                   jax.ShapeDtypeStruct((n, D_MODEL), MXU_DTYPE)],
        compiler_params=_params(dimension_semantics=("arbitrary",)),
    )(x2, g_row, w_all, dep)


def _loss_head(x2, tgt2, g_row):
    n = x2.shape[0]
    tm = TM_FWD

    def body(x_ref, t_ref, g_ref, dx_ref, loss_ref, dg_ref):
        @pl.when(pl.program_id(0) == 0)
        def _():
            loss_ref[...] = jnp.zeros_like(loss_ref)
            dg_ref[...] = jnp.zeros_like(dg_ref)

        x = x_ref[...]
        g = g_ref[...]
        r = lax.rsqrt(jnp.mean(x * x, axis=-1, keepdims=True) + NORM_EPS)
        xh = x * r
        e = xh * g - t_ref[...]
        loss_ref[...] += jnp.sum(jnp.sum(e * e, axis=-1, keepdims=True), axis=0, keepdims=True) * (0.5 / D_MODEL)
        dout = e * (1.0 / D_MODEL)
        dg_ref[...] += jnp.sum(dout * xh, axis=0, keepdims=True)
        gdy = dout * g
        dx_ref[...] = r * (gdy - xh * jnp.mean(xh * gdy, axis=-1, keepdims=True))

    return pl.pallas_call(
        body, name="loss_head",
        grid=(n // tm,),
        in_specs=[pl.BlockSpec((tm, D_MODEL), lambda i: (i, 0)),
                  pl.BlockSpec((tm, D_MODEL), lambda i: (i, 0)),
                  pl.BlockSpec((1, D_MODEL), lambda i: (0, 0))],
        out_specs=[pl.BlockSpec((tm, D_MODEL), lambda i: (i, 0)),
                   pl.BlockSpec((1, 1), lambda i: (0, 0)),
                   pl.BlockSpec((1, D_MODEL), lambda i: (0, 0))],
        out_shape=[jax.ShapeDtypeStruct((n, D_MODEL), F32),
                   jax.ShapeDtypeStruct((1, 1), F32),
                   jax.ShapeDtypeStruct((1, D_MODEL), F32)],
        compiler_params=_params(dimension_semantics=("arbitrary",)),
    )(x2, tgt2, g_row)


def _outproj_bwd(dx2, yg, w_out, dep):
    n = dx2.shape[0]
    tm = TM_BWD
    n_steps = n // tm

    def body(dx_ref, y_ref, w_ref, dep_ref, dy_ref, dw_ref, acc_ref):
        i = pl.program_id(0)

        @pl.when(i == 0)
        def _():
            acc_ref[...] = jnp.zeros_like(acc_ref)

        dxb = _mx(dx_ref[...])
        dy_ref[...] = _mm_nt(dxb, w_ref[...])
        acc_ref[...] += _mm_tn(y_ref[...], dxb)

        @pl.when(i == n_steps - 1)
        def _():
            dw_ref[...] = _mx(acc_ref[...])

    return pl.pallas_call(
        body, name="outproj_bwd",
        grid=(n_steps,),
        in_specs=[pl.BlockSpec((tm, D_MODEL), lambda i: (i, 0)),
                  pl.BlockSpec((tm, MIX), lambda i: (i, 0)),
                  pl.BlockSpec((MIX, D_MODEL), lambda i: (0, 0)),
                  ANY_SPEC],
        out_specs=[pl.BlockSpec((tm, MIX), lambda i: (i, 0)),
                   pl.BlockSpec((MIX, D_MODEL), lambda i: (0, 0))],
        out_shape=[jax.ShapeDtypeStruct((n, MIX), F32),
                   jax.ShapeDtypeStruct((MIX, D_MODEL), MXU_DTYPE)],
        scratch_shapes=[pltpu.VMEM((MIX, D_MODEL), F32)],
        compiler_params=_params(dimension_semantics=("arbitrary",)),
    )(dx2, yg, w_out, dep)


def _inproj_bwd(dz, h, x2, dx_in, g_row, w_all, dep):
    n = x2.shape[0]
    tm = TM_BWD
    n_steps = n // tm

    def body(dz_ref, h_ref, x_ref, dxi_ref, g_ref, w_ref, dep_ref, dxo_ref, dw_ref, dg_ref, acc_ref, wcat_ref):
        i = pl.program_id(0)

        @pl.when(i == 0)
        def _():
            acc_ref[...] = jnp.zeros_like(acc_ref)
            dg_ref[...] = jnp.zeros_like(dg_ref)
            for d in range(N_DEV):
                wcat_ref[:, d * W_IN_COLS:(d + 1) * W_IN_COLS] = w_ref[d]

        hb = h_ref[...]
        for d in range(N_DEV):
            acc_ref[d] += _mm_tn(hb, dz_ref[:, d * W_IN_COLS:(d + 1) * W_IN_COLS])
        dh = _mm_nt(dz_ref[...], wcat_ref[...])
        x = x_ref[...]
        r = lax.rsqrt(jnp.mean(x * x, axis=-1, keepdims=True) + NORM_EPS)
        xh = x * r
        dg_ref[...] += jnp.sum(dh * xh, axis=0, keepdims=True)
        gdy = dh * g_ref[...]
        dxo_ref[...] = dxi_ref[...] + r * (gdy - xh * jnp.mean(xh * gdy, axis=-1, keepdims=True))

        @pl.when(i == n_steps - 1)
        def _():
            dw_ref[...] = _mx(acc_ref[...])

    return pl.pallas_call(
        body, name="inproj_bwd",
        grid=(n_steps,),
        in_specs=[pl.BlockSpec((tm, 2 * MIX), lambda i: (i, 0)),
                  pl.BlockSpec((tm, D_MODEL), lambda i: (i, 0)),
                  pl.BlockSpec((tm, D_MODEL), lambda i: (i, 0)),
                  pl.BlockSpec((tm, D_MODEL), lambda i: (i, 0)),
                  pl.BlockSpec((1, D_MODEL), lambda i: (0, 0)),
                  pl.BlockSpec((N_DEV, D_MODEL, W_IN_COLS), lambda i: (0, 0, 0)),
                  ANY_SPEC],
        out_specs=[pl.BlockSpec((tm, D_MODEL), lambda i: (i, 0)),
                   pl.BlockSpec((N_DEV, D_MODEL, W_IN_COLS), lambda i: (0, 0, 0)),
                   pl.BlockSpec((1, D_MODEL), lambda i: (0, 0))],
        out_shape=[jax.ShapeDtypeStruct((n, D_MODEL), F32),
                   jax.ShapeDtypeStruct((N_DEV, D_MODEL, W_IN_COLS), MXU_DTYPE),
                   jax.ShapeDtypeStruct((1, D_MODEL), F32)],
        scratch_shapes=[pltpu.VMEM((N_DEV, D_MODEL, W_IN_COLS), F32),
                        pltpu.VMEM((D_MODEL, 2 * MIX), MXU_DTYPE)],
        compiler_params=_params(dimension_semantics=("arbitrary",)),
    )(dz, h, x2, dx_in, g_row, w_all, dep)


def _row_pos(t0, rows):
    return t0 + lax.broadcasted_iota(jnp.int32, (rows, LANES), 0)


def _pool_window_mean(upad, g, t0, t_blk):
    k = 2 << g
    w = upad
    sh = 1
    while sh < k:
        w = w + pltpu.roll(w, sh, 0)
        sh *= 2
    count = jnp.minimum(_row_pos(t0, t_blk) + 1, k).astype(F32)
    return w[HALO:] / count - upad[HALO:]


def _pool_window_bwd(qpad, g, t_blk):
    k = 2 << g
    n = t_blk + HALO
    w = qpad
    sh = 1
    while sh < k:
        w = w + pltpu.roll(w, n - sh, 0)
        sh *= 2
    return w[:t_blk]


class _StateBuf:
    def __init__(self, refs, t_blk):
        self.refs = refs
        self.t_blk = t_blk

    def put_chunk(self, b, j, val):
        for c in range(4):
            self.refs[4 * b + c][pl.ds(j, self.t_blk, stride=STATE_ROWS), :] = val[:, c * LANES:(c + 1) * LANES]

    def get_chunk(self, b, j):
        return jnp.concatenate(
            [self.refs[4 * b + c][pl.ds(j, self.t_blk, stride=STATE_ROWS), :] for c in range(4)], axis=-1)

    def load(self, b, r, part):
        return jnp.concatenate(
            [self.refs[4 * b + 2 * part + h][pl.ds(r, STATE_ROWS), :] for h in range(2)], axis=-1)

    def store(self, b, r, part, val):
        for h in range(2):
            self.refs[4 * b + 2 * part + h][pl.ds(r, STATE_ROWS), :] = val[:, h * LANES:(h + 1) * LANES]


def _state_scratch(nb, t_blk):
    return [pltpu.VMEM((t_blk * STATE_ROWS, LANES), F32) for _ in range(4 * nb)]


def _ssm_project_in(u_ssm, wb_ref, buf, nb):
    t_blk = u_ssm.shape[0] // nb
    ub = _mx(u_ssm)
    for j in range(STATE_ROWS):
        m = j // 2
        bu = _mm(ub[:, m * LANES:(m + 1) * LANES], wb_ref[j])
        for b in range(nb):
            buf.put_chunk(b, j, bu[b * t_blk:(b + 1) * t_blk])


def _scan_forward(buf, lbr, lbi, init, nb):
    def body(t, carry):
        r = pl.multiple_of(t * STATE_ROWS, STATE_ROWS)
        out = []
        for b in range(nb):
            sr, si = carry[2 * b], carry[2 * b + 1]
            nr = lbr * sr - lbi * si + buf.load(b, r, 0)
            ni = lbr * si + lbi * sr + buf.load(b, r, 1)
            buf.store(b, r, 0, nr)
            buf.store(b, r, 1, ni)
            out += [nr, ni]
        return tuple(out)

    return lax.fori_loop(0, buf.t_blk, body, init, unroll=4)


def _ssm_project_out(chunk, wc_ref):
    tiles = []
    for m in range(4):
        acc = None
        for j in (2 * m, 2 * m + 1):
            part = _mm_nt(chunk(j), wc_ref[j])
            acc = part if acc is None else acc + part
        tiles.append(acc)
    return jnp.concatenate(tiles, axis=-1)


def _layer_fwd(x3, z3, g_row, w_in, pool_w, pool_scale, lbr, lbi, wb, wc, d_skip, glu_w, glu_b, w_out, dep):
    nb, seq, _ = x3.shape
    t_blk = min(T_BLK, seq)
    n_t = seq // t_blk
    halo_per_blk = t_blk // HALO
    rows = nb * t_blk
    fused = z3 is None

    def body(*refs):
        if fused:
            (x_ref, g_ref, wi_ref, pw_ref, ps_ref, lbr_ref, lbi_ref, wb_ref, wc_ref, dsk_ref, gw_ref, gb_ref, wo_ref,
             dep_ref, z_ref, h_ref, yg_ref, sc_ref, act_ref, dact_ref, pooled_ref, ypre_ref, xo_ref,
             carry_ref, halo_ref, *s_refs) = refs
        else:
            (x_ref, z_ref, zh_ref, pw_ref, ps_ref, lbr_ref, lbi_ref, wb_ref, wc_ref, dsk_ref, gw_ref, gb_ref, wo_ref,
             dep_ref, yg_ref, sc_ref, act_ref, dact_ref, pooled_ref, ypre_ref, xo_ref, carry_ref, *s_refs) = refs
        i = pl.program_id(0)
        t0 = i * t_blk
        buf = _StateBuf(s_refs, t_blk)
        both = lambda lo, hi: z_ref[:, :, lo:hi].reshape(rows, hi - lo)

        @pl.when(i == 0)
        def _():
            carry_ref[...] = jnp.zeros_like(carry_ref)
            if fused:
                halo_ref[...] = jnp.zeros_like(halo_ref)

        x = x_ref[...].reshape(rows, D_MODEL)
        if fused:
            r = lax.rsqrt(jnp.mean(x * x, axis=-1, keepdims=True) + NORM_EPS)
            h = _mx(x * r * g_ref[...])
            h_ref[...] = h.reshape(nb, t_blk, D_MODEL)
            for d in range(N_DEV):
                z_ref[:, :, d * W_IN_COLS:(d + 1) * W_IN_COLS] = _mm(h, wi_ref[d]).reshape(nb, t_blk, W_IN_COLS)

        u_ssm = both(POOL_W, MIX)
        _ssm_project_in(u_ssm, wb_ref, buf, nb)
        init = tuple(carry_ref[b, :, h * STATE_COLS:(h + 1) * STATE_COLS] for b in range(nb) for h in range(2))
        fin = _scan_forward(buf, lbr_ref[...], lbi_ref[...], init, nb)
        for b in range(nb):
            carry_ref[b, :, 0:STATE_COLS] = fin[2 * b]
            carry_ref[b, :, STATE_COLS:2 * STATE_COLS] = fin[2 * b + 1]

        def chunk(j):
            states = _mx(jnp.concatenate([buf.get_chunk(b, j) for b in range(nb)], axis=0))
            sc_ref[:, j] = states.reshape(nb, t_blk, 2 * STATE_COLS)
            return states

        y = _ssm_project_out(chunk, wc_ref) + dsk_ref[...] * u_ssm
        yg, dgelu = _gelu_and_grad(y)
        ygb = _mx(yg)
        act_ref[...] = ygb.reshape(nb, t_blk, SSM_W)
        dact_ref[...] = _mx(dgelu).reshape(nb, t_blk, SSM_W)
        o_ssm = yg * _sigmoid(_mm(ygb, gw_ref[...]) + gb_ref[...])
        gp = both(MIX + POOL_W, 2 * MIX)
        parts = []
        first = (i == 0)
        for g in range(N_POOL_G):
            cols = slice(g * POOL_GC, (g + 1) * POOL_GC)
            pooled = []
            for b in range(nb):
                halo = halo_ref[b, :, cols] if fused else jnp.where(first, 0.0, zh_ref[b, :, cols])
                pooled.append(_pool_window_mean(jnp.concatenate([halo, z_ref[b, :, cols]], axis=0), g, t0, t_blk))
            pb = _mx(jnp.concatenate(pooled, axis=0))
            ypre = _mm(pb, pw_ref[g])
            pooled_ref[:, :, cols] = pb.reshape(nb, t_blk, POOL_GC)
            ypre_ref[:, :, cols] = ypre.reshape(nb, t_blk, POOL_GC)
            gpp = both(MIX + g * POOL_GC, MIX + (g + 1) * POOL_GC)
            parts.append(_mx(ypre * ps_ref[:, cols] * (gpp * _sigmoid(gpp))))
        parts.append(_mx(o_ssm * (gp * _sigmoid(gp))))
        gated = jnp.concatenate(parts, axis=-1)
        yg_ref[...] = gated.reshape(nb, t_blk, MIX)
        xo_ref[...] = (x + _mm(gated, wo_ref[...])).reshape(nb, t_blk, D_MODEL)
        if fused:
            halo_ref[...] = z_ref[:, t_blk - HALO:, 0:POOL_W]

    const = lambda *shape: pl.BlockSpec(shape, lambda i: (0,) * len(shape))
    tokens = lambda width: pl.BlockSpec((nb, t_blk, width), lambda i: (0, i, 0))
    mixer_specs = [const(N_POOL_G, POOL_GC, POOL_GC), const(1, POOL_W),
                   const(STATE_ROWS, STATE_COLS), const(STATE_ROWS, STATE_COLS),
                   const(STATE_ROWS, LANES, 2 * STATE_COLS), const(STATE_ROWS, LANES, 2 * STATE_COLS),
                   const(1, SSM_W), const(SSM_W, SSM_W), const(1, SSM_W), const(MIX, D_MODEL), ANY_SPEC]
    mixer_args = (pool_w, pool_scale, lbr, lbi, wb, wc, d_skip, glu_w, glu_b, w_out, dep)
    out_specs = [tokens(MIX), pl.BlockSpec((nb, STATE_ROWS, t_blk, 2 * STATE_COLS), lambda i: (0, 0, i, 0)),
                 tokens(SSM_W), tokens(SSM_W), tokens(POOL_W), tokens(POOL_W), tokens(D_MODEL)]
    out_shape = [jax.ShapeDtypeStruct((nb, seq, MIX), MXU_DTYPE),
                 jax.ShapeDtypeStruct((nb, STATE_ROWS, seq, 2 * STATE_COLS), MXU_DTYPE),
                 jax.ShapeDtypeStruct((nb, seq, SSM_W), MXU_DTYPE),
                 jax.ShapeDtypeStruct((nb, seq, SSM_W), MXU_DTYPE),
                 jax.ShapeDtypeStruct((nb, seq, POOL_W), MXU_DTYPE),
                 jax.ShapeDtypeStruct((nb, seq, POOL_W), F32),
                 jax.ShapeDtypeStruct((nb, seq, D_MODEL), F32)]
    scratch = [pltpu.VMEM((nb, STATE_ROWS, 2 * STATE_COLS), F32)]
    if fused:
        in_specs = [tokens(D_MODEL), const(1, D_MODEL), const(N_DEV, D_MODEL, W_IN_COLS)] + mixer_specs
        args = (x3, g_row, w_in) + mixer_args
        out_specs = [tokens(2 * MIX), tokens(D_MODEL)] + out_specs
        out_shape = [jax.ShapeDtypeStruct((nb, seq, 2 * MIX), F32),
                     jax.ShapeDtypeStruct((nb, seq, D_MODEL), MXU_DTYPE)] + out_shape
        scratch = scratch + [pltpu.VMEM((nb, HALO, POOL_W), F32)]
    else:
        in_specs = [tokens(D_MODEL), tokens(2 * MIX),
                    pl.BlockSpec((nb, HALO, POOL_W), lambda i: (0, jnp.maximum(i * halo_per_blk - 1, 0), 0))] + mixer_specs
        args = (x3, z3, z3) + mixer_args
    return pl.pallas_call(
        body, name="layer_fwd" if fused else "mixer_fwd",
        grid=(n_t,),
        in_specs=in_specs, out_specs=out_specs, out_shape=out_shape,
        scratch_shapes=scratch + _state_scratch(nb, t_blk),
        compiler_params=_params(dimension_semantics=("arbitrary",)),
    )(*args)


def _mixer_bwd(z3, dy3, states, kept, pool_w, pool_scale, lbr, lbi, wb, wc, d_skip, glu_w, glu_b):
    nb, seq, _ = z3.shape
    t_blk = min(T_BLK, seq)
    n_t = seq // t_blk
    halo_per_blk = t_blk // HALO
    rows = nb * t_blk

    def body(z_ref, dy_ref, sc_ref, sch_ref, act_ref, dact_ref, pooled_ref, ypre_ref, pw_ref, ps_ref, lbr_ref, lbi_ref, wb_ref, wc_ref, dsk_ref,
             gw_ref, gb_ref,
             dz_ref, dpw_ref, dps_ref, dlbr_ref, dlbi_ref, dwb_ref, dwc_ref, ddsk_ref, dgw_ref, dgb_ref,
             gcarry_ref, qcarry_ref, du_ref, dgw_acc, *g_refs):
        i = pl.program_id(0)
        blk = n_t - 1 - i
        t0 = blk * t_blk
        gbuf = _StateBuf(g_refs, t_blk)

        @pl.when(i == 0)
        def _():
            gcarry_ref[...] = jnp.zeros_like(gcarry_ref)
            qcarry_ref[...] = jnp.zeros_like(qcarry_ref)
            for ref in (dpw_ref, dps_ref, dlbr_ref, dlbi_ref, dwb_ref, dwc_ref, ddsk_ref, dgw_acc, dgb_ref):
                ref[...] = jnp.zeros_like(ref)

        lbr_v = lbr_ref[...]
        lbi_v = lbi_ref[...]

        both = lambda ref, lo, hi: ref[:, :, lo:hi].reshape(rows, hi - lo)
        split = lambda val: val.reshape(nb, t_blk, val.shape[-1])
        states = lambda j: sc_ref[:, j].reshape(rows, 2 * STATE_COLS)
        first = (blk == 0)

        u_ssm = both(z_ref, POOL_W, MIX)
        ygb = act_ref[...].reshape(rows, SSM_W)
        yg = ygb.astype(F32)
        dgelu = dact_ref[...].reshape(rows, SSM_W).astype(F32)
        sg = _sigmoid(_mm(ygb, gw_ref[...]) + gb_ref[...])
        o_ssm = yg * sg
        gp = both(z_ref, MIX + POOL_W, 2 * MIX)
        sgm = _sigmoid(gp)
        dyv = both(dy_ref, POOL_W, MIX)
        dz_ref[:, :, MIX + POOL_W:2 * MIX] = split(_mx(dyv * o_ssm * (sgm * (1.0 + gp * (1.0 - sgm)))))
        do = dyv * (gp * sgm)
        dv = do * yg * (sg * (1.0 - sg))
        dvb = _mx(dv)
        dgb_ref[...] += jnp.sum(dv, axis=0, keepdims=True)
        dgw_acc[...] += _mm_tn(ygb, dvb)
        dyp = (do * sg + _mm_nt(dvb, gw_ref[...])) * dgelu
        ddsk_ref[...] += jnp.sum(dyp * u_ssm, axis=0, keepdims=True)
        dypb = _mx(dyp)
        for j in range(STATE_ROWS):
            m = j // 2
            dyt = dypb[:, m * LANES:(m + 1) * LANES]
            ds = _mm(dyt, wc_ref[j])
            for b in range(nb):
                gbuf.put_chunk(b, j, ds[b * t_blk:(b + 1) * t_blk])
            dwc_ref[j] += _mm_tn(dyt, states(j))
        du_ref[...] = split(dsk_ref[...] * dyp)

        for g in range(N_POOL_G):
            cols = slice(g * POOL_GC, (g + 1) * POOL_GC)
            pb = both(pooled_ref, g * POOL_GC, (g + 1) * POOL_GC)
            ypre = both(ypre_ref, g * POOL_GC, (g + 1) * POOL_GC)
            gpp = both(z_ref, MIX + g * POOL_GC, MIX + (g + 1) * POOL_GC)
            sgp = _sigmoid(gpp)
            dyg = both(dy_ref, g * POOL_GC, (g + 1) * POOL_GC)
            scale = ps_ref[:, cols]
            dz_ref[:, :, MIX + g * POOL_GC:MIX + (g + 1) * POOL_GC] = split(_mx(
                dyg * (ypre * scale) * (sgp * (1.0 + gpp * (1.0 - sgp)))))
            dyc = dyg * (gpp * sgp)
            dps_ref[:, cols] += jnp.sum(dyc * ypre, axis=0, keepdims=True)
            dypre = _mx(dyc * scale)
            dpw_ref[g] += _mm_tn(pb, dypre)
            dpooled = _mm_nt(dypre, pw_ref[g])
            count = jnp.minimum(_row_pos(t0, t_blk) + 1, 2 << g).astype(F32)
            for b in range(nb):
                dp = dpooled[b * t_blk:(b + 1) * t_blk]
                q = dp / count
                qpad = jnp.concatenate([q, qcarry_ref[b, :, cols]], axis=0)
                qcarry_ref[b, :, cols] = q[:HALO]
                dz_ref[b, :, cols] = _mx(_pool_window_bwd(qpad, g, t_blk) - dp)

        def rev_body(k, carry):
            r = pl.multiple_of((t_blk - 1 - k) * STATE_ROWS, STATE_ROWS)
            out = []
            for b in range(nb):
                gr, gi = carry[2 * b], carry[2 * b + 1]
                ngr = lbr_v * gr + lbi_v * gi + gbuf.load(b, r, 0)
                ngi = lbr_v * gi - lbi_v * gr + gbuf.load(b, r, 1)
                gbuf.store(b, r, 0, ngr)
                gbuf.store(b, r, 1, ngi)
                out += [ngr, ngi]
            return tuple(out)

        init_g = tuple(gcarry_ref[b, :, h * STATE_COLS:(h + 1) * STATE_COLS] for b in range(nb) for h in range(2))
        fin = lax.fori_loop(0, t_blk, rev_body, init_g, unroll=4)
        for b in range(nb):
            gcarry_ref[b, :, 0:STATE_COLS] = fin[2 * b]
            gcarry_ref[b, :, STATE_COLS:2 * STATE_COLS] = fin[2 * b + 1]

        ub = _mx(u_ssm)
        for m in range(4):
            acc = both(du_ref, m * LANES, (m + 1) * LANES)
            for j in (2 * m, 2 * m + 1):
                g = jnp.concatenate([gbuf.get_chunk(b, j) for b in range(nb)], axis=0)
                gj = _mx(g)
                acc = acc + _mm_nt(gj, wb_ref[j])
                dwb_ref[j] += _mm_tn(ub[:, m * LANES:(m + 1) * LANES], gj)
                shifted = []
                for b in range(nb):
                    before = jnp.where(first, 0.0, sch_ref[b, j].astype(F32))
                    spad = jnp.concatenate([before, sc_ref[b, j].astype(F32)], axis=0)
                    shifted.append(pltpu.roll(spad, 1, 0)[HALO:])
                s_prev = jnp.concatenate(shifted, axis=0)
                g_re, g_im = g[:, :STATE_COLS], g[:, STATE_COLS:]
                p_re, p_im = s_prev[:, :STATE_COLS], s_prev[:, STATE_COLS:]
                dlbr_ref[j:j + 1, :] += jnp.sum(g_re * p_re + g_im * p_im, axis=0, keepdims=True)
                dlbi_ref[j:j + 1, :] += jnp.sum(g_im * p_re - g_re * p_im, axis=0, keepdims=True)
            dz_ref[:, :, POOL_W + m * LANES:POOL_W + (m + 1) * LANES] = split(_mx(acc))

        @pl.when(i == n_t - 1)
        def _():
            dgw_ref[...] = _mx(dgw_acc[...])

    const = lambda *shape: pl.BlockSpec(shape, lambda i: (0,) * len(shape))
    rev = lambda i: n_t - 1 - i
    out_shape = [jax.ShapeDtypeStruct((nb, seq, 2 * MIX), MXU_DTYPE),
                 jax.ShapeDtypeStruct((N_POOL_G, POOL_GC, POOL_GC), F32),
                 jax.ShapeDtypeStruct((1, POOL_W), F32),
                 jax.ShapeDtypeStruct((STATE_ROWS, STATE_COLS), F32),
                 jax.ShapeDtypeStruct((STATE_ROWS, STATE_COLS), F32),
                 jax.ShapeDtypeStruct((STATE_ROWS, LANES, 2 * STATE_COLS), F32),
                 jax.ShapeDtypeStruct((STATE_ROWS, LANES, 2 * STATE_COLS), F32),
                 jax.ShapeDtypeStruct((1, SSM_W), F32),
                 jax.ShapeDtypeStruct((SSM_W, SSM_W), MXU_DTYPE),
                 jax.ShapeDtypeStruct((1, SSM_W), F32)]
    return pl.pallas_call(
        body, name="mixer_bwd",
        grid=(n_t,),
        in_specs=[pl.BlockSpec((nb, t_blk, 2 * MIX), lambda i: (0, rev(i), 0)),
                  pl.BlockSpec((nb, t_blk, MIX), lambda i: (0, rev(i), 0)),
                  pl.BlockSpec((nb, STATE_ROWS, t_blk, 2 * STATE_COLS), lambda i: (0, 0, rev(i), 0)),
                  pl.BlockSpec((nb, STATE_ROWS, HALO, 2 * STATE_COLS),
                               lambda i: (0, 0, jnp.maximum(rev(i) * halo_per_blk - 1, 0), 0)),
                  pl.BlockSpec((nb, t_blk, SSM_W), lambda i: (0, rev(i), 0)),
                  pl.BlockSpec((nb, t_blk, SSM_W), lambda i: (0, rev(i), 0)),
                  pl.BlockSpec((nb, t_blk, POOL_W), lambda i: (0, rev(i), 0)),
                  pl.BlockSpec((nb, t_blk, POOL_W), lambda i: (0, rev(i), 0)),
                  const(N_POOL_G, POOL_GC, POOL_GC), const(1, POOL_W),
                  const(STATE_ROWS, STATE_COLS), const(STATE_ROWS, STATE_COLS),
                  const(STATE_ROWS, LANES, 2 * STATE_COLS), const(STATE_ROWS, LANES, 2 * STATE_COLS),
                  const(1, SSM_W), const(SSM_W, SSM_W), const(1, SSM_W)],
        out_specs=[pl.BlockSpec((nb, t_blk, 2 * MIX), lambda i: (0, rev(i), 0))]
                  + [const(*s.shape) for s in out_shape[1:]],
        out_shape=out_shape,
        scratch_shapes=[pltpu.VMEM((nb, STATE_ROWS, 2 * STATE_COLS), F32),
                        pltpu.VMEM((nb, HALO, POOL_W), F32),
                        pltpu.VMEM((nb, t_blk, SSM_W), F32),
                        pltpu.VMEM((SSM_W, SSM_W), F32)]
                       + _state_scratch(nb, t_blk),
        compiler_params=_params(dimension_semantics=("arbitrary",)),
    )(z3, dy3, states, states, *kept, pool_w, pool_scale, lbr, lbi, wb, wc, d_skip, glu_w, glu_b)


def _mesh_place():
    x, y, c = lax.axis_index("x"), lax.axis_index("y"), lax.axis_index("c")
    return x, y, c


def _flip(place, k):
    x, y, c = place
    return (1 - x if k & 4 else x, 1 - y if k & 2 else y, 1 - c if k & 1 else c)


def _index(place):
    x, y, c = place
    return 4 * x + 2 * y + c


HBM_SPEC = pl.BlockSpec(memory_space=pltpu.HBM)
SEM_SPEC = pl.BlockSpec(memory_space=pltpu.SEMAPHORE)
_EFFECT = pltpu.SideEffectType.DATAFLOW_SIDE_EFFECTING
N_PEERS = N_DEV - 1


def _exchange_copies(src_refs, land_refs, send_sems, recv_sems):
    me = _mesh_place()
    mine = _index(me)
    out = []
    for a, land_ref in enumerate(land_refs):
        for k in range(1, N_DEV):
            peer = _flip(me, k)
            theirs = _index(peer)
            n = a * N_PEERS + k - 1
            src = src_refs[a].at[theirs] if src_refs else land_ref.at[mine]
            send = pltpu.make_async_remote_copy(
                src_ref=src, dst_ref=land_ref.at[mine], send_sem=send_sems.at[n], recv_sem=recv_sems.at[n],
                device_id=peer, device_id_type=MESH)
            recv = pltpu.make_async_remote_copy(
                src_ref=src, dst_ref=land_ref.at[theirs], send_sem=send_sems.at[n], recv_sem=recv_sems.at[n],
                device_id=peer, device_id_type=MESH)
            out.append((send, recv))
    return out


def _exchange_start(srcs, lands, after, name):
    arrays = tuple(srcs) + tuple(lands)
    n_src, n_all = len(srcs), len(arrays)
    n_copies = len(lands) * N_PEERS

    def body(*refs):
        send_sems, recv_sems = refs[n_all + 1], refs[n_all + 2]
        token = refs[-1]
        for send, _ in _exchange_copies(refs[:n_src], refs[n_src:n_all], send_sems, recv_sems):
            send.start()
        token[...] = jnp.zeros_like(token)

    res = pl.pallas_call(
        body, name=name,
        in_specs=[HBM_SPEC] * n_all + [ANY_SPEC],
        out_specs=[SEM_SPEC, SEM_SPEC] + [HBM_SPEC] * n_all + [VMEM_SPEC],
        out_shape=[pltpu.SemaphoreType.DMA((n_copies,)), pltpu.SemaphoreType.DMA((n_copies,))]
                  + [pltpu.HBM(a.shape, a.dtype) for a in arrays] + [jax.ShapeDtypeStruct((SUBLANES, LANES), F32)],
        input_output_aliases={i: 2 + i for i in range(n_all)},
        compiler_params=pltpu.CompilerParams(has_side_effects=_EFFECT),
    )(*[pltpu.with_memory_space_constraint(a, pltpu.HBM) for a in arrays], after)
    return tuple(res[:-1]), res[-1]


def _exchange_wait(handle, n_lands, after, name):
    send_sems, recv_sems = handle[0], handle[1]
    arrays = handle[2:]
    n_all = len(arrays)
    n_src = n_all - n_lands

    def body(*refs):
        for send, recv in _exchange_copies(refs[:n_src], refs[n_src:n_all], refs[n_all], refs[n_all + 1]):
            send.wait_send()
            recv.wait_recv()

    res = pl.pallas_call(
        body, name=name,
        in_specs=[HBM_SPEC] * n_all + [SEM_SPEC, SEM_SPEC, ANY_SPEC],
        out_specs=[HBM_SPEC] * n_all,
        out_shape=[pltpu.HBM(a.shape, a.dtype) for a in arrays],
        input_output_aliases={i: i for i in range(n_all)},
        compiler_params=pltpu.CompilerParams(has_side_effects=_EFFECT),
    )(*arrays, send_sems, recv_sems, after)
    return tuple(res[:n_src]), tuple(res[n_src:])


def _weight_zones(w_in, glu_w, w_out, my_idx):
    shards = (w_in, glu_w, w_out)
    depth = w_in.shape[0]

    def body(idx_ref, *refs):
        ins, zones = refs[:len(shards)], refs[len(shards):]
        for l in range(depth):
            for a, src in enumerate(ins):
                zones[l * len(shards) + a][0] = _mx(src[l])

    whole = lambda s: pl.BlockSpec(s.shape, lambda i, idx: (0,) * s.ndim)
    return pl.pallas_call(
        body, name="weight_zones",
        grid_spec=pltpu.PrefetchScalarGridSpec(
            num_scalar_prefetch=1, grid=(1,),
            in_specs=[whole(s) for s in shards],
            out_specs=[pl.BlockSpec((1,) + s.shape[1:], lambda i, idx: (idx[0], 0, 0))
                       for _ in range(depth) for s in shards]),
        out_shape=[jax.ShapeDtypeStruct((N_DEV,) + s.shape[1:], MXU_DTYPE) for _ in range(depth) for s in shards],
        compiler_params=_params(dimension_semantics=("arbitrary",)),
    )(my_idx.reshape(1).astype(jnp.int32), *shards)


def _allreduce_packed(p):
    rows = p.shape[0]
    half = rows // 2
    quarter = half // 4

    def body(p_ref, o_ref, part_ref, sib_ref, got_ref, send_sems, recv_sems):
        x, y, c = _mesh_place()
        sibling = (x, y, 1 - c)
        chip = 2 * x + y
        chips = [(k, (1 - x if k & 2 else x, 1 - y if k & 1 else y, c), chip ^ k) for k in (1, 2, 3)]
        my_half = pl.multiple_of(c * half, SUBLANES)
        other_half = pl.multiple_of((1 - c) * half, SUBLANES)

        def copy(n, src, dst, to):
            return pltpu.make_async_remote_copy(src_ref=src, dst_ref=dst, send_sem=send_sems.at[n],
                                                recv_sem=recv_sems.at[n], device_id=to, device_id_type=MESH)

        def quarter_of(ref, base, q):
            return ref.at[pl.ds(pl.multiple_of(base + q * quarter, SUBLANES), quarter)]

        swap = copy(0, p_ref.at[pl.ds(other_half, half)], sib_ref, sibling)
        swap.start()
        swap.wait()
        part_ref[...] = p_ref[pl.ds(my_half, half), :] + sib_ref[...]

        scatter = [copy(k, quarter_of(part_ref, 0, q), got_ref.at[k - 1], to) for k, to, q in chips]
        for cp in scatter:
            cp.start()
        total = part_ref[pl.ds(pl.multiple_of(chip * quarter, SUBLANES), quarter), :]
        for cp, (k, _, _) in zip(scatter, chips):
            cp.wait()
            total = total + got_ref[k - 1]
        mine = pl.multiple_of(my_half + chip * quarter, SUBLANES)
        o_ref[pl.ds(mine, quarter), :] = total

        gather = [copy(3 + k, o_ref.at[pl.ds(mine, quarter)], o_ref.at[pl.ds(mine, quarter)], to) for k, to, _ in chips]
        for cp in gather:
            cp.start()
        for k, to, q in chips:
            theirs = quarter_of(o_ref, my_half, q)
            copy(3 + k, theirs, theirs, to).wait_recv()
        for cp in gather:
            cp.wait_send()

        back = copy(7, o_ref.at[pl.ds(my_half, half)], o_ref.at[pl.ds(my_half, half)], sibling)
        back.start()
        copy(7, o_ref.at[pl.ds(other_half, half)], o_ref.at[pl.ds(other_half, half)], sibling).wait_recv()
        back.wait_send()

    return pl.pallas_call(
        body, name="comm_allreduce_packed",
        in_specs=[VMEM_SPEC],
        out_specs=VMEM_SPEC,
        out_shape=jax.ShapeDtypeStruct(p.shape, F32),
        scratch_shapes=[pltpu.VMEM((half, LANES), F32),
                        pltpu.VMEM((half, LANES), F32),
                        pltpu.VMEM((3, quarter, LANES), F32),
                        pltpu.SemaphoreType.DMA((8,)),
                        pltpu.SemaphoreType.DMA((8,))],
        compiler_params=_params(),
    )(p)


def _adamw_math(w, g, m, v):
    m = ADAM_B1 * m + (1.0 - ADAM_B1) * g
    v = ADAM_B2 * v + (1.0 - ADAM_B2) * (g * g)
    m_hat = m / (1.0 - ADAM_B1 ** ADAM_STEP)
    v_hat = v / (1.0 - ADAM_B2 ** ADAM_STEP)
    delta = -ADAM_LR * (m_hat / (jnp.sqrt(v_hat) + ADAM_EPS) + ADAM_WD * w)
    return delta, m, v


def _adamw_summed(received, own, my_idx, w, m, v, name):
    depth, r, c = w.shape
    tr = min(r, 128)

    def body(idx_ref, *refs):
        r_refs, o_refs = refs[:depth], refs[depth:2 * depth]
        w_ref, m_ref, v_ref, g_ref, d_ref, nm_ref, nv_ref = refs[2 * depth:]
        me = idx_ref[0]
        for l in range(depth):
            g = jnp.zeros((tr, c), F32)
            for q in range(N_DEV):
                g = g + jnp.where(q == me, o_refs[l][0], r_refs[l][q]).astype(F32)
            g_ref[l] = g
            d_ref[l], nm_ref[l], nv_ref[l] = _adamw_math(w_ref[l], g, m_ref[l], v_ref[l])

    blk = pl.BlockSpec((depth, tr, c), lambda i, idx: (0, i, 0))
    return pl.pallas_call(
        body, name=name,
        grid_spec=pltpu.PrefetchScalarGridSpec(
            num_scalar_prefetch=1, grid=(r // tr,),
            in_specs=[pl.BlockSpec((N_DEV, tr, c), lambda i, idx: (0, i, 0))] * depth
                     + [pl.BlockSpec((1, tr, c), lambda i, idx: (idx[0], i, 0))] * depth
                     + [blk, blk, blk],
            out_specs=[blk] * 4),
        out_shape=[jax.ShapeDtypeStruct((depth, r, c), F32)] * 4,
        compiler_params=_params(dimension_semantics=("arbitrary",)),
    )(my_idx.reshape(1).astype(jnp.int32), *received, *own, w, m, v)


def _adamw_small(ws, gs, ms, vs):
    n = len(ws)
    depth = ws[0].shape[0]
    quarters = 4

    def spec(a):
        per_layer = a.shape[0] == depth
        split = a.ndim >= 3 and a.shape[1] % quarters == 0 and a.shape[1] >= quarters
        block = (1, a.shape[1] // quarters if split else a.shape[1]) + a.shape[2:]
        rest = (0,) * (a.ndim - 2)
        return pl.BlockSpec(block, lambda l, s: ((l if per_layer else 0), (s if split else 0)) + rest)

    def body(*refs):
        w_refs, g_refs, m_refs, v_refs = (refs[k * n:(k + 1) * n] for k in range(4))
        d_refs, nm_refs, nv_refs = (refs[(4 + k) * n:(5 + k) * n] for k in range(3))
        for k in range(n):
            d_refs[k][...], nm_refs[k][...], nv_refs[k][...] = _adamw_math(
                w_refs[k][...], g_refs[k][...], m_refs[k][...], v_refs[k][...])

    specs = [spec(a) for a in ws]
    shapes = [jax.ShapeDtypeStruct(a.shape, F32) for a in ws]
    res = pl.pallas_call(
        body, name="adamw_small",
        grid=(depth, quarters),
        in_specs=specs * 4,
        out_specs=specs * 3,
        out_shape=shapes * 3,
        compiler_params=_params(dimension_semantics=("arbitrary", "arbitrary")),
    )(*ws, *gs, *ms, *vs)
    return res[:n], res[n:2 * n], res[2 * n:]


_PACK_ROWS = SUBLANES * N_DEV


def _pack(arrays):
    flat = jnp.concatenate([a.reshape(-1) for a in arrays])
    per = _PACK_ROWS * LANES
    total = -(-flat.shape[0] // per) * per
    flat = jnp.pad(flat, (0, total - flat.shape[0]))
    return flat.reshape(total // LANES, LANES)


def _unpack(packed, like):
    flat = packed.reshape(-1)
    out = []
    off = 0
    for a in like:
        out.append(flat[off:off + a.size].reshape(a.shape))
        off += a.size
    return out


def kernel(x, norm_g, w_in, pool_w, pool_scale, a_re, a_im, log_dt, b_re, b_im, c_re, c_im, d_skip, glu_w, glu_b, w_out, final_g, loss_target, m_norm_g, m_w_in, m_pool_w, m_pool_scale, m_a_re, m_a_im, m_log_dt, m_b_re, m_b_im, m_c_re, m_c_im, m_d_skip, m_glu_w, m_glu_b, m_w_out, m_final_g, v_norm_g, v_w_in, v_pool_w, v_pool_scale, v_a_re, v_a_im, v_log_dt, v_b_re, v_b_im, v_c_re, v_c_im, v_d_skip, v_glu_w, v_glu_b, v_w_out, v_final_g):
    nb, seq, _ = x.shape
    n_tok = nb * seq
    depth = norm_g.shape[0]

    my_idx = _index(_mesh_place())

    zones = _weight_zones(w_in, glu_w, w_out, my_idx)

    def gather_start(l, after):
        return _exchange_start((), zones[3 * l:3 * l + 3], after, f"comm_gather_start_{l}")

    def gather_wait(handle, after, l):
        _, (win, glu, wout) = _exchange_wait(handle, 3, after, f"comm_gather_wait_{l}")
        return win, glu.reshape(SSM_W, SSM_W), wout.reshape(MIX, D_MODEL)

    xs = [x.reshape(n_tok, D_MODEL)]
    first_w_in, dep = _exchange_start((), zones[0:1], xs[0], "comm_gather_start_0_w_in")

    (lbr, lbi, rb, rc), dense_vjp = jax.vjp(jax.vmap(_ssm_dense), a_re, a_im, log_dt + dep[0, 0], b_re, b_im, c_re, c_im)
    chunk_all = jax.vmap(_ssm_chunked)
    (wb, wct), chunk_vjp = jax.vjp(lambda p, q: (chunk_all(p), chunk_all(q)), rb, rc)
    wb_m, wct_m = _mx(wb), _mx(wct)
    pool_w_m = _mx(pool_w)

    def layer_params(l):
        return (pool_w_m[l], pool_scale[l][None], lbr[l], lbi[l], wb_m[l], wct_m[l], d_skip[l][None],
                weights[l][1], glu_b[l][None])

    saved = []
    weights = []
    for l in range(depth):
        if l == 0:
            _, (win,) = _exchange_wait(first_w_in, 1, wct_m, "comm_gather_wait_0_w_in")
            rest, dep = _exchange_start((), zones[1:3], win, "comm_gather_start_0_rest")
            z, h = _inproj_fwd(xs[-1], norm_g[l][None], win, dep)
            _, (glu, wout) = _exchange_wait(rest, 2, z, "comm_gather_wait_0_rest")
            weights.append((win, glu.reshape(SSM_W, SSM_W), wout.reshape(MIX, D_MODEL)))
            handle, dep = gather_start(1, weights[0][2])
            z3 = z.reshape(nb, seq, 2 * MIX)
            yg, states, *kept, x_next = _layer_fwd(xs[-1].reshape(nb, seq, D_MODEL), z3, None, None,
                                                   *layer_params(l), weights[l][2], dep)
        else:
            weights.append(gather_wait(handle, xs[-1], l))
            if l + 1 < depth:
                handle, dep = gather_start(l + 1, weights[l][0])
            z3, h3, yg, states, *kept, x_next = _layer_fwd(xs[-1].reshape(nb, seq, D_MODEL), None, norm_g[l][None],
                                                           weights[l][0], *layer_params(l), weights[l][2], dep)
            h = h3.reshape(n_tok, D_MODEL)
        xs.append(x_next.reshape(n_tok, D_MODEL))
        saved.append((z3, h, yg.reshape(n_tok, MIX), states, kept))

    dx, loss_part, d_final_g = _loss_head(xs[-1], loss_target.reshape(n_tok, D_MODEL), final_g[None])

    small = {k: [None] * depth for k in
             ("norm_g", "pool_w", "pool_scale", "lbr", "lbi", "wb", "wct", "d_skip", "glu_b")}
    received = [None] * depth
    sent = [None] * depth
    pending = None
    early = None
    for l in reversed(range(depth)):
        z3, h, yg2, states, kept = saved[l]
        dy, d_wout = _outproj_bwd(dx, yg2, weights[l][2], dep)
        (dz, d_pw, d_ps, d_lbr, d_lbi, d_wb, d_wct, d_dsk, d_gw, d_gb) = _mixer_bwd(
            z3, dy.reshape(nb, seq, MIX), states, kept, *layer_params(l))
        rest = (d_gw.reshape(N_DEV, SSM_W // N_DEV, SSM_W), d_wout.reshape(N_DEV, MIX // N_DEV, D_MODEL))
        if l == 0:
            early, dep = _exchange_start(rest, tuple(lax.empty(s.shape, s.dtype) for s in rest), dz,
                                         "comm_grads_start_0_rest")
        dx, d_win, d_ng = _inproj_bwd(dz.reshape(n_tok, 2 * MIX), h, xs[l], dx, norm_g[l][None], weights[l][0], dep)
        for k, val in (("norm_g", d_ng[0]), ("pool_w", d_pw), ("pool_scale", d_ps[0]), ("lbr", d_lbr),
                       ("lbi", d_lbi), ("wb", d_wb), ("wct", d_wct), ("d_skip", d_dsk[0]), ("glu_b", d_gb[0])):
            small[k][l] = val
        if pending is not None:
            sent[l + 1], received[l + 1] = _exchange_wait(pending, 3, dx, f"comm_grads_wait_{l + 1}")
        srcs = (d_win,) if l == 0 else (d_win,) + rest
        lands = tuple(lax.empty(s.shape, s.dtype) for s in srcs)
        pending, dep = _exchange_start(srcs, lands, dx, f"comm_grads_start_{l}")
    stack = lambda k: jnp.stack(small[k])
    d_rb, d_rc = chunk_vjp((stack("wb"), stack("wct")))
    local = [stack("norm_g"), stack("pool_w"), stack("pool_scale"), stack("lbr"), stack("lbi"), d_rb, d_rc,
             stack("d_skip"), stack("glu_b"), d_final_g[0] + dep[0, 0], loss_part[0]]
    (g_norm_g, g_pool_w, g_pool_scale, g_lbr, g_lbi, g_rb, g_rc, g_d_skip, g_glu_b, g_final_g, loss) = _unpack(
        _allreduce_packed(_pack(local)), local)
    loss = loss[0]
    g_a_re, g_a_im, g_log_dt, g_b_re, g_b_im, g_c_re, g_c_im = dense_vjp((g_lbr, g_lbi, g_rb, g_rc))

    names = ["norm_g", "pool_w", "pool_scale", "a_re", "a_im", "log_dt", "b_re", "b_im", "c_re", "c_im",
             "d_skip", "glu_b", "final_g"]
    rows = {"norm_g", "pool_scale", "log_dt", "d_skip", "glu_b"}
    small_w = [norm_g, pool_w, pool_scale, a_re, a_im, log_dt, b_re, b_im, c_re, c_im, d_skip, glu_b, final_g]
    small_g = [g_norm_g, g_pool_w, g_pool_scale, g_a_re, g_a_im, g_log_dt, g_b_re, g_b_im, g_c_re, g_c_im,
               g_d_skip, g_glu_b, g_final_g]
    small_m = [m_norm_g, m_pool_w, m_pool_scale, m_a_re, m_a_im, m_log_dt, m_b_re, m_b_im, m_c_re, m_c_im,
               m_d_skip, m_glu_b, m_final_g]
    small_v = [v_norm_g, v_pool_w, v_pool_scale, v_a_re, v_a_im, v_log_dt, v_b_re, v_b_im, v_c_re, v_c_im,
               v_d_skip, v_glu_b, v_final_g]

    wide_last = {"b_re", "b_im"}

    def blocked(arrays):
        return [a.reshape(1, 1, -1) if n == "final_g" else a[:, None, :] if n in rows
                else a.swapaxes(2, 3) if n in wide_last else a for n, a in zip(names, arrays)]

    small_d, small_nm, small_nv = _adamw_small(blocked(small_w), blocked(small_g), blocked(small_m), blocked(small_v))
    res = {}
    for kind, arrays in (("grad", small_g), ("delta", small_d), ("m", small_nm), ("v", small_nv)):
        for n, a, like in zip(names, arrays, small_w):
            if kind != "grad" and n in wide_last:
                a = a.swapaxes(2, 3)
            res[kind, n] = a.reshape(like.shape)

    (s_win,), (r_win,) = _exchange_wait(pending, 1, small_d[0], "comm_grads_wait_0")
    (s_glu, s_wout), (r_glu, r_wout) = _exchange_wait(early, 2, small_d[0], "comm_grads_wait_0_rest")
    sent[0], received[0] = (s_win, s_glu, s_wout), (r_win, r_glu, r_wout)
    shard_res = {}
    for pos, (n, w, m, v) in enumerate((("w_in", w_in, m_w_in, v_w_in), ("glu_w", glu_w, m_glu_w, v_glu_w),
                                        ("w_out", w_out, m_w_out, v_w_out))):
        shard_res[n] = _adamw_summed([received[l][pos] for l in range(depth)], [sent[l][pos] for l in range(depth)],
                                     my_idx, w, m, v, "adamw_" + n)
    for n in ("w_in", "glu_w", "w_out"):
        for pos, kind in enumerate(("grad", "delta", "m", "v")):
            res[kind, n] = shard_res[n][pos]

    order = ["norm_g", "w_in", "pool_w", "pool_scale", "a_re", "a_im", "log_dt", "b_re", "b_im", "c_re", "c_im",
             "d_skip", "glu_w", "glu_b", "w_out", "final_g"]
    outs = [loss, dx.reshape(nb, seq, D_MODEL)]
    for kind in ("grad", "delta", "m", "v"):
        outs += [res[kind, n] for n in order]
    return tuple(outs)
```

```python
import math

import jax
import jax.numpy as jnp
from jax import lax
from jax.experimental import pallas as pl
from jax.experimental.pallas import tpu as pltpu

F32 = jnp.float32
MXU_DTYPE = jnp.bfloat16

D_MODEL = 1024
MIX = 1024
POOL_W = 512
SSM_W = 512
N_POOL_G = 4
POOL_GC = 128
SSM_C = 16
SSM_P = 64
NORM_EPS = 1e-5
N_DEV = 8
W_IN_COLS = 2 * MIX // N_DEV

ADAM_LR = 0.001
ADAM_B1 = 0.9
ADAM_B2 = 0.999
ADAM_EPS = 1e-08
ADAM_WD = 0.01
ADAM_STEP = 10

SUBLANES = 8
LANES = 128
HALO = 16
STATE_ROWS = 8
STATE_COLS = 256
CHUNK_GROUPS = STATE_COLS // SSM_P
CHUNK_CH = CHUNK_GROUPS * SSM_C
T_BLK = 256
SCAN_UNROLL = 4
TM_FWD = 512
TM_BWD = 512
VMEM_LIMIT = 56 * 1024 * 1024

MESH = pl.DeviceIdType.MESH
VMEM_SPEC = pl.BlockSpec(memory_space=pltpu.VMEM)
ANY_SPEC = pl.BlockSpec(memory_space=pl.ANY)


def _mm(a, b):
    return jnp.dot(a, b, preferred_element_type=F32)


def _mm_tn(a, b):
    return lax.dot_general(a, b, (((0,), (0,)), ((), ())), preferred_element_type=F32)


def _mm_nt(a, b):
    return lax.dot_general(a, b, (((1,), (1,)), ((), ())), preferred_element_type=F32)


def _mx(a):
    return a.astype(MXU_DTYPE)


def _sigmoid(v):
    return 1.0 / (1.0 + jnp.exp(-v))


_GELU_C = math.sqrt(2.0 / math.pi)
_GELU_A = 0.044715


def _gelu_and_grad(y):
    th = jnp.tanh(_GELU_C * (y + _GELU_A * y * y * y))
    val = 0.5 * y * (1.0 + th)
    grad = 0.5 * (1.0 + th) + 0.5 * y * (1.0 - th * th) * (_GELU_C * (1.0 + 3.0 * _GELU_A * y * y))
    return val, grad


def _params(**kw):
    return pltpu.CompilerParams(vmem_limit_bytes=VMEM_LIMIT, **kw)


def _ssm_dense(a_re, a_im, log_dt, b_re, b_im, c_re, c_im):
    dt = jnp.exp(log_dt)[:, None]
    mag = jnp.exp(a_re * dt)
    ang = a_im * dt
    lb_re = mag * jnp.cos(ang)
    lb_im = mag * jnp.sin(ang)
    den = a_re * a_re + a_im * a_im
    n_re = lb_re - 1.0
    n_im = lb_im
    f_re = (n_re * a_re + n_im * a_im) / den
    f_im = (n_im * a_re - n_re * a_im) / den
    bb_re = f_re[..., None] * b_re - f_im[..., None] * b_im
    bb_im = f_re[..., None] * b_im + f_im[..., None] * b_re

    bb = jnp.stack([bb_re, bb_im], axis=0).reshape(2, STATE_ROWS, CHUNK_GROUPS, SSM_P, SSM_C)
    rb = bb.transpose(1, 4, 0, 2, 3).reshape(STATE_ROWS, SSM_C, 2 * STATE_COLS)
    cc = jnp.stack([c_re, -c_im], axis=0).reshape(2, STATE_ROWS, CHUNK_GROUPS, SSM_C, SSM_P)
    rc = cc.transpose(1, 3, 0, 2, 4).reshape(STATE_ROWS, SSM_C, 2 * STATE_COLS)
    return (lb_re.reshape(STATE_ROWS, STATE_COLS), lb_im.reshape(STATE_ROWS, STATE_COLS), rb, rc)


def _ssm_chunked(per_channel):
    row_group = jnp.arange(CHUNK_CH) // SSM_C
    col_group = (jnp.arange(2 * STATE_COLS) // SSM_P) % CHUNK_GROUPS
    own_group = (row_group[:, None] == col_group[None, :]).astype(F32)
    even = (jnp.arange(STATE_ROWS) % 2 == 0).astype(F32)[:, None, None]
    half = jnp.tile(per_channel, (1, CHUNK_GROUPS, 1)) * own_group
    return jnp.concatenate([half * even, half * (1.0 - even)], axis=1)


def _inproj_fwd(x2, g_row, w_all, dep):
    n = x2.shape[0]
    tm = TM_FWD

    def body(x_ref, g_ref, w_ref, dep_ref, z_ref, h_ref):
        x = x_ref[...]
        r = lax.rsqrt(jnp.mean(x * x, axis=-1, keepdims=True) + NORM_EPS)
        h = _mx(x * r * g_ref[...])
        h_ref[...] = h
        for d in range(N_DEV):
            z_ref[:, d * W_IN_COLS:(d + 1) * W_IN_COLS] = _mm(h, w_ref[d])

    return pl.pallas_call(
        body, name="inproj_fwd",
        grid=(n // tm,),
        in_specs=[pl.BlockSpec((tm, D_MODEL), lambda i: (i, 0)),
                  pl.BlockSpec((1, D_MODEL), lambda i: (0, 0)),
                  pl.BlockSpec((N_DEV, D_MODEL, W_IN_COLS), lambda i: (0, 0, 0)),
                  ANY_SPEC],
        out_specs=[pl.BlockSpec((tm, 2 * MIX), lambda i: (i, 0)),
                   pl.BlockSpec((tm, D_MODEL), lambda i: (i, 0))],
        out_shape=[jax.ShapeDtypeStruct((n, 2 * MIX), F32),
                   jax.ShapeDtypeStruct((n, D_MODEL), MXU_DTYPE)],
        compiler_params=_params(dimension_semantics=("arbitrary",)),
    )(x2, g_row, w_all, dep)


def _loss_head(x2, tgt2, g_row):
    n = x2.shape[0]
    tm = TM_FWD

    def body(x_ref, t_ref, g_ref, dx_ref, loss_ref, dg_ref):
        @pl.when(pl.program_id(0) == 0)
        def _():
            loss_ref[...] = jnp.zeros_like(loss_ref)
            dg_ref[...] = jnp.zeros_like(dg_ref)

        x = x_ref[...]
        g = g_ref[...]
        r = lax.rsqrt(jnp.mean(x * x, axis=-1, keepdims=True) + NORM_EPS)
        xh = x * r
        e = xh * g - t_ref[...]
        loss_ref[...] += jnp.sum(jnp.sum(e * e, axis=-1, keepdims=True), axis=0, keepdims=True) * (0.5 / D_MODEL)
        dout = e * (1.0 / D_MODEL)
        dg_ref[...] += jnp.sum(dout * xh, axis=0, keepdims=True)
        gdy = dout * g
        dx_ref[...] = r * (gdy - xh * jnp.mean(xh * gdy, axis=-1, keepdims=True))

    return pl.pallas_call(
        body, name="loss_head",
        grid=(n // tm,),
        in_specs=[pl.BlockSpec((tm, D_MODEL), lambda i: (i, 0)),
                  pl.BlockSpec((tm, D_MODEL), lambda i: (i, 0)),
                  pl.BlockSpec((1, D_MODEL), lambda i: (0, 0))],
        out_specs=[pl.BlockSpec((tm, D_MODEL), lambda i: (i, 0)),
                   pl.BlockSpec((1, 1), lambda i: (0, 0)),
                   pl.BlockSpec((1, D_MODEL), lambda i: (0, 0))],
        out_shape=[jax.ShapeDtypeStruct((n, D_MODEL), F32),
                   jax.ShapeDtypeStruct((1, 1), F32),
                   jax.ShapeDtypeStruct((1, D_MODEL), F32)],
        compiler_params=_params(dimension_semantics=("arbitrary",)),
    )(x2, tgt2, g_row)


def _outproj_bwd(dx2, yg, w_out, dep):
    n = dx2.shape[0]
    tm = TM_BWD
    n_steps = n // tm

    def body(dx_ref, y_ref, w_ref, dep_ref, dy_ref, dw_ref, acc_ref):
        i = pl.program_id(0)

        @pl.when(i == 0)
        def _():
            acc_ref[...] = jnp.zeros_like(acc_ref)

        dxb = _mx(dx_ref[...])
        dy_ref[...] = _mm_nt(dxb, w_ref[...])
        acc_ref[...] += _mm_tn(y_ref[...], dxb)

        @pl.when(i == n_steps - 1)
        def _():
            dw_ref[...] = _mx(acc_ref[...])

    return pl.pallas_call(
        body, name="outproj_bwd",
        grid=(n_steps,),
        in_specs=[pl.BlockSpec((tm, D_MODEL), lambda i: (i, 0)),
                  pl.BlockSpec((tm, MIX), lambda i: (i, 0)),
                  pl.BlockSpec((MIX, D_MODEL), lambda i: (0, 0)),
                  ANY_SPEC],
        out_specs=[pl.BlockSpec((tm, MIX), lambda i: (i, 0)),
                   pl.BlockSpec((MIX, D_MODEL), lambda i: (0, 0))],
        out_shape=[jax.ShapeDtypeStruct((n, MIX), F32),
                   jax.ShapeDtypeStruct((MIX, D_MODEL), MXU_DTYPE)],
        scratch_shapes=[pltpu.VMEM((MIX, D_MODEL), F32)],
        compiler_params=_params(dimension_semantics=("arbitrary",)),
    )(dx2, yg, w_out, dep)


def _inproj_bwd(dz, h, x2, dx_in, g_row, w_all, dep):
    n = x2.shape[0]
    tm = TM_BWD
    n_steps = n // tm

    def body(dz_ref, h_ref, x_ref, dxi_ref, g_ref, w_ref, dep_ref, dxo_ref, dw_ref, dg_ref, acc_ref, wcat_ref):
        i = pl.program_id(0)

        @pl.when(i == 0)
        def _():
            acc_ref[...] = jnp.zeros_like(acc_ref)
            dg_ref[...] = jnp.zeros_like(dg_ref)
            for d in range(N_DEV):
                wcat_ref[:, d * W_IN_COLS:(d + 1) * W_IN_COLS] = w_ref[d]

        hb = h_ref[...]
        for d in range(N_DEV):
            acc_ref[d] += _mm_tn(hb, dz_ref[:, d * W_IN_COLS:(d + 1) * W_IN_COLS])
        dh = _mm_nt(dz_ref[...], wcat_ref[...])
        x = x_ref[...]
        r = lax.rsqrt(jnp.mean(x * x, axis=-1, keepdims=True) + NORM_EPS)
        xh = x * r
        dg_ref[...] += jnp.sum(dh * xh, axis=0, keepdims=True)
        gdy = dh * g_ref[...]
        dxo_ref[...] = dxi_ref[...] + r * (gdy - xh * jnp.mean(xh * gdy, axis=-1, keepdims=True))

        @pl.when(i == n_steps - 1)
        def _():
            dw_ref[...] = _mx(acc_ref[...])

    return pl.pallas_call(
        body, name="inproj_bwd",
        grid=(n_steps,),
        in_specs=[pl.BlockSpec((tm, 2 * MIX), lambda i: (i, 0)),
                  pl.BlockSpec((tm, D_MODEL), lambda i: (i, 0)),
                  pl.BlockSpec((tm, D_MODEL), lambda i: (i, 0)),
                  pl.BlockSpec((tm, D_MODEL), lambda i: (i, 0)),
                  pl.BlockSpec((1, D_MODEL), lambda i: (0, 0)),
                  pl.BlockSpec((N_DEV, D_MODEL, W_IN_COLS), lambda i: (0, 0, 0)),
                  ANY_SPEC],
        out_specs=[pl.BlockSpec((tm, D_MODEL), lambda i: (i, 0)),
                   pl.BlockSpec((N_DEV, D_MODEL, W_IN_COLS), lambda i: (0, 0, 0)),
                   pl.BlockSpec((1, D_MODEL), lambda i: (0, 0))],
        out_shape=[jax.ShapeDtypeStruct((n, D_MODEL), F32),
                   jax.ShapeDtypeStruct((N_DEV, D_MODEL, W_IN_COLS), MXU_DTYPE),
                   jax.ShapeDtypeStruct((1, D_MODEL), F32)],
        scratch_shapes=[pltpu.VMEM((N_DEV, D_MODEL, W_IN_COLS), F32),
                        pltpu.VMEM((D_MODEL, 2 * MIX), MXU_DTYPE)],
        compiler_params=_params(dimension_semantics=("arbitrary",)),
    )(dz, h, x2, dx_in, g_row, w_all, dep)


def _row_pos(t0, rows):
    return t0 + lax.broadcasted_iota(jnp.int32, (rows, LANES), 0)


def _pool_window_mean(upad, g, t0, t_blk):
    k = 2 << g
    w = upad
    sh = 1
    while sh < k:
        w = w + pltpu.roll(w, sh, 0)
        sh *= 2
    count = jnp.minimum(_row_pos(t0, t_blk) + 1, k).astype(F32)
    return w[HALO:] / count - upad[HALO:]


def _pool_window_bwd(qpad, g, t_blk):
    k = 2 << g
    n = t_blk + HALO
    w = qpad
    sh = 1
    while sh < k:
        w = w + pltpu.roll(w, n - sh, 0)
        sh *= 2
    return w[:t_blk]


class _StateBuf:
    def __init__(self, refs, t_blk):
        self.refs = refs
        self.t_blk = t_blk

    def put_chunk(self, b, j, val):
        for c in range(4):
            self.refs[4 * b + c][pl.ds(j, self.t_blk, stride=STATE_ROWS), :] = val[:, c * LANES:(c + 1) * LANES]

    def get_chunk(self, b, j):
        return jnp.concatenate(
            [self.refs[4 * b + c][pl.ds(j, self.t_blk, stride=STATE_ROWS), :] for c in range(4)], axis=-1)

    def load(self, b, r, part):
        return jnp.concatenate(
            [self.refs[4 * b + 2 * part + h][pl.ds(r, STATE_ROWS), :] for h in range(2)], axis=-1)

    def store(self, b, r, part, val):
        for h in range(2):
            self.refs[4 * b + 2 * part + h][pl.ds(r, STATE_ROWS), :] = val[:, h * LANES:(h + 1) * LANES]


def _state_scratch(nb, t_blk):
    return [pltpu.VMEM((t_blk * STATE_ROWS, LANES), F32) for _ in range(4 * nb)]


def _ssm_project_in(u_ssm, wb_ref, buf, nb):
    t_blk = u_ssm.shape[0] // nb
    ub = _mx(u_ssm)
    for j in range(STATE_ROWS):
        m = j // 2
        bu = _mm(ub[:, m * LANES:(m + 1) * LANES], wb_ref[j])
        for b in range(nb):
            buf.put_chunk(b, j, bu[b * t_blk:(b + 1) * t_blk])


def _scan_forward(buf, lbr, lbi, init, nb):
    def step(t, carry):
        r = pl.multiple_of(t * STATE_ROWS, STATE_ROWS)
        out = []
        for b in range(nb):
            sr, si = carry[2 * b], carry[2 * b + 1]
            nr = lbr * sr - lbi * si + buf.load(b, r, 0)
            ni = lbr * si + lbi * sr + buf.load(b, r, 1)
            buf.store(b, r, 0, nr)
            buf.store(b, r, 1, ni)
            out += [nr, ni]
        return tuple(out)

    def body(i, carry):
        for u in range(SCAN_UNROLL):
            carry = step(i * SCAN_UNROLL + u, carry)
        return carry

    return lax.fori_loop(0, buf.t_blk // SCAN_UNROLL, body, init)


def _ssm_project_out(chunk, wc_ref):
    tiles = []
    for m in range(4):
        acc = None
        for j in (2 * m, 2 * m + 1):
            part = _mm_nt(chunk(j), wc_ref[j])
            acc = part if acc is None else acc + part
        tiles.append(acc)
    return jnp.concatenate(tiles, axis=-1)


def _layer_fwd(x3, z3, g_row, w_in, pool_w, pool_scale, lbr, lbi, wb, wc, d_skip, glu_w, glu_b, w_out, dep):
    nb, seq, _ = x3.shape
    t_blk = min(T_BLK, seq)
    n_t = seq // t_blk
    halo_per_blk = t_blk // HALO
    rows = nb * t_blk
    fused = z3 is None

    def body(*refs):
        if fused:
            (x_ref, g_ref, wi_ref, pw_ref, ps_ref, lbr_ref, lbi_ref, wb_ref, wc_ref, dsk_ref, gw_ref, gb_ref, wo_ref,
             dep_ref, z_ref, h_ref, yg_ref, sc_ref, act_ref, dact_ref, pooled_ref, ypre_ref, xo_ref,
             carry_ref, halo_ref, *s_refs) = refs
        else:
            (x_ref, z_ref, zh_ref, pw_ref, ps_ref, lbr_ref, lbi_ref, wb_ref, wc_ref, dsk_ref, gw_ref, gb_ref, wo_ref,
             dep_ref, yg_ref, sc_ref, act_ref, dact_ref, pooled_ref, ypre_ref, xo_ref, carry_ref, *s_refs) = refs
        i = pl.program_id(0)
        t0 = i * t_blk
        buf = _StateBuf(s_refs, t_blk)
        both = lambda lo, hi: z_ref[:, :, lo:hi].reshape(rows, hi - lo)

        @pl.when(i == 0)
        def _():
            carry_ref[...] = jnp.zeros_like(carry_ref)
            if fused:
                halo_ref[...] = jnp.zeros_like(halo_ref)

        x = x_ref[...].reshape(rows, D_MODEL)
        if fused:
            r = lax.rsqrt(jnp.mean(x * x, axis=-1, keepdims=True) + NORM_EPS)
            h = _mx(x * r * g_ref[...])
            h_ref[...] = h.reshape(nb, t_blk, D_MODEL)
            for d in range(N_DEV):
                z_ref[:, :, d * W_IN_COLS:(d + 1) * W_IN_COLS] = _mm(h, wi_ref[d]).reshape(nb, t_blk, W_IN_COLS)

        u_ssm = both(POOL_W, MIX)
        _ssm_project_in(u_ssm, wb_ref, buf, nb)
        init = tuple(carry_ref[b, :, h * STATE_COLS:(h + 1) * STATE_COLS] for b in range(nb) for h in range(2))
        fin = _scan_forward(buf, lbr_ref[...], lbi_ref[...], init, nb)
        for b in range(nb):
            carry_ref[b, :, 0:STATE_COLS] = fin[2 * b]
            carry_ref[b, :, STATE_COLS:2 * STATE_COLS] = fin[2 * b + 1]

        def chunk(j):
            states = _mx(jnp.concatenate([buf.get_chunk(b, j) for b in range(nb)], axis=0))
            sc_ref[:, j] = states.reshape(nb, t_blk, 2 * STATE_COLS)
            return states

        y = _ssm_project_out(chunk, wc_ref) + dsk_ref[...] * u_ssm
        yg, dgelu = _gelu_and_grad(y)
        ygb = _mx(yg)
        act_ref[...] = ygb.reshape(nb, t_blk, SSM_W)
        dact_ref[...] = _mx(dgelu).reshape(nb, t_blk, SSM_W)
        o_ssm = yg * _sigmoid(_mm(ygb, gw_ref[...]) + gb_ref[...])
        gp = both(MIX + POOL_W, 2 * MIX)
        parts = []
        first = (i == 0)
        for g in range(N_POOL_G):
            cols = slice(g * POOL_GC, (g + 1) * POOL_GC)
            pooled = []
            for b in range(nb):
                halo = halo_ref[b, :, cols] if fused else jnp.where(first, 0.0, zh_ref[b, :, cols])
                pooled.append(_pool_window_mean(jnp.concatenate([halo, z_ref[b, :, cols]], axis=0), g, t0, t_blk))
            pb = _mx(jnp.concatenate(pooled, axis=0))
            ypre = _mm(pb, pw_ref[g])
            pooled_ref[:, :, cols] = pb.reshape(nb, t_blk, POOL_GC)
            ypre_ref[:, :, cols] = ypre.reshape(nb, t_blk, POOL_GC)
            gpp = both(MIX + g * POOL_GC, MIX + (g + 1) * POOL_GC)
            parts.append(_mx(ypre * ps_ref[:, cols] * (gpp * _sigmoid(gpp))))
        parts.append(_mx(o_ssm * (gp * _sigmoid(gp))))
        gated = jnp.concatenate(parts, axis=-1)
        yg_ref[...] = gated.reshape(nb, t_blk, MIX)
        xo_ref[...] = (x + _mm(gated, wo_ref[...])).reshape(nb, t_blk, D_MODEL)
        if fused:
            halo_ref[...] = z_ref[:, t_blk - HALO:, 0:POOL_W]

    const = lambda *shape: pl.BlockSpec(shape, lambda i: (0,) * len(shape))
    tokens = lambda width: pl.BlockSpec((nb, t_blk, width), lambda i: (0, i, 0))
    mixer_specs = [const(N_POOL_G, POOL_GC, POOL_GC), const(1, POOL_W),
                   const(STATE_ROWS, STATE_COLS), const(STATE_ROWS, STATE_COLS),
                   const(STATE_ROWS, LANES, 2 * STATE_COLS), const(STATE_ROWS, LANES, 2 * STATE_COLS),
                   const(1, SSM_W), const(SSM_W, SSM_W), const(1, SSM_W), const(MIX, D_MODEL), ANY_SPEC]
    mixer_args = (pool_w, pool_scale, lbr, lbi, wb, wc, d_skip, glu_w, glu_b, w_out, dep)
    out_specs = [tokens(MIX), pl.BlockSpec((nb, STATE_ROWS, t_blk, 2 * STATE_COLS), lambda i: (0, 0, i, 0)),
                 tokens(SSM_W), tokens(SSM_W), tokens(POOL_W), tokens(POOL_W), tokens(D_MODEL)]
    out_shape = [jax.ShapeDtypeStruct((nb, seq, MIX), MXU_DTYPE),
                 jax.ShapeDtypeStruct((nb, STATE_ROWS, seq, 2 * STATE_COLS), MXU_DTYPE),
                 jax.ShapeDtypeStruct((nb, seq, SSM_W), MXU_DTYPE),
                 jax.ShapeDtypeStruct((nb, seq, SSM_W), MXU_DTYPE),
                 jax.ShapeDtypeStruct((nb, seq, POOL_W), MXU_DTYPE),
                 jax.ShapeDtypeStruct((nb, seq, POOL_W), F32),
                 jax.ShapeDtypeStruct((nb, seq, D_MODEL), F32)]
    scratch = [pltpu.VMEM((nb, STATE_ROWS, 2 * STATE_COLS), F32)]
    if fused:
        in_specs = [tokens(D_MODEL), const(1, D_MODEL), const(N_DEV, D_MODEL, W_IN_COLS)] + mixer_specs
        args = (x3, g_row, w_in) + mixer_args
        out_specs = [tokens(2 * MIX), tokens(D_MODEL)] + out_specs
        out_shape = [jax.ShapeDtypeStruct((nb, seq, 2 * MIX), F32),
                     jax.ShapeDtypeStruct((nb, seq, D_MODEL), MXU_DTYPE)] + out_shape
        scratch = scratch + [pltpu.VMEM((nb, HALO, POOL_W), F32)]
    else:
        in_specs = [tokens(D_MODEL), tokens(2 * MIX),
                    pl.BlockSpec((nb, HALO, POOL_W), lambda i: (0, jnp.maximum(i * halo_per_blk - 1, 0), 0))] + mixer_specs
        args = (x3, z3, z3) + mixer_args
    return pl.pallas_call(
        body, name="layer_fwd" if fused else "mixer_fwd",
        grid=(n_t,),
        in_specs=in_specs, out_specs=out_specs, out_shape=out_shape,
        scratch_shapes=scratch + _state_scratch(nb, t_blk),
        compiler_params=_params(dimension_semantics=("arbitrary",)),
    )(*args)


def _mixer_bwd(z3, dy3, states, kept, pool_w, pool_scale, lbr, lbi, wb, wc, d_skip, glu_w, glu_b):
    nb, seq, _ = z3.shape
    t_blk = min(T_BLK, seq)
    n_t = seq // t_blk
    halo_per_blk = t_blk // HALO
    rows = nb * t_blk

    def body(z_ref, dy_ref, sc_ref, sch_ref, act_ref, dact_ref, pooled_ref, ypre_ref, pw_ref, ps_ref, lbr_ref, lbi_ref, wb_ref, wc_ref, dsk_ref,
             gw_ref, gb_ref,
             dz_ref, dpw_ref, dps_ref, dlbr_ref, dlbi_ref, dwb_ref, dwc_ref, ddsk_ref, dgw_ref, dgb_ref,
             gcarry_ref, qcarry_ref, du_ref, dgw_acc, *g_refs):
        i = pl.program_id(0)
        blk = n_t - 1 - i
        t0 = blk * t_blk
        gbuf = _StateBuf(g_refs, t_blk)

        @pl.when(i == 0)
        def _():
            gcarry_ref[...] = jnp.zeros_like(gcarry_ref)
            qcarry_ref[...] = jnp.zeros_like(qcarry_ref)
            for ref in (dpw_ref, dps_ref, dlbr_ref, dlbi_ref, dwb_ref, dwc_ref, ddsk_ref, dgw_acc, dgb_ref):
                ref[...] = jnp.zeros_like(ref)

        lbr_v = lbr_ref[...]
        lbi_v = lbi_ref[...]

        both = lambda ref, lo, hi: ref[:, :, lo:hi].reshape(rows, hi - lo)
        split = lambda val: val.reshape(nb, t_blk, val.shape[-1])
        states = lambda j: sc_ref[:, j].reshape(rows, 2 * STATE_COLS)
        first = (blk == 0)

        u_ssm = both(z_ref, POOL_W, MIX)
        ygb = act_ref[...].reshape(rows, SSM_W)
        yg = ygb.astype(F32)
        dgelu = dact_ref[...].reshape(rows, SSM_W).astype(F32)
        sg = _sigmoid(_mm(ygb, gw_ref[...]) + gb_ref[...])
        o_ssm = yg * sg
        gp = both(z_ref, MIX + POOL_W, 2 * MIX)
        sgm = _sigmoid(gp)
        dyv = both(dy_ref, POOL_W, MIX)
        dz_ref[:, :, MIX + POOL_W:2 * MIX] = split(_mx(dyv * o_ssm * (sgm * (1.0 + gp * (1.0 - sgm)))))
        do = dyv * (gp * sgm)
        dv = do * yg * (sg * (1.0 - sg))
        dvb = _mx(dv)
        dgb_ref[...] += jnp.sum(dv, axis=0, keepdims=True)
        dgw_acc[...] += _mm_tn(ygb, dvb)
        dyp = (do * sg + _mm_nt(dvb, gw_ref[...])) * dgelu
        ddsk_ref[...] += jnp.sum(dyp * u_ssm, axis=0, keepdims=True)
        dypb = _mx(dyp)
        for j in range(STATE_ROWS):
            m = j // 2
            dyt = dypb[:, m * LANES:(m + 1) * LANES]
            ds = _mm(dyt, wc_ref[j])
            for b in range(nb):
                gbuf.put_chunk(b, j, ds[b * t_blk:(b + 1) * t_blk])
            dwc_ref[j] += _mm_tn(dyt, states(j))
        du_ref[...] = split(dsk_ref[...] * dyp)

        for g in range(N_POOL_G):
            cols = slice(g * POOL_GC, (g + 1) * POOL_GC)
            pb = both(pooled_ref, g * POOL_GC, (g + 1) * POOL_GC)
            ypre = both(ypre_ref, g * POOL_GC, (g + 1) * POOL_GC)
            gpp = both(z_ref, MIX + g * POOL_GC, MIX + (g + 1) * POOL_GC)
            sgp = _sigmoid(gpp)
            dyg = both(dy_ref, g * POOL_GC, (g + 1) * POOL_GC)
            scale = ps_ref[:, cols]
            dz_ref[:, :, MIX + g * POOL_GC:MIX + (g + 1) * POOL_GC] = split(_mx(
                dyg * (ypre * scale) * (sgp * (1.0 + gpp * (1.0 - sgp)))))
            dyc = dyg * (gpp * sgp)
            dps_ref[:, cols] += jnp.sum(dyc * ypre, axis=0, keepdims=True)
            dypre = _mx(dyc * scale)
            dpw_ref[g] += _mm_tn(pb, dypre)
            dpooled = _mm_nt(dypre, pw_ref[g])
            count = jnp.minimum(_row_pos(t0, t_blk) + 1, 2 << g).astype(F32)
            for b in range(nb):
                dp = dpooled[b * t_blk:(b + 1) * t_blk]
                q = dp / count
                qpad = jnp.concatenate([q, qcarry_ref[b, :, cols]], axis=0)
                qcarry_ref[b, :, cols] = q[:HALO]
                dz_ref[b, :, cols] = _mx(_pool_window_bwd(qpad, g, t_blk) - dp)

        def rev_step(t, carry):
            r = pl.multiple_of(t * STATE_ROWS, STATE_ROWS)
            out = []
            for b in range(nb):
                gr, gi = carry[2 * b], carry[2 * b + 1]
                ngr = lbr_v * gr + lbi_v * gi + gbuf.load(b, r, 0)
                ngi = lbr_v * gi - lbi_v * gr + gbuf.load(b, r, 1)
                gbuf.store(b, r, 0, ngr)
                gbuf.store(b, r, 1, ngi)
                out += [ngr, ngi]
            return tuple(out)

        def rev_body(i, carry):
            for u in range(SCAN_UNROLL):
                carry = rev_step(t_blk - 1 - (i * SCAN_UNROLL + u), carry)
            return carry

        init_g = tuple(gcarry_ref[b, :, h * STATE_COLS:(h + 1) * STATE_COLS] for b in range(nb) for h in range(2))
        fin = lax.fori_loop(0, t_blk // SCAN_UNROLL, rev_body, init_g)
        for b in range(nb):
            gcarry_ref[b, :, 0:STATE_COLS] = fin[2 * b]
            gcarry_ref[b, :, STATE_COLS:2 * STATE_COLS] = fin[2 * b + 1]

        ub = _mx(u_ssm)
        for m in range(4):
            acc = both(du_ref, m * LANES, (m + 1) * LANES)
            for j in (2 * m, 2 * m + 1):
                g = jnp.concatenate([gbuf.get_chunk(b, j) for b in range(nb)], axis=0)
                gj = _mx(g)
                acc = acc + _mm_nt(gj, wb_ref[j])
                dwb_ref[j] += _mm_tn(ub[:, m * LANES:(m + 1) * LANES], gj)
                shifted = []
                for b in range(nb):
                    before = jnp.where(first, 0.0, sch_ref[b, j].astype(F32))
                    spad = jnp.concatenate([before, sc_ref[b, j].astype(F32)], axis=0)
                    shifted.append(pltpu.roll(spad, 1, 0)[HALO:])
                s_prev = jnp.concatenate(shifted, axis=0)
                g_re, g_im = g[:, :STATE_COLS], g[:, STATE_COLS:]
                p_re, p_im = s_prev[:, :STATE_COLS], s_prev[:, STATE_COLS:]
                dlbr_ref[j:j + 1, :] += jnp.sum(g_re * p_re + g_im * p_im, axis=0, keepdims=True)
                dlbi_ref[j:j + 1, :] += jnp.sum(g_im * p_re - g_re * p_im, axis=0, keepdims=True)
            dz_ref[:, :, POOL_W + m * LANES:POOL_W + (m + 1) * LANES] = split(_mx(acc))

        @pl.when(i == n_t - 1)
        def _():
            dgw_ref[...] = _mx(dgw_acc[...])

    const = lambda *shape: pl.BlockSpec(shape, lambda i: (0,) * len(shape))
    rev = lambda i: n_t - 1 - i
    out_shape = [jax.ShapeDtypeStruct((nb, seq, 2 * MIX), MXU_DTYPE),
                 jax.ShapeDtypeStruct((N_POOL_G, POOL_GC, POOL_GC), F32),
                 jax.ShapeDtypeStruct((1, POOL_W), F32),
                 jax.ShapeDtypeStruct((STATE_ROWS, STATE_COLS), F32),
                 jax.ShapeDtypeStruct((STATE_ROWS, STATE_COLS), F32),
                 jax.ShapeDtypeStruct((STATE_ROWS, LANES, 2 * STATE_COLS), F32),
                 jax.ShapeDtypeStruct((STATE_ROWS, LANES, 2 * STATE_COLS), F32),
                 jax.ShapeDtypeStruct((1, SSM_W), F32),
                 jax.ShapeDtypeStruct((SSM_W, SSM_W), MXU_DTYPE),
                 jax.ShapeDtypeStruct((1, SSM_W), F32)]
    return pl.pallas_call(
        body, name="mixer_bwd",
        grid=(n_t,),
        in_specs=[pl.BlockSpec((nb, t_blk, 2 * MIX), lambda i: (0, rev(i), 0)),
                  pl.BlockSpec((nb, t_blk, MIX), lambda i: (0, rev(i), 0)),
                  pl.BlockSpec((nb, STATE_ROWS, t_blk, 2 * STATE_COLS), lambda i: (0, 0, rev(i), 0)),
                  pl.BlockSpec((nb, STATE_ROWS, HALO, 2 * STATE_COLS),
                               lambda i: (0, 0, jnp.maximum(rev(i) * halo_per_blk - 1, 0), 0)),
                  pl.BlockSpec((nb, t_blk, SSM_W), lambda i: (0, rev(i), 0)),
                  pl.BlockSpec((nb, t_blk, SSM_W), lambda i: (0, rev(i), 0)),
                  pl.BlockSpec((nb, t_blk, POOL_W), lambda i: (0, rev(i), 0)),
                  pl.BlockSpec((nb, t_blk, POOL_W), lambda i: (0, rev(i), 0)),
                  const(N_POOL_G, POOL_GC, POOL_GC), const(1, POOL_W),
                  const(STATE_ROWS, STATE_COLS), const(STATE_ROWS, STATE_COLS),
                  const(STATE_ROWS, LANES, 2 * STATE_COLS), const(STATE_ROWS, LANES, 2 * STATE_COLS),
                  const(1, SSM_W), const(SSM_W, SSM_W), const(1, SSM_W)],
        out_specs=[pl.BlockSpec((nb, t_blk, 2 * MIX), lambda i: (0, rev(i), 0))]
                  + [const(*s.shape) for s in out_shape[1:]],
        out_shape=out_shape,
        scratch_shapes=[pltpu.VMEM((nb, STATE_ROWS, 2 * STATE_COLS), F32),
                        pltpu.VMEM((nb, HALO, POOL_W), F32),
                        pltpu.VMEM((nb, t_blk, SSM_W), F32),
                        pltpu.VMEM((SSM_W, SSM_W), F32)]
                       + _state_scratch(nb, t_blk),
        compiler_params=_params(dimension_semantics=("arbitrary",)),
    )(z3, dy3, states, states, *kept, pool_w, pool_scale, lbr, lbi, wb, wc, d_skip, glu_w, glu_b)


def _mesh_place():
    x, y, c = lax.axis_index("x"), lax.axis_index("y"), lax.axis_index("c")
    return x, y, c


def _flip(place, k):
    x, y, c = place
    return (1 - x if k & 4 else x, 1 - y if k & 2 else y, 1 - c if k & 1 else c)


def _index(place):
    x, y, c = place
    return 4 * x + 2 * y + c


HBM_SPEC = pl.BlockSpec(memory_space=pltpu.HBM)
SEM_SPEC = pl.BlockSpec(memory_space=pltpu.SEMAPHORE)
_EFFECT = pltpu.SideEffectType.DATAFLOW_SIDE_EFFECTING
N_PEERS = N_DEV - 1


def _exchange_copies(src_refs, land_refs, send_sems, recv_sems):
    me = _mesh_place()
    mine = _index(me)
    out = []
    for a, land_ref in enumerate(land_refs):
        for k in range(1, N_DEV):
            peer = _flip(me, k)
            theirs = _index(peer)
            n = a * N_PEERS + k - 1
            src = src_refs[a].at[theirs] if src_refs else land_ref.at[mine]
            send = pltpu.make_async_remote_copy(
                src_ref=src, dst_ref=land_ref.at[mine], send_sem=send_sems.at[n], recv_sem=recv_sems.at[n],
                device_id=peer, device_id_type=MESH)
            recv = pltpu.make_async_remote_copy(
                src_ref=src, dst_ref=land_ref.at[theirs], send_sem=send_sems.at[n], recv_sem=recv_sems.at[n],
                device_id=peer, device_id_type=MESH)
            out.append((send, recv))
    return out


def _exchange_start(srcs, lands, after, name):
    arrays = tuple(srcs) + tuple(lands)
    n_src, n_all = len(srcs), len(arrays)
    n_copies = len(lands) * N_PEERS

    def body(*refs):
        send_sems, recv_sems = refs[n_all + 1], refs[n_all + 2]
        token = refs[-1]
        for send, _ in _exchange_copies(refs[:n_src], refs[n_src:n_all], send_sems, recv_sems):
            send.start()
        token[...] = jnp.zeros_like(token)

    res = pl.pallas_call(
        body, name=name,
        in_specs=[HBM_SPEC] * n_all + [ANY_SPEC],
        out_specs=[SEM_SPEC, SEM_SPEC] + [HBM_SPEC] * n_all + [VMEM_SPEC],
        out_shape=[pltpu.SemaphoreType.DMA((n_copies,)), pltpu.SemaphoreType.DMA((n_copies,))]
                  + [pltpu.HBM(a.shape, a.dtype) for a in arrays] + [jax.ShapeDtypeStruct((SUBLANES, LANES), F32)],
        input_output_aliases={i: 2 + i for i in range(n_all)},
        compiler_params=pltpu.CompilerParams(has_side_effects=_EFFECT),
    )(*[pltpu.with_memory_space_constraint(a, pltpu.HBM) for a in arrays], after)
    return tuple(res[:-1]), res[-1]


def _exchange_wait(handle, n_lands, after, name):
    send_sems, recv_sems = handle[0], handle[1]
    arrays = handle[2:]
    n_all = len(arrays)
    n_src = n_all - n_lands

    def body(*refs):
        for send, recv in _exchange_copies(refs[:n_src], refs[n_src:n_all], refs[n_all], refs[n_all + 1]):
            send.wait_send()
            recv.wait_recv()

    res = pl.pallas_call(
        body, name=name,
        in_specs=[HBM_SPEC] * n_all + [SEM_SPEC, SEM_SPEC, ANY_SPEC],
        out_specs=[HBM_SPEC] * n_all,
        out_shape=[pltpu.HBM(a.shape, a.dtype) for a in arrays],
        input_output_aliases={i: i for i in range(n_all)},
        compiler_params=pltpu.CompilerParams(has_side_effects=_EFFECT),
    )(*arrays, send_sems, recv_sems, after)
    return tuple(res[:n_src]), tuple(res[n_src:])


def _weight_zones(w_in, glu_w, w_out, my_idx):
    shards = (w_in, glu_w, w_out)
    depth = w_in.shape[0]

    def body(idx_ref, *refs):
        ins, zones = refs[:len(shards)], refs[len(shards):]
        for l in range(depth):
            for a, src in enumerate(ins):
                zones[l * len(shards) + a][0] = _mx(src[l])

    whole = lambda s: pl.BlockSpec(s.shape, lambda i, idx: (0,) * s.ndim)
    return pl.pallas_call(
        body, name="weight_zones",
        grid_spec=pltpu.PrefetchScalarGridSpec(
            num_scalar_prefetch=1, grid=(1,),
            in_specs=[whole(s) for s in shards],
            out_specs=[pl.BlockSpec((1,) + s.shape[1:], lambda i, idx: (idx[0], 0, 0))
                       for _ in range(depth) for s in shards]),
        out_shape=[jax.ShapeDtypeStruct((N_DEV,) + s.shape[1:], MXU_DTYPE) for _ in range(depth) for s in shards],
        compiler_params=_params(dimension_semantics=("arbitrary",)),
    )(my_idx.reshape(1).astype(jnp.int32), *shards)


def _allreduce_packed(p):
    rows = p.shape[0]
    half = rows // 2
    quarter = half // 4

    def body(p_ref, o_ref, part_ref, sib_ref, got_ref, send_sems, recv_sems):
        x, y, c = _mesh_place()
        sibling = (x, y, 1 - c)
        chip = 2 * x + y
        chips = [(k, (1 - x if k & 2 else x, 1 - y if k & 1 else y, c), chip ^ k) for k in (1, 2, 3)]
        my_half = pl.multiple_of(c * half, SUBLANES)
        other_half = pl.multiple_of((1 - c) * half, SUBLANES)

        def copy(n, src, dst, to):
            return pltpu.make_async_remote_copy(src_ref=src, dst_ref=dst, send_sem=send_sems.at[n],
                                                recv_sem=recv_sems.at[n], device_id=to, device_id_type=MESH)

        def quarter_of(ref, base, q):
            return ref.at[pl.ds(pl.multiple_of(base + q * quarter, SUBLANES), quarter)]

        swap = copy(0, p_ref.at[pl.ds(other_half, half)], sib_ref, sibling)
        swap.start()
        swap.wait()
        part_ref[...] = p_ref[pl.ds(my_half, half), :] + sib_ref[...]

        scatter = [copy(k, quarter_of(part_ref, 0, q), got_ref.at[k - 1], to) for k, to, q in chips]
        for cp in scatter:
            cp.start()
        total = part_ref[pl.ds(pl.multiple_of(chip * quarter, SUBLANES), quarter), :]
        for cp, (k, _, _) in zip(scatter, chips):
            cp.wait()
            total = total + got_ref[k - 1]
        mine = pl.multiple_of(my_half + chip * quarter, SUBLANES)
        o_ref[pl.ds(mine, quarter), :] = total

        gather = [copy(3 + k, o_ref.at[pl.ds(mine, quarter)], o_ref.at[pl.ds(mine, quarter)], to) for k, to, _ in chips]
        for cp in gather:
            cp.start()
        for k, to, q in chips:
            theirs = quarter_of(o_ref, my_half, q)
            copy(3 + k, theirs, theirs, to).wait_recv()
        for cp in gather:
            cp.wait_send()

        back = copy(7, o_ref.at[pl.ds(my_half, half)], o_ref.at[pl.ds(my_half, half)], sibling)
        back.start()
        copy(7, o_ref.at[pl.ds(other_half, half)], o_ref.at[pl.ds(other_half, half)], sibling).wait_recv()
        back.wait_send()

    return pl.pallas_call(
        body, name="comm_allreduce_packed",
        in_specs=[VMEM_SPEC],
        out_specs=VMEM_SPEC,
        out_shape=jax.ShapeDtypeStruct(p.shape, F32),
        scratch_shapes=[pltpu.VMEM((half, LANES), F32),
                        pltpu.VMEM((half, LANES), F32),
                        pltpu.VMEM((3, quarter, LANES), F32),
                        pltpu.SemaphoreType.DMA((8,)),
                        pltpu.SemaphoreType.DMA((8,))],
        compiler_params=_params(),
    )(p)


def _adamw_math(w, g, m, v):
    m = ADAM_B1 * m + (1.0 - ADAM_B1) * g
    v = ADAM_B2 * v + (1.0 - ADAM_B2) * (g * g)
    m_hat = m / (1.0 - ADAM_B1 ** ADAM_STEP)
    v_hat = v / (1.0 - ADAM_B2 ** ADAM_STEP)
    delta = -ADAM_LR * (m_hat / (jnp.sqrt(v_hat) + ADAM_EPS) + ADAM_WD * w)
    return delta, m, v


def _adamw_summed(received, own, my_idx, w, m, v, name):
    depth, r, c = w.shape
    tr = min(r, 128)

    def body(idx_ref, *refs):
        r_refs, o_refs = refs[:depth], refs[depth:2 * depth]
        w_ref, m_ref, v_ref, g_ref, d_ref, nm_ref, nv_ref = refs[2 * depth:]
        me = idx_ref[0]
        for l in range(depth):
            g = jnp.zeros((tr, c), F32)
            for q in range(N_DEV):
                g = g + jnp.where(q == me, o_refs[l][0], r_refs[l][q]).astype(F32)
            g_ref[l] = g
            d_ref[l], nm_ref[l], nv_ref[l] = _adamw_math(w_ref[l], g, m_ref[l], v_ref[l])

    blk = pl.BlockSpec((depth, tr, c), lambda i, idx: (0, i, 0))
    return pl.pallas_call(
        body, name=name,
        grid_spec=pltpu.PrefetchScalarGridSpec(
            num_scalar_prefetch=1, grid=(r // tr,),
            in_specs=[pl.BlockSpec((N_DEV, tr, c), lambda i, idx: (0, i, 0))] * depth
                     + [pl.BlockSpec((1, tr, c), lambda i, idx: (idx[0], i, 0))] * depth
                     + [blk, blk, blk],
            out_specs=[blk] * 4),
        out_shape=[jax.ShapeDtypeStruct((depth, r, c), F32)] * 4,
        compiler_params=_params(dimension_semantics=("arbitrary",)),
    )(my_idx.reshape(1).astype(jnp.int32), *received, *own, w, m, v)


def _adamw_small(ws, gs, ms, vs):
    n = len(ws)
    depth = ws[0].shape[0]
    quarters = 4

    def spec(a):
        per_layer = a.shape[0] == depth
        split = a.ndim >= 3 and a.shape[1] % quarters == 0 and a.shape[1] >= quarters
        block = (1, a.shape[1] // quarters if split else a.shape[1]) + a.shape[2:]
        rest = (0,) * (a.ndim - 2)
        return pl.BlockSpec(block, lambda l, s: ((l if per_layer else 0), (s if split else 0)) + rest)

    def body(*refs):
        w_refs, g_refs, m_refs, v_refs = (refs[k * n:(k + 1) * n] for k in range(4))
        d_refs, nm_refs, nv_refs = (refs[(4 + k) * n:(5 + k) * n] for k in range(3))
        for k in range(n):
            d_refs[k][...], nm_refs[k][...], nv_refs[k][...] = _adamw_math(
                w_refs[k][...], g_refs[k][...], m_refs[k][...], v_refs[k][...])

    specs = [spec(a) for a in ws]
    shapes = [jax.ShapeDtypeStruct(a.shape, F32) for a in ws]
    res = pl.pallas_call(
        body, name="adamw_small",
        grid=(depth, quarters),
        in_specs=specs * 4,
        out_specs=specs * 3,
        out_shape=shapes * 3,
        compiler_params=_params(dimension_semantics=("arbitrary", "arbitrary")),
    )(*ws, *gs, *ms, *vs)
    return res[:n], res[n:2 * n], res[2 * n:]


_PACK_ROWS = SUBLANES * N_DEV


def _pack(arrays):
    flat = jnp.concatenate([a.reshape(-1) for a in arrays])
    per = _PACK_ROWS * LANES
    total = -(-flat.shape[0] // per) * per
    flat = jnp.pad(flat, (0, total - flat.shape[0]))
    return flat.reshape(total // LANES, LANES)


def _unpack(packed, like):
    flat = packed.reshape(-1)
    out = []
    off = 0
    for a in like:
        out.append(flat[off:off + a.size].reshape(a.shape))
        off += a.size
    return out


def kernel(x, norm_g, w_in, pool_w, pool_scale, a_re, a_im, log_dt, b_re, b_im, c_re, c_im, d_skip, glu_w, glu_b, w_out, final_g, loss_target, m_norm_g, m_w_in, m_pool_w, m_pool_scale, m_a_re, m_a_im, m_log_dt, m_b_re, m_b_im, m_c_re, m_c_im, m_d_skip, m_glu_w, m_glu_b, m_w_out, m_final_g, v_norm_g, v_w_in, v_pool_w, v_pool_scale, v_a_re, v_a_im, v_log_dt, v_b_re, v_b_im, v_c_re, v_c_im, v_d_skip, v_glu_w, v_glu_b, v_w_out, v_final_g):
    nb, seq, _ = x.shape
    n_tok = nb * seq
    depth = norm_g.shape[0]

    my_idx = _index(_mesh_place())

    zones = _weight_zones(w_in, glu_w, w_out, my_idx)

    def gather_start(l, after):
        return _exchange_start((), zones[3 * l:3 * l + 3], after, f"comm_gather_start_{l}")

    def gather_wait(handle, after, l):
        _, (win, glu, wout) = _exchange_wait(handle, 3, after, f"comm_gather_wait_{l}")
        return win, glu.reshape(SSM_W, SSM_W), wout.reshape(MIX, D_MODEL)

    xs = [x.reshape(n_tok, D_MODEL)]
    first_w_in, dep = _exchange_start((), zones[0:1], xs[0], "comm_gather_start_0_w_in")

    (lbr, lbi, rb, rc), dense_vjp = jax.vjp(jax.vmap(_ssm_dense), a_re, a_im, log_dt + dep[0, 0], b_re, b_im, c_re, c_im)
    chunk_all = jax.vmap(_ssm_chunked)
    (wb, wct), chunk_vjp = jax.vjp(lambda p, q: (chunk_all(p), chunk_all(q)), rb, rc)
    wb_m, wct_m = _mx(wb), _mx(wct)
    pool_w_m = _mx(pool_w)

    def layer_params(l):
        return (pool_w_m[l], pool_scale[l][None], lbr[l], lbi[l], wb_m[l], wct_m[l], d_skip[l][None],
                weights[l][1], glu_b[l][None])

    saved = []
    weights = []
    for l in range(depth):
        if l == 0:
            _, (win,) = _exchange_wait(first_w_in, 1, wct_m, "comm_gather_wait_0_w_in")
            rest, dep = _exchange_start((), zones[1:3], win, "comm_gather_start_0_rest")
            z, h = _inproj_fwd(xs[-1], norm_g[l][None], win, dep)
            _, (glu, wout) = _exchange_wait(rest, 2, z, "comm_gather_wait_0_rest")
            weights.append((win, glu.reshape(SSM_W, SSM_W), wout.reshape(MIX, D_MODEL)))
            handle, dep = gather_start(1, weights[0][2])
            z3 = z.reshape(nb, seq, 2 * MIX)
            yg, states, *kept, x_next = _layer_fwd(xs[-1].reshape(nb, seq, D_MODEL), z3, None, None,
                                                   *layer_params(l), weights[l][2], dep)
        else:
            weights.append(gather_wait(handle, xs[-1], l))
            if l + 1 < depth:
                handle, dep = gather_start(l + 1, weights[l][0])
            z3, h3, yg, states, *kept, x_next = _layer_fwd(xs[-1].reshape(nb, seq, D_MODEL), None, norm_g[l][None],
                                                           weights[l][0], *layer_params(l), weights[l][2], dep)
            h = h3.reshape(n_tok, D_MODEL)
        xs.append(x_next.reshape(n_tok, D_MODEL))
        saved.append((z3, h, yg.reshape(n_tok, MIX), states, kept))

    dx, loss_part, d_final_g = _loss_head(xs[-1], loss_target.reshape(n_tok, D_MODEL), final_g[None])

    small = {k: [None] * depth for k in
             ("norm_g", "pool_w", "pool_scale", "lbr", "lbi", "wb", "wct", "d_skip", "glu_b")}
    received = [None] * depth
    sent = [None] * depth
    pending = None
    early = None
    for l in reversed(range(depth)):
        z3, h, yg2, states, kept = saved[l]
        dy, d_wout = _outproj_bwd(dx, yg2, weights[l][2], dep)
        (dz, d_pw, d_ps, d_lbr, d_lbi, d_wb, d_wct, d_dsk, d_gw, d_gb) = _mixer_bwd(
            z3, dy.reshape(nb, seq, MIX), states, kept, *layer_params(l))
        rest = (d_gw.reshape(N_DEV, SSM_W // N_DEV, SSM_W), d_wout.reshape(N_DEV, MIX // N_DEV, D_MODEL))
        if l == 0:
            early, dep = _exchange_start(rest, tuple(lax.empty(s.shape, s.dtype) for s in rest), dz,
                                         "comm_grads_start_0_rest")
        dx, d_win, d_ng = _inproj_bwd(dz.reshape(n_tok, 2 * MIX), h, xs[l], dx, norm_g[l][None], weights[l][0], dep)
        for k, val in (("norm_g", d_ng[0]), ("pool_w", d_pw), ("pool_scale", d_ps[0]), ("lbr", d_lbr),
                       ("lbi", d_lbi), ("wb", d_wb), ("wct", d_wct), ("d_skip", d_dsk[0]), ("glu_b", d_gb[0])):
            small[k][l] = val
        if pending is not None:
            sent[l + 1], received[l + 1] = _exchange_wait(pending, 3, dx, f"comm_grads_wait_{l + 1}")
        srcs = (d_win,) if l == 0 else (d_win,) + rest
        lands = tuple(lax.empty(s.shape, s.dtype) for s in srcs)
        pending, dep = _exchange_start(srcs, lands, dx, f"comm_grads_start_{l}")
    stack = lambda k: jnp.stack(small[k])
    d_rb, d_rc = chunk_vjp((stack("wb"), stack("wct")))
    local = [stack("norm_g"), stack("pool_w"), stack("pool_scale"), stack("lbr"), stack("lbi"), d_rb, d_rc,
             stack("d_skip"), stack("glu_b"), d_final_g[0] + dep[0, 0], loss_part[0]]
    (g_norm_g, g_pool_w, g_pool_scale, g_lbr, g_lbi, g_rb, g_rc, g_d_skip, g_glu_b, g_final_g, loss) = _unpack(
        _allreduce_packed(_pack(local)), local)
    loss = loss[0]
    g_a_re, g_a_im, g_log_dt, g_b_re, g_b_im, g_c_re, g_c_im = dense_vjp((g_lbr, g_lbi, g_rb, g_rc))

    names = ["norm_g", "pool_w", "pool_scale", "a_re", "a_im", "log_dt", "b_re", "b_im", "c_re", "c_im",
             "d_skip", "glu_b", "final_g"]
    rows = {"norm_g", "pool_scale", "log_dt", "d_skip", "glu_b"}
    small_w = [norm_g, pool_w, pool_scale, a_re, a_im, log_dt, b_re, b_im, c_re, c_im, d_skip, glu_b, final_g]
    small_g = [g_norm_g, g_pool_w, g_pool_scale, g_a_re, g_a_im, g_log_dt, g_b_re, g_b_im, g_c_re, g_c_im,
               g_d_skip, g_glu_b, g_final_g]
    small_m = [m_norm_g, m_pool_w, m_pool_scale, m_a_re, m_a_im, m_log_dt, m_b_re, m_b_im, m_c_re, m_c_im,
               m_d_skip, m_glu_b, m_final_g]
    small_v = [v_norm_g, v_pool_w, v_pool_scale, v_a_re, v_a_im, v_log_dt, v_b_re, v_b_im, v_c_re, v_c_im,
               v_d_skip, v_glu_b, v_final_g]

    wide_last = {"b_re", "b_im"}

    def blocked(arrays):
        return [a.reshape(1, 1, -1) if n == "final_g" else a[:, None, :] if n in rows
                else a.swapaxes(2, 3) if n in wide_last else a for n, a in zip(names, arrays)]

    small_d, small_nm, small_nv = _adamw_small(blocked(small_w), blocked(small_g), blocked(small_m), blocked(small_v))
    res = {}
    for kind, arrays in (("grad", small_g), ("delta", small_d), ("m", small_nm), ("v", small_nv)):
        for n, a, like in zip(names, arrays, small_w):
            if kind != "grad" and n in wide_last:
                a = a.swapaxes(2, 3)
            res[kind, n] = a.reshape(like.shape)

    (s_win,), (r_win,) = _exchange_wait(pending, 1, small_d[0], "comm_grads_wait_0")
    (s_glu, s_wout), (r_glu, r_wout) = _exchange_wait(early, 2, small_d[0], "comm_grads_wait_0_rest")
    sent[0], received[0] = (s_win, s_glu, s_wout), (r_win, r_glu, r_wout)
    shard_res = {}
    for pos, (n, w, m, v) in enumerate((("w_in", w_in, m_w_in, v_w_in), ("glu_w", glu_w, m_glu_w, v_glu_w),
                                        ("w_out", w_out, m_w_out, v_w_out))):
        shard_res[n] = _adamw_summed([received[l][pos] for l in range(depth)], [sent[l][pos] for l in range(depth)],
                                     my_idx, w, m, v, "adamw_" + n)
    for n in ("w_in", "glu_w", "w_out"):
        for pos, kind in enumerate(("grad", "delta", "m", "v")):
            res[kind, n] = shard_res[n][pos]

    order = ["norm_g", "w_in", "pool_w", "pool_scale", "a_re", "a_im", "log_dt", "b_re", "b_im", "c_re", "c_im",
             "d_skip", "glu_w", "glu_b", "w_out", "final_g"]
    outs = [loss, dx.reshape(nb, seq, D_MODEL)]
    for kind in ("grad", "delta", "m", "v"):
        outs += [res[kind, n] for n in order]
    return tuple(outs)
```

```python
import math

import jax
import jax.numpy as jnp
from jax import lax
from jax.experimental import pallas as pl
from jax.experimental.pallas import tpu as pltpu

F32 = jnp.float32
MXU_DTYPE = jnp.bfloat16

D_MODEL = 1024
MIX = 1024
POOL_W = 512
SSM_W = 512
N_POOL_G = 4
POOL_GC = 128
SSM_C = 16
SSM_P = 64
NORM_EPS = 1e-5
N_DEV = 8
W_IN_COLS = 2 * MIX // N_DEV

ADAM_LR = 0.001
ADAM_B1 = 0.9
ADAM_B2 = 0.999
ADAM_EPS = 1e-08
ADAM_WD = 0.01
ADAM_STEP = 10

SUBLANES = 8
LANES = 128
HALO = 16
STATE_ROWS = 8
STATE_COLS = 256
CHUNK_GROUPS = STATE_COLS // SSM_P
CHUNK_CH = CHUNK_GROUPS * SSM_C
T_BLK = 256
SCAN_UNROLL = 16
TM_FWD = 512
TM_BWD = 512
VMEM_LIMIT = 56 * 1024 * 1024

MESH = pl.DeviceIdType.MESH
VMEM_SPEC = pl.BlockSpec(memory_space=pltpu.VMEM)
ANY_SPEC = pl.BlockSpec(memory_space=pl.ANY)


def _mm(a, b):
    return jnp.dot(a, b, preferred_element_type=F32)


def _mm_tn(a, b):
    return lax.dot_general(a, b, (((0,), (0,)), ((), ())), preferred_element_type=F32)


def _mm_nt(a, b):
    return lax.dot_general(a, b, (((1,), (1,)), ((), ())), preferred_element_type=F32)


def _mx(a):
    return a.astype(MXU_DTYPE)


def _sigmoid(v):
    return 1.0 / (1.0 + jnp.exp(-v))


_GELU_C = math.sqrt(2.0 / math.pi)
_GELU_A = 0.044715


def _gelu_and_grad(y):
    th = jnp.tanh(_GELU_C * (y + _GELU_A * y * y * y))
    val = 0.5 * y * (1.0 + th)
    grad = 0.5 * (1.0 + th) + 0.5 * y * (1.0 - th * th) * (_GELU_C * (1.0 + 3.0 * _GELU_A * y * y))
    return val, grad


def _params(**kw):
    return pltpu.CompilerParams(vmem_limit_bytes=VMEM_LIMIT, **kw)


def _ssm_dense(a_re, a_im, log_dt, b_re, b_im, c_re, c_im):
    dt = jnp.exp(log_dt)[:, None]
    mag = jnp.exp(a_re * dt)
    ang = a_im * dt
    lb_re = mag * jnp.cos(ang)
    lb_im = mag * jnp.sin(ang)
    den = a_re * a_re + a_im * a_im
    n_re = lb_re - 1.0
    n_im = lb_im
    f_re = (n_re * a_re + n_im * a_im) / den
    f_im = (n_im * a_re - n_re * a_im) / den
    bb_re = f_re[..., None] * b_re - f_im[..., None] * b_im
    bb_im = f_re[..., None] * b_im + f_im[..., None] * b_re

    bb = jnp.stack([bb_re, bb_im], axis=0).reshape(2, STATE_ROWS, CHUNK_GROUPS, SSM_P, SSM_C)
    rb = bb.transpose(1, 4, 0, 2, 3).reshape(STATE_ROWS, SSM_C, 2 * STATE_COLS)
    cc = jnp.stack([c_re, -c_im], axis=0).reshape(2, STATE_ROWS, CHUNK_GROUPS, SSM_C, SSM_P)
    rc = cc.transpose(1, 3, 0, 2, 4).reshape(STATE_ROWS, SSM_C, 2 * STATE_COLS)
    return (lb_re.reshape(STATE_ROWS, STATE_COLS), lb_im.reshape(STATE_ROWS, STATE_COLS), rb, rc)


def _ssm_chunked(per_channel):
    row_group = jnp.arange(CHUNK_CH) // SSM_C
    col_group = (jnp.arange(2 * STATE_COLS) // SSM_P) % CHUNK_GROUPS
    own_group = (row_group[:, None] == col_group[None, :]).astype(F32)
    even = (jnp.arange(STATE_ROWS) % 2 == 0).astype(F32)[:, None, None]
    half = jnp.tile(per_channel, (1, CHUNK_GROUPS, 1)) * own_group
    return jnp.concatenate([half * even, half * (1.0 - even)], axis=1)


def _inproj_fwd(x2, g_row, w_all, dep):
    n = x2.shape[0]
    tm = TM_FWD

    def body(x_ref, g_ref, w_ref, dep_ref, z_ref, h_ref):
        x = x_ref[...]
        r = lax.rsqrt(jnp.mean(x * x, axis=-1, keepdims=True) + NORM_EPS)
        h = _mx(x * r * g_ref[...])
        h_ref[...] = h
        for d in range(N_DEV):
            z_ref[:, d * W_IN_COLS:(d + 1) * W_IN_COLS] = _mm(h, w_ref[d])

    return pl.pallas_call(
        body, name="inproj_fwd",
        grid=(n // tm,),
        in_specs=[pl.BlockSpec((tm, D_MODEL), lambda i: (i, 0)),
                  pl.BlockSpec((1, D_MODEL), lambda i: (0, 0)),
                  pl.BlockSpec((N_DEV, D_MODEL, W_IN_COLS), lambda i: (0, 0, 0)),
                  ANY_SPEC],
        out_specs=[pl.BlockSpec((tm, 2 * MIX), lambda i: (i, 0)),
                   pl.BlockSpec((tm, D_MODEL), lambda i: (i, 0))],
        out_shape=[jax.ShapeDtypeStruct((n, 2 * MIX), F32),
                   jax.ShapeDtypeStruct((n, D_MODEL), MXU_DTYPE)],
        compiler_params=_params(dimension_semantics=("arbitrary",)),
    )(x2, g_row, w_all, dep)


def _loss_head(x2, tgt2, g_row):
    n = x2.shape[0]
    tm = TM_FWD

    def body(x_ref, t_ref, g_ref, dx_ref, loss_ref, dg_ref):
        @pl.when(pl.program_id(0) == 0)
        def _():
            loss_ref[...] = jnp.zeros_like(loss_ref)
            dg_ref[...] = jnp.zeros_like(dg_ref)

        x = x_ref[...]
        g = g_ref[...]
        r = lax.rsqrt(jnp.mean(x * x, axis=-1, keepdims=True) + NORM_EPS)
        xh = x * r
        e = xh * g - t_ref[...]
        loss_ref[...] += jnp.sum(jnp.sum(e * e, axis=-1, keepdims=True), axis=0, keepdims=True) * (0.5 / D_MODEL)
        dout = e * (1.0 / D_MODEL)
        dg_ref[...] += jnp.sum(dout * xh, axis=0, keepdims=True)
        gdy = dout * g
        dx_ref[...] = r * (gdy - xh * jnp.mean(xh * gdy, axis=-1, keepdims=True))

    return pl.pallas_call(
        body, name="loss_head",
        grid=(n // tm,),
        in_specs=[pl.BlockSpec((tm, D_MODEL), lambda i: (i, 0)),
                  pl.BlockSpec((tm, D_MODEL), lambda i: (i, 0)),
                  pl.BlockSpec((1, D_MODEL), lambda i: (0, 0))],
        out_specs=[pl.BlockSpec((tm, D_MODEL), lambda i: (i, 0)),
                   pl.BlockSpec((1, 1), lambda i: (0, 0)),
                   pl.BlockSpec((1, D_MODEL), lambda i: (0, 0))],
        out_shape=[jax.ShapeDtypeStruct((n, D_MODEL), F32),
                   jax.ShapeDtypeStruct((1, 1), F32),
                   jax.ShapeDtypeStruct((1, D_MODEL), F32)],
        compiler_params=_params(dimension_semantics=("arbitrary",)),
    )(x2, tgt2, g_row)


def _outproj_bwd(dx2, yg, w_out, dep):
    n = dx2.shape[0]
    tm = TM_BWD
    n_steps = n // tm

    def body(dx_ref, y_ref, w_ref, dep_ref, dy_ref, dw_ref, acc_ref):
        i = pl.program_id(0)

        @pl.when(i == 0)
        def _():
            acc_ref[...] = jnp.zeros_like(acc_ref)

        dxb = _mx(dx_ref[...])
        dy_ref[...] = _mm_nt(dxb, w_ref[...])
        acc_ref[...] += _mm_tn(y_ref[...], dxb)

        @pl.when(i == n_steps - 1)
        def _():
            dw_ref[...] = _mx(acc_ref[...])

    return pl.pallas_call(
        body, name="outproj_bwd",
        grid=(n_steps,),
        in_specs=[pl.BlockSpec((tm, D_MODEL), lambda i: (i, 0)),
                  pl.BlockSpec((tm, MIX), lambda i: (i, 0)),
                  pl.BlockSpec((MIX, D_MODEL), lambda i: (0, 0)),
                  ANY_SPEC],
        out_specs=[pl.BlockSpec((tm, MIX), lambda i: (i, 0)),
                   pl.BlockSpec((MIX, D_MODEL), lambda i: (0, 0))],
        out_shape=[jax.ShapeDtypeStruct((n, MIX), F32),
                   jax.ShapeDtypeStruct((MIX, D_MODEL), MXU_DTYPE)],
        scratch_shapes=[pltpu.VMEM((MIX, D_MODEL), F32)],
        compiler_params=_params(dimension_semantics=("arbitrary",)),
    )(dx2, yg, w_out, dep)


def _inproj_bwd(dz, h, x2, dx_in, g_row, w_all, dep):
    n = x2.shape[0]
    tm = TM_BWD
    n_steps = n // tm

    def body(dz_ref, h_ref, x_ref, dxi_ref, g_ref, w_ref, dep_ref, dxo_ref, dw_ref, dg_ref, acc_ref, wcat_ref):
        i = pl.program_id(0)

        @pl.when(i == 0)
        def _():
            acc_ref[...] = jnp.zeros_like(acc_ref)
            dg_ref[...] = jnp.zeros_like(dg_ref)
            for d in range(N_DEV):
                wcat_ref[:, d * W_IN_COLS:(d + 1) * W_IN_COLS] = w_ref[d]

        hb = h_ref[...]
        for d in range(N_DEV):
            acc_ref[d] += _mm_tn(hb, dz_ref[:, d * W_IN_COLS:(d + 1) * W_IN_COLS])
        dh = _mm_nt(dz_ref[...], wcat_ref[...])
        x = x_ref[...]
        r = lax.rsqrt(jnp.mean(x * x, axis=-1, keepdims=True) + NORM_EPS)
        xh = x * r
        dg_ref[...] += jnp.sum(dh * xh, axis=0, keepdims=True)
        gdy = dh * g_ref[...]
        dxo_ref[...] = dxi_ref[...] + r * (gdy - xh * jnp.mean(xh * gdy, axis=-1, keepdims=True))

        @pl.when(i == n_steps - 1)
        def _():
            dw_ref[...] = _mx(acc_ref[...])

    return pl.pallas_call(
        body, name="inproj_bwd",
        grid=(n_steps,),
        in_specs=[pl.BlockSpec((tm, 2 * MIX), lambda i: (i, 0)),
                  pl.BlockSpec((tm, D_MODEL), lambda i: (i, 0)),
                  pl.BlockSpec((tm, D_MODEL), lambda i: (i, 0)),
                  pl.BlockSpec((tm, D_MODEL), lambda i: (i, 0)),
                  pl.BlockSpec((1, D_MODEL), lambda i: (0, 0)),
                  pl.BlockSpec((N_DEV, D_MODEL, W_IN_COLS), lambda i: (0, 0, 0)),
                  ANY_SPEC],
        out_specs=[pl.BlockSpec((tm, D_MODEL), lambda i: (i, 0)),
                   pl.BlockSpec((N_DEV, D_MODEL, W_IN_COLS), lambda i: (0, 0, 0)),
                   pl.BlockSpec((1, D_MODEL), lambda i: (0, 0))],
        out_shape=[jax.ShapeDtypeStruct((n, D_MODEL), F32),
                   jax.ShapeDtypeStruct((N_DEV, D_MODEL, W_IN_COLS), MXU_DTYPE),
                   jax.ShapeDtypeStruct((1, D_MODEL), F32)],
        scratch_shapes=[pltpu.VMEM((N_DEV, D_MODEL, W_IN_COLS), F32),
                        pltpu.VMEM((D_MODEL, 2 * MIX), MXU_DTYPE)],
        compiler_params=_params(dimension_semantics=("arbitrary",)),
    )(dz, h, x2, dx_in, g_row, w_all, dep)


def _row_pos(t0, rows):
    return t0 + lax.broadcasted_iota(jnp.int32, (rows, LANES), 0)


def _pool_window_mean(upad, g, t0, t_blk):
    k = 2 << g
    w = upad
    sh = 1
    while sh < k:
        w = w + pltpu.roll(w, sh, 0)
        sh *= 2
    count = jnp.minimum(_row_pos(t0, t_blk) + 1, k).astype(F32)
    return w[HALO:] / count - upad[HALO:]


def _pool_window_bwd(qpad, g, t_blk):
    k = 2 << g
    n = t_blk + HALO
    w = qpad
    sh = 1
    while sh < k:
        w = w + pltpu.roll(w, n - sh, 0)
        sh *= 2
    return w[:t_blk]


class _StateBuf:
    def __init__(self, refs, t_blk):
        self.refs = refs
        self.t_blk = t_blk

    def put_chunk(self, b, j, val):
        for c in range(4):
            self.refs[4 * b + c][pl.ds(j, self.t_blk, stride=STATE_ROWS), :] = val[:, c * LANES:(c + 1) * LANES]

    def get_chunk(self, b, j):
        return jnp.concatenate(
            [self.refs[4 * b + c][pl.ds(j, self.t_blk, stride=STATE_ROWS), :] for c in range(4)], axis=-1)

    def load(self, b, r, part):
        return jnp.concatenate(
            [self.refs[4 * b + 2 * part + h][pl.ds(r, STATE_ROWS), :] for h in range(2)], axis=-1)

    def store(self, b, r, part, val):
        for h in range(2):
            self.refs[4 * b + 2 * part + h][pl.ds(r, STATE_ROWS), :] = val[:, h * LANES:(h + 1) * LANES]


def _state_scratch(nb, t_blk):
    return [pltpu.VMEM((t_blk * STATE_ROWS, LANES), F32) for _ in range(4 * nb)]


def _ssm_project_in(u_ssm, wb_ref, buf, nb):
    t_blk = u_ssm.shape[0] // nb
    ub = _mx(u_ssm)
    for j in range(STATE_ROWS):
        m = j // 2
        bu = _mm(ub[:, m * LANES:(m + 1) * LANES], wb_ref[j])
        for b in range(nb):
            buf.put_chunk(b, j, bu[b * t_blk:(b + 1) * t_blk])


def _scan_forward(buf, lbr, lbi, init, nb):
    def step(t, carry):
        r = pl.multiple_of(t * STATE_ROWS, STATE_ROWS)
        out = []
        for b in range(nb):
            sr, si = carry[2 * b], carry[2 * b + 1]
            nr = lbr * sr - lbi * si + buf.load(b, r, 0)
            ni = lbr * si + lbi * sr + buf.load(b, r, 1)
            buf.store(b, r, 0, nr)
            buf.store(b, r, 1, ni)
            out += [nr, ni]
        return tuple(out)

    def body(i, carry):
        for u in range(SCAN_UNROLL):
            carry = step(i * SCAN_UNROLL + u, carry)
        return carry

    return lax.fori_loop(0, buf.t_blk // SCAN_UNROLL, body, init)


def _ssm_project_out(chunk, wc_ref):
    tiles = []
    for m in range(4):
        acc = None
        for j in (2 * m, 2 * m + 1):
            part = _mm_nt(chunk(j), wc_ref[j])
            acc = part if acc is None else acc + part
        tiles.append(acc)
    return jnp.concatenate(tiles, axis=-1)


def _layer_fwd(x3, z3, g_row, w_in, pool_w, pool_scale, lbr, lbi, wb, wc, d_skip, glu_w, glu_b, w_out, dep):
    nb, seq, _ = x3.shape
    t_blk = min(T_BLK, seq)
    n_t = seq // t_blk
    halo_per_blk = t_blk // HALO
    rows = nb * t_blk
    fused = z3 is None

    def body(*refs):
        if fused:
            (x_ref, g_ref, wi_ref, pw_ref, ps_ref, lbr_ref, lbi_ref, wb_ref, wc_ref, dsk_ref, gw_ref, gb_ref, wo_ref,
             dep_ref, z_ref, h_ref, yg_ref, sc_ref, act_ref, dact_ref, pooled_ref, ypre_ref, xo_ref,
             carry_ref, halo_ref, *s_refs) = refs
        else:
            (x_ref, z_ref, zh_ref, pw_ref, ps_ref, lbr_ref, lbi_ref, wb_ref, wc_ref, dsk_ref, gw_ref, gb_ref, wo_ref,
             dep_ref, yg_ref, sc_ref, act_ref, dact_ref, pooled_ref, ypre_ref, xo_ref, carry_ref, *s_refs) = refs
        i = pl.program_id(0)
        t0 = i * t_blk
        buf = _StateBuf(s_refs, t_blk)
        both = lambda lo, hi: z_ref[:, :, lo:hi].reshape(rows, hi - lo)

        @pl.when(i == 0)
        def _():
            carry_ref[...] = jnp.zeros_like(carry_ref)
            if fused:
                halo_ref[...] = jnp.zeros_like(halo_ref)

        x = x_ref[...].reshape(rows, D_MODEL)
        if fused:
            r = lax.rsqrt(jnp.mean(x * x, axis=-1, keepdims=True) + NORM_EPS)
            h = _mx(x * r * g_ref[...])
            h_ref[...] = h.reshape(nb, t_blk, D_MODEL)
            for d in range(N_DEV):
                z_ref[:, :, d * W_IN_COLS:(d + 1) * W_IN_COLS] = _mm(h, wi_ref[d]).reshape(nb, t_blk, W_IN_COLS)

        u_ssm = both(POOL_W, MIX)
        _ssm_project_in(u_ssm, wb_ref, buf, nb)
        init = tuple(carry_ref[b, :, h * STATE_COLS:(h + 1) * STATE_COLS] for b in range(nb) for h in range(2))
        fin = _scan_forward(buf, lbr_ref[...], lbi_ref[...], init, nb)
        for b in range(nb):
            carry_ref[b, :, 0:STATE_COLS] = fin[2 * b]
            carry_ref[b, :, STATE_COLS:2 * STATE_COLS] = fin[2 * b + 1]

        def chunk(j):
            states = _mx(jnp.concatenate([buf.get_chunk(b, j) for b in range(nb)], axis=0))
            sc_ref[:, j] = states.reshape(nb, t_blk, 2 * STATE_COLS)
            return states

        y = _ssm_project_out(chunk, wc_ref) + dsk_ref[...] * u_ssm
        yg, dgelu = _gelu_and_grad(y)
        ygb = _mx(yg)
        act_ref[...] = ygb.reshape(nb, t_blk, SSM_W)
        dact_ref[...] = _mx(dgelu).reshape(nb, t_blk, SSM_W)
        o_ssm = yg * _sigmoid(_mm(ygb, gw_ref[...]) + gb_ref[...])
        gp = both(MIX + POOL_W, 2 * MIX)
        parts = []
        first = (i == 0)
        for g in range(N_POOL_G):
            cols = slice(g * POOL_GC, (g + 1) * POOL_GC)
            pooled = []
            for b in range(nb):
                halo = halo_ref[b, :, cols] if fused else jnp.where(first, 0.0, zh_ref[b, :, cols])
                pooled.append(_pool_window_mean(jnp.concatenate([halo, z_ref[b, :, cols]], axis=0), g, t0, t_blk))
            pb = _mx(jnp.concatenate(pooled, axis=0))
            ypre = _mm(pb, pw_ref[g])
            pooled_ref[:, :, cols] = pb.reshape(nb, t_blk, POOL_GC)
            ypre_ref[:, :, cols] = ypre.reshape(nb, t_blk, POOL_GC)
            gpp = both(MIX + g * POOL_GC, MIX + (g + 1) * POOL_GC)
            parts.append(_mx(ypre * ps_ref[:, cols] * (gpp * _sigmoid(gpp))))
        parts.append(_mx(o_ssm * (gp * _sigmoid(gp))))
        gated = jnp.concatenate(parts, axis=-1)
        yg_ref[...] = gated.reshape(nb, t_blk, MIX)
        xo_ref[...] = (x + _mm(gated, wo_ref[...])).reshape(nb, t_blk, D_MODEL)
        if fused:
            halo_ref[...] = z_ref[:, t_blk - HALO:, 0:POOL_W]

    const = lambda *shape: pl.BlockSpec(shape, lambda i: (0,) * len(shape))
    tokens = lambda width: pl.BlockSpec((nb, t_blk, width), lambda i: (0, i, 0))
    mixer_specs = [const(N_POOL_G, POOL_GC, POOL_GC), const(1, POOL_W),
                   const(STATE_ROWS, STATE_COLS), const(STATE_ROWS, STATE_COLS),
                   const(STATE_ROWS, LANES, 2 * STATE_COLS), const(STATE_ROWS, LANES, 2 * STATE_COLS),
                   const(1, SSM_W), const(SSM_W, SSM_W), const(1, SSM_W), const(MIX, D_MODEL), ANY_SPEC]
    mixer_args = (pool_w, pool_scale, lbr, lbi, wb, wc, d_skip, glu_w, glu_b, w_out, dep)
    out_specs = [tokens(MIX), pl.BlockSpec((nb, STATE_ROWS, t_blk, 2 * STATE_COLS), lambda i: (0, 0, i, 0)),
                 tokens(SSM_W), tokens(SSM_W), tokens(POOL_W), tokens(POOL_W), tokens(D_MODEL)]
    out_shape = [jax.ShapeDtypeStruct((nb, seq, MIX), MXU_DTYPE),
                 jax.ShapeDtypeStruct((nb, STATE_ROWS, seq, 2 * STATE_COLS), MXU_DTYPE),
                 jax.ShapeDtypeStruct((nb, seq, SSM_W), MXU_DTYPE),
                 jax.ShapeDtypeStruct((nb, seq, SSM_W), MXU_DTYPE),
                 jax.ShapeDtypeStruct((nb, seq, POOL_W), MXU_DTYPE),
                 jax.ShapeDtypeStruct((nb, seq, POOL_W), F32),
                 jax.ShapeDtypeStruct((nb, seq, D_MODEL), F32)]
    scratch = [pltpu.VMEM((nb, STATE_ROWS, 2 * STATE_COLS), F32)]
    if fused:
        in_specs = [tokens(D_MODEL), const(1, D_MODEL), const(N_DEV, D_MODEL, W_IN_COLS)] + mixer_specs
        args = (x3, g_row, w_in) + mixer_args
        out_specs = [tokens(2 * MIX), tokens(D_MODEL)] + out_specs
        out_shape = [jax.ShapeDtypeStruct((nb, seq, 2 * MIX), F32),
                     jax.ShapeDtypeStruct((nb, seq, D_MODEL), MXU_DTYPE)] + out_shape
        scratch = scratch + [pltpu.VMEM((nb, HALO, POOL_W), F32)]
    else:
        in_specs = [tokens(D_MODEL), tokens(2 * MIX),
                    pl.BlockSpec((nb, HALO, POOL_W), lambda i: (0, jnp.maximum(i * halo_per_blk - 1, 0), 0))] + mixer_specs
        args = (x3, z3, z3) + mixer_args
    return pl.pallas_call(
        body, name="layer_fwd" if fused else "mixer_fwd",
        grid=(n_t,),
        in_specs=in_specs, out_specs=out_specs, out_shape=out_shape,
        scratch_shapes=scratch + _state_scratch(nb, t_blk),
        compiler_params=_params(dimension_semantics=("arbitrary",)),
    )(*args)


def _mixer_bwd(z3, dy3, states, kept, pool_w, pool_scale, lbr, lbi, wb, wc, d_skip, glu_w, glu_b):
    nb, seq, _ = z3.shape
    t_blk = min(T_BLK, seq)
    n_t = seq // t_blk
    halo_per_blk = t_blk // HALO
    rows = nb * t_blk

    def body(z_ref, dy_ref, sc_ref, sch_ref, act_ref, dact_ref, pooled_ref, ypre_ref, pw_ref, ps_ref, lbr_ref, lbi_ref, wb_ref, wc_ref, dsk_ref,
             gw_ref, gb_ref,
             dz_ref, dpw_ref, dps_ref, dlbr_ref, dlbi_ref, dwb_ref, dwc_ref, ddsk_ref, dgw_ref, dgb_ref,
             gcarry_ref, qcarry_ref, du_ref, dgw_acc, *g_refs):
        i = pl.program_id(0)
        blk = n_t - 1 - i
        t0 = blk * t_blk
        gbuf = _StateBuf(g_refs, t_blk)

        @pl.when(i == 0)
        def _():
            gcarry_ref[...] = jnp.zeros_like(gcarry_ref)
            qcarry_ref[...] = jnp.zeros_like(qcarry_ref)
            for ref in (dpw_ref, dps_ref, dlbr_ref, dlbi_ref, dwb_ref, dwc_ref, ddsk_ref, dgw_acc, dgb_ref):
                ref[...] = jnp.zeros_like(ref)

        lbr_v = lbr_ref[...]
        lbi_v = lbi_ref[...]

        both = lambda ref, lo, hi: ref[:, :, lo:hi].reshape(rows, hi - lo)
        split = lambda val: val.reshape(nb, t_blk, val.shape[-1])
        states = lambda j: sc_ref[:, j].reshape(rows, 2 * STATE_COLS)
        first = (blk == 0)

        u_ssm = both(z_ref, POOL_W, MIX)
        ygb = act_ref[...].reshape(rows, SSM_W)
        yg = ygb.astype(F32)
        dgelu = dact_ref[...].reshape(rows, SSM_W).astype(F32)
        sg = _sigmoid(_mm(ygb, gw_ref[...]) + gb_ref[...])
        o_ssm = yg * sg
        gp = both(z_ref, MIX + POOL_W, 2 * MIX)
        sgm = _sigmoid(gp)
        dyv = both(dy_ref, POOL_W, MIX)
        dz_ref[:, :, MIX + POOL_W:2 * MIX] = split(_mx(dyv * o_ssm * (sgm * (1.0 + gp * (1.0 - sgm)))))
        do = dyv * (gp * sgm)
        dv = do * yg * (sg * (1.0 - sg))
        dvb = _mx(dv)
        dgb_ref[...] += jnp.sum(dv, axis=0, keepdims=True)
        dgw_acc[...] += _mm_tn(ygb, dvb)
        dyp = (do * sg + _mm_nt(dvb, gw_ref[...])) * dgelu
        ddsk_ref[...] += jnp.sum(dyp * u_ssm, axis=0, keepdims=True)
        dypb = _mx(dyp)
        for j in range(STATE_ROWS):
            m = j // 2
            dyt = dypb[:, m * LANES:(m + 1) * LANES]
            ds = _mm(dyt, wc_ref[j])
            for b in range(nb):
                gbuf.put_chunk(b, j, ds[b * t_blk:(b + 1) * t_blk])
            dwc_ref[j] += _mm_tn(dyt, states(j))
        du_ref[...] = split(dsk_ref[...] * dyp)

        for g in range(N_POOL_G):
            cols = slice(g * POOL_GC, (g + 1) * POOL_GC)
            pb = both(pooled_ref, g * POOL_GC, (g + 1) * POOL_GC)
            ypre = both(ypre_ref, g * POOL_GC, (g + 1) * POOL_GC)
            gpp = both(z_ref, MIX + g * POOL_GC, MIX + (g + 1) * POOL_GC)
            sgp = _sigmoid(gpp)
            dyg = both(dy_ref, g * POOL_GC, (g + 1) * POOL_GC)
            scale = ps_ref[:, cols]
            dz_ref[:, :, MIX + g * POOL_GC:MIX + (g + 1) * POOL_GC] = split(_mx(
                dyg * (ypre * scale) * (sgp * (1.0 + gpp * (1.0 - sgp)))))
            dyc = dyg * (gpp * sgp)
            dps_ref[:, cols] += jnp.sum(dyc * ypre, axis=0, keepdims=True)
            dypre = _mx(dyc * scale)
            dpw_ref[g] += _mm_tn(pb, dypre)
            dpooled = _mm_nt(dypre, pw_ref[g])
            count = jnp.minimum(_row_pos(t0, t_blk) + 1, 2 << g).astype(F32)
            for b in range(nb):
                dp = dpooled[b * t_blk:(b + 1) * t_blk]
                q = dp / count
                qpad = jnp.concatenate([q, qcarry_ref[b, :, cols]], axis=0)
                qcarry_ref[b, :, cols] = q[:HALO]
                dz_ref[b, :, cols] = _mx(_pool_window_bwd(qpad, g, t_blk) - dp)

        def rev_step(t, carry):
            r = pl.multiple_of(t * STATE_ROWS, STATE_ROWS)
            out = []
            for b in range(nb):
                gr, gi = carry[2 * b], carry[2 * b + 1]
                ngr = lbr_v * gr + lbi_v * gi + gbuf.load(b, r, 0)
                ngi = lbr_v * gi - lbi_v * gr + gbuf.load(b, r, 1)
                gbuf.store(b, r, 0, ngr)
                gbuf.store(b, r, 1, ngi)
                out += [ngr, ngi]
            return tuple(out)

        def rev_body(i, carry):
            for u in range(SCAN_UNROLL):
                carry = rev_step(t_blk - 1 - (i * SCAN_UNROLL + u), carry)
            return carry

        init_g = tuple(gcarry_ref[b, :, h * STATE_COLS:(h + 1) * STATE_COLS] for b in range(nb) for h in range(2))
        fin = lax.fori_loop(0, t_blk // SCAN_UNROLL, rev_body, init_g)
        for b in range(nb):
            gcarry_ref[b, :, 0:STATE_COLS] = fin[2 * b]
            gcarry_ref[b, :, STATE_COLS:2 * STATE_COLS] = fin[2 * b + 1]

        ub = _mx(u_ssm)
        for m in range(4):
            acc = both(du_ref, m * LANES, (m + 1) * LANES)
            for j in (2 * m, 2 * m + 1):
                g = jnp.concatenate([gbuf.get_chunk(b, j) for b in range(nb)], axis=0)
                gj = _mx(g)
                acc = acc + _mm_nt(gj, wb_ref[j])
                dwb_ref[j] += _mm_tn(ub[:, m * LANES:(m + 1) * LANES], gj)
                shifted = []
                for b in range(nb):
                    before = jnp.where(first, 0.0, sch_ref[b, j].astype(F32))
                    spad = jnp.concatenate([before, sc_ref[b, j].astype(F32)], axis=0)
                    shifted.append(pltpu.roll(spad, 1, 0)[HALO:])
                s_prev = jnp.concatenate(shifted, axis=0)
                g_re, g_im = g[:, :STATE_COLS], g[:, STATE_COLS:]
                p_re, p_im = s_prev[:, :STATE_COLS], s_prev[:, STATE_COLS:]
                dlbr_ref[j:j + 1, :] += jnp.sum(g_re * p_re + g_im * p_im, axis=0, keepdims=True)
                dlbi_ref[j:j + 1, :] += jnp.sum(g_im * p_re - g_re * p_im, axis=0, keepdims=True)
            dz_ref[:, :, POOL_W + m * LANES:POOL_W + (m + 1) * LANES] = split(_mx(acc))

        @pl.when(i == n_t - 1)
        def _():
            dgw_ref[...] = _mx(dgw_acc[...])

    const = lambda *shape: pl.BlockSpec(shape, lambda i: (0,) * len(shape))
    rev = lambda i: n_t - 1 - i
    out_shape = [jax.ShapeDtypeStruct((nb, seq, 2 * MIX), MXU_DTYPE),
                 jax.ShapeDtypeStruct((N_POOL_G, POOL_GC, POOL_GC), F32),
                 jax.ShapeDtypeStruct((1, POOL_W), F32),
                 jax.ShapeDtypeStruct((STATE_ROWS, STATE_COLS), F32),
                 jax.ShapeDtypeStruct((STATE_ROWS, STATE_COLS), F32),
                 jax.ShapeDtypeStruct((STATE_ROWS, LANES, 2 * STATE_COLS), F32),
                 jax.ShapeDtypeStruct((STATE_ROWS, LANES, 2 * STATE_COLS), F32),
                 jax.ShapeDtypeStruct((1, SSM_W), F32),
                 jax.ShapeDtypeStruct((SSM_W, SSM_W), MXU_DTYPE),
                 jax.ShapeDtypeStruct((1, SSM_W), F32)]
    return pl.pallas_call(
        body, name="mixer_bwd",
        grid=(n_t,),
        in_specs=[pl.BlockSpec((nb, t_blk, 2 * MIX), lambda i: (0, rev(i), 0)),
                  pl.BlockSpec((nb, t_blk, MIX), lambda i: (0, rev(i), 0)),
                  pl.BlockSpec((nb, STATE_ROWS, t_blk, 2 * STATE_COLS), lambda i: (0, 0, rev(i), 0)),
                  pl.BlockSpec((nb, STATE_ROWS, HALO, 2 * STATE_COLS),
                               lambda i: (0, 0, jnp.maximum(rev(i) * halo_per_blk - 1, 0), 0)),
                  pl.BlockSpec((nb, t_blk, SSM_W), lambda i: (0, rev(i), 0)),
                  pl.BlockSpec((nb, t_blk, SSM_W), lambda i: (0, rev(i), 0)),
                  pl.BlockSpec((nb, t_blk, POOL_W), lambda i: (0, rev(i), 0)),
                  pl.BlockSpec((nb, t_blk, POOL_W), lambda i: (0, rev(i), 0)),
                  const(N_POOL_G, POOL_GC, POOL_GC), const(1, POOL_W),
                  const(STATE_ROWS, STATE_COLS), const(STATE_ROWS, STATE_COLS),
                  const(STATE_ROWS, LANES, 2 * STATE_COLS), const(STATE_ROWS, LANES, 2 * STATE_COLS),
                  const(1, SSM_W), const(SSM_W, SSM_W), const(1, SSM_W)],
        out_specs=[pl.BlockSpec((nb, t_blk, 2 * MIX), lambda i: (0, rev(i), 0))]
                  + [const(*s.shape) for s in out_shape[1:]],
        out_shape=out_shape,
        scratch_shapes=[pltpu.VMEM((nb, STATE_ROWS, 2 * STATE_COLS), F32),
                        pltpu.VMEM((nb, HALO, POOL_W), F32),
                        pltpu.VMEM((nb, t_blk, SSM_W), F32),
                        pltpu.VMEM((SSM_W, SSM_W), F32)]
                       + _state_scratch(nb, t_blk),
        compiler_params=_params(dimension_semantics=("arbitrary",)),
    )(z3, dy3, states, states, *kept, pool_w, pool_scale, lbr, lbi, wb, wc, d_skip, glu_w, glu_b)


def _mesh_place():
    x, y, c = lax.axis_index("x"), lax.axis_index("y"), lax.axis_index("c")
    return x, y, c


def _flip(place, k):
    x, y, c = place
    return (1 - x if k & 4 else x, 1 - y if k & 2 else y, 1 - c if k & 1 else c)


def _index(place):
    x, y, c = place
    return 4 * x + 2 * y + c


HBM_SPEC = pl.BlockSpec(memory_space=pltpu.HBM)
SEM_SPEC = pl.BlockSpec(memory_space=pltpu.SEMAPHORE)
_EFFECT = pltpu.SideEffectType.DATAFLOW_SIDE_EFFECTING
N_PEERS = N_DEV - 1


def _exchange_copies(src_refs, land_refs, send_sems, recv_sems):
    me = _mesh_place()
    mine = _index(me)
    out = []
    for a, land_ref in enumerate(land_refs):
        for k in range(1, N_DEV):
            peer = _flip(me, k)
            theirs = _index(peer)
            n = a * N_PEERS + k - 1
            src = src_refs[a].at[theirs] if src_refs else land_ref.at[mine]
            send = pltpu.make_async_remote_copy(
                src_ref=src, dst_ref=land_ref.at[mine], send_sem=send_sems.at[n], recv_sem=recv_sems.at[n],
                device_id=peer, device_id_type=MESH)
            recv = pltpu.make_async_remote_copy(
                src_ref=src, dst_ref=land_ref.at[theirs], send_sem=send_sems.at[n], recv_sem=recv_sems.at[n],
                device_id=peer, device_id_type=MESH)
            out.append((send, recv))
    return out


def _exchange_start(srcs, lands, after, name):
    arrays = tuple(srcs) + tuple(lands)
    n_src, n_all = len(srcs), len(arrays)
    n_copies = len(lands) * N_PEERS

    def body(*refs):
        send_sems, recv_sems = refs[n_all + 1], refs[n_all + 2]
        token = refs[-1]
        for send, _ in _exchange_copies(refs[:n_src], refs[n_src:n_all], send_sems, recv_sems):
            send.start()
        token[...] = jnp.zeros_like(token)

    res = pl.pallas_call(
        body, name=name,
        in_specs=[HBM_SPEC] * n_all + [ANY_SPEC],
        out_specs=[SEM_SPEC, SEM_SPEC] + [HBM_SPEC] * n_all + [VMEM_SPEC],
        out_shape=[pltpu.SemaphoreType.DMA((n_copies,)), pltpu.SemaphoreType.DMA((n_copies,))]
                  + [pltpu.HBM(a.shape, a.dtype) for a in arrays] + [jax.ShapeDtypeStruct((SUBLANES, LANES), F32)],
        input_output_aliases={i: 2 + i for i in range(n_all)},
        compiler_params=pltpu.CompilerParams(has_side_effects=_EFFECT),
    )(*[pltpu.with_memory_space_constraint(a, pltpu.HBM) for a in arrays], after)
    return tuple(res[:-1]), res[-1]


def _exchange_wait(handle, n_lands, after, name):
    send_sems, recv_sems = handle[0], handle[1]
    arrays = handle[2:]
    n_all = len(arrays)
    n_src = n_all - n_lands

    def body(*refs):
        for send, recv in _exchange_copies(refs[:n_src], refs[n_src:n_all], refs[n_all], refs[n_all + 1]):
            send.wait_send()
            recv.wait_recv()

    res = pl.pallas_call(
        body, name=name,
        in_specs=[HBM_SPEC] * n_all + [SEM_SPEC, SEM_SPEC, ANY_SPEC],
        out_specs=[HBM_SPEC] * n_all,
        out_shape=[pltpu.HBM(a.shape, a.dtype) for a in arrays],
        input_output_aliases={i: i for i in range(n_all)},
        compiler_params=pltpu.CompilerParams(has_side_effects=_EFFECT),
    )(*arrays, send_sems, recv_sems, after)
    return tuple(res[:n_src]), tuple(res[n_src:])


def _weight_zones(w_in, glu_w, w_out, my_idx):
    shards = (w_in, glu_w, w_out)
    depth = w_in.shape[0]

    def body(idx_ref, *refs):
        ins, zones = refs[:len(shards)], refs[len(shards):]
        for l in range(depth):
            for a, src in enumerate(ins):
                zones[l * len(shards) + a][0] = _mx(src[l])

    whole = lambda s: pl.BlockSpec(s.shape, lambda i, idx: (0,) * s.ndim)
    return pl.pallas_call(
        body, name="weight_zones",
        grid_spec=pltpu.PrefetchScalarGridSpec(
            num_scalar_prefetch=1, grid=(1,),
            in_specs=[whole(s) for s in shards],
            out_specs=[pl.BlockSpec((1,) + s.shape[1:], lambda i, idx: (idx[0], 0, 0))
                       for _ in range(depth) for s in shards]),
        out_shape=[jax.ShapeDtypeStruct((N_DEV,) + s.shape[1:], MXU_DTYPE) for _ in range(depth) for s in shards],
        compiler_params=_params(dimension_semantics=("arbitrary",)),
    )(my_idx.reshape(1).astype(jnp.int32), *shards)


def _allreduce_packed(p):
    rows = p.shape[0]
    half = rows // 2
    quarter = half // 4

    def body(p_ref, o_ref, part_ref, sib_ref, got_ref, send_sems, recv_sems):
        x, y, c = _mesh_place()
        sibling = (x, y, 1 - c)
        chip = 2 * x + y
        chips = [(k, (1 - x if k & 2 else x, 1 - y if k & 1 else y, c), chip ^ k) for k in (1, 2, 3)]
        my_half = pl.multiple_of(c * half, SUBLANES)
        other_half = pl.multiple_of((1 - c) * half, SUBLANES)

        def copy(n, src, dst, to):
            return pltpu.make_async_remote_copy(src_ref=src, dst_ref=dst, send_sem=send_sems.at[n],
                                                recv_sem=recv_sems.at[n], device_id=to, device_id_type=MESH)

        def quarter_of(ref, base, q):
            return ref.at[pl.ds(pl.multiple_of(base + q * quarter, SUBLANES), quarter)]

        swap = copy(0, p_ref.at[pl.ds(other_half, half)], sib_ref, sibling)
        swap.start()
        swap.wait()
        part_ref[...] = p_ref[pl.ds(my_half, half), :] + sib_ref[...]

        scatter = [copy(k, quarter_of(part_ref, 0, q), got_ref.at[k - 1], to) for k, to, q in chips]
        for cp in scatter:
            cp.start()
        total = part_ref[pl.ds(pl.multiple_of(chip * quarter, SUBLANES), quarter), :]
        for cp, (k, _, _) in zip(scatter, chips):
            cp.wait()
            total = total + got_ref[k - 1]
        mine = pl.multiple_of(my_half + chip * quarter, SUBLANES)
        o_ref[pl.ds(mine, quarter), :] = total

        gather = [copy(3 + k, o_ref.at[pl.ds(mine, quarter)], o_ref.at[pl.ds(mine, quarter)], to) for k, to, _ in chips]
        for cp in gather:
            cp.start()
        for k, to, q in chips:
            theirs = quarter_of(o_ref, my_half, q)
            copy(3 + k, theirs, theirs, to).wait_recv()
        for cp in gather:
            cp.wait_send()

        back = copy(7, o_ref.at[pl.ds(my_half, half)], o_ref.at[pl.ds(my_half, half)], sibling)
        back.start()
        copy(7, o_ref.at[pl.ds(other_half, half)], o_ref.at[pl.ds(other_half, half)], sibling).wait_recv()
        back.wait_send()

    return pl.pallas_call(
        body, name="comm_allreduce_packed",
        in_specs=[VMEM_SPEC],
        out_specs=VMEM_SPEC,
        out_shape=jax.ShapeDtypeStruct(p.shape, F32),
        scratch_shapes=[pltpu.VMEM((half, LANES), F32),
                        pltpu.VMEM((half, LANES), F32),
                        pltpu.VMEM((3, quarter, LANES), F32),
                        pltpu.SemaphoreType.DMA((8,)),
                        pltpu.SemaphoreType.DMA((8,))],
        compiler_params=_params(),
    )(p)


def _adamw_math(w, g, m, v):
    m = ADAM_B1 * m + (1.0 - ADAM_B1) * g
    v = ADAM_B2 * v + (1.0 - ADAM_B2) * (g * g)
    m_hat = m / (1.0 - ADAM_B1 ** ADAM_STEP)
    v_hat = v / (1.0 - ADAM_B2 ** ADAM_STEP)
    delta = -ADAM_LR * (m_hat / (jnp.sqrt(v_hat) + ADAM_EPS) + ADAM_WD * w)
    return delta, m, v


def _adamw_summed(received, own, my_idx, w, m, v, name):
    depth, r, c = w.shape
    tr = min(r, 128)

    def body(idx_ref, *refs):
        r_refs, o_refs = refs[:depth], refs[depth:2 * depth]
        w_ref, m_ref, v_ref, g_ref, d_ref, nm_ref, nv_ref = refs[2 * depth:]
        me = idx_ref[0]
        for l in range(depth):
            g = jnp.zeros((tr, c), F32)
            for q in range(N_DEV):
                g = g + jnp.where(q == me, o_refs[l][0], r_refs[l][q]).astype(F32)
            g_ref[l] = g
            d_ref[l], nm_ref[l], nv_ref[l] = _adamw_math(w_ref[l], g, m_ref[l], v_ref[l])

    blk = pl.BlockSpec((depth, tr, c), lambda i, idx: (0, i, 0))
    return pl.pallas_call(
        body, name=name,
        grid_spec=pltpu.PrefetchScalarGridSpec(
            num_scalar_prefetch=1, grid=(r // tr,),
            in_specs=[pl.BlockSpec((N_DEV, tr, c), lambda i, idx: (0, i, 0))] * depth
                     + [pl.BlockSpec((1, tr, c), lambda i, idx: (idx[0], i, 0))] * depth
                     + [blk, blk, blk],
            out_specs=[blk] * 4),
        out_shape=[jax.ShapeDtypeStruct((depth, r, c), F32)] * 4,
        compiler_params=_params(dimension_semantics=("arbitrary",)),
    )(my_idx.reshape(1).astype(jnp.int32), *received, *own, w, m, v)


def _adamw_small(ws, gs, ms, vs):
    n = len(ws)
    depth = ws[0].shape[0]
    quarters = 4

    def spec(a):
        per_layer = a.shape[0] == depth
        split = a.ndim >= 3 and a.shape[1] % quarters == 0 and a.shape[1] >= quarters
        block = (1, a.shape[1] // quarters if split else a.shape[1]) + a.shape[2:]
        rest = (0,) * (a.ndim - 2)
        return pl.BlockSpec(block, lambda l, s: ((l if per_layer else 0), (s if split else 0)) + rest)

    def body(*refs):
        w_refs, g_refs, m_refs, v_refs = (refs[k * n:(k + 1) * n] for k in range(4))
        d_refs, nm_refs, nv_refs = (refs[(4 + k) * n:(5 + k) * n] for k in range(3))
        for k in range(n):
            d_refs[k][...], nm_refs[k][...], nv_refs[k][...] = _adamw_math(
                w_refs[k][...], g_refs[k][...], m_refs[k][...], v_refs[k][...])

    specs = [spec(a) for a in ws]
    shapes = [jax.ShapeDtypeStruct(a.shape, F32) for a in ws]
    res = pl.pallas_call(
        body, name="adamw_small",
        grid=(depth, quarters),
        in_specs=specs * 4,
        out_specs=specs * 3,
        out_shape=shapes * 3,
        compiler_params=_params(dimension_semantics=("arbitrary", "arbitrary")),
    )(*ws, *gs, *ms, *vs)
    return res[:n], res[n:2 * n], res[2 * n:]


_PACK_ROWS = SUBLANES * N_DEV


def _pack(arrays):
    flat = jnp.concatenate([a.reshape(-1) for a in arrays])
    per = _PACK_ROWS * LANES
    total = -(-flat.shape[0] // per) * per
    flat = jnp.pad(flat, (0, total - flat.shape[0]))
    return flat.reshape(total // LANES, LANES)


def _unpack(packed, like):
    flat = packed.reshape(-1)
    out = []
    off = 0
    for a in like:
        out.append(flat[off:off + a.size].reshape(a.shape))
        off += a.size
    return out


def kernel(x, norm_g, w_in, pool_w, pool_scale, a_re, a_im, log_dt, b_re, b_im, c_re, c_im, d_skip, glu_w, glu_b, w_out, final_g, loss_target, m_norm_g, m_w_in, m_pool_w, m_pool_scale, m_a_re, m_a_im, m_log_dt, m_b_re, m_b_im, m_c_re, m_c_im, m_d_skip, m_glu_w, m_glu_b, m_w_out, m_final_g, v_norm_g, v_w_in, v_pool_w, v_pool_scale, v_a_re, v_a_im, v_log_dt, v_b_re, v_b_im, v_c_re, v_c_im, v_d_skip, v_glu_w, v_glu_b, v_w_out, v_final_g):
    nb, seq, _ = x.shape
    n_tok = nb * seq
    depth = norm_g.shape[0]

    my_idx = _index(_mesh_place())

    zones = _weight_zones(w_in, glu_w, w_out, my_idx)

    def gather_start(l, after):
        return _exchange_start((), zones[3 * l:3 * l + 3], after, f"comm_gather_start_{l}")

    def gather_wait(handle, after, l):
        _, (win, glu, wout) = _exchange_wait(handle, 3, after, f"comm_gather_wait_{l}")
        return win, glu.reshape(SSM_W, SSM_W), wout.reshape(MIX, D_MODEL)

    xs = [x.reshape(n_tok, D_MODEL)]
    first_w_in, dep = _exchange_start((), zones[0:1], xs[0], "comm_gather_start_0_w_in")

    (lbr, lbi, rb, rc), dense_vjp = jax.vjp(jax.vmap(_ssm_dense), a_re, a_im, log_dt + dep[0, 0], b_re, b_im, c_re, c_im)
    chunk_all = jax.vmap(_ssm_chunked)
    (wb, wct), chunk_vjp = jax.vjp(lambda p, q: (chunk_all(p), chunk_all(q)), rb, rc)
    wb_m, wct_m = _mx(wb), _mx(wct)
    pool_w_m = _mx(pool_w)

    def layer_params(l):
        return (pool_w_m[l], pool_scale[l][None], lbr[l], lbi[l], wb_m[l], wct_m[l], d_skip[l][None],
                weights[l][1], glu_b[l][None])

    saved = []
    weights = []
    for l in range(depth):
        if l == 0:
            _, (win,) = _exchange_wait(first_w_in, 1, wct_m, "comm_gather_wait_0_w_in")
            rest, dep = _exchange_start((), zones[1:3], win, "comm_gather_start_0_rest")
            z, h = _inproj_fwd(xs[-1], norm_g[l][None], win, dep)
            _, (glu, wout) = _exchange_wait(rest, 2, z, "comm_gather_wait_0_rest")
            weights.append((win, glu.reshape(SSM_W, SSM_W), wout.reshape(MIX, D_MODEL)))
            handle, dep = gather_start(1, weights[0][2])
            z3 = z.reshape(nb, seq, 2 * MIX)
            yg, states, *kept, x_next = _layer_fwd(xs[-1].reshape(nb, seq, D_MODEL), z3, None, None,
                                                   *layer_params(l), weights[l][2], dep)
        else:
            weights.append(gather_wait(handle, xs[-1], l))
            if l + 1 < depth:
                handle, dep = gather_start(l + 1, weights[l][0])
            z3, h3, yg, states, *kept, x_next = _layer_fwd(xs[-1].reshape(nb, seq, D_MODEL), None, norm_g[l][None],
                                                           weights[l][0], *layer_params(l), weights[l][2], dep)
            h = h3.reshape(n_tok, D_MODEL)
        xs.append(x_next.reshape(n_tok, D_MODEL))
        saved.append((z3, h, yg.reshape(n_tok, MIX), states, kept))

    dx, loss_part, d_final_g = _loss_head(xs[-1], loss_target.reshape(n_tok, D_MODEL), final_g[None])

    small = {k: [None] * depth for k in
             ("norm_g", "pool_w", "pool_scale", "lbr", "lbi", "wb", "wct", "d_skip", "glu_b")}
    received = [None] * depth
    sent = [None] * depth
    pending = None
    early = None
    for l in reversed(range(depth)):
        z3, h, yg2, states, kept = saved[l]
        dy, d_wout = _outproj_bwd(dx, yg2, weights[l][2], dep)
        (dz, d_pw, d_ps, d_lbr, d_lbi, d_wb, d_wct, d_dsk, d_gw, d_gb) = _mixer_bwd(
            z3, dy.reshape(nb, seq, MIX), states, kept, *layer_params(l))
        rest = (d_gw.reshape(N_DEV, SSM_W // N_DEV, SSM_W), d_wout.reshape(N_DEV, MIX // N_DEV, D_MODEL))
        if l == 0:
            early, dep = _exchange_start(rest, tuple(lax.empty(s.shape, s.dtype) for s in rest), dz,
                                         "comm_grads_start_0_rest")
        dx, d_win, d_ng = _inproj_bwd(dz.reshape(n_tok, 2 * MIX), h, xs[l], dx, norm_g[l][None], weights[l][0], dep)
        for k, val in (("norm_g", d_ng[0]), ("pool_w", d_pw), ("pool_scale", d_ps[0]), ("lbr", d_lbr),
                       ("lbi", d_lbi), ("wb", d_wb), ("wct", d_wct), ("d_skip", d_dsk[0]), ("glu_b", d_gb[0])):
            small[k][l] = val
        if pending is not None:
            sent[l + 1], received[l + 1] = _exchange_wait(pending, 3, dx, f"comm_grads_wait_{l + 1}")
        srcs = (d_win,) if l == 0 else (d_win,) + rest
        lands = tuple(lax.empty(s.shape, s.dtype) for s in srcs)
        pending, dep = _exchange_start(srcs, lands, dx, f"comm_grads_start_{l}")
    stack = lambda k: jnp.stack(small[k])
    d_rb, d_rc = chunk_vjp((stack("wb"), stack("wct")))
    local = [stack("norm_g"), stack("pool_w"), stack("pool_scale"), stack("lbr"), stack("lbi"), d_rb, d_rc,
             stack("d_skip"), stack("glu_b"), d_final_g[0] + dep[0, 0], loss_part[0]]
    (g_norm_g, g_pool_w, g_pool_scale, g_lbr, g_lbi, g_rb, g_rc, g_d_skip, g_glu_b, g_final_g, loss) = _unpack(
        _allreduce_packed(_pack(local)), local)
    loss = loss[0]
    g_a_re, g_a_im, g_log_dt, g_b_re, g_b_im, g_c_re, g_c_im = dense_vjp((g_lbr, g_lbi, g_rb, g_rc))

    names = ["norm_g", "pool_w", "pool_scale", "a_re", "a_im", "log_dt", "b_re", "b_im", "c_re", "c_im",
             "d_skip", "glu_b", "final_g"]
    rows = {"norm_g", "pool_scale", "log_dt", "d_skip", "glu_b"}
    small_w = [norm_g, pool_w, pool_scale, a_re, a_im, log_dt, b_re, b_im, c_re, c_im, d_skip, glu_b, final_g]
    small_g = [g_norm_g, g_pool_w, g_pool_scale, g_a_re, g_a_im, g_log_dt, g_b_re, g_b_im, g_c_re, g_c_im,
               g_d_skip, g_glu_b, g_final_g]
    small_m = [m_norm_g, m_pool_w, m_pool_scale, m_a_re, m_a_im, m_log_dt, m_b_re, m_b_im, m_c_re, m_c_im,
               m_d_skip, m_glu_b, m_final_g]
    small_v = [v_norm_g, v_pool_w, v_pool_scale, v_a_re, v_a_im, v_log_dt, v_b_re, v_b_im, v_c_re, v_c_im,
               v_d_skip, v_glu_b, v_final_g]

    wide_last = {"b_re", "b_im"}

    def blocked(arrays):
        return [a.reshape(1, 1, -1) if n == "final_g" else a[:, None, :] if n in rows
                else a.swapaxes(2, 3) if n in wide_last else a for n, a in zip(names, arrays)]

    small_d, small_nm, small_nv = _adamw_small(blocked(small_w), blocked(small_g), blocked(small_m), blocked(small_v))
    res = {}
    for kind, arrays in (("grad", small_g), ("delta", small_d), ("m", small_nm), ("v", small_nv)):
        for n, a, like in zip(names, arrays, small_w):
            if kind != "grad" and n in wide_last:
                a = a.swapaxes(2, 3)
            res[kind, n] = a.reshape(like.shape)

    (s_win,), (r_win,) = _exchange_wait(pending, 1, small_d[0], "comm_grads_wait_0")
    (s_glu, s_wout), (r_glu, r_wout) = _exchange_wait(early, 2, small_d[0], "comm_grads_wait_0_rest")
    sent[0], received[0] = (s_win, s_glu, s_wout), (r_win, r_glu, r_wout)
    shard_res = {}
    for pos, (n, w, m, v) in enumerate((("w_in", w_in, m_w_in, v_w_in), ("glu_w", glu_w, m_glu_w, v_glu_w),
                                        ("w_out", w_out, m_w_out, v_w_out))):
        shard_res[n] = _adamw_summed([received[l][pos] for l in range(depth)], [sent[l][pos] for l in range(depth)],
                                     my_idx, w, m, v, "adamw_" + n)
    for n in ("w_in", "glu_w", "w_out"):
        for pos, kind in enumerate(("grad", "delta", "m", "v")):
            res[kind, n] = shard_res[n][pos]

    order = ["norm_g", "w_in", "pool_w", "pool_scale", "a_re", "a_im", "log_dt", "b_re", "b_im", "c_re", "c_im",
             "d_skip", "glu_w", "glu_b", "w_out", "final_g"]
    outs = [loss, dx.reshape(nb, seq, D_MODEL)]
    for kind in ("grad", "delta", "m", "v"):
        outs += [res[kind, n] for n in order]
    return tuple(outs)
```

```python
import math

import jax
import jax.numpy as jnp
from jax import lax
from jax.experimental import pallas as pl
from jax.experimental.pallas import tpu as pltpu

F32 = jnp.float32
MXU_DTYPE = jnp.bfloat16

D_MODEL = 1024
MIX = 1024
POOL_W = 512
SSM_W = 512
N_POOL_G = 4
POOL_GC = 128
SSM_C = 16
SSM_P = 64
NORM_EPS = 1e-5
N_DEV = 8
W_IN_COLS = 2 * MIX // N_DEV

ADAM_LR = 0.001
ADAM_B1 = 0.9
ADAM_B2 = 0.999
ADAM_EPS = 1e-08
ADAM_WD = 0.01
ADAM_STEP = 10

SUBLANES = 8
LANES = 128
HALO = 16
STATE_ROWS = 8
STATE_COLS = 256
CHUNK_GROUPS = STATE_COLS // SSM_P
CHUNK_CH = CHUNK_GROUPS * SSM_C
T_BLK = 256
SCAN_UNROLL = 16
TM_FWD = 512
TM_BWD = 512
VMEM_LIMIT = 56 * 1024 * 1024

MESH = pl.DeviceIdType.MESH
VMEM_SPEC = pl.BlockSpec(memory_space=pltpu.VMEM)
ANY_SPEC = pl.BlockSpec(memory_space=pl.ANY)


def _mm(a, b):
    return jnp.dot(a, b, preferred_element_type=F32)


def _mm_tn(a, b):
    return lax.dot_general(a, b, (((0,), (0,)), ((), ())), preferred_element_type=F32)


def _mm_nt(a, b):
    return lax.dot_general(a, b, (((1,), (1,)), ((), ())), preferred_element_type=F32)


def _mx(a):
    return a.astype(MXU_DTYPE)


def _sigmoid(v):
    return 1.0 / (1.0 + jnp.exp(-v))


_GELU_C = math.sqrt(2.0 / math.pi)
_GELU_A = 0.044715


def _gelu_and_grad(y):
    th = jnp.tanh(_GELU_C * (y + _GELU_A * y * y * y))
    val = 0.5 * y * (1.0 + th)
    grad = 0.5 * (1.0 + th) + 0.5 * y * (1.0 - th * th) * (_GELU_C * (1.0 + 3.0 * _GELU_A * y * y))
    return val, grad


def _params(**kw):
    return pltpu.CompilerParams(vmem_limit_bytes=VMEM_LIMIT, **kw)


def _of_layer(layer, *shape):
    return pl.BlockSpec((None,) + shape, lambda i: (layer,) + (0,) * len(shape))


def _ssm_dense(a_re, a_im, log_dt, b_re, b_im, c_re, c_im):
    dt = jnp.exp(log_dt)[:, None]
    mag = jnp.exp(a_re * dt)
    ang = a_im * dt
    lb_re = mag * jnp.cos(ang)
    lb_im = mag * jnp.sin(ang)
    den = a_re * a_re + a_im * a_im
    n_re = lb_re - 1.0
    n_im = lb_im
    f_re = (n_re * a_re + n_im * a_im) / den
    f_im = (n_im * a_re - n_re * a_im) / den
    bb_re = f_re[..., None] * b_re - f_im[..., None] * b_im
    bb_im = f_re[..., None] * b_im + f_im[..., None] * b_re

    bb = jnp.stack([bb_re, bb_im], axis=0).reshape(2, STATE_ROWS, CHUNK_GROUPS, SSM_P, SSM_C)
    rb = bb.transpose(1, 4, 0, 2, 3).reshape(STATE_ROWS, SSM_C, 2 * STATE_COLS)
    cc = jnp.stack([c_re, -c_im], axis=0).reshape(2, STATE_ROWS, CHUNK_GROUPS, SSM_C, SSM_P)
    rc = cc.transpose(1, 3, 0, 2, 4).reshape(STATE_ROWS, SSM_C, 2 * STATE_COLS)
    return (lb_re.reshape(STATE_ROWS, STATE_COLS), lb_im.reshape(STATE_ROWS, STATE_COLS), rb, rc)


def _ssm_chunked(per_channel):
    row_group = jnp.arange(CHUNK_CH) // SSM_C
    col_group = (jnp.arange(2 * STATE_COLS) // SSM_P) % CHUNK_GROUPS
    own_group = (row_group[:, None] == col_group[None, :]).astype(F32)
    even = (jnp.arange(STATE_ROWS) % 2 == 0).astype(F32)[:, None, None]
    half = jnp.tile(per_channel, (1, CHUNK_GROUPS, 1)) * own_group
    return jnp.concatenate([half * even, half * (1.0 - even)], axis=1)


def _inproj_fwd(x2, g_rows, w_all, dep, layer):
    n = x2.shape[0]
    tm = TM_FWD

    def body(x_ref, g_ref, w_ref, dep_ref, z_ref, h_ref):
        x = x_ref[...]
        r = lax.rsqrt(jnp.mean(x * x, axis=-1, keepdims=True) + NORM_EPS)
        h = _mx(x * r * g_ref[...])
        h_ref[...] = h
        for d in range(N_DEV):
            z_ref[:, d * W_IN_COLS:(d + 1) * W_IN_COLS] = _mm(h, w_ref[d])

    return pl.pallas_call(
        body, name="inproj_fwd",
        grid=(n // tm,),
        in_specs=[pl.BlockSpec((tm, D_MODEL), lambda i: (i, 0)),
                  _of_layer(layer, 1, D_MODEL),
                  pl.BlockSpec((N_DEV, D_MODEL, W_IN_COLS), lambda i: (0, 0, 0)),
                  ANY_SPEC],
        out_specs=[pl.BlockSpec((tm, 2 * MIX), lambda i: (i, 0)),
                   pl.BlockSpec((tm, D_MODEL), lambda i: (i, 0))],
        out_shape=[jax.ShapeDtypeStruct((n, 2 * MIX), F32),
                   jax.ShapeDtypeStruct((n, D_MODEL), MXU_DTYPE)],
        compiler_params=_params(dimension_semantics=("arbitrary",)),
    )(x2, g_rows, w_all, dep)


def _loss_head(x2, tgt2, g_row):
    n = x2.shape[0]
    tm = TM_FWD

    def body(x_ref, t_ref, g_ref, dx_ref, loss_ref, dg_ref):
        @pl.when(pl.program_id(0) == 0)
        def _():
            loss_ref[...] = jnp.zeros_like(loss_ref)
            dg_ref[...] = jnp.zeros_like(dg_ref)

        x = x_ref[...]
        g = g_ref[...]
        r = lax.rsqrt(jnp.mean(x * x, axis=-1, keepdims=True) + NORM_EPS)
        xh = x * r
        e = xh * g - t_ref[...]
        loss_ref[...] += jnp.sum(jnp.sum(e * e, axis=-1, keepdims=True), axis=0, keepdims=True) * (0.5 / D_MODEL)
        dout = e * (1.0 / D_MODEL)
        dg_ref[...] += jnp.sum(dout * xh, axis=0, keepdims=True)
        gdy = dout * g
        dx_ref[...] = r * (gdy - xh * jnp.mean(xh * gdy, axis=-1, keepdims=True))

    return pl.pallas_call(
        body, name="loss_head",
        grid=(n // tm,),
        in_specs=[pl.BlockSpec((tm, D_MODEL), lambda i: (i, 0)),
                  pl.BlockSpec((tm, D_MODEL), lambda i: (i, 0)),
                  pl.BlockSpec((1, D_MODEL), lambda i: (0, 0))],
        out_specs=[pl.BlockSpec((tm, D_MODEL), lambda i: (i, 0)),
                   pl.BlockSpec((1, 1), lambda i: (0, 0)),
                   pl.BlockSpec((1, D_MODEL), lambda i: (0, 0))],
        out_shape=[jax.ShapeDtypeStruct((n, D_MODEL), F32),
                   jax.ShapeDtypeStruct((1, 1), F32),
                   jax.ShapeDtypeStruct((1, D_MODEL), F32)],
        compiler_params=_params(dimension_semantics=("arbitrary",)),
    )(x2, tgt2, g_row)


def _outproj_bwd(dx2, yg, w_out, dep):
    n = dx2.shape[0]
    tm = TM_BWD
    n_steps = n // tm

    def body(dx_ref, y_ref, w_ref, dep_ref, dy_ref, dw_ref, acc_ref):
        i = pl.program_id(0)

        @pl.when(i == 0)
        def _():
            acc_ref[...] = jnp.zeros_like(acc_ref)

        dxb = _mx(dx_ref[...])
        dy_ref[...] = _mm_nt(dxb, w_ref[...])
        acc_ref[...] += _mm_tn(y_ref[...], dxb)

        @pl.when(i == n_steps - 1)
        def _():
            dw_ref[...] = _mx(acc_ref[...])

    return pl.pallas_call(
        body, name="outproj_bwd",
        grid=(n_steps,),
        in_specs=[pl.BlockSpec((tm, D_MODEL), lambda i: (i, 0)),
                  pl.BlockSpec((tm, MIX), lambda i: (i, 0)),
                  pl.BlockSpec((MIX, D_MODEL), lambda i: (0, 0)),
                  ANY_SPEC],
        out_specs=[pl.BlockSpec((tm, MIX), lambda i: (i, 0)),
                   pl.BlockSpec((MIX, D_MODEL), lambda i: (0, 0))],
        out_shape=[jax.ShapeDtypeStruct((n, MIX), F32),
                   jax.ShapeDtypeStruct((MIX, D_MODEL), MXU_DTYPE)],
        scratch_shapes=[pltpu.VMEM((MIX, D_MODEL), F32)],
        compiler_params=_params(dimension_semantics=("arbitrary",)),
    )(dx2, yg, w_out, dep)


def _inproj_bwd(dz, h, x2, dx_in, g_rows, w_all, dep, layer):
    n = x2.shape[0]
    tm = TM_BWD
    n_steps = n // tm

    def body(dz_ref, h_ref, x_ref, dxi_ref, g_ref, w_ref, dep_ref, dxo_ref, dw_ref, dg_ref, acc_ref, wcat_ref):
        i = pl.program_id(0)

        @pl.when(i == 0)
        def _():
            acc_ref[...] = jnp.zeros_like(acc_ref)
            dg_ref[...] = jnp.zeros_like(dg_ref)
            for d in range(N_DEV):
                wcat_ref[:, d * W_IN_COLS:(d + 1) * W_IN_COLS] = w_ref[d]

        hb = h_ref[...]
        for d in range(N_DEV):
            acc_ref[d] += _mm_tn(hb, dz_ref[:, d * W_IN_COLS:(d + 1) * W_IN_COLS])
        dh = _mm_nt(dz_ref[...], wcat_ref[...])
        x = x_ref[...]
        r = lax.rsqrt(jnp.mean(x * x, axis=-1, keepdims=True) + NORM_EPS)
        xh = x * r
        dg_ref[...] += jnp.sum(dh * xh, axis=0, keepdims=True)
        gdy = dh * g_ref[...]
        dxo_ref[...] = dxi_ref[...] + r * (gdy - xh * jnp.mean(xh * gdy, axis=-1, keepdims=True))

        @pl.when(i == n_steps - 1)
        def _():
            dw_ref[...] = _mx(acc_ref[...])

    return pl.pallas_call(
        body, name="inproj_bwd",
        grid=(n_steps,),
        in_specs=[pl.BlockSpec((tm, 2 * MIX), lambda i: (i, 0)),
                  pl.BlockSpec((tm, D_MODEL), lambda i: (i, 0)),
                  pl.BlockSpec((tm, D_MODEL), lambda i: (i, 0)),
                  pl.BlockSpec((tm, D_MODEL), lambda i: (i, 0)),
                  _of_layer(layer, 1, D_MODEL),
                  pl.BlockSpec((N_DEV, D_MODEL, W_IN_COLS), lambda i: (0, 0, 0)),
                  ANY_SPEC],
        out_specs=[pl.BlockSpec((tm, D_MODEL), lambda i: (i, 0)),
                   pl.BlockSpec((N_DEV, D_MODEL, W_IN_COLS), lambda i: (0, 0, 0)),
                   pl.BlockSpec((1, D_MODEL), lambda i: (0, 0))],
        out_shape=[jax.ShapeDtypeStruct((n, D_MODEL), F32),
                   jax.ShapeDtypeStruct((N_DEV, D_MODEL, W_IN_COLS), MXU_DTYPE),
                   jax.ShapeDtypeStruct((1, D_MODEL), F32)],
        scratch_shapes=[pltpu.VMEM((N_DEV, D_MODEL, W_IN_COLS), F32),
                        pltpu.VMEM((D_MODEL, 2 * MIX), MXU_DTYPE)],
        compiler_params=_params(dimension_semantics=("arbitrary",)),
    )(dz, h, x2, dx_in, g_rows, w_all, dep)


def _row_pos(t0, rows):
    return t0 + lax.broadcasted_iota(jnp.int32, (rows, LANES), 0)


def _pool_window_mean(upad, g, t0, t_blk):
    k = 2 << g
    w = upad
    sh = 1
    while sh < k:
        w = w + pltpu.roll(w, sh, 0)
        sh *= 2
    count = jnp.minimum(_row_pos(t0, t_blk) + 1, k).astype(F32)
    return w[HALO:] / count - upad[HALO:]


def _pool_window_bwd(qpad, g, t_blk):
    k = 2 << g
    n = t_blk + HALO
    w = qpad
    sh = 1
    while sh < k:
        w = w + pltpu.roll(w, n - sh, 0)
        sh *= 2
    return w[:t_blk]


class _StateBuf:
    def __init__(self, refs, t_blk):
        self.refs = refs
        self.t_blk = t_blk

    def put_chunk(self, b, j, val):
        for c in range(4):
            self.refs[4 * b + c][pl.ds(j, self.t_blk, stride=STATE_ROWS), :] = val[:, c * LANES:(c + 1) * LANES]

    def get_chunk(self, b, j):
        return jnp.concatenate(
            [self.refs[4 * b + c][pl.ds(j, self.t_blk, stride=STATE_ROWS), :] for c in range(4)], axis=-1)

    def load(self, b, r, part):
        return jnp.concatenate(
            [self.refs[4 * b + 2 * part + h][pl.ds(r, STATE_ROWS), :] for h in range(2)], axis=-1)

    def store(self, b, r, part, val):
        for h in range(2):
            self.refs[4 * b + 2 * part + h][pl.ds(r, STATE_ROWS), :] = val[:, h * LANES:(h + 1) * LANES]


def _state_scratch(nb, t_blk):
    return [pltpu.VMEM((t_blk * STATE_ROWS, LANES), F32) for _ in range(4 * nb)]


def _ssm_project_in(u_ssm, wb_ref, buf, nb):
    t_blk = u_ssm.shape[0] // nb
    ub = _mx(u_ssm)
    for j in range(STATE_ROWS):
        m = j // 2
        bu = _mm(ub[:, m * LANES:(m + 1) * LANES], wb_ref[j])
        for b in range(nb):
            buf.put_chunk(b, j, bu[b * t_blk:(b + 1) * t_blk])


def _scan_forward(buf, lbr, lbi, init, nb):
    def step(t, carry):
        r = pl.multiple_of(t * STATE_ROWS, STATE_ROWS)
        out = []
        for b in range(nb):
            sr, si = carry[2 * b], carry[2 * b + 1]
            nr = lbr * sr - lbi * si + buf.load(b, r, 0)
            ni = lbr * si + lbi * sr + buf.load(b, r, 1)
            buf.store(b, r, 0, nr)
            buf.store(b, r, 1, ni)
            out += [nr, ni]
        return tuple(out)

    def body(i, carry):
        for u in range(SCAN_UNROLL):
            carry = step(i * SCAN_UNROLL + u, carry)
        return carry

    return lax.fori_loop(0, buf.t_blk // SCAN_UNROLL, body, init)


def _ssm_project_out(chunk, wc_ref):
    tiles = []
    for m in range(4):
        acc = None
        for j in (2 * m, 2 * m + 1):
            part = _mm_nt(chunk(j), wc_ref[j])
            acc = part if acc is None else acc + part
        tiles.append(acc)
    return jnp.concatenate(tiles, axis=-1)


def _layer_fwd(x3, z3, g_rows, w_in, pool_w, pool_scale, lbr, lbi, wb, wc, d_skip, glu_w, glu_b, w_out, dep, layer):
    nb, seq, _ = x3.shape
    t_blk = min(T_BLK, seq)
    n_t = seq // t_blk
    halo_per_blk = t_blk // HALO
    rows = nb * t_blk
    fused = z3 is None

    def body(*refs):
        if fused:
            (x_ref, g_ref, wi_ref, pw_ref, ps_ref, lbr_ref, lbi_ref, wb_ref, wc_ref, dsk_ref, gw_ref, gb_ref, wo_ref,
             dep_ref, z_ref, h_ref, yg_ref, sc_ref, act_ref, dact_ref, pooled_ref, ypre_ref, xo_ref,
             carry_ref, halo_ref, *s_refs) = refs
        else:
            (x_ref, z_ref, zh_ref, pw_ref, ps_ref, lbr_ref, lbi_ref, wb_ref, wc_ref, dsk_ref, gw_ref, gb_ref, wo_ref,
             dep_ref, yg_ref, sc_ref, act_ref, dact_ref, pooled_ref, ypre_ref, xo_ref, carry_ref, *s_refs) = refs
        i = pl.program_id(0)
        t0 = i * t_blk
        buf = _StateBuf(s_refs, t_blk)
        both = lambda lo, hi: z_ref[:, :, lo:hi].reshape(rows, hi - lo)

        @pl.when(i == 0)
        def _():
            carry_ref[...] = jnp.zeros_like(carry_ref)
            if fused:
                halo_ref[...] = jnp.zeros_like(halo_ref)

        x = x_ref[...].reshape(rows, D_MODEL)
        if fused:
            r = lax.rsqrt(jnp.mean(x * x, axis=-1, keepdims=True) + NORM_EPS)
            h = _mx(x * r * g_ref[...])
            h_ref[...] = h.reshape(nb, t_blk, D_MODEL)
            for d in range(N_DEV):
                z_ref[:, :, d * W_IN_COLS:(d + 1) * W_IN_COLS] = _mm(h, wi_ref[d]).reshape(nb, t_blk, W_IN_COLS)

        u_ssm = both(POOL_W, MIX)
        _ssm_project_in(u_ssm, wb_ref, buf, nb)
        init = tuple(carry_ref[b, :, h * STATE_COLS:(h + 1) * STATE_COLS] for b in range(nb) for h in range(2))
        fin = _scan_forward(buf, lbr_ref[...], lbi_ref[...], init, nb)
        for b in range(nb):
            carry_ref[b, :, 0:STATE_COLS] = fin[2 * b]
            carry_ref[b, :, STATE_COLS:2 * STATE_COLS] = fin[2 * b + 1]

        def chunk(j):
            states = _mx(jnp.concatenate([buf.get_chunk(b, j) for b in range(nb)], axis=0))
            sc_ref[:, j] = states.reshape(nb, t_blk, 2 * STATE_COLS)
            return states

        y = _ssm_project_out(chunk, wc_ref) + dsk_ref[...] * u_ssm
        yg, dgelu = _gelu_and_grad(y)
        ygb = _mx(yg)
        act_ref[...] = ygb.reshape(nb, t_blk, SSM_W)
        dact_ref[...] = _mx(dgelu).reshape(nb, t_blk, SSM_W)
        o_ssm = yg * _sigmoid(_mm(ygb, gw_ref[...]) + gb_ref[...])
        gp = both(MIX + POOL_W, 2 * MIX)
        parts = []
        first = (i == 0)
        for g in range(N_POOL_G):
            cols = slice(g * POOL_GC, (g + 1) * POOL_GC)
            pooled = []
            for b in range(nb):
                halo = halo_ref[b, :, cols] if fused else jnp.where(first, 0.0, zh_ref[b, :, cols])
                pooled.append(_pool_window_mean(jnp.concatenate([halo, z_ref[b, :, cols]], axis=0), g, t0, t_blk))
            pb = _mx(jnp.concatenate(pooled, axis=0))
            ypre = _mm(pb, pw_ref[g])
            pooled_ref[:, :, cols] = pb.reshape(nb, t_blk, POOL_GC)
            ypre_ref[:, :, cols] = ypre.reshape(nb, t_blk, POOL_GC)
            gpp = both(MIX + g * POOL_GC, MIX + (g + 1) * POOL_GC)
            parts.append(_mx(ypre * ps_ref[:, cols] * (gpp * _sigmoid(gpp))))
        parts.append(_mx(o_ssm * (gp * _sigmoid(gp))))
        gated = jnp.concatenate(parts, axis=-1)
        yg_ref[...] = gated.reshape(nb, t_blk, MIX)
        xo_ref[...] = (x + _mm(gated, wo_ref[...])).reshape(nb, t_blk, D_MODEL)
        if fused:
            halo_ref[...] = z_ref[:, t_blk - HALO:, 0:POOL_W]

    const = lambda *shape: pl.BlockSpec(shape, lambda i: (0,) * len(shape))
    tokens = lambda width: pl.BlockSpec((nb, t_blk, width), lambda i: (0, i, 0))
    mixer_specs = [_of_layer(layer, N_POOL_G, POOL_GC, POOL_GC), _of_layer(layer, 1, POOL_W),
                   _of_layer(layer, STATE_ROWS, STATE_COLS), _of_layer(layer, STATE_ROWS, STATE_COLS),
                   _of_layer(layer, STATE_ROWS, LANES, 2 * STATE_COLS),
                   _of_layer(layer, STATE_ROWS, LANES, 2 * STATE_COLS),
                   _of_layer(layer, 1, SSM_W), const(SSM_W, SSM_W), _of_layer(layer, 1, SSM_W),
                   const(MIX, D_MODEL), ANY_SPEC]
    mixer_args = (pool_w, pool_scale, lbr, lbi, wb, wc, d_skip, glu_w, glu_b, w_out, dep)
    out_specs = [tokens(MIX), pl.BlockSpec((nb, STATE_ROWS, t_blk, 2 * STATE_COLS), lambda i: (0, 0, i, 0)),
                 tokens(SSM_W), tokens(SSM_W), tokens(POOL_W), tokens(POOL_W), tokens(D_MODEL)]
    out_shape = [jax.ShapeDtypeStruct((nb, seq, MIX), MXU_DTYPE),
                 jax.ShapeDtypeStruct((nb, STATE_ROWS, seq, 2 * STATE_COLS), MXU_DTYPE),
                 jax.ShapeDtypeStruct((nb, seq, SSM_W), MXU_DTYPE),
                 jax.ShapeDtypeStruct((nb, seq, SSM_W), MXU_DTYPE),
                 jax.ShapeDtypeStruct((nb, seq, POOL_W), MXU_DTYPE),
                 jax.ShapeDtypeStruct((nb, seq, POOL_W), F32),
                 jax.ShapeDtypeStruct((nb, seq, D_MODEL), F32)]
    scratch = [pltpu.VMEM((nb, STATE_ROWS, 2 * STATE_COLS), F32)]
    if fused:
        in_specs = [tokens(D_MODEL), _of_layer(layer, 1, D_MODEL), const(N_DEV, D_MODEL, W_IN_COLS)] + mixer_specs
        args = (x3, g_rows, w_in) + mixer_args
        out_specs = [tokens(2 * MIX), tokens(D_MODEL)] + out_specs
        out_shape = [jax.ShapeDtypeStruct((nb, seq, 2 * MIX), F32),
                     jax.ShapeDtypeStruct((nb, seq, D_MODEL), MXU_DTYPE)] + out_shape
        scratch = scratch + [pltpu.VMEM((nb, HALO, POOL_W), F32)]
    else:
        in_specs = [tokens(D_MODEL), tokens(2 * MIX),
                    pl.BlockSpec((nb, HALO, POOL_W), lambda i: (0, jnp.maximum(i * halo_per_blk - 1, 0), 0))] + mixer_specs
        args = (x3, z3, z3) + mixer_args
    return pl.pallas_call(
        body, name="layer_fwd" if fused else "mixer_fwd",
        grid=(n_t,),
        in_specs=in_specs, out_specs=out_specs, out_shape=out_shape,
        scratch_shapes=scratch + _state_scratch(nb, t_blk),
        compiler_params=_params(dimension_semantics=("arbitrary",)),
    )(*args)


def _mixer_bwd(z3, dy3, states, kept, pool_w, pool_scale, lbr, lbi, wb, wc, d_skip, glu_w, glu_b, layer):
    nb, seq, _ = z3.shape
    t_blk = min(T_BLK, seq)
    n_t = seq // t_blk
    halo_per_blk = t_blk // HALO
    rows = nb * t_blk

    def body(z_ref, dy_ref, sc_ref, sch_ref, act_ref, dact_ref, pooled_ref, ypre_ref, pw_ref, ps_ref, lbr_ref, lbi_ref, wb_ref, wc_ref, dsk_ref,
             gw_ref, gb_ref,
             dz_ref, dpw_ref, dps_ref, dlbr_ref, dlbi_ref, dwb_ref, dwc_ref, ddsk_ref, dgw_ref, dgb_ref,
             gcarry_ref, qcarry_ref, du_ref, dgw_acc, *g_refs):
        i = pl.program_id(0)
        blk = n_t - 1 - i
        t0 = blk * t_blk
        gbuf = _StateBuf(g_refs, t_blk)

        @pl.when(i == 0)
        def _():
            gcarry_ref[...] = jnp.zeros_like(gcarry_ref)
            qcarry_ref[...] = jnp.zeros_like(qcarry_ref)
            for ref in (dpw_ref, dps_ref, dlbr_ref, dlbi_ref, dwb_ref, dwc_ref, ddsk_ref, dgw_acc, dgb_ref):
                ref[...] = jnp.zeros_like(ref)

        lbr_v = lbr_ref[...]
        lbi_v = lbi_ref[...]

        both = lambda ref, lo, hi: ref[:, :, lo:hi].reshape(rows, hi - lo)
        split = lambda val: val.reshape(nb, t_blk, val.shape[-1])
        states = lambda j: sc_ref[:, j].reshape(rows, 2 * STATE_COLS)
        first = (blk == 0)

        u_ssm = both(z_ref, POOL_W, MIX)
        ygb = act_ref[...].reshape(rows, SSM_W)
        yg = ygb.astype(F32)
        dgelu = dact_ref[...].reshape(rows, SSM_W).astype(F32)
        sg = _sigmoid(_mm(ygb, gw_ref[...]) + gb_ref[...])
        o_ssm = yg * sg
        gp = both(z_ref, MIX + POOL_W, 2 * MIX)
        sgm = _sigmoid(gp)
        dyv = both(dy_ref, POOL_W, MIX)
        dz_ref[:, :, MIX + POOL_W:2 * MIX] = split(_mx(dyv * o_ssm * (sgm * (1.0 + gp * (1.0 - sgm)))))
        do = dyv * (gp * sgm)
        dv = do * yg * (sg * (1.0 - sg))
        dvb = _mx(dv)
        dgb_ref[...] += jnp.sum(dv, axis=0, keepdims=True)
        dgw_acc[...] += _mm_tn(ygb, dvb)
        dyp = (do * sg + _mm_nt(dvb, gw_ref[...])) * dgelu
        ddsk_ref[...] += jnp.sum(dyp * u_ssm, axis=0, keepdims=True)
        dypb = _mx(dyp)
        for j in range(STATE_ROWS):
            m = j // 2
            dyt = dypb[:, m * LANES:(m + 1) * LANES]
            ds = _mm(dyt, wc_ref[j])
            for b in range(nb):
                gbuf.put_chunk(b, j, ds[b * t_blk:(b + 1) * t_blk])
            dwc_ref[j] += _mm_tn(dyt, states(j))
        du_ref[...] = split(dsk_ref[...] * dyp)

        for g in range(N_POOL_G):
            cols = slice(g * POOL_GC, (g + 1) * POOL_GC)
            pb = both(pooled_ref, g * POOL_GC, (g + 1) * POOL_GC)
            ypre = both(ypre_ref, g * POOL_GC, (g + 1) * POOL_GC)
            gpp = both(z_ref, MIX + g * POOL_GC, MIX + (g + 1) * POOL_GC)
            sgp = _sigmoid(gpp)
            dyg = both(dy_ref, g * POOL_GC, (g + 1) * POOL_GC)
            scale = ps_ref[:, cols]
            dz_ref[:, :, MIX + g * POOL_GC:MIX + (g + 1) * POOL_GC] = split(_mx(
                dyg * (ypre * scale) * (sgp * (1.0 + gpp * (1.0 - sgp)))))
            dyc = dyg * (gpp * sgp)
            dps_ref[:, cols] += jnp.sum(dyc * ypre, axis=0, keepdims=True)
            dypre = _mx(dyc * scale)
            dpw_ref[g] += _mm_tn(pb, dypre)
            dpooled = _mm_nt(dypre, pw_ref[g])
            count = jnp.minimum(_row_pos(t0, t_blk) + 1, 2 << g).astype(F32)
            for b in range(nb):
                dp = dpooled[b * t_blk:(b + 1) * t_blk]
                q = dp / count
                qpad = jnp.concatenate([q, qcarry_ref[b, :, cols]], axis=0)
                qcarry_ref[b, :, cols] = q[:HALO]
                dz_ref[b, :, cols] = _mx(_pool_window_bwd(qpad, g, t_blk) - dp)

        def rev_step(t, carry):
            r = pl.multiple_of(t * STATE_ROWS, STATE_ROWS)
            out = []
            for b in range(nb):
                gr, gi = carry[2 * b], carry[2 * b + 1]
                ngr = lbr_v * gr + lbi_v * gi + gbuf.load(b, r, 0)
                ngi = lbr_v * gi - lbi_v * gr + gbuf.load(b, r, 1)
                gbuf.store(b, r, 0, ngr)
                gbuf.store(b, r, 1, ngi)
                out += [ngr, ngi]
            return tuple(out)

        def rev_body(i, carry):
            for u in range(SCAN_UNROLL):
                carry = rev_step(t_blk - 1 - (i * SCAN_UNROLL + u), carry)
            return carry

        init_g = tuple(gcarry_ref[b, :, h * STATE_COLS:(h + 1) * STATE_COLS] for b in range(nb) for h in range(2))
        fin = lax.fori_loop(0, t_blk // SCAN_UNROLL, rev_body, init_g)
        for b in range(nb):
            gcarry_ref[b, :, 0:STATE_COLS] = fin[2 * b]
            gcarry_ref[b, :, STATE_COLS:2 * STATE_COLS] = fin[2 * b + 1]

        ub = _mx(u_ssm)
        for m in range(4):
            acc = both(du_ref, m * LANES, (m + 1) * LANES)
            for j in (2 * m, 2 * m + 1):
                g = jnp.concatenate([gbuf.get_chunk(b, j) for b in range(nb)], axis=0)
                gj = _mx(g)
                acc = acc + _mm_nt(gj, wb_ref[j])
                dwb_ref[j] += _mm_tn(ub[:, m * LANES:(m + 1) * LANES], gj)
                shifted = []
                for b in range(nb):
                    before = jnp.where(first, 0.0, sch_ref[b, j].astype(F32))
                    spad = jnp.concatenate([before, sc_ref[b, j].astype(F32)], axis=0)
                    shifted.append(pltpu.roll(spad, 1, 0)[HALO:])
                s_prev = jnp.concatenate(shifted, axis=0)
                g_re, g_im = g[:, :STATE_COLS], g[:, STATE_COLS:]
                p_re, p_im = s_prev[:, :STATE_COLS], s_prev[:, STATE_COLS:]
                dlbr_ref[j:j + 1, :] += jnp.sum(g_re * p_re + g_im * p_im, axis=0, keepdims=True)
                dlbi_ref[j:j + 1, :] += jnp.sum(g_im * p_re - g_re * p_im, axis=0, keepdims=True)
            dz_ref[:, :, POOL_W + m * LANES:POOL_W + (m + 1) * LANES] = split(_mx(acc))

        @pl.when(i == n_t - 1)
        def _():
            dgw_ref[...] = _mx(dgw_acc[...])

    const = lambda *shape: pl.BlockSpec(shape, lambda i: (0,) * len(shape))
    rev = lambda i: n_t - 1 - i
    out_shape = [jax.ShapeDtypeStruct((nb, seq, 2 * MIX), MXU_DTYPE),
                 jax.ShapeDtypeStruct((N_POOL_G, POOL_GC, POOL_GC), F32),
                 jax.ShapeDtypeStruct((1, POOL_W), F32),
                 jax.ShapeDtypeStruct((STATE_ROWS, STATE_COLS), F32),
                 jax.ShapeDtypeStruct((STATE_ROWS, STATE_COLS), F32),
                 jax.ShapeDtypeStruct((STATE_ROWS, LANES, 2 * STATE_COLS), F32),
                 jax.ShapeDtypeStruct((STATE_ROWS, LANES, 2 * STATE_COLS), F32),
                 jax.ShapeDtypeStruct((1, SSM_W), F32),
                 jax.ShapeDtypeStruct((SSM_W, SSM_W), MXU_DTYPE),
                 jax.ShapeDtypeStruct((1, SSM_W), F32)]
    return pl.pallas_call(
        body, name="mixer_bwd",
        grid=(n_t,),
        in_specs=[pl.BlockSpec((nb, t_blk, 2 * MIX), lambda i: (0, rev(i), 0)),
                  pl.BlockSpec((nb, t_blk, MIX), lambda i: (0, rev(i), 0)),
                  pl.BlockSpec((nb, STATE_ROWS, t_blk, 2 * STATE_COLS), lambda i: (0, 0, rev(i), 0)),
                  pl.BlockSpec((nb, STATE_ROWS, HALO, 2 * STATE_COLS),
                               lambda i: (0, 0, jnp.maximum(rev(i) * halo_per_blk - 1, 0), 0)),
                  pl.BlockSpec((nb, t_blk, SSM_W), lambda i: (0, rev(i), 0)),
                  pl.BlockSpec((nb, t_blk, SSM_W), lambda i: (0, rev(i), 0)),
                  pl.BlockSpec((nb, t_blk, POOL_W), lambda i: (0, rev(i), 0)),
                  pl.BlockSpec((nb, t_blk, POOL_W), lambda i: (0, rev(i), 0)),
                  _of_layer(layer, N_POOL_G, POOL_GC, POOL_GC), _of_layer(layer, 1, POOL_W),
                  _of_layer(layer, STATE_ROWS, STATE_COLS), _of_layer(layer, STATE_ROWS, STATE_COLS),
                  _of_layer(layer, STATE_ROWS, LANES, 2 * STATE_COLS),
                  _of_layer(layer, STATE_ROWS, LANES, 2 * STATE_COLS),
                  _of_layer(layer, 1, SSM_W), const(SSM_W, SSM_W), _of_layer(layer, 1, SSM_W)],
        out_specs=[pl.BlockSpec((nb, t_blk, 2 * MIX), lambda i: (0, rev(i), 0))]
                  + [const(*s.shape) for s in out_shape[1:]],
        out_shape=out_shape,
        scratch_shapes=[pltpu.VMEM((nb, STATE_ROWS, 2 * STATE_COLS), F32),
                        pltpu.VMEM((nb, HALO, POOL_W), F32),
                        pltpu.VMEM((nb, t_blk, SSM_W), F32),
                        pltpu.VMEM((SSM_W, SSM_W), F32)]
                       + _state_scratch(nb, t_blk),
        compiler_params=_params(dimension_semantics=("arbitrary",)),
    )(z3, dy3, states, states, *kept, pool_w, pool_scale, lbr, lbi, wb, wc, d_skip, glu_w, glu_b)


def _mesh_place():
    x, y, c = lax.axis_index("x"), lax.axis_index("y"), lax.axis_index("c")
    return x, y, c


def _flip(place, k):
    x, y, c = place
    return (1 - x if k & 4 else x, 1 - y if k & 2 else y, 1 - c if k & 1 else c)


def _index(place):
    x, y, c = place
    return 4 * x + 2 * y + c


HBM_SPEC = pl.BlockSpec(memory_space=pltpu.HBM)
SEM_SPEC = pl.BlockSpec(memory_space=pltpu.SEMAPHORE)
_EFFECT = pltpu.SideEffectType.DATAFLOW_SIDE_EFFECTING
N_PEERS = N_DEV - 1


def _exchange_copies(src_refs, land_refs, send_sems, recv_sems):
    me = _mesh_place()
    mine = _index(me)
    out = []
    for a, land_ref in enumerate(land_refs):
        for k in range(1, N_DEV):
            peer = _flip(me, k)
            theirs = _index(peer)
            n = a * N_PEERS + k - 1
            src = src_refs[a].at[theirs] if src_refs else land_ref.at[mine]
            send = pltpu.make_async_remote_copy(
                src_ref=src, dst_ref=land_ref.at[mine], send_sem=send_sems.at[n], recv_sem=recv_sems.at[n],
                device_id=peer, device_id_type=MESH)
            recv = pltpu.make_async_remote_copy(
                src_ref=src, dst_ref=land_ref.at[theirs], send_sem=send_sems.at[n], recv_sem=recv_sems.at[n],
                device_id=peer, device_id_type=MESH)
            out.append((send, recv))
    return out


def _exchange_start(srcs, lands, after, name):
    arrays = tuple(srcs) + tuple(lands)
    n_src, n_all = len(srcs), len(arrays)
    n_copies = len(lands) * N_PEERS

    def body(*refs):
        send_sems, recv_sems = refs[n_all + 1], refs[n_all + 2]
        token = refs[-1]
        for send, _ in _exchange_copies(refs[:n_src], refs[n_src:n_all], send_sems, recv_sems):
            send.start()
        token[...] = jnp.zeros_like(token)

    res = pl.pallas_call(
        body, name=name,
        in_specs=[HBM_SPEC] * n_all + [ANY_SPEC],
        out_specs=[SEM_SPEC, SEM_SPEC] + [HBM_SPEC] * n_all + [VMEM_SPEC],
        out_shape=[pltpu.SemaphoreType.DMA((n_copies,)), pltpu.SemaphoreType.DMA((n_copies,))]
                  + [pltpu.HBM(a.shape, a.dtype) for a in arrays] + [jax.ShapeDtypeStruct((SUBLANES, LANES), F32)],
        input_output_aliases={i: 2 + i for i in range(n_all)},
        compiler_params=pltpu.CompilerParams(has_side_effects=_EFFECT),
    )(*[pltpu.with_memory_space_constraint(a, pltpu.HBM) for a in arrays], after)
    return tuple(res[:-1]), res[-1]


def _exchange_wait(handle, n_lands, after, name):
    send_sems, recv_sems = handle[0], handle[1]
    arrays = handle[2:]
    n_all = len(arrays)
    n_src = n_all - n_lands

    def body(*refs):
        for send, recv in _exchange_copies(refs[:n_src], refs[n_src:n_all], refs[n_all], refs[n_all + 1]):
            send.wait_send()
            recv.wait_recv()

    res = pl.pallas_call(
        body, name=name,
        in_specs=[HBM_SPEC] * n_all + [SEM_SPEC, SEM_SPEC, ANY_SPEC],
        out_specs=[HBM_SPEC] * n_all,
        out_shape=[pltpu.HBM(a.shape, a.dtype) for a in arrays],
        input_output_aliases={i: i for i in range(n_all)},
        compiler_params=pltpu.CompilerParams(has_side_effects=_EFFECT),
    )(*arrays, send_sems, recv_sems, after)
    return tuple(res[:n_src]), tuple(res[n_src:])


def _weight_zones(w_in, glu_w, w_out, my_idx):
    shards = (w_in, glu_w, w_out)
    depth = w_in.shape[0]

    def body(idx_ref, *refs):
        ins, zones = refs[:len(shards)], refs[len(shards):]
        for l in range(depth):
            for a, src in enumerate(ins):
                zones[l * len(shards) + a][0] = _mx(src[l])

    whole = lambda s: pl.BlockSpec(s.shape, lambda i, idx: (0,) * s.ndim)
    return pl.pallas_call(
        body, name="weight_zones",
        grid_spec=pltpu.PrefetchScalarGridSpec(
            num_scalar_prefetch=1, grid=(1,),
            in_specs=[whole(s) for s in shards],
            out_specs=[pl.BlockSpec((1,) + s.shape[1:], lambda i, idx: (idx[0], 0, 0))
                       for _ in range(depth) for s in shards]),
        out_shape=[jax.ShapeDtypeStruct((N_DEV,) + s.shape[1:], MXU_DTYPE) for _ in range(depth) for s in shards],
        compiler_params=_params(dimension_semantics=("arbitrary",)),
    )(my_idx.reshape(1).astype(jnp.int32), *shards)


def _allreduce_packed(p):
    rows = p.shape[0]
    half = rows // 2
    quarter = half // 4

    def body(p_ref, o_ref, part_ref, sib_ref, got_ref, send_sems, recv_sems):
        x, y, c = _mesh_place()
        sibling = (x, y, 1 - c)
        chip = 2 * x + y
        chips = [(k, (1 - x if k & 2 else x, 1 - y if k & 1 else y, c), chip ^ k) for k in (1, 2, 3)]
        my_half = pl.multiple_of(c * half, SUBLANES)
        other_half = pl.multiple_of((1 - c) * half, SUBLANES)

        def copy(n, src, dst, to):
            return pltpu.make_async_remote_copy(src_ref=src, dst_ref=dst, send_sem=send_sems.at[n],
                                                recv_sem=recv_sems.at[n], device_id=to, device_id_type=MESH)

        def quarter_of(ref, base, q):
            return ref.at[pl.ds(pl.multiple_of(base + q * quarter, SUBLANES), quarter)]

        swap = copy(0, p_ref.at[pl.ds(other_half, half)], sib_ref, sibling)
        swap.start()
        swap.wait()
        part_ref[...] = p_ref[pl.ds(my_half, half), :] + sib_ref[...]

        scatter = [copy(k, quarter_of(part_ref, 0, q), got_ref.at[k - 1], to) for k, to, q in chips]
        for cp in scatter:
            cp.start()
        total = part_ref[pl.ds(pl.multiple_of(chip * quarter, SUBLANES), quarter), :]
        for cp, (k, _, _) in zip(scatter, chips):
            cp.wait()
            total = total + got_ref[k - 1]
        mine = pl.multiple_of(my_half + chip * quarter, SUBLANES)
        o_ref[pl.ds(mine, quarter), :] = total

        gather = [copy(3 + k, o_ref.at[pl.ds(mine, quarter)], o_ref.at[pl.ds(mine, quarter)], to) for k, to, _ in chips]
        for cp in gather:
            cp.start()
        for k, to, q in chips:
            theirs = quarter_of(o_ref, my_half, q)
            copy(3 + k, theirs, theirs, to).wait_recv()
        for cp in gather:
            cp.wait_send()

        back = copy(7, o_ref.at[pl.ds(my_half, half)], o_ref.at[pl.ds(my_half, half)], sibling)
        back.start()
        copy(7, o_ref.at[pl.ds(other_half, half)], o_ref.at[pl.ds(other_half, half)], sibling).wait_recv()
        back.wait_send()

    return pl.pallas_call(
        body, name="comm_allreduce_packed",
        in_specs=[VMEM_SPEC],
        out_specs=VMEM_SPEC,
        out_shape=jax.ShapeDtypeStruct(p.shape, F32),
        scratch_shapes=[pltpu.VMEM((half, LANES), F32),
                        pltpu.VMEM((half, LANES), F32),
                        pltpu.VMEM((3, quarter, LANES), F32),
                        pltpu.SemaphoreType.DMA((8,)),
                        pltpu.SemaphoreType.DMA((8,))],
        compiler_params=_params(),
    )(p)


def _adamw_math(w, g, m, v):
    m = ADAM_B1 * m + (1.0 - ADAM_B1) * g
    v = ADAM_B2 * v + (1.0 - ADAM_B2) * (g * g)
    m_hat = m / (1.0 - ADAM_B1 ** ADAM_STEP)
    v_hat = v / (1.0 - ADAM_B2 ** ADAM_STEP)
    delta = -ADAM_LR * (m_hat / (jnp.sqrt(v_hat) + ADAM_EPS) + ADAM_WD * w)
    return delta, m, v


def _adamw_summed(received, own, my_idx, w, m, v, name):
    depth, r, c = w.shape
    tr = min(r, 128)

    def body(idx_ref, *refs):
        r_refs, o_refs = refs[:depth], refs[depth:2 * depth]
        w_ref, m_ref, v_ref, g_ref, d_ref, nm_ref, nv_ref = refs[2 * depth:]
        me = idx_ref[0]
        for l in range(depth):
            g = jnp.zeros((tr, c), F32)
            for q in range(N_DEV):
                g = g + jnp.where(q == me, o_refs[l][0], r_refs[l][q]).astype(F32)
            g_ref[l] = g
            d_ref[l], nm_ref[l], nv_ref[l] = _adamw_math(w_ref[l], g, m_ref[l], v_ref[l])

    blk = pl.BlockSpec((depth, tr, c), lambda i, idx: (0, i, 0))
    return pl.pallas_call(
        body, name=name,
        grid_spec=pltpu.PrefetchScalarGridSpec(
            num_scalar_prefetch=1, grid=(r // tr,),
            in_specs=[pl.BlockSpec((N_DEV, tr, c), lambda i, idx: (0, i, 0))] * depth
                     + [pl.BlockSpec((1, tr, c), lambda i, idx: (idx[0], i, 0))] * depth
                     + [blk, blk, blk],
            out_specs=[blk] * 4),
        out_shape=[jax.ShapeDtypeStruct((depth, r, c), F32)] * 4,
        compiler_params=_params(dimension_semantics=("arbitrary",)),
    )(my_idx.reshape(1).astype(jnp.int32), *received, *own, w, m, v)


def _adamw_small(ws, gs, ms, vs):
    n = len(ws)
    depth = ws[0].shape[0]

    def spec(a):
        per_layer = a.shape[0] == depth
        rest = (0,) * (a.ndim - 1)
        return pl.BlockSpec((1,) + a.shape[1:], lambda l: ((l if per_layer else 0),) + rest)

    def body(*refs):
        w_refs, g_refs, m_refs, v_refs = (refs[k * n:(k + 1) * n] for k in range(4))
        d_refs, nm_refs, nv_refs = (refs[(4 + k) * n:(5 + k) * n] for k in range(3))
        for k in range(n):
            d_refs[k][...], nm_refs[k][...], nv_refs[k][...] = _adamw_math(
                w_refs[k][...], g_refs[k][...], m_refs[k][...], v_refs[k][...])

    specs = [spec(a) for a in ws]
    shapes = [jax.ShapeDtypeStruct(a.shape, F32) for a in ws]
    res = pl.pallas_call(
        body, name="adamw_small",
        grid=(depth,),
        in_specs=specs * 4,
        out_specs=specs * 3,
        out_shape=shapes * 3,
        compiler_params=_params(dimension_semantics=("arbitrary",)),
    )(*ws, *gs, *ms, *vs)
    return res[:n], res[n:2 * n], res[2 * n:]


_PACK_ROWS = SUBLANES * N_DEV


def _pack(arrays):
    flat = jnp.concatenate([a.reshape(-1) for a in arrays])
    per = _PACK_ROWS * LANES
    total = -(-flat.shape[0] // per) * per
    flat = jnp.pad(flat, (0, total - flat.shape[0]))
    return flat.reshape(total // LANES, LANES)


def _unpack(packed, like):
    flat = packed.reshape(-1)
    out = []
    off = 0
    for a in like:
        out.append(flat[off:off + a.size].reshape(a.shape))
        off += a.size
    return out


def kernel(x, norm_g, w_in, pool_w, pool_scale, a_re, a_im, log_dt, b_re, b_im, c_re, c_im, d_skip, glu_w, glu_b, w_out, final_g, loss_target, m_norm_g, m_w_in, m_pool_w, m_pool_scale, m_a_re, m_a_im, m_log_dt, m_b_re, m_b_im, m_c_re, m_c_im, m_d_skip, m_glu_w, m_glu_b, m_w_out, m_final_g, v_norm_g, v_w_in, v_pool_w, v_pool_scale, v_a_re, v_a_im, v_log_dt, v_b_re, v_b_im, v_c_re, v_c_im, v_d_skip, v_glu_w, v_glu_b, v_w_out, v_final_g):
    nb, seq, _ = x.shape
    n_tok = nb * seq
    depth = norm_g.shape[0]

    my_idx = _index(_mesh_place())

    zones = _weight_zones(w_in, glu_w, w_out, my_idx)

    def gather_start(l, after):
        return _exchange_start((), zones[3 * l:3 * l + 3], after, f"comm_gather_start_{l}")

    def gather_wait(handle, after, l):
        _, (win, glu, wout) = _exchange_wait(handle, 3, after, f"comm_gather_wait_{l}")
        return win, glu.reshape(SSM_W, SSM_W), wout.reshape(MIX, D_MODEL)

    xs = [x.reshape(n_tok, D_MODEL)]
    first_w_in, dep = _exchange_start((), zones[0:1], xs[0], "comm_gather_start_0_w_in")

    (lbr, lbi, rb, rc), dense_vjp = jax.vjp(jax.vmap(_ssm_dense), a_re, a_im, log_dt + dep[0, 0], b_re, b_im, c_re, c_im)
    chunk_all = jax.vmap(_ssm_chunked)
    (wb, wct), chunk_vjp = jax.vjp(lambda p, q: (chunk_all(p), chunk_all(q)), rb, rc)
    wb_m, wct_m = _mx(wb), _mx(wct)
    pool_w_m = _mx(pool_w)
    rows_of = lambda a: a[:, None, :]
    norm_rows, scale_rows, skip_rows, bias_rows = rows_of(norm_g), rows_of(pool_scale), rows_of(d_skip), rows_of(glu_b)

    def layer_params(l):
        return (pool_w_m, scale_rows, lbr, lbi, wb_m, wct_m, skip_rows, weights[l][1], bias_rows)

    saved = []
    weights = []
    for l in range(depth):
        if l == 0:
            _, (win,) = _exchange_wait(first_w_in, 1, wct_m, "comm_gather_wait_0_w_in")
            rest, dep = _exchange_start((), zones[1:3], win, "comm_gather_start_0_rest")
            z, h = _inproj_fwd(xs[-1], norm_rows, win, dep, l)
            _, (glu, wout) = _exchange_wait(rest, 2, z, "comm_gather_wait_0_rest")
            weights.append((win, glu.reshape(SSM_W, SSM_W), wout.reshape(MIX, D_MODEL)))
            handle, dep = gather_start(1, weights[0][2])
            z3 = z.reshape(nb, seq, 2 * MIX)
            yg, states, *kept, x_next = _layer_fwd(xs[-1].reshape(nb, seq, D_MODEL), z3, None, None,
                                                   *layer_params(l), weights[l][2], dep, l)
        else:
            weights.append(gather_wait(handle, xs[-1], l))
            if l + 1 < depth:
                handle, dep = gather_start(l + 1, weights[l][0])
            z3, h3, yg, states, *kept, x_next = _layer_fwd(xs[-1].reshape(nb, seq, D_MODEL), None, norm_rows,
                                                           weights[l][0], *layer_params(l), weights[l][2], dep, l)
            h = h3.reshape(n_tok, D_MODEL)
        xs.append(x_next.reshape(n_tok, D_MODEL))
        saved.append((z3, h, yg.reshape(n_tok, MIX), states, kept))

    dx, loss_part, d_final_g = _loss_head(xs[-1], loss_target.reshape(n_tok, D_MODEL), final_g[None])

    small = {k: [None] * depth for k in
             ("norm_g", "pool_w", "pool_scale", "lbr", "lbi", "wb", "wct", "d_skip", "glu_b")}
    received = [None] * depth
    sent = [None] * depth
    pending = None
    early = None
    for l in reversed(range(depth)):
        z3, h, yg2, states, kept = saved[l]
        dy, d_wout = _outproj_bwd(dx, yg2, weights[l][2], dep)
        (dz, d_pw, d_ps, d_lbr, d_lbi, d_wb, d_wct, d_dsk, d_gw, d_gb) = _mixer_bwd(
            z3, dy.reshape(nb, seq, MIX), states, kept, *layer_params(l), l)
        rest = (d_gw.reshape(N_DEV, SSM_W // N_DEV, SSM_W), d_wout.reshape(N_DEV, MIX // N_DEV, D_MODEL))
        if l == 0:
            early, dep = _exchange_start(rest, tuple(lax.empty(s.shape, s.dtype) for s in rest), dz,
                                         "comm_grads_start_0_rest")
        dx, d_win, d_ng = _inproj_bwd(dz.reshape(n_tok, 2 * MIX), h, xs[l], dx, norm_rows, weights[l][0], dep, l)
        for k, val in (("norm_g", d_ng[0]), ("pool_w", d_pw), ("pool_scale", d_ps[0]), ("lbr", d_lbr),
                       ("lbi", d_lbi), ("wb", d_wb), ("wct", d_wct), ("d_skip", d_dsk[0]), ("glu_b", d_gb[0])):
            small[k][l] = val
        if pending is not None:
            sent[l + 1], received[l + 1] = _exchange_wait(pending, 3, dx, f"comm_grads_wait_{l + 1}")
        srcs = (d_win,) if l == 0 else (d_win,) + rest
        lands = tuple(lax.empty(s.shape, s.dtype) for s in srcs)
        pending, dep = _exchange_start(srcs, lands, dx, f"comm_grads_start_{l}")
    stack = lambda k: jnp.stack(small[k])
    d_rb, d_rc = chunk_vjp((stack("wb"), stack("wct")))
    local = [stack("norm_g"), stack("pool_w"), stack("pool_scale"), stack("lbr"), stack("lbi"), d_rb, d_rc,
             stack("d_skip"), stack("glu_b"), d_final_g[0] + dep[0, 0], loss_part[0]]
    (g_norm_g, g_pool_w, g_pool_scale, g_lbr, g_lbi, g_rb, g_rc, g_d_skip, g_glu_b, g_final_g, loss) = _unpack(
        _allreduce_packed(_pack(local)), local)
    loss = loss[0]
    g_a_re, g_a_im, g_log_dt, g_b_re, g_b_im, g_c_re, g_c_im = dense_vjp((g_lbr, g_lbi, g_rb, g_rc))

    names = ["norm_g", "pool_w", "pool_scale", "a_re", "a_im", "log_dt", "b_re", "b_im", "c_re", "c_im",
             "d_skip", "glu_b", "final_g"]
    rows = {"norm_g", "pool_scale", "log_dt", "d_skip", "glu_b"}
    small_w = [norm_g, pool_w, pool_scale, a_re, a_im, log_dt, b_re, b_im, c_re, c_im, d_skip, glu_b, final_g]
    small_g = [g_norm_g, g_pool_w, g_pool_scale, g_a_re, g_a_im, g_log_dt, g_b_re, g_b_im, g_c_re, g_c_im,
               g_d_skip, g_glu_b, g_final_g]
    small_m = [m_norm_g, m_pool_w, m_pool_scale, m_a_re, m_a_im, m_log_dt, m_b_re, m_b_im, m_c_re, m_c_im,
               m_d_skip, m_glu_b, m_final_g]
    small_v = [v_norm_g, v_pool_w, v_pool_scale, v_a_re, v_a_im, v_log_dt, v_b_re, v_b_im, v_c_re, v_c_im,
               v_d_skip, v_glu_b, v_final_g]

    wide_last = {"b_re", "b_im"}

    def blocked(arrays):
        return [a.reshape(1, 1, -1) if n == "final_g" else a[:, None, :] if n in rows
                else a.swapaxes(2, 3) if n in wide_last else a for n, a in zip(names, arrays)]

    small_d, small_nm, small_nv = _adamw_small(blocked(small_w), blocked(small_g), blocked(small_m), blocked(small_v))
    res = {}
    for kind, arrays in (("grad", small_g), ("delta", small_d), ("m", small_nm), ("v", small_nv)):
        for n, a, like in zip(names, arrays, small_w):
            if kind != "grad" and n in wide_last:
                a = a.swapaxes(2, 3)
            res[kind, n] = a.reshape(like.shape)

    (s_win,), (r_win,) = _exchange_wait(pending, 1, small_d[0], "comm_grads_wait_0")
    (s_glu, s_wout), (r_glu, r_wout) = _exchange_wait(early, 2, small_d[0], "comm_grads_wait_0_rest")
    sent[0], received[0] = (s_win, s_glu, s_wout), (r_win, r_glu, r_wout)
    shard_res = {}
    for pos, (n, w, m, v) in enumerate((("w_in", w_in, m_w_in, v_w_in), ("glu_w", glu_w, m_glu_w, v_glu_w),
                                        ("w_out", w_out, m_w_out, v_w_out))):
        shard_res[n] = _adamw_summed([received[l][pos] for l in range(depth)], [sent[l][pos] for l in range(depth)],
                                     my_idx, w, m, v, "adamw_" + n)
    for n in ("w_in", "glu_w", "w_out"):
        for pos, kind in enumerate(("grad", "delta", "m", "v")):
            res[kind, n] = shard_res[n][pos]

    order = ["norm_g", "w_in", "pool_w", "pool_scale", "a_re", "a_im", "log_dt", "b_re", "b_im", "c_re", "c_im",
             "d_skip", "glu_w", "glu_b", "w_out", "final_g"]
    outs = [loss, dx.reshape(nb, seq, D_MODEL)]
    for kind in ("grad", "delta", "m", "v"):
        outs += [res[kind, n] for n in order]
    return tuple(outs)
```

```python
import math

import jax
import jax.numpy as jnp
from jax import lax
from jax.experimental import pallas as pl
from jax.experimental.pallas import tpu as pltpu

F32 = jnp.float32
MXU_DTYPE = jnp.bfloat16

D_MODEL = 1024
MIX = 1024
POOL_W = 512
SSM_W = 512
N_POOL_G = 4
POOL_GC = 128
SSM_C = 16
SSM_P = 64
NORM_EPS = 1e-5
N_DEV = 8
W_IN_COLS = 2 * MIX // N_DEV

ADAM_LR = 0.001
ADAM_B1 = 0.9
ADAM_B2 = 0.999
ADAM_EPS = 1e-08
ADAM_WD = 0.01
ADAM_STEP = 10

SUBLANES = 8
LANES = 128
HALO = 16
STATE_ROWS = 8
STATE_COLS = 256
CHUNK_GROUPS = STATE_COLS // SSM_P
CHUNK_CH = CHUNK_GROUPS * SSM_C
T_BLK = 256
SCAN_UNROLL = 16
TM_FWD = 512
TM_BWD = 512
TM_LOSS = 1024
VMEM_LIMIT = 56 * 1024 * 1024

MESH = pl.DeviceIdType.MESH
VMEM_SPEC = pl.BlockSpec(memory_space=pltpu.VMEM)
ANY_SPEC = pl.BlockSpec(memory_space=pl.ANY)


def _mm(a, b):
    return jnp.dot(a, b, preferred_element_type=F32)


def _mm_tn(a, b):
    return lax.dot_general(a, b, (((0,), (0,)), ((), ())), preferred_element_type=F32)


def _mm_nt(a, b):
    return lax.dot_general(a, b, (((1,), (1,)), ((), ())), preferred_element_type=F32)


def _mx(a):
    return a.astype(MXU_DTYPE)


def _sigmoid(v):
    return 1.0 / (1.0 + jnp.exp(-v))


_GELU_C = math.sqrt(2.0 / math.pi)
_GELU_A = 0.044715


def _gelu_and_grad(y):
    th = jnp.tanh(_GELU_C * (y + _GELU_A * y * y * y))
    val = 0.5 * y * (1.0 + th)
    grad = 0.5 * (1.0 + th) + 0.5 * y * (1.0 - th * th) * (_GELU_C * (1.0 + 3.0 * _GELU_A * y * y))
    return val, grad


def _params(**kw):
    return pltpu.CompilerParams(vmem_limit_bytes=VMEM_LIMIT, **kw)


def _of_layer(layer, *shape):
    return pl.BlockSpec((None,) + shape, lambda i: (layer,) + (0,) * len(shape))


def _ssm_dense(a_re, a_im, log_dt, b_re, b_im, c_re, c_im):
    dt = jnp.exp(log_dt)[:, None]
    mag = jnp.exp(a_re * dt)
    ang = a_im * dt
    lb_re = mag * jnp.cos(ang)
    lb_im = mag * jnp.sin(ang)
    den = a_re * a_re + a_im * a_im
    n_re = lb_re - 1.0
    n_im = lb_im
    f_re = (n_re * a_re + n_im * a_im) / den
    f_im = (n_im * a_re - n_re * a_im) / den
    bb_re = f_re[..., None] * b_re - f_im[..., None] * b_im
    bb_im = f_re[..., None] * b_im + f_im[..., None] * b_re

    bb = jnp.stack([bb_re, bb_im], axis=0).reshape(2, STATE_ROWS, CHUNK_GROUPS, SSM_P, SSM_C)
    rb = bb.transpose(1, 4, 0, 2, 3).reshape(STATE_ROWS, SSM_C, 2 * STATE_COLS)
    cc = jnp.stack([c_re, -c_im], axis=0).reshape(2, STATE_ROWS, CHUNK_GROUPS, SSM_C, SSM_P)
    rc = cc.transpose(1, 3, 0, 2, 4).reshape(STATE_ROWS, SSM_C, 2 * STATE_COLS)
    return (lb_re.reshape(STATE_ROWS, STATE_COLS), lb_im.reshape(STATE_ROWS, STATE_COLS), rb, rc)


def _ssm_chunked(per_channel):
    row_group = jnp.arange(CHUNK_CH) // SSM_C
    col_group = (jnp.arange(2 * STATE_COLS) // SSM_P) % CHUNK_GROUPS
    own_group = (row_group[:, None] == col_group[None, :]).astype(F32)
    even = (jnp.arange(STATE_ROWS) % 2 == 0).astype(F32)[:, None, None]
    half = jnp.tile(per_channel, (1, CHUNK_GROUPS, 1)) * own_group
    return jnp.concatenate([half * even, half * (1.0 - even)], axis=1)


def _inproj_fwd(x2, g_rows, w_all, dep, layer):
    n = x2.shape[0]
    tm = TM_FWD

    def body(x_ref, g_ref, w_ref, dep_ref, z_ref, h_ref):
        x = x_ref[...]
        r = lax.rsqrt(jnp.mean(x * x, axis=-1, keepdims=True) + NORM_EPS)
        h = _mx(x * r * g_ref[...])
        h_ref[...] = h
        for d in range(N_DEV):
            z_ref[:, d * W_IN_COLS:(d + 1) * W_IN_COLS] = _mm(h, w_ref[d])

    return pl.pallas_call(
        body, name="inproj_fwd",
        grid=(n // tm,),
        in_specs=[pl.BlockSpec((tm, D_MODEL), lambda i: (i, 0)),
                  _of_layer(layer, 1, D_MODEL),
                  pl.BlockSpec((N_DEV, D_MODEL, W_IN_COLS), lambda i: (0, 0, 0)),
                  ANY_SPEC],
        out_specs=[pl.BlockSpec((tm, 2 * MIX), lambda i: (i, 0)),
                   pl.BlockSpec((tm, D_MODEL), lambda i: (i, 0))],
        out_shape=[jax.ShapeDtypeStruct((n, 2 * MIX), F32),
                   jax.ShapeDtypeStruct((n, D_MODEL), MXU_DTYPE)],
        compiler_params=_params(dimension_semantics=("arbitrary",)),
    )(x2, g_rows, w_all, dep)


def _loss_head(x2, tgt2, g_row):
    n = x2.shape[0]
    tm = TM_LOSS

    def body(x_ref, t_ref, g_ref, dx_ref, loss_ref, dg_ref):
        @pl.when(pl.program_id(0) == 0)
        def _():
            loss_ref[...] = jnp.zeros_like(loss_ref)
            dg_ref[...] = jnp.zeros_like(dg_ref)

        x = x_ref[...]
        g = g_ref[...]
        r = lax.rsqrt(jnp.mean(x * x, axis=-1, keepdims=True) + NORM_EPS)
        xh = x * r
        e = xh * g - t_ref[...]
        loss_ref[...] += jnp.sum(jnp.sum(e * e, axis=-1, keepdims=True), axis=0, keepdims=True) * (0.5 / D_MODEL)
        dout = e * (1.0 / D_MODEL)
        dg_ref[...] += jnp.sum(dout * xh, axis=0, keepdims=True)
        gdy = dout * g
        dx_ref[...] = r * (gdy - xh * jnp.mean(xh * gdy, axis=-1, keepdims=True))

    return pl.pallas_call(
        body, name="loss_head",
        grid=(n // tm,),
        in_specs=[pl.BlockSpec((tm, D_MODEL), lambda i: (i, 0)),
                  pl.BlockSpec((tm, D_MODEL), lambda i: (i, 0)),
                  pl.BlockSpec((1, D_MODEL), lambda i: (0, 0))],
        out_specs=[pl.BlockSpec((tm, D_MODEL), lambda i: (i, 0)),
                   pl.BlockSpec((1, 1), lambda i: (0, 0)),
                   pl.BlockSpec((1, D_MODEL), lambda i: (0, 0))],
        out_shape=[jax.ShapeDtypeStruct((n, D_MODEL), F32),
                   jax.ShapeDtypeStruct((1, 1), F32),
                   jax.ShapeDtypeStruct((1, D_MODEL), F32)],
        compiler_params=_params(dimension_semantics=("arbitrary",)),
    )(x2, tgt2, g_row)


def _outproj_bwd(dx2, yg, w_out, dep):
    n = dx2.shape[0]
    tm = TM_BWD
    n_steps = n // tm

    def body(dx_ref, y_ref, w_ref, dep_ref, dy_ref, dw_ref, acc_ref):
        i = pl.program_id(0)

        @pl.when(i == 0)
        def _():
            acc_ref[...] = jnp.zeros_like(acc_ref)

        dxb = _mx(dx_ref[...])
        dy_ref[...] = _mm_nt(dxb, w_ref[...])
        acc_ref[...] += _mm_tn(y_ref[...], dxb)

        @pl.when(i == n_steps - 1)
        def _():
            dw_ref[...] = _mx(acc_ref[...])

    return pl.pallas_call(
        body, name="outproj_bwd",
        grid=(n_steps,),
        in_specs=[pl.BlockSpec((tm, D_MODEL), lambda i: (i, 0)),
                  pl.BlockSpec((tm, MIX), lambda i: (i, 0)),
                  pl.BlockSpec((MIX, D_MODEL), lambda i: (0, 0)),
                  ANY_SPEC],
        out_specs=[pl.BlockSpec((tm, MIX), lambda i: (i, 0)),
                   pl.BlockSpec((MIX, D_MODEL), lambda i: (0, 0))],
        out_shape=[jax.ShapeDtypeStruct((n, MIX), F32),
                   jax.ShapeDtypeStruct((MIX, D_MODEL), MXU_DTYPE)],
        scratch_shapes=[pltpu.VMEM((MIX, D_MODEL), F32)],
        compiler_params=_params(dimension_semantics=("arbitrary",)),
    )(dx2, yg, w_out, dep)


def _inproj_bwd(dz, h, x2, dx_in, g_rows, w_all, dep, layer):
    n = x2.shape[0]
    tm = TM_BWD
    n_steps = n // tm

    def body(dz_ref, h_ref, x_ref, dxi_ref, g_ref, w_ref, dep_ref, dxo_ref, dw_ref, dg_ref, acc_ref, wcat_ref):
        i = pl.program_id(0)

        @pl.when(i == 0)
        def _():
            acc_ref[...] = jnp.zeros_like(acc_ref)
            dg_ref[...] = jnp.zeros_like(dg_ref)
            for d in range(N_DEV):
                wcat_ref[:, d * W_IN_COLS:(d + 1) * W_IN_COLS] = w_ref[d]

        hb = h_ref[...]
        for d in range(N_DEV):
            acc_ref[d] += _mm_tn(hb, dz_ref[:, d * W_IN_COLS:(d + 1) * W_IN_COLS])
        dh = _mm_nt(dz_ref[...], wcat_ref[...])
        x = x_ref[...]
        r = lax.rsqrt(jnp.mean(x * x, axis=-1, keepdims=True) + NORM_EPS)
        xh = x * r
        dg_ref[...] += jnp.sum(dh * xh, axis=0, keepdims=True)
        gdy = dh * g_ref[...]
        dxo_ref[...] = dxi_ref[...] + r * (gdy - xh * jnp.mean(xh * gdy, axis=-1, keepdims=True))

        @pl.when(i == n_steps - 1)
        def _():
            dw_ref[...] = _mx(acc_ref[...])

    return pl.pallas_call(
        body, name="inproj_bwd",
        grid=(n_steps,),
        in_specs=[pl.BlockSpec((tm, 2 * MIX), lambda i: (i, 0)),
                  pl.BlockSpec((tm, D_MODEL), lambda i: (i, 0)),
                  pl.BlockSpec((tm, D_MODEL), lambda i: (i, 0)),
                  pl.BlockSpec((tm, D_MODEL), lambda i: (i, 0)),
                  _of_layer(layer, 1, D_MODEL),
                  pl.BlockSpec((N_DEV, D_MODEL, W_IN_COLS), lambda i: (0, 0, 0)),
                  ANY_SPEC],
        out_specs=[pl.BlockSpec((tm, D_MODEL), lambda i: (i, 0)),
                   pl.BlockSpec((N_DEV, D_MODEL, W_IN_COLS), lambda i: (0, 0, 0)),
                   pl.BlockSpec((1, D_MODEL), lambda i: (0, 0))],
        out_shape=[jax.ShapeDtypeStruct((n, D_MODEL), F32),
                   jax.ShapeDtypeStruct((N_DEV, D_MODEL, W_IN_COLS), MXU_DTYPE),
                   jax.ShapeDtypeStruct((1, D_MODEL), F32)],
        scratch_shapes=[pltpu.VMEM((N_DEV, D_MODEL, W_IN_COLS), F32),
                        pltpu.VMEM((D_MODEL, 2 * MIX), MXU_DTYPE)],
        compiler_params=_params(dimension_semantics=("arbitrary",)),
    )(dz, h, x2, dx_in, g_rows, w_all, dep)


def _row_pos(t0, rows):
    return t0 + lax.broadcasted_iota(jnp.int32, (rows, LANES), 0)


def _pool_window_mean(upad, g, t0, t_blk):
    k = 2 << g
    w = upad
    sh = 1
    while sh < k:
        w = w + pltpu.roll(w, sh, 0)
        sh *= 2
    count = jnp.minimum(_row_pos(t0, t_blk) + 1, k).astype(F32)
    return w[HALO:] / count - upad[HALO:]


def _pool_window_bwd(qpad, g, t_blk):
    k = 2 << g
    n = t_blk + HALO
    w = qpad
    sh = 1
    while sh < k:
        w = w + pltpu.roll(w, n - sh, 0)
        sh *= 2
    return w[:t_blk]


class _StateBuf:
    def __init__(self, refs, t_blk):
        self.refs = refs
        self.t_blk = t_blk

    def put_chunk(self, b, j, val):
        for c in range(4):
            self.refs[4 * b + c][pl.ds(j, self.t_blk, stride=STATE_ROWS), :] = val[:, c * LANES:(c + 1) * LANES]

    def get_chunk(self, b, j):
        return jnp.concatenate(
            [self.refs[4 * b + c][pl.ds(j, self.t_blk, stride=STATE_ROWS), :] for c in range(4)], axis=-1)

    def load(self, b, r, part):
        return jnp.concatenate(
            [self.refs[4 * b + 2 * part + h][pl.ds(r, STATE_ROWS), :] for h in range(2)], axis=-1)

    def store(self, b, r, part, val):
        for h in range(2):
            self.refs[4 * b + 2 * part + h][pl.ds(r, STATE_ROWS), :] = val[:, h * LANES:(h + 1) * LANES]


def _state_scratch(nb, t_blk):
    return [pltpu.VMEM((t_blk * STATE_ROWS, LANES), F32) for _ in range(4 * nb)]


def _ssm_project_in(u_ssm, wb_ref, buf, nb):
    t_blk = u_ssm.shape[0] // nb
    ub = _mx(u_ssm)
    for j in range(STATE_ROWS):
        m = j // 2
        bu = _mm(ub[:, m * LANES:(m + 1) * LANES], wb_ref[j])
        for b in range(nb):
            buf.put_chunk(b, j, bu[b * t_blk:(b + 1) * t_blk])


def _scan_forward(buf, lbr, lbi, init, nb):
    def step(t, carry):
        r = pl.multiple_of(t * STATE_ROWS, STATE_ROWS)
        out = []
        for b in range(nb):
            sr, si = carry[2 * b], carry[2 * b + 1]
            nr = lbr * sr - lbi * si + buf.load(b, r, 0)
            ni = lbr * si + lbi * sr + buf.load(b, r, 1)
            buf.store(b, r, 0, nr)
            buf.store(b, r, 1, ni)
            out += [nr, ni]
        return tuple(out)

    def body(i, carry):
        for u in range(SCAN_UNROLL):
            carry = step(i * SCAN_UNROLL + u, carry)
        return carry

    return lax.fori_loop(0, buf.t_blk // SCAN_UNROLL, body, init)


def _ssm_project_out(chunk, wc_ref):
    tiles = []
    for m in range(4):
        acc = None
        for j in (2 * m, 2 * m + 1):
            part = _mm_nt(chunk(j), wc_ref[j])
            acc = part if acc is None else acc + part
        tiles.append(acc)
    return jnp.concatenate(tiles, axis=-1)


def _layer_fwd(x3, z3, g_rows, w_in, pool_w, pool_scale, lbr, lbi, wb, wc, d_skip, glu_w, glu_b, w_out, dep, layer):
    nb, seq, _ = x3.shape
    t_blk = min(T_BLK, seq)
    n_t = seq // t_blk
    halo_per_blk = t_blk // HALO
    rows = nb * t_blk
    fused = z3 is None

    def body(*refs):
        if fused:
            (x_ref, g_ref, wi_ref, pw_ref, ps_ref, lbr_ref, lbi_ref, wb_ref, wc_ref, dsk_ref, gw_ref, gb_ref, wo_ref,
             dep_ref, z_ref, h_ref, yg_ref, sc_ref, act_ref, dact_ref, pooled_ref, ypre_ref, xo_ref,
             carry_ref, halo_ref, *s_refs) = refs
        else:
            (x_ref, z_ref, zh_ref, pw_ref, ps_ref, lbr_ref, lbi_ref, wb_ref, wc_ref, dsk_ref, gw_ref, gb_ref, wo_ref,
             dep_ref, yg_ref, sc_ref, act_ref, dact_ref, pooled_ref, ypre_ref, xo_ref, carry_ref, *s_refs) = refs
        i = pl.program_id(0)
        t0 = i * t_blk
        buf = _StateBuf(s_refs, t_blk)
        both = lambda lo, hi: z_ref[:, :, lo:hi].reshape(rows, hi - lo)

        @pl.when(i == 0)
        def _():
            carry_ref[...] = jnp.zeros_like(carry_ref)
            if fused:
                halo_ref[...] = jnp.zeros_like(halo_ref)

        x = x_ref[...].reshape(rows, D_MODEL)
        if fused:
            r = lax.rsqrt(jnp.mean(x * x, axis=-1, keepdims=True) + NORM_EPS)
            h = _mx(x * r * g_ref[...])
            h_ref[...] = h.reshape(nb, t_blk, D_MODEL)
            for d in range(N_DEV):
                z_ref[:, :, d * W_IN_COLS:(d + 1) * W_IN_COLS] = _mm(h, wi_ref[d]).reshape(nb, t_blk, W_IN_COLS)

        u_ssm = both(POOL_W, MIX)
        _ssm_project_in(u_ssm, wb_ref, buf, nb)
        init = tuple(carry_ref[b, :, h * STATE_COLS:(h + 1) * STATE_COLS] for b in range(nb) for h in range(2))
        fin = _scan_forward(buf, lbr_ref[...], lbi_ref[...], init, nb)
        for b in range(nb):
            carry_ref[b, :, 0:STATE_COLS] = fin[2 * b]
            carry_ref[b, :, STATE_COLS:2 * STATE_COLS] = fin[2 * b + 1]

        def chunk(j):
            states = _mx(jnp.concatenate([buf.get_chunk(b, j) for b in range(nb)], axis=0))
            sc_ref[:, j] = states.reshape(nb, t_blk, 2 * STATE_COLS)
            return states

        y = _ssm_project_out(chunk, wc_ref) + dsk_ref[...] * u_ssm
        yg, dgelu = _gelu_and_grad(y)
        ygb = _mx(yg)
        act_ref[...] = ygb.reshape(nb, t_blk, SSM_W)
        dact_ref[...] = _mx(dgelu).reshape(nb, t_blk, SSM_W)
        o_ssm = yg * _sigmoid(_mm(ygb, gw_ref[...]) + gb_ref[...])
        gp = both(MIX + POOL_W, 2 * MIX)
        parts = []
        first = (i == 0)
        for g in range(N_POOL_G):
            cols = slice(g * POOL_GC, (g + 1) * POOL_GC)
            pooled = []
            for b in range(nb):
                halo = halo_ref[b, :, cols] if fused else jnp.where(first, 0.0, zh_ref[b, :, cols])
                pooled.append(_pool_window_mean(jnp.concatenate([halo, z_ref[b, :, cols]], axis=0), g, t0, t_blk))
            pb = _mx(jnp.concatenate(pooled, axis=0))
            ypre = _mm(pb, pw_ref[g])
            pooled_ref[:, :, cols] = pb.reshape(nb, t_blk, POOL_GC)
            ypre_ref[:, :, cols] = ypre.reshape(nb, t_blk, POOL_GC)
            gpp = both(MIX + g * POOL_GC, MIX + (g + 1) * POOL_GC)
            parts.append(_mx(ypre * ps_ref[:, cols] * (gpp * _sigmoid(gpp))))
        parts.append(_mx(o_ssm * (gp * _sigmoid(gp))))
        gated = jnp.concatenate(parts, axis=-1)
        yg_ref[...] = gated.reshape(nb, t_blk, MIX)
        xo_ref[...] = (x + _mm(gated, wo_ref[...])).reshape(nb, t_blk, D_MODEL)
        if fused:
            halo_ref[...] = z_ref[:, t_blk - HALO:, 0:POOL_W]

    const = lambda *shape: pl.BlockSpec(shape, lambda i: (0,) * len(shape))
    tokens = lambda width: pl.BlockSpec((nb, t_blk, width), lambda i: (0, i, 0))
    mixer_specs = [_of_layer(layer, N_POOL_G, POOL_GC, POOL_GC), _of_layer(layer, 1, POOL_W),
                   _of_layer(layer, STATE_ROWS, STATE_COLS), _of_layer(layer, STATE_ROWS, STATE_COLS),
                   _of_layer(layer, STATE_ROWS, LANES, 2 * STATE_COLS),
                   _of_layer(layer, STATE_ROWS, LANES, 2 * STATE_COLS),
                   _of_layer(layer, 1, SSM_W), const(SSM_W, SSM_W), _of_layer(layer, 1, SSM_W),
                   const(MIX, D_MODEL), ANY_SPEC]
    mixer_args = (pool_w, pool_scale, lbr, lbi, wb, wc, d_skip, glu_w, glu_b, w_out, dep)
    out_specs = [tokens(MIX), pl.BlockSpec((nb, STATE_ROWS, t_blk, 2 * STATE_COLS), lambda i: (0, 0, i, 0)),
                 tokens(SSM_W), tokens(SSM_W), tokens(POOL_W), tokens(POOL_W), tokens(D_MODEL)]
    out_shape = [jax.ShapeDtypeStruct((nb, seq, MIX), MXU_DTYPE),
                 jax.ShapeDtypeStruct((nb, STATE_ROWS, seq, 2 * STATE_COLS), MXU_DTYPE),
                 jax.ShapeDtypeStruct((nb, seq, SSM_W), MXU_DTYPE),
                 jax.ShapeDtypeStruct((nb, seq, SSM_W), MXU_DTYPE),
                 jax.ShapeDtypeStruct((nb, seq, POOL_W), MXU_DTYPE),
                 jax.ShapeDtypeStruct((nb, seq, POOL_W), F32),
                 jax.ShapeDtypeStruct((nb, seq, D_MODEL), F32)]
    scratch = [pltpu.VMEM((nb, STATE_ROWS, 2 * STATE_COLS), F32)]
    if fused:
        in_specs = [tokens(D_MODEL), _of_layer(layer, 1, D_MODEL), const(N_DEV, D_MODEL, W_IN_COLS)] + mixer_specs
        args = (x3, g_rows, w_in) + mixer_args
        out_specs = [tokens(2 * MIX), tokens(D_MODEL)] + out_specs
        out_shape = [jax.ShapeDtypeStruct((nb, seq, 2 * MIX), F32),
                     jax.ShapeDtypeStruct((nb, seq, D_MODEL), MXU_DTYPE)] + out_shape
        scratch = scratch + [pltpu.VMEM((nb, HALO, POOL_W), F32)]
    else:
        in_specs = [tokens(D_MODEL), tokens(2 * MIX),
                    pl.BlockSpec((nb, HALO, POOL_W), lambda i: (0, jnp.maximum(i * halo_per_blk - 1, 0), 0))] + mixer_specs
        args = (x3, z3, z3) + mixer_args
    return pl.pallas_call(
        body, name="layer_fwd" if fused else "mixer_fwd",
        grid=(n_t,),
        in_specs=in_specs, out_specs=out_specs, out_shape=out_shape,
        scratch_shapes=scratch + _state_scratch(nb, t_blk),
        compiler_params=_params(dimension_semantics=("arbitrary",)),
    )(*args)


def _mixer_bwd(z3, dy3, states, kept, pool_w, pool_scale, lbr, lbi, wb, wc, d_skip, glu_w, glu_b, layer):
    nb, seq, _ = z3.shape
    t_blk = min(T_BLK, seq)
    n_t = seq // t_blk
    halo_per_blk = t_blk // HALO
    rows = nb * t_blk

    def body(z_ref, dy_ref, sc_ref, sch_ref, act_ref, dact_ref, pooled_ref, ypre_ref, pw_ref, ps_ref, lbr_ref, lbi_ref, wb_ref, wc_ref, dsk_ref,
             gw_ref, gb_ref,
             dz_ref, dpw_ref, dps_ref, dlbr_ref, dlbi_ref, dwb_ref, dwc_ref, ddsk_ref, dgw_ref, dgb_ref,
             gcarry_ref, qcarry_ref, du_ref, dgw_acc, *g_refs):
        i = pl.program_id(0)
        blk = n_t - 1 - i
        t0 = blk * t_blk
        gbuf = _StateBuf(g_refs, t_blk)

        @pl.when(i == 0)
        def _():
            gcarry_ref[...] = jnp.zeros_like(gcarry_ref)
            qcarry_ref[...] = jnp.zeros_like(qcarry_ref)
            for ref in (dpw_ref, dps_ref, dlbr_ref, dlbi_ref, dwb_ref, dwc_ref, ddsk_ref, dgw_acc, dgb_ref):
                ref[...] = jnp.zeros_like(ref)

        lbr_v = lbr_ref[...]
        lbi_v = lbi_ref[...]

        both = lambda ref, lo, hi: ref[:, :, lo:hi].reshape(rows, hi - lo)
        split = lambda val: val.reshape(nb, t_blk, val.shape[-1])
        states = lambda j: sc_ref[:, j].reshape(rows, 2 * STATE_COLS)
        first = (blk == 0)

        u_ssm = both(z_ref, POOL_W, MIX)
        ygb = act_ref[...].reshape(rows, SSM_W)
        yg = ygb.astype(F32)
        dgelu = dact_ref[...].reshape(rows, SSM_W).astype(F32)
        sg = _sigmoid(_mm(ygb, gw_ref[...]) + gb_ref[...])
        o_ssm = yg * sg
        gp = both(z_ref, MIX + POOL_W, 2 * MIX)
        sgm = _sigmoid(gp)
        dyv = both(dy_ref, POOL_W, MIX)
        dz_ref[:, :, MIX + POOL_W:2 * MIX] = split(_mx(dyv * o_ssm * (sgm * (1.0 + gp * (1.0 - sgm)))))
        do = dyv * (gp * sgm)
        dv = do * yg * (sg * (1.0 - sg))
        dvb = _mx(dv)
        dgb_ref[...] += jnp.sum(dv, axis=0, keepdims=True)
        dgw_acc[...] += _mm_tn(ygb, dvb)
        dyp = (do * sg + _mm_nt(dvb, gw_ref[...])) * dgelu
        ddsk_ref[...] += jnp.sum(dyp * u_ssm, axis=0, keepdims=True)
        dypb = _mx(dyp)
        for j in range(STATE_ROWS):
            m = j // 2
            dyt = dypb[:, m * LANES:(m + 1) * LANES]
            ds = _mm(dyt, wc_ref[j])
            for b in range(nb):
                gbuf.put_chunk(b, j, ds[b * t_blk:(b + 1) * t_blk])
            dwc_ref[j] += _mm_tn(dyt, states(j))
        du_ref[...] = split(dsk_ref[...] * dyp)

        for g in range(N_POOL_G):
            cols = slice(g * POOL_GC, (g + 1) * POOL_GC)
            pb = both(pooled_ref, g * POOL_GC, (g + 1) * POOL_GC)
            ypre = both(ypre_ref, g * POOL_GC, (g + 1) * POOL_GC)
            gpp = both(z_ref, MIX + g * POOL_GC, MIX + (g + 1) * POOL_GC)
            sgp = _sigmoid(gpp)
            dyg = both(dy_ref, g * POOL_GC, (g + 1) * POOL_GC)
            scale = ps_ref[:, cols]
            dz_ref[:, :, MIX + g * POOL_GC:MIX + (g + 1) * POOL_GC] = split(_mx(
                dyg * (ypre * scale) * (sgp * (1.0 + gpp * (1.0 - sgp)))))
            dyc = dyg * (gpp * sgp)
            dps_ref[:, cols] += jnp.sum(dyc * ypre, axis=0, keepdims=True)
            dypre = _mx(dyc * scale)
            dpw_ref[g] += _mm_tn(pb, dypre)
            dpooled = _mm_nt(dypre, pw_ref[g])
            count = jnp.minimum(_row_pos(t0, t_blk) + 1, 2 << g).astype(F32)
            for b in range(nb):
                dp = dpooled[b * t_blk:(b + 1) * t_blk]
                q = dp / count
                qpad = jnp.concatenate([q, qcarry_ref[b, :, cols]], axis=0)
                qcarry_ref[b, :, cols] = q[:HALO]
                dz_ref[b, :, cols] = _mx(_pool_window_bwd(qpad, g, t_blk) - dp)

        def rev_step(t, carry):
            r = pl.multiple_of(t * STATE_ROWS, STATE_ROWS)
            out = []
            for b in range(nb):
                gr, gi = carry[2 * b], carry[2 * b + 1]
                ngr = lbr_v * gr + lbi_v * gi + gbuf.load(b, r, 0)
                ngi = lbr_v * gi - lbi_v * gr + gbuf.load(b, r, 1)
                gbuf.store(b, r, 0, ngr)
                gbuf.store(b, r, 1, ngi)
                out += [ngr, ngi]
            return tuple(out)

        def rev_body(i, carry):
            for u in range(SCAN_UNROLL):
                carry = rev_step(t_blk - 1 - (i * SCAN_UNROLL + u), carry)
            return carry

        init_g = tuple(gcarry_ref[b, :, h * STATE_COLS:(h + 1) * STATE_COLS] for b in range(nb) for h in range(2))
        fin = lax.fori_loop(0, t_blk // SCAN_UNROLL, rev_body, init_g)
        for b in range(nb):
            gcarry_ref[b, :, 0:STATE_COLS] = fin[2 * b]
            gcarry_ref[b, :, STATE_COLS:2 * STATE_COLS] = fin[2 * b + 1]

        ub = _mx(u_ssm)
        for m in range(4):
            acc = both(du_ref, m * LANES, (m + 1) * LANES)
            for j in (2 * m, 2 * m + 1):
                g = jnp.concatenate([gbuf.get_chunk(b, j) for b in range(nb)], axis=0)
                gj = _mx(g)
                acc = acc + _mm_nt(gj, wb_ref[j])
                dwb_ref[j] += _mm_tn(ub[:, m * LANES:(m + 1) * LANES], gj)
                shifted = []
                for b in range(nb):
                    before = jnp.where(first, 0.0, sch_ref[b, j].astype(F32))
                    spad = jnp.concatenate([before, sc_ref[b, j].astype(F32)], axis=0)
                    shifted.append(pltpu.roll(spad, 1, 0)[HALO:])
                s_prev = jnp.concatenate(shifted, axis=0)
                g_re, g_im = g[:, :STATE_COLS], g[:, STATE_COLS:]
                p_re, p_im = s_prev[:, :STATE_COLS], s_prev[:, STATE_COLS:]
                dlbr_ref[j:j + 1, :] += jnp.sum(g_re * p_re + g_im * p_im, axis=0, keepdims=True)
                dlbi_ref[j:j + 1, :] += jnp.sum(g_im * p_re - g_re * p_im, axis=0, keepdims=True)
            dz_ref[:, :, POOL_W + m * LANES:POOL_W + (m + 1) * LANES] = split(_mx(acc))

        @pl.when(i == n_t - 1)
        def _():
            dgw_ref[...] = _mx(dgw_acc[...])

    const = lambda *shape: pl.BlockSpec(shape, lambda i: (0,) * len(shape))
    rev = lambda i: n_t - 1 - i
    out_shape = [jax.ShapeDtypeStruct((nb, seq, 2 * MIX), MXU_DTYPE),
                 jax.ShapeDtypeStruct((N_POOL_G, POOL_GC, POOL_GC), F32),
                 jax.ShapeDtypeStruct((1, POOL_W), F32),
                 jax.ShapeDtypeStruct((STATE_ROWS, STATE_COLS), F32),
                 jax.ShapeDtypeStruct((STATE_ROWS, STATE_COLS), F32),
                 jax.ShapeDtypeStruct((STATE_ROWS, LANES, 2 * STATE_COLS), F32),
                 jax.ShapeDtypeStruct((STATE_ROWS, LANES, 2 * STATE_COLS), F32),
                 jax.ShapeDtypeStruct((1, SSM_W), F32),
                 jax.ShapeDtypeStruct((SSM_W, SSM_W), MXU_DTYPE),
                 jax.ShapeDtypeStruct((1, SSM_W), F32)]
    return pl.pallas_call(
        body, name="mixer_bwd",
        grid=(n_t,),
        in_specs=[pl.BlockSpec((nb, t_blk, 2 * MIX), lambda i: (0, rev(i), 0)),
                  pl.BlockSpec((nb, t_blk, MIX), lambda i: (0, rev(i), 0)),
                  pl.BlockSpec((nb, STATE_ROWS, t_blk, 2 * STATE_COLS), lambda i: (0, 0, rev(i), 0)),
                  pl.BlockSpec((nb, STATE_ROWS, HALO, 2 * STATE_COLS),
                               lambda i: (0, 0, jnp.maximum(rev(i) * halo_per_blk - 1, 0), 0)),
                  pl.BlockSpec((nb, t_blk, SSM_W), lambda i: (0, rev(i), 0)),
                  pl.BlockSpec((nb, t_blk, SSM_W), lambda i: (0, rev(i), 0)),
                  pl.BlockSpec((nb, t_blk, POOL_W), lambda i: (0, rev(i), 0)),
                  pl.BlockSpec((nb, t_blk, POOL_W), lambda i: (0, rev(i), 0)),
                  _of_layer(layer, N_POOL_G, POOL_GC, POOL_GC), _of_layer(layer, 1, POOL_W),
                  _of_layer(layer, STATE_ROWS, STATE_COLS), _of_layer(layer, STATE_ROWS, STATE_COLS),
                  _of_layer(layer, STATE_ROWS, LANES, 2 * STATE_COLS),
                  _of_layer(layer, STATE_ROWS, LANES, 2 * STATE_COLS),
                  _of_layer(layer, 1, SSM_W), const(SSM_W, SSM_W), _of_layer(layer, 1, SSM_W)],
        out_specs=[pl.BlockSpec((nb, t_blk, 2 * MIX), lambda i: (0, rev(i), 0))]
                  + [const(*s.shape) for s in out_shape[1:]],
        out_shape=out_shape,
        scratch_shapes=[pltpu.VMEM((nb, STATE_ROWS, 2 * STATE_COLS), F32),
                        pltpu.VMEM((nb, HALO, POOL_W), F32),
                        pltpu.VMEM((nb, t_blk, SSM_W), F32),
                        pltpu.VMEM((SSM_W, SSM_W), F32)]
                       + _state_scratch(nb, t_blk),
        compiler_params=_params(dimension_semantics=("arbitrary",)),
    )(z3, dy3, states, states, *kept, pool_w, pool_scale, lbr, lbi, wb, wc, d_skip, glu_w, glu_b)


def _mesh_place():
    x, y, c = lax.axis_index("x"), lax.axis_index("y"), lax.axis_index("c")
    return x, y, c


def _flip(place, k):
    x, y, c = place
    return (1 - x if k & 4 else x, 1 - y if k & 2 else y, 1 - c if k & 1 else c)


def _index(place):
    x, y, c = place
    return 4 * x + 2 * y + c


HBM_SPEC = pl.BlockSpec(memory_space=pltpu.HBM)
SEM_SPEC = pl.BlockSpec(memory_space=pltpu.SEMAPHORE)
_EFFECT = pltpu.SideEffectType.DATAFLOW_SIDE_EFFECTING
N_PEERS = N_DEV - 1


def _exchange_copies(src_refs, land_refs, send_sems, recv_sems):
    me = _mesh_place()
    mine = _index(me)
    out = []
    for a, land_ref in enumerate(land_refs):
        for k in range(1, N_DEV):
            peer = _flip(me, k)
            theirs = _index(peer)
            n = a * N_PEERS + k - 1
            src = src_refs[a].at[theirs] if src_refs else land_ref.at[mine]
            send = pltpu.make_async_remote_copy(
                src_ref=src, dst_ref=land_ref.at[mine], send_sem=send_sems.at[n], recv_sem=recv_sems.at[n],
                device_id=peer, device_id_type=MESH)
            recv = pltpu.make_async_remote_copy(
                src_ref=src, dst_ref=land_ref.at[theirs], send_sem=send_sems.at[n], recv_sem=recv_sems.at[n],
                device_id=peer, device_id_type=MESH)
            out.append((send, recv))
    return out


def _exchange_start(srcs, lands, after, name):
    arrays = tuple(srcs) + tuple(lands)
    n_src, n_all = len(srcs), len(arrays)
    n_copies = len(lands) * N_PEERS

    def body(*refs):
        send_sems, recv_sems = refs[n_all + 1], refs[n_all + 2]
        token = refs[-1]
        for send, _ in _exchange_copies(refs[:n_src], refs[n_src:n_all], send_sems, recv_sems):
            send.start()
        token[...] = jnp.zeros_like(token)

    res = pl.pallas_call(
        body, name=name,
        in_specs=[HBM_SPEC] * n_all + [ANY_SPEC],
        out_specs=[SEM_SPEC, SEM_SPEC] + [HBM_SPEC] * n_all + [VMEM_SPEC],
        out_shape=[pltpu.SemaphoreType.DMA((n_copies,)), pltpu.SemaphoreType.DMA((n_copies,))]
                  + [pltpu.HBM(a.shape, a.dtype) for a in arrays] + [jax.ShapeDtypeStruct((SUBLANES, LANES), F32)],
        input_output_aliases={i: 2 + i for i in range(n_all)},
        compiler_params=pltpu.CompilerParams(has_side_effects=_EFFECT),
    )(*[pltpu.with_memory_space_constraint(a, pltpu.HBM) for a in arrays], after)
    return tuple(res[:-1]), res[-1]


def _exchange_wait(handle, n_lands, after, name):
    send_sems, recv_sems = handle[0], handle[1]
    arrays = handle[2:]
    n_all = len(arrays)
    n_src = n_all - n_lands

    def body(*refs):
        for send, recv in _exchange_copies(refs[:n_src], refs[n_src:n_all], refs[n_all], refs[n_all + 1]):
            send.wait_send()
            recv.wait_recv()

    res = pl.pallas_call(
        body, name=name,
        in_specs=[HBM_SPEC] * n_all + [SEM_SPEC, SEM_SPEC, ANY_SPEC],
        out_specs=[HBM_SPEC] * n_all,
        out_shape=[pltpu.HBM(a.shape, a.dtype) for a in arrays],
        input_output_aliases={i: i for i in range(n_all)},
        compiler_params=pltpu.CompilerParams(has_side_effects=_EFFECT),
    )(*arrays, send_sems, recv_sems, after)
    return tuple(res[:n_src]), tuple(res[n_src:])


def _weight_zones(shards, which, my_idx, after=None):
    order = [] if after is None else [after]

    def body(idx_ref, *refs):
        ins, zones = refs[:len(shards)], refs[len(shards) + len(order):]
        for zone, (a, l) in zip(zones, which):
            zone[0] = _mx(ins[a][l])

    whole = lambda s: pl.BlockSpec(s.shape, lambda i, idx: (0,) * s.ndim)
    return pl.pallas_call(
        body, name="weight_zones",
        grid_spec=pltpu.PrefetchScalarGridSpec(
            num_scalar_prefetch=1, grid=(1,),
            in_specs=[whole(s) for s in shards] + [ANY_SPEC] * len(order),
            out_specs=[pl.BlockSpec((1,) + shards[a].shape[1:], lambda i, idx: (idx[0], 0, 0)) for a, _ in which]),
        out_shape=[jax.ShapeDtypeStruct((N_DEV,) + shards[a].shape[1:], MXU_DTYPE) for a, _ in which],
        compiler_params=_params(dimension_semantics=("arbitrary",)),
    )(my_idx.reshape(1).astype(jnp.int32), *shards, *order)


def _allreduce_packed(p):
    rows = p.shape[0]
    half = rows // 2
    quarter = half // 4

    def body(p_ref, o_ref, part_ref, sib_ref, got_ref, send_sems, recv_sems):
        x, y, c = _mesh_place()
        sibling = (x, y, 1 - c)
        chip = 2 * x + y
        chips = [(k, (1 - x if k & 2 else x, 1 - y if k & 1 else y, c), chip ^ k) for k in (1, 2, 3)]
        my_half = pl.multiple_of(c * half, SUBLANES)
        other_half = pl.multiple_of((1 - c) * half, SUBLANES)

        def copy(n, src, dst, to):
            return pltpu.make_async_remote_copy(src_ref=src, dst_ref=dst, send_sem=send_sems.at[n],
                                                recv_sem=recv_sems.at[n], device_id=to, device_id_type=MESH)

        def quarter_of(ref, base, q):
            return ref.at[pl.ds(pl.multiple_of(base + q * quarter, SUBLANES), quarter)]

        swap = copy(0, p_ref.at[pl.ds(other_half, half)], sib_ref, sibling)
        swap.start()
        swap.wait()
        part_ref[...] = p_ref[pl.ds(my_half, half), :] + sib_ref[...]

        scatter = [copy(k, quarter_of(part_ref, 0, q), got_ref.at[k - 1], to) for k, to, q in chips]
        for cp in scatter:
            cp.start()
        total = part_ref[pl.ds(pl.multiple_of(chip * quarter, SUBLANES), quarter), :]
        for cp, (k, _, _) in zip(scatter, chips):
            cp.wait()
            total = total + got_ref[k - 1]
        mine = pl.multiple_of(my_half + chip * quarter, SUBLANES)
        o_ref[pl.ds(mine, quarter), :] = total

        gather = [copy(3 + k, o_ref.at[pl.ds(mine, quarter)], o_ref.at[pl.ds(mine, quarter)], to) for k, to, _ in chips]
        for cp in gather:
            cp.start()
        for k, to, q in chips:
            theirs = quarter_of(o_ref, my_half, q)
            copy(3 + k, theirs, theirs, to).wait_recv()
        for cp in gather:
            cp.wait_send()

        back = copy(7, o_ref.at[pl.ds(my_half, half)], o_ref.at[pl.ds(my_half, half)], sibling)
        back.start()
        copy(7, o_ref.at[pl.ds(other_half, half)], o_ref.at[pl.ds(other_half, half)], sibling).wait_recv()
        back.wait_send()

    return pl.pallas_call(
        body, name="comm_allreduce_packed",
        in_specs=[VMEM_SPEC],
        out_specs=VMEM_SPEC,
        out_shape=jax.ShapeDtypeStruct(p.shape, F32),
        scratch_shapes=[pltpu.VMEM((half, LANES), F32),
                        pltpu.VMEM((half, LANES), F32),
                        pltpu.VMEM((3, quarter, LANES), F32),
                        pltpu.SemaphoreType.DMA((8,)),
                        pltpu.SemaphoreType.DMA((8,))],
        compiler_params=_params(),
    )(p)


def _adamw_math(w, g, m, v):
    m = ADAM_B1 * m + (1.0 - ADAM_B1) * g
    v = ADAM_B2 * v + (1.0 - ADAM_B2) * (g * g)
    m_hat = m / (1.0 - ADAM_B1 ** ADAM_STEP)
    v_hat = v / (1.0 - ADAM_B2 ** ADAM_STEP)
    delta = -ADAM_LR * (m_hat / (jnp.sqrt(v_hat) + ADAM_EPS) + ADAM_WD * w)
    return delta, m, v


def _adamw_summed(received, own, my_idx, w, m, v, name):
    depth, r, c = w.shape
    tr = min(r, 128)

    def body(idx_ref, *refs):
        r_refs, o_refs = refs[:depth], refs[depth:2 * depth]
        w_ref, m_ref, v_ref, g_ref, d_ref, nm_ref, nv_ref = refs[2 * depth:]
        me = idx_ref[0]
        for l in range(depth):
            g = jnp.zeros((tr, c), F32)
            for q in range(N_DEV):
                g = g + jnp.where(q == me, o_refs[l][0], r_refs[l][q]).astype(F32)
            g_ref[l] = g
            d_ref[l], nm_ref[l], nv_ref[l] = _adamw_math(w_ref[l], g, m_ref[l], v_ref[l])

    blk = pl.BlockSpec((depth, tr, c), lambda i, idx: (0, i, 0))
    return pl.pallas_call(
        body, name=name,
        grid_spec=pltpu.PrefetchScalarGridSpec(
            num_scalar_prefetch=1, grid=(r // tr,),
            in_specs=[pl.BlockSpec((N_DEV, tr, c), lambda i, idx: (0, i, 0))] * depth
                     + [pl.BlockSpec((1, tr, c), lambda i, idx: (idx[0], i, 0))] * depth
                     + [blk, blk, blk],
            out_specs=[blk] * 4),
        out_shape=[jax.ShapeDtypeStruct((depth, r, c), F32)] * 4,
        compiler_params=_params(dimension_semantics=("arbitrary",)),
    )(my_idx.reshape(1).astype(jnp.int32), *received, *own, w, m, v)


def _adamw_small(ws, gs, ms, vs):
    n = len(ws)
    depth = ws[0].shape[0]

    def spec(a):
        per_layer = a.shape[0] == depth
        rest = (0,) * (a.ndim - 1)
        return pl.BlockSpec((1,) + a.shape[1:], lambda l: ((l if per_layer else 0),) + rest)

    def body(*refs):
        w_refs, g_refs, m_refs, v_refs = (refs[k * n:(k + 1) * n] for k in range(4))
        d_refs, nm_refs, nv_refs = (refs[(4 + k) * n:(5 + k) * n] for k in range(3))
        for k in range(n):
            d_refs[k][...], nm_refs[k][...], nv_refs[k][...] = _adamw_math(
                w_refs[k][...], g_refs[k][...], m_refs[k][...], v_refs[k][...])

    specs = [spec(a) for a in ws]
    shapes = [jax.ShapeDtypeStruct(a.shape, F32) for a in ws]
    res = pl.pallas_call(
        body, name="adamw_small",
        grid=(depth,),
        in_specs=specs * 4,
        out_specs=specs * 3,
        out_shape=shapes * 3,
        compiler_params=_params(dimension_semantics=("arbitrary",)),
    )(*ws, *gs, *ms, *vs)
    return res[:n], res[n:2 * n], res[2 * n:]


_PACK_ROWS = SUBLANES * N_DEV


def _pack(arrays):
    flat = jnp.concatenate([a.reshape(-1) for a in arrays])
    per = _PACK_ROWS * LANES
    total = -(-flat.shape[0] // per) * per
    flat = jnp.pad(flat, (0, total - flat.shape[0]))
    return flat.reshape(total // LANES, LANES)


def _unpack(packed, like):
    flat = packed.reshape(-1)
    out = []
    off = 0
    for a in like:
        out.append(flat[off:off + a.size].reshape(a.shape))
        off += a.size
    return out


def kernel(x, norm_g, w_in, pool_w, pool_scale, a_re, a_im, log_dt, b_re, b_im, c_re, c_im, d_skip, glu_w, glu_b, w_out, final_g, loss_target, m_norm_g, m_w_in, m_pool_w, m_pool_scale, m_a_re, m_a_im, m_log_dt, m_b_re, m_b_im, m_c_re, m_c_im, m_d_skip, m_glu_w, m_glu_b, m_w_out, m_final_g, v_norm_g, v_w_in, v_pool_w, v_pool_scale, v_a_re, v_a_im, v_log_dt, v_b_re, v_b_im, v_c_re, v_c_im, v_d_skip, v_glu_w, v_glu_b, v_w_out, v_final_g):
    nb, seq, _ = x.shape
    n_tok = nb * seq
    depth = norm_g.shape[0]

    my_idx = _index(_mesh_place())

    shards = (w_in, glu_w, w_out)
    (w_in_zone_0,) = _weight_zones(shards, [(0, 0)], my_idx)

    def gather_start(l, after):
        return _exchange_start((), zones[3 * l:3 * l + 3], after, f"comm_gather_start_{l}")

    def gather_wait(handle, after, l):
        _, (win, glu, wout) = _exchange_wait(handle, 3, after, f"comm_gather_wait_{l}")
        return win, glu.reshape(SSM_W, SSM_W), wout.reshape(MIX, D_MODEL)

    xs = [x.reshape(n_tok, D_MODEL)]
    first_w_in, dep = _exchange_start((), (w_in_zone_0,), xs[0], "comm_gather_start_0_w_in")
    others = [(a, l) for l in range(depth) for a in range(3) if (a, l) != (0, 0)]
    zones = [None] + list(_weight_zones(shards, others, my_idx, dep))

    (lbr, lbi, rb, rc), dense_vjp = jax.vjp(jax.vmap(_ssm_dense), a_re, a_im, log_dt + dep[0, 0], b_re, b_im, c_re, c_im)
    chunk_all = jax.vmap(_ssm_chunked)
    wb, wct = chunk_all(rb), chunk_all(rc)
    _, chunk_back = jax.vjp(_ssm_chunked, rb[0])
    wb_m, wct_m = _mx(wb), _mx(wct)
    pool_w_m = _mx(pool_w)
    rows_of = lambda a: a[:, None, :]
    norm_rows, scale_rows, skip_rows, bias_rows = rows_of(norm_g), rows_of(pool_scale), rows_of(d_skip), rows_of(glu_b)

    def layer_params(l):
        return (pool_w_m, scale_rows, lbr, lbi, wb_m, wct_m, skip_rows, weights[l][1], bias_rows)

    saved = []
    weights = []
    for l in range(depth):
        if l == 0:
            _, (win,) = _exchange_wait(first_w_in, 1, wct_m, "comm_gather_wait_0_w_in")
            rest, dep = _exchange_start((), zones[1:3], win, "comm_gather_start_0_rest")
            z, h = _inproj_fwd(xs[-1], norm_rows, win, dep, l)
            _, (glu, wout) = _exchange_wait(rest, 2, z, "comm_gather_wait_0_rest")
            weights.append((win, glu.reshape(SSM_W, SSM_W), wout.reshape(MIX, D_MODEL)))
            handle, dep = gather_start(1, weights[0][2])
            z3 = z.reshape(nb, seq, 2 * MIX)
            yg, states, *kept, x_next = _layer_fwd(xs[-1].reshape(nb, seq, D_MODEL), z3, None, None,
                                                   *layer_params(l), weights[l][2], dep, l)
        else:
            weights.append(gather_wait(handle, xs[-1], l))
            if l + 1 < depth:
                handle, dep = gather_start(l + 1, weights[l][0])
            z3, h3, yg, states, *kept, x_next = _layer_fwd(xs[-1].reshape(nb, seq, D_MODEL), None, norm_rows,
                                                           weights[l][0], *layer_params(l), weights[l][2], dep, l)
            h = h3.reshape(n_tok, D_MODEL)
        xs.append(x_next.reshape(n_tok, D_MODEL))
        saved.append((z3, h, yg.reshape(n_tok, MIX), states, kept))

    dx, loss_part, d_final_g = _loss_head(xs[-1], loss_target.reshape(n_tok, D_MODEL), final_g[None])

    small = {k: [None] * depth for k in
             ("norm_g", "pool_w", "pool_scale", "lbr", "lbi", "wb", "wct", "d_skip", "glu_b")}
    received = [None] * depth
    sent = [None] * depth
    pending = None
    early = None
    for l in reversed(range(depth)):
        z3, h, yg2, states, kept = saved[l]
        dy, d_wout = _outproj_bwd(dx, yg2, weights[l][2], dep)
        (dz, d_pw, d_ps, d_lbr, d_lbi, d_wb, d_wct, d_dsk, d_gw, d_gb) = _mixer_bwd(
            z3, dy.reshape(nb, seq, MIX), states, kept, *layer_params(l), l)
        rest = (d_gw.reshape(N_DEV, SSM_W // N_DEV, SSM_W), d_wout.reshape(N_DEV, MIX // N_DEV, D_MODEL))
        if l == 0:
            early, dep = _exchange_start(rest, tuple(lax.empty(s.shape, s.dtype) for s in rest), dz,
                                         "comm_grads_start_0_rest")
        dx, d_win, d_ng = _inproj_bwd(dz.reshape(n_tok, 2 * MIX), h, xs[l], dx, norm_rows, weights[l][0], dep, l)
        for k, val in (("norm_g", d_ng[0]), ("pool_w", d_pw), ("pool_scale", d_ps[0]), ("lbr", d_lbr),
                       ("lbi", d_lbi), ("wb", d_wb), ("wct", d_wct), ("d_skip", d_dsk[0]), ("glu_b", d_gb[0])):
            small[k][l] = val
        if pending is not None:
            sent[l + 1], received[l + 1] = _exchange_wait(pending, 3, dx, f"comm_grads_wait_{l + 1}")
        srcs = (d_win,) if l == 0 else (d_win,) + rest
        lands = tuple(lax.empty(s.shape, s.dtype) for s in srcs)
        pending, dep = _exchange_start(srcs, lands, dx, f"comm_grads_start_{l}")
    stack = lambda k: jnp.stack(small[k])
    d_rb = jnp.stack([chunk_back(d)[0] for d in small["wb"]])
    d_rc = jnp.stack([chunk_back(d)[0] for d in small["wct"]])
    local = [stack("norm_g"), stack("pool_w"), stack("pool_scale"), stack("lbr"), stack("lbi"), d_rb, d_rc,
             stack("d_skip"), stack("glu_b"), d_final_g[0] + dep[0, 0], loss_part[0]]
    (g_norm_g, g_pool_w, g_pool_scale, g_lbr, g_lbi, g_rb, g_rc, g_d_skip, g_glu_b, g_final_g, loss) = _unpack(
        _allreduce_packed(_pack(local)), local)
    loss = loss[0]
    g_a_re, g_a_im, g_log_dt, g_b_re, g_b_im, g_c_re, g_c_im = dense_vjp((g_lbr, g_lbi, g_rb, g_rc))

    names = ["norm_g", "pool_w", "pool_scale", "a_re", "a_im", "log_dt", "b_re", "b_im", "c_re", "c_im",
             "d_skip", "glu_b", "final_g"]
    rows = {"norm_g", "pool_scale", "log_dt", "d_skip", "glu_b"}
    small_w = [norm_g, pool_w, pool_scale, a_re, a_im, log_dt, b_re, b_im, c_re, c_im, d_skip, glu_b, final_g]
    small_g = [g_norm_g, g_pool_w, g_pool_scale, g_a_re, g_a_im, g_log_dt, g_b_re, g_b_im, g_c_re, g_c_im,
               g_d_skip, g_glu_b, g_final_g]
    small_m = [m_norm_g, m_pool_w, m_pool_scale, m_a_re, m_a_im, m_log_dt, m_b_re, m_b_im, m_c_re, m_c_im,
               m_d_skip, m_glu_b, m_final_g]
    small_v = [v_norm_g, v_pool_w, v_pool_scale, v_a_re, v_a_im, v_log_dt, v_b_re, v_b_im, v_c_re, v_c_im,
               v_d_skip, v_glu_b, v_final_g]

    wide_last = {"b_re", "b_im"}

    def blocked(arrays):
        return [a.reshape(1, 1, -1) if n == "final_g" else a[:, None, :] if n in rows
                else a.swapaxes(2, 3) if n in wide_last else a for n, a in zip(names, arrays)]

    small_d, small_nm, small_nv = _adamw_small(blocked(small_w), blocked(small_g), blocked(small_m), blocked(small_v))
    res = {}
    for kind, arrays in (("grad", small_g), ("delta", small_d), ("m", small_nm), ("v", small_nv)):
        for n, a, like in zip(names, arrays, small_w):
            if kind != "grad" and n in wide_last:
                a = a.swapaxes(2, 3)
            res[kind, n] = a.reshape(like.shape)

    (s_win,), (r_win,) = _exchange_wait(pending, 1, small_d[0], "comm_grads_wait_0")
    (s_glu, s_wout), (r_glu, r_wout) = _exchange_wait(early, 2, small_d[0], "comm_grads_wait_0_rest")
    sent[0], received[0] = (s_win, s_glu, s_wout), (r_win, r_glu, r_wout)
    shard_res = {}
    for pos, (n, w, m, v) in enumerate((("w_in", w_in, m_w_in, v_w_in), ("glu_w", glu_w, m_glu_w, v_glu_w),
                                        ("w_out", w_out, m_w_out, v_w_out))):
        shard_res[n] = _adamw_summed([received[l][pos] for l in range(depth)], [sent[l][pos] for l in range(depth)],
                                     my_idx, w, m, v, "adamw_" + n)
    for n in ("w_in", "glu_w", "w_out"):
        for pos, kind in enumerate(("grad", "delta", "m", "v")):
            res[kind, n] = shard_res[n][pos]

    order = ["norm_g", "w_in", "pool_w", "pool_scale", "a_re", "a_im", "log_dt", "b_re", "b_im", "c_re", "c_im",
             "d_skip", "glu_w", "glu_b", "w_out", "final_g"]
    outs = [loss, dx.reshape(nb, seq, D_MODEL)]
    for kind in ("grad", "delta", "m", "v"):
        outs += [res[kind, n] for n in order]
    return tuple(outs)
```

```python
import math

import jax
import jax.numpy as jnp
from jax import lax
from jax.experimental import pallas as pl
from jax.experimental.pallas import tpu as pltpu

F32 = jnp.float32
MXU_DTYPE = jnp.bfloat16

D_MODEL = 1024
MIX = 1024
POOL_W = 512
SSM_W = 512
N_POOL_G = 4
POOL_GC = 128
SSM_C = 16
SSM_P = 64
NORM_EPS = 1e-5
N_DEV = 8
W_IN_COLS = 2 * MIX // N_DEV

ADAM_LR = 0.001
ADAM_B1 = 0.9
ADAM_B2 = 0.999
ADAM_EPS = 1e-08
ADAM_WD = 0.01
ADAM_STEP = 10

SUBLANES = 8
LANES = 128
HALO = 16
STATE_ROWS = 8
STATE_COLS = 256
CHUNK_GROUPS = STATE_COLS // SSM_P
CHUNK_CH = CHUNK_GROUPS * SSM_C
T_BLK = 256
SCAN_UNROLL = 32
TM_FWD = 512
TM_BWD = 512
VMEM_LIMIT = 56 * 1024 * 1024

MESH = pl.DeviceIdType.MESH
VMEM_SPEC = pl.BlockSpec(memory_space=pltpu.VMEM)
ANY_SPEC = pl.BlockSpec(memory_space=pl.ANY)


def _mm(a, b):
    return jnp.dot(a, b, preferred_element_type=F32)


def _mm_tn(a, b):
    return lax.dot_general(a, b, (((0,), (0,)), ((), ())), preferred_element_type=F32)


def _mm_nt(a, b):
    return lax.dot_general(a, b, (((1,), (1,)), ((), ())), preferred_element_type=F32)


def _mx(a):
    return a.astype(MXU_DTYPE)


def _sigmoid(v):
    return 1.0 / (1.0 + jnp.exp(-v))


_GELU_C = math.sqrt(2.0 / math.pi)
_GELU_A = 0.044715


def _gelu_and_grad(y):
    th = jnp.tanh(_GELU_C * (y + _GELU_A * y * y * y))
    val = 0.5 * y * (1.0 + th)
    grad = 0.5 * (1.0 + th) + 0.5 * y * (1.0 - th * th) * (_GELU_C * (1.0 + 3.0 * _GELU_A * y * y))
    return val, grad


def _params(**kw):
    return pltpu.CompilerParams(vmem_limit_bytes=VMEM_LIMIT, **kw)


def _of_layer(layer, *shape):
    return pl.BlockSpec((None,) + shape, lambda i: (layer,) + (0,) * len(shape))


def _ssm_dense(a_re, a_im, log_dt, b_re, b_im, c_re, c_im):
    dt = jnp.exp(log_dt)[:, None]
    mag = jnp.exp(a_re * dt)
    ang = a_im * dt
    lb_re = mag * jnp.cos(ang)
    lb_im = mag * jnp.sin(ang)
    den = a_re * a_re + a_im * a_im
    n_re = lb_re - 1.0
    n_im = lb_im
    f_re = (n_re * a_re + n_im * a_im) / den
    f_im = (n_im * a_re - n_re * a_im) / den
    bb_re = f_re[..., None] * b_re - f_im[..., None] * b_im
    bb_im = f_re[..., None] * b_im + f_im[..., None] * b_re

    bb = jnp.stack([bb_re, bb_im], axis=0).reshape(2, STATE_ROWS, CHUNK_GROUPS, SSM_P, SSM_C)
    rb = bb.transpose(1, 4, 0, 2, 3).reshape(STATE_ROWS, SSM_C, 2 * STATE_COLS)
    cc = jnp.stack([c_re, -c_im], axis=0).reshape(2, STATE_ROWS, CHUNK_GROUPS, SSM_C, SSM_P)
    rc = cc.transpose(1, 3, 0, 2, 4).reshape(STATE_ROWS, SSM_C, 2 * STATE_COLS)
    return (lb_re.reshape(STATE_ROWS, STATE_COLS), lb_im.reshape(STATE_ROWS, STATE_COLS), rb, rc)


def _ssm_chunked(per_channel):
    row_group = jnp.arange(CHUNK_CH) // SSM_C
    col_group = (jnp.arange(2 * STATE_COLS) // SSM_P) % CHUNK_GROUPS
    own_group = (row_group[:, None] == col_group[None, :]).astype(F32)
    even = (jnp.arange(STATE_ROWS) % 2 == 0).astype(F32)[:, None, None]
    half = jnp.tile(per_channel, (1, CHUNK_GROUPS, 1)) * own_group
    return jnp.concatenate([half * even, half * (1.0 - even)], axis=1)


def _inproj_fwd(x2, g_rows, w_all, dep, layer):
    n = x2.shape[0]
    tm = TM_FWD

    def body(x_ref, g_ref, w_ref, dep_ref, z_ref, h_ref):
        x = x_ref[...]
        r = lax.rsqrt(jnp.mean(x * x, axis=-1, keepdims=True) + NORM_EPS)
        h = _mx(x * r * g_ref[...])
        h_ref[...] = h
        for d in range(N_DEV):
            z_ref[:, d * W_IN_COLS:(d + 1) * W_IN_COLS] = _mm(h, w_ref[d])

    return pl.pallas_call(
        body, name="inproj_fwd",
        grid=(n // tm,),
        in_specs=[pl.BlockSpec((tm, D_MODEL), lambda i: (i, 0)),
                  _of_layer(layer, 1, D_MODEL),
                  pl.BlockSpec((N_DEV, D_MODEL, W_IN_COLS), lambda i: (0, 0, 0)),
                  ANY_SPEC],
        out_specs=[pl.BlockSpec((tm, 2 * MIX), lambda i: (i, 0)),
                   pl.BlockSpec((tm, D_MODEL), lambda i: (i, 0))],
        out_shape=[jax.ShapeDtypeStruct((n, 2 * MIX), F32),
                   jax.ShapeDtypeStruct((n, D_MODEL), MXU_DTYPE)],
        compiler_params=_params(dimension_semantics=("arbitrary",)),
    )(x2, g_rows, w_all, dep)


def _loss_head(x2, tgt2, g_row):
    n = x2.shape[0]
    tm = TM_FWD

    def body(x_ref, t_ref, g_ref, dx_ref, loss_ref, dg_ref):
        @pl.when(pl.program_id(0) == 0)
        def _():
            loss_ref[...] = jnp.zeros_like(loss_ref)
            dg_ref[...] = jnp.zeros_like(dg_ref)

        x = x_ref[...]
        g = g_ref[...]
        r = lax.rsqrt(jnp.mean(x * x, axis=-1, keepdims=True) + NORM_EPS)
        xh = x * r
        e = xh * g - t_ref[...]
        loss_ref[...] += jnp.sum(jnp.sum(e * e, axis=-1, keepdims=True), axis=0, keepdims=True) * (0.5 / D_MODEL)
        dout = e * (1.0 / D_MODEL)
        dg_ref[...] += jnp.sum(dout * xh, axis=0, keepdims=True)
        gdy = dout * g
        dx_ref[...] = r * (gdy - xh * jnp.mean(xh * gdy, axis=-1, keepdims=True))

    return pl.pallas_call(
        body, name="loss_head",
        grid=(n // tm,),
        in_specs=[pl.BlockSpec((tm, D_MODEL), lambda i: (i, 0)),
                  pl.BlockSpec((tm, D_MODEL), lambda i: (i, 0)),
                  pl.BlockSpec((1, D_MODEL), lambda i: (0, 0))],
        out_specs=[pl.BlockSpec((tm, D_MODEL), lambda i: (i, 0)),
                   pl.BlockSpec((1, 1), lambda i: (0, 0)),
                   pl.BlockSpec((1, D_MODEL), lambda i: (0, 0))],
        out_shape=[jax.ShapeDtypeStruct((n, D_MODEL), F32),
                   jax.ShapeDtypeStruct((1, 1), F32),
                   jax.ShapeDtypeStruct((1, D_MODEL), F32)],
        compiler_params=_params(dimension_semantics=("arbitrary",)),
    )(x2, tgt2, g_row)


def _outproj_bwd(dx2, yg, w_out, dep):
    n = dx2.shape[0]
    tm = TM_BWD
    n_steps = n // tm

    def body(dx_ref, y_ref, w_ref, dep_ref, dy_ref, dw_ref, acc_ref):
        i = pl.program_id(0)

        @pl.when(i == 0)
        def _():
            acc_ref[...] = jnp.zeros_like(acc_ref)

        dxb = _mx(dx_ref[...])
        dy_ref[...] = _mm_nt(dxb, w_ref[...])
        acc_ref[...] += _mm_tn(y_ref[...], dxb)

        @pl.when(i == n_steps - 1)
        def _():
            dw_ref[...] = _mx(acc_ref[...])

    return pl.pallas_call(
        body, name="outproj_bwd",
        grid=(n_steps,),
        in_specs=[pl.BlockSpec((tm, D_MODEL), lambda i: (i, 0)),
                  pl.BlockSpec((tm, MIX), lambda i: (i, 0)),
                  pl.BlockSpec((MIX, D_MODEL), lambda i: (0, 0)),
                  ANY_SPEC],
        out_specs=[pl.BlockSpec((tm, MIX), lambda i: (i, 0)),
                   pl.BlockSpec((MIX, D_MODEL), lambda i: (0, 0))],
        out_shape=[jax.ShapeDtypeStruct((n, MIX), F32),
                   jax.ShapeDtypeStruct((MIX, D_MODEL), MXU_DTYPE)],
        scratch_shapes=[pltpu.VMEM((MIX, D_MODEL), F32)],
        compiler_params=_params(dimension_semantics=("arbitrary",)),
    )(dx2, yg, w_out, dep)


def _inproj_bwd(dz, h, x2, dx_in, g_rows, w_all, dep, layer):
    n = x2.shape[0]
    tm = TM_BWD
    n_steps = n // tm

    def body(dz_ref, h_ref, x_ref, dxi_ref, g_ref, w_ref, dep_ref, dxo_ref, dw_ref, dg_ref, acc_ref, wcat_ref):
        i = pl.program_id(0)

        @pl.when(i == 0)
        def _():
            acc_ref[...] = jnp.zeros_like(acc_ref)
            dg_ref[...] = jnp.zeros_like(dg_ref)
            for d in range(N_DEV):
                wcat_ref[:, d * W_IN_COLS:(d + 1) * W_IN_COLS] = w_ref[d]

        hb = h_ref[...]
        for d in range(N_DEV):
            acc_ref[d] += _mm_tn(hb, dz_ref[:, d * W_IN_COLS:(d + 1) * W_IN_COLS])
        dh = _mm_nt(dz_ref[...], wcat_ref[...])
        x = x_ref[...]
        r = lax.rsqrt(jnp.mean(x * x, axis=-1, keepdims=True) + NORM_EPS)
        xh = x * r
        dg_ref[...] += jnp.sum(dh * xh, axis=0, keepdims=True)
        gdy = dh * g_ref[...]
        dxo_ref[...] = dxi_ref[...] + r * (gdy - xh * jnp.mean(xh * gdy, axis=-1, keepdims=True))

        @pl.when(i == n_steps - 1)
        def _():
            dw_ref[...] = _mx(acc_ref[...])

    return pl.pallas_call(
        body, name="inproj_bwd",
        grid=(n_steps,),
        in_specs=[pl.BlockSpec((tm, 2 * MIX), lambda i: (i, 0)),
                  pl.BlockSpec((tm, D_MODEL), lambda i: (i, 0)),
                  pl.BlockSpec((tm, D_MODEL), lambda i: (i, 0)),
                  pl.BlockSpec((tm, D_MODEL), lambda i: (i, 0)),
                  _of_layer(layer, 1, D_MODEL),
                  pl.BlockSpec((N_DEV, D_MODEL, W_IN_COLS), lambda i: (0, 0, 0)),
                  ANY_SPEC],
        out_specs=[pl.BlockSpec((tm, D_MODEL), lambda i: (i, 0)),
                   pl.BlockSpec((N_DEV, D_MODEL, W_IN_COLS), lambda i: (0, 0, 0)),
                   pl.BlockSpec((1, D_MODEL), lambda i: (0, 0))],
        out_shape=[jax.ShapeDtypeStruct((n, D_MODEL), F32),
                   jax.ShapeDtypeStruct((N_DEV, D_MODEL, W_IN_COLS), MXU_DTYPE),
                   jax.ShapeDtypeStruct((1, D_MODEL), F32)],
        scratch_shapes=[pltpu.VMEM((N_DEV, D_MODEL, W_IN_COLS), F32),
                        pltpu.VMEM((D_MODEL, 2 * MIX), MXU_DTYPE)],
        compiler_params=_params(dimension_semantics=("arbitrary",)),
    )(dz, h, x2, dx_in, g_rows, w_all, dep)


def _row_pos(t0, rows):
    return t0 + lax.broadcasted_iota(jnp.int32, (rows, LANES), 0)


def _pool_window_mean(upad, g, t0, t_blk):
    k = 2 << g
    w = upad
    sh = 1
    while sh < k:
        w = w + pltpu.roll(w, sh, 0)
        sh *= 2
    count = jnp.minimum(_row_pos(t0, t_blk) + 1, k).astype(F32)
    return w[HALO:] / count - upad[HALO:]


def _pool_window_bwd(qpad, g, t_blk):
    k = 2 << g
    n = t_blk + HALO
    w = qpad
    sh = 1
    while sh < k:
        w = w + pltpu.roll(w, n - sh, 0)
        sh *= 2
    return w[:t_blk]


class _StateBuf:
    def __init__(self, refs, t_blk):
        self.refs = refs
        self.t_blk = t_blk

    def put_chunk(self, b, j, val):
        for c in range(4):
            self.refs[4 * b + c][pl.ds(j, self.t_blk, stride=STATE_ROWS), :] = val[:, c * LANES:(c + 1) * LANES]

    def get_chunk(self, b, j):
        return jnp.concatenate(
            [self.refs[4 * b + c][pl.ds(j, self.t_blk, stride=STATE_ROWS), :] for c in range(4)], axis=-1)

    def load(self, b, r, part):
        return jnp.concatenate(
            [self.refs[4 * b + 2 * part + h][pl.ds(r, STATE_ROWS), :] for h in range(2)], axis=-1)

    def store(self, b, r, part, val):
        for h in range(2):
            self.refs[4 * b + 2 * part + h][pl.ds(r, STATE_ROWS), :] = val[:, h * LANES:(h + 1) * LANES]


def _state_scratch(nb, t_blk):
    return [pltpu.VMEM((t_blk * STATE_ROWS, LANES), F32) for _ in range(4 * nb)]


def _ssm_project_in(u_ssm, wb_ref, buf, nb):
    t_blk = u_ssm.shape[0] // nb
    ub = _mx(u_ssm)
    for j in range(STATE_ROWS):
        m = j // 2
        bu = _mm(ub[:, m * LANES:(m + 1) * LANES], wb_ref[j])
        for b in range(nb):
            buf.put_chunk(b, j, bu[b * t_blk:(b + 1) * t_blk])


def _scan_forward(buf, lbr, lbi, init, nb):
    def step(t, carry):
        r = pl.multiple_of(t * STATE_ROWS, STATE_ROWS)
        out = []
        for b in range(nb):
            sr, si = carry[2 * b], carry[2 * b + 1]
            nr = lbr * sr - lbi * si + buf.load(b, r, 0)
            ni = lbr * si + lbi * sr + buf.load(b, r, 1)
            buf.store(b, r, 0, nr)
            buf.store(b, r, 1, ni)
            out += [nr, ni]
        return tuple(out)

    def body(i, carry):
        for u in range(SCAN_UNROLL):
            carry = step(i * SCAN_UNROLL + u, carry)
        return carry

    return lax.fori_loop(0, buf.t_blk // SCAN_UNROLL, body, init)


def _ssm_project_out(chunk, wc_ref):
    tiles = []
    for m in range(4):
        acc = None
        for j in (2 * m, 2 * m + 1):
            part = _mm_nt(chunk(j), wc_ref[j])
            acc = part if acc is None else acc + part
        tiles.append(acc)
    return jnp.concatenate(tiles, axis=-1)


def _layer_fwd(x3, z3, g_rows, w_in, pool_w, pool_scale, lbr, lbi, wb, wc, d_skip, glu_w, glu_b, w_out, dep, layer):
    nb, seq, _ = x3.shape
    t_blk = min(T_BLK, seq)
    n_t = seq // t_blk
    halo_per_blk = t_blk // HALO
    rows = nb * t_blk
    fused = z3 is None

    def body(*refs):
        if fused:
            (x_ref, g_ref, wi_ref, pw_ref, ps_ref, lbr_ref, lbi_ref, wb_ref, wc_ref, dsk_ref, gw_ref, gb_ref, wo_ref,
             dep_ref, z_ref, h_ref, yg_ref, sc_ref, act_ref, dact_ref, pooled_ref, ypre_ref, xo_ref,
             carry_ref, halo_ref, *s_refs) = refs
        else:
            (x_ref, z_ref, zh_ref, pw_ref, ps_ref, lbr_ref, lbi_ref, wb_ref, wc_ref, dsk_ref, gw_ref, gb_ref, wo_ref,
             dep_ref, yg_ref, sc_ref, act_ref, dact_ref, pooled_ref, ypre_ref, xo_ref, carry_ref, *s_refs) = refs
        i = pl.program_id(0)
        t0 = i * t_blk
        buf = _StateBuf(s_refs, t_blk)
        both = lambda lo, hi: z_ref[:, :, lo:hi].reshape(rows, hi - lo)

        @pl.when(i == 0)
        def _():
            carry_ref[...] = jnp.zeros_like(carry_ref)
            if fused:
                halo_ref[...] = jnp.zeros_like(halo_ref)

        x = x_ref[...].reshape(rows, D_MODEL)
        if fused:
            r = lax.rsqrt(jnp.mean(x * x, axis=-1, keepdims=True) + NORM_EPS)
            h = _mx(x * r * g_ref[...])
            h_ref[...] = h.reshape(nb, t_blk, D_MODEL)
            for d in range(N_DEV):
                z_ref[:, :, d * W_IN_COLS:(d + 1) * W_IN_COLS] = _mm(h, wi_ref[d]).reshape(nb, t_blk, W_IN_COLS)

        u_ssm = both(POOL_W, MIX)
        _ssm_project_in(u_ssm, wb_ref, buf, nb)
        init = tuple(carry_ref[b, :, h * STATE_COLS:(h + 1) * STATE_COLS] for b in range(nb) for h in range(2))
        fin = _scan_forward(buf, lbr_ref[...], lbi_ref[...], init, nb)
        for b in range(nb):
            carry_ref[b, :, 0:STATE_COLS] = fin[2 * b]
            carry_ref[b, :, STATE_COLS:2 * STATE_COLS] = fin[2 * b + 1]

        def chunk(j):
            states = _mx(jnp.concatenate([buf.get_chunk(b, j) for b in range(nb)], axis=0))
            sc_ref[:, j] = states.reshape(nb, t_blk, 2 * STATE_COLS)
            return states

        y = _ssm_project_out(chunk, wc_ref) + dsk_ref[...] * u_ssm
        yg, dgelu = _gelu_and_grad(y)
        ygb = _mx(yg)
        act_ref[...] = ygb.reshape(nb, t_blk, SSM_W)
        dact_ref[...] = _mx(dgelu).reshape(nb, t_blk, SSM_W)
        o_ssm = yg * _sigmoid(_mm(ygb, gw_ref[...]) + gb_ref[...])
        gp = both(MIX + POOL_W, 2 * MIX)
        parts = []
        first = (i == 0)
        for g in range(N_POOL_G):
            cols = slice(g * POOL_GC, (g + 1) * POOL_GC)
            pooled = []
            for b in range(nb):
                halo = halo_ref[b, :, cols] if fused else jnp.where(first, 0.0, zh_ref[b, :, cols])
                pooled.append(_pool_window_mean(jnp.concatenate([halo, z_ref[b, :, cols]], axis=0), g, t0, t_blk))
            pb = _mx(jnp.concatenate(pooled, axis=0))
            ypre = _mm(pb, pw_ref[g])
            pooled_ref[:, :, cols] = pb.reshape(nb, t_blk, POOL_GC)
            ypre_ref[:, :, cols] = ypre.reshape(nb, t_blk, POOL_GC)
            gpp = both(MIX + g * POOL_GC, MIX + (g + 1) * POOL_GC)
            parts.append(_mx(ypre * ps_ref[:, cols] * (gpp * _sigmoid(gpp))))
        parts.append(_mx(o_ssm * (gp * _sigmoid(gp))))
        gated = jnp.concatenate(parts, axis=-1)
        yg_ref[...] = gated.reshape(nb, t_blk, MIX)
        xo_ref[...] = (x + _mm(gated, wo_ref[...])).reshape(nb, t_blk, D_MODEL)
        if fused:
            halo_ref[...] = z_ref[:, t_blk - HALO:, 0:POOL_W]

    const = lambda *shape: pl.BlockSpec(shape, lambda i: (0,) * len(shape))
    tokens = lambda width: pl.BlockSpec((nb, t_blk, width), lambda i: (0, i, 0))
    mixer_specs = [_of_layer(layer, N_POOL_G, POOL_GC, POOL_GC), _of_layer(layer, 1, POOL_W),
                   _of_layer(layer, STATE_ROWS, STATE_COLS), _of_layer(layer, STATE_ROWS, STATE_COLS),
                   _of_layer(layer, STATE_ROWS, LANES, 2 * STATE_COLS),
                   _of_layer(layer, STATE_ROWS, LANES, 2 * STATE_COLS),
                   _of_layer(layer, 1, SSM_W), const(SSM_W, SSM_W), _of_layer(layer, 1, SSM_W),
                   const(MIX, D_MODEL), ANY_SPEC]
    mixer_args = (pool_w, pool_scale, lbr, lbi, wb, wc, d_skip, glu_w, glu_b, w_out, dep)
    out_specs = [tokens(MIX), pl.BlockSpec((nb, STATE_ROWS, t_blk, 2 * STATE_COLS), lambda i: (0, 0, i, 0)),
                 tokens(SSM_W), tokens(SSM_W), tokens(POOL_W), tokens(POOL_W), tokens(D_MODEL)]
    out_shape = [jax.ShapeDtypeStruct((nb, seq, MIX), MXU_DTYPE),
                 jax.ShapeDtypeStruct((nb, STATE_ROWS, seq, 2 * STATE_COLS), MXU_DTYPE),
                 jax.ShapeDtypeStruct((nb, seq, SSM_W), MXU_DTYPE),
                 jax.ShapeDtypeStruct((nb, seq, SSM_W), MXU_DTYPE),
                 jax.ShapeDtypeStruct((nb, seq, POOL_W), MXU_DTYPE),
                 jax.ShapeDtypeStruct((nb, seq, POOL_W), F32),
                 jax.ShapeDtypeStruct((nb, seq, D_MODEL), F32)]
    scratch = [pltpu.VMEM((nb, STATE_ROWS, 2 * STATE_COLS), F32)]
    if fused:
        in_specs = [tokens(D_MODEL), _of_layer(layer, 1, D_MODEL), const(N_DEV, D_MODEL, W_IN_COLS)] + mixer_specs
        args = (x3, g_rows, w_in) + mixer_args
        out_specs = [tokens(2 * MIX), tokens(D_MODEL)] + out_specs
        out_shape = [jax.ShapeDtypeStruct((nb, seq, 2 * MIX), F32),
                     jax.ShapeDtypeStruct((nb, seq, D_MODEL), MXU_DTYPE)] + out_shape
        scratch = scratch + [pltpu.VMEM((nb, HALO, POOL_W), F32)]
    else:
        in_specs = [tokens(D_MODEL), tokens(2 * MIX),
                    pl.BlockSpec((nb, HALO, POOL_W), lambda i: (0, jnp.maximum(i * halo_per_blk - 1, 0), 0))] + mixer_specs
        args = (x3, z3, z3) + mixer_args
    return pl.pallas_call(
        body, name="layer_fwd" if fused else "mixer_fwd",
        grid=(n_t,),
        in_specs=in_specs, out_specs=out_specs, out_shape=out_shape,
        scratch_shapes=scratch + _state_scratch(nb, t_blk),
        compiler_params=_params(dimension_semantics=("arbitrary",)),
    )(*args)


def _mixer_bwd(z3, dy3, states, kept, pool_w, pool_scale, lbr, lbi, wb, wc, d_skip, glu_w, glu_b, layer):
    nb, seq, _ = z3.shape
    t_blk = min(T_BLK, seq)
    n_t = seq // t_blk
    halo_per_blk = t_blk // HALO
    rows = nb * t_blk

    def body(z_ref, dy_ref, sc_ref, sch_ref, act_ref, dact_ref, pooled_ref, ypre_ref, pw_ref, ps_ref, lbr_ref, lbi_ref, wb_ref, wc_ref, dsk_ref,
             gw_ref, gb_ref,
             dz_ref, dpw_ref, dps_ref, dlbr_ref, dlbi_ref, dwb_ref, dwc_ref, ddsk_ref, dgw_ref, dgb_ref,
             gcarry_ref, qcarry_ref, du_ref, dgw_acc, *g_refs):
        i = pl.program_id(0)
        blk = n_t - 1 - i
        t0 = blk * t_blk
        gbuf = _StateBuf(g_refs, t_blk)

        @pl.when(i == 0)
        def _():
            gcarry_ref[...] = jnp.zeros_like(gcarry_ref)
            qcarry_ref[...] = jnp.zeros_like(qcarry_ref)
            for ref in (dpw_ref, dps_ref, dlbr_ref, dlbi_ref, dwb_ref, dwc_ref, ddsk_ref, dgw_acc, dgb_ref):
                ref[...] = jnp.zeros_like(ref)

        lbr_v = lbr_ref[...]
        lbi_v = lbi_ref[...]

        both = lambda ref, lo, hi: ref[:, :, lo:hi].reshape(rows, hi - lo)
        split = lambda val: val.reshape(nb, t_blk, val.shape[-1])
        states = lambda j: sc_ref[:, j].reshape(rows, 2 * STATE_COLS)
        first = (blk == 0)

        u_ssm = both(z_ref, POOL_W, MIX)
        ygb = act_ref[...].reshape(rows, SSM_W)
        yg = ygb.astype(F32)
        dgelu = dact_ref[...].reshape(rows, SSM_W).astype(F32)
        sg = _sigmoid(_mm(ygb, gw_ref[...]) + gb_ref[...])
        o_ssm = yg * sg
        gp = both(z_ref, MIX + POOL_W, 2 * MIX)
        sgm = _sigmoid(gp)
        dyv = both(dy_ref, POOL_W, MIX)
        dz_ref[:, :, MIX + POOL_W:2 * MIX] = split(_mx(dyv * o_ssm * (sgm * (1.0 + gp * (1.0 - sgm)))))
        do = dyv * (gp * sgm)
        dv = do * yg * (sg * (1.0 - sg))
        dvb = _mx(dv)
        dgb_ref[...] += jnp.sum(dv, axis=0, keepdims=True)
        dgw_acc[...] += _mm_tn(ygb, dvb)
        dyp = (do * sg + _mm_nt(dvb, gw_ref[...])) * dgelu
        ddsk_ref[...] += jnp.sum(dyp * u_ssm, axis=0, keepdims=True)
        dypb = _mx(dyp)
        for j in range(STATE_ROWS):
            m = j // 2
            dyt = dypb[:, m * LANES:(m + 1) * LANES]
            ds = _mm(dyt, wc_ref[j])
            for b in range(nb):
                gbuf.put_chunk(b, j, ds[b * t_blk:(b + 1) * t_blk])
            dwc_ref[j] += _mm_tn(dyt, states(j))
        du_ref[...] = split(dsk_ref[...] * dyp)

        for g in range(N_POOL_G):
            cols = slice(g * POOL_GC, (g + 1) * POOL_GC)
            pb = both(pooled_ref, g * POOL_GC, (g + 1) * POOL_GC)
            ypre = both(ypre_ref, g * POOL_GC, (g + 1) * POOL_GC)
            gpp = both(z_ref, MIX + g * POOL_GC, MIX + (g + 1) * POOL_GC)
            sgp = _sigmoid(gpp)
            dyg = both(dy_ref, g * POOL_GC, (g + 1) * POOL_GC)
            scale = ps_ref[:, cols]
            dz_ref[:, :, MIX + g * POOL_GC:MIX + (g + 1) * POOL_GC] = split(_mx(
                dyg * (ypre * scale) * (sgp * (1.0 + gpp * (1.0 - sgp)))))
            dyc = dyg * (gpp * sgp)
            dps_ref[:, cols] += jnp.sum(dyc * ypre, axis=0, keepdims=True)
            dypre = _mx(dyc * scale)
            dpw_ref[g] += _mm_tn(pb, dypre)
            dpooled = _mm_nt(dypre, pw_ref[g])
            count = jnp.minimum(_row_pos(t0, t_blk) + 1, 2 << g).astype(F32)
            for b in range(nb):
                dp = dpooled[b * t_blk:(b + 1) * t_blk]
                q = dp / count
                qpad = jnp.concatenate([q, qcarry_ref[b, :, cols]], axis=0)
                qcarry_ref[b, :, cols] = q[:HALO]
                dz_ref[b, :, cols] = _mx(_pool_window_bwd(qpad, g, t_blk) - dp)

        def rev_step(t, carry):
            r = pl.multiple_of(t * STATE_ROWS, STATE_ROWS)
            out = []
            for b in range(nb):
                gr, gi = carry[2 * b], carry[2 * b + 1]
                ngr = lbr_v * gr + lbi_v * gi + gbuf.load(b, r, 0)
                ngi = lbr_v * gi - lbi_v * gr + gbuf.load(b, r, 1)
                gbuf.store(b, r, 0, ngr)
                gbuf.store(b, r, 1, ngi)
                out += [ngr, ngi]
            return tuple(out)

        def rev_body(i, carry):
            for u in range(SCAN_UNROLL):
                carry = rev_step(t_blk - 1 - (i * SCAN_UNROLL + u), carry)
            return carry

        init_g = tuple(gcarry_ref[b, :, h * STATE_COLS:(h + 1) * STATE_COLS] for b in range(nb) for h in range(2))
        fin = lax.fori_loop(0, t_blk // SCAN_UNROLL, rev_body, init_g)
        for b in range(nb):
            gcarry_ref[b, :, 0:STATE_COLS] = fin[2 * b]
            gcarry_ref[b, :, STATE_COLS:2 * STATE_COLS] = fin[2 * b + 1]

        ub = _mx(u_ssm)
        for m in range(4):
            acc = both(du_ref, m * LANES, (m + 1) * LANES)
            for j in (2 * m, 2 * m + 1):
                g = jnp.concatenate([gbuf.get_chunk(b, j) for b in range(nb)], axis=0)
                gj = _mx(g)
                acc = acc + _mm_nt(gj, wb_ref[j])
                dwb_ref[j] += _mm_tn(ub[:, m * LANES:(m + 1) * LANES], gj)
                shifted = []
                for b in range(nb):
                    before = jnp.where(first, 0.0, sch_ref[b, j].astype(F32))
                    spad = jnp.concatenate([before, sc_ref[b, j].astype(F32)], axis=0)
                    shifted.append(pltpu.roll(spad, 1, 0)[HALO:])
                s_prev = jnp.concatenate(shifted, axis=0)
                g_re, g_im = g[:, :STATE_COLS], g[:, STATE_COLS:]
                p_re, p_im = s_prev[:, :STATE_COLS], s_prev[:, STATE_COLS:]
                dlbr_ref[j:j + 1, :] += jnp.sum(g_re * p_re + g_im * p_im, axis=0, keepdims=True)
                dlbi_ref[j:j + 1, :] += jnp.sum(g_im * p_re - g_re * p_im, axis=0, keepdims=True)
            dz_ref[:, :, POOL_W + m * LANES:POOL_W + (m + 1) * LANES] = split(_mx(acc))

        @pl.when(i == n_t - 1)
        def _():
            dgw_ref[...] = _mx(dgw_acc[...])

    const = lambda *shape: pl.BlockSpec(shape, lambda i: (0,) * len(shape))
    rev = lambda i: n_t - 1 - i
    out_shape = [jax.ShapeDtypeStruct((nb, seq, 2 * MIX), MXU_DTYPE),
                 jax.ShapeDtypeStruct((N_POOL_G, POOL_GC, POOL_GC), F32),
                 jax.ShapeDtypeStruct((1, POOL_W), F32),
                 jax.ShapeDtypeStruct((STATE_ROWS, STATE_COLS), F32),
                 jax.ShapeDtypeStruct((STATE_ROWS, STATE_COLS), F32),
                 jax.ShapeDtypeStruct((STATE_ROWS, LANES, 2 * STATE_COLS), F32),
                 jax.ShapeDtypeStruct((STATE_ROWS, LANES, 2 * STATE_COLS), F32),
                 jax.ShapeDtypeStruct((1, SSM_W), F32),
                 jax.ShapeDtypeStruct((SSM_W, SSM_W), MXU_DTYPE),
                 jax.ShapeDtypeStruct((1, SSM_W), F32)]
    return pl.pallas_call(
        body, name="mixer_bwd",
        grid=(n_t,),
        in_specs=[pl.BlockSpec((nb, t_blk, 2 * MIX), lambda i: (0, rev(i), 0)),
                  pl.BlockSpec((nb, t_blk, MIX), lambda i: (0, rev(i), 0)),
                  pl.BlockSpec((nb, STATE_ROWS, t_blk, 2 * STATE_COLS), lambda i: (0, 0, rev(i), 0)),
                  pl.BlockSpec((nb, STATE_ROWS, HALO, 2 * STATE_COLS),
                               lambda i: (0, 0, jnp.maximum(rev(i) * halo_per_blk - 1, 0), 0)),
                  pl.BlockSpec((nb, t_blk, SSM_W), lambda i: (0, rev(i), 0)),
                  pl.BlockSpec((nb, t_blk, SSM_W), lambda i: (0, rev(i), 0)),
                  pl.BlockSpec((nb, t_blk, POOL_W), lambda i: (0, rev(i), 0)),
                  pl.BlockSpec((nb, t_blk, POOL_W), lambda i: (0, rev(i), 0)),
                  _of_layer(layer, N_POOL_G, POOL_GC, POOL_GC), _of_layer(layer, 1, POOL_W),
                  _of_layer(layer, STATE_ROWS, STATE_COLS), _of_layer(layer, STATE_ROWS, STATE_COLS),
                  _of_layer(layer, STATE_ROWS, LANES, 2 * STATE_COLS),
                  _of_layer(layer, STATE_ROWS, LANES, 2 * STATE_COLS),
                  _of_layer(layer, 1, SSM_W), const(SSM_W, SSM_W), _of_layer(layer, 1, SSM_W)],
        out_specs=[pl.BlockSpec((nb, t_blk, 2 * MIX), lambda i: (0, rev(i), 0))]
                  + [const(*s.shape) for s in out_shape[1:]],
        out_shape=out_shape,
        scratch_shapes=[pltpu.VMEM((nb, STATE_ROWS, 2 * STATE_COLS), F32),
                        pltpu.VMEM((nb, HALO, POOL_W), F32),
                        pltpu.VMEM((nb, t_blk, SSM_W), F32),
                        pltpu.VMEM((SSM_W, SSM_W), F32)]
                       + _state_scratch(nb, t_blk),
        compiler_params=_params(dimension_semantics=("arbitrary",)),
    )(z3, dy3, states, states, *kept, pool_w, pool_scale, lbr, lbi, wb, wc, d_skip, glu_w, glu_b)


def _mesh_place():
    x, y, c = lax.axis_index("x"), lax.axis_index("y"), lax.axis_index("c")
    return x, y, c


def _flip(place, k):
    x, y, c = place
    return (1 - x if k & 4 else x, 1 - y if k & 2 else y, 1 - c if k & 1 else c)


def _index(place):
    x, y, c = place
    return 4 * x + 2 * y + c


HBM_SPEC = pl.BlockSpec(memory_space=pltpu.HBM)
SEM_SPEC = pl.BlockSpec(memory_space=pltpu.SEMAPHORE)
_EFFECT = pltpu.SideEffectType.DATAFLOW_SIDE_EFFECTING
N_PEERS = N_DEV - 1


def _exchange_copies(src_refs, land_refs, send_sems, recv_sems):
    me = _mesh_place()
    mine = _index(me)
    out = []
    for a, land_ref in enumerate(land_refs):
        for k in range(1, N_DEV):
            peer = _flip(me, k)
            theirs = _index(peer)
            n = a * N_PEERS + k - 1
            src = src_refs[a].at[theirs] if src_refs else land_ref.at[mine]
            send = pltpu.make_async_remote_copy(
                src_ref=src, dst_ref=land_ref.at[mine], send_sem=send_sems.at[n], recv_sem=recv_sems.at[n],
                device_id=peer, device_id_type=MESH)
            recv = pltpu.make_async_remote_copy(
                src_ref=src, dst_ref=land_ref.at[theirs], send_sem=send_sems.at[n], recv_sem=recv_sems.at[n],
                device_id=peer, device_id_type=MESH)
            out.append((send, recv))
    return out


def _exchange_start(srcs, lands, after, name):
    arrays = tuple(srcs) + tuple(lands)
    n_src, n_all = len(srcs), len(arrays)
    n_copies = len(lands) * N_PEERS

    def body(*refs):
        send_sems, recv_sems = refs[n_all + 1], refs[n_all + 2]
        token = refs[-1]
        for send, _ in _exchange_copies(refs[:n_src], refs[n_src:n_all], send_sems, recv_sems):
            send.start()
        token[...] = jnp.zeros_like(token)

    res = pl.pallas_call(
        body, name=name,
        in_specs=[HBM_SPEC] * n_all + [ANY_SPEC],
        out_specs=[SEM_SPEC, SEM_SPEC] + [HBM_SPEC] * n_all + [VMEM_SPEC],
        out_shape=[pltpu.SemaphoreType.DMA((n_copies,)), pltpu.SemaphoreType.DMA((n_copies,))]
                  + [pltpu.HBM(a.shape, a.dtype) for a in arrays] + [jax.ShapeDtypeStruct((SUBLANES, LANES), F32)],
        input_output_aliases={i: 2 + i for i in range(n_all)},
        compiler_params=pltpu.CompilerParams(has_side_effects=_EFFECT),
    )(*[pltpu.with_memory_space_constraint(a, pltpu.HBM) for a in arrays], after)
    return tuple(res[:-1]), res[-1]


def _exchange_wait(handle, n_lands, after, name):
    send_sems, recv_sems = handle[0], handle[1]
    arrays = handle[2:]
    n_all = len(arrays)
    n_src = n_all - n_lands

    def body(*refs):
        for send, recv in _exchange_copies(refs[:n_src], refs[n_src:n_all], refs[n_all], refs[n_all + 1]):
            send.wait_send()
            recv.wait_recv()

    res = pl.pallas_call(
        body, name=name,
        in_specs=[HBM_SPEC] * n_all + [SEM_SPEC, SEM_SPEC, ANY_SPEC],
        out_specs=[HBM_SPEC] * n_all,
        out_shape=[pltpu.HBM(a.shape, a.dtype) for a in arrays],
        input_output_aliases={i: i for i in range(n_all)},
        compiler_params=pltpu.CompilerParams(has_side_effects=_EFFECT),
    )(*arrays, send_sems, recv_sems, after)
    return tuple(res[:n_src]), tuple(res[n_src:])


def _weight_zones(w_in, glu_w, w_out, my_idx):
    shards = (w_in, glu_w, w_out)
    depth = w_in.shape[0]

    def body(idx_ref, *refs):
        ins, zones = refs[:len(shards)], refs[len(shards):]
        for l in range(depth):
            for a, src in enumerate(ins):
                zones[l * len(shards) + a][0] = _mx(src[l])

    whole = lambda s: pl.BlockSpec(s.shape, lambda i, idx: (0,) * s.ndim)
    return pl.pallas_call(
        body, name="weight_zones",
        grid_spec=pltpu.PrefetchScalarGridSpec(
            num_scalar_prefetch=1, grid=(1,),
            in_specs=[whole(s) for s in shards],
            out_specs=[pl.BlockSpec((1,) + s.shape[1:], lambda i, idx: (idx[0], 0, 0))
                       for _ in range(depth) for s in shards]),
        out_shape=[jax.ShapeDtypeStruct((N_DEV,) + s.shape[1:], MXU_DTYPE) for _ in range(depth) for s in shards],
        compiler_params=_params(dimension_semantics=("arbitrary",)),
    )(my_idx.reshape(1).astype(jnp.int32), *shards)


def _allreduce_packed(p):
    rows = p.shape[0]
    half = rows // 2
    quarter = half // 4

    def body(p_ref, o_ref, part_ref, sib_ref, got_ref, send_sems, recv_sems):
        x, y, c = _mesh_place()
        sibling = (x, y, 1 - c)
        chip = 2 * x + y
        chips = [(k, (1 - x if k & 2 else x, 1 - y if k & 1 else y, c), chip ^ k) for k in (1, 2, 3)]
        my_half = pl.multiple_of(c * half, SUBLANES)
        other_half = pl.multiple_of((1 - c) * half, SUBLANES)

        def copy(n, src, dst, to):
            return pltpu.make_async_remote_copy(src_ref=src, dst_ref=dst, send_sem=send_sems.at[n],
                                                recv_sem=recv_sems.at[n], device_id=to, device_id_type=MESH)

        def quarter_of(ref, base, q):
            return ref.at[pl.ds(pl.multiple_of(base + q * quarter, SUBLANES), quarter)]

        swap = copy(0, p_ref.at[pl.ds(other_half, half)], sib_ref, sibling)
        swap.start()
        swap.wait()
        part_ref[...] = p_ref[pl.ds(my_half, half), :] + sib_ref[...]

        scatter = [copy(k, quarter_of(part_ref, 0, q), got_ref.at[k - 1], to) for k, to, q in chips]
        for cp in scatter:
            cp.start()
        total = part_ref[pl.ds(pl.multiple_of(chip * quarter, SUBLANES), quarter), :]
        for cp, (k, _, _) in zip(scatter, chips):
            cp.wait()
            total = total + got_ref[k - 1]
        mine = pl.multiple_of(my_half + chip * quarter, SUBLANES)
        o_ref[pl.ds(mine, quarter), :] = total

        gather = [copy(3 + k, o_ref.at[pl.ds(mine, quarter)], o_ref.at[pl.ds(mine, quarter)], to) for k, to, _ in chips]
        for cp in gather:
            cp.start()
        for k, to, q in chips:
            theirs = quarter_of(o_ref, my_half, q)
            copy(3 + k, theirs, theirs, to).wait_recv()
        for cp in gather:
            cp.wait_send()

        back = copy(7, o_ref.at[pl.ds(my_half, half)], o_ref.at[pl.ds(my_half, half)], sibling)
        back.start()
        copy(7, o_ref.at[pl.ds(other_half, half)], o_ref.at[pl.ds(other_half, half)], sibling).wait_recv()
        back.wait_send()

    return pl.pallas_call(
        body, name="comm_allreduce_packed",
        in_specs=[VMEM_SPEC],
        out_specs=VMEM_SPEC,
        out_shape=jax.ShapeDtypeStruct(p.shape, F32),
        scratch_shapes=[pltpu.VMEM((half, LANES), F32),
                        pltpu.VMEM((half, LANES), F32),
                        pltpu.VMEM((3, quarter, LANES), F32),
                        pltpu.SemaphoreType.DMA((8,)),
                        pltpu.SemaphoreType.DMA((8,))],
        compiler_params=_params(),
    )(p)


def _adamw_math(w, g, m, v):
    m = ADAM_B1 * m + (1.0 - ADAM_B1) * g
    v = ADAM_B2 * v + (1.0 - ADAM_B2) * (g * g)
    m_hat = m / (1.0 - ADAM_B1 ** ADAM_STEP)
    v_hat = v / (1.0 - ADAM_B2 ** ADAM_STEP)
    delta = -ADAM_LR * (m_hat / (jnp.sqrt(v_hat) + ADAM_EPS) + ADAM_WD * w)
    return delta, m, v


def _adamw_summed(received, own, my_idx, w, m, v, name):
    depth, r, c = w.shape
    tr = min(r, 128)

    def body(idx_ref, *refs):
        r_refs, o_refs = refs[:depth], refs[depth:2 * depth]
        w_ref, m_ref, v_ref, g_ref, d_ref, nm_ref, nv_ref = refs[2 * depth:]
        me = idx_ref[0]
        for l in range(depth):
            g = jnp.zeros((tr, c), F32)
            for q in range(N_DEV):
                g = g + jnp.where(q == me, o_refs[l][0], r_refs[l][q]).astype(F32)
            g_ref[l] = g
            d_ref[l], nm_ref[l], nv_ref[l] = _adamw_math(w_ref[l], g, m_ref[l], v_ref[l])

    blk = pl.BlockSpec((depth, tr, c), lambda i, idx: (0, i, 0))
    return pl.pallas_call(
        body, name=name,
        grid_spec=pltpu.PrefetchScalarGridSpec(
            num_scalar_prefetch=1, grid=(r // tr,),
            in_specs=[pl.BlockSpec((N_DEV, tr, c), lambda i, idx: (0, i, 0))] * depth
                     + [pl.BlockSpec((1, tr, c), lambda i, idx: (idx[0], i, 0))] * depth
                     + [blk, blk, blk],
            out_specs=[blk] * 4),
        out_shape=[jax.ShapeDtypeStruct((depth, r, c), F32)] * 4,
        compiler_params=_params(dimension_semantics=("arbitrary",)),
    )(my_idx.reshape(1).astype(jnp.int32), *received, *own, w, m, v)


def _adamw_small(ws, gs, ms, vs):
    n = len(ws)
    depth = ws[0].shape[0]

    def spec(a):
        per_layer = a.shape[0] == depth
        rest = (0,) * (a.ndim - 1)
        return pl.BlockSpec((1,) + a.shape[1:], lambda l: ((l if per_layer else 0),) + rest)

    def body(*refs):
        w_refs, g_refs, m_refs, v_refs = (refs[k * n:(k + 1) * n] for k in range(4))
        d_refs, nm_refs, nv_refs = (refs[(4 + k) * n:(5 + k) * n] for k in range(3))
        for k in range(n):
            d_refs[k][...], nm_refs[k][...], nv_refs[k][...] = _adamw_math(
                w_refs[k][...], g_refs[k][...], m_refs[k][...], v_refs[k][...])

    specs = [spec(a) for a in ws]
    shapes = [jax.ShapeDtypeStruct(a.shape, F32) for a in ws]
    res = pl.pallas_call(
        body, name="adamw_small",
        grid=(depth,),
        in_specs=specs * 4,
        out_specs=specs * 3,
        out_shape=shapes * 3,
        compiler_params=_params(dimension_semantics=("arbitrary",)),
    )(*ws, *gs, *ms, *vs)
    return res[:n], res[n:2 * n], res[2 * n:]


_PACK_ROWS = SUBLANES * N_DEV


def _pack(arrays):
    flat = jnp.concatenate([a.reshape(-1) for a in arrays])
    per = _PACK_ROWS * LANES
    total = -(-flat.shape[0] // per) * per
    flat = jnp.pad(flat, (0, total - flat.shape[0]))
    return flat.reshape(total // LANES, LANES)


def _unpack(packed, like):
    flat = packed.reshape(-1)
    out = []
    off = 0
    for a in like:
        out.append(flat[off:off + a.size].reshape(a.shape))
        off += a.size
    return out


def kernel(x, norm_g, w_in, pool_w, pool_scale, a_re, a_im, log_dt, b_re, b_im, c_re, c_im, d_skip, glu_w, glu_b, w_out, final_g, loss_target, m_norm_g, m_w_in, m_pool_w, m_pool_scale, m_a_re, m_a_im, m_log_dt, m_b_re, m_b_im, m_c_re, m_c_im, m_d_skip, m_glu_w, m_glu_b, m_w_out, m_final_g, v_norm_g, v_w_in, v_pool_w, v_pool_scale, v_a_re, v_a_im, v_log_dt, v_b_re, v_b_im, v_c_re, v_c_im, v_d_skip, v_glu_w, v_glu_b, v_w_out, v_final_g):
    nb, seq, _ = x.shape
    n_tok = nb * seq
    depth = norm_g.shape[0]

    my_idx = _index(_mesh_place())

    zones = _weight_zones(w_in, glu_w, w_out, my_idx)

    def gather_start(l, after):
        return _exchange_start((), zones[3 * l:3 * l + 3], after, f"comm_gather_start_{l}")

    def gather_wait(handle, after, l):
        _, (win, glu, wout) = _exchange_wait(handle, 3, after, f"comm_gather_wait_{l}")
        return win, glu.reshape(SSM_W, SSM_W), wout.reshape(MIX, D_MODEL)

    xs = [x.reshape(n_tok, D_MODEL)]
    first_w_in, dep = _exchange_start((), zones[0:1], xs[0], "comm_gather_start_0_w_in")

    (lbr, lbi, rb, rc), dense_vjp = jax.vjp(jax.vmap(_ssm_dense), a_re, a_im, log_dt + dep[0, 0], b_re, b_im, c_re, c_im)
    chunk_all = jax.vmap(_ssm_chunked)
    (wb, wct), chunk_vjp = jax.vjp(lambda p, q: (chunk_all(p), chunk_all(q)), rb, rc)
    wb_m, wct_m = _mx(wb), _mx(wct)
    pool_w_m = _mx(pool_w)
    rows_of = lambda a: a[:, None, :]
    norm_rows, scale_rows, skip_rows, bias_rows = rows_of(norm_g), rows_of(pool_scale), rows_of(d_skip), rows_of(glu_b)

    def layer_params(l):
        return (pool_w_m, scale_rows, lbr, lbi, wb_m, wct_m, skip_rows, weights[l][1], bias_rows)

    saved = []
    weights = []
    for l in range(depth):
        if l == 0:
            _, (win,) = _exchange_wait(first_w_in, 1, wct_m, "comm_gather_wait_0_w_in")
            rest, dep = _exchange_start((), zones[1:3], win, "comm_gather_start_0_rest")
            z, h = _inproj_fwd(xs[-1], norm_rows, win, dep, l)
            _, (glu, wout) = _exchange_wait(rest, 2, z, "comm_gather_wait_0_rest")
            weights.append((win, glu.reshape(SSM_W, SSM_W), wout.reshape(MIX, D_MODEL)))
            handle, dep = gather_start(1, weights[0][2])
            z3 = z.reshape(nb, seq, 2 * MIX)
            yg, states, *kept, x_next = _layer_fwd(xs[-1].reshape(nb, seq, D_MODEL), z3, None, None,
                                                   *layer_params(l), weights[l][2], dep, l)
        else:
            weights.append(gather_wait(handle, xs[-1], l))
            if l + 1 < depth:
                handle, dep = gather_start(l + 1, weights[l][0])
            z3, h3, yg, states, *kept, x_next = _layer_fwd(xs[-1].reshape(nb, seq, D_MODEL), None, norm_rows,
                                                           weights[l][0], *layer_params(l), weights[l][2], dep, l)
            h = h3.reshape(n_tok, D_MODEL)
        xs.append(x_next.reshape(n_tok, D_MODEL))
        saved.append((z3, h, yg.reshape(n_tok, MIX), states, kept))

    dx, loss_part, d_final_g = _loss_head(xs[-1], loss_target.reshape(n_tok, D_MODEL), final_g[None])

    small = {k: [None] * depth for k in
             ("norm_g", "pool_w", "pool_scale", "lbr", "lbi", "wb", "wct", "d_skip", "glu_b")}
    received = [None] * depth
    sent = [None] * depth
    pending = None
    early = None
    for l in reversed(range(depth)):
        z3, h, yg2, states, kept = saved[l]
        dy, d_wout = _outproj_bwd(dx, yg2, weights[l][2], dep)
        (dz, d_pw, d_ps, d_lbr, d_lbi, d_wb, d_wct, d_dsk, d_gw, d_gb) = _mixer_bwd(
            z3, dy.reshape(nb, seq, MIX), states, kept, *layer_params(l), l)
        rest = (d_gw.reshape(N_DEV, SSM_W // N_DEV, SSM_W), d_wout.reshape(N_DEV, MIX // N_DEV, D_MODEL))
        if l == 0:
            early, dep = _exchange_start(rest, tuple(lax.empty(s.shape, s.dtype) for s in rest), dz,
                                         "comm_grads_start_0_rest")
        dx, d_win, d_ng = _inproj_bwd(dz.reshape(n_tok, 2 * MIX), h, xs[l], dx, norm_rows, weights[l][0], dep, l)
        for k, val in (("norm_g", d_ng[0]), ("pool_w", d_pw), ("pool_scale", d_ps[0]), ("lbr", d_lbr),
                       ("lbi", d_lbi), ("wb", d_wb), ("wct", d_wct), ("d_skip", d_dsk[0]), ("glu_b", d_gb[0])):
            small[k][l] = val
        if pending is not None:
            sent[l + 1], received[l + 1] = _exchange_wait(pending, 3, dx, f"comm_grads_wait_{l + 1}")
        srcs = (d_win,) if l == 0 else (d_win,) + rest
        lands = tuple(lax.empty(s.shape, s.dtype) for s in srcs)
        pending, dep = _exchange_start(srcs, lands, dx, f"comm_grads_start_{l}")
    stack = lambda k: jnp.stack(small[k])
    d_rb, d_rc = chunk_vjp((stack("wb"), stack("wct")))
    local = [stack("norm_g"), stack("pool_w"), stack("pool_scale"), stack("lbr"), stack("lbi"), d_rb, d_rc,
             stack("d_skip"), stack("glu_b"), d_final_g[0] + dep[0, 0], loss_part[0]]
    (g_norm_g, g_pool_w, g_pool_scale, g_lbr, g_lbi, g_rb, g_rc, g_d_skip, g_glu_b, g_final_g, loss) = _unpack(
        _allreduce_packed(_pack(local)), local)
    loss = loss[0]
    g_a_re, g_a_im, g_log_dt, g_b_re, g_b_im, g_c_re, g_c_im = dense_vjp((g_lbr, g_lbi, g_rb, g_rc))

    names = ["norm_g", "pool_w", "pool_scale", "a_re", "a_im", "log_dt", "b_re", "b_im", "c_re", "c_im",
             "d_skip", "glu_b", "final_g"]
    rows = {"norm_g", "pool_scale", "log_dt", "d_skip", "glu_b"}
    small_w = [norm_g, pool_w, pool_scale, a_re, a_im, log_dt, b_re, b_im, c_re, c_im, d_skip, glu_b, final_g]
    small_g = [g_norm_g, g_pool_w, g_pool_scale, g_a_re, g_a_im, g_log_dt, g_b_re, g_b_im, g_c_re, g_c_im,
               g_d_skip, g_glu_b, g_final_g]
    small_m = [m_norm_g, m_pool_w, m_pool_scale, m_a_re, m_a_im, m_log_dt, m_b_re, m_b_im, m_c_re, m_c_im,
               m_d_skip, m_glu_b, m_final_g]
    small_v = [v_norm_g, v_pool_w, v_pool_scale, v_a_re, v_a_im, v_log_dt, v_b_re, v_b_im, v_c_re, v_c_im,
               v_d_skip, v_glu_b, v_final_g]

    wide_last = {"b_re", "b_im"}

    def blocked(arrays):
        return [a.reshape(1, 1, -1) if n == "final_g" else a[:, None, :] if n in rows
                else a.swapaxes(2, 3) if n in wide_last else a for n, a in zip(names, arrays)]

    small_d, small_nm, small_nv = _adamw_small(blocked(small_w), blocked(small_g), blocked(small_m), blocked(small_v))
    res = {}
    for kind, arrays in (("grad", small_g), ("delta", small_d), ("m", small_nm), ("v", small_nv)):
        for n, a, like in zip(names, arrays, small_w):
            if kind != "grad" and n in wide_last:
                a = a.swapaxes(2, 3)
            res[kind, n] = a.reshape(like.shape)

    (s_win,), (r_win,) = _exchange_wait(pending, 1, small_d[0], "comm_grads_wait_0")
    (s_glu, s_wout), (r_glu, r_wout) = _exchange_wait(early, 2, small_d[0], "comm_grads_wait_0_rest")
    sent[0], received[0] = (s_win, s_glu, s_wout), (r_win, r_glu, r_wout)
    shard_res = {}
    for pos, (n, w, m, v) in enumerate((("w_in", w_in, m_w_in, v_w_in), ("glu_w", glu_w, m_glu_w, v_glu_w),
                                        ("w_out", w_out, m_w_out, v_w_out))):
        shard_res[n] = _adamw_summed([received[l][pos] for l in range(depth)], [sent[l][pos] for l in range(depth)],
                                     my_idx, w, m, v, "adamw_" + n)
    for n in ("w_in", "glu_w", "w_out"):
        for pos, kind in enumerate(("grad", "delta", "m", "v")):
            res[kind, n] = shard_res[n][pos]

    order = ["norm_g", "w_in", "pool_w", "pool_scale", "a_re", "a_im", "log_dt", "b_re", "b_im", "c_re", "c_im",
             "d_skip", "glu_w", "glu_b", "w_out", "final_g"]
    outs = [loss, dx.reshape(nb, seq, D_MODEL)]
    for kind in ("grad", "delta", "m", "v"):
        outs += [res[kind, n] for n in order]
    return tuple(outs)
```

```python
import math

import jax
import jax.numpy as jnp
from jax import lax
from jax.experimental import pallas as pl
from jax.experimental.pallas import tpu as pltpu

F32 = jnp.float32
MXU_DTYPE = jnp.bfloat16

D_MODEL = 1024
MIX = 1024
POOL_W = 512
SSM_W = 512
N_POOL_G = 4
POOL_GC = 128
SSM_C = 16
SSM_P = 64
NORM_EPS = 1e-5
N_DEV = 8
W_IN_COLS = 2 * MIX // N_DEV

ADAM_LR = 0.001
ADAM_B1 = 0.9
ADAM_B2 = 0.999
ADAM_EPS = 1e-08
ADAM_WD = 0.01
ADAM_STEP = 10

SUBLANES = 8
LANES = 128
HALO = 16
STATE_ROWS = 8
STATE_COLS = 256
CHUNK_GROUPS = STATE_COLS // SSM_P
CHUNK_CH = CHUNK_GROUPS * SSM_C
T_BLK = 256
SCAN_UNROLL = 16
TM_FWD = 512
TM_BWD = 512
VMEM_LIMIT = 56 * 1024 * 1024

MESH = pl.DeviceIdType.MESH
VMEM_SPEC = pl.BlockSpec(memory_space=pltpu.VMEM)
ANY_SPEC = pl.BlockSpec(memory_space=pl.ANY)


def _mm(a, b):
    return jnp.dot(a, b, preferred_element_type=F32)


def _mm_tn(a, b):
    return lax.dot_general(a, b, (((0,), (0,)), ((), ())), preferred_element_type=F32)


def _mm_nt(a, b):
    return lax.dot_general(a, b, (((1,), (1,)), ((), ())), preferred_element_type=F32)


def _mx(a):
    return a.astype(MXU_DTYPE)


def _sigmoid(v):
    return 1.0 / (1.0 + jnp.exp(-v))


_GELU_C = math.sqrt(2.0 / math.pi)
_GELU_A = 0.044715


def _gelu_and_grad(y):
    th = jnp.tanh(_GELU_C * (y + _GELU_A * y * y * y))
    val = 0.5 * y * (1.0 + th)
    grad = 0.5 * (1.0 + th) + 0.5 * y * (1.0 - th * th) * (_GELU_C * (1.0 + 3.0 * _GELU_A * y * y))
    return val, grad


def _params(**kw):
    return pltpu.CompilerParams(vmem_limit_bytes=VMEM_LIMIT, **kw)


def _of_layer(layer, *shape):
    return pl.BlockSpec((None,) + shape, lambda i: (layer,) + (0,) * len(shape))


def _ssm_dense(a_re, a_im, log_dt, b_re, b_im, c_re, c_im):
    dt = jnp.exp(log_dt)[:, None]
    mag = jnp.exp(a_re * dt)
    ang = a_im * dt
    lb_re = mag * jnp.cos(ang)
    lb_im = mag * jnp.sin(ang)
    den = a_re * a_re + a_im * a_im
    n_re = lb_re - 1.0
    n_im = lb_im
    f_re = (n_re * a_re + n_im * a_im) / den
    f_im = (n_im * a_re - n_re * a_im) / den
    bb_re = f_re[..., None] * b_re - f_im[..., None] * b_im
    bb_im = f_re[..., None] * b_im + f_im[..., None] * b_re

    bb = jnp.stack([bb_re, bb_im], axis=0).reshape(2, STATE_ROWS, CHUNK_GROUPS, SSM_P, SSM_C)
    rb = bb.transpose(1, 4, 0, 2, 3).reshape(STATE_ROWS, SSM_C, 2 * STATE_COLS)
    cc = jnp.stack([c_re, -c_im], axis=0).reshape(2, STATE_ROWS, CHUNK_GROUPS, SSM_C, SSM_P)
    rc = cc.transpose(1, 3, 0, 2, 4).reshape(STATE_ROWS, SSM_C, 2 * STATE_COLS)
    return (lb_re.reshape(STATE_ROWS, STATE_COLS), lb_im.reshape(STATE_ROWS, STATE_COLS), rb, rc)


def _ssm_chunked(per_channel):
    row_group = jnp.arange(CHUNK_CH) // SSM_C
    col_group = (jnp.arange(2 * STATE_COLS) // SSM_P) % CHUNK_GROUPS
    own_group = (row_group[:, None] == col_group[None, :]).astype(F32)
    even = (jnp.arange(STATE_ROWS) % 2 == 0).astype(F32)[:, None, None]
    half = jnp.tile(per_channel, (1, CHUNK_GROUPS, 1)) * own_group
    return jnp.concatenate([half * even, half * (1.0 - even)], axis=1)


def _inproj_fwd(x2, g_rows, w_all, dep, layer):
    n = x2.shape[0]
    tm = TM_FWD

    def body(x_ref, g_ref, w_ref, dep_ref, z_ref, h_ref):
        x = x_ref[...]
        r = lax.rsqrt(jnp.mean(x * x, axis=-1, keepdims=True) + NORM_EPS)
        h = _mx(x * r * g_ref[...])
        h_ref[...] = h
        for d in range(N_DEV):
            z_ref[:, d * W_IN_COLS:(d + 1) * W_IN_COLS] = _mm(h, w_ref[d])

    return pl.pallas_call(
        body, name="inproj_fwd",
        grid=(n // tm,),
        in_specs=[pl.BlockSpec((tm, D_MODEL), lambda i: (i, 0)),
                  _of_layer(layer, 1, D_MODEL),
                  pl.BlockSpec((N_DEV, D_MODEL, W_IN_COLS), lambda i: (0, 0, 0)),
                  ANY_SPEC],
        out_specs=[pl.BlockSpec((tm, 2 * MIX), lambda i: (i, 0)),
                   pl.BlockSpec((tm, D_MODEL), lambda i: (i, 0))],
        out_shape=[jax.ShapeDtypeStruct((n, 2 * MIX), F32),
                   jax.ShapeDtypeStruct((n, D_MODEL), MXU_DTYPE)],
        compiler_params=_params(dimension_semantics=("arbitrary",)),
    )(x2, g_rows, w_all, dep)


def _loss_head(x2, tgt2, g_row):
    n = x2.shape[0]
    tm = TM_FWD

    def body(x_ref, t_ref, g_ref, dx_ref, loss_ref, dg_ref):
        @pl.when(pl.program_id(0) == 0)
        def _():
            loss_ref[...] = jnp.zeros_like(loss_ref)
            dg_ref[...] = jnp.zeros_like(dg_ref)

        x = x_ref[...]
        g = g_ref[...]
        r = lax.rsqrt(jnp.mean(x * x, axis=-1, keepdims=True) + NORM_EPS)
        xh = x * r
        e = xh * g - t_ref[...]
        loss_ref[...] += jnp.sum(jnp.sum(e * e, axis=-1, keepdims=True), axis=0, keepdims=True) * (0.5 / D_MODEL)
        dout = e * (1.0 / D_MODEL)
        dg_ref[...] += jnp.sum(dout * xh, axis=0, keepdims=True)
        gdy = dout * g
        dx_ref[...] = r * (gdy - xh * jnp.mean(xh * gdy, axis=-1, keepdims=True))

    return pl.pallas_call(
        body, name="loss_head",
        grid=(n // tm,),
        in_specs=[pl.BlockSpec((tm, D_MODEL), lambda i: (i, 0)),
                  pl.BlockSpec((tm, D_MODEL), lambda i: (i, 0)),
                  pl.BlockSpec((1, D_MODEL), lambda i: (0, 0))],
        out_specs=[pl.BlockSpec((tm, D_MODEL), lambda i: (i, 0)),
                   pl.BlockSpec((1, 1), lambda i: (0, 0)),
                   pl.BlockSpec((1, D_MODEL), lambda i: (0, 0))],
        out_shape=[jax.ShapeDtypeStruct((n, D_MODEL), F32),
                   jax.ShapeDtypeStruct((1, 1), F32),
                   jax.ShapeDtypeStruct((1, D_MODEL), F32)],
        compiler_params=_params(dimension_semantics=("arbitrary",)),
    )(x2, tgt2, g_row)


def _outproj_bwd(dx2, yg, w_out, dep):
    n = dx2.shape[0]
    tm = TM_BWD
    n_steps = n // tm

    def body(dx_ref, y_ref, w_ref, dep_ref, dy_ref, dw_ref, acc_ref):
        i = pl.program_id(0)

        @pl.when(i == 0)
        def _():
            acc_ref[...] = jnp.zeros_like(acc_ref)

        dxb = _mx(dx_ref[...])
        dy_ref[...] = _mm_nt(dxb, w_ref[...])
        acc_ref[...] += _mm_tn(y_ref[...], dxb)

        @pl.when(i == n_steps - 1)
        def _():
            dw_ref[...] = _mx(acc_ref[...])

    return pl.pallas_call(
        body, name="outproj_bwd",
        grid=(n_steps,),
        in_specs=[pl.BlockSpec((tm, D_MODEL), lambda i: (i, 0)),
                  pl.BlockSpec((tm, MIX), lambda i: (i, 0)),
                  pl.BlockSpec((MIX, D_MODEL), lambda i: (0, 0)),
                  ANY_SPEC],
        out_specs=[pl.BlockSpec((tm, MIX), lambda i: (i, 0)),
                   pl.BlockSpec((MIX, D_MODEL), lambda i: (0, 0))],
        out_shape=[jax.ShapeDtypeStruct((n, MIX), F32),
                   jax.ShapeDtypeStruct((MIX, D_MODEL), MXU_DTYPE)],
        scratch_shapes=[pltpu.VMEM((MIX, D_MODEL), F32)],
        compiler_params=_params(dimension_semantics=("arbitrary",)),
    )(dx2, yg, w_out, dep)


def _inproj_bwd(dz, h, x2, dx_in, g_rows, w_all, dep, layer):
    n = x2.shape[0]
    tm = TM_BWD
    n_steps = n // tm

    def body(dz_ref, h_ref, x_ref, dxi_ref, g_ref, w_ref, dep_ref, dxo_ref, dw_ref, dg_ref, acc_ref, wcat_ref):
        i = pl.program_id(0)

        @pl.when(i == 0)
        def _():
            acc_ref[...] = jnp.zeros_like(acc_ref)
            dg_ref[...] = jnp.zeros_like(dg_ref)
            for d in range(N_DEV):
                wcat_ref[:, d * W_IN_COLS:(d + 1) * W_IN_COLS] = w_ref[d]

        hb = h_ref[...]
        for d in range(N_DEV):
            acc_ref[d] += _mm_tn(hb, dz_ref[:, d * W_IN_COLS:(d + 1) * W_IN_COLS])
        dh = _mm_nt(dz_ref[...], wcat_ref[...])
        x = x_ref[...]
        r = lax.rsqrt(jnp.mean(x * x, axis=-1, keepdims=True) + NORM_EPS)
        xh = x * r
        dg_ref[...] += jnp.sum(dh * xh, axis=0, keepdims=True)
        gdy = dh * g_ref[...]
        dxo_ref[...] = dxi_ref[...] + r * (gdy - xh * jnp.mean(xh * gdy, axis=-1, keepdims=True))

        @pl.when(i == n_steps - 1)
        def _():
            dw_ref[...] = _mx(acc_ref[...])

    return pl.pallas_call(
        body, name="inproj_bwd",
        grid=(n_steps,),
        in_specs=[pl.BlockSpec((tm, 2 * MIX), lambda i: (i, 0)),
                  pl.BlockSpec((tm, D_MODEL), lambda i: (i, 0)),
                  pl.BlockSpec((tm, D_MODEL), lambda i: (i, 0)),
                  pl.BlockSpec((tm, D_MODEL), lambda i: (i, 0)),
                  _of_layer(layer, 1, D_MODEL),
                  pl.BlockSpec((N_DEV, D_MODEL, W_IN_COLS), lambda i: (0, 0, 0)),
                  ANY_SPEC],
        out_specs=[pl.BlockSpec((tm, D_MODEL), lambda i: (i, 0)),
                   pl.BlockSpec((N_DEV, D_MODEL, W_IN_COLS), lambda i: (0, 0, 0)),
                   pl.BlockSpec((1, D_MODEL), lambda i: (0, 0))],
        out_shape=[jax.ShapeDtypeStruct((n, D_MODEL), F32),
                   jax.ShapeDtypeStruct((N_DEV, D_MODEL, W_IN_COLS), MXU_DTYPE),
                   jax.ShapeDtypeStruct((1, D_MODEL), F32)],
        scratch_shapes=[pltpu.VMEM((N_DEV, D_MODEL, W_IN_COLS), F32),
                        pltpu.VMEM((D_MODEL, 2 * MIX), MXU_DTYPE)],
        compiler_params=_params(dimension_semantics=("arbitrary",)),
    )(dz, h, x2, dx_in, g_rows, w_all, dep)


def _row_pos(t0, rows):
    return t0 + lax.broadcasted_iota(jnp.int32, (rows, LANES), 0)


def _pool_window_mean(upad, g, t0, t_blk):
    k = 2 << g
    w = upad
    sh = 1
    while sh < k:
        w = w + pltpu.roll(w, sh, 0)
        sh *= 2
    count = jnp.minimum(_row_pos(t0, t_blk) + 1, k).astype(F32)
    return w[HALO:] / count - upad[HALO:]


def _pool_window_bwd(qpad, g, t_blk):
    k = 2 << g
    n = t_blk + HALO
    w = qpad
    sh = 1
    while sh < k:
        w = w + pltpu.roll(w, n - sh, 0)
        sh *= 2
    return w[:t_blk]


class _StateBuf:
    def __init__(self, refs, t_blk):
        self.refs = refs
        self.t_blk = t_blk

    def put_chunk(self, b, j, val):
        for c in range(4):
            self.refs[4 * b + c][pl.ds(j, self.t_blk, stride=STATE_ROWS), :] = val[:, c * LANES:(c + 1) * LANES]

    def get_chunk(self, b, j):
        return jnp.concatenate(
            [self.refs[4 * b + c][pl.ds(j, self.t_blk, stride=STATE_ROWS), :] for c in range(4)], axis=-1)

    def load(self, b, r, part):
        return jnp.concatenate(
            [self.refs[4 * b + 2 * part + h][pl.ds(r, STATE_ROWS), :] for h in range(2)], axis=-1)

    def store(self, b, r, part, val):
        for h in range(2):
            self.refs[4 * b + 2 * part + h][pl.ds(r, STATE_ROWS), :] = val[:, h * LANES:(h + 1) * LANES]


def _state_scratch(nb, t_blk):
    return [pltpu.VMEM((t_blk * STATE_ROWS, LANES), F32) for _ in range(4 * nb)]


def _ssm_project_in(u_ssm, wb_ref, buf, nb):
    t_blk = u_ssm.shape[0] // nb
    ub = _mx(u_ssm)
    for j in range(STATE_ROWS):
        m = j // 2
        bu = _mm(ub[:, m * LANES:(m + 1) * LANES], wb_ref[j])
        for b in range(nb):
            buf.put_chunk(b, j, bu[b * t_blk:(b + 1) * t_blk])


def _scan_forward(buf, lbr, lbi, init, nb):
    def step(t, carry):
        r = pl.multiple_of(t * STATE_ROWS, STATE_ROWS)
        out = []
        for b in range(nb):
            sr, si = carry[2 * b], carry[2 * b + 1]
            nr = lbr * sr - lbi * si + buf.load(b, r, 0)
            ni = lbr * si + lbi * sr + buf.load(b, r, 1)
            buf.store(b, r, 0, nr)
            buf.store(b, r, 1, ni)
            out += [nr, ni]
        return tuple(out)

    def body(i, carry):
        for u in range(SCAN_UNROLL):
            carry = step(i * SCAN_UNROLL + u, carry)
        return carry

    return lax.fori_loop(0, buf.t_blk // SCAN_UNROLL, body, init)


def _ssm_project_out(chunk, wc_ref):
    tiles = []
    for m in range(4):
        acc = None
        for j in (2 * m, 2 * m + 1):
            part = _mm_nt(chunk(j), wc_ref[j])
            acc = part if acc is None else acc + part
        tiles.append(acc)
    return jnp.concatenate(tiles, axis=-1)


def _layer_fwd(x3, z3, g_rows, w_in, pool_w, pool_scale, lbr, lbi, wb, wc, d_skip, glu_w, glu_b, w_out, dep, layer):
    nb, seq, _ = x3.shape
    t_blk = min(T_BLK, seq)
    n_t = seq // t_blk
    halo_per_blk = t_blk // HALO
    rows = nb * t_blk
    fused = z3 is None

    def body(*refs):
        if fused:
            (x_ref, g_ref, wi_ref, pw_ref, ps_ref, lbr_ref, lbi_ref, wb_ref, wc_ref, dsk_ref, gw_ref, gb_ref, wo_ref,
             dep_ref, z_ref, h_ref, yg_ref, sc_ref, act_ref, dact_ref, pooled_ref, ypre_ref, xo_ref,
             carry_ref, halo_ref, *s_refs) = refs
        else:
            (x_ref, z_ref, zh_ref, pw_ref, ps_ref, lbr_ref, lbi_ref, wb_ref, wc_ref, dsk_ref, gw_ref, gb_ref, wo_ref,
             dep_ref, yg_ref, sc_ref, act_ref, dact_ref, pooled_ref, ypre_ref, xo_ref, carry_ref, *s_refs) = refs
        i = pl.program_id(0)
        t0 = i * t_blk
        buf = _StateBuf(s_refs, t_blk)
        both = lambda lo, hi: z_ref[:, :, lo:hi].reshape(rows, hi - lo)

        @pl.when(i == 0)
        def _():
            carry_ref[...] = jnp.zeros_like(carry_ref)
            if fused:
                halo_ref[...] = jnp.zeros_like(halo_ref)

        x = x_ref[...].reshape(rows, D_MODEL)
        if fused:
            r = lax.rsqrt(jnp.mean(x * x, axis=-1, keepdims=True) + NORM_EPS)
            h = _mx(x * r * g_ref[...])
            h_ref[...] = h.reshape(nb, t_blk, D_MODEL)
            for d in range(N_DEV):
                z_ref[:, :, d * W_IN_COLS:(d + 1) * W_IN_COLS] = _mm(h, wi_ref[d]).reshape(nb, t_blk, W_IN_COLS)

        u_ssm = both(POOL_W, MIX)
        _ssm_project_in(u_ssm, wb_ref, buf, nb)
        init = tuple(carry_ref[b, :, h * STATE_COLS:(h + 1) * STATE_COLS] for b in range(nb) for h in range(2))
        fin = _scan_forward(buf, lbr_ref[...], lbi_ref[...], init, nb)
        for b in range(nb):
            carry_ref[b, :, 0:STATE_COLS] = fin[2 * b]
            carry_ref[b, :, STATE_COLS:2 * STATE_COLS] = fin[2 * b + 1]

        def chunk(j):
            states = _mx(jnp.concatenate([buf.get_chunk(b, j) for b in range(nb)], axis=0))
            sc_ref[:, j] = states.reshape(nb, t_blk, 2 * STATE_COLS)
            return states

        y = _ssm_project_out(chunk, wc_ref) + dsk_ref[...] * u_ssm
        yg, dgelu = _gelu_and_grad(y)
        ygb = _mx(yg)
        act_ref[...] = ygb.reshape(nb, t_blk, SSM_W)
        dact_ref[...] = _mx(dgelu).reshape(nb, t_blk, SSM_W)
        o_ssm = yg * _sigmoid(_mm(ygb, gw_ref[...]) + gb_ref[...])
        gp = both(MIX + POOL_W, 2 * MIX)
        parts = []
        first = (i == 0)
        for g in range(N_POOL_G):
            cols = slice(g * POOL_GC, (g + 1) * POOL_GC)
            pooled = []
            for b in range(nb):
                halo = halo_ref[b, :, cols] if fused else jnp.where(first, 0.0, zh_ref[b, :, cols])
                pooled.append(_pool_window_mean(jnp.concatenate([halo, z_ref[b, :, cols]], axis=0), g, t0, t_blk))
            pb = _mx(jnp.concatenate(pooled, axis=0))
            ypre = _mm(pb, pw_ref[g])
            pooled_ref[:, :, cols] = pb.reshape(nb, t_blk, POOL_GC)
            ypre_ref[:, :, cols] = ypre.reshape(nb, t_blk, POOL_GC)
            gpp = both(MIX + g * POOL_GC, MIX + (g + 1) * POOL_GC)
            parts.append(_mx(ypre * ps_ref[:, cols] * (gpp * _sigmoid(gpp))))
        parts.append(_mx(o_ssm * (gp * _sigmoid(gp))))
        gated = jnp.concatenate(parts, axis=-1)
        yg_ref[...] = gated.reshape(nb, t_blk, MIX)
        xo_ref[...] = (x + _mm(gated, wo_ref[...])).reshape(nb, t_blk, D_MODEL)
        if fused:
            halo_ref[...] = z_ref[:, t_blk - HALO:, 0:POOL_W]

    const = lambda *shape: pl.BlockSpec(shape, lambda i: (0,) * len(shape))
    tokens = lambda width: pl.BlockSpec((nb, t_blk, width), lambda i: (0, i, 0))
    mixer_specs = [_of_layer(layer, N_POOL_G, POOL_GC, POOL_GC), _of_layer(layer, 1, POOL_W),
                   _of_layer(layer, STATE_ROWS, STATE_COLS), _of_layer(layer, STATE_ROWS, STATE_COLS),
                   _of_layer(layer, STATE_ROWS, LANES, 2 * STATE_COLS),
                   _of_layer(layer, STATE_ROWS, LANES, 2 * STATE_COLS),
                   _of_layer(layer, 1, SSM_W), const(SSM_W, SSM_W), _of_layer(layer, 1, SSM_W),
                   const(MIX, D_MODEL), ANY_SPEC]
    mixer_args = (pool_w, pool_scale, lbr, lbi, wb, wc, d_skip, glu_w, glu_b, w_out, dep)
    out_specs = [tokens(MIX), pl.BlockSpec((nb, STATE_ROWS, t_blk, 2 * STATE_COLS), lambda i: (0, 0, i, 0)),
                 tokens(SSM_W), tokens(SSM_W), tokens(POOL_W), tokens(POOL_W), tokens(D_MODEL)]
    out_shape = [jax.ShapeDtypeStruct((nb, seq, MIX), MXU_DTYPE),
                 jax.ShapeDtypeStruct((nb, STATE_ROWS, seq, 2 * STATE_COLS), MXU_DTYPE),
                 jax.ShapeDtypeStruct((nb, seq, SSM_W), MXU_DTYPE),
                 jax.ShapeDtypeStruct((nb, seq, SSM_W), MXU_DTYPE),
                 jax.ShapeDtypeStruct((nb, seq, POOL_W), MXU_DTYPE),
                 jax.ShapeDtypeStruct((nb, seq, POOL_W), F32),
                 jax.ShapeDtypeStruct((nb, seq, D_MODEL), F32)]
    scratch = [pltpu.VMEM((nb, STATE_ROWS, 2 * STATE_COLS), F32)]
    if fused:
        in_specs = [tokens(D_MODEL), _of_layer(layer, 1, D_MODEL), const(N_DEV, D_MODEL, W_IN_COLS)] + mixer_specs
        args = (x3, g_rows, w_in) + mixer_args
        out_specs = [tokens(2 * MIX), tokens(D_MODEL)] + out_specs
        out_shape = [jax.ShapeDtypeStruct((nb, seq, 2 * MIX), F32),
                     jax.ShapeDtypeStruct((nb, seq, D_MODEL), MXU_DTYPE)] + out_shape
        scratch = scratch + [pltpu.VMEM((nb, HALO, POOL_W), F32)]
    else:
        in_specs = [tokens(D_MODEL), tokens(2 * MIX),
                    pl.BlockSpec((nb, HALO, POOL_W), lambda i: (0, jnp.maximum(i * halo_per_blk - 1, 0), 0))] + mixer_specs
        args = (x3, z3, z3) + mixer_args
    return pl.pallas_call(
        body, name="layer_fwd" if fused else "mixer_fwd",
        grid=(n_t,),
        in_specs=in_specs, out_specs=out_specs, out_shape=out_shape,
        scratch_shapes=scratch + _state_scratch(nb, t_blk),
        compiler_params=_params(dimension_semantics=("arbitrary",)),
    )(*args)


def _mixer_bwd(z3, dy3, states, kept, pool_w, pool_scale, lbr, lbi, wb, wc, d_skip, glu_w, glu_b, layer):
    nb, seq, _ = z3.shape
    t_blk = min(T_BLK, seq)
    n_t = seq // t_blk
    halo_per_blk = t_blk // HALO
    rows = nb * t_blk

    def body(z_ref, dy_ref, sc_ref, sch_ref, act_ref, dact_ref, pooled_ref, ypre_ref, pw_ref, ps_ref, lbr_ref, lbi_ref, wb_ref, wc_ref, dsk_ref,
             gw_ref, gb_ref,
             dz_ref, dpw_ref, dps_ref, dlbr_ref, dlbi_ref, dwb_ref, dwc_ref, ddsk_ref, dgw_ref, dgb_ref,
             gcarry_ref, qcarry_ref, du_ref, dgw_acc, *g_refs):
        i = pl.program_id(0)
        blk = n_t - 1 - i
        t0 = blk * t_blk
        gbuf = _StateBuf(g_refs, t_blk)

        @pl.when(i == 0)
        def _():
            gcarry_ref[...] = jnp.zeros_like(gcarry_ref)
            qcarry_ref[...] = jnp.zeros_like(qcarry_ref)
            for ref in (dpw_ref, dps_ref, dlbr_ref, dlbi_ref, dwb_ref, dwc_ref, ddsk_ref, dgw_acc, dgb_ref):
                ref[...] = jnp.zeros_like(ref)

        lbr_v = lbr_ref[...]
        lbi_v = lbi_ref[...]

        both = lambda ref, lo, hi: ref[:, :, lo:hi].reshape(rows, hi - lo)
        split = lambda val: val.reshape(nb, t_blk, val.shape[-1])
        states = lambda j: sc_ref[:, j].reshape(rows, 2 * STATE_COLS)
        first = (blk == 0)

        u_ssm = both(z_ref, POOL_W, MIX)
        ygb = act_ref[...].reshape(rows, SSM_W)
        yg = ygb.astype(F32)
        dgelu = dact_ref[...].reshape(rows, SSM_W).astype(F32)
        sg = _sigmoid(_mm(ygb, gw_ref[...]) + gb_ref[...])
        o_ssm = yg * sg
        gp = both(z_ref, MIX + POOL_W, 2 * MIX)
        sgm = _sigmoid(gp)
        dyv = both(dy_ref, POOL_W, MIX)
        dz_ref[:, :, MIX + POOL_W:2 * MIX] = split(_mx(dyv * o_ssm * (sgm * (1.0 + gp * (1.0 - sgm)))))
        do = dyv * (gp * sgm)
        dv = do * yg * (sg * (1.0 - sg))
        dvb = _mx(dv)
        dgb_ref[...] += jnp.sum(dv, axis=0, keepdims=True)
        dgw_acc[...] += _mm_tn(ygb, dvb)
        dyp = (do * sg + _mm_nt(dvb, gw_ref[...])) * dgelu
        ddsk_ref[...] += jnp.sum(dyp * u_ssm, axis=0, keepdims=True)
        dypb = _mx(dyp)
        for j in range(STATE_ROWS):
            m = j // 2
            dyt = dypb[:, m * LANES:(m + 1) * LANES]
            ds = _mm(dyt, wc_ref[j])
            for b in range(nb):
                gbuf.put_chunk(b, j, ds[b * t_blk:(b + 1) * t_blk])
            dwc_ref[j] += _mm_tn(dyt, states(j))
        du_ref[...] = split(dsk_ref[...] * dyp)

        for g in range(N_POOL_G):
            cols = slice(g * POOL_GC, (g + 1) * POOL_GC)
            pb = both(pooled_ref, g * POOL_GC, (g + 1) * POOL_GC)
            ypre = both(ypre_ref, g * POOL_GC, (g + 1) * POOL_GC)
            gpp = both(z_ref, MIX + g * POOL_GC, MIX + (g + 1) * POOL_GC)
            sgp = _sigmoid(gpp)
            dyg = both(dy_ref, g * POOL_GC, (g + 1) * POOL_GC)
            scale = ps_ref[:, cols]
            dz_ref[:, :, MIX + g * POOL_GC:MIX + (g + 1) * POOL_GC] = split(_mx(
                dyg * (ypre * scale) * (sgp * (1.0 + gpp * (1.0 - sgp)))))
            dyc = dyg * (gpp * sgp)
            dps_ref[:, cols] += jnp.sum(dyc * ypre, axis=0, keepdims=True)
            dypre = _mx(dyc * scale)
            dpw_ref[g] += _mm_tn(pb, dypre)
            dpooled = _mm_nt(dypre, pw_ref[g])
            count = jnp.minimum(_row_pos(t0, t_blk) + 1, 2 << g).astype(F32)
            for b in range(nb):
                dp = dpooled[b * t_blk:(b + 1) * t_blk]
                q = dp / count
                qpad = jnp.concatenate([q, qcarry_ref[b, :, cols]], axis=0)
                qcarry_ref[b, :, cols] = q[:HALO]
                dz_ref[b, :, cols] = _mx(_pool_window_bwd(qpad, g, t_blk) - dp)

        def rev_step(t, carry):
            r = pl.multiple_of(t * STATE_ROWS, STATE_ROWS)
            out = []
            for b in range(nb):
                gr, gi = carry[2 * b], carry[2 * b + 1]
                ngr = lbr_v * gr + lbi_v * gi + gbuf.load(b, r, 0)
                ngi = lbr_v * gi - lbi_v * gr + gbuf.load(b, r, 1)
                gbuf.store(b, r, 0, ngr)
                gbuf.store(b, r, 1, ngi)
                out += [ngr, ngi]
            return tuple(out)

        def rev_body(i, carry):
            for u in range(SCAN_UNROLL):
                carry = rev_step(t_blk - 1 - (i * SCAN_UNROLL + u), carry)
            return carry

        init_g = tuple(gcarry_ref[b, :, h * STATE_COLS:(h + 1) * STATE_COLS] for b in range(nb) for h in range(2))
        fin = lax.fori_loop(0, t_blk // SCAN_UNROLL, rev_body, init_g)
        for b in range(nb):
            gcarry_ref[b, :, 0:STATE_COLS] = fin[2 * b]
            gcarry_ref[b, :, STATE_COLS:2 * STATE_COLS] = fin[2 * b + 1]

        ub = _mx(u_ssm)
        for m in range(4):
            acc = both(du_ref, m * LANES, (m + 1) * LANES)
            for j in (2 * m, 2 * m + 1):
                g = jnp.concatenate([gbuf.get_chunk(b, j) for b in range(nb)], axis=0)
                gj = _mx(g)
                acc = acc + _mm_nt(gj, wb_ref[j])
                dwb_ref[j] += _mm_tn(ub[:, m * LANES:(m + 1) * LANES], gj)
                shifted = []
                for b in range(nb):
                    before = jnp.where(first, 0.0, sch_ref[b, j].astype(F32))
                    spad = jnp.concatenate([before, sc_ref[b, j].astype(F32)], axis=0)
                    shifted.append(pltpu.roll(spad, 1, 0)[HALO:])
                s_prev = jnp.concatenate(shifted, axis=0)
                g_re, g_im = g[:, :STATE_COLS], g[:, STATE_COLS:]
                p_re, p_im = s_prev[:, :STATE_COLS], s_prev[:, STATE_COLS:]
                dlbr_ref[j:j + 1, :] += jnp.sum(g_re * p_re + g_im * p_im, axis=0, keepdims=True)
                dlbi_ref[j:j + 1, :] += jnp.sum(g_im * p_re - g_re * p_im, axis=0, keepdims=True)
            dz_ref[:, :, POOL_W + m * LANES:POOL_W + (m + 1) * LANES] = split(_mx(acc))

        @pl.when(i == n_t - 1)
        def _():
            dgw_ref[...] = _mx(dgw_acc[...])

    const = lambda *shape: pl.BlockSpec(shape, lambda i: (0,) * len(shape))
    rev = lambda i: n_t - 1 - i
    out_shape = [jax.ShapeDtypeStruct((nb, seq, 2 * MIX), MXU_DTYPE),
                 jax.ShapeDtypeStruct((N_POOL_G, POOL_GC, POOL_GC), F32),
                 jax.ShapeDtypeStruct((1, POOL_W), F32),
                 jax.ShapeDtypeStruct((STATE_ROWS, STATE_COLS), F32),
                 jax.ShapeDtypeStruct((STATE_ROWS, STATE_COLS), F32),
                 jax.ShapeDtypeStruct((STATE_ROWS, LANES, 2 * STATE_COLS), F32),
                 jax.ShapeDtypeStruct((STATE_ROWS, LANES, 2 * STATE_COLS), F32),
                 jax.ShapeDtypeStruct((1, SSM_W), F32),
                 jax.ShapeDtypeStruct((SSM_W, SSM_W), MXU_DTYPE),
                 jax.ShapeDtypeStruct((1, SSM_W), F32)]
    return pl.pallas_call(
        body, name="mixer_bwd",
        grid=(n_t,),
        in_specs=[pl.BlockSpec((nb, t_blk, 2 * MIX), lambda i: (0, rev(i), 0)),
                  pl.BlockSpec((nb, t_blk, MIX), lambda i: (0, rev(i), 0)),
                  pl.BlockSpec((nb, STATE_ROWS, t_blk, 2 * STATE_COLS), lambda i: (0, 0, rev(i), 0)),
                  pl.BlockSpec((nb, STATE_ROWS, HALO, 2 * STATE_COLS),
                               lambda i: (0, 0, jnp.maximum(rev(i) * halo_per_blk - 1, 0), 0)),
                  pl.BlockSpec((nb, t_blk, SSM_W), lambda i: (0, rev(i), 0)),
                  pl.BlockSpec((nb, t_blk, SSM_W), lambda i: (0, rev(i), 0)),
                  pl.BlockSpec((nb, t_blk, POOL_W), lambda i: (0, rev(i), 0)),
                  pl.BlockSpec((nb, t_blk, POOL_W), lambda i: (0, rev(i), 0)),
                  _of_layer(layer, N_POOL_G, POOL_GC, POOL_GC), _of_layer(layer, 1, POOL_W),
                  _of_layer(layer, STATE_ROWS, STATE_COLS), _of_layer(layer, STATE_ROWS, STATE_COLS),
                  _of_layer(layer, STATE_ROWS, LANES, 2 * STATE_COLS),
                  _of_layer(layer, STATE_ROWS, LANES, 2 * STATE_COLS),
                  _of_layer(layer, 1, SSM_W), const(SSM_W, SSM_W), _of_layer(layer, 1, SSM_W)],
        out_specs=[pl.BlockSpec((nb, t_blk, 2 * MIX), lambda i: (0, rev(i), 0))]
                  + [const(*s.shape) for s in out_shape[1:]],
        out_shape=out_shape,
        scratch_shapes=[pltpu.VMEM((nb, STATE_ROWS, 2 * STATE_COLS), F32),
                        pltpu.VMEM((nb, HALO, POOL_W), F32),
                        pltpu.VMEM((nb, t_blk, SSM_W), F32),
                        pltpu.VMEM((SSM_W, SSM_W), F32)]
                       + _state_scratch(nb, t_blk),
        compiler_params=_params(dimension_semantics=("arbitrary",)),
    )(z3, dy3, states, states, *kept, pool_w, pool_scale, lbr, lbi, wb, wc, d_skip, glu_w, glu_b)


def _mesh_place():
    x, y, c = lax.axis_index("x"), lax.axis_index("y"), lax.axis_index("c")
    return x, y, c


def _flip(place, k):
    x, y, c = place
    return (1 - x if k & 4 else x, 1 - y if k & 2 else y, 1 - c if k & 1 else c)


def _index(place):
    x, y, c = place
    return 4 * x + 2 * y + c


HBM_SPEC = pl.BlockSpec(memory_space=pltpu.HBM)
SEM_SPEC = pl.BlockSpec(memory_space=pltpu.SEMAPHORE)
_EFFECT = pltpu.SideEffectType.DATAFLOW_SIDE_EFFECTING
N_PEERS = N_DEV - 1


def _exchange_copies(src_refs, land_refs, send_sems, recv_sems):
    me = _mesh_place()
    mine = _index(me)
    out = []
    for a, land_ref in enumerate(land_refs):
        for k in range(1, N_DEV):
            peer = _flip(me, k)
            theirs = _index(peer)
            n = a * N_PEERS + k - 1
            src = src_refs[a].at[theirs] if src_refs else land_ref.at[mine]
            send = pltpu.make_async_remote_copy(
                src_ref=src, dst_ref=land_ref.at[mine], send_sem=send_sems.at[n], recv_sem=recv_sems.at[n],
                device_id=peer, device_id_type=MESH)
            recv = pltpu.make_async_remote_copy(
                src_ref=src, dst_ref=land_ref.at[theirs], send_sem=send_sems.at[n], recv_sem=recv_sems.at[n],
                device_id=peer, device_id_type=MESH)
            out.append((send, recv))
    return out


def _exchange_start(srcs, lands, after, name):
    arrays = tuple(srcs) + tuple(lands)
    n_src, n_all = len(srcs), len(arrays)
    n_copies = len(lands) * N_PEERS

    def body(*refs):
        send_sems, recv_sems = refs[n_all + 1], refs[n_all + 2]
        token = refs[-1]
        for send, _ in _exchange_copies(refs[:n_src], refs[n_src:n_all], send_sems, recv_sems):
            send.start()
        token[...] = jnp.zeros_like(token)

    res = pl.pallas_call(
        body, name=name,
        in_specs=[HBM_SPEC] * n_all + [ANY_SPEC],
        out_specs=[SEM_SPEC, SEM_SPEC] + [HBM_SPEC] * n_all + [VMEM_SPEC],
        out_shape=[pltpu.SemaphoreType.DMA((n_copies,)), pltpu.SemaphoreType.DMA((n_copies,))]
                  + [pltpu.HBM(a.shape, a.dtype) for a in arrays] + [jax.ShapeDtypeStruct((SUBLANES, LANES), F32)],
        input_output_aliases={i: 2 + i for i in range(n_all)},
        compiler_params=pltpu.CompilerParams(has_side_effects=_EFFECT),
    )(*[pltpu.with_memory_space_constraint(a, pltpu.HBM) for a in arrays], after)
    return tuple(res[:-1]), res[-1]


def _exchange_wait(handle, n_lands, after, name):
    send_sems, recv_sems = handle[0], handle[1]
    arrays = handle[2:]
    n_all = len(arrays)
    n_src = n_all - n_lands

    def body(*refs):
        for send, recv in _exchange_copies(refs[:n_src], refs[n_src:n_all], refs[n_all], refs[n_all + 1]):
            send.wait_send()
            recv.wait_recv()

    res = pl.pallas_call(
        body, name=name,
        in_specs=[HBM_SPEC] * n_all + [SEM_SPEC, SEM_SPEC, ANY_SPEC],
        out_specs=[HBM_SPEC] * n_all,
        out_shape=[pltpu.HBM(a.shape, a.dtype) for a in arrays],
        input_output_aliases={i: i for i in range(n_all)},
        compiler_params=pltpu.CompilerParams(has_side_effects=_EFFECT),
    )(*arrays, send_sems, recv_sems, after)
    return tuple(res[:n_src]), tuple(res[n_src:])


def _gather_now(zone):
    def body(zone_ref, out_ref, send_sems, recv_sems):
        me = _mesh_place()
        sibling = _flip(me, 1)
        chips = [2, 4, 6]

        def copy(n, block, to):
            slot = out_ref.at[_index(block)]
            return pltpu.make_async_remote_copy(src_ref=slot, dst_ref=slot, send_sem=send_sems.at[n],
                                                recv_sem=recv_sems.at[n], device_id=to, device_id_type=MESH)

        started = [copy(0, me, sibling)] + [copy(1 + n, me, _flip(me, k)) for n, k in enumerate(chips)]
        for cp in started:
            cp.start()
        for n, k in enumerate(chips):
            copy(1 + n, _flip(me, k), me).wait_recv()
            passed_on = copy(4 + n, _flip(me, k), sibling)
            passed_on.start()
            started.append(passed_on)
        copy(0, sibling, me).wait_recv()
        for n, k in enumerate(chips):
            copy(4 + n, _flip(sibling, k), me).wait_recv()
        for cp in started:
            cp.wait_send()

    return pl.pallas_call(
        body, name="comm_gather_now",
        in_specs=[ANY_SPEC],
        out_specs=ANY_SPEC,
        out_shape=jax.ShapeDtypeStruct(zone.shape, zone.dtype),
        input_output_aliases={0: 0},
        scratch_shapes=[pltpu.SemaphoreType.DMA((N_PEERS,)), pltpu.SemaphoreType.DMA((N_PEERS,))],
        compiler_params=_params(),
    )(zone)


def _weight_zones(w_in, glu_w, w_out, my_idx):
    shards = (w_in, glu_w, w_out)
    depth = w_in.shape[0]

    def body(idx_ref, *refs):
        ins, zones = refs[:len(shards)], refs[len(shards):]
        for l in range(depth):
            for a, src in enumerate(ins):
                zones[l * len(shards) + a][0] = _mx(src[l])

    whole = lambda s: pl.BlockSpec(s.shape, lambda i, idx: (0,) * s.ndim)
    return pl.pallas_call(
        body, name="weight_zones",
        grid_spec=pltpu.PrefetchScalarGridSpec(
            num_scalar_prefetch=1, grid=(1,),
            in_specs=[whole(s) for s in shards],
            out_specs=[pl.BlockSpec((1,) + s.shape[1:], lambda i, idx: (idx[0], 0, 0))
                       for _ in range(depth) for s in shards]),
        out_shape=[jax.ShapeDtypeStruct((N_DEV,) + s.shape[1:], MXU_DTYPE) for _ in range(depth) for s in shards],
        compiler_params=_params(dimension_semantics=("arbitrary",)),
    )(my_idx.reshape(1).astype(jnp.int32), *shards)


def _allreduce_packed(p):
    rows = p.shape[0]
    half = rows // 2
    quarter = half // 4

    def body(p_ref, o_ref, part_ref, sib_ref, got_ref, send_sems, recv_sems):
        x, y, c = _mesh_place()
        sibling = (x, y, 1 - c)
        chip = 2 * x + y
        chips = [(k, (1 - x if k & 2 else x, 1 - y if k & 1 else y, c), chip ^ k) for k in (1, 2, 3)]
        my_half = pl.multiple_of(c * half, SUBLANES)
        other_half = pl.multiple_of((1 - c) * half, SUBLANES)

        def copy(n, src, dst, to):
            return pltpu.make_async_remote_copy(src_ref=src, dst_ref=dst, send_sem=send_sems.at[n],
                                                recv_sem=recv_sems.at[n], device_id=to, device_id_type=MESH)

        def quarter_of(ref, base, q):
            return ref.at[pl.ds(pl.multiple_of(base + q * quarter, SUBLANES), quarter)]

        swap = copy(0, p_ref.at[pl.ds(other_half, half)], sib_ref, sibling)
        swap.start()
        swap.wait()
        part_ref[...] = p_ref[pl.ds(my_half, half), :] + sib_ref[...]

        scatter = [copy(k, quarter_of(part_ref, 0, q), got_ref.at[k - 1], to) for k, to, q in chips]
        for cp in scatter:
            cp.start()
        total = part_ref[pl.ds(pl.multiple_of(chip * quarter, SUBLANES), quarter), :]
        for cp, (k, _, _) in zip(scatter, chips):
            cp.wait()
            total = total + got_ref[k - 1]
        mine = pl.multiple_of(my_half + chip * quarter, SUBLANES)
        o_ref[pl.ds(mine, quarter), :] = total

        gather = [copy(3 + k, o_ref.at[pl.ds(mine, quarter)], o_ref.at[pl.ds(mine, quarter)], to) for k, to, _ in chips]
        for cp in gather:
            cp.start()
        for k, to, q in chips:
            theirs = quarter_of(o_ref, my_half, q)
            copy(3 + k, theirs, theirs, to).wait_recv()
        for cp in gather:
            cp.wait_send()

        back = copy(7, o_ref.at[pl.ds(my_half, half)], o_ref.at[pl.ds(my_half, half)], sibling)
        back.start()
        copy(7, o_ref.at[pl.ds(other_half, half)], o_ref.at[pl.ds(other_half, half)], sibling).wait_recv()
        back.wait_send()

    return pl.pallas_call(
        body, name="comm_allreduce_packed",
        in_specs=[VMEM_SPEC],
        out_specs=VMEM_SPEC,
        out_shape=jax.ShapeDtypeStruct(p.shape, F32),
        scratch_shapes=[pltpu.VMEM((half, LANES), F32),
                        pltpu.VMEM((half, LANES), F32),
                        pltpu.VMEM((3, quarter, LANES), F32),
                        pltpu.SemaphoreType.DMA((8,)),
                        pltpu.SemaphoreType.DMA((8,))],
        compiler_params=_params(),
    )(p)


def _adamw_math(w, g, m, v):
    m = ADAM_B1 * m + (1.0 - ADAM_B1) * g
    v = ADAM_B2 * v + (1.0 - ADAM_B2) * (g * g)
    m_hat = m / (1.0 - ADAM_B1 ** ADAM_STEP)
    v_hat = v / (1.0 - ADAM_B2 ** ADAM_STEP)
    delta = -ADAM_LR * (m_hat / (jnp.sqrt(v_hat) + ADAM_EPS) + ADAM_WD * w)
    return delta, m, v


def _adamw_summed(received, own, my_idx, w, m, v, name):
    depth, r, c = w.shape
    tr = min(r, 128)

    def body(idx_ref, *refs):
        r_refs, o_refs = refs[:depth], refs[depth:2 * depth]
        w_ref, m_ref, v_ref, g_ref, d_ref, nm_ref, nv_ref = refs[2 * depth:]
        me = idx_ref[0]
        for l in range(depth):
            g = jnp.zeros((tr, c), F32)
            for q in range(N_DEV):
                g = g + jnp.where(q == me, o_refs[l][0], r_refs[l][q]).astype(F32)
            g_ref[l] = g
            d_ref[l], nm_ref[l], nv_ref[l] = _adamw_math(w_ref[l], g, m_ref[l], v_ref[l])

    blk = pl.BlockSpec((depth, tr, c), lambda i, idx: (0, i, 0))
    return pl.pallas_call(
        body, name=name,
        grid_spec=pltpu.PrefetchScalarGridSpec(
            num_scalar_prefetch=1, grid=(r // tr,),
            in_specs=[pl.BlockSpec((N_DEV, tr, c), lambda i, idx: (0, i, 0))] * depth
                     + [pl.BlockSpec((1, tr, c), lambda i, idx: (idx[0], i, 0))] * depth
                     + [blk, blk, blk],
            out_specs=[blk] * 4),
        out_shape=[jax.ShapeDtypeStruct((depth, r, c), F32)] * 4,
        compiler_params=_params(dimension_semantics=("arbitrary",)),
    )(my_idx.reshape(1).astype(jnp.int32), *received, *own, w, m, v)


def _adamw_small(ws, gs, ms, vs):
    n = len(ws)
    depth = ws[0].shape[0]

    def spec(a):
        per_layer = a.shape[0] == depth
        rest = (0,) * (a.ndim - 1)
        return pl.BlockSpec((1,) + a.shape[1:], lambda l: ((l if per_layer else 0),) + rest)

    def body(*refs):
        w_refs, g_refs, m_refs, v_refs = (refs[k * n:(k + 1) * n] for k in range(4))
        d_refs, nm_refs, nv_refs = (refs[(4 + k) * n:(5 + k) * n] for k in range(3))
        for k in range(n):
            d_refs[k][...], nm_refs[k][...], nv_refs[k][...] = _adamw_math(
                w_refs[k][...], g_refs[k][...], m_refs[k][...], v_refs[k][...])

    specs = [spec(a) for a in ws]
    shapes = [jax.ShapeDtypeStruct(a.shape, F32) for a in ws]
    res = pl.pallas_call(
        body, name="adamw_small",
        grid=(depth,),
        in_specs=specs * 4,
        out_specs=specs * 3,
        out_shape=shapes * 3,
        compiler_params=_params(dimension_semantics=("arbitrary",)),
    )(*ws, *gs, *ms, *vs)
    return res[:n], res[n:2 * n], res[2 * n:]


_PACK_ROWS = SUBLANES * N_DEV


def _pack(arrays):
    flat = jnp.concatenate([a.reshape(-1) for a in arrays])
    per = _PACK_ROWS * LANES
    total = -(-flat.shape[0] // per) * per
    flat = jnp.pad(flat, (0, total - flat.shape[0]))
    return flat.reshape(total // LANES, LANES)


def _unpack(packed, like):
    flat = packed.reshape(-1)
    out = []
    off = 0
    for a in like:
        out.append(flat[off:off + a.size].reshape(a.shape))
        off += a.size
    return out


def kernel(x, norm_g, w_in, pool_w, pool_scale, a_re, a_im, log_dt, b_re, b_im, c_re, c_im, d_skip, glu_w, glu_b, w_out, final_g, loss_target, m_norm_g, m_w_in, m_pool_w, m_pool_scale, m_a_re, m_a_im, m_log_dt, m_b_re, m_b_im, m_c_re, m_c_im, m_d_skip, m_glu_w, m_glu_b, m_w_out, m_final_g, v_norm_g, v_w_in, v_pool_w, v_pool_scale, v_a_re, v_a_im, v_log_dt, v_b_re, v_b_im, v_c_re, v_c_im, v_d_skip, v_glu_w, v_glu_b, v_w_out, v_final_g):
    nb, seq, _ = x.shape
    n_tok = nb * seq
    depth = norm_g.shape[0]

    my_idx = _index(_mesh_place())

    zones = _weight_zones(w_in, glu_w, w_out, my_idx)

    def gather_start(l, after):
        return _exchange_start((), zones[3 * l:3 * l + 3], after, f"comm_gather_start_{l}")

    def gather_wait(handle, after, l):
        _, (win, glu, wout) = _exchange_wait(handle, 3, after, f"comm_gather_wait_{l}")
        return win, glu.reshape(SSM_W, SSM_W), wout.reshape(MIX, D_MODEL)

    xs = [x.reshape(n_tok, D_MODEL)]
    win = _gather_now(zones[0])
    rest, dep = _exchange_start((), zones[1:3], win, "comm_gather_start_0_rest")

    (lbr, lbi, rb, rc), dense_vjp = jax.vjp(jax.vmap(_ssm_dense), a_re, a_im, log_dt + dep[0, 0], b_re, b_im, c_re, c_im)
    chunk_all = jax.vmap(_ssm_chunked)
    (wb, wct), chunk_vjp = jax.vjp(lambda p, q: (chunk_all(p), chunk_all(q)), rb, rc)
    wb_m, wct_m = _mx(wb), _mx(wct)
    pool_w_m = _mx(pool_w)
    rows_of = lambda a: a[:, None, :]
    norm_rows, scale_rows, skip_rows, bias_rows = rows_of(norm_g), rows_of(pool_scale), rows_of(d_skip), rows_of(glu_b)

    def layer_params(l):
        return (pool_w_m, scale_rows, lbr, lbi, wb_m, wct_m, skip_rows, weights[l][1], bias_rows)

    saved = []
    weights = []
    for l in range(depth):
        if l == 0:
            z, h = _inproj_fwd(xs[-1], norm_rows, win, dep, l)
            _, (glu, wout) = _exchange_wait(rest, 2, z, "comm_gather_wait_0_rest")
            weights.append((win, glu.reshape(SSM_W, SSM_W), wout.reshape(MIX, D_MODEL)))
            handle, dep = gather_start(1, weights[0][2])
            z3 = z.reshape(nb, seq, 2 * MIX)
            yg, states, *kept, x_next = _layer_fwd(xs[-1].reshape(nb, seq, D_MODEL), z3, None, None,
                                                   *layer_params(l), weights[l][2], dep, l)
        else:
            weights.append(gather_wait(handle, xs[-1], l))
            if l + 1 < depth:
                handle, dep = gather_start(l + 1, weights[l][0])
            z3, h3, yg, states, *kept, x_next = _layer_fwd(xs[-1].reshape(nb, seq, D_MODEL), None, norm_rows,
                                                           weights[l][0], *layer_params(l), weights[l][2], dep, l)
            h = h3.reshape(n_tok, D_MODEL)
        xs.append(x_next.reshape(n_tok, D_MODEL))
        saved.append((z3, h, yg.reshape(n_tok, MIX), states, kept))

    dx, loss_part, d_final_g = _loss_head(xs[-1], loss_target.reshape(n_tok, D_MODEL), final_g[None])

    small = {k: [None] * depth for k in
             ("norm_g", "pool_w", "pool_scale", "lbr", "lbi", "wb", "wct", "d_skip", "glu_b")}
    received = [None] * depth
    sent = [None] * depth
    pending = None
    early = None
    for l in reversed(range(depth)):
        z3, h, yg2, states, kept = saved[l]
        dy, d_wout = _outproj_bwd(dx, yg2, weights[l][2], dep)
        (dz, d_pw, d_ps, d_lbr, d_lbi, d_wb, d_wct, d_dsk, d_gw, d_gb) = _mixer_bwd(
            z3, dy.reshape(nb, seq, MIX), states, kept, *layer_params(l), l)
        rest = (d_gw.reshape(N_DEV, SSM_W // N_DEV, SSM_W), d_wout.reshape(N_DEV, MIX // N_DEV, D_MODEL))
        if l == 0:
            early, dep = _exchange_start(rest, tuple(lax.empty(s.shape, s.dtype) for s in rest), dz,
                                         "comm_grads_start_0_rest")
        dx, d_win, d_ng = _inproj_bwd(dz.reshape(n_tok, 2 * MIX), h, xs[l], dx, norm_rows, weights[l][0], dep, l)
        for k, val in (("norm_g", d_ng[0]), ("pool_w", d_pw), ("pool_scale", d_ps[0]), ("lbr", d_lbr),
                       ("lbi", d_lbi), ("wb", d_wb), ("wct", d_wct), ("d_skip", d_dsk[0]), ("glu_b", d_gb[0])):
            small[k][l] = val
        if pending is not None:
            sent[l + 1], received[l + 1] = _exchange_wait(pending, 3, dx, f"comm_grads_wait_{l + 1}")
        srcs = (d_win,) if l == 0 else (d_win,) + rest
        lands = tuple(lax.empty(s.shape, s.dtype) for s in srcs)
        pending, dep = _exchange_start(srcs, lands, dx, f"comm_grads_start_{l}")
    stack = lambda k: jnp.stack(small[k])
    d_rb, d_rc = chunk_vjp((stack("wb"), stack("wct")))
    local = [stack("norm_g"), stack("pool_w"), stack("pool_scale"), stack("lbr"), stack("lbi"), d_rb, d_rc,
             stack("d_skip"), stack("glu_b"), d_final_g[0] + dep[0, 0], loss_part[0]]
    (g_norm_g, g_pool_w, g_pool_scale, g_lbr, g_lbi, g_rb, g_rc, g_d_skip, g_glu_b, g_final_g, loss) = _unpack(
        _allreduce_packed(_pack(local)), local)
    loss = loss[0]
    g_a_re, g_a_im, g_log_dt, g_b_re, g_b_im, g_c_re, g_c_im = dense_vjp((g_lbr, g_lbi, g_rb, g_rc))

    names = ["norm_g", "pool_w", "pool_scale", "a_re", "a_im", "log_dt", "b_re", "b_im", "c_re", "c_im",
             "d_skip", "glu_b", "final_g"]
    rows = {"norm_g", "pool_scale", "log_dt", "d_skip", "glu_b"}
    small_w = [norm_g, pool_w, pool_scale, a_re, a_im, log_dt, b_re, b_im, c_re, c_im, d_skip, glu_b, final_g]
    small_g = [g_norm_g, g_pool_w, g_pool_scale, g_a_re, g_a_im, g_log_dt, g_b_re, g_b_im, g_c_re, g_c_im,
               g_d_skip, g_glu_b, g_final_g]
    small_m = [m_norm_g, m_pool_w, m_pool_scale, m_a_re, m_a_im, m_log_dt, m_b_re, m_b_im, m_c_re, m_c_im,
               m_d_skip, m_glu_b, m_final_g]
    small_v = [v_norm_g, v_pool_w, v_pool_scale, v_a_re, v_a_im, v_log_dt, v_b_re, v_b_im, v_c_re, v_c_im,
               v_d_skip, v_glu_b, v_final_g]

    wide_last = {"b_re", "b_im"}

    def blocked(arrays):
        return [a.reshape(1, 1, -1) if n == "final_g" else a[:, None, :] if n in rows
                else a.swapaxes(2, 3) if n in wide_last else a for n, a in zip(names, arrays)]

    small_d, small_nm, small_nv = _adamw_small(blocked(small_w), blocked(small_g), blocked(small_m), blocked(small_v))
    res = {}
    for kind, arrays in (("grad", small_g), ("delta", small_d), ("m", small_nm), ("v", small_nv)):
        for n, a, like in zip(names, arrays, small_w):
            if kind != "grad" and n in wide_last:
                a = a.swapaxes(2, 3)
            res[kind, n] = a.reshape(like.shape)

    (s_win,), (r_win,) = _exchange_wait(pending, 1, small_d[0], "comm_grads_wait_0")
    (s_glu, s_wout), (r_glu, r_wout) = _exchange_wait(early, 2, small_d[0], "comm_grads_wait_0_rest")
    sent[0], received[0] = (s_win, s_glu, s_wout), (r_win, r_glu, r_wout)
    shard_res = {}
    for pos, (n, w, m, v) in enumerate((("w_in", w_in, m_w_in, v_w_in), ("glu_w", glu_w, m_glu_w, v_glu_w),
                                        ("w_out", w_out, m_w_out, v_w_out))):
        shard_res[n] = _adamw_summed([received[l][pos] for l in range(depth)], [sent[l][pos] for l in range(depth)],
                                     my_idx, w, m, v, "adamw_" + n)
    for n in ("w_in", "glu_w", "w_out"):
        for pos, kind in enumerate(("grad", "delta", "m", "v")):
            res[kind, n] = shard_res[n][pos]

    order = ["norm_g", "w_in", "pool_w", "pool_scale", "a_re", "a_im", "log_dt", "b_re", "b_im", "c_re", "c_im",
             "d_skip", "glu_w", "glu_b", "w_out", "final_g"]
    outs = [loss, dx.reshape(nb, seq, D_MODEL)]
    for kind in ("grad", "delta", "m", "v"):
        outs += [res[kind, n] for n in order]
    return tuple(outs)
```

```python
import math

import jax
import jax.numpy as jnp
from jax import lax
from jax.experimental import pallas as pl
from jax.experimental.pallas import tpu as pltpu

F32 = jnp.float32
MXU_DTYPE = jnp.bfloat16

D_MODEL = 1024
MIX = 1024
POOL_W = 512
SSM_W = 512
N_POOL_G = 4
POOL_GC = 128
SSM_C = 16
SSM_P = 64
NORM_EPS = 1e-5
N_DEV = 8
W_IN_COLS = 2 * MIX // N_DEV

ADAM_LR = 0.001
ADAM_B1 = 0.9
ADAM_B2 = 0.999
ADAM_EPS = 1e-08
ADAM_WD = 0.01
ADAM_STEP = 10

SUBLANES = 8
LANES = 128
HALO = 16
STATE_ROWS = 8
STATE_COLS = 256
CHUNK_GROUPS = STATE_COLS // SSM_P
CHUNK_CH = CHUNK_GROUPS * SSM_C
T_BLK = 256
SCAN_UNROLL = 16
TM_FWD = 512
TM_BWD = 512
VMEM_LIMIT = 56 * 1024 * 1024

MESH = pl.DeviceIdType.MESH
VMEM_SPEC = pl.BlockSpec(memory_space=pltpu.VMEM)
ANY_SPEC = pl.BlockSpec(memory_space=pl.ANY)


def _mm(a, b):
    return jnp.dot(a, b, preferred_element_type=F32)


def _mm_tn(a, b):
    return lax.dot_general(a, b, (((0,), (0,)), ((), ())), preferred_element_type=F32)


def _mm_nt(a, b):
    return lax.dot_general(a, b, (((1,), (1,)), ((), ())), preferred_element_type=F32)


def _mx(a):
    return a.astype(MXU_DTYPE)


def _sigmoid(v):
    return 1.0 / (1.0 + jnp.exp(-v))


_GELU_C = math.sqrt(2.0 / math.pi)
_GELU_A = 0.044715


def _gelu_and_grad(y):
    th = jnp.tanh(_GELU_C * (y + _GELU_A * y * y * y))
    val = 0.5 * y * (1.0 + th)
    grad = 0.5 * (1.0 + th) + 0.5 * y * (1.0 - th * th) * (_GELU_C * (1.0 + 3.0 * _GELU_A * y * y))
    return val, grad


def _params(**kw):
    return pltpu.CompilerParams(vmem_limit_bytes=VMEM_LIMIT, **kw)


def _of_layer(layer, *shape):
    return pl.BlockSpec((None,) + shape, lambda i: (layer,) + (0,) * len(shape))


def _ssm_dense(a_re, a_im, log_dt, b_re, b_im, c_re, c_im):
    dt = jnp.exp(log_dt)[:, None]
    mag = jnp.exp(a_re * dt)
    ang = a_im * dt
    lb_re = mag * jnp.cos(ang)
    lb_im = mag * jnp.sin(ang)
    den = a_re * a_re + a_im * a_im
    n_re = lb_re - 1.0
    n_im = lb_im
    f_re = (n_re * a_re + n_im * a_im) / den
    f_im = (n_im * a_re - n_re * a_im) / den
    bb_re = f_re[..., None] * b_re - f_im[..., None] * b_im
    bb_im = f_re[..., None] * b_im + f_im[..., None] * b_re

    bb = jnp.stack([bb_re, bb_im], axis=0).reshape(2, STATE_ROWS, CHUNK_GROUPS, SSM_P, SSM_C)
    rb = bb.transpose(1, 4, 0, 2, 3).reshape(STATE_ROWS, SSM_C, 2 * STATE_COLS)
    cc = jnp.stack([c_re, -c_im], axis=0).reshape(2, STATE_ROWS, CHUNK_GROUPS, SSM_C, SSM_P)
    rc = cc.transpose(1, 3, 0, 2, 4).reshape(STATE_ROWS, SSM_C, 2 * STATE_COLS)
    return (lb_re.reshape(STATE_ROWS, STATE_COLS), lb_im.reshape(STATE_ROWS, STATE_COLS), rb, rc)


def _ssm_chunked(per_channel):
    row_group = jnp.arange(CHUNK_CH) // SSM_C
    col_group = (jnp.arange(2 * STATE_COLS) // SSM_P) % CHUNK_GROUPS
    own_group = (row_group[:, None] == col_group[None, :]).astype(F32)
    even = (jnp.arange(STATE_ROWS) % 2 == 0).astype(F32)[:, None, None]
    half = jnp.tile(per_channel, (1, CHUNK_GROUPS, 1)) * own_group
    return jnp.concatenate([half * even, half * (1.0 - even)], axis=1)


def _inproj_fwd(x2, g_rows, w_all, dep, layer):
    n = x2.shape[0]
    tm = TM_FWD

    def body(x_ref, g_ref, w_ref, dep_ref, z_ref, h_ref):
        x = x_ref[...]
        r = lax.rsqrt(jnp.mean(x * x, axis=-1, keepdims=True) + NORM_EPS)
        h = _mx(x * r * g_ref[...])
        h_ref[...] = h
        for d in range(N_DEV):
            z_ref[:, d * W_IN_COLS:(d + 1) * W_IN_COLS] = _mm(h, w_ref[d])

    return pl.pallas_call(
        body, name="inproj_fwd",
        grid=(n // tm,),
        in_specs=[pl.BlockSpec((tm, D_MODEL), lambda i: (i, 0)),
                  _of_layer(layer, 1, D_MODEL),
                  pl.BlockSpec((N_DEV, D_MODEL, W_IN_COLS), lambda i: (0, 0, 0)),
                  ANY_SPEC],
        out_specs=[pl.BlockSpec((tm, 2 * MIX), lambda i: (i, 0)),
                   pl.BlockSpec((tm, D_MODEL), lambda i: (i, 0))],
        out_shape=[jax.ShapeDtypeStruct((n, 2 * MIX), F32),
                   jax.ShapeDtypeStruct((n, D_MODEL), MXU_DTYPE)],
        compiler_params=_params(dimension_semantics=("arbitrary",)),
    )(x2, g_rows, w_all, dep)


def _loss_head(x2, tgt2, g_row):
    n = x2.shape[0]
    tm = TM_FWD

    def body(x_ref, t_ref, g_ref, dx_ref, loss_ref, dg_ref):
        @pl.when(pl.program_id(0) == 0)
        def _():
            loss_ref[...] = jnp.zeros_like(loss_ref)
            dg_ref[...] = jnp.zeros_like(dg_ref)

        x = x_ref[...]
        g = g_ref[...]
        r = lax.rsqrt(jnp.mean(x * x, axis=-1, keepdims=True) + NORM_EPS)
        xh = x * r
        e = xh * g - t_ref[...]
        loss_ref[...] += jnp.sum(jnp.sum(e * e, axis=-1, keepdims=True), axis=0, keepdims=True) * (0.5 / D_MODEL)
        dout = e * (1.0 / D_MODEL)
        dg_ref[...] += jnp.sum(dout * xh, axis=0, keepdims=True)
        gdy = dout * g
        dx_ref[...] = r * (gdy - xh * jnp.mean(xh * gdy, axis=-1, keepdims=True))

    return pl.pallas_call(
        body, name="loss_head",
        grid=(n // tm,),
        in_specs=[pl.BlockSpec((tm, D_MODEL), lambda i: (i, 0)),
                  pl.BlockSpec((tm, D_MODEL), lambda i: (i, 0)),
                  pl.BlockSpec((1, D_MODEL), lambda i: (0, 0))],
        out_specs=[pl.BlockSpec((tm, D_MODEL), lambda i: (i, 0)),
                   pl.BlockSpec((1, 1), lambda i: (0, 0)),
                   pl.BlockSpec((1, D_MODEL), lambda i: (0, 0))],
        out_shape=[jax.ShapeDtypeStruct((n, D_MODEL), F32),
                   jax.ShapeDtypeStruct((1, 1), F32),
                   jax.ShapeDtypeStruct((1, D_MODEL), F32)],
        compiler_params=_params(dimension_semantics=("arbitrary",)),
    )(x2, tgt2, g_row)


def _outproj_bwd(dx2, yg, w_out, dep):
    n = dx2.shape[0]
    tm = TM_BWD
    n_steps = n // tm

    def body(dx_ref, y_ref, w_ref, dep_ref, dy_ref, dw_ref, acc_ref):
        i = pl.program_id(0)

        @pl.when(i == 0)
        def _():
            acc_ref[...] = jnp.zeros_like(acc_ref)

        dxb = _mx(dx_ref[...])
        dy_ref[...] = _mm_nt(dxb, w_ref[...])
        acc_ref[...] += _mm_tn(y_ref[...], dxb)

        @pl.when(i == n_steps - 1)
        def _():
            dw_ref[...] = _mx(acc_ref[...])

    return pl.pallas_call(
        body, name="outproj_bwd",
        grid=(n_steps,),
        in_specs=[pl.BlockSpec((tm, D_MODEL), lambda i: (i, 0)),
                  pl.BlockSpec((tm, MIX), lambda i: (i, 0)),
                  pl.BlockSpec((MIX, D_MODEL), lambda i: (0, 0)),
                  ANY_SPEC],
        out_specs=[pl.BlockSpec((tm, MIX), lambda i: (i, 0)),
                   pl.BlockSpec((MIX, D_MODEL), lambda i: (0, 0))],
        out_shape=[jax.ShapeDtypeStruct((n, MIX), F32),
                   jax.ShapeDtypeStruct((MIX, D_MODEL), MXU_DTYPE)],
        scratch_shapes=[pltpu.VMEM((MIX, D_MODEL), F32)],
        compiler_params=_params(dimension_semantics=("arbitrary",)),
    )(dx2, yg, w_out, dep)


def _inproj_bwd(dz, h, x2, dx_in, g_rows, w_all, dep, layer):
    n = x2.shape[0]
    tm = TM_BWD
    n_steps = n // tm

    def body(dz_ref, h_ref, x_ref, dxi_ref, g_ref, w_ref, dep_ref, dxo_ref, dw_ref, dg_ref, acc_ref, wcat_ref):
        i = pl.program_id(0)

        @pl.when(i == 0)
        def _():
            acc_ref[...] = jnp.zeros_like(acc_ref)
            dg_ref[...] = jnp.zeros_like(dg_ref)
            for d in range(N_DEV):
                wcat_ref[:, d * W_IN_COLS:(d + 1) * W_IN_COLS] = w_ref[d]

        hb = h_ref[...]
        for d in range(N_DEV):
            acc_ref[d] += _mm_tn(hb, dz_ref[:, d * W_IN_COLS:(d + 1) * W_IN_COLS])
        dh = _mm_nt(dz_ref[...], wcat_ref[...])
        x = x_ref[...]
        r = lax.rsqrt(jnp.mean(x * x, axis=-1, keepdims=True) + NORM_EPS)
        xh = x * r
        dg_ref[...] += jnp.sum(dh * xh, axis=0, keepdims=True)
        gdy = dh * g_ref[...]
        dxo_ref[...] = dxi_ref[...] + r * (gdy - xh * jnp.mean(xh * gdy, axis=-1, keepdims=True))

        @pl.when(i == n_steps - 1)
        def _():
            dw_ref[...] = _mx(acc_ref[...])

    return pl.pallas_call(
        body, name="inproj_bwd",
        grid=(n_steps,),
        in_specs=[pl.BlockSpec((tm, 2 * MIX), lambda i: (i, 0)),
                  pl.BlockSpec((tm, D_MODEL), lambda i: (i, 0)),
                  pl.BlockSpec((tm, D_MODEL), lambda i: (i, 0)),
                  pl.BlockSpec((tm, D_MODEL), lambda i: (i, 0)),
                  _of_layer(layer, 1, D_MODEL),
                  pl.BlockSpec((N_DEV, D_MODEL, W_IN_COLS), lambda i: (0, 0, 0)),
                  ANY_SPEC],
        out_specs=[pl.BlockSpec((tm, D_MODEL), lambda i: (i, 0)),
                   pl.BlockSpec((N_DEV, D_MODEL, W_IN_COLS), lambda i: (0, 0, 0)),
                   pl.BlockSpec((1, D_MODEL), lambda i: (0, 0))],
        out_shape=[jax.ShapeDtypeStruct((n, D_MODEL), F32),
                   jax.ShapeDtypeStruct((N_DEV, D_MODEL, W_IN_COLS), MXU_DTYPE),
                   jax.ShapeDtypeStruct((1, D_MODEL), F32)],
        scratch_shapes=[pltpu.VMEM((N_DEV, D_MODEL, W_IN_COLS), F32),
                        pltpu.VMEM((D_MODEL, 2 * MIX), MXU_DTYPE)],
        compiler_params=_params(dimension_semantics=("arbitrary",)),
    )(dz, h, x2, dx_in, g_rows, w_all, dep)


def _row_pos(t0, rows):
    return t0 + lax.broadcasted_iota(jnp.int32, (rows, LANES), 0)


def _pool_window_mean(upad, g, t0, t_blk):
    k = 2 << g
    w = upad
    sh = 1
    while sh < k:
        w = w + pltpu.roll(w, sh, 0)
        sh *= 2
    count = jnp.minimum(_row_pos(t0, t_blk) + 1, k).astype(F32)
    return w[HALO:] / count - upad[HALO:]


def _pool_window_bwd(qpad, g, t_blk):
    k = 2 << g
    n = t_blk + HALO
    w = qpad
    sh = 1
    while sh < k:
        w = w + pltpu.roll(w, n - sh, 0)
        sh *= 2
    return w[:t_blk]


class _StateBuf:
    def __init__(self, refs, t_blk):
        self.refs = refs
        self.t_blk = t_blk

    def put_chunk(self, b, j, val):
        for c in range(4):
            self.refs[4 * b + c][pl.ds(j, self.t_blk, stride=STATE_ROWS), :] = val[:, c * LANES:(c + 1) * LANES]

    def get_chunk(self, b, j):
        return jnp.concatenate(
            [self.refs[4 * b + c][pl.ds(j, self.t_blk, stride=STATE_ROWS), :] for c in range(4)], axis=-1)

    def load(self, b, r, part):
        return jnp.concatenate(
            [self.refs[4 * b + 2 * part + h][pl.ds(r, STATE_ROWS), :] for h in range(2)], axis=-1)

    def store(self, b, r, part, val):
        for h in range(2):
            self.refs[4 * b + 2 * part + h][pl.ds(r, STATE_ROWS), :] = val[:, h * LANES:(h + 1) * LANES]


def _state_scratch(nb, t_blk):
    return [pltpu.VMEM((t_blk * STATE_ROWS, LANES), F32) for _ in range(4 * nb)]


def _ssm_project_in(u_ssm, wb_ref, buf, nb):
    t_blk = u_ssm.shape[0] // nb
    ub = _mx(u_ssm)
    for j in range(STATE_ROWS):
        m = j // 2
        bu = _mm(ub[:, m * LANES:(m + 1) * LANES], wb_ref[j])
        for b in range(nb):
            buf.put_chunk(b, j, bu[b * t_blk:(b + 1) * t_blk])


def _scan_forward(buf, lbr, lbi, init, nb):
    def step(t, carry):
        r = pl.multiple_of(t * STATE_ROWS, STATE_ROWS)
        out = []
        for b in range(nb):
            sr, si = carry[2 * b], carry[2 * b + 1]
            nr = lbr * sr - lbi * si + buf.load(b, r, 0)
            ni = lbr * si + lbi * sr + buf.load(b, r, 1)
            buf.store(b, r, 0, nr)
            buf.store(b, r, 1, ni)
            out += [nr, ni]
        return tuple(out)

    def body(i, carry):
        for u in range(SCAN_UNROLL):
            carry = step(i * SCAN_UNROLL + u, carry)
        return carry

    return lax.fori_loop(0, buf.t_blk // SCAN_UNROLL, body, init)


def _ssm_project_out(chunk, wc_ref):
    tiles = []
    for m in range(4):
        acc = None
        for j in (2 * m, 2 * m + 1):
            part = _mm_nt(chunk(j), wc_ref[j])
            acc = part if acc is None else acc + part
        tiles.append(acc)
    return jnp.concatenate(tiles, axis=-1)


def _layer_fwd(x3, z3, g_rows, w_in, pool_w, pool_scale, lbr, lbi, wb, wc, d_skip, glu_w, glu_b, w_out, dep, layer):
    nb, seq, _ = x3.shape
    t_blk = min(T_BLK, seq)
    n_t = seq // t_blk
    halo_per_blk = t_blk // HALO
    rows = nb * t_blk
    fused = z3 is None

    def body(*refs):
        if fused:
            (x_ref, g_ref, wi_ref, pw_ref, ps_ref, lbr_ref, lbi_ref, wb_ref, wc_ref, dsk_ref, gw_ref, gb_ref, wo_ref,
             dep_ref, z_ref, h_ref, yg_ref, sc_ref, act_ref, dact_ref, pooled_ref, ypre_ref, xo_ref,
             carry_ref, halo_ref, *s_refs) = refs
        else:
            (x_ref, z_ref, zh_ref, pw_ref, ps_ref, lbr_ref, lbi_ref, wb_ref, wc_ref, dsk_ref, gw_ref, gb_ref, wo_ref,
             dep_ref, yg_ref, sc_ref, act_ref, dact_ref, pooled_ref, ypre_ref, xo_ref, carry_ref, *s_refs) = refs
        i = pl.program_id(0)
        t0 = i * t_blk
        buf = _StateBuf(s_refs, t_blk)
        both = lambda lo, hi: z_ref[:, :, lo:hi].reshape(rows, hi - lo)

        @pl.when(i == 0)
        def _():
            carry_ref[...] = jnp.zeros_like(carry_ref)
            if fused:
                halo_ref[...] = jnp.zeros_like(halo_ref)

        x = x_ref[...].reshape(rows, D_MODEL)
        if fused:
            r = lax.rsqrt(jnp.mean(x * x, axis=-1, keepdims=True) + NORM_EPS)
            h = _mx(x * r * g_ref[...])
            h_ref[...] = h.reshape(nb, t_blk, D_MODEL)
            for d in range(N_DEV):
                z_ref[:, :, d * W_IN_COLS:(d + 1) * W_IN_COLS] = _mm(h, wi_ref[d]).reshape(nb, t_blk, W_IN_COLS)

        u_ssm = both(POOL_W, MIX)
        _ssm_project_in(u_ssm, wb_ref, buf, nb)
        init = tuple(carry_ref[b, :, h * STATE_COLS:(h + 1) * STATE_COLS] for b in range(nb) for h in range(2))
        fin = _scan_forward(buf, lbr_ref[...], lbi_ref[...], init, nb)
        for b in range(nb):
            carry_ref[b, :, 0:STATE_COLS] = fin[2 * b]
            carry_ref[b, :, STATE_COLS:2 * STATE_COLS] = fin[2 * b + 1]

        def chunk(j):
            states = _mx(jnp.concatenate([buf.get_chunk(b, j) for b in range(nb)], axis=0))
            sc_ref[:, j] = states.reshape(nb, t_blk, 2 * STATE_COLS)
            return states

        y = _ssm_project_out(chunk, wc_ref) + dsk_ref[...] * u_ssm
        yg, dgelu = _gelu_and_grad(y)
        ygb = _mx(yg)
        act_ref[...] = ygb.reshape(nb, t_blk, SSM_W)
        dact_ref[...] = _mx(dgelu).reshape(nb, t_blk, SSM_W)
        o_ssm = yg * _sigmoid(_mm(ygb, gw_ref[...]) + gb_ref[...])
        gp = both(MIX + POOL_W, 2 * MIX)
        parts = []
        first = (i == 0)
        for g in range(N_POOL_G):
            cols = slice(g * POOL_GC, (g + 1) * POOL_GC)
            pooled = []
            for b in range(nb):
                halo = halo_ref[b, :, cols] if fused else jnp.where(first, 0.0, zh_ref[b, :, cols])
                pooled.append(_pool_window_mean(jnp.concatenate([halo, z_ref[b, :, cols]], axis=0), g, t0, t_blk))
            pb = _mx(jnp.concatenate(pooled, axis=0))
            ypre = _mm(pb, pw_ref[g])
            pooled_ref[:, :, cols] = pb.reshape(nb, t_blk, POOL_GC)
            ypre_ref[:, :, cols] = ypre.reshape(nb, t_blk, POOL_GC)
            gpp = both(MIX + g * POOL_GC, MIX + (g + 1) * POOL_GC)
            parts.append(_mx(ypre * ps_ref[:, cols] * (gpp * _sigmoid(gpp))))
        parts.append(_mx(o_ssm * (gp * _sigmoid(gp))))
        gated = jnp.concatenate(parts, axis=-1)
        yg_ref[...] = gated.reshape(nb, t_blk, MIX)
        xo_ref[...] = (x + _mm(gated, wo_ref[...])).reshape(nb, t_blk, D_MODEL)
        if fused:
            halo_ref[...] = z_ref[:, t_blk - HALO:, 0:POOL_W]

    const = lambda *shape: pl.BlockSpec(shape, lambda i: (0,) * len(shape))
    tokens = lambda width: pl.BlockSpec((nb, t_blk, width), lambda i: (0, i, 0))
    mixer_specs = [_of_layer(layer, N_POOL_G, POOL_GC, POOL_GC), _of_layer(layer, 1, POOL_W),
                   _of_layer(layer, STATE_ROWS, STATE_COLS), _of_layer(layer, STATE_ROWS, STATE_COLS),
                   _of_layer(layer, STATE_ROWS, LANES, 2 * STATE_COLS),
                   _of_layer(layer, STATE_ROWS, LANES, 2 * STATE_COLS),
                   _of_layer(layer, 1, SSM_W), const(SSM_W, SSM_W), _of_layer(layer, 1, SSM_W),
                   const(MIX, D_MODEL), ANY_SPEC]
    mixer_args = (pool_w, pool_scale, lbr, lbi, wb, wc, d_skip, glu_w, glu_b, w_out, dep)
    out_specs = [tokens(MIX), pl.BlockSpec((nb, STATE_ROWS, t_blk, 2 * STATE_COLS), lambda i: (0, 0, i, 0)),
                 tokens(SSM_W), tokens(SSM_W), tokens(POOL_W), tokens(POOL_W), tokens(D_MODEL)]
    out_shape = [jax.ShapeDtypeStruct((nb, seq, MIX), MXU_DTYPE),
                 jax.ShapeDtypeStruct((nb, STATE_ROWS, seq, 2 * STATE_COLS), MXU_DTYPE),
                 jax.ShapeDtypeStruct((nb, seq, SSM_W), MXU_DTYPE),
                 jax.ShapeDtypeStruct((nb, seq, SSM_W), MXU_DTYPE),
                 jax.ShapeDtypeStruct((nb, seq, POOL_W), MXU_DTYPE),
                 jax.ShapeDtypeStruct((nb, seq, POOL_W), F32),
                 jax.ShapeDtypeStruct((nb, seq, D_MODEL), F32)]
    scratch = [pltpu.VMEM((nb, STATE_ROWS, 2 * STATE_COLS), F32)]
    if fused:
        in_specs = [tokens(D_MODEL), _of_layer(layer, 1, D_MODEL), const(N_DEV, D_MODEL, W_IN_COLS)] + mixer_specs
        args = (x3, g_rows, w_in) + mixer_args
        out_specs = [tokens(2 * MIX), tokens(D_MODEL)] + out_specs
        out_shape = [jax.ShapeDtypeStruct((nb, seq, 2 * MIX), F32),
                     jax.ShapeDtypeStruct((nb, seq, D_MODEL), MXU_DTYPE)] + out_shape
        scratch = scratch + [pltpu.VMEM((nb, HALO, POOL_W), F32)]
    else:
        in_specs = [tokens(D_MODEL), tokens(2 * MIX),
                    pl.BlockSpec((nb, HALO, POOL_W), lambda i: (0, jnp.maximum(i * halo_per_blk - 1, 0), 0))] + mixer_specs
        args = (x3, z3, z3) + mixer_args
    return pl.pallas_call(
        body, name="layer_fwd" if fused else "mixer_fwd",
        grid=(n_t,),
        in_specs=in_specs, out_specs=out_specs, out_shape=out_shape,
        scratch_shapes=scratch + _state_scratch(nb, t_blk),
        compiler_params=_params(dimension_semantics=("arbitrary",)),
    )(*args)


def _mixer_bwd(z3, dy3, states, kept, pool_w, pool_scale, lbr, lbi, wb, wc, d_skip, glu_w, glu_b, layer):
    nb, seq, _ = z3.shape
    t_blk = min(T_BLK, seq)
    n_t = seq // t_blk
    halo_per_blk = t_blk // HALO
    rows = nb * t_blk

    def body(z_ref, dy_ref, sc_ref, sch_ref, act_ref, dact_ref, pooled_ref, ypre_ref, pw_ref, ps_ref, lbr_ref, lbi_ref, wb_ref, wc_ref, dsk_ref,
             gw_ref, gb_ref,
             dz_ref, dpw_ref, dps_ref, dlbr_ref, dlbi_ref, dwb_ref, dwc_ref, ddsk_ref, dgw_ref, dgb_ref,
             gcarry_ref, qcarry_ref, du_ref, dgw_acc, *g_refs):
        i = pl.program_id(0)
        blk = n_t - 1 - i
        t0 = blk * t_blk
        gbuf = _StateBuf(g_refs, t_blk)

        @pl.when(i == 0)
        def _():
            gcarry_ref[...] = jnp.zeros_like(gcarry_ref)
            qcarry_ref[...] = jnp.zeros_like(qcarry_ref)
            for ref in (dpw_ref, dps_ref, dlbr_ref, dlbi_ref, dwb_ref, dwc_ref, ddsk_ref, dgw_acc, dgb_ref):
                ref[...] = jnp.zeros_like(ref)

        lbr_v = lbr_ref[...]
        lbi_v = lbi_ref[...]

        both = lambda ref, lo, hi: ref[:, :, lo:hi].reshape(rows, hi - lo)
        split = lambda val: val.reshape(nb, t_blk, val.shape[-1])
        states = lambda j: sc_ref[:, j].reshape(rows, 2 * STATE_COLS)
        first = (blk == 0)

        u_ssm = both(z_ref, POOL_W, MIX)
        ygb = act_ref[...].reshape(rows, SSM_W)
        yg = ygb.astype(F32)
        dgelu = dact_ref[...].reshape(rows, SSM_W).astype(F32)
        sg = _sigmoid(_mm(ygb, gw_ref[...]) + gb_ref[...])
        o_ssm = yg * sg
        gp = both(z_ref, MIX + POOL_W, 2 * MIX)
        sgm = _sigmoid(gp)
        dyv = both(dy_ref, POOL_W, MIX)
        dz_ref[:, :, MIX + POOL_W:2 * MIX] = split(_mx(dyv * o_ssm * (sgm * (1.0 + gp * (1.0 - sgm)))))
        do = dyv * (gp * sgm)
        dv = do * yg * (sg * (1.0 - sg))
        dvb = _mx(dv)
        dgb_ref[...] += jnp.sum(dv, axis=0, keepdims=True)
        dgw_acc[...] += _mm_tn(ygb, dvb)
        dyp = (do * sg + _mm_nt(dvb, gw_ref[...])) * dgelu
        ddsk_ref[...] += jnp.sum(dyp * u_ssm, axis=0, keepdims=True)
        dypb = _mx(dyp)
        for j in range(STATE_ROWS):
            m = j // 2
            dyt = dypb[:, m * LANES:(m + 1) * LANES]
            ds = _mm(dyt, wc_ref[j])
            for b in range(nb):
                gbuf.put_chunk(b, j, ds[b * t_blk:(b + 1) * t_blk])
            dwc_ref[j] += _mm_tn(dyt, states(j))
        du_ref[...] = split(dsk_ref[...] * dyp)

        for g in range(N_POOL_G):
            cols = slice(g * POOL_GC, (g + 1) * POOL_GC)
            pb = both(pooled_ref, g * POOL_GC, (g + 1) * POOL_GC)
            ypre = both(ypre_ref, g * POOL_GC, (g + 1) * POOL_GC)
            gpp = both(z_ref, MIX + g * POOL_GC, MIX + (g + 1) * POOL_GC)
            sgp = _sigmoid(gpp)
            dyg = both(dy_ref, g * POOL_GC, (g + 1) * POOL_GC)
            scale = ps_ref[:, cols]
            dz_ref[:, :, MIX + g * POOL_GC:MIX + (g + 1) * POOL_GC] = split(_mx(
                dyg * (ypre * scale) * (sgp * (1.0 + gpp * (1.0 - sgp)))))
            dyc = dyg * (gpp * sgp)
            dps_ref[:, cols] += jnp.sum(dyc * ypre, axis=0, keepdims=True)
            dypre = _mx(dyc * scale)
            dpw_ref[g] += _mm_tn(pb, dypre)
            dpooled = _mm_nt(dypre, pw_ref[g])
            count = jnp.minimum(_row_pos(t0, t_blk) + 1, 2 << g).astype(F32)
            for b in range(nb):
                dp = dpooled[b * t_blk:(b + 1) * t_blk]
                q = dp / count
                qpad = jnp.concatenate([q, qcarry_ref[b, :, cols]], axis=0)
                qcarry_ref[b, :, cols] = q[:HALO]
                dz_ref[b, :, cols] = _mx(_pool_window_bwd(qpad, g, t_blk) - dp)

        def rev_step(t, carry):
            r = pl.multiple_of(t * STATE_ROWS, STATE_ROWS)
            out = []
            for b in range(nb):
                gr, gi = carry[2 * b], carry[2 * b + 1]
                ngr = lbr_v * gr + lbi_v * gi + gbuf.load(b, r, 0)
                ngi = lbr_v * gi - lbi_v * gr + gbuf.load(b, r, 1)
                gbuf.store(b, r, 0, ngr)
                gbuf.store(b, r, 1, ngi)
                out += [ngr, ngi]
            return tuple(out)

        def rev_body(i, carry):
            for u in range(SCAN_UNROLL):
                carry = rev_step(t_blk - 1 - (i * SCAN_UNROLL + u), carry)
            return carry

        init_g = tuple(gcarry_ref[b, :, h * STATE_COLS:(h + 1) * STATE_COLS] for b in range(nb) for h in range(2))
        fin = lax.fori_loop(0, t_blk // SCAN_UNROLL, rev_body, init_g)
        for b in range(nb):
            gcarry_ref[b, :, 0:STATE_COLS] = fin[2 * b]
            gcarry_ref[b, :, STATE_COLS:2 * STATE_COLS] = fin[2 * b + 1]

        ub = _mx(u_ssm)
        for m in range(4):
            acc = both(du_ref, m * LANES, (m + 1) * LANES)
            for j in (2 * m, 2 * m + 1):
                g = jnp.concatenate([gbuf.get_chunk(b, j) for b in range(nb)], axis=0)
                gj = _mx(g)
                acc = acc + _mm_nt(gj, wb_ref[j])
                dwb_ref[j] += _mm_tn(ub[:, m * LANES:(m + 1) * LANES], gj)
                shifted = []
                for b in range(nb):
                    before = jnp.where(first, 0.0, sch_ref[b, j].astype(F32))
                    spad = jnp.concatenate([before, sc_ref[b, j].astype(F32)], axis=0)
                    shifted.append(pltpu.roll(spad, 1, 0)[HALO:])
                s_prev = jnp.concatenate(shifted, axis=0)
                g_re, g_im = g[:, :STATE_COLS], g[:, STATE_COLS:]
                p_re, p_im = s_prev[:, :STATE_COLS], s_prev[:, STATE_COLS:]
                dlbr_ref[j:j + 1, :] += jnp.sum(g_re * p_re + g_im * p_im, axis=0, keepdims=True)
                dlbi_ref[j:j + 1, :] += jnp.sum(g_im * p_re - g_re * p_im, axis=0, keepdims=True)
            dz_ref[:, :, POOL_W + m * LANES:POOL_W + (m + 1) * LANES] = split(_mx(acc))

        @pl.when(i == n_t - 1)
        def _():
            dgw_ref[...] = _mx(dgw_acc[...])

    const = lambda *shape: pl.BlockSpec(shape, lambda i: (0,) * len(shape))
    rev = lambda i: n_t - 1 - i
    out_shape = [jax.ShapeDtypeStruct((nb, seq, 2 * MIX), MXU_DTYPE),
                 jax.ShapeDtypeStruct((N_POOL_G, POOL_GC, POOL_GC), F32),
                 jax.ShapeDtypeStruct((1, POOL_W), F32),
                 jax.ShapeDtypeStruct((STATE_ROWS, STATE_COLS), F32),
                 jax.ShapeDtypeStruct((STATE_ROWS, STATE_COLS), F32),
                 jax.ShapeDtypeStruct((STATE_ROWS, LANES, 2 * STATE_COLS), F32),
                 jax.ShapeDtypeStruct((STATE_ROWS, LANES, 2 * STATE_COLS), F32),
                 jax.ShapeDtypeStruct((1, SSM_W), F32),
                 jax.ShapeDtypeStruct((SSM_W, SSM_W), MXU_DTYPE),
                 jax.ShapeDtypeStruct((1, SSM_W), F32)]
    return pl.pallas_call(
        body, name="mixer_bwd",
        grid=(n_t,),
        in_specs=[pl.BlockSpec((nb, t_blk, 2 * MIX), lambda i: (0, rev(i), 0)),
                  pl.BlockSpec((nb, t_blk, MIX), lambda i: (0, rev(i), 0)),
                  pl.BlockSpec((nb, STATE_ROWS, t_blk, 2 * STATE_COLS), lambda i: (0, 0, rev(i), 0)),
                  pl.BlockSpec((nb, STATE_ROWS, HALO, 2 * STATE_COLS),
                               lambda i: (0, 0, jnp.maximum(rev(i) * halo_per_blk - 1, 0), 0)),
                  pl.BlockSpec((nb, t_blk, SSM_W), lambda i: (0, rev(i), 0)),
                  pl.BlockSpec((nb, t_blk, SSM_W), lambda i: (0, rev(i), 0)),
                  pl.BlockSpec((nb, t_blk, POOL_W), lambda i: (0, rev(i), 0)),
                  pl.BlockSpec((nb, t_blk, POOL_W), lambda i: (0, rev(i), 0)),
                  _of_layer(layer, N_POOL_G, POOL_GC, POOL_GC), _of_layer(layer, 1, POOL_W),
                  _of_layer(layer, STATE_ROWS, STATE_COLS), _of_layer(layer, STATE_ROWS, STATE_COLS),
                  _of_layer(layer, STATE_ROWS, LANES, 2 * STATE_COLS),
                  _of_layer(layer, STATE_ROWS, LANES, 2 * STATE_COLS),
                  _of_layer(layer, 1, SSM_W), const(SSM_W, SSM_W), _of_layer(layer, 1, SSM_W)],
        out_specs=[pl.BlockSpec((nb, t_blk, 2 * MIX), lambda i: (0, rev(i), 0))]
                  + [const(*s.shape) for s in out_shape[1:]],
        out_shape=out_shape,
        scratch_shapes=[pltpu.VMEM((nb, STATE_ROWS, 2 * STATE_COLS), F32),
                        pltpu.VMEM((nb, HALO, POOL_W), F32),
                        pltpu.VMEM((nb, t_blk, SSM_W), F32),
                        pltpu.VMEM((SSM_W, SSM_W), F32)]
                       + _state_scratch(nb, t_blk),
        compiler_params=_params(dimension_semantics=("arbitrary",)),
    )(z3, dy3, states, states, *kept, pool_w, pool_scale, lbr, lbi, wb, wc, d_skip, glu_w, glu_b)


def _mesh_place():
    x, y, c = lax.axis_index("x"), lax.axis_index("y"), lax.axis_index("c")
    return x, y, c


def _flip(place, k):
    x, y, c = place
    return (1 - x if k & 4 else x, 1 - y if k & 2 else y, 1 - c if k & 1 else c)


def _index(place):
    x, y, c = place
    return 4 * x + 2 * y + c


HBM_SPEC = pl.BlockSpec(memory_space=pltpu.HBM)
SEM_SPEC = pl.BlockSpec(memory_space=pltpu.SEMAPHORE)
_EFFECT = pltpu.SideEffectType.DATAFLOW_SIDE_EFFECTING
N_PEERS = N_DEV - 1


def _exchange_copies(src_refs, land_refs, send_sems, recv_sems):
    me = _mesh_place()
    mine = _index(me)
    out = []
    for a, land_ref in enumerate(land_refs):
        for k in range(1, N_DEV):
            peer = _flip(me, k)
            theirs = _index(peer)
            n = a * N_PEERS + k - 1
            src = src_refs[a].at[theirs] if src_refs else land_ref.at[mine]
            send = pltpu.make_async_remote_copy(
                src_ref=src, dst_ref=land_ref.at[mine], send_sem=send_sems.at[n], recv_sem=recv_sems.at[n],
                device_id=peer, device_id_type=MESH)
            recv = pltpu.make_async_remote_copy(
                src_ref=src, dst_ref=land_ref.at[theirs], send_sem=send_sems.at[n], recv_sem=recv_sems.at[n],
                device_id=peer, device_id_type=MESH)
            out.append((send, recv))
    return out


def _exchange_start(srcs, lands, after, name):
    arrays = tuple(srcs) + tuple(lands)
    n_src, n_all = len(srcs), len(arrays)
    n_copies = len(lands) * N_PEERS

    def body(*refs):
        send_sems, recv_sems = refs[n_all + 1], refs[n_all + 2]
        token = refs[-1]
        for send, _ in _exchange_copies(refs[:n_src], refs[n_src:n_all], send_sems, recv_sems):
            send.start()
        token[...] = jnp.zeros_like(token)

    res = pl.pallas_call(
        body, name=name,
        in_specs=[HBM_SPEC] * n_all + [ANY_SPEC],
        out_specs=[SEM_SPEC, SEM_SPEC] + [HBM_SPEC] * n_all + [VMEM_SPEC],
        out_shape=[pltpu.SemaphoreType.DMA((n_copies,)), pltpu.SemaphoreType.DMA((n_copies,))]
                  + [pltpu.HBM(a.shape, a.dtype) for a in arrays] + [jax.ShapeDtypeStruct((SUBLANES, LANES), F32)],
        input_output_aliases={i: 2 + i for i in range(n_all)},
        compiler_params=pltpu.CompilerParams(has_side_effects=_EFFECT),
    )(*[pltpu.with_memory_space_constraint(a, pltpu.HBM) for a in arrays], after)
    return tuple(res[:-1]), res[-1]


def _exchange_wait(handle, n_lands, after, name):
    send_sems, recv_sems = handle[0], handle[1]
    arrays = handle[2:]
    n_all = len(arrays)
    n_src = n_all - n_lands

    def body(*refs):
        for send, recv in _exchange_copies(refs[:n_src], refs[n_src:n_all], refs[n_all], refs[n_all + 1]):
            send.wait_send()
            recv.wait_recv()

    res = pl.pallas_call(
        body, name=name,
        in_specs=[HBM_SPEC] * n_all + [SEM_SPEC, SEM_SPEC, ANY_SPEC],
        out_specs=[HBM_SPEC] * n_all,
        out_shape=[pltpu.HBM(a.shape, a.dtype) for a in arrays],
        input_output_aliases={i: i for i in range(n_all)},
        compiler_params=pltpu.CompilerParams(has_side_effects=_EFFECT),
    )(*arrays, send_sems, recv_sems, after)
    return tuple(res[:n_src]), tuple(res[n_src:])


def _weight_zones(w_in, glu_w, w_out, my_idx):
    shards = (w_in, glu_w, w_out)
    depth = w_in.shape[0]

    def body(idx_ref, *refs):
        ins, zones = refs[:len(shards)], refs[len(shards):]
        for l in range(depth):
            for a, src in enumerate(ins):
                zones[l * len(shards) + a][0] = _mx(src[l])

    whole = lambda s: pl.BlockSpec(s.shape, lambda i, idx: (0,) * s.ndim)
    return pl.pallas_call(
        body, name="weight_zones",
        grid_spec=pltpu.PrefetchScalarGridSpec(
            num_scalar_prefetch=1, grid=(1,),
            in_specs=[whole(s) for s in shards],
            out_specs=[pl.BlockSpec((1,) + s.shape[1:], lambda i, idx: (idx[0], 0, 0))
                       for _ in range(depth) for s in shards]),
        out_shape=[jax.ShapeDtypeStruct((N_DEV,) + s.shape[1:], MXU_DTYPE) for _ in range(depth) for s in shards],
        compiler_params=_params(dimension_semantics=("arbitrary",)),
    )(my_idx.reshape(1).astype(jnp.int32), *shards)


def _allreduce_packed(p):
    rows = p.shape[0]
    half = rows // 2
    quarter = half // 4

    def body(p_ref, o_ref, part_ref, sib_ref, got_ref, send_sems, recv_sems):
        x, y, c = _mesh_place()
        sibling = (x, y, 1 - c)
        chip = 2 * x + y
        chips = [(k, (1 - x if k & 2 else x, 1 - y if k & 1 else y, c), chip ^ k) for k in (1, 2, 3)]
        my_half = pl.multiple_of(c * half, SUBLANES)
        other_half = pl.multiple_of((1 - c) * half, SUBLANES)

        def copy(n, src, dst, to):
            return pltpu.make_async_remote_copy(src_ref=src, dst_ref=dst, send_sem=send_sems.at[n],
                                                recv_sem=recv_sems.at[n], device_id=to, device_id_type=MESH)

        def quarter_of(ref, base, q):
            return ref.at[pl.ds(pl.multiple_of(base + q * quarter, SUBLANES), quarter)]

        swap = copy(0, p_ref.at[pl.ds(other_half, half)], sib_ref, sibling)
        swap.start()
        swap.wait()
        part_ref[...] = p_ref[pl.ds(my_half, half), :] + sib_ref[...]

        scatter = [copy(k, quarter_of(part_ref, 0, q), got_ref.at[k - 1], to) for k, to, q in chips]
        for cp in scatter:
            cp.start()
        total = part_ref[pl.ds(pl.multiple_of(chip * quarter, SUBLANES), quarter), :]
        for cp, (k, _, _) in zip(scatter, chips):
            cp.wait()
            total = total + got_ref[k - 1]
        mine = pl.multiple_of(my_half + chip * quarter, SUBLANES)
        o_ref[pl.ds(mine, quarter), :] = total

        gather = [copy(3 + k, o_ref.at[pl.ds(mine, quarter)], o_ref.at[pl.ds(mine, quarter)], to) for k, to, _ in chips]
        for cp in gather:
            cp.start()
        for k, to, q in chips:
            theirs = quarter_of(o_ref, my_half, q)
            copy(3 + k, theirs, theirs, to).wait_recv()
        for cp in gather:
            cp.wait_send()

        back = copy(7, o_ref.at[pl.ds(my_half, half)], o_ref.at[pl.ds(my_half, half)], sibling)
        back.start()
        copy(7, o_ref.at[pl.ds(other_half, half)], o_ref.at[pl.ds(other_half, half)], sibling).wait_recv()
        back.wait_send()

    return pl.pallas_call(
        body, name="comm_allreduce_packed",
        in_specs=[VMEM_SPEC],
        out_specs=VMEM_SPEC,
        out_shape=jax.ShapeDtypeStruct(p.shape, F32),
        scratch_shapes=[pltpu.VMEM((half, LANES), F32),
                        pltpu.VMEM((half, LANES), F32),
                        pltpu.VMEM((3, quarter, LANES), F32),
                        pltpu.SemaphoreType.DMA((8,)),
                        pltpu.SemaphoreType.DMA((8,))],
        compiler_params=_params(),
    )(p)


def _adamw_math(w, g, m, v):
    m = ADAM_B1 * m + (1.0 - ADAM_B1) * g
    v = ADAM_B2 * v + (1.0 - ADAM_B2) * (g * g)
    m_hat = m / (1.0 - ADAM_B1 ** ADAM_STEP)
    v_hat = v / (1.0 - ADAM_B2 ** ADAM_STEP)
    delta = -ADAM_LR * (m_hat / (jnp.sqrt(v_hat) + ADAM_EPS) + ADAM_WD * w)
    return delta, m, v


def _adamw_summed(received, own, my_idx, w, m, v, name):
    depth, r, c = w.shape
    tr = min(r, 128)

    def body(idx_ref, *refs):
        r_refs, o_refs = refs[:depth], refs[depth:2 * depth]
        w_ref, m_ref, v_ref, g_ref, d_ref, nm_ref, nv_ref = refs[2 * depth:]
        me = idx_ref[0]
        for l in range(depth):
            g = jnp.zeros((tr, c), F32)
            for q in range(N_DEV):
                g = g + jnp.where(q == me, o_refs[l][0], r_refs[l][q]).astype(F32)
            g_ref[l] = g
            d_ref[l], nm_ref[l], nv_ref[l] = _adamw_math(w_ref[l], g, m_ref[l], v_ref[l])

    blk = pl.BlockSpec((depth, tr, c), lambda i, idx: (0, i, 0))
    return pl.pallas_call(
        body, name=name,
        grid_spec=pltpu.PrefetchScalarGridSpec(
            num_scalar_prefetch=1, grid=(r // tr,),
            in_specs=[pl.BlockSpec((N_DEV, tr, c), lambda i, idx: (0, i, 0))] * depth
                     + [pl.BlockSpec((1, tr, c), lambda i, idx: (idx[0], i, 0))] * depth
                     + [blk, blk, blk],
            out_specs=[blk] * 4),
        out_shape=[jax.ShapeDtypeStruct((depth, r, c), F32)] * 4,
        compiler_params=_params(dimension_semantics=("arbitrary",)),
    )(my_idx.reshape(1).astype(jnp.int32), *received, *own, w, m, v)


def _adamw_small(ws, gs, ms, vs):
    n = len(ws)
    depth = ws[0].shape[0]

    def spec(a):
        per_layer = a.shape[0] == depth
        rest = (0,) * (a.ndim - 1)
        return pl.BlockSpec((1,) + a.shape[1:], lambda l: ((l if per_layer else 0),) + rest)

    def body(*refs):
        w_refs, g_refs, m_refs, v_refs = (refs[k * n:(k + 1) * n] for k in range(4))
        d_refs, nm_refs, nv_refs = (refs[(4 + k) * n:(5 + k) * n] for k in range(3))
        for k in range(n):
            d_refs[k][...], nm_refs[k][...], nv_refs[k][...] = _adamw_math(
                w_refs[k][...], g_refs[k][...], m_refs[k][...], v_refs[k][...])

    specs = [spec(a) for a in ws]
    shapes = [jax.ShapeDtypeStruct(a.shape, F32) for a in ws]
    res = pl.pallas_call(
        body, name="adamw_small",
        grid=(depth,),
        in_specs=specs * 4,
        out_specs=specs * 3,
        out_shape=shapes * 3,
        compiler_params=_params(dimension_semantics=("arbitrary",)),
    )(*ws, *gs, *ms, *vs)
    return res[:n], res[n:2 * n], res[2 * n:]


_PACK_ROWS = SUBLANES * N_DEV


def _pack(arrays):
    flat = jnp.concatenate([a.reshape(-1) for a in arrays])
    per = _PACK_ROWS * LANES
    total = -(-flat.shape[0] // per) * per
    flat = jnp.pad(flat, (0, total - flat.shape[0]))
    return flat.reshape(total // LANES, LANES)


def _unpack(packed, like):
    flat = packed.reshape(-1)
    out = []
    off = 0
    for a in like:
        out.append(flat[off:off + a.size].reshape(a.shape))
        off += a.size
    return out


def kernel(x, norm_g, w_in, pool_w, pool_scale, a_re, a_im, log_dt, b_re, b_im, c_re, c_im, d_skip, glu_w, glu_b, w_out, final_g, loss_target, m_norm_g, m_w_in, m_pool_w, m_pool_scale, m_a_re, m_a_im, m_log_dt, m_b_re, m_b_im, m_c_re, m_c_im, m_d_skip, m_glu_w, m_glu_b, m_w_out, m_final_g, v_norm_g, v_w_in, v_pool_w, v_pool_scale, v_a_re, v_a_im, v_log_dt, v_b_re, v_b_im, v_c_re, v_c_im, v_d_skip, v_glu_w, v_glu_b, v_w_out, v_final_g):
    nb, seq, _ = x.shape
    n_tok = nb * seq
    depth = norm_g.shape[0]

    my_idx = _index(_mesh_place())

    zones = _weight_zones(w_in, glu_w, w_out, my_idx)

    def gather_start(l, after):
        return _exchange_start((), zones[3 * l:3 * l + 3], after, f"comm_gather_start_{l}")

    def gather_wait(handle, after, l):
        _, (win, glu, wout) = _exchange_wait(handle, 3, after, f"comm_gather_wait_{l}")
        return win, glu.reshape(SSM_W, SSM_W), wout.reshape(MIX, D_MODEL)

    xs = [x.reshape(n_tok, D_MODEL)]
    first_w_in, dep = _exchange_start((), zones[0:1], xs[0], "comm_gather_start_0_w_in")

    (lbr, lbi, rb, rc), dense_vjp = jax.vjp(jax.vmap(_ssm_dense), a_re, a_im, log_dt + dep[0, 0], b_re, b_im, c_re, c_im)
    chunk_all = jax.vmap(_ssm_chunked)
    (wb, wct), chunk_vjp = jax.vjp(lambda p, q: (chunk_all(p), chunk_all(q)), rb, rc)
    wb_m, wct_m = _mx(wb), _mx(wct)
    pool_w_m = _mx(pool_w)
    rows_of = lambda a: a[:, None, :]
    norm_rows, scale_rows, skip_rows, bias_rows = rows_of(norm_g), rows_of(pool_scale), rows_of(d_skip), rows_of(glu_b)

    def layer_params(l):
        return (pool_w_m, scale_rows, lbr, lbi, wb_m, wct_m, skip_rows, weights[l][1], bias_rows)

    saved = []
    weights = []
    for l in range(depth):
        if l == 0:
            _, (win,) = _exchange_wait(first_w_in, 1, wct_m, "comm_gather_wait_0_w_in")
            rest, dep = _exchange_start((), zones[1:3], win, "comm_gather_start_0_rest")
            z, h = _inproj_fwd(xs[-1], norm_rows, win, dep, l)
            _, (glu, wout) = _exchange_wait(rest, 2, z, "comm_gather_wait_0_rest")
            weights.append((win, glu.reshape(SSM_W, SSM_W), wout.reshape(MIX, D_MODEL)))
            handle, dep = gather_start(1, weights[0][2])
            z3 = z.reshape(nb, seq, 2 * MIX)
            yg, states, *kept, x_next = _layer_fwd(xs[-1].reshape(nb, seq, D_MODEL), z3, None, None,
                                                   *layer_params(l), weights[l][2], dep, l)
        else:
            weights.append(gather_wait(handle, xs[-1], l))
            if l + 1 < depth:
                handle, dep = gather_start(l + 1, weights[l][0])
            z3, h3, yg, states, *kept, x_next = _layer_fwd(xs[-1].reshape(nb, seq, D_MODEL), None, norm_rows,
                                                           weights[l][0], *layer_params(l), weights[l][2], dep, l)
            h = h3.reshape(n_tok, D_MODEL)
        xs.append(x_next.reshape(n_tok, D_MODEL))
        saved.append((z3, h, yg.reshape(n_tok, MIX), states, kept))

    dx, loss_part, d_final_g = _loss_head(xs[-1], loss_target.reshape(n_tok, D_MODEL), final_g[None])

    small = {k: [None] * depth for k in
             ("norm_g", "pool_w", "pool_scale", "lbr", "lbi", "wb", "wct", "d_skip", "glu_b")}
    received = [None] * depth
    sent = [None] * depth
    pending = None
    early = None
    for l in reversed(range(depth)):
        z3, h, yg2, states, kept = saved[l]
        dy, d_wout = _outproj_bwd(dx, yg2, weights[l][2], dep)
        (dz, d_pw, d_ps, d_lbr, d_lbi, d_wb, d_wct, d_dsk, d_gw, d_gb) = _mixer_bwd(
            z3, dy.reshape(nb, seq, MIX), states, kept, *layer_params(l), l)
        rest = (d_gw.reshape(N_DEV, SSM_W // N_DEV, SSM_W), d_wout.reshape(N_DEV, MIX // N_DEV, D_MODEL))
        if l == 0:
            early, dep = _exchange_start(rest, tuple(lax.empty(s.shape, s.dtype) for s in rest), dz,
                                         "comm_grads_start_0_rest")
        dx, d_win, d_ng = _inproj_bwd(dz.reshape(n_tok, 2 * MIX), h, xs[l], dx, norm_rows, weights[l][0], dep, l)
        for k, val in (("norm_g", d_ng[0]), ("pool_w", d_pw), ("pool_scale", d_ps[0]), ("lbr", d_lbr),
                       ("lbi", d_lbi), ("wb", d_wb), ("wct", d_wct), ("d_skip", d_dsk[0]), ("glu_b", d_gb[0])):
            small[k][l] = val
        if pending is not None:
            sent[l + 1], received[l + 1] = _exchange_wait(pending, 3, dx, f"comm_grads_wait_{l + 1}")
        srcs = (d_win,) if l == 0 else (d_win,) + rest
        lands = tuple(lax.empty(s.shape, s.dtype) for s in srcs)
        pending, dep = _exchange_start(srcs, lands, dx, f"comm_grads_start_{l}")
    shard_res = {}
    shard_inputs = {"w_in": (w_in, m_w_in, v_w_in), "glu_w": (glu_w, m_glu_w, v_glu_w), "w_out": (w_out, m_w_out, v_w_out)}

    def shard_adamw(n, pos):
        w, m, v = shard_inputs[n]
        shard_res[n] = _adamw_summed([received[l][pos] for l in range(depth)], [sent[l][pos] for l in range(depth)],
                                     my_idx, w, m, v, "adamw_" + n)

    (s_glu, s_wout), (r_glu, r_wout) = _exchange_wait(early, 2, dx, "comm_grads_wait_0_rest")
    sent[0], received[0] = (None, s_glu, s_wout), (None, r_glu, r_wout)
    shard_adamw("glu_w", 1)
    shard_adamw("w_out", 2)

    stack = lambda k: jnp.stack(small[k])
    d_rb, d_rc = chunk_vjp((stack("wb"), stack("wct")))
    local = [stack("norm_g"), stack("pool_w"), stack("pool_scale"), stack("lbr"), stack("lbi"), d_rb, d_rc,
             stack("d_skip"), stack("glu_b"), d_final_g[0] + dep[0, 0], loss_part[0]]
    (g_norm_g, g_pool_w, g_pool_scale, g_lbr, g_lbi, g_rb, g_rc, g_d_skip, g_glu_b, g_final_g, loss) = _unpack(
        _allreduce_packed(_pack(local)), local)
    loss = loss[0]
    g_a_re, g_a_im, g_log_dt, g_b_re, g_b_im, g_c_re, g_c_im = dense_vjp((g_lbr, g_lbi, g_rb, g_rc))

    names = ["norm_g", "pool_w", "pool_scale", "a_re", "a_im", "log_dt", "b_re", "b_im", "c_re", "c_im",
             "d_skip", "glu_b", "final_g"]
    rows = {"norm_g", "pool_scale", "log_dt", "d_skip", "glu_b"}
    small_w = [norm_g, pool_w, pool_scale, a_re, a_im, log_dt, b_re, b_im, c_re, c_im, d_skip, glu_b, final_g]
    small_g = [g_norm_g, g_pool_w, g_pool_scale, g_a_re, g_a_im, g_log_dt, g_b_re, g_b_im, g_c_re, g_c_im,
               g_d_skip, g_glu_b, g_final_g]
    small_m = [m_norm_g, m_pool_w, m_pool_scale, m_a_re, m_a_im, m_log_dt, m_b_re, m_b_im, m_c_re, m_c_im,
               m_d_skip, m_glu_b, m_final_g]
    small_v = [v_norm_g, v_pool_w, v_pool_scale, v_a_re, v_a_im, v_log_dt, v_b_re, v_b_im, v_c_re, v_c_im,
               v_d_skip, v_glu_b, v_final_g]

    wide_last = {"b_re", "b_im"}

    def blocked(arrays):
        return [a.reshape(1, 1, -1) if n == "final_g" else a[:, None, :] if n in rows
                else a.swapaxes(2, 3) if n in wide_last else a for n, a in zip(names, arrays)]

    small_d, small_nm, small_nv = _adamw_small(blocked(small_w), blocked(small_g), blocked(small_m), blocked(small_v))
    res = {}
    for kind, arrays in (("grad", small_g), ("delta", small_d), ("m", small_nm), ("v", small_nv)):
        for n, a, like in zip(names, arrays, small_w):
            if kind != "grad" and n in wide_last:
                a = a.swapaxes(2, 3)
            res[kind, n] = a.reshape(like.shape)

    (s_win,), (r_win,) = _exchange_wait(pending, 1, small_d[0], "comm_grads_wait_0")
    sent[0], received[0] = (s_win, s_glu, s_wout), (r_win, r_glu, r_wout)
    shard_adamw("w_in", 0)
    for n in ("w_in", "glu_w", "w_out"):
        for pos, kind in enumerate(("grad", "delta", "m", "v")):
            res[kind, n] = shard_res[n][pos]

    order = ["norm_g", "w_in", "pool_w", "pool_scale", "a_re", "a_im", "log_dt", "b_re", "b_im", "c_re", "c_im",
             "d_skip", "glu_w", "glu_b", "w_out", "final_g"]
    outs = [loss, dx.reshape(nb, seq, D_MODEL)]
    for kind in ("grad", "delta", "m", "v"):
        outs += [res[kind, n] for n in order]
    return tuple(outs)
```

```python
import math

import jax
import jax.numpy as jnp
from jax import lax
from jax.experimental import pallas as pl
from jax.experimental.pallas import tpu as pltpu

F32 = jnp.float32
MXU_DTYPE = jnp.bfloat16

D_MODEL = 1024
MIX = 1024
POOL_W = 512
SSM_W = 512
N_POOL_G = 4
POOL_GC = 128
SSM_C = 16
SSM_P = 64
NORM_EPS = 1e-5
N_DEV = 8
W_IN_COLS = 2 * MIX // N_DEV

ADAM_LR = 0.001
ADAM_B1 = 0.9
ADAM_B2 = 0.999
ADAM_EPS = 1e-08
ADAM_WD = 0.01
ADAM_STEP = 10

SUBLANES = 8
LANES = 128
HALO = 16
STATE_ROWS = 8
STATE_COLS = 256
CHUNK_GROUPS = STATE_COLS // SSM_P
CHUNK_CH = CHUNK_GROUPS * SSM_C
T_BLK = 256
SCAN_UNROLL = 16
TM_FWD = 512
TM_BWD = 512
VMEM_LIMIT = 56 * 1024 * 1024

MESH = pl.DeviceIdType.MESH
VMEM_SPEC = pl.BlockSpec(memory_space=pltpu.VMEM)
ANY_SPEC = pl.BlockSpec(memory_space=pl.ANY)


def _mm(a, b):
    return jnp.dot(a, b, preferred_element_type=F32)


def _mm_tn(a, b):
    return lax.dot_general(a, b, (((0,), (0,)), ((), ())), preferred_element_type=F32)


def _mm_nt(a, b):
    return lax.dot_general(a, b, (((1,), (1,)), ((), ())), preferred_element_type=F32)


def _mx(a):
    return a.astype(MXU_DTYPE)


def _sigmoid(v):
    return 1.0 / (1.0 + jnp.exp(-v))


_GELU_C = math.sqrt(2.0 / math.pi)
_GELU_A = 0.044715


def _gelu_and_grad(y):
    th = jnp.tanh(_GELU_C * (y + _GELU_A * y * y * y))
    val = 0.5 * y * (1.0 + th)
    grad = 0.5 * (1.0 + th) + 0.5 * y * (1.0 - th * th) * (_GELU_C * (1.0 + 3.0 * _GELU_A * y * y))
    return val, grad


def _params(**kw):
    return pltpu.CompilerParams(vmem_limit_bytes=VMEM_LIMIT, **kw)


def _of_layer(layer, *shape):
    return pl.BlockSpec((None,) + shape, lambda i: (layer,) + (0,) * len(shape))


def _ssm_dense(a_re, a_im, log_dt, b_re, b_im, c_re, c_im):
    dt = jnp.exp(log_dt)[:, None]
    mag = jnp.exp(a_re * dt)
    ang = a_im * dt
    lb_re = mag * jnp.cos(ang)
    lb_im = mag * jnp.sin(ang)
    den = a_re * a_re + a_im * a_im
    n_re = lb_re - 1.0
    n_im = lb_im
    f_re = (n_re * a_re + n_im * a_im) / den
    f_im = (n_im * a_re - n_re * a_im) / den
    bb_re = f_re[..., None] * b_re - f_im[..., None] * b_im
    bb_im = f_re[..., None] * b_im + f_im[..., None] * b_re

    bb = jnp.stack([bb_re, bb_im], axis=0).reshape(2, STATE_ROWS, CHUNK_GROUPS, SSM_P, SSM_C)
    rb = bb.transpose(1, 4, 0, 2, 3).reshape(STATE_ROWS, SSM_C, 2 * STATE_COLS)
    cc = jnp.stack([c_re, -c_im], axis=0).reshape(2, STATE_ROWS, CHUNK_GROUPS, SSM_C, SSM_P)
    rc = cc.transpose(1, 3, 0, 2, 4).reshape(STATE_ROWS, SSM_C, 2 * STATE_COLS)
    return (lb_re.reshape(STATE_ROWS, STATE_COLS), lb_im.reshape(STATE_ROWS, STATE_COLS), rb, rc)


def _ssm_chunked(per_channel):
    row_group = jnp.arange(CHUNK_CH) // SSM_C
    col_group = (jnp.arange(2 * STATE_COLS) // SSM_P) % CHUNK_GROUPS
    own_group = (row_group[:, None] == col_group[None, :]).astype(F32)
    even = (jnp.arange(STATE_ROWS) % 2 == 0).astype(F32)[:, None, None]
    half = jnp.tile(per_channel, (1, CHUNK_GROUPS, 1)) * own_group
    return jnp.concatenate([half * even, half * (1.0 - even)], axis=1)


def _inproj_fwd(x2, g_rows, w_all, dep, layer):
    n = x2.shape[0]
    tm = TM_FWD

    def body(x_ref, g_ref, w_ref, dep_ref, z_ref, h_ref):
        x = x_ref[...]
        r = lax.rsqrt(jnp.mean(x * x, axis=-1, keepdims=True) + NORM_EPS)
        h = _mx(x * r * g_ref[...])
        h_ref[...] = h
        for d in range(N_DEV):
            z_ref[:, d * W_IN_COLS:(d + 1) * W_IN_COLS] = _mm(h, w_ref[d])

    return pl.pallas_call(
        body, name="inproj_fwd",
        grid=(n // tm,),
        in_specs=[pl.BlockSpec((tm, D_MODEL), lambda i: (i, 0)),
                  _of_layer(layer, 1, D_MODEL),
                  pl.BlockSpec((N_DEV, D_MODEL, W_IN_COLS), lambda i: (0, 0, 0)),
                  ANY_SPEC],
        out_specs=[pl.BlockSpec((tm, 2 * MIX), lambda i: (i, 0)),
                   pl.BlockSpec((tm, D_MODEL), lambda i: (i, 0))],
        out_shape=[jax.ShapeDtypeStruct((n, 2 * MIX), F32),
                   jax.ShapeDtypeStruct((n, D_MODEL), MXU_DTYPE)],
        compiler_params=_params(dimension_semantics=("arbitrary",)),
    )(x2, g_rows, w_all, dep)


def _loss_head(x2, tgt2, g_row):
    n = x2.shape[0]
    tm = TM_FWD

    def body(x_ref, t_ref, g_ref, dx_ref, loss_ref, dg_ref):
        @pl.when(pl.program_id(0) == 0)
        def _():
            loss_ref[...] = jnp.zeros_like(loss_ref)
            dg_ref[...] = jnp.zeros_like(dg_ref)

        x = x_ref[...]
        g = g_ref[...]
        r = lax.rsqrt(jnp.mean(x * x, axis=-1, keepdims=True) + NORM_EPS)
        xh = x * r
        e = xh * g - t_ref[...]
        loss_ref[...] += jnp.sum(jnp.sum(e * e, axis=-1, keepdims=True), axis=0, keepdims=True) * (0.5 / D_MODEL)
        dout = e * (1.0 / D_MODEL)
        dg_ref[...] += jnp.sum(dout * xh, axis=0, keepdims=True)
        gdy = dout * g
        dx_ref[...] = r * (gdy - xh * jnp.mean(xh * gdy, axis=-1, keepdims=True))

    return pl.pallas_call(
        body, name="loss_head",
        grid=(n // tm,),
        in_specs=[pl.BlockSpec((tm, D_MODEL), lambda i: (i, 0)),
                  pl.BlockSpec((tm, D_MODEL), lambda i: (i, 0)),
                  pl.BlockSpec((1, D_MODEL), lambda i: (0, 0))],
        out_specs=[pl.BlockSpec((tm, D_MODEL), lambda i: (i, 0)),
                   pl.BlockSpec((1, 1), lambda i: (0, 0)),
                   pl.BlockSpec((1, D_MODEL), lambda i: (0, 0))],
        out_shape=[jax.ShapeDtypeStruct((n, D_MODEL), F32),
                   jax.ShapeDtypeStruct((1, 1), F32),
                   jax.ShapeDtypeStruct((1, D_MODEL), F32)],
        compiler_params=_params(dimension_semantics=("arbitrary",)),
    )(x2, tgt2, g_row)


def _outproj_bwd(dx2, yg, w_out, dep):
    n = dx2.shape[0]
    tm = TM_BWD
    n_steps = n // tm

    def body(dx_ref, y_ref, w_ref, dep_ref, dy_ref, dw_ref, acc_ref):
        i = pl.program_id(0)

        @pl.when(i == 0)
        def _():
            acc_ref[...] = jnp.zeros_like(acc_ref)

        dxb = _mx(dx_ref[...])
        dy_ref[...] = _mm_nt(dxb, w_ref[...])
        acc_ref[...] += _mm_tn(y_ref[...], dxb)

        @pl.when(i == n_steps - 1)
        def _():
            dw_ref[...] = _mx(acc_ref[...])

    return pl.pallas_call(
        body, name="outproj_bwd",
        grid=(n_steps,),
        in_specs=[pl.BlockSpec((tm, D_MODEL), lambda i: (i, 0)),
                  pl.BlockSpec((tm, MIX), lambda i: (i, 0)),
                  pl.BlockSpec((MIX, D_MODEL), lambda i: (0, 0)),
                  ANY_SPEC],
        out_specs=[pl.BlockSpec((tm, MIX), lambda i: (i, 0)),
                   pl.BlockSpec((MIX, D_MODEL), lambda i: (0, 0))],
        out_shape=[jax.ShapeDtypeStruct((n, MIX), F32),
                   jax.ShapeDtypeStruct((MIX, D_MODEL), MXU_DTYPE)],
        scratch_shapes=[pltpu.VMEM((MIX, D_MODEL), F32)],
        compiler_params=_params(dimension_semantics=("arbitrary",)),
    )(dx2, yg, w_out, dep)


def _inproj_bwd(dz, h, x2, dx_in, g_rows, w_all, dep, layer):
    n = x2.shape[0]
    tm = TM_BWD
    n_steps = n // tm

    def body(dz_ref, h_ref, x_ref, dxi_ref, g_ref, w_ref, dep_ref, dxo_ref, dw_ref, dg_ref, acc_ref, wcat_ref):
        i = pl.program_id(0)

        @pl.when(i == 0)
        def _():
            acc_ref[...] = jnp.zeros_like(acc_ref)
            dg_ref[...] = jnp.zeros_like(dg_ref)
            for d in range(N_DEV):
                wcat_ref[:, d * W_IN_COLS:(d + 1) * W_IN_COLS] = w_ref[d]

        hb = h_ref[...]
        for d in range(N_DEV):
            acc_ref[d] += _mm_tn(hb, dz_ref[:, d * W_IN_COLS:(d + 1) * W_IN_COLS])
        dh = _mm_nt(dz_ref[...], wcat_ref[...])
        x = x_ref[...]
        r = lax.rsqrt(jnp.mean(x * x, axis=-1, keepdims=True) + NORM_EPS)
        xh = x * r
        dg_ref[...] += jnp.sum(dh * xh, axis=0, keepdims=True)
        gdy = dh * g_ref[...]
        dxo_ref[...] = dxi_ref[...] + r * (gdy - xh * jnp.mean(xh * gdy, axis=-1, keepdims=True))

        @pl.when(i == n_steps - 1)
        def _():
            dw_ref[...] = _mx(acc_ref[...])

    return pl.pallas_call(
        body, name="inproj_bwd",
        grid=(n_steps,),
        in_specs=[pl.BlockSpec((tm, 2 * MIX), lambda i: (i, 0)),
                  pl.BlockSpec((tm, D_MODEL), lambda i: (i, 0)),
                  pl.BlockSpec((tm, D_MODEL), lambda i: (i, 0)),
                  pl.BlockSpec((tm, D_MODEL), lambda i: (i, 0)),
                  _of_layer(layer, 1, D_MODEL),
                  pl.BlockSpec((N_DEV, D_MODEL, W_IN_COLS), lambda i: (0, 0, 0)),
                  ANY_SPEC],
        out_specs=[pl.BlockSpec((tm, D_MODEL), lambda i: (i, 0)),
                   pl.BlockSpec((N_DEV, D_MODEL, W_IN_COLS), lambda i: (0, 0, 0)),
                   pl.BlockSpec((1, D_MODEL), lambda i: (0, 0))],
        out_shape=[jax.ShapeDtypeStruct((n, D_MODEL), F32),
                   jax.ShapeDtypeStruct((N_DEV, D_MODEL, W_IN_COLS), MXU_DTYPE),
                   jax.ShapeDtypeStruct((1, D_MODEL), F32)],
        scratch_shapes=[pltpu.VMEM((N_DEV, D_MODEL, W_IN_COLS), F32),
                        pltpu.VMEM((D_MODEL, 2 * MIX), MXU_DTYPE)],
        compiler_params=_params(dimension_semantics=("arbitrary",)),
    )(dz, h, x2, dx_in, g_rows, w_all, dep)


def _row_pos(t0, rows):
    return t0 + lax.broadcasted_iota(jnp.int32, (rows, LANES), 0)


def _pool_window_mean(upad, g, t0, t_blk):
    k = 2 << g
    w = upad
    sh = 1
    while sh < k:
        w = w + pltpu.roll(w, sh, 0)
        sh *= 2
    count = jnp.minimum(_row_pos(t0, t_blk) + 1, k).astype(F32)
    return w[HALO:] / count - upad[HALO:]


def _pool_window_bwd(qpad, g, t_blk):
    k = 2 << g
    n = t_blk + HALO
    w = qpad
    sh = 1
    while sh < k:
        w = w + pltpu.roll(w, n - sh, 0)
        sh *= 2
    return w[:t_blk]


class _StateBuf:
    def __init__(self, refs, t_blk):
        self.refs = refs
        self.t_blk = t_blk

    def put_chunk(self, b, j, val):
        for c in range(4):
            self.refs[4 * b + c][pl.ds(j, self.t_blk, stride=STATE_ROWS), :] = val[:, c * LANES:(c + 1) * LANES]

    def get_chunk(self, b, j):
        return jnp.concatenate(
            [self.refs[4 * b + c][pl.ds(j, self.t_blk, stride=STATE_ROWS), :] for c in range(4)], axis=-1)

    def load(self, b, r, part):
        return jnp.concatenate(
            [self.refs[4 * b + 2 * part + h][pl.ds(r, STATE_ROWS), :] for h in range(2)], axis=-1)

    def store(self, b, r, part, val):
        for h in range(2):
            self.refs[4 * b + 2 * part + h][pl.ds(r, STATE_ROWS), :] = val[:, h * LANES:(h + 1) * LANES]


def _state_scratch(nb, t_blk):
    return [pltpu.VMEM((t_blk * STATE_ROWS, LANES), F32) for _ in range(4 * nb)]


def _ssm_project_in(u_ssm, wb_ref, buf, nb):
    t_blk = u_ssm.shape[0] // nb
    ub = _mx(u_ssm)
    for j in range(STATE_ROWS):
        m = j // 2
        bu = _mm(ub[:, m * LANES:(m + 1) * LANES], wb_ref[j])
        for b in range(nb):
            buf.put_chunk(b, j, bu[b * t_blk:(b + 1) * t_blk])


def _scan_forward(buf, lbr, lbi, init, nb):
    def step(t, carry):
        r = pl.multiple_of(t * STATE_ROWS, STATE_ROWS)
        out = []
        for b in range(nb):
            sr, si = carry[2 * b], carry[2 * b + 1]
            nr = lbr * sr - lbi * si + buf.load(b, r, 0)
            ni = lbr * si + lbi * sr + buf.load(b, r, 1)
            buf.store(b, r, 0, nr)
            buf.store(b, r, 1, ni)
            out += [nr, ni]
        return tuple(out)

    def body(i, carry):
        for u in range(SCAN_UNROLL):
            carry = step(i * SCAN_UNROLL + u, carry)
        return carry

    return lax.fori_loop(0, buf.t_blk // SCAN_UNROLL, body, init)


def _ssm_project_out(chunk, wc_ref):
    tiles = []
    for m in range(4):
        acc = None
        for j in (2 * m, 2 * m + 1):
            part = _mm_nt(chunk(j), wc_ref[j])
            acc = part if acc is None else acc + part
        tiles.append(acc)
    return jnp.concatenate(tiles, axis=-1)


def _layer_fwd(x3, z3, g_rows, w_in, pool_w, pool_scale, lbr, lbi, wb, wc, d_skip, glu_w, glu_b, w_out, dep, layer):
    nb, seq, _ = x3.shape
    t_blk = min(T_BLK, seq)
    n_t = seq // t_blk
    halo_per_blk = t_blk // HALO
    rows = nb * t_blk
    fused = z3 is None

    def body(*refs):
        if fused:
            (x_ref, g_ref, wi_ref, pw_ref, ps_ref, lbr_ref, lbi_ref, wb_ref, wc_ref, dsk_ref, gw_ref, gb_ref, wo_ref,
             dep_ref, z_ref, h_ref, yg_ref, sc_ref, act_ref, dact_ref, pooled_ref, ypre_ref, xo_ref,
             carry_ref, halo_ref, *s_refs) = refs
        else:
            (x_ref, z_ref, zh_ref, pw_ref, ps_ref, lbr_ref, lbi_ref, wb_ref, wc_ref, dsk_ref, gw_ref, gb_ref, wo_ref,
             dep_ref, yg_ref, sc_ref, act_ref, dact_ref, pooled_ref, ypre_ref, xo_ref, carry_ref, *s_refs) = refs
        i = pl.program_id(0)
        t0 = i * t_blk
        buf = _StateBuf(s_refs, t_blk)
        both = lambda lo, hi: z_ref[:, :, lo:hi].reshape(rows, hi - lo)

        @pl.when(i == 0)
        def _():
            carry_ref[...] = jnp.zeros_like(carry_ref)
            if fused:
                halo_ref[...] = jnp.zeros_like(halo_ref)

        x = x_ref[...].reshape(rows, D_MODEL)
        if fused:
            r = lax.rsqrt(jnp.mean(x * x, axis=-1, keepdims=True) + NORM_EPS)
            h = _mx(x * r * g_ref[...])
            h_ref[...] = h.reshape(nb, t_blk, D_MODEL)
            for d in range(N_DEV):
                z_ref[:, :, d * W_IN_COLS:(d + 1) * W_IN_COLS] = _mm(h, wi_ref[d]).reshape(nb, t_blk, W_IN_COLS)

        u_ssm = both(POOL_W, MIX)
        _ssm_project_in(u_ssm, wb_ref, buf, nb)
        init = tuple(carry_ref[b, :, h * STATE_COLS:(h + 1) * STATE_COLS] for b in range(nb) for h in range(2))
        fin = _scan_forward(buf, lbr_ref[...], lbi_ref[...], init, nb)
        for b in range(nb):
            carry_ref[b, :, 0:STATE_COLS] = fin[2 * b]
            carry_ref[b, :, STATE_COLS:2 * STATE_COLS] = fin[2 * b + 1]

        def chunk(j):
            states = _mx(jnp.concatenate([buf.get_chunk(b, j) for b in range(nb)], axis=0))
            sc_ref[:, j] = states.reshape(nb, t_blk, 2 * STATE_COLS)
            return states

        y = _ssm_project_out(chunk, wc_ref) + dsk_ref[...] * u_ssm
        yg, dgelu = _gelu_and_grad(y)
        ygb = _mx(yg)
        act_ref[...] = ygb.reshape(nb, t_blk, SSM_W)
        dact_ref[...] = _mx(dgelu).reshape(nb, t_blk, SSM_W)
        o_ssm = yg * _sigmoid(_mm(ygb, gw_ref[...]) + gb_ref[...])
        gp = both(MIX + POOL_W, 2 * MIX)
        parts = []
        first = (i == 0)
        for g in range(N_POOL_G):
            cols = slice(g * POOL_GC, (g + 1) * POOL_GC)
            pooled = []
            for b in range(nb):
                halo = halo_ref[b, :, cols] if fused else jnp.where(first, 0.0, zh_ref[b, :, cols])
                pooled.append(_pool_window_mean(jnp.concatenate([halo, z_ref[b, :, cols]], axis=0), g, t0, t_blk))
            pb = _mx(jnp.concatenate(pooled, axis=0))
            ypre = _mm(pb, pw_ref[g])
            pooled_ref[:, :, cols] = pb.reshape(nb, t_blk, POOL_GC)
            ypre_ref[:, :, cols] = ypre.reshape(nb, t_blk, POOL_GC)
            gpp = both(MIX + g * POOL_GC, MIX + (g + 1) * POOL_GC)
            parts.append(_mx(ypre * ps_ref[:, cols] * (gpp * _sigmoid(gpp))))
        parts.append(_mx(o_ssm * (gp * _sigmoid(gp))))
        gated = jnp.concatenate(parts, axis=-1)
        yg_ref[...] = gated.reshape(nb, t_blk, MIX)
        xo_ref[...] = (x + _mm(gated, wo_ref[...])).reshape(nb, t_blk, D_MODEL)
        if fused:
            halo_ref[...] = z_ref[:, t_blk - HALO:, 0:POOL_W]

    const = lambda *shape: pl.BlockSpec(shape, lambda i: (0,) * len(shape))
    tokens = lambda width: pl.BlockSpec((nb, t_blk, width), lambda i: (0, i, 0))
    mixer_specs = [_of_layer(layer, N_POOL_G, POOL_GC, POOL_GC), _of_layer(layer, 1, POOL_W),
                   _of_layer(layer, STATE_ROWS, STATE_COLS), _of_layer(layer, STATE_ROWS, STATE_COLS),
                   _of_layer(layer, STATE_ROWS, LANES, 2 * STATE_COLS),
                   _of_layer(layer, STATE_ROWS, LANES, 2 * STATE_COLS),
                   _of_layer(layer, 1, SSM_W), const(SSM_W, SSM_W), _of_layer(layer, 1, SSM_W),
                   const(MIX, D_MODEL), ANY_SPEC]
    mixer_args = (pool_w, pool_scale, lbr, lbi, wb, wc, d_skip, glu_w, glu_b, w_out, dep)
    out_specs = [tokens(MIX), pl.BlockSpec((nb, STATE_ROWS, t_blk, 2 * STATE_COLS), lambda i: (0, 0, i, 0)),
                 tokens(SSM_W), tokens(SSM_W), tokens(POOL_W), tokens(POOL_W), tokens(D_MODEL)]
    out_shape = [jax.ShapeDtypeStruct((nb, seq, MIX), MXU_DTYPE),
                 jax.ShapeDtypeStruct((nb, STATE_ROWS, seq, 2 * STATE_COLS), MXU_DTYPE),
                 jax.ShapeDtypeStruct((nb, seq, SSM_W), MXU_DTYPE),
                 jax.ShapeDtypeStruct((nb, seq, SSM_W), MXU_DTYPE),
                 jax.ShapeDtypeStruct((nb, seq, POOL_W), MXU_DTYPE),
                 jax.ShapeDtypeStruct((nb, seq, POOL_W), F32),
                 jax.ShapeDtypeStruct((nb, seq, D_MODEL), F32)]
    scratch = [pltpu.VMEM((nb, STATE_ROWS, 2 * STATE_COLS), F32)]
    if fused:
        in_specs = [tokens(D_MODEL), _of_layer(layer, 1, D_MODEL), const(N_DEV, D_MODEL, W_IN_COLS)] + mixer_specs
        args = (x3, g_rows, w_in) + mixer_args
        out_specs = [tokens(2 * MIX), tokens(D_MODEL)] + out_specs
        out_shape = [jax.ShapeDtypeStruct((nb, seq, 2 * MIX), F32),
                     jax.ShapeDtypeStruct((nb, seq, D_MODEL), MXU_DTYPE)] + out_shape
        scratch = scratch + [pltpu.VMEM((nb, HALO, POOL_W), F32)]
    else:
        in_specs = [tokens(D_MODEL), tokens(2 * MIX),
                    pl.BlockSpec((nb, HALO, POOL_W), lambda i: (0, jnp.maximum(i * halo_per_blk - 1, 0), 0))] + mixer_specs
        args = (x3, z3, z3) + mixer_args
    return pl.pallas_call(
        body, name="layer_fwd" if fused else "mixer_fwd",
        grid=(n_t,),
        in_specs=in_specs, out_specs=out_specs, out_shape=out_shape,
        scratch_shapes=scratch + _state_scratch(nb, t_blk),
        compiler_params=_params(dimension_semantics=("arbitrary",)),
    )(*args)


def _mixer_bwd(z3, dy3, states, kept, pool_w, pool_scale, lbr, lbi, wb, wc, d_skip, glu_w, glu_b, layer):
    nb, seq, _ = z3.shape
    t_blk = min(T_BLK, seq)
    n_t = seq // t_blk
    halo_per_blk = t_blk // HALO
    rows = nb * t_blk

    def body(z_ref, dy_ref, sc_ref, sch_ref, act_ref, dact_ref, pooled_ref, ypre_ref, pw_ref, ps_ref, lbr_ref, lbi_ref, wb_ref, wc_ref, dsk_ref,
             gw_ref, gb_ref,
             dz_ref, dpw_ref, dps_ref, dlbr_ref, dlbi_ref, dwb_ref, dwc_ref, ddsk_ref, dgw_ref, dgb_ref,
             gcarry_ref, qcarry_ref, du_ref, dgw_acc, *g_refs):
        i = pl.program_id(0)
        blk = n_t - 1 - i
        t0 = blk * t_blk
        gbuf = _StateBuf(g_refs, t_blk)

        @pl.when(i == 0)
        def _():
            gcarry_ref[...] = jnp.zeros_like(gcarry_ref)
            qcarry_ref[...] = jnp.zeros_like(qcarry_ref)
            for ref in (dpw_ref, dps_ref, dlbr_ref, dlbi_ref, dwb_ref, dwc_ref, ddsk_ref, dgw_acc, dgb_ref):
                ref[...] = jnp.zeros_like(ref)

        lbr_v = lbr_ref[...]
        lbi_v = lbi_ref[...]

        both = lambda ref, lo, hi: ref[:, :, lo:hi].reshape(rows, hi - lo)
        split = lambda val: val.reshape(nb, t_blk, val.shape[-1])
        states = lambda j: sc_ref[:, j].reshape(rows, 2 * STATE_COLS)
        first = (blk == 0)

        u_ssm = both(z_ref, POOL_W, MIX)
        ygb = act_ref[...].reshape(rows, SSM_W)
        yg = ygb.astype(F32)
        dgelu = dact_ref[...].reshape(rows, SSM_W).astype(F32)
        sg = _sigmoid(_mm(ygb, gw_ref[...]) + gb_ref[...])
        o_ssm = yg * sg
        gp = both(z_ref, MIX + POOL_W, 2 * MIX)
        sgm = _sigmoid(gp)
        dyv = both(dy_ref, POOL_W, MIX)
        dz_ref[:, :, MIX + POOL_W:2 * MIX] = split(_mx(dyv * o_ssm * (sgm * (1.0 + gp * (1.0 - sgm)))))
        do = dyv * (gp * sgm)
        dv = do * yg * (sg * (1.0 - sg))
        dvb = _mx(dv)
        dgb_ref[...] += jnp.sum(dv, axis=0, keepdims=True)
        dgw_acc[...] += _mm_tn(ygb, dvb)
        dyp = (do * sg + _mm_nt(dvb, gw_ref[...])) * dgelu
        ddsk_ref[...] += jnp.sum(dyp * u_ssm, axis=0, keepdims=True)
        dypb = _mx(dyp)
        for j in range(STATE_ROWS):
            m = j // 2
            dyt = dypb[:, m * LANES:(m + 1) * LANES]
            ds = _mm(dyt, wc_ref[j])
            for b in range(nb):
                gbuf.put_chunk(b, j, ds[b * t_blk:(b + 1) * t_blk])
            dwc_ref[j] += _mm_tn(dyt, states(j))
        du_ref[...] = split(dsk_ref[...] * dyp)

        for g in range(N_POOL_G):
            cols = slice(g * POOL_GC, (g + 1) * POOL_GC)
            pb = both(pooled_ref, g * POOL_GC, (g + 1) * POOL_GC)
            ypre = both(ypre_ref, g * POOL_GC, (g + 1) * POOL_GC)
            gpp = both(z_ref, MIX + g * POOL_GC, MIX + (g + 1) * POOL_GC)
            sgp = _sigmoid(gpp)
            dyg = both(dy_ref, g * POOL_GC, (g + 1) * POOL_GC)
            scale = ps_ref[:, cols]
            dz_ref[:, :, MIX + g * POOL_GC:MIX + (g + 1) * POOL_GC] = split(_mx(
                dyg * (ypre * scale) * (sgp * (1.0 + gpp * (1.0 - sgp)))))
            dyc = dyg * (gpp * sgp)
            dps_ref[:, cols] += jnp.sum(dyc * ypre, axis=0, keepdims=True)
            dypre = _mx(dyc * scale)
            dpw_ref[g] += _mm_tn(pb, dypre)
            dpooled = _mm_nt(dypre, pw_ref[g])
            count = jnp.minimum(_row_pos(t0, t_blk) + 1, 2 << g).astype(F32)
            for b in range(nb):
                dp = dpooled[b * t_blk:(b + 1) * t_blk]
                q = dp / count
                qpad = jnp.concatenate([q, qcarry_ref[b, :, cols]], axis=0)
                qcarry_ref[b, :, cols] = q[:HALO]
                dz_ref[b, :, cols] = _mx(_pool_window_bwd(qpad, g, t_blk) - dp)

        def rev_step(t, carry):
            r = pl.multiple_of(t * STATE_ROWS, STATE_ROWS)
            out = []
            for b in range(nb):
                gr, gi = carry[2 * b], carry[2 * b + 1]
                ngr = lbr_v * gr + lbi_v * gi + gbuf.load(b, r, 0)
                ngi = lbr_v * gi - lbi_v * gr + gbuf.load(b, r, 1)
                gbuf.store(b, r, 0, ngr)
                gbuf.store(b, r, 1, ngi)
                out += [ngr, ngi]
            return tuple(out)

        def rev_body(i, carry):
            for u in range(SCAN_UNROLL):
                carry = rev_step(t_blk - 1 - (i * SCAN_UNROLL + u), carry)
            return carry

        init_g = tuple(gcarry_ref[b, :, h * STATE_COLS:(h + 1) * STATE_COLS] for b in range(nb) for h in range(2))
        fin = lax.fori_loop(0, t_blk // SCAN_UNROLL, rev_body, init_g)
        for b in range(nb):
            gcarry_ref[b, :, 0:STATE_COLS] = fin[2 * b]
            gcarry_ref[b, :, STATE_COLS:2 * STATE_COLS] = fin[2 * b + 1]

        ub = _mx(u_ssm)
        for m in range(4):
            acc = both(du_ref, m * LANES, (m + 1) * LANES)
            for j in (2 * m, 2 * m + 1):
                g = jnp.concatenate([gbuf.get_chunk(b, j) for b in range(nb)], axis=0)
                gj = _mx(g)
                acc = acc + _mm_nt(gj, wb_ref[j])
                dwb_ref[j] += _mm_tn(ub[:, m * LANES:(m + 1) * LANES], gj)
                shifted = []
                for b in range(nb):
                    before = jnp.where(first, 0.0, sch_ref[b, j].astype(F32))
                    spad = jnp.concatenate([before, sc_ref[b, j].astype(F32)], axis=0)
                    shifted.append(pltpu.roll(spad, 1, 0)[HALO:])
                s_prev = jnp.concatenate(shifted, axis=0)
                g_re, g_im = g[:, :STATE_COLS], g[:, STATE_COLS:]
                p_re, p_im = s_prev[:, :STATE_COLS], s_prev[:, STATE_COLS:]
                dlbr_ref[j:j + 1, :] += jnp.sum(g_re * p_re + g_im * p_im, axis=0, keepdims=True)
                dlbi_ref[j:j + 1, :] += jnp.sum(g_im * p_re - g_re * p_im, axis=0, keepdims=True)
            dz_ref[:, :, POOL_W + m * LANES:POOL_W + (m + 1) * LANES] = split(_mx(acc))

        @pl.when(i == n_t - 1)
        def _():
            dgw_ref[...] = _mx(dgw_acc[...])

    const = lambda *shape: pl.BlockSpec(shape, lambda i: (0,) * len(shape))
    rev = lambda i: n_t - 1 - i
    out_shape = [jax.ShapeDtypeStruct((nb, seq, 2 * MIX), MXU_DTYPE),
                 jax.ShapeDtypeStruct((N_POOL_G, POOL_GC, POOL_GC), F32),
                 jax.ShapeDtypeStruct((1, POOL_W), F32),
                 jax.ShapeDtypeStruct((STATE_ROWS, STATE_COLS), F32),
                 jax.ShapeDtypeStruct((STATE_ROWS, STATE_COLS), F32),
                 jax.ShapeDtypeStruct((STATE_ROWS, LANES, 2 * STATE_COLS), F32),
                 jax.ShapeDtypeStruct((STATE_ROWS, LANES, 2 * STATE_COLS), F32),
                 jax.ShapeDtypeStruct((1, SSM_W), F32),
                 jax.ShapeDtypeStruct((SSM_W, SSM_W), MXU_DTYPE),
                 jax.ShapeDtypeStruct((1, SSM_W), F32)]
    return pl.pallas_call(
        body, name="mixer_bwd",
        grid=(n_t,),
        in_specs=[pl.BlockSpec((nb, t_blk, 2 * MIX), lambda i: (0, rev(i), 0)),
                  pl.BlockSpec((nb, t_blk, MIX), lambda i: (0, rev(i), 0)),
                  pl.BlockSpec((nb, STATE_ROWS, t_blk, 2 * STATE_COLS), lambda i: (0, 0, rev(i), 0)),
                  pl.BlockSpec((nb, STATE_ROWS, HALO, 2 * STATE_COLS),
                               lambda i: (0, 0, jnp.maximum(rev(i) * halo_per_blk - 1, 0), 0)),
                  pl.BlockSpec((nb, t_blk, SSM_W), lambda i: (0, rev(i), 0)),
                  pl.BlockSpec((nb, t_blk, SSM_W), lambda i: (0, rev(i), 0)),
                  pl.BlockSpec((nb, t_blk, POOL_W), lambda i: (0, rev(i), 0)),
                  pl.BlockSpec((nb, t_blk, POOL_W), lambda i: (0, rev(i), 0)),
                  _of_layer(layer, N_POOL_G, POOL_GC, POOL_GC), _of_layer(layer, 1, POOL_W),
                  _of_layer(layer, STATE_ROWS, STATE_COLS), _of_layer(layer, STATE_ROWS, STATE_COLS),
                  _of_layer(layer, STATE_ROWS, LANES, 2 * STATE_COLS),
                  _of_layer(layer, STATE_ROWS, LANES, 2 * STATE_COLS),
                  _of_layer(layer, 1, SSM_W), const(SSM_W, SSM_W), _of_layer(layer, 1, SSM_W)],
        out_specs=[pl.BlockSpec((nb, t_blk, 2 * MIX), lambda i: (0, rev(i), 0))]
                  + [const(*s.shape) for s in out_shape[1:]],
        out_shape=out_shape,
        scratch_shapes=[pltpu.VMEM((nb, STATE_ROWS, 2 * STATE_COLS), F32),
                        pltpu.VMEM((nb, HALO, POOL_W), F32),
                        pltpu.VMEM((nb, t_blk, SSM_W), F32),
                        pltpu.VMEM((SSM_W, SSM_W), F32)]
                       + _state_scratch(nb, t_blk),
        compiler_params=_params(dimension_semantics=("arbitrary",)),
    )(z3, dy3, states, states, *kept, pool_w, pool_scale, lbr, lbi, wb, wc, d_skip, glu_w, glu_b)


def _mesh_place():
    x, y, c = lax.axis_index("x"), lax.axis_index("y"), lax.axis_index("c")
    return x, y, c


def _flip(place, k):
    x, y, c = place
    return (1 - x if k & 4 else x, 1 - y if k & 2 else y, 1 - c if k & 1 else c)


def _index(place):
    x, y, c = place
    return 4 * x + 2 * y + c


HBM_SPEC = pl.BlockSpec(memory_space=pltpu.HBM)
SEM_SPEC = pl.BlockSpec(memory_space=pltpu.SEMAPHORE)
_EFFECT = pltpu.SideEffectType.DATAFLOW_SIDE_EFFECTING
N_PEERS = N_DEV - 1


def _exchange_copies(src_refs, land_refs, send_sems, recv_sems):
    me = _mesh_place()
    mine = _index(me)
    out = []
    for a, land_ref in enumerate(land_refs):
        for k in range(1, N_DEV):
            peer = _flip(me, k)
            theirs = _index(peer)
            n = a * N_PEERS + k - 1
            src = src_refs[a].at[theirs] if src_refs else land_ref.at[mine]
            send = pltpu.make_async_remote_copy(
                src_ref=src, dst_ref=land_ref.at[mine], send_sem=send_sems.at[n], recv_sem=recv_sems.at[n],
                device_id=peer, device_id_type=MESH)
            recv = pltpu.make_async_remote_copy(
                src_ref=src, dst_ref=land_ref.at[theirs], send_sem=send_sems.at[n], recv_sem=recv_sems.at[n],
                device_id=peer, device_id_type=MESH)
            out.append((send, recv))
    return out


def _exchange_start(srcs, lands, after, name):
    arrays = tuple(srcs) + tuple(lands)
    n_src, n_all = len(srcs), len(arrays)
    n_copies = len(lands) * N_PEERS

    def body(*refs):
        send_sems, recv_sems = refs[n_all + 1], refs[n_all + 2]
        token = refs[-1]
        for send, _ in _exchange_copies(refs[:n_src], refs[n_src:n_all], send_sems, recv_sems):
            send.start()
        token[...] = jnp.zeros_like(token)

    res = pl.pallas_call(
        body, name=name,
        in_specs=[HBM_SPEC] * n_all + [ANY_SPEC],
        out_specs=[SEM_SPEC, SEM_SPEC] + [HBM_SPEC] * n_all + [VMEM_SPEC],
        out_shape=[pltpu.SemaphoreType.DMA((n_copies,)), pltpu.SemaphoreType.DMA((n_copies,))]
                  + [pltpu.HBM(a.shape, a.dtype) for a in arrays] + [jax.ShapeDtypeStruct((SUBLANES, LANES), F32)],
        input_output_aliases={i: 2 + i for i in range(n_all)},
        compiler_params=pltpu.CompilerParams(has_side_effects=_EFFECT),
    )(*[pltpu.with_memory_space_constraint(a, pltpu.HBM) for a in arrays], after)
    return tuple(res[:-1]), res[-1]


def _exchange_wait(handle, n_lands, after, name):
    send_sems, recv_sems = handle[0], handle[1]
    arrays = handle[2:]
    n_all = len(arrays)
    n_src = n_all - n_lands

    def body(*refs):
        for send, recv in _exchange_copies(refs[:n_src], refs[n_src:n_all], refs[n_all], refs[n_all + 1]):
            send.wait_send()
            recv.wait_recv()

    res = pl.pallas_call(
        body, name=name,
        in_specs=[HBM_SPEC] * n_all + [SEM_SPEC, SEM_SPEC, ANY_SPEC],
        out_specs=[HBM_SPEC] * n_all,
        out_shape=[pltpu.HBM(a.shape, a.dtype) for a in arrays],
        input_output_aliases={i: i for i in range(n_all)},
        compiler_params=pltpu.CompilerParams(has_side_effects=_EFFECT),
    )(*arrays, send_sems, recv_sems, after)
    return tuple(res[:n_src]), tuple(res[n_src:])


def _weight_zones(w_in, glu_w, w_out, my_idx):
    shards = (w_in, glu_w, w_out)
    depth = w_in.shape[0]

    def body(idx_ref, *refs):
        ins, zones = refs[:len(shards)], refs[len(shards):]
        for l in range(depth):
            for a, src in enumerate(ins):
                zones[l * len(shards) + a][0] = _mx(src[l])

    whole = lambda s: pl.BlockSpec(s.shape, lambda i, idx: (0,) * s.ndim)
    return pl.pallas_call(
        body, name="weight_zones",
        grid_spec=pltpu.PrefetchScalarGridSpec(
            num_scalar_prefetch=1, grid=(1,),
            in_specs=[whole(s) for s in shards],
            out_specs=[pl.BlockSpec((1,) + s.shape[1:], lambda i, idx: (idx[0], 0, 0))
                       for _ in range(depth) for s in shards]),
        out_shape=[jax.ShapeDtypeStruct((N_DEV,) + s.shape[1:], MXU_DTYPE) for _ in range(depth) for s in shards],
        compiler_params=_params(dimension_semantics=("arbitrary",)),
    )(my_idx.reshape(1).astype(jnp.int32), *shards)


def _allreduce_parts(parts):
    n_parts = len(parts)
    pieces = [(a, i, k) for i, a in enumerate(parts) for k in range(a.shape[1] // LANES)]
    offsets = [0]
    for a, _, _ in pieces:
        offsets.append(offsets[-1] + a.shape[0])
    rows = offsets[-1]
    assert rows % _PACK_ROWS == 0, rows
    half = rows // 2
    quarter = half // 4

    def body(*refs):
        in_refs, out_refs = refs[:n_parts], refs[n_parts:2 * n_parts]
        p_ref, o_ref, part_ref, sib_ref, got_ref, send_sems, recv_sems = refs[2 * n_parts:]
        for (a, i, k), off in zip(pieces, offsets):
            p_ref[off:off + a.shape[0], :] = in_refs[i][:, k * LANES:(k + 1) * LANES]

        x, y, c = _mesh_place()
        sibling = (x, y, 1 - c)
        chip = 2 * x + y
        chips = [(k, (1 - x if k & 2 else x, 1 - y if k & 1 else y, c), chip ^ k) for k in (1, 2, 3)]
        my_half = pl.multiple_of(c * half, SUBLANES)
        other_half = pl.multiple_of((1 - c) * half, SUBLANES)

        def copy(n, src, dst, to):
            return pltpu.make_async_remote_copy(src_ref=src, dst_ref=dst, send_sem=send_sems.at[n],
                                                recv_sem=recv_sems.at[n], device_id=to, device_id_type=MESH)

        def quarter_of(ref, base, q):
            return ref.at[pl.ds(pl.multiple_of(base + q * quarter, SUBLANES), quarter)]

        swap = copy(0, p_ref.at[pl.ds(other_half, half)], sib_ref, sibling)
        swap.start()
        swap.wait()
        part_ref[...] = p_ref[pl.ds(my_half, half), :] + sib_ref[...]

        scatter = [copy(k, quarter_of(part_ref, 0, q), got_ref.at[k - 1], to) for k, to, q in chips]
        for cp in scatter:
            cp.start()
        total = part_ref[pl.ds(pl.multiple_of(chip * quarter, SUBLANES), quarter), :]
        for cp, (k, _, _) in zip(scatter, chips):
            cp.wait()
            total = total + got_ref[k - 1]
        mine = pl.multiple_of(my_half + chip * quarter, SUBLANES)
        o_ref[pl.ds(mine, quarter), :] = total

        gather = [copy(3 + k, o_ref.at[pl.ds(mine, quarter)], o_ref.at[pl.ds(mine, quarter)], to) for k, to, _ in chips]
        for cp in gather:
            cp.start()
        for k, to, q in chips:
            theirs = quarter_of(o_ref, my_half, q)
            copy(3 + k, theirs, theirs, to).wait_recv()
        for cp in gather:
            cp.wait_send()

        back = copy(7, o_ref.at[pl.ds(my_half, half)], o_ref.at[pl.ds(my_half, half)], sibling)
        back.start()
        copy(7, o_ref.at[pl.ds(other_half, half)], o_ref.at[pl.ds(other_half, half)], sibling).wait_recv()
        back.wait_send()

        for (a, i, k), off in zip(pieces, offsets):
            out_refs[i][:, k * LANES:(k + 1) * LANES] = o_ref[off:off + a.shape[0], :]

    return pl.pallas_call(
        body, name="comm_allreduce_parts",
        in_specs=[VMEM_SPEC] * n_parts,
        out_specs=[VMEM_SPEC] * n_parts,
        out_shape=[jax.ShapeDtypeStruct(a.shape, F32) for a in parts],
        scratch_shapes=[pltpu.VMEM((rows, LANES), F32),
                        pltpu.VMEM((rows, LANES), F32),
                        pltpu.VMEM((half, LANES), F32),
                        pltpu.VMEM((half, LANES), F32),
                        pltpu.VMEM((3, quarter, LANES), F32),
                        pltpu.SemaphoreType.DMA((8,)),
                        pltpu.SemaphoreType.DMA((8,))],
        compiler_params=_params(),
    )(*parts)


def _adamw_math(w, g, m, v):
    m = ADAM_B1 * m + (1.0 - ADAM_B1) * g
    v = ADAM_B2 * v + (1.0 - ADAM_B2) * (g * g)
    m_hat = m / (1.0 - ADAM_B1 ** ADAM_STEP)
    v_hat = v / (1.0 - ADAM_B2 ** ADAM_STEP)
    delta = -ADAM_LR * (m_hat / (jnp.sqrt(v_hat) + ADAM_EPS) + ADAM_WD * w)
    return delta, m, v


def _adamw_summed(received, own, my_idx, w, m, v, name):
    depth, r, c = w.shape
    tr = min(r, 128)

    def body(idx_ref, *refs):
        r_refs, o_refs = refs[:depth], refs[depth:2 * depth]
        w_ref, m_ref, v_ref, g_ref, d_ref, nm_ref, nv_ref = refs[2 * depth:]
        me = idx_ref[0]
        for l in range(depth):
            g = jnp.zeros((tr, c), F32)
            for q in range(N_DEV):
                g = g + jnp.where(q == me, o_refs[l][0], r_refs[l][q]).astype(F32)
            g_ref[l] = g
            d_ref[l], nm_ref[l], nv_ref[l] = _adamw_math(w_ref[l], g, m_ref[l], v_ref[l])

    blk = pl.BlockSpec((depth, tr, c), lambda i, idx: (0, i, 0))
    return pl.pallas_call(
        body, name=name,
        grid_spec=pltpu.PrefetchScalarGridSpec(
            num_scalar_prefetch=1, grid=(r // tr,),
            in_specs=[pl.BlockSpec((N_DEV, tr, c), lambda i, idx: (0, i, 0))] * depth
                     + [pl.BlockSpec((1, tr, c), lambda i, idx: (idx[0], i, 0))] * depth
                     + [blk, blk, blk],
            out_specs=[blk] * 4),
        out_shape=[jax.ShapeDtypeStruct((depth, r, c), F32)] * 4,
        compiler_params=_params(dimension_semantics=("arbitrary",)),
    )(my_idx.reshape(1).astype(jnp.int32), *received, *own, w, m, v)


def _adamw_small(ws, gs, ms, vs):
    n = len(ws)
    depth = ws[0].shape[0]

    def spec(a):
        per_layer = a.shape[0] == depth
        rest = (0,) * (a.ndim - 1)
        return pl.BlockSpec((1,) + a.shape[1:], lambda l: ((l if per_layer else 0),) + rest)

    def body(*refs):
        w_refs, g_refs, m_refs, v_refs = (refs[k * n:(k + 1) * n] for k in range(4))
        d_refs, nm_refs, nv_refs = (refs[(4 + k) * n:(5 + k) * n] for k in range(3))
        for k in range(n):
            d_refs[k][...], nm_refs[k][...], nv_refs[k][...] = _adamw_math(
                w_refs[k][...], g_refs[k][...], m_refs[k][...], v_refs[k][...])

    specs = [spec(a) for a in ws]
    shapes = [jax.ShapeDtypeStruct(a.shape, F32) for a in ws]
    res = pl.pallas_call(
        body, name="adamw_small",
        grid=(depth,),
        in_specs=specs * 4,
        out_specs=specs * 3,
        out_shape=shapes * 3,
        compiler_params=_params(dimension_semantics=("arbitrary",)),
    )(*ws, *gs, *ms, *vs)
    return res[:n], res[n:2 * n], res[2 * n:]


_PACK_ROWS = SUBLANES * N_DEV


def _pack(arrays):
    flat = jnp.concatenate([a.reshape(-1) for a in arrays])
    per = _PACK_ROWS * LANES
    total = -(-flat.shape[0] // per) * per
    flat = jnp.pad(flat, (0, total - flat.shape[0]))
    return flat.reshape(total // LANES, LANES)


def _unpack(packed, like):
    flat = packed.reshape(-1)
    out = []
    off = 0
    for a in like:
        out.append(flat[off:off + a.size].reshape(a.shape))
        off += a.size
    return out


def kernel(x, norm_g, w_in, pool_w, pool_scale, a_re, a_im, log_dt, b_re, b_im, c_re, c_im, d_skip, glu_w, glu_b, w_out, final_g, loss_target, m_norm_g, m_w_in, m_pool_w, m_pool_scale, m_a_re, m_a_im, m_log_dt, m_b_re, m_b_im, m_c_re, m_c_im, m_d_skip, m_glu_w, m_glu_b, m_w_out, m_final_g, v_norm_g, v_w_in, v_pool_w, v_pool_scale, v_a_re, v_a_im, v_log_dt, v_b_re, v_b_im, v_c_re, v_c_im, v_d_skip, v_glu_w, v_glu_b, v_w_out, v_final_g):
    nb, seq, _ = x.shape
    n_tok = nb * seq
    depth = norm_g.shape[0]

    my_idx = _index(_mesh_place())

    zones = _weight_zones(w_in, glu_w, w_out, my_idx)

    def gather_start(l, after):
        return _exchange_start((), zones[3 * l:3 * l + 3], after, f"comm_gather_start_{l}")

    def gather_wait(handle, after, l):
        _, (win, glu, wout) = _exchange_wait(handle, 3, after, f"comm_gather_wait_{l}")
        return win, glu.reshape(SSM_W, SSM_W), wout.reshape(MIX, D_MODEL)

    xs = [x.reshape(n_tok, D_MODEL)]
    first_w_in, dep = _exchange_start((), zones[0:1], xs[0], "comm_gather_start_0_w_in")

    (lbr, lbi, rb, rc), dense_vjp = jax.vjp(jax.vmap(_ssm_dense), a_re, a_im, log_dt + dep[0, 0], b_re, b_im, c_re, c_im)
    chunk_all = jax.vmap(_ssm_chunked)
    (wb, wct), chunk_vjp = jax.vjp(lambda p, q: (chunk_all(p), chunk_all(q)), rb, rc)
    wb_m, wct_m = _mx(wb), _mx(wct)
    pool_w_m = _mx(pool_w)
    rows_of = lambda a: a[:, None, :]
    norm_rows, scale_rows, skip_rows, bias_rows = rows_of(norm_g), rows_of(pool_scale), rows_of(d_skip), rows_of(glu_b)

    def layer_params(l):
        return (pool_w_m, scale_rows, lbr, lbi, wb_m, wct_m, skip_rows, weights[l][1], bias_rows)

    saved = []
    weights = []
    for l in range(depth):
        if l == 0:
            _, (win,) = _exchange_wait(first_w_in, 1, wct_m, "comm_gather_wait_0_w_in")
            rest, dep = _exchange_start((), zones[1:3], win, "comm_gather_start_0_rest")
            z, h = _inproj_fwd(xs[-1], norm_rows, win, dep, l)
            _, (glu, wout) = _exchange_wait(rest, 2, z, "comm_gather_wait_0_rest")
            weights.append((win, glu.reshape(SSM_W, SSM_W), wout.reshape(MIX, D_MODEL)))
            handle, dep = gather_start(1, weights[0][2])
            z3 = z.reshape(nb, seq, 2 * MIX)
            yg, states, *kept, x_next = _layer_fwd(xs[-1].reshape(nb, seq, D_MODEL), z3, None, None,
                                                   *layer_params(l), weights[l][2], dep, l)
        else:
            weights.append(gather_wait(handle, xs[-1], l))
            if l + 1 < depth:
                handle, dep = gather_start(l + 1, weights[l][0])
            z3, h3, yg, states, *kept, x_next = _layer_fwd(xs[-1].reshape(nb, seq, D_MODEL), None, norm_rows,
                                                           weights[l][0], *layer_params(l), weights[l][2], dep, l)
            h = h3.reshape(n_tok, D_MODEL)
        xs.append(x_next.reshape(n_tok, D_MODEL))
        saved.append((z3, h, yg.reshape(n_tok, MIX), states, kept))

    dx, loss_part, d_final_g = _loss_head(xs[-1], loss_target.reshape(n_tok, D_MODEL), final_g[None])

    small = {k: [None] * depth for k in
             ("norm_g", "pool_w", "pool_scale", "lbr", "lbi", "wb", "wct", "d_skip", "glu_b")}
    received = [None] * depth
    sent = [None] * depth
    pending = None
    early = None
    for l in reversed(range(depth)):
        z3, h, yg2, states, kept = saved[l]
        dy, d_wout = _outproj_bwd(dx, yg2, weights[l][2], dep)
        (dz, d_pw, d_ps, d_lbr, d_lbi, d_wb, d_wct, d_dsk, d_gw, d_gb) = _mixer_bwd(
            z3, dy.reshape(nb, seq, MIX), states, kept, *layer_params(l), l)
        rest = (d_gw.reshape(N_DEV, SSM_W // N_DEV, SSM_W), d_wout.reshape(N_DEV, MIX // N_DEV, D_MODEL))
        if l == 0:
            early, dep = _exchange_start(rest, tuple(lax.empty(s.shape, s.dtype) for s in rest), dz,
                                         "comm_grads_start_0_rest")
        dx, d_win, d_ng = _inproj_bwd(dz.reshape(n_tok, 2 * MIX), h, xs[l], dx, norm_rows, weights[l][0], dep, l)
        for k, val in (("norm_g", d_ng[0]), ("pool_w", d_pw), ("pool_scale", d_ps[0]), ("lbr", d_lbr),
                       ("lbi", d_lbi), ("wb", d_wb), ("wct", d_wct), ("d_skip", d_dsk[0]), ("glu_b", d_gb[0])):
            small[k][l] = val
        if pending is not None:
            sent[l + 1], received[l + 1] = _exchange_wait(pending, 3, dx, f"comm_grads_wait_{l + 1}")
        srcs = (d_win,) if l == 0 else (d_win,) + rest
        lands = tuple(lax.empty(s.shape, s.dtype) for s in srcs)
        pending, dep = _exchange_start(srcs, lands, dx, f"comm_grads_start_{l}")
    shard_res = {}
    shard_inputs = {"w_in": (w_in, m_w_in, v_w_in), "glu_w": (glu_w, m_glu_w, v_glu_w), "w_out": (w_out, m_w_out, v_w_out)}

    def shard_adamw(n, pos):
        w, m, v = shard_inputs[n]
        shard_res[n] = _adamw_summed([received[l][pos] for l in range(depth)], [sent[l][pos] for l in range(depth)],
                                     my_idx, w, m, v, "adamw_" + n)

    (s_glu, s_wout), (r_glu, r_wout) = _exchange_wait(early, 2, dx, "comm_grads_wait_0_rest")
    sent[0], received[0] = (None, s_glu, s_wout), (None, r_glu, r_wout)
    shard_adamw("glu_w", 1)
    shard_adamw("w_out", 2)

    stack = lambda k: jnp.stack(small[k])
    d_rb, d_rc = chunk_vjp((stack("wb"), stack("wct")))
    few = [stack("norm_g"), stack("pool_scale"), stack("lbr"), stack("lbi"), stack("d_skip"), stack("glu_b"),
           d_final_g[0] + dep[0, 0], loss_part[0]]
    big = [stack("pool_w"), d_rb, d_rc]
    summed = _allreduce_parts([a.reshape(-1, a.shape[-1]) for a in big] + [_pack(few)])
    g_pool_w, g_rb, g_rc = (g.reshape(a.shape) for g, a in zip(summed, big))
    g_norm_g, g_pool_scale, g_lbr, g_lbi, g_d_skip, g_glu_b, g_final_g, loss = _unpack(summed[-1], few)
    loss = loss[0]
    g_a_re, g_a_im, g_log_dt, g_b_re, g_b_im, g_c_re, g_c_im = dense_vjp((g_lbr, g_lbi, g_rb, g_rc))

    names = ["norm_g", "pool_w", "pool_scale", "a_re", "a_im", "log_dt", "b_re", "b_im", "c_re", "c_im",
             "d_skip", "glu_b", "final_g"]
    rows = {"norm_g", "pool_scale", "log_dt", "d_skip", "glu_b"}
    small_w = [norm_g, pool_w, pool_scale, a_re, a_im, log_dt, b_re, b_im, c_re, c_im, d_skip, glu_b, final_g]
    small_g = [g_norm_g, g_pool_w, g_pool_scale, g_a_re, g_a_im, g_log_dt, g_b_re, g_b_im, g_c_re, g_c_im,
               g_d_skip, g_glu_b, g_final_g]
    small_m = [m_norm_g, m_pool_w, m_pool_scale, m_a_re, m_a_im, m_log_dt, m_b_re, m_b_im, m_c_re, m_c_im,
               m_d_skip, m_glu_b, m_final_g]
    small_v = [v_norm_g, v_pool_w, v_pool_scale, v_a_re, v_a_im, v_log_dt, v_b_re, v_b_im, v_c_re, v_c_im,
               v_d_skip, v_glu_b, v_final_g]

    wide_last = {"b_re", "b_im"}

    def blocked(arrays):
        return [a.reshape(1, 1, -1) if n == "final_g" else a[:, None, :] if n in rows
                else a.swapaxes(2, 3) if n in wide_last else a for n, a in zip(names, arrays)]

    small_d, small_nm, small_nv = _adamw_small(blocked(small_w), blocked(small_g), blocked(small_m), blocked(small_v))
    res = {}
    for kind, arrays in (("grad", small_g), ("delta", small_d), ("m", small_nm), ("v", small_nv)):
        for n, a, like in zip(names, arrays, small_w):
            if kind != "grad" and n in wide_last:
                a = a.swapaxes(2, 3)
            res[kind, n] = a.reshape(like.shape)

    (s_win,), (r_win,) = _exchange_wait(pending, 1, small_d[0], "comm_grads_wait_0")
    sent[0], received[0] = (s_win, s_glu, s_wout), (r_win, r_glu, r_wout)
    shard_adamw("w_in", 0)
    for n in ("w_in", "glu_w", "w_out"):
        for pos, kind in enumerate(("grad", "delta", "m", "v")):
            res[kind, n] = shard_res[n][pos]

    order = ["norm_g", "w_in", "pool_w", "pool_scale", "a_re", "a_im", "log_dt", "b_re", "b_im", "c_re", "c_im",
             "d_skip", "glu_w", "glu_b", "w_out", "final_g"]
    outs = [loss, dx.reshape(nb, seq, D_MODEL)]
    for kind in ("grad", "delta", "m", "v"):
        outs += [res[kind, n] for n in order]
    return tuple(outs)
```

```python
import math

import jax
import jax.numpy as jnp
from jax import lax
from jax.experimental import pallas as pl
from jax.experimental.pallas import tpu as pltpu

F32 = jnp.float32
MXU_DTYPE = jnp.bfloat16

D_MODEL = 1024
MIX = 1024
POOL_W = 512
SSM_W = 512
N_POOL_G = 4
POOL_GC = 128
SSM_C = 16
SSM_P = 64
NORM_EPS = 1e-5
N_DEV = 8
W_IN_COLS = 2 * MIX // N_DEV

ADAM_LR = 0.001
ADAM_B1 = 0.9
ADAM_B2 = 0.999
ADAM_EPS = 1e-08
ADAM_WD = 0.01
ADAM_STEP = 10

SUBLANES = 8
LANES = 128
HALO = 16
STATE_ROWS = 8
STATE_COLS = 256
CHUNK_GROUPS = STATE_COLS // SSM_P
CHUNK_CH = CHUNK_GROUPS * SSM_C
T_BLK = 256
SCAN_UNROLL = 16
TM_FWD = 512
TM_BWD = 512
VMEM_LIMIT = 56 * 1024 * 1024

MESH = pl.DeviceIdType.MESH
VMEM_SPEC = pl.BlockSpec(memory_space=pltpu.VMEM)
ANY_SPEC = pl.BlockSpec(memory_space=pl.ANY)


def _mm(a, b):
    return jnp.dot(a, b, preferred_element_type=F32)


def _mm_tn(a, b):
    return lax.dot_general(a, b, (((0,), (0,)), ((), ())), preferred_element_type=F32)


def _mm_nt(a, b):
    return lax.dot_general(a, b, (((1,), (1,)), ((), ())), preferred_element_type=F32)


def _mx(a):
    return a.astype(MXU_DTYPE)


def _sigmoid(v):
    return 1.0 / (1.0 + jnp.exp(-v))


_GELU_C = math.sqrt(2.0 / math.pi)
_GELU_A = 0.044715


def _gelu_and_grad(y):
    th = jnp.tanh(_GELU_C * (y + _GELU_A * y * y * y))
    val = 0.5 * y * (1.0 + th)
    grad = 0.5 * (1.0 + th) + 0.5 * y * (1.0 - th * th) * (_GELU_C * (1.0 + 3.0 * _GELU_A * y * y))
    return val, grad


def _params(**kw):
    return pltpu.CompilerParams(vmem_limit_bytes=VMEM_LIMIT, **kw)


def _of_layer(layer, *shape):
    return pl.BlockSpec((None,) + shape, lambda i: (layer,) + (0,) * len(shape))


def _ssm_dense(a_re, a_im, log_dt, b_re, b_im, c_re, c_im):
    dt = jnp.exp(log_dt)[:, None]
    mag = jnp.exp(a_re * dt)
    ang = a_im * dt
    lb_re = mag * jnp.cos(ang)
    lb_im = mag * jnp.sin(ang)
    den = a_re * a_re + a_im * a_im
    n_re = lb_re - 1.0
    n_im = lb_im
    f_re = (n_re * a_re + n_im * a_im) / den
    f_im = (n_im * a_re - n_re * a_im) / den
    bb_re = f_re[..., None] * b_re - f_im[..., None] * b_im
    bb_im = f_re[..., None] * b_im + f_im[..., None] * b_re

    bb = jnp.stack([bb_re, bb_im], axis=0).reshape(2, STATE_ROWS, CHUNK_GROUPS, SSM_P, SSM_C)
    rb = bb.transpose(1, 4, 0, 2, 3).reshape(STATE_ROWS, SSM_C, 2 * STATE_COLS)
    cc = jnp.stack([c_re, -c_im], axis=0).reshape(2, STATE_ROWS, CHUNK_GROUPS, SSM_C, SSM_P)
    rc = cc.transpose(1, 3, 0, 2, 4).reshape(STATE_ROWS, SSM_C, 2 * STATE_COLS)
    return (lb_re.reshape(STATE_ROWS, STATE_COLS), lb_im.reshape(STATE_ROWS, STATE_COLS), rb, rc)


def _ssm_chunked(per_channel):
    row_group = jnp.arange(CHUNK_CH) // SSM_C
    col_group = (jnp.arange(2 * STATE_COLS) // SSM_P) % CHUNK_GROUPS
    own_group = (row_group[:, None] == col_group[None, :]).astype(F32)
    even = (jnp.arange(STATE_ROWS) % 2 == 0).astype(F32)[:, None, None]
    half = jnp.tile(per_channel, (1, CHUNK_GROUPS, 1)) * own_group
    return jnp.concatenate([half * even, half * (1.0 - even)], axis=1)


def _inproj_fwd(x2, g_rows, w_all, dep, layer):
    n = x2.shape[0]
    tm = TM_FWD

    def body(x_ref, g_ref, w_ref, dep_ref, z_ref, h_ref):
        x = x_ref[...]
        r = lax.rsqrt(jnp.mean(x * x, axis=-1, keepdims=True) + NORM_EPS)
        h = _mx(x * r * g_ref[...])
        h_ref[...] = h
        for d in range(N_DEV):
            z_ref[:, d * W_IN_COLS:(d + 1) * W_IN_COLS] = _mm(h, w_ref[d])

    return pl.pallas_call(
        body, name="inproj_fwd",
        grid=(n // tm,),
        in_specs=[pl.BlockSpec((tm, D_MODEL), lambda i: (i, 0)),
                  _of_layer(layer, 1, D_MODEL),
                  pl.BlockSpec((N_DEV, D_MODEL, W_IN_COLS), lambda i: (0, 0, 0)),
                  ANY_SPEC],
        out_specs=[pl.BlockSpec((tm, 2 * MIX), lambda i: (i, 0)),
                   pl.BlockSpec((tm, D_MODEL), lambda i: (i, 0))],
        out_shape=[jax.ShapeDtypeStruct((n, 2 * MIX), F32),
                   jax.ShapeDtypeStruct((n, D_MODEL), MXU_DTYPE)],
        compiler_params=_params(dimension_semantics=("arbitrary",)),
    )(x2, g_rows, w_all, dep)


def _loss_head(x2, tgt2, g_row):
    n = x2.shape[0]
    tm = TM_FWD

    def body(x_ref, t_ref, g_ref, dx_ref, loss_ref, dg_ref):
        @pl.when(pl.program_id(0) == 0)
        def _():
            loss_ref[...] = jnp.zeros_like(loss_ref)
            dg_ref[...] = jnp.zeros_like(dg_ref)

        x = x_ref[...]
        g = g_ref[...]
        r = lax.rsqrt(jnp.mean(x * x, axis=-1, keepdims=True) + NORM_EPS)
        xh = x * r
        e = xh * g - t_ref[...]
        loss_ref[...] += jnp.sum(jnp.sum(e * e, axis=-1, keepdims=True), axis=0, keepdims=True) * (0.5 / D_MODEL)
        dout = e * (1.0 / D_MODEL)
        dg_ref[...] += jnp.sum(dout * xh, axis=0, keepdims=True)
        gdy = dout * g
        dx_ref[...] = r * (gdy - xh * jnp.mean(xh * gdy, axis=-1, keepdims=True))

    return pl.pallas_call(
        body, name="loss_head",
        grid=(n // tm,),
        in_specs=[pl.BlockSpec((tm, D_MODEL), lambda i: (i, 0)),
                  pl.BlockSpec((tm, D_MODEL), lambda i: (i, 0)),
                  pl.BlockSpec((1, D_MODEL), lambda i: (0, 0))],
        out_specs=[pl.BlockSpec((tm, D_MODEL), lambda i: (i, 0)),
                   pl.BlockSpec((1, 1), lambda i: (0, 0)),
                   pl.BlockSpec((1, D_MODEL), lambda i: (0, 0))],
        out_shape=[jax.ShapeDtypeStruct((n, D_MODEL), F32),
                   jax.ShapeDtypeStruct((1, 1), F32),
                   jax.ShapeDtypeStruct((1, D_MODEL), F32)],
        compiler_params=_params(dimension_semantics=("arbitrary",)),
    )(x2, tgt2, g_row)


def _outproj_bwd(dx2, yg, w_out, dep):
    n = dx2.shape[0]
    tm = TM_BWD
    n_steps = n // tm

    def body(dx_ref, y_ref, w_ref, dep_ref, dy_ref, dw_ref, acc_ref):
        i = pl.program_id(0)

        @pl.when(i == 0)
        def _():
            acc_ref[...] = jnp.zeros_like(acc_ref)

        dxb = _mx(dx_ref[...])
        dy_ref[...] = _mm_nt(dxb, w_ref[...])
        acc_ref[...] += _mm_tn(y_ref[...], dxb)

        @pl.when(i == n_steps - 1)
        def _():
            dw_ref[...] = _mx(acc_ref[...])

    return pl.pallas_call(
        body, name="outproj_bwd",
        grid=(n_steps,),
        in_specs=[pl.BlockSpec((tm, D_MODEL), lambda i: (i, 0)),
                  pl.BlockSpec((tm, MIX), lambda i: (i, 0)),
                  pl.BlockSpec((MIX, D_MODEL), lambda i: (0, 0)),
                  ANY_SPEC],
        out_specs=[pl.BlockSpec((tm, MIX), lambda i: (i, 0)),
                   pl.BlockSpec((MIX, D_MODEL), lambda i: (0, 0))],
        out_shape=[jax.ShapeDtypeStruct((n, MIX), F32),
                   jax.ShapeDtypeStruct((MIX, D_MODEL), MXU_DTYPE)],
        scratch_shapes=[pltpu.VMEM((MIX, D_MODEL), F32)],
        compiler_params=_params(dimension_semantics=("arbitrary",)),
    )(dx2, yg, w_out, dep)


def _inproj_bwd(dz, h, x2, dx_in, g_rows, w_all, dep, layer):
    n = x2.shape[0]
    tm = TM_BWD
    n_steps = n // tm

    def body(dz_ref, h_ref, x_ref, dxi_ref, g_ref, w_ref, dep_ref, dxo_ref, dw_ref, dg_ref, acc_ref, wcat_ref):
        i = pl.program_id(0)

        @pl.when(i == 0)
        def _():
            acc_ref[...] = jnp.zeros_like(acc_ref)
            dg_ref[...] = jnp.zeros_like(dg_ref)
            for d in range(N_DEV):
                wcat_ref[:, d * W_IN_COLS:(d + 1) * W_IN_COLS] = w_ref[d]

        hb = h_ref[...]
        for d in range(N_DEV):
            acc_ref[d] += _mm_tn(hb, dz_ref[:, d * W_IN_COLS:(d + 1) * W_IN_COLS])
        dh = _mm_nt(dz_ref[...], wcat_ref[...])
        x = x_ref[...]
        r = lax.rsqrt(jnp.mean(x * x, axis=-1, keepdims=True) + NORM_EPS)
        xh = x * r
        dg_ref[...] += jnp.sum(dh * xh, axis=0, keepdims=True)
        gdy = dh * g_ref[...]
        dxo_ref[...] = dxi_ref[...] + r * (gdy - xh * jnp.mean(xh * gdy, axis=-1, keepdims=True))

        @pl.when(i == n_steps - 1)
        def _():
            dw_ref[...] = _mx(acc_ref[...])

    return pl.pallas_call(
        body, name="inproj_bwd",
        grid=(n_steps,),
        in_specs=[pl.BlockSpec((tm, 2 * MIX), lambda i: (i, 0)),
                  pl.BlockSpec((tm, D_MODEL), lambda i: (i, 0)),
                  pl.BlockSpec((tm, D_MODEL), lambda i: (i, 0)),
                  pl.BlockSpec((tm, D_MODEL), lambda i: (i, 0)),
                  _of_layer(layer, 1, D_MODEL),
                  pl.BlockSpec((N_DEV, D_MODEL, W_IN_COLS), lambda i: (0, 0, 0)),
                  ANY_SPEC],
        out_specs=[pl.BlockSpec((tm, D_MODEL), lambda i: (i, 0)),
                   pl.BlockSpec((N_DEV, D_MODEL, W_IN_COLS), lambda i: (0, 0, 0)),
                   pl.BlockSpec((1, D_MODEL), lambda i: (0, 0))],
        out_shape=[jax.ShapeDtypeStruct((n, D_MODEL), F32),
                   jax.ShapeDtypeStruct((N_DEV, D_MODEL, W_IN_COLS), MXU_DTYPE),
                   jax.ShapeDtypeStruct((1, D_MODEL), F32)],
        scratch_shapes=[pltpu.VMEM((N_DEV, D_MODEL, W_IN_COLS), F32),
                        pltpu.VMEM((D_MODEL, 2 * MIX), MXU_DTYPE)],
        compiler_params=_params(dimension_semantics=("arbitrary",)),
    )(dz, h, x2, dx_in, g_rows, w_all, dep)


def _row_pos(t0, rows):
    return t0 + lax.broadcasted_iota(jnp.int32, (rows, LANES), 0)


def _pool_window_mean(upad, g, t0, t_blk):
    k = 2 << g
    w = upad
    sh = 1
    while sh < k:
        w = w + pltpu.roll(w, sh, 0)
        sh *= 2
    count = jnp.minimum(_row_pos(t0, t_blk) + 1, k).astype(F32)
    return w[HALO:] / count - upad[HALO:]


def _pool_window_bwd(qpad, g, t_blk):
    k = 2 << g
    n = t_blk + HALO
    w = qpad
    sh = 1
    while sh < k:
        w = w + pltpu.roll(w, n - sh, 0)
        sh *= 2
    return w[:t_blk]


class _StateBuf:
    def __init__(self, refs, t_blk):
        self.refs = refs
        self.t_blk = t_blk

    def put_chunk(self, b, j, val):
        for c in range(4):
            self.refs[4 * b + c][pl.ds(j, self.t_blk, stride=STATE_ROWS), :] = val[:, c * LANES:(c + 1) * LANES]

    def get_chunk(self, b, j):
        return jnp.concatenate(
            [self.refs[4 * b + c][pl.ds(j, self.t_blk, stride=STATE_ROWS), :] for c in range(4)], axis=-1)

    def load(self, b, r, part):
        return jnp.concatenate(
            [self.refs[4 * b + 2 * part + h][pl.ds(r, STATE_ROWS), :] for h in range(2)], axis=-1)

    def store(self, b, r, part, val):
        for h in range(2):
            self.refs[4 * b + 2 * part + h][pl.ds(r, STATE_ROWS), :] = val[:, h * LANES:(h + 1) * LANES]


def _state_scratch(nb, t_blk):
    return [pltpu.VMEM((t_blk * STATE_ROWS, LANES), F32) for _ in range(4 * nb)]


def _ssm_project_in(u_ssm, wb_ref, buf, nb):
    t_blk = u_ssm.shape[0] // nb
    ub = _mx(u_ssm)
    for j in range(STATE_ROWS):
        m = j // 2
        bu = _mm(ub[:, m * LANES:(m + 1) * LANES], wb_ref[j])
        for b in range(nb):
            buf.put_chunk(b, j, bu[b * t_blk:(b + 1) * t_blk])


def _scan_forward(buf, lbr, lbi, init, nb):
    def step(t, carry):
        r = pl.multiple_of(t * STATE_ROWS, STATE_ROWS)
        out = []
        for b in range(nb):
            sr, si = carry[2 * b], carry[2 * b + 1]
            nr = lbr * sr - lbi * si + buf.load(b, r, 0)
            ni = lbr * si + lbi * sr + buf.load(b, r, 1)
            buf.store(b, r, 0, nr)
            buf.store(b, r, 1, ni)
            out += [nr, ni]
        return tuple(out)

    def body(i, carry):
        for u in range(SCAN_UNROLL):
            carry = step(i * SCAN_UNROLL + u, carry)
        return carry

    return lax.fori_loop(0, buf.t_blk // SCAN_UNROLL, body, init)


def _ssm_project_out(chunk, wc_ref):
    tiles = []
    for m in range(4):
        acc = None
        for j in (2 * m, 2 * m + 1):
            part = _mm_nt(chunk(j), wc_ref[j])
            acc = part if acc is None else acc + part
        tiles.append(acc)
    return jnp.concatenate(tiles, axis=-1)


def _layer_fwd(x3, z3, g_rows, w_in, pool_w, pool_scale, lbr, lbi, wb, wc, d_skip, glu_w, glu_b, w_out, dep, layer):
    nb, seq, _ = x3.shape
    t_blk = min(T_BLK, seq)
    n_t = seq // t_blk
    halo_per_blk = t_blk // HALO
    rows = nb * t_blk
    fused = z3 is None

    def body(*refs):
        if fused:
            (x_ref, g_ref, wi_ref, pw_ref, ps_ref, lbr_ref, lbi_ref, wb_ref, wc_ref, dsk_ref, gw_ref, gb_ref, wo_ref,
             dep_ref, z_ref, h_ref, yg_ref, sc_ref, act_ref, dact_ref, pooled_ref, ypre_ref, xo_ref,
             carry_ref, halo_ref, *s_refs) = refs
        else:
            (x_ref, z_ref, zh_ref, pw_ref, ps_ref, lbr_ref, lbi_ref, wb_ref, wc_ref, dsk_ref, gw_ref, gb_ref, wo_ref,
             dep_ref, yg_ref, sc_ref, act_ref, dact_ref, pooled_ref, ypre_ref, xo_ref, carry_ref, *s_refs) = refs
        i = pl.program_id(0)
        t0 = i * t_blk
        buf = _StateBuf(s_refs, t_blk)
        both = lambda lo, hi: z_ref[:, :, lo:hi].reshape(rows, hi - lo)

        @pl.when(i == 0)
        def _():
            carry_ref[...] = jnp.zeros_like(carry_ref)
            if fused:
                halo_ref[...] = jnp.zeros_like(halo_ref)

        x = x_ref[...].reshape(rows, D_MODEL)
        if fused:
            r = lax.rsqrt(jnp.mean(x * x, axis=-1, keepdims=True) + NORM_EPS)
            h = _mx(x * r * g_ref[...])
            h_ref[...] = h.reshape(nb, t_blk, D_MODEL)
            for d in range(N_DEV):
                z_ref[:, :, d * W_IN_COLS:(d + 1) * W_IN_COLS] = _mm(h, wi_ref[d]).reshape(nb, t_blk, W_IN_COLS)

        u_ssm = both(POOL_W, MIX)
        _ssm_project_in(u_ssm, wb_ref, buf, nb)
        init = tuple(carry_ref[b, :, h * STATE_COLS:(h + 1) * STATE_COLS] for b in range(nb) for h in range(2))
        fin = _scan_forward(buf, lbr_ref[...], lbi_ref[...], init, nb)
        for b in range(nb):
            carry_ref[b, :, 0:STATE_COLS] = fin[2 * b]
            carry_ref[b, :, STATE_COLS:2 * STATE_COLS] = fin[2 * b + 1]

        def chunk(j):
            states = _mx(jnp.concatenate([buf.get_chunk(b, j) for b in range(nb)], axis=0))
            sc_ref[:, j] = states.reshape(nb, t_blk, 2 * STATE_COLS)
            return states

        y = _ssm_project_out(chunk, wc_ref) + dsk_ref[...] * u_ssm
        yg, dgelu = _gelu_and_grad(y)
        ygb = _mx(yg)
        act_ref[...] = ygb.reshape(nb, t_blk, SSM_W)
        dact_ref[...] = _mx(dgelu).reshape(nb, t_blk, SSM_W)
        o_ssm = yg * _sigmoid(_mm(ygb, gw_ref[...]) + gb_ref[...])
        gp = both(MIX + POOL_W, 2 * MIX)
        parts = []
        first = (i == 0)
        for g in range(N_POOL_G):
            cols = slice(g * POOL_GC, (g + 1) * POOL_GC)
            pooled = []
            for b in range(nb):
                halo = halo_ref[b, :, cols] if fused else jnp.where(first, 0.0, zh_ref[b, :, cols])
                pooled.append(_pool_window_mean(jnp.concatenate([halo, z_ref[b, :, cols]], axis=0), g, t0, t_blk))
            pb = _mx(jnp.concatenate(pooled, axis=0))
            ypre = _mm(pb, pw_ref[g])
            pooled_ref[:, :, cols] = pb.reshape(nb, t_blk, POOL_GC)
            ypre_ref[:, :, cols] = ypre.reshape(nb, t_blk, POOL_GC)
            gpp = both(MIX + g * POOL_GC, MIX + (g + 1) * POOL_GC)
            parts.append(_mx(ypre * ps_ref[:, cols] * (gpp * _sigmoid(gpp))))
        parts.append(_mx(o_ssm * (gp * _sigmoid(gp))))
        gated = jnp.concatenate(parts, axis=-1)
        yg_ref[...] = gated.reshape(nb, t_blk, MIX)
        xo_ref[...] = (x + _mm(gated, wo_ref[...])).reshape(nb, t_blk, D_MODEL)
        if fused:
            halo_ref[...] = z_ref[:, t_blk - HALO:, 0:POOL_W]

    const = lambda *shape: pl.BlockSpec(shape, lambda i: (0,) * len(shape))
    tokens = lambda width: pl.BlockSpec((nb, t_blk, width), lambda i: (0, i, 0))
    mixer_specs = [_of_layer(layer, N_POOL_G, POOL_GC, POOL_GC), _of_layer(layer, 1, POOL_W),
                   _of_layer(layer, STATE_ROWS, STATE_COLS), _of_layer(layer, STATE_ROWS, STATE_COLS),
                   _of_layer(layer, STATE_ROWS, LANES, 2 * STATE_COLS),
                   _of_layer(layer, STATE_ROWS, LANES, 2 * STATE_COLS),
                   _of_layer(layer, 1, SSM_W), const(SSM_W, SSM_W), _of_layer(layer, 1, SSM_W),
                   const(MIX, D_MODEL), ANY_SPEC]
    mixer_args = (pool_w, pool_scale, lbr, lbi, wb, wc, d_skip, glu_w, glu_b, w_out, dep)
    out_specs = [tokens(MIX), pl.BlockSpec((nb, STATE_ROWS, t_blk, 2 * STATE_COLS), lambda i: (0, 0, i, 0)),
                 tokens(SSM_W), tokens(SSM_W), tokens(POOL_W), tokens(POOL_W), tokens(D_MODEL)]
    out_shape = [jax.ShapeDtypeStruct((nb, seq, MIX), MXU_DTYPE),
                 jax.ShapeDtypeStruct((nb, STATE_ROWS, seq, 2 * STATE_COLS), MXU_DTYPE),
                 jax.ShapeDtypeStruct((nb, seq, SSM_W), MXU_DTYPE),
                 jax.ShapeDtypeStruct((nb, seq, SSM_W), MXU_DTYPE),
                 jax.ShapeDtypeStruct((nb, seq, POOL_W), MXU_DTYPE),
                 jax.ShapeDtypeStruct((nb, seq, POOL_W), F32),
                 jax.ShapeDtypeStruct((nb, seq, D_MODEL), F32)]
    scratch = [pltpu.VMEM((nb, STATE_ROWS, 2 * STATE_COLS), F32)]
    if fused:
        in_specs = [tokens(D_MODEL), _of_layer(layer, 1, D_MODEL), const(N_DEV, D_MODEL, W_IN_COLS)] + mixer_specs
        args = (x3, g_rows, w_in) + mixer_args
        out_specs = [tokens(2 * MIX), tokens(D_MODEL)] + out_specs
        out_shape = [jax.ShapeDtypeStruct((nb, seq, 2 * MIX), F32),
                     jax.ShapeDtypeStruct((nb, seq, D_MODEL), MXU_DTYPE)] + out_shape
        scratch = scratch + [pltpu.VMEM((nb, HALO, POOL_W), F32)]
    else:
        in_specs = [tokens(D_MODEL), tokens(2 * MIX),
                    pl.BlockSpec((nb, HALO, POOL_W), lambda i: (0, jnp.maximum(i * halo_per_blk - 1, 0), 0))] + mixer_specs
        args = (x3, z3, z3) + mixer_args
    return pl.pallas_call(
        body, name="layer_fwd" if fused else "mixer_fwd",
        grid=(n_t,),
        in_specs=in_specs, out_specs=out_specs, out_shape=out_shape,
        scratch_shapes=scratch + _state_scratch(nb, t_blk),
        compiler_params=_params(dimension_semantics=("arbitrary",)),
    )(*args)


def _mixer_bwd(z3, dy3, states, kept, pool_w, pool_scale, lbr, lbi, wb, wc, d_skip, glu_w, glu_b, layer):
    nb, seq, _ = z3.shape
    t_blk = min(T_BLK, seq)
    n_t = seq // t_blk
    halo_per_blk = t_blk // HALO
    rows = nb * t_blk

    def body(z_ref, dy_ref, sc_ref, sch_ref, act_ref, dact_ref, pooled_ref, ypre_ref, pw_ref, ps_ref, lbr_ref, lbi_ref, wb_ref, wc_ref, dsk_ref,
             gw_ref, gb_ref,
             dz_ref, dpw_ref, dps_ref, dlbr_ref, dlbi_ref, dwb_ref, dwc_ref, ddsk_ref, dgw_ref, dgb_ref,
             gcarry_ref, qcarry_ref, du_ref, dgw_acc, *g_refs):
        i = pl.program_id(0)
        blk = n_t - 1 - i
        t0 = blk * t_blk
        gbuf = _StateBuf(g_refs, t_blk)

        @pl.when(i == 0)
        def _():
            gcarry_ref[...] = jnp.zeros_like(gcarry_ref)
            qcarry_ref[...] = jnp.zeros_like(qcarry_ref)
            for ref in (dpw_ref, dps_ref, dlbr_ref, dlbi_ref, dwb_ref, dwc_ref, ddsk_ref, dgw_acc, dgb_ref):
                ref[...] = jnp.zeros_like(ref)

        lbr_v = lbr_ref[...]
        lbi_v = lbi_ref[...]

        both = lambda ref, lo, hi: ref[:, :, lo:hi].reshape(rows, hi - lo)
        split = lambda val: val.reshape(nb, t_blk, val.shape[-1])
        states = lambda j: sc_ref[:, j].reshape(rows, 2 * STATE_COLS)
        first = (blk == 0)

        u_ssm = both(z_ref, POOL_W, MIX)
        ygb = act_ref[...].reshape(rows, SSM_W)
        yg = ygb.astype(F32)
        dgelu = dact_ref[...].reshape(rows, SSM_W).astype(F32)
        sg = _sigmoid(_mm(ygb, gw_ref[...]) + gb_ref[...])
        o_ssm = yg * sg
        gp = both(z_ref, MIX + POOL_W, 2 * MIX)
        sgm = _sigmoid(gp)
        dyv = both(dy_ref, POOL_W, MIX)
        dz_ref[:, :, MIX + POOL_W:2 * MIX] = split(_mx(dyv * o_ssm * (sgm * (1.0 + gp * (1.0 - sgm)))))
        do = dyv * (gp * sgm)
        dv = do * yg * (sg * (1.0 - sg))
        dvb = _mx(dv)
        dgb_ref[...] += jnp.sum(dv, axis=0, keepdims=True)
        dgw_acc[...] += _mm_tn(ygb, dvb)
        dyp = (do * sg + _mm_nt(dvb, gw_ref[...])) * dgelu
        ddsk_ref[...] += jnp.sum(dyp * u_ssm, axis=0, keepdims=True)
        dypb = _mx(dyp)
        for j in range(STATE_ROWS):
            m = j // 2
            dyt = dypb[:, m * LANES:(m + 1) * LANES]
            ds = _mm(dyt, wc_ref[j])
            for b in range(nb):
                gbuf.put_chunk(b, j, ds[b * t_blk:(b + 1) * t_blk])
            dwc_ref[j] += _mm_tn(dyt, states(j))
        du_ref[...] = split(dsk_ref[...] * dyp)

        for g in range(N_POOL_G):
            cols = slice(g * POOL_GC, (g + 1) * POOL_GC)
            pb = both(pooled_ref, g * POOL_GC, (g + 1) * POOL_GC)
            ypre = both(ypre_ref, g * POOL_GC, (g + 1) * POOL_GC)
            gpp = both(z_ref, MIX + g * POOL_GC, MIX + (g + 1) * POOL_GC)
            sgp = _sigmoid(gpp)
            dyg = both(dy_ref, g * POOL_GC, (g + 1) * POOL_GC)
            scale = ps_ref[:, cols]
            dz_ref[:, :, MIX + g * POOL_GC:MIX + (g + 1) * POOL_GC] = split(_mx(
                dyg * (ypre * scale) * (sgp * (1.0 + gpp * (1.0 - sgp)))))
            dyc = dyg * (gpp * sgp)
            dps_ref[:, cols] += jnp.sum(dyc * ypre, axis=0, keepdims=True)
            dypre = _mx(dyc * scale)
            dpw_ref[g] += _mm_tn(pb, dypre)
            dpooled = _mm_nt(dypre, pw_ref[g])
            count = jnp.minimum(_row_pos(t0, t_blk) + 1, 2 << g).astype(F32)
            for b in range(nb):
                dp = dpooled[b * t_blk:(b + 1) * t_blk]
                q = dp / count
                qpad = jnp.concatenate([q, qcarry_ref[b, :, cols]], axis=0)
                qcarry_ref[b, :, cols] = q[:HALO]
                dz_ref[b, :, cols] = _mx(_pool_window_bwd(qpad, g, t_blk) - dp)

        def rev_step(t, carry):
            r = pl.multiple_of(t * STATE_ROWS, STATE_ROWS)
            out = []
            for b in range(nb):
                gr, gi = carry[2 * b], carry[2 * b + 1]
                ngr = lbr_v * gr + lbi_v * gi + gbuf.load(b, r, 0)
                ngi = lbr_v * gi - lbi_v * gr + gbuf.load(b, r, 1)
                gbuf.store(b, r, 0, ngr)
                gbuf.store(b, r, 1, ngi)
                out += [ngr, ngi]
            return tuple(out)

        def rev_body(i, carry):
            for u in range(SCAN_UNROLL):
                carry = rev_step(t_blk - 1 - (i * SCAN_UNROLL + u), carry)
            return carry

        init_g = tuple(gcarry_ref[b, :, h * STATE_COLS:(h + 1) * STATE_COLS] for b in range(nb) for h in range(2))
        fin = lax.fori_loop(0, t_blk // SCAN_UNROLL, rev_body, init_g)
        for b in range(nb):
            gcarry_ref[b, :, 0:STATE_COLS] = fin[2 * b]
            gcarry_ref[b, :, STATE_COLS:2 * STATE_COLS] = fin[2 * b + 1]

        ub = _mx(u_ssm)
        for m in range(4):
            acc = both(du_ref, m * LANES, (m + 1) * LANES)
            for j in (2 * m, 2 * m + 1):
                g = jnp.concatenate([gbuf.get_chunk(b, j) for b in range(nb)], axis=0)
                gj = _mx(g)
                acc = acc + _mm_nt(gj, wb_ref[j])
                dwb_ref[j] += _mm_tn(ub[:, m * LANES:(m + 1) * LANES], gj)
                shifted = []
                for b in range(nb):
                    before = jnp.where(first, 0.0, sch_ref[b, j].astype(F32))
                    spad = jnp.concatenate([before, sc_ref[b, j].astype(F32)], axis=0)
                    shifted.append(pltpu.roll(spad, 1, 0)[HALO:])
                s_prev = jnp.concatenate(shifted, axis=0)
                g_re, g_im = g[:, :STATE_COLS], g[:, STATE_COLS:]
                p_re, p_im = s_prev[:, :STATE_COLS], s_prev[:, STATE_COLS:]
                dlbr_ref[j:j + 1, :] += jnp.sum(g_re * p_re + g_im * p_im, axis=0, keepdims=True)
                dlbi_ref[j:j + 1, :] += jnp.sum(g_im * p_re - g_re * p_im, axis=0, keepdims=True)
            dz_ref[:, :, POOL_W + m * LANES:POOL_W + (m + 1) * LANES] = split(_mx(acc))

        @pl.when(i == n_t - 1)
        def _():
            dgw_ref[...] = _mx(dgw_acc[...])

    const = lambda *shape: pl.BlockSpec(shape, lambda i: (0,) * len(shape))
    rev = lambda i: n_t - 1 - i
    out_shape = [jax.ShapeDtypeStruct((nb, seq, 2 * MIX), MXU_DTYPE),
                 jax.ShapeDtypeStruct((N_POOL_G, POOL_GC, POOL_GC), F32),
                 jax.ShapeDtypeStruct((1, POOL_W), F32),
                 jax.ShapeDtypeStruct((STATE_ROWS, STATE_COLS), F32),
                 jax.ShapeDtypeStruct((STATE_ROWS, STATE_COLS), F32),
                 jax.ShapeDtypeStruct((STATE_ROWS, LANES, 2 * STATE_COLS), F32),
                 jax.ShapeDtypeStruct((STATE_ROWS, LANES, 2 * STATE_COLS), F32),
                 jax.ShapeDtypeStruct((1, SSM_W), F32),
                 jax.ShapeDtypeStruct((SSM_W, SSM_W), MXU_DTYPE),
                 jax.ShapeDtypeStruct((1, SSM_W), F32)]
    return pl.pallas_call(
        body, name="mixer_bwd",
        grid=(n_t,),
        in_specs=[pl.BlockSpec((nb, t_blk, 2 * MIX), lambda i: (0, rev(i), 0)),
                  pl.BlockSpec((nb, t_blk, MIX), lambda i: (0, rev(i), 0)),
                  pl.BlockSpec((nb, STATE_ROWS, t_blk, 2 * STATE_COLS), lambda i: (0, 0, rev(i), 0)),
                  pl.BlockSpec((nb, STATE_ROWS, HALO, 2 * STATE_COLS),
                               lambda i: (0, 0, jnp.maximum(rev(i) * halo_per_blk - 1, 0), 0)),
                  pl.BlockSpec((nb, t_blk, SSM_W), lambda i: (0, rev(i), 0)),
                  pl.BlockSpec((nb, t_blk, SSM_W), lambda i: (0, rev(i), 0)),
                  pl.BlockSpec((nb, t_blk, POOL_W), lambda i: (0, rev(i), 0)),
                  pl.BlockSpec((nb, t_blk, POOL_W), lambda i: (0, rev(i), 0)),
                  _of_layer(layer, N_POOL_G, POOL_GC, POOL_GC), _of_layer(layer, 1, POOL_W),
                  _of_layer(layer, STATE_ROWS, STATE_COLS), _of_layer(layer, STATE_ROWS, STATE_COLS),
                  _of_layer(layer, STATE_ROWS, LANES, 2 * STATE_COLS),
                  _of_layer(layer, STATE_ROWS, LANES, 2 * STATE_COLS),
                  _of_layer(layer, 1, SSM_W), const(SSM_W, SSM_W), _of_layer(layer, 1, SSM_W)],
        out_specs=[pl.BlockSpec((nb, t_blk, 2 * MIX), lambda i: (0, rev(i), 0))]
                  + [const(*s.shape) for s in out_shape[1:]],
        out_shape=out_shape,
        scratch_shapes=[pltpu.VMEM((nb, STATE_ROWS, 2 * STATE_COLS), F32),
                        pltpu.VMEM((nb, HALO, POOL_W), F32),
                        pltpu.VMEM((nb, t_blk, SSM_W), F32),
                        pltpu.VMEM((SSM_W, SSM_W), F32)]
                       + _state_scratch(nb, t_blk),
        compiler_params=_params(dimension_semantics=("arbitrary",)),
    )(z3, dy3, states, states, *kept, pool_w, pool_scale, lbr, lbi, wb, wc, d_skip, glu_w, glu_b)


def _mesh_place():
    x, y, c = lax.axis_index("x"), lax.axis_index("y"), lax.axis_index("c")
    return x, y, c


def _flip(place, k):
    x, y, c = place
    return (1 - x if k & 4 else x, 1 - y if k & 2 else y, 1 - c if k & 1 else c)


def _index(place):
    x, y, c = place
    return 4 * x + 2 * y + c


HBM_SPEC = pl.BlockSpec(memory_space=pltpu.HBM)
SEM_SPEC = pl.BlockSpec(memory_space=pltpu.SEMAPHORE)
_EFFECT = pltpu.SideEffectType.DATAFLOW_SIDE_EFFECTING
N_PEERS = N_DEV - 1


def _exchange_copies(src_refs, land_refs, send_sems, recv_sems):
    me = _mesh_place()
    mine = _index(me)
    out = []
    for a, land_ref in enumerate(land_refs):
        for k in range(1, N_DEV):
            peer = _flip(me, k)
            theirs = _index(peer)
            n = a * N_PEERS + k - 1
            src = src_refs[a].at[theirs] if src_refs else land_ref.at[mine]
            send = pltpu.make_async_remote_copy(
                src_ref=src, dst_ref=land_ref.at[mine], send_sem=send_sems.at[n], recv_sem=recv_sems.at[n],
                device_id=peer, device_id_type=MESH)
            recv = pltpu.make_async_remote_copy(
                src_ref=src, dst_ref=land_ref.at[theirs], send_sem=send_sems.at[n], recv_sem=recv_sems.at[n],
                device_id=peer, device_id_type=MESH)
            out.append((send, recv))
    return out


def _exchange_start(srcs, lands, after, name):
    arrays = tuple(srcs) + tuple(lands)
    n_src, n_all = len(srcs), len(arrays)
    n_copies = len(lands) * N_PEERS

    def body(*refs):
        send_sems, recv_sems = refs[n_all + 1], refs[n_all + 2]
        token = refs[-1]
        for send, _ in _exchange_copies(refs[:n_src], refs[n_src:n_all], send_sems, recv_sems):
            send.start()
        token[...] = jnp.zeros_like(token)

    res = pl.pallas_call(
        body, name=name,
        in_specs=[HBM_SPEC] * n_all + [ANY_SPEC],
        out_specs=[SEM_SPEC, SEM_SPEC] + [HBM_SPEC] * n_all + [VMEM_SPEC],
        out_shape=[pltpu.SemaphoreType.DMA((n_copies,)), pltpu.SemaphoreType.DMA((n_copies,))]
                  + [pltpu.HBM(a.shape, a.dtype) for a in arrays] + [jax.ShapeDtypeStruct((SUBLANES, LANES), F32)],
        input_output_aliases={i: 2 + i for i in range(n_all)},
        compiler_params=pltpu.CompilerParams(has_side_effects=_EFFECT),
    )(*[pltpu.with_memory_space_constraint(a, pltpu.HBM) for a in arrays], after)
    return tuple(res[:-1]), res[-1]


def _exchange_wait(handle, n_lands, after, name):
    send_sems, recv_sems = handle[0], handle[1]
    arrays = handle[2:]
    n_all = len(arrays)
    n_src = n_all - n_lands

    def body(*refs):
        for send, recv in _exchange_copies(refs[:n_src], refs[n_src:n_all], refs[n_all], refs[n_all + 1]):
            send.wait_send()
            recv.wait_recv()

    res = pl.pallas_call(
        body, name=name,
        in_specs=[HBM_SPEC] * n_all + [SEM_SPEC, SEM_SPEC, ANY_SPEC],
        out_specs=[HBM_SPEC] * n_all,
        out_shape=[pltpu.HBM(a.shape, a.dtype) for a in arrays],
        input_output_aliases={i: i for i in range(n_all)},
        compiler_params=pltpu.CompilerParams(has_side_effects=_EFFECT),
    )(*arrays, send_sems, recv_sems, after)
    return tuple(res[:n_src]), tuple(res[n_src:])


def _weight_zones(w_in, glu_w, w_out, my_idx):
    shards = (w_in, glu_w, w_out)
    depth = w_in.shape[0]

    def body(idx_ref, *refs):
        ins, zones = refs[:len(shards)], refs[len(shards):]
        for l in range(depth):
            for a, src in enumerate(ins):
                zones[l * len(shards) + a][0] = _mx(src[l])

    whole = lambda s: pl.BlockSpec(s.shape, lambda i, idx: (0,) * s.ndim)
    return pl.pallas_call(
        body, name="weight_zones",
        grid_spec=pltpu.PrefetchScalarGridSpec(
            num_scalar_prefetch=1, grid=(1,),
            in_specs=[whole(s) for s in shards],
            out_specs=[pl.BlockSpec((1,) + s.shape[1:], lambda i, idx: (idx[0], 0, 0))
                       for _ in range(depth) for s in shards]),
        out_shape=[jax.ShapeDtypeStruct((N_DEV,) + s.shape[1:], MXU_DTYPE) for _ in range(depth) for s in shards],
        compiler_params=_params(dimension_semantics=("arbitrary",)),
    )(my_idx.reshape(1).astype(jnp.int32), *shards)


def _allreduce_parts(parts):
    n_parts = len(parts)
    pieces = [(a, i, k) for i, a in enumerate(parts) for k in range(a.shape[1] // LANES)]
    offsets = [0]
    for a, _, _ in pieces:
        offsets.append(offsets[-1] + a.shape[0])
    rows = offsets[-1]
    assert rows % _PACK_ROWS == 0, rows
    half = rows // 2
    quarter = half // 4

    def body(*refs):
        in_refs, out_refs = refs[:n_parts], refs[n_parts:2 * n_parts]
        p_ref, o_ref, part_ref, sib_ref, got_ref, send_sems, recv_sems = refs[2 * n_parts:]
        for (a, i, k), off in zip(pieces, offsets):
            p_ref[off:off + a.shape[0], :] = in_refs[i][:, k * LANES:(k + 1) * LANES]

        x, y, c = _mesh_place()
        sibling = (x, y, 1 - c)
        chip = 2 * x + y
        chips = [(k, (1 - x if k & 2 else x, 1 - y if k & 1 else y, c), chip ^ k) for k in (1, 2, 3)]
        my_half = pl.multiple_of(c * half, SUBLANES)
        other_half = pl.multiple_of((1 - c) * half, SUBLANES)

        def copy(n, src, dst, to):
            return pltpu.make_async_remote_copy(src_ref=src, dst_ref=dst, send_sem=send_sems.at[n],
                                                recv_sem=recv_sems.at[n], device_id=to, device_id_type=MESH)

        def quarter_of(ref, base, q):
            return ref.at[pl.ds(pl.multiple_of(base + q * quarter, SUBLANES), quarter)]

        swap = copy(0, p_ref.at[pl.ds(other_half, half)], sib_ref, sibling)
        swap.start()
        swap.wait()
        part_ref[...] = p_ref[pl.ds(my_half, half), :] + sib_ref[...]

        scatter = [copy(k, quarter_of(part_ref, 0, q), got_ref.at[k - 1], to) for k, to, q in chips]
        for cp in scatter:
            cp.start()
        total = part_ref[pl.ds(pl.multiple_of(chip * quarter, SUBLANES), quarter), :]
        for cp, (k, _, _) in zip(scatter, chips):
            cp.wait()
            total = total + got_ref[k - 1]
        mine = pl.multiple_of(my_half + chip * quarter, SUBLANES)
        o_ref[pl.ds(mine, quarter), :] = total

        gather = [copy(3 + k, o_ref.at[pl.ds(mine, quarter)], o_ref.at[pl.ds(mine, quarter)], to) for k, to, _ in chips]
        for cp in gather:
            cp.start()
        for k, to, q in chips:
            theirs = quarter_of(o_ref, my_half, q)
            copy(3 + k, theirs, theirs, to).wait_recv()
        for cp in gather:
            cp.wait_send()

        back = copy(7, o_ref.at[pl.ds(my_half, half)], o_ref.at[pl.ds(my_half, half)], sibling)
        back.start()
        copy(7, o_ref.at[pl.ds(other_half, half)], o_ref.at[pl.ds(other_half, half)], sibling).wait_recv()
        back.wait_send()

        for (a, i, k), off in zip(pieces, offsets):
            out_refs[i][:, k * LANES:(k + 1) * LANES] = o_ref[off:off + a.shape[0], :]

    return pl.pallas_call(
        body, name="comm_allreduce_parts",
        in_specs=[VMEM_SPEC] * n_parts,
        out_specs=[VMEM_SPEC] * n_parts,
        out_shape=[jax.ShapeDtypeStruct(a.shape, F32) for a in parts],
        scratch_shapes=[pltpu.VMEM((rows, LANES), F32),
                        pltpu.VMEM((rows, LANES), F32),
                        pltpu.VMEM((half, LANES), F32),
                        pltpu.VMEM((half, LANES), F32),
                        pltpu.VMEM((3, quarter, LANES), F32),
                        pltpu.SemaphoreType.DMA((8,)),
                        pltpu.SemaphoreType.DMA((8,))],
        compiler_params=_params(),
    )(*parts)


def _adamw_math(w, g, m, v):
    m = ADAM_B1 * m + (1.0 - ADAM_B1) * g
    v = ADAM_B2 * v + (1.0 - ADAM_B2) * (g * g)
    m_hat = m / (1.0 - ADAM_B1 ** ADAM_STEP)
    v_hat = v / (1.0 - ADAM_B2 ** ADAM_STEP)
    delta = -ADAM_LR * (m_hat / (jnp.sqrt(v_hat) + ADAM_EPS) + ADAM_WD * w)
    return delta, m, v


def _adamw_summed(received, own, my_idx, w, m, v, name):
    depth, r, c = w.shape
    tr = min(r, 128)

    def body(idx_ref, *refs):
        r_refs, o_refs = refs[:depth], refs[depth:2 * depth]
        w_ref, m_ref, v_ref, g_ref, d_ref, nm_ref, nv_ref = refs[2 * depth:]
        me = idx_ref[0]
        for l in range(depth):
            g = jnp.zeros((tr, c), F32)
            for q in range(N_DEV):
                g = g + jnp.where(q == me, o_refs[l][0], r_refs[l][q]).astype(F32)
            g_ref[l] = g
            d_ref[l], nm_ref[l], nv_ref[l] = _adamw_math(w_ref[l], g, m_ref[l], v_ref[l])

    blk = pl.BlockSpec((depth, tr, c), lambda i, idx: (0, i, 0))
    return pl.pallas_call(
        body, name=name,
        grid_spec=pltpu.PrefetchScalarGridSpec(
            num_scalar_prefetch=1, grid=(r // tr,),
            in_specs=[pl.BlockSpec((N_DEV, tr, c), lambda i, idx: (0, i, 0))] * depth
                     + [pl.BlockSpec((1, tr, c), lambda i, idx: (idx[0], i, 0))] * depth
                     + [blk, blk, blk],
            out_specs=[blk] * 4),
        out_shape=[jax.ShapeDtypeStruct((depth, r, c), F32)] * 4,
        compiler_params=_params(dimension_semantics=("arbitrary",)),
    )(my_idx.reshape(1).astype(jnp.int32), *received, *own, w, m, v)


def _adamw_small(ws, gs, ms, vs):
    n = len(ws)
    depth = ws[0].shape[0]

    def spec(a):
        per_layer = a.shape[0] == depth
        rest = (0,) * (a.ndim - 1)
        return pl.BlockSpec((1,) + a.shape[1:], lambda l: ((l if per_layer else 0),) + rest)

    def body(*refs):
        w_refs, g_refs, m_refs, v_refs = (refs[k * n:(k + 1) * n] for k in range(4))
        d_refs, nm_refs, nv_refs = (refs[(4 + k) * n:(5 + k) * n] for k in range(3))
        for k in range(n):
            d_refs[k][...], nm_refs[k][...], nv_refs[k][...] = _adamw_math(
                w_refs[k][...], g_refs[k][...], m_refs[k][...], v_refs[k][...])

    specs = [spec(a) for a in ws]
    shapes = [jax.ShapeDtypeStruct(a.shape, F32) for a in ws]
    res = pl.pallas_call(
        body, name="adamw_small",
        grid=(depth,),
        in_specs=specs * 4,
        out_specs=specs * 3,
        out_shape=shapes * 3,
        compiler_params=_params(dimension_semantics=("arbitrary",)),
    )(*ws, *gs, *ms, *vs)
    return res[:n], res[n:2 * n], res[2 * n:]


_PACK_ROWS = SUBLANES * N_DEV


def _pack(arrays):
    flat = jnp.concatenate([a.reshape(-1) for a in arrays])
    per = _PACK_ROWS * LANES
    total = -(-flat.shape[0] // per) * per
    flat = jnp.pad(flat, (0, total - flat.shape[0]))
    return flat.reshape(total // LANES, LANES)


def _unpack(packed, like):
    flat = packed.reshape(-1)
    out = []
    off = 0
    for a in like:
        out.append(flat[off:off + a.size].reshape(a.shape))
        off += a.size
    return out


def kernel(x, norm_g, w_in, pool_w, pool_scale, a_re, a_im, log_dt, b_re, b_im, c_re, c_im, d_skip, glu_w, glu_b, w_out, final_g, loss_target, m_norm_g, m_w_in, m_pool_w, m_pool_scale, m_a_re, m_a_im, m_log_dt, m_b_re, m_b_im, m_c_re, m_c_im, m_d_skip, m_glu_w, m_glu_b, m_w_out, m_final_g, v_norm_g, v_w_in, v_pool_w, v_pool_scale, v_a_re, v_a_im, v_log_dt, v_b_re, v_b_im, v_c_re, v_c_im, v_d_skip, v_glu_w, v_glu_b, v_w_out, v_final_g):
    nb, seq, _ = x.shape
    n_tok = nb * seq
    depth = norm_g.shape[0]

    my_idx = _index(_mesh_place())

    zones = _weight_zones(w_in, glu_w, w_out, my_idx)

    def gather_start(l, after):
        return _exchange_start((), zones[3 * l:3 * l + 3], after, f"comm_gather_start_{l}")

    def gather_wait(handle, after, l):
        _, (win, glu, wout) = _exchange_wait(handle, 3, after, f"comm_gather_wait_{l}")
        return win, glu.reshape(SSM_W, SSM_W), wout.reshape(MIX, D_MODEL)

    xs = [x.reshape(n_tok, D_MODEL)]
    first_w_in, dep = _exchange_start((), zones[0:1], xs[0], "comm_gather_start_0_w_in")

    (lbr, lbi, rb, rc), dense_vjp = jax.vjp(jax.vmap(_ssm_dense), a_re, a_im, log_dt + dep[0, 0], b_re, b_im, c_re, c_im)
    chunk_all = jax.vmap(_ssm_chunked)
    (wb, wct), chunk_vjp = jax.vjp(lambda p, q: (chunk_all(p), chunk_all(q)), rb, rc)
    wb_m, wct_m = _mx(wb), _mx(wct)
    pool_w_m = _mx(pool_w)
    rows_of = lambda a: a[:, None, :]
    norm_rows, scale_rows, skip_rows, bias_rows = rows_of(norm_g), rows_of(pool_scale), rows_of(d_skip), rows_of(glu_b)

    def layer_params(l):
        return (pool_w_m, scale_rows, lbr, lbi, wb_m, wct_m, skip_rows, weights[l][1], bias_rows)

    saved = []
    weights = []
    for l in range(depth):
        if l == 0:
            _, (win,) = _exchange_wait(first_w_in, 1, wct_m, "comm_gather_wait_0_w_in")
            rest, dep = _exchange_start((), zones[1:3], win, "comm_gather_start_0_rest")
            z, h = _inproj_fwd(xs[-1], norm_rows, win, dep, l)
            _, (glu, wout) = _exchange_wait(rest, 2, z, "comm_gather_wait_0_rest")
            weights.append((win, glu.reshape(SSM_W, SSM_W), wout.reshape(MIX, D_MODEL)))
            handle, dep = gather_start(1, weights[0][2])
            z3 = z.reshape(nb, seq, 2 * MIX)
            yg, states, *kept, x_next = _layer_fwd(xs[-1].reshape(nb, seq, D_MODEL), z3, None, None,
                                                   *layer_params(l), weights[l][2], dep, l)
        else:
            weights.append(gather_wait(handle, xs[-1], l))
            if l + 1 < depth:
                handle, dep = gather_start(l + 1, weights[l][0])
            z3, h3, yg, states, *kept, x_next = _layer_fwd(xs[-1].reshape(nb, seq, D_MODEL), None, norm_rows,
                                                           weights[l][0], *layer_params(l), weights[l][2], dep, l)
            h = h3.reshape(n_tok, D_MODEL)
        xs.append(x_next.reshape(n_tok, D_MODEL))
        saved.append((z3, h, yg.reshape(n_tok, MIX), states, kept))

    dx, loss_part, d_final_g = _loss_head(xs[-1], loss_target.reshape(n_tok, D_MODEL), final_g[None])

    small = {k: [None] * depth for k in
             ("norm_g", "pool_w", "pool_scale", "lbr", "lbi", "wb", "wct", "d_skip", "glu_b")}
    received = [None] * depth
    sent = [None] * depth
    pending = None
    early = None
    for l in reversed(range(depth)):
        z3, h, yg2, states, kept = saved[l]
        dy, d_wout = _outproj_bwd(dx, yg2, weights[l][2], dep)
        (dz, d_pw, d_ps, d_lbr, d_lbi, d_wb, d_wct, d_dsk, d_gw, d_gb) = _mixer_bwd(
            z3, dy.reshape(nb, seq, MIX), states, kept, *layer_params(l), l)
        rest = (d_gw.reshape(N_DEV, SSM_W // N_DEV, SSM_W), d_wout.reshape(N_DEV, MIX // N_DEV, D_MODEL))
        for k, val in (("pool_w", d_pw), ("pool_scale", d_ps[0]), ("lbr", d_lbr), ("lbi", d_lbi), ("wb", d_wb),
                       ("wct", d_wct), ("d_skip", d_dsk[0]), ("glu_b", d_gb[0])):
            small[k][l] = val
        if l == 0:
            stack = lambda k: jnp.stack(small[k])
            d_rb, d_rc = chunk_vjp((stack("wb"), stack("wct")))
            few = [jnp.stack(small["norm_g"][1:]), stack("pool_scale"), stack("lbr"), stack("lbi"), stack("d_skip"),
                   stack("glu_b"), d_final_g[0], loss_part[0]]
            big = [stack("pool_w"), d_rb, d_rc]
            summed = _allreduce_parts([a.reshape(-1, a.shape[-1]) for a in big] + [_pack(few)])
            early, dep = _exchange_start(rest, tuple(lax.empty(s.shape, s.dtype) for s in rest), summed[-1],
                                         "comm_grads_start_0_rest")
        dx, d_win, d_ng = _inproj_bwd(dz.reshape(n_tok, 2 * MIX), h, xs[l], dx, norm_rows, weights[l][0], dep, l)
        small["norm_g"][l] = d_ng[0]
        if pending is not None:
            sent[l + 1], received[l + 1] = _exchange_wait(pending, 3, dx, f"comm_grads_wait_{l + 1}")
        ng_all = jnp.broadcast_to(jnp.tile(d_ng[0].reshape(-1, LANES), (SUBLANES * LANES // D_MODEL, 1)),
                                  (N_DEV, SUBLANES, LANES))
        srcs = (d_win, ng_all) if l == 0 else (d_win,) + rest
        lands = tuple(lax.empty(s.shape, s.dtype) for s in srcs)
        pending, dep = _exchange_start(srcs, lands, dx, f"comm_grads_start_{l}")
    shard_res = {}
    shard_inputs = {"w_in": (w_in, m_w_in, v_w_in), "glu_w": (glu_w, m_glu_w, v_glu_w), "w_out": (w_out, m_w_out, v_w_out)}

    def shard_adamw(n, pos):
        w, m, v = shard_inputs[n]
        shard_res[n] = _adamw_summed([received[l][pos] for l in range(depth)], [sent[l][pos] for l in range(depth)],
                                     my_idx, w, m, v, "adamw_" + n)

    (s_glu, s_wout), (r_glu, r_wout) = _exchange_wait(early, 2, dx, "comm_grads_wait_0_rest")
    sent[0], received[0] = (None, s_glu, s_wout), (None, r_glu, r_wout)
    shard_adamw("glu_w", 1)
    shard_adamw("w_out", 2)

    g_pool_w, g_rb, g_rc = (g.reshape(a.shape) for g, a in zip(summed, big))
    g_norm_rest, g_pool_scale, g_lbr, g_lbi, g_d_skip, g_glu_b, g_final_g, loss = _unpack(summed[-1], few)
    loss = loss[0]
    g_a_re, g_a_im, g_log_dt, g_b_re, g_b_im, g_c_re, g_c_im = dense_vjp((g_lbr, g_lbi, g_rb, g_rc))
    (s_win, s_ng), (r_win, r_ng) = _exchange_wait(pending, 2, g_b_re + shard_res["w_out"][1].reshape(-1)[0],
                                                  "comm_grads_wait_0")
    each = jnp.where(jnp.arange(N_DEV)[:, None, None] == my_idx, s_ng, r_ng)
    g_norm_0 = each[0]
    for p in range(1, N_DEV):
        g_norm_0 = g_norm_0 + each[p]
    g_norm_g = jnp.concatenate([g_norm_0[:D_MODEL // LANES].reshape(1, D_MODEL), g_norm_rest])

    names = ["norm_g", "pool_w", "pool_scale", "a_re", "a_im", "log_dt", "b_re", "b_im", "c_re", "c_im",
             "d_skip", "glu_b", "final_g"]
    rows = {"norm_g", "pool_scale", "log_dt", "d_skip", "glu_b"}
    small_w = [norm_g, pool_w, pool_scale, a_re, a_im, log_dt, b_re, b_im, c_re, c_im, d_skip, glu_b, final_g]
    small_g = [g_norm_g, g_pool_w, g_pool_scale, g_a_re, g_a_im, g_log_dt, g_b_re, g_b_im, g_c_re, g_c_im,
               g_d_skip, g_glu_b, g_final_g]
    small_m = [m_norm_g, m_pool_w, m_pool_scale, m_a_re, m_a_im, m_log_dt, m_b_re, m_b_im, m_c_re, m_c_im,
               m_d_skip, m_glu_b, m_final_g]
    small_v = [v_norm_g, v_pool_w, v_pool_scale, v_a_re, v_a_im, v_log_dt, v_b_re, v_b_im, v_c_re, v_c_im,
               v_d_skip, v_glu_b, v_final_g]

    wide_last = {"b_re", "b_im"}

    def blocked(arrays):
        return [a.reshape(1, 1, -1) if n == "final_g" else a[:, None, :] if n in rows
                else a.swapaxes(2, 3) if n in wide_last else a for n, a in zip(names, arrays)]

    small_d, small_nm, small_nv = _adamw_small(blocked(small_w), blocked(small_g), blocked(small_m), blocked(small_v))
    res = {}
    for kind, arrays in (("grad", small_g), ("delta", small_d), ("m", small_nm), ("v", small_nv)):
        for n, a, like in zip(names, arrays, small_w):
            if kind != "grad" and n in wide_last:
                a = a.swapaxes(2, 3)
            res[kind, n] = a.reshape(like.shape)

    sent[0], received[0] = (s_win, s_glu, s_wout), (r_win, r_glu, r_wout)
    shard_adamw("w_in", 0)
    for n in ("w_in", "glu_w", "w_out"):
        for pos, kind in enumerate(("grad", "delta", "m", "v")):
            res[kind, n] = shard_res[n][pos]

    order = ["norm_g", "w_in", "pool_w", "pool_scale", "a_re", "a_im", "log_dt", "b_re", "b_im", "c_re", "c_im",
             "d_skip", "glu_w", "glu_b", "w_out", "final_g"]
    outs = [loss, dx.reshape(nb, seq, D_MODEL)]
    for kind in ("grad", "delta", "m", "v"):
        outs += [res[kind, n] for n in order]
    return tuple(outs)
```

```python
import math

import jax
import jax.numpy as jnp
from jax import lax
from jax.experimental import pallas as pl
from jax.experimental.pallas import tpu as pltpu

F32 = jnp.float32
MXU_DTYPE = jnp.bfloat16

D_MODEL = 1024
MIX = 1024
POOL_W = 512
SSM_W = 512
N_POOL_G = 4
POOL_GC = 128
SSM_C = 16
SSM_P = 64
NORM_EPS = 1e-5
N_DEV = 8
W_IN_COLS = 2 * MIX // N_DEV

ADAM_LR = 0.001
ADAM_B1 = 0.9
ADAM_B2 = 0.999
ADAM_EPS = 1e-08
ADAM_WD = 0.01
ADAM_STEP = 10

SUBLANES = 8
LANES = 128
HALO = 16
STATE_ROWS = 8
STATE_COLS = 256
CHUNK_GROUPS = STATE_COLS // SSM_P
CHUNK_CH = CHUNK_GROUPS * SSM_C
T_BLK = 256
SCAN_UNROLL = 16
TM_FWD = 512
TM_BWD = 512
VMEM_LIMIT = 56 * 1024 * 1024

MESH = pl.DeviceIdType.MESH
VMEM_SPEC = pl.BlockSpec(memory_space=pltpu.VMEM)
ANY_SPEC = pl.BlockSpec(memory_space=pl.ANY)


def _mm(a, b):
    return jnp.dot(a, b, preferred_element_type=F32)


def _mm_tn(a, b):
    return lax.dot_general(a, b, (((0,), (0,)), ((), ())), preferred_element_type=F32)


def _mm_nt(a, b):
    return lax.dot_general(a, b, (((1,), (1,)), ((), ())), preferred_element_type=F32)


def _mx(a):
    return a.astype(MXU_DTYPE)


def _sigmoid(v):
    return 1.0 / (1.0 + jnp.exp(-v))


_GELU_C = math.sqrt(2.0 / math.pi)
_GELU_A = 0.044715


def _gelu_and_grad(y):
    th = jnp.tanh(_GELU_C * (y + _GELU_A * y * y * y))
    val = 0.5 * y * (1.0 + th)
    grad = 0.5 * (1.0 + th) + 0.5 * y * (1.0 - th * th) * (_GELU_C * (1.0 + 3.0 * _GELU_A * y * y))
    return val, grad


def _params(**kw):
    return pltpu.CompilerParams(vmem_limit_bytes=VMEM_LIMIT, **kw)


def _of_layer(layer, *shape):
    return pl.BlockSpec((None,) + shape, lambda i: (layer,) + (0,) * len(shape))


def _ssm_dense(a_re, a_im, log_dt, b_re, b_im, c_re, c_im):
    dt = jnp.exp(log_dt)[:, None]
    mag = jnp.exp(a_re * dt)
    ang = a_im * dt
    lb_re = mag * jnp.cos(ang)
    lb_im = mag * jnp.sin(ang)
    den = a_re * a_re + a_im * a_im
    n_re = lb_re - 1.0
    n_im = lb_im
    f_re = (n_re * a_re + n_im * a_im) / den
    f_im = (n_im * a_re - n_re * a_im) / den
    bb_re = f_re[..., None] * b_re - f_im[..., None] * b_im
    bb_im = f_re[..., None] * b_im + f_im[..., None] * b_re

    bb = jnp.stack([bb_re, bb_im], axis=0).reshape(2, STATE_ROWS, CHUNK_GROUPS, SSM_P, SSM_C)
    rb = bb.transpose(1, 4, 0, 2, 3).reshape(STATE_ROWS, SSM_C, 2 * STATE_COLS)
    cc = jnp.stack([c_re, -c_im], axis=0).reshape(2, STATE_ROWS, CHUNK_GROUPS, SSM_C, SSM_P)
    rc = cc.transpose(1, 3, 0, 2, 4).reshape(STATE_ROWS, SSM_C, 2 * STATE_COLS)
    return (lb_re.reshape(STATE_ROWS, STATE_COLS), lb_im.reshape(STATE_ROWS, STATE_COLS), rb, rc)


def _ssm_chunked(per_channel):
    row_group = jnp.arange(CHUNK_CH) // SSM_C
    col_group = (jnp.arange(2 * STATE_COLS) // SSM_P) % CHUNK_GROUPS
    own_group = (row_group[:, None] == col_group[None, :]).astype(F32)
    even = (jnp.arange(STATE_ROWS) % 2 == 0).astype(F32)[:, None, None]
    half = jnp.tile(per_channel, (1, CHUNK_GROUPS, 1)) * own_group
    return jnp.concatenate([half * even, half * (1.0 - even)], axis=1)


def _inproj_fwd(x2, g_rows, w_all, dep, layer):
    n = x2.shape[0]
    tm = TM_FWD

    def body(x_ref, g_ref, w_ref, dep_ref, z_ref, h_ref):
        x = x_ref[...]
        r = lax.rsqrt(jnp.mean(x * x, axis=-1, keepdims=True) + NORM_EPS)
        h = _mx(x * r * g_ref[...])
        h_ref[...] = h
        for d in range(N_DEV):
            z_ref[:, d * W_IN_COLS:(d + 1) * W_IN_COLS] = _mm(h, w_ref[d])

    return pl.pallas_call(
        body, name="inproj_fwd",
        grid=(n // tm,),
        in_specs=[pl.BlockSpec((tm, D_MODEL), lambda i: (i, 0)),
                  _of_layer(layer, 1, D_MODEL),
                  pl.BlockSpec((N_DEV, D_MODEL, W_IN_COLS), lambda i: (0, 0, 0)),
                  ANY_SPEC],
        out_specs=[pl.BlockSpec((tm, 2 * MIX), lambda i: (i, 0)),
                   pl.BlockSpec((tm, D_MODEL), lambda i: (i, 0))],
        out_shape=[jax.ShapeDtypeStruct((n, 2 * MIX), F32),
                   jax.ShapeDtypeStruct((n, D_MODEL), MXU_DTYPE)],
        compiler_params=_params(dimension_semantics=("arbitrary",)),
    )(x2, g_rows, w_all, dep)


def _loss_head(x2, tgt2, g_row):
    n = x2.shape[0]
    tm = TM_FWD

    def body(x_ref, t_ref, g_ref, dx_ref, loss_ref, dg_ref):
        @pl.when(pl.program_id(0) == 0)
        def _():
            loss_ref[...] = jnp.zeros_like(loss_ref)
            dg_ref[...] = jnp.zeros_like(dg_ref)

        x = x_ref[...]
        g = g_ref[...]
        r = lax.rsqrt(jnp.mean(x * x, axis=-1, keepdims=True) + NORM_EPS)
        xh = x * r
        e = xh * g - t_ref[...]
        loss_ref[...] += jnp.sum(jnp.sum(e * e, axis=-1, keepdims=True), axis=0, keepdims=True) * (0.5 / D_MODEL)
        dout = e * (1.0 / D_MODEL)
        dg_ref[...] += jnp.sum(dout * xh, axis=0, keepdims=True)
        gdy = dout * g
        dx_ref[...] = r * (gdy - xh * jnp.mean(xh * gdy, axis=-1, keepdims=True))

    return pl.pallas_call(
        body, name="loss_head",
        grid=(n // tm,),
        in_specs=[pl.BlockSpec((tm, D_MODEL), lambda i: (i, 0)),
                  pl.BlockSpec((tm, D_MODEL), lambda i: (i, 0)),
                  pl.BlockSpec((1, D_MODEL), lambda i: (0, 0))],
        out_specs=[pl.BlockSpec((tm, D_MODEL), lambda i: (i, 0)),
                   pl.BlockSpec((1, 1), lambda i: (0, 0)),
                   pl.BlockSpec((1, D_MODEL), lambda i: (0, 0))],
        out_shape=[jax.ShapeDtypeStruct((n, D_MODEL), F32),
                   jax.ShapeDtypeStruct((1, 1), F32),
                   jax.ShapeDtypeStruct((1, D_MODEL), F32)],
        compiler_params=_params(dimension_semantics=("arbitrary",)),
    )(x2, tgt2, g_row)


def _outproj_bwd(dx2, yg, w_out, dep):
    n = dx2.shape[0]
    tm = TM_BWD
    n_steps = n // tm

    def body(dx_ref, y_ref, w_ref, dep_ref, dy_ref, dw_ref, acc_ref):
        i = pl.program_id(0)

        @pl.when(i == 0)
        def _():
            acc_ref[...] = jnp.zeros_like(acc_ref)

        dxb = _mx(dx_ref[...])
        dy_ref[...] = _mm_nt(dxb, w_ref[...])
        acc_ref[...] += _mm_tn(y_ref[...], dxb)

        @pl.when(i == n_steps - 1)
        def _():
            dw_ref[...] = _mx(acc_ref[...])

    return pl.pallas_call(
        body, name="outproj_bwd",
        grid=(n_steps,),
        in_specs=[pl.BlockSpec((tm, D_MODEL), lambda i: (i, 0)),
                  pl.BlockSpec((tm, MIX), lambda i: (i, 0)),
                  pl.BlockSpec((MIX, D_MODEL), lambda i: (0, 0)),
                  ANY_SPEC],
        out_specs=[pl.BlockSpec((tm, MIX), lambda i: (i, 0)),
                   pl.BlockSpec((MIX, D_MODEL), lambda i: (0, 0))],
        out_shape=[jax.ShapeDtypeStruct((n, MIX), F32),
                   jax.ShapeDtypeStruct((MIX, D_MODEL), MXU_DTYPE)],
        scratch_shapes=[pltpu.VMEM((MIX, D_MODEL), F32)],
        compiler_params=_params(dimension_semantics=("arbitrary",)),
    )(dx2, yg, w_out, dep)


def _inproj_bwd(dz, h, x2, dx_in, g_rows, w_all, dep, layer):
    n = x2.shape[0]
    tm = TM_BWD
    n_steps = n // tm

    def body(dz_ref, h_ref, x_ref, dxi_ref, g_ref, w_ref, dep_ref, dxo_ref, dw_ref, dg_ref, acc_ref, wcat_ref):
        i = pl.program_id(0)

        @pl.when(i == 0)
        def _():
            acc_ref[...] = jnp.zeros_like(acc_ref)
            dg_ref[...] = jnp.zeros_like(dg_ref)
            for d in range(N_DEV):
                wcat_ref[:, d * W_IN_COLS:(d + 1) * W_IN_COLS] = w_ref[d]

        hb = h_ref[...]
        for d in range(N_DEV):
            acc_ref[d] += _mm_tn(hb, dz_ref[:, d * W_IN_COLS:(d + 1) * W_IN_COLS])
        dh = _mm_nt(dz_ref[...], wcat_ref[...])
        x = x_ref[...]
        r = lax.rsqrt(jnp.mean(x * x, axis=-1, keepdims=True) + NORM_EPS)
        xh = x * r
        dg_ref[...] += jnp.sum(dh * xh, axis=0, keepdims=True)
        gdy = dh * g_ref[...]
        dxo_ref[...] = dxi_ref[...] + r * (gdy - xh * jnp.mean(xh * gdy, axis=-1, keepdims=True))

        @pl.when(i == n_steps - 1)
        def _():
            dw_ref[...] = _mx(acc_ref[...])

    return pl.pallas_call(
        body, name="inproj_bwd",
        grid=(n_steps,),
        in_specs=[pl.BlockSpec((tm, 2 * MIX), lambda i: (i, 0)),
                  pl.BlockSpec((tm, D_MODEL), lambda i: (i, 0)),
                  pl.BlockSpec((tm, D_MODEL), lambda i: (i, 0)),
                  pl.BlockSpec((tm, D_MODEL), lambda i: (i, 0)),
                  _of_layer(layer, 1, D_MODEL),
                  pl.BlockSpec((N_DEV, D_MODEL, W_IN_COLS), lambda i: (0, 0, 0)),
                  ANY_SPEC],
        out_specs=[pl.BlockSpec((tm, D_MODEL), lambda i: (i, 0)),
                   pl.BlockSpec((N_DEV, D_MODEL, W_IN_COLS), lambda i: (0, 0, 0)),
                   pl.BlockSpec((1, D_MODEL), lambda i: (0, 0))],
        out_shape=[jax.ShapeDtypeStruct((n, D_MODEL), F32),
                   jax.ShapeDtypeStruct((N_DEV, D_MODEL, W_IN_COLS), MXU_DTYPE),
                   jax.ShapeDtypeStruct((1, D_MODEL), F32)],
        scratch_shapes=[pltpu.VMEM((N_DEV, D_MODEL, W_IN_COLS), F32),
                        pltpu.VMEM((D_MODEL, 2 * MIX), MXU_DTYPE)],
        compiler_params=_params(dimension_semantics=("arbitrary",)),
    )(dz, h, x2, dx_in, g_rows, w_all, dep)


def _inproj_bwd_dw(dz, h, dep):
    n = h.shape[0]
    tm = TM_BWD
    n_steps = n // tm

    def body(dz_ref, h_ref, dep_ref, dw_ref, acc_ref):
        i = pl.program_id(0)

        @pl.when(i == 0)
        def _():
            acc_ref[...] = jnp.zeros_like(acc_ref)

        hb = h_ref[...]
        for d in range(N_DEV):
            acc_ref[d] += _mm_tn(hb, dz_ref[:, d * W_IN_COLS:(d + 1) * W_IN_COLS])

        @pl.when(i == n_steps - 1)
        def _():
            dw_ref[...] = _mx(acc_ref[...])

    return pl.pallas_call(
        body, name="inproj_bwd_dw",
        grid=(n_steps,),
        in_specs=[pl.BlockSpec((tm, 2 * MIX), lambda i: (i, 0)),
                  pl.BlockSpec((tm, D_MODEL), lambda i: (i, 0)),
                  ANY_SPEC],
        out_specs=pl.BlockSpec((N_DEV, D_MODEL, W_IN_COLS), lambda i: (0, 0, 0)),
        out_shape=jax.ShapeDtypeStruct((N_DEV, D_MODEL, W_IN_COLS), MXU_DTYPE),
        scratch_shapes=[pltpu.VMEM((N_DEV, D_MODEL, W_IN_COLS), F32)],
        compiler_params=_params(dimension_semantics=("arbitrary",)),
    )(dz, h, dep)


def _inproj_bwd_dx(dz, x2, dx_in, g_rows, w_all, dep, layer):
    n = x2.shape[0]
    tm = TM_BWD

    def body(dz_ref, x_ref, dxi_ref, g_ref, w_ref, dep_ref, dxo_ref, dg_ref, wcat_ref):
        @pl.when(pl.program_id(0) == 0)
        def _():
            dg_ref[...] = jnp.zeros_like(dg_ref)
            for d in range(N_DEV):
                wcat_ref[:, d * W_IN_COLS:(d + 1) * W_IN_COLS] = w_ref[d]

        dh = _mm_nt(dz_ref[...], wcat_ref[...])
        x = x_ref[...]
        r = lax.rsqrt(jnp.mean(x * x, axis=-1, keepdims=True) + NORM_EPS)
        xh = x * r
        dg_ref[...] += jnp.sum(dh * xh, axis=0, keepdims=True)
        gdy = dh * g_ref[...]
        dxo_ref[...] = dxi_ref[...] + r * (gdy - xh * jnp.mean(xh * gdy, axis=-1, keepdims=True))

    return pl.pallas_call(
        body, name="inproj_bwd_dx",
        grid=(n // tm,),
        in_specs=[pl.BlockSpec((tm, 2 * MIX), lambda i: (i, 0)),
                  pl.BlockSpec((tm, D_MODEL), lambda i: (i, 0)),
                  pl.BlockSpec((tm, D_MODEL), lambda i: (i, 0)),
                  _of_layer(layer, 1, D_MODEL),
                  pl.BlockSpec((N_DEV, D_MODEL, W_IN_COLS), lambda i: (0, 0, 0)),
                  ANY_SPEC],
        out_specs=[pl.BlockSpec((tm, D_MODEL), lambda i: (i, 0)),
                   pl.BlockSpec((1, D_MODEL), lambda i: (0, 0))],
        out_shape=[jax.ShapeDtypeStruct((n, D_MODEL), F32),
                   jax.ShapeDtypeStruct((1, D_MODEL), F32)],
        scratch_shapes=[pltpu.VMEM((D_MODEL, 2 * MIX), MXU_DTYPE)],
        compiler_params=_params(dimension_semantics=("arbitrary",)),
    )(dz, x2, dx_in, g_rows, w_all, dep)


def _row_pos(t0, rows):
    return t0 + lax.broadcasted_iota(jnp.int32, (rows, LANES), 0)


def _pool_window_mean(upad, g, t0, t_blk):
    k = 2 << g
    w = upad
    sh = 1
    while sh < k:
        w = w + pltpu.roll(w, sh, 0)
        sh *= 2
    count = jnp.minimum(_row_pos(t0, t_blk) + 1, k).astype(F32)
    return w[HALO:] / count - upad[HALO:]


def _pool_window_bwd(qpad, g, t_blk):
    k = 2 << g
    n = t_blk + HALO
    w = qpad
    sh = 1
    while sh < k:
        w = w + pltpu.roll(w, n - sh, 0)
        sh *= 2
    return w[:t_blk]


class _StateBuf:
    def __init__(self, refs, t_blk):
        self.refs = refs
        self.t_blk = t_blk

    def put_chunk(self, b, j, val):
        for c in range(4):
            self.refs[4 * b + c][pl.ds(j, self.t_blk, stride=STATE_ROWS), :] = val[:, c * LANES:(c + 1) * LANES]

    def get_chunk(self, b, j):
        return jnp.concatenate(
            [self.refs[4 * b + c][pl.ds(j, self.t_blk, stride=STATE_ROWS), :] for c in range(4)], axis=-1)

    def load(self, b, r, part):
        return jnp.concatenate(
            [self.refs[4 * b + 2 * part + h][pl.ds(r, STATE_ROWS), :] for h in range(2)], axis=-1)

    def store(self, b, r, part, val):
        for h in range(2):
            self.refs[4 * b + 2 * part + h][pl.ds(r, STATE_ROWS), :] = val[:, h * LANES:(h + 1) * LANES]


def _state_scratch(nb, t_blk):
    return [pltpu.VMEM((t_blk * STATE_ROWS, LANES), F32) for _ in range(4 * nb)]


def _ssm_project_in(u_ssm, wb_ref, buf, nb):
    t_blk = u_ssm.shape[0] // nb
    ub = _mx(u_ssm)
    for j in range(STATE_ROWS):
        m = j // 2
        bu = _mm(ub[:, m * LANES:(m + 1) * LANES], wb_ref[j])
        for b in range(nb):
            buf.put_chunk(b, j, bu[b * t_blk:(b + 1) * t_blk])


def _scan_forward(buf, lbr, lbi, init, nb):
    def step(t, carry):
        r = pl.multiple_of(t * STATE_ROWS, STATE_ROWS)
        out = []
        for b in range(nb):
            sr, si = carry[2 * b], carry[2 * b + 1]
            nr = lbr * sr - lbi * si + buf.load(b, r, 0)
            ni = lbr * si + lbi * sr + buf.load(b, r, 1)
            buf.store(b, r, 0, nr)
            buf.store(b, r, 1, ni)
            out += [nr, ni]
        return tuple(out)

    def body(i, carry):
        for u in range(SCAN_UNROLL):
            carry = step(i * SCAN_UNROLL + u, carry)
        return carry

    return lax.fori_loop(0, buf.t_blk // SCAN_UNROLL, body, init)


def _ssm_project_out(chunk, wc_ref):
    tiles = []
    for m in range(4):
        acc = None
        for j in (2 * m, 2 * m + 1):
            part = _mm_nt(chunk(j), wc_ref[j])
            acc = part if acc is None else acc + part
        tiles.append(acc)
    return jnp.concatenate(tiles, axis=-1)


def _layer_fwd(x3, z3, g_rows, w_in, pool_w, pool_scale, lbr, lbi, wb, wc, d_skip, glu_w, glu_b, w_out, dep, layer):
    nb, seq, _ = x3.shape
    t_blk = min(T_BLK, seq)
    n_t = seq // t_blk
    halo_per_blk = t_blk // HALO
    rows = nb * t_blk
    fused = z3 is None

    def body(*refs):
        if fused:
            (x_ref, g_ref, wi_ref, pw_ref, ps_ref, lbr_ref, lbi_ref, wb_ref, wc_ref, dsk_ref, gw_ref, gb_ref, wo_ref,
             dep_ref, z_ref, h_ref, yg_ref, sc_ref, act_ref, dact_ref, pooled_ref, ypre_ref, xo_ref,
             carry_ref, halo_ref, *s_refs) = refs
        else:
            (x_ref, z_ref, zh_ref, pw_ref, ps_ref, lbr_ref, lbi_ref, wb_ref, wc_ref, dsk_ref, gw_ref, gb_ref, wo_ref,
             dep_ref, yg_ref, sc_ref, act_ref, dact_ref, pooled_ref, ypre_ref, xo_ref, carry_ref, *s_refs) = refs
        i = pl.program_id(0)
        t0 = i * t_blk
        buf = _StateBuf(s_refs, t_blk)
        both = lambda lo, hi: z_ref[:, :, lo:hi].reshape(rows, hi - lo)

        @pl.when(i == 0)
        def _():
            carry_ref[...] = jnp.zeros_like(carry_ref)
            if fused:
                halo_ref[...] = jnp.zeros_like(halo_ref)

        x = x_ref[...].reshape(rows, D_MODEL)
        if fused:
            r = lax.rsqrt(jnp.mean(x * x, axis=-1, keepdims=True) + NORM_EPS)
            h = _mx(x * r * g_ref[...])
            h_ref[...] = h.reshape(nb, t_blk, D_MODEL)
            for d in range(N_DEV):
                z_ref[:, :, d * W_IN_COLS:(d + 1) * W_IN_COLS] = _mm(h, wi_ref[d]).reshape(nb, t_blk, W_IN_COLS)

        u_ssm = both(POOL_W, MIX)
        _ssm_project_in(u_ssm, wb_ref, buf, nb)
        init = tuple(carry_ref[b, :, h * STATE_COLS:(h + 1) * STATE_COLS] for b in range(nb) for h in range(2))
        fin = _scan_forward(buf, lbr_ref[...], lbi_ref[...], init, nb)
        for b in range(nb):
            carry_ref[b, :, 0:STATE_COLS] = fin[2 * b]
            carry_ref[b, :, STATE_COLS:2 * STATE_COLS] = fin[2 * b + 1]

        def chunk(j):
            states = _mx(jnp.concatenate([buf.get_chunk(b, j) for b in range(nb)], axis=0))
            sc_ref[:, j] = states.reshape(nb, t_blk, 2 * STATE_COLS)
            return states

        y = _ssm_project_out(chunk, wc_ref) + dsk_ref[...] * u_ssm
        yg, dgelu = _gelu_and_grad(y)
        ygb = _mx(yg)
        act_ref[...] = ygb.reshape(nb, t_blk, SSM_W)
        dact_ref[...] = _mx(dgelu).reshape(nb, t_blk, SSM_W)
        o_ssm = yg * _sigmoid(_mm(ygb, gw_ref[...]) + gb_ref[...])
        gp = both(MIX + POOL_W, 2 * MIX)
        parts = []
        first = (i == 0)
        for g in range(N_POOL_G):
            cols = slice(g * POOL_GC, (g + 1) * POOL_GC)
            pooled = []
            for b in range(nb):
                halo = halo_ref[b, :, cols] if fused else jnp.where(first, 0.0, zh_ref[b, :, cols])
                pooled.append(_pool_window_mean(jnp.concatenate([halo, z_ref[b, :, cols]], axis=0), g, t0, t_blk))
            pb = _mx(jnp.concatenate(pooled, axis=0))
            ypre = _mm(pb, pw_ref[g])
            pooled_ref[:, :, cols] = pb.reshape(nb, t_blk, POOL_GC)
            ypre_ref[:, :, cols] = ypre.reshape(nb, t_blk, POOL_GC)
            gpp = both(MIX + g * POOL_GC, MIX + (g + 1) * POOL_GC)
            parts.append(_mx(ypre * ps_ref[:, cols] * (gpp * _sigmoid(gpp))))
        parts.append(_mx(o_ssm * (gp * _sigmoid(gp))))
        gated = jnp.concatenate(parts, axis=-1)
        yg_ref[...] = gated.reshape(nb, t_blk, MIX)
        xo_ref[...] = (x + _mm(gated, wo_ref[...])).reshape(nb, t_blk, D_MODEL)
        if fused:
            halo_ref[...] = z_ref[:, t_blk - HALO:, 0:POOL_W]

    const = lambda *shape: pl.BlockSpec(shape, lambda i: (0,) * len(shape))
    tokens = lambda width: pl.BlockSpec((nb, t_blk, width), lambda i: (0, i, 0))
    mixer_specs = [_of_layer(layer, N_POOL_G, POOL_GC, POOL_GC), _of_layer(layer, 1, POOL_W),
                   _of_layer(layer, STATE_ROWS, STATE_COLS), _of_layer(layer, STATE_ROWS, STATE_COLS),
                   _of_layer(layer, STATE_ROWS, LANES, 2 * STATE_COLS),
                   _of_layer(layer, STATE_ROWS, LANES, 2 * STATE_COLS),
                   _of_layer(layer, 1, SSM_W), const(SSM_W, SSM_W), _of_layer(layer, 1, SSM_W),
                   const(MIX, D_MODEL), ANY_SPEC]
    mixer_args = (pool_w, pool_scale, lbr, lbi, wb, wc, d_skip, glu_w, glu_b, w_out, dep)
    out_specs = [tokens(MIX), pl.BlockSpec((nb, STATE_ROWS, t_blk, 2 * STATE_COLS), lambda i: (0, 0, i, 0)),
                 tokens(SSM_W), tokens(SSM_W), tokens(POOL_W), tokens(POOL_W), tokens(D_MODEL)]
    out_shape = [jax.ShapeDtypeStruct((nb, seq, MIX), MXU_DTYPE),
                 jax.ShapeDtypeStruct((nb, STATE_ROWS, seq, 2 * STATE_COLS), MXU_DTYPE),
                 jax.ShapeDtypeStruct((nb, seq, SSM_W), MXU_DTYPE),
                 jax.ShapeDtypeStruct((nb, seq, SSM_W), MXU_DTYPE),
                 jax.ShapeDtypeStruct((nb, seq, POOL_W), MXU_DTYPE),
                 jax.ShapeDtypeStruct((nb, seq, POOL_W), F32),
                 jax.ShapeDtypeStruct((nb, seq, D_MODEL), F32)]
    scratch = [pltpu.VMEM((nb, STATE_ROWS, 2 * STATE_COLS), F32)]
    if fused:
        in_specs = [tokens(D_MODEL), _of_layer(layer, 1, D_MODEL), const(N_DEV, D_MODEL, W_IN_COLS)] + mixer_specs
        args = (x3, g_rows, w_in) + mixer_args
        out_specs = [tokens(2 * MIX), tokens(D_MODEL)] + out_specs
        out_shape = [jax.ShapeDtypeStruct((nb, seq, 2 * MIX), F32),
                     jax.ShapeDtypeStruct((nb, seq, D_MODEL), MXU_DTYPE)] + out_shape
        scratch = scratch + [pltpu.VMEM((nb, HALO, POOL_W), F32)]
    else:
        in_specs = [tokens(D_MODEL), tokens(2 * MIX),
                    pl.BlockSpec((nb, HALO, POOL_W), lambda i: (0, jnp.maximum(i * halo_per_blk - 1, 0), 0))] + mixer_specs
        args = (x3, z3, z3) + mixer_args
    return pl.pallas_call(
        body, name="layer_fwd" if fused else "mixer_fwd",
        grid=(n_t,),
        in_specs=in_specs, out_specs=out_specs, out_shape=out_shape,
        scratch_shapes=scratch + _state_scratch(nb, t_blk),
        compiler_params=_params(dimension_semantics=("arbitrary",)),
    )(*args)


def _mixer_bwd(z3, dy3, states, kept, pool_w, pool_scale, lbr, lbi, wb, wc, d_skip, glu_w, glu_b, layer):
    nb, seq, _ = z3.shape
    t_blk = min(T_BLK, seq)
    n_t = seq // t_blk
    halo_per_blk = t_blk // HALO
    rows = nb * t_blk

    def body(z_ref, dy_ref, sc_ref, sch_ref, act_ref, dact_ref, pooled_ref, ypre_ref, pw_ref, ps_ref, lbr_ref, lbi_ref, wb_ref, wc_ref, dsk_ref,
             gw_ref, gb_ref,
             dz_ref, dpw_ref, dps_ref, dlbr_ref, dlbi_ref, dwb_ref, dwc_ref, ddsk_ref, dgw_ref, dgb_ref,
             gcarry_ref, qcarry_ref, du_ref, dgw_acc, *g_refs):
        i = pl.program_id(0)
        blk = n_t - 1 - i
        t0 = blk * t_blk
        gbuf = _StateBuf(g_refs, t_blk)

        @pl.when(i == 0)
        def _():
            gcarry_ref[...] = jnp.zeros_like(gcarry_ref)
            qcarry_ref[...] = jnp.zeros_like(qcarry_ref)
            for ref in (dpw_ref, dps_ref, dlbr_ref, dlbi_ref, dwb_ref, dwc_ref, ddsk_ref, dgw_acc, dgb_ref):
                ref[...] = jnp.zeros_like(ref)

        lbr_v = lbr_ref[...]
        lbi_v = lbi_ref[...]

        both = lambda ref, lo, hi: ref[:, :, lo:hi].reshape(rows, hi - lo)
        split = lambda val: val.reshape(nb, t_blk, val.shape[-1])
        states = lambda j: sc_ref[:, j].reshape(rows, 2 * STATE_COLS)
        first = (blk == 0)

        u_ssm = both(z_ref, POOL_W, MIX)
        ygb = act_ref[...].reshape(rows, SSM_W)
        yg = ygb.astype(F32)
        dgelu = dact_ref[...].reshape(rows, SSM_W).astype(F32)
        sg = _sigmoid(_mm(ygb, gw_ref[...]) + gb_ref[...])
        o_ssm = yg * sg
        gp = both(z_ref, MIX + POOL_W, 2 * MIX)
        sgm = _sigmoid(gp)
        dyv = both(dy_ref, POOL_W, MIX)
        dz_ref[:, :, MIX + POOL_W:2 * MIX] = split(_mx(dyv * o_ssm * (sgm * (1.0 + gp * (1.0 - sgm)))))
        do = dyv * (gp * sgm)
        dv = do * yg * (sg * (1.0 - sg))
        dvb = _mx(dv)
        dgb_ref[...] += jnp.sum(dv, axis=0, keepdims=True)
        dgw_acc[...] += _mm_tn(ygb, dvb)
        dyp = (do * sg + _mm_nt(dvb, gw_ref[...])) * dgelu
        ddsk_ref[...] += jnp.sum(dyp * u_ssm, axis=0, keepdims=True)
        dypb = _mx(dyp)
        for j in range(STATE_ROWS):
            m = j // 2
            dyt = dypb[:, m * LANES:(m + 1) * LANES]
            ds = _mm(dyt, wc_ref[j])
            for b in range(nb):
                gbuf.put_chunk(b, j, ds[b * t_blk:(b + 1) * t_blk])
            dwc_ref[j] += _mm_tn(dyt, states(j))
        du_ref[...] = split(dsk_ref[...] * dyp)

        for g in range(N_POOL_G):
            cols = slice(g * POOL_GC, (g + 1) * POOL_GC)
            pb = both(pooled_ref, g * POOL_GC, (g + 1) * POOL_GC)
            ypre = both(ypre_ref, g * POOL_GC, (g + 1) * POOL_GC)
            gpp = both(z_ref, MIX + g * POOL_GC, MIX + (g + 1) * POOL_GC)
            sgp = _sigmoid(gpp)
            dyg = both(dy_ref, g * POOL_GC, (g + 1) * POOL_GC)
            scale = ps_ref[:, cols]
            dz_ref[:, :, MIX + g * POOL_GC:MIX + (g + 1) * POOL_GC] = split(_mx(
                dyg * (ypre * scale) * (sgp * (1.0 + gpp * (1.0 - sgp)))))
            dyc = dyg * (gpp * sgp)
            dps_ref[:, cols] += jnp.sum(dyc * ypre, axis=0, keepdims=True)
            dypre = _mx(dyc * scale)
            dpw_ref[g] += _mm_tn(pb, dypre)
            dpooled = _mm_nt(dypre, pw_ref[g])
            count = jnp.minimum(_row_pos(t0, t_blk) + 1, 2 << g).astype(F32)
            for b in range(nb):
                dp = dpooled[b * t_blk:(b + 1) * t_blk]
                q = dp / count
                qpad = jnp.concatenate([q, qcarry_ref[b, :, cols]], axis=0)
                qcarry_ref[b, :, cols] = q[:HALO]
                dz_ref[b, :, cols] = _mx(_pool_window_bwd(qpad, g, t_blk) - dp)

        def rev_step(t, carry):
            r = pl.multiple_of(t * STATE_ROWS, STATE_ROWS)
            out = []
            for b in range(nb):
                gr, gi = carry[2 * b], carry[2 * b + 1]
                ngr = lbr_v * gr + lbi_v * gi + gbuf.load(b, r, 0)
                ngi = lbr_v * gi - lbi_v * gr + gbuf.load(b, r, 1)
                gbuf.store(b, r, 0, ngr)
                gbuf.store(b, r, 1, ngi)
                out += [ngr, ngi]
            return tuple(out)

        def rev_body(i, carry):
            for u in range(SCAN_UNROLL):
                carry = rev_step(t_blk - 1 - (i * SCAN_UNROLL + u), carry)
            return carry

        init_g = tuple(gcarry_ref[b, :, h * STATE_COLS:(h + 1) * STATE_COLS] for b in range(nb) for h in range(2))
        fin = lax.fori_loop(0, t_blk // SCAN_UNROLL, rev_body, init_g)
        for b in range(nb):
            gcarry_ref[b, :, 0:STATE_COLS] = fin[2 * b]
            gcarry_ref[b, :, STATE_COLS:2 * STATE_COLS] = fin[2 * b + 1]

        ub = _mx(u_ssm)
        for m in range(4):
            acc = both(du_ref, m * LANES, (m + 1) * LANES)
            for j in (2 * m, 2 * m + 1):
                g = jnp.concatenate([gbuf.get_chunk(b, j) for b in range(nb)], axis=0)
                gj = _mx(g)
                acc = acc + _mm_nt(gj, wb_ref[j])
                dwb_ref[j] += _mm_tn(ub[:, m * LANES:(m + 1) * LANES], gj)
                shifted = []
                for b in range(nb):
                    before = jnp.where(first, 0.0, sch_ref[b, j].astype(F32))
                    spad = jnp.concatenate([before, sc_ref[b, j].astype(F32)], axis=0)
                    shifted.append(pltpu.roll(spad, 1, 0)[HALO:])
                s_prev = jnp.concatenate(shifted, axis=0)
                g_re, g_im = g[:, :STATE_COLS], g[:, STATE_COLS:]
                p_re, p_im = s_prev[:, :STATE_COLS], s_prev[:, STATE_COLS:]
                dlbr_ref[j:j + 1, :] += jnp.sum(g_re * p_re + g_im * p_im, axis=0, keepdims=True)
                dlbi_ref[j:j + 1, :] += jnp.sum(g_im * p_re - g_re * p_im, axis=0, keepdims=True)
            dz_ref[:, :, POOL_W + m * LANES:POOL_W + (m + 1) * LANES] = split(_mx(acc))

        @pl.when(i == n_t - 1)
        def _():
            dgw_ref[...] = _mx(dgw_acc[...])

    const = lambda *shape: pl.BlockSpec(shape, lambda i: (0,) * len(shape))
    rev = lambda i: n_t - 1 - i
    out_shape = [jax.ShapeDtypeStruct((nb, seq, 2 * MIX), MXU_DTYPE),
                 jax.ShapeDtypeStruct((N_POOL_G, POOL_GC, POOL_GC), F32),
                 jax.ShapeDtypeStruct((1, POOL_W), F32),
                 jax.ShapeDtypeStruct((STATE_ROWS, STATE_COLS), F32),
                 jax.ShapeDtypeStruct((STATE_ROWS, STATE_COLS), F32),
                 jax.ShapeDtypeStruct((STATE_ROWS, LANES, 2 * STATE_COLS), F32),
                 jax.ShapeDtypeStruct((STATE_ROWS, LANES, 2 * STATE_COLS), F32),
                 jax.ShapeDtypeStruct((1, SSM_W), F32),
                 jax.ShapeDtypeStruct((SSM_W, SSM_W), MXU_DTYPE),
                 jax.ShapeDtypeStruct((1, SSM_W), F32)]
    return pl.pallas_call(
        body, name="mixer_bwd",
        grid=(n_t,),
        in_specs=[pl.BlockSpec((nb, t_blk, 2 * MIX), lambda i: (0, rev(i), 0)),
                  pl.BlockSpec((nb, t_blk, MIX), lambda i: (0, rev(i), 0)),
                  pl.BlockSpec((nb, STATE_ROWS, t_blk, 2 * STATE_COLS), lambda i: (0, 0, rev(i), 0)),
                  pl.BlockSpec((nb, STATE_ROWS, HALO, 2 * STATE_COLS),
                               lambda i: (0, 0, jnp.maximum(rev(i) * halo_per_blk - 1, 0), 0)),
                  pl.BlockSpec((nb, t_blk, SSM_W), lambda i: (0, rev(i), 0)),
                  pl.BlockSpec((nb, t_blk, SSM_W), lambda i: (0, rev(i), 0)),
                  pl.BlockSpec((nb, t_blk, POOL_W), lambda i: (0, rev(i), 0)),
                  pl.BlockSpec((nb, t_blk, POOL_W), lambda i: (0, rev(i), 0)),
                  _of_layer(layer, N_POOL_G, POOL_GC, POOL_GC), _of_layer(layer, 1, POOL_W),
                  _of_layer(layer, STATE_ROWS, STATE_COLS), _of_layer(layer, STATE_ROWS, STATE_COLS),
                  _of_layer(layer, STATE_ROWS, LANES, 2 * STATE_COLS),
                  _of_layer(layer, STATE_ROWS, LANES, 2 * STATE_COLS),
                  _of_layer(layer, 1, SSM_W), const(SSM_W, SSM_W), _of_layer(layer, 1, SSM_W)],
        out_specs=[pl.BlockSpec((nb, t_blk, 2 * MIX), lambda i: (0, rev(i), 0))]
                  + [const(*s.shape) for s in out_shape[1:]],
        out_shape=out_shape,
        scratch_shapes=[pltpu.VMEM((nb, STATE_ROWS, 2 * STATE_COLS), F32),
                        pltpu.VMEM((nb, HALO, POOL_W), F32),
                        pltpu.VMEM((nb, t_blk, SSM_W), F32),
                        pltpu.VMEM((SSM_W, SSM_W), F32)]
                       + _state_scratch(nb, t_blk),
        compiler_params=_params(dimension_semantics=("arbitrary",)),
    )(z3, dy3, states, states, *kept, pool_w, pool_scale, lbr, lbi, wb, wc, d_skip, glu_w, glu_b)


def _mesh_place():
    x, y, c = lax.axis_index("x"), lax.axis_index("y"), lax.axis_index("c")
    return x, y, c


def _flip(place, k):
    x, y, c = place
    return (1 - x if k & 4 else x, 1 - y if k & 2 else y, 1 - c if k & 1 else c)


def _index(place):
    x, y, c = place
    return 4 * x + 2 * y + c


HBM_SPEC = pl.BlockSpec(memory_space=pltpu.HBM)
SEM_SPEC = pl.BlockSpec(memory_space=pltpu.SEMAPHORE)
_EFFECT = pltpu.SideEffectType.DATAFLOW_SIDE_EFFECTING
N_PEERS = N_DEV - 1


def _exchange_copies(src_refs, land_refs, send_sems, recv_sems):
    me = _mesh_place()
    mine = _index(me)
    out = []
    for a, land_ref in enumerate(land_refs):
        for k in range(1, N_DEV):
            peer = _flip(me, k)
            theirs = _index(peer)
            n = a * N_PEERS + k - 1
            src = src_refs[a].at[theirs] if src_refs else land_ref.at[mine]
            send = pltpu.make_async_remote_copy(
                src_ref=src, dst_ref=land_ref.at[mine], send_sem=send_sems.at[n], recv_sem=recv_sems.at[n],
                device_id=peer, device_id_type=MESH)
            recv = pltpu.make_async_remote_copy(
                src_ref=src, dst_ref=land_ref.at[theirs], send_sem=send_sems.at[n], recv_sem=recv_sems.at[n],
                device_id=peer, device_id_type=MESH)
            out.append((send, recv))
    return out


def _exchange_start(srcs, lands, after, name):
    arrays = tuple(srcs) + tuple(lands)
    n_src, n_all = len(srcs), len(arrays)
    n_copies = len(lands) * N_PEERS

    def body(*refs):
        send_sems, recv_sems = refs[n_all + 1], refs[n_all + 2]
        token = refs[-1]
        for send, _ in _exchange_copies(refs[:n_src], refs[n_src:n_all], send_sems, recv_sems):
            send.start()
        token[...] = jnp.zeros_like(token)

    res = pl.pallas_call(
        body, name=name,
        in_specs=[HBM_SPEC] * n_all + [ANY_SPEC],
        out_specs=[SEM_SPEC, SEM_SPEC] + [HBM_SPEC] * n_all + [VMEM_SPEC],
        out_shape=[pltpu.SemaphoreType.DMA((n_copies,)), pltpu.SemaphoreType.DMA((n_copies,))]
                  + [pltpu.HBM(a.shape, a.dtype) for a in arrays] + [jax.ShapeDtypeStruct((SUBLANES, LANES), F32)],
        input_output_aliases={i: 2 + i for i in range(n_all)},
        compiler_params=pltpu.CompilerParams(has_side_effects=_EFFECT),
    )(*[pltpu.with_memory_space_constraint(a, pltpu.HBM) for a in arrays], after)
    return tuple(res[:-1]), res[-1]


def _exchange_wait(handle, n_lands, after, name):
    send_sems, recv_sems = handle[0], handle[1]
    arrays = handle[2:]
    n_all = len(arrays)
    n_src = n_all - n_lands

    def body(*refs):
        for send, recv in _exchange_copies(refs[:n_src], refs[n_src:n_all], refs[n_all], refs[n_all + 1]):
            send.wait_send()
            recv.wait_recv()

    res = pl.pallas_call(
        body, name=name,
        in_specs=[HBM_SPEC] * n_all + [SEM_SPEC, SEM_SPEC, ANY_SPEC],
        out_specs=[HBM_SPEC] * n_all,
        out_shape=[pltpu.HBM(a.shape, a.dtype) for a in arrays],
        input_output_aliases={i: i for i in range(n_all)},
        compiler_params=pltpu.CompilerParams(has_side_effects=_EFFECT),
    )(*arrays, send_sems, recv_sems, after)
    return tuple(res[:n_src]), tuple(res[n_src:])


def _weight_zones(w_in, glu_w, w_out, my_idx):
    shards = (w_in, glu_w, w_out)
    depth = w_in.shape[0]

    def body(idx_ref, *refs):
        ins, zones = refs[:len(shards)], refs[len(shards):]
        for l in range(depth):
            for a, src in enumerate(ins):
                zones[l * len(shards) + a][0] = _mx(src[l])

    whole = lambda s: pl.BlockSpec(s.shape, lambda i, idx: (0,) * s.ndim)
    return pl.pallas_call(
        body, name="weight_zones",
        grid_spec=pltpu.PrefetchScalarGridSpec(
            num_scalar_prefetch=1, grid=(1,),
            in_specs=[whole(s) for s in shards],
            out_specs=[pl.BlockSpec((1,) + s.shape[1:], lambda i, idx: (idx[0], 0, 0))
                       for _ in range(depth) for s in shards]),
        out_shape=[jax.ShapeDtypeStruct((N_DEV,) + s.shape[1:], MXU_DTYPE) for _ in range(depth) for s in shards],
        compiler_params=_params(dimension_semantics=("arbitrary",)),
    )(my_idx.reshape(1).astype(jnp.int32), *shards)


def _allreduce_parts(parts):
    n_parts = len(parts)
    pieces = [(a, i, k) for i, a in enumerate(parts) for k in range(a.shape[1] // LANES)]
    offsets = [0]
    for a, _, _ in pieces:
        offsets.append(offsets[-1] + a.shape[0])
    rows = offsets[-1]
    assert rows % _PACK_ROWS == 0, rows
    half = rows // 2
    quarter = half // 4

    def body(*refs):
        in_refs, out_refs = refs[:n_parts], refs[n_parts:2 * n_parts]
        p_ref, o_ref, part_ref, sib_ref, got_ref, send_sems, recv_sems = refs[2 * n_parts:]
        for (a, i, k), off in zip(pieces, offsets):
            p_ref[off:off + a.shape[0], :] = in_refs[i][:, k * LANES:(k + 1) * LANES]

        x, y, c = _mesh_place()
        sibling = (x, y, 1 - c)
        chip = 2 * x + y
        chips = [(k, (1 - x if k & 2 else x, 1 - y if k & 1 else y, c), chip ^ k) for k in (1, 2, 3)]
        my_half = pl.multiple_of(c * half, SUBLANES)
        other_half = pl.multiple_of((1 - c) * half, SUBLANES)

        def copy(n, src, dst, to):
            return pltpu.make_async_remote_copy(src_ref=src, dst_ref=dst, send_sem=send_sems.at[n],
                                                recv_sem=recv_sems.at[n], device_id=to, device_id_type=MESH)

        def quarter_of(ref, base, q):
            return ref.at[pl.ds(pl.multiple_of(base + q * quarter, SUBLANES), quarter)]

        swap = copy(0, p_ref.at[pl.ds(other_half, half)], sib_ref, sibling)
        swap.start()
        swap.wait()
        part_ref[...] = p_ref[pl.ds(my_half, half), :] + sib_ref[...]

        scatter = [copy(k, quarter_of(part_ref, 0, q), got_ref.at[k - 1], to) for k, to, q in chips]
        for cp in scatter:
            cp.start()
        total = part_ref[pl.ds(pl.multiple_of(chip * quarter, SUBLANES), quarter), :]
        for cp, (k, _, _) in zip(scatter, chips):
            cp.wait()
            total = total + got_ref[k - 1]
        mine = pl.multiple_of(my_half + chip * quarter, SUBLANES)
        o_ref[pl.ds(mine, quarter), :] = total

        gather = [copy(3 + k, o_ref.at[pl.ds(mine, quarter)], o_ref.at[pl.ds(mine, quarter)], to) for k, to, _ in chips]
        for cp in gather:
            cp.start()
        for k, to, q in chips:
            theirs = quarter_of(o_ref, my_half, q)
            copy(3 + k, theirs, theirs, to).wait_recv()
        for cp in gather:
            cp.wait_send()

        back = copy(7, o_ref.at[pl.ds(my_half, half)], o_ref.at[pl.ds(my_half, half)], sibling)
        back.start()
        copy(7, o_ref.at[pl.ds(other_half, half)], o_ref.at[pl.ds(other_half, half)], sibling).wait_recv()
        back.wait_send()

        for (a, i, k), off in zip(pieces, offsets):
            out_refs[i][:, k * LANES:(k + 1) * LANES] = o_ref[off:off + a.shape[0], :]

    return pl.pallas_call(
        body, name="comm_allreduce_parts",
        in_specs=[VMEM_SPEC] * n_parts,
        out_specs=[VMEM_SPEC] * n_parts,
        out_shape=[jax.ShapeDtypeStruct(a.shape, F32) for a in parts],
        scratch_shapes=[pltpu.VMEM((rows, LANES), F32),
                        pltpu.VMEM((rows, LANES), F32),
                        pltpu.VMEM((half, LANES), F32),
                        pltpu.VMEM((half, LANES), F32),
                        pltpu.VMEM((3, quarter, LANES), F32),
                        pltpu.SemaphoreType.DMA((8,)),
                        pltpu.SemaphoreType.DMA((8,))],
        compiler_params=_params(),
    )(*parts)


def _adamw_math(w, g, m, v):
    m = ADAM_B1 * m + (1.0 - ADAM_B1) * g
    v = ADAM_B2 * v + (1.0 - ADAM_B2) * (g * g)
    m_hat = m / (1.0 - ADAM_B1 ** ADAM_STEP)
    v_hat = v / (1.0 - ADAM_B2 ** ADAM_STEP)
    delta = -ADAM_LR * (m_hat / (jnp.sqrt(v_hat) + ADAM_EPS) + ADAM_WD * w)
    return delta, m, v


def _adamw_summed(received, own, my_idx, w, m, v, name):
    depth, r, c = w.shape
    tr = min(r, 128)

    def body(idx_ref, *refs):
        r_refs, o_refs = refs[:depth], refs[depth:2 * depth]
        w_ref, m_ref, v_ref, g_ref, d_ref, nm_ref, nv_ref = refs[2 * depth:]
        me = idx_ref[0]
        for l in range(depth):
            g = jnp.zeros((tr, c), F32)
            for q in range(N_DEV):
                g = g + jnp.where(q == me, o_refs[l][0], r_refs[l][q]).astype(F32)
            g_ref[l] = g
            d_ref[l], nm_ref[l], nv_ref[l] = _adamw_math(w_ref[l], g, m_ref[l], v_ref[l])

    blk = pl.BlockSpec((depth, tr, c), lambda i, idx: (0, i, 0))
    return pl.pallas_call(
        body, name=name,
        grid_spec=pltpu.PrefetchScalarGridSpec(
            num_scalar_prefetch=1, grid=(r // tr,),
            in_specs=[pl.BlockSpec((N_DEV, tr, c), lambda i, idx: (0, i, 0))] * depth
                     + [pl.BlockSpec((1, tr, c), lambda i, idx: (idx[0], i, 0))] * depth
                     + [blk, blk, blk],
            out_specs=[blk] * 4),
        out_shape=[jax.ShapeDtypeStruct((depth, r, c), F32)] * 4,
        compiler_params=_params(dimension_semantics=("arbitrary",)),
    )(my_idx.reshape(1).astype(jnp.int32), *received, *own, w, m, v)


def _adamw_small(ws, gs, ms, vs):
    n = len(ws)
    depth = ws[0].shape[0]

    def spec(a):
        per_layer = a.shape[0] == depth
        rest = (0,) * (a.ndim - 1)
        return pl.BlockSpec((1,) + a.shape[1:], lambda l: ((l if per_layer else 0),) + rest)

    def body(*refs):
        w_refs, g_refs, m_refs, v_refs = (refs[k * n:(k + 1) * n] for k in range(4))
        d_refs, nm_refs, nv_refs = (refs[(4 + k) * n:(5 + k) * n] for k in range(3))
        for k in range(n):
            d_refs[k][...], nm_refs[k][...], nv_refs[k][...] = _adamw_math(
                w_refs[k][...], g_refs[k][...], m_refs[k][...], v_refs[k][...])

    specs = [spec(a) for a in ws]
    shapes = [jax.ShapeDtypeStruct(a.shape, F32) for a in ws]
    res = pl.pallas_call(
        body, name="adamw_small",
        grid=(depth,),
        in_specs=specs * 4,
        out_specs=specs * 3,
        out_shape=shapes * 3,
        compiler_params=_params(dimension_semantics=("arbitrary",)),
    )(*ws, *gs, *ms, *vs)
    return res[:n], res[n:2 * n], res[2 * n:]


_PACK_ROWS = SUBLANES * N_DEV


def _pack(arrays):
    flat = jnp.concatenate([a.reshape(-1) for a in arrays])
    per = _PACK_ROWS * LANES
    total = -(-flat.shape[0] // per) * per
    flat = jnp.pad(flat, (0, total - flat.shape[0]))
    return flat.reshape(total // LANES, LANES)


def _unpack(packed, like):
    flat = packed.reshape(-1)
    out = []
    off = 0
    for a in like:
        out.append(flat[off:off + a.size].reshape(a.shape))
        off += a.size
    return out


def kernel(x, norm_g, w_in, pool_w, pool_scale, a_re, a_im, log_dt, b_re, b_im, c_re, c_im, d_skip, glu_w, glu_b, w_out, final_g, loss_target, m_norm_g, m_w_in, m_pool_w, m_pool_scale, m_a_re, m_a_im, m_log_dt, m_b_re, m_b_im, m_c_re, m_c_im, m_d_skip, m_glu_w, m_glu_b, m_w_out, m_final_g, v_norm_g, v_w_in, v_pool_w, v_pool_scale, v_a_re, v_a_im, v_log_dt, v_b_re, v_b_im, v_c_re, v_c_im, v_d_skip, v_glu_w, v_glu_b, v_w_out, v_final_g):
    nb, seq, _ = x.shape
    n_tok = nb * seq
    depth = norm_g.shape[0]

    my_idx = _index(_mesh_place())

    zones = _weight_zones(w_in, glu_w, w_out, my_idx)

    def gather_start(l, after):
        return _exchange_start((), zones[3 * l:3 * l + 3], after, f"comm_gather_start_{l}")

    def gather_wait(handle, after, l):
        _, (win, glu, wout) = _exchange_wait(handle, 3, after, f"comm_gather_wait_{l}")
        return win, glu.reshape(SSM_W, SSM_W), wout.reshape(MIX, D_MODEL)

    xs = [x.reshape(n_tok, D_MODEL)]
    first_w_in, dep = _exchange_start((), zones[0:1], xs[0], "comm_gather_start_0_w_in")

    (lbr, lbi, rb, rc), dense_vjp = jax.vjp(jax.vmap(_ssm_dense), a_re, a_im, log_dt + dep[0, 0], b_re, b_im, c_re, c_im)
    chunk_all = jax.vmap(_ssm_chunked)
    (wb, wct), chunk_vjp = jax.vjp(lambda p, q: (chunk_all(p), chunk_all(q)), rb, rc)
    wb_m, wct_m = _mx(wb), _mx(wct)
    pool_w_m = _mx(pool_w)
    rows_of = lambda a: a[:, None, :]
    norm_rows, scale_rows, skip_rows, bias_rows = rows_of(norm_g), rows_of(pool_scale), rows_of(d_skip), rows_of(glu_b)

    def layer_params(l):
        return (pool_w_m, scale_rows, lbr, lbi, wb_m, wct_m, skip_rows, weights[l][1], bias_rows)

    saved = []
    weights = []
    for l in range(depth):
        if l == 0:
            _, (win,) = _exchange_wait(first_w_in, 1, wct_m, "comm_gather_wait_0_w_in")
            rest, dep = _exchange_start((), zones[1:3], win, "comm_gather_start_0_rest")
            z, h = _inproj_fwd(xs[-1], norm_rows, win, dep, l)
            _, (glu, wout) = _exchange_wait(rest, 2, z, "comm_gather_wait_0_rest")
            weights.append((win, glu.reshape(SSM_W, SSM_W), wout.reshape(MIX, D_MODEL)))
            handle, dep = gather_start(1, weights[0][2])
            z3 = z.reshape(nb, seq, 2 * MIX)
            yg, states, *kept, x_next = _layer_fwd(xs[-1].reshape(nb, seq, D_MODEL), z3, None, None,
                                                   *layer_params(l), weights[l][2], dep, l)
        else:
            weights.append(gather_wait(handle, xs[-1], l))
            if l + 1 < depth:
                handle, dep = gather_start(l + 1, weights[l][0])
            z3, h3, yg, states, *kept, x_next = _layer_fwd(xs[-1].reshape(nb, seq, D_MODEL), None, norm_rows,
                                                           weights[l][0], *layer_params(l), weights[l][2], dep, l)
            h = h3.reshape(n_tok, D_MODEL)
        xs.append(x_next.reshape(n_tok, D_MODEL))
        saved.append((z3, h, yg.reshape(n_tok, MIX), states, kept))

    dx, loss_part, d_final_g = _loss_head(xs[-1], loss_target.reshape(n_tok, D_MODEL), final_g[None])

    small = {k: [None] * depth for k in
             ("norm_g", "pool_w", "pool_scale", "lbr", "lbi", "wb", "wct", "d_skip", "glu_b")}
    received = [None] * depth
    sent = [None] * depth
    pending = None
    early = None
    for l in reversed(range(depth)):
        z3, h, yg2, states, kept = saved[l]
        dy, d_wout = _outproj_bwd(dx, yg2, weights[l][2], dep)
        (dz, d_pw, d_ps, d_lbr, d_lbi, d_wb, d_wct, d_dsk, d_gw, d_gb) = _mixer_bwd(
            z3, dy.reshape(nb, seq, MIX), states, kept, *layer_params(l), l)
        rest = (d_gw.reshape(N_DEV, SSM_W // N_DEV, SSM_W), d_wout.reshape(N_DEV, MIX // N_DEV, D_MODEL))
        if l == 0:
            early, dep = _exchange_start(rest, tuple(lax.empty(s.shape, s.dtype) for s in rest), dz,
                                         "comm_grads_start_0_rest")
        dz2 = dz.reshape(n_tok, 2 * MIX)
        if l == 0:
            d_win = _inproj_bwd_dw(dz2, h, dep)
            done = d_win
        else:
            dx, d_win, d_ng = _inproj_bwd(dz2, h, xs[l], dx, norm_rows, weights[l][0], dep, l)
            done = dx
        if pending is not None:
            sent[l + 1], received[l + 1] = _exchange_wait(pending, 3, done, f"comm_grads_wait_{l + 1}")
        srcs = (d_win,) if l == 0 else (d_win,) + rest
        lands = tuple(lax.empty(s.shape, s.dtype) for s in srcs)
        pending, dep = _exchange_start(srcs, lands, done, f"comm_grads_start_{l}")
        if l == 0:
            dx, d_ng = _inproj_bwd_dx(dz2, xs[l], dx, norm_rows, weights[l][0], dep, l)
        for k, val in (("norm_g", d_ng[0]), ("pool_w", d_pw), ("pool_scale", d_ps[0]), ("lbr", d_lbr),
                       ("lbi", d_lbi), ("wb", d_wb), ("wct", d_wct), ("d_skip", d_dsk[0]), ("glu_b", d_gb[0])):
            small[k][l] = val
    shard_res = {}
    shard_inputs = {"w_in": (w_in, m_w_in, v_w_in), "glu_w": (glu_w, m_glu_w, v_glu_w), "w_out": (w_out, m_w_out, v_w_out)}

    def shard_adamw(n, pos):
        w, m, v = shard_inputs[n]
        shard_res[n] = _adamw_summed([received[l][pos] for l in range(depth)], [sent[l][pos] for l in range(depth)],
                                     my_idx, w, m, v, "adamw_" + n)

    (s_glu, s_wout), (r_glu, r_wout) = _exchange_wait(early, 2, dx, "comm_grads_wait_0_rest")
    sent[0], received[0] = (None, s_glu, s_wout), (None, r_glu, r_wout)
    shard_adamw("glu_w", 1)
    shard_adamw("w_out", 2)

    stack = lambda k: jnp.stack(small[k])
    d_rb, d_rc = chunk_vjp((stack("wb"), stack("wct")))
    few = [stack("norm_g"), stack("pool_scale"), stack("lbr"), stack("lbi"), stack("d_skip"), stack("glu_b"),
           d_final_g[0] + dep[0, 0], loss_part[0]]
    big = [stack("pool_w"), d_rb, d_rc]
    summed = _allreduce_parts([a.reshape(-1, a.shape[-1]) for a in big] + [_pack(few)])
    g_pool_w, g_rb, g_rc = (g.reshape(a.shape) for g, a in zip(summed, big))
    g_norm_g, g_pool_scale, g_lbr, g_lbi, g_d_skip, g_glu_b, g_final_g, loss = _unpack(summed[-1], few)
    loss = loss[0]
    g_a_re, g_a_im, g_log_dt, g_b_re, g_b_im, g_c_re, g_c_im = dense_vjp((g_lbr, g_lbi, g_rb, g_rc))

    names = ["norm_g", "pool_w", "pool_scale", "a_re", "a_im", "log_dt", "b_re", "b_im", "c_re", "c_im",
             "d_skip", "glu_b", "final_g"]
    rows = {"norm_g", "pool_scale", "log_dt", "d_skip", "glu_b"}
    small_w = [norm_g, pool_w, pool_scale, a_re, a_im, log_dt, b_re, b_im, c_re, c_im, d_skip, glu_b, final_g]
    small_g = [g_norm_g, g_pool_w, g_pool_scale, g_a_re, g_a_im, g_log_dt, g_b_re, g_b_im, g_c_re, g_c_im,
               g_d_skip, g_glu_b, g_final_g]
    small_m = [m_norm_g, m_pool_w, m_pool_scale, m_a_re, m_a_im, m_log_dt, m_b_re, m_b_im, m_c_re, m_c_im,
               m_d_skip, m_glu_b, m_final_g]
    small_v = [v_norm_g, v_pool_w, v_pool_scale, v_a_re, v_a_im, v_log_dt, v_b_re, v_b_im, v_c_re, v_c_im,
               v_d_skip, v_glu_b, v_final_g]

    wide_last = {"b_re", "b_im"}

    def blocked(arrays):
        return [a.reshape(1, 1, -1) if n == "final_g" else a[:, None, :] if n in rows
                else a.swapaxes(2, 3) if n in wide_last else a for n, a in zip(names, arrays)]

    small_d, small_nm, small_nv = _adamw_small(blocked(small_w), blocked(small_g), blocked(small_m), blocked(small_v))
    res = {}
    for kind, arrays in (("grad", small_g), ("delta", small_d), ("m", small_nm), ("v", small_nv)):
        for n, a, like in zip(names, arrays, small_w):
            if kind != "grad" and n in wide_last:
                a = a.swapaxes(2, 3)
            res[kind, n] = a.reshape(like.shape)

    (s_win,), (r_win,) = _exchange_wait(pending, 1, small_d[0], "comm_grads_wait_0")
    sent[0], received[0] = (s_win, s_glu, s_wout), (r_win, r_glu, r_wout)
    shard_adamw("w_in", 0)
    for n in ("w_in", "glu_w", "w_out"):
        for pos, kind in enumerate(("grad", "delta", "m", "v")):
            res[kind, n] = shard_res[n][pos]

    order = ["norm_g", "w_in", "pool_w", "pool_scale", "a_re", "a_im", "log_dt", "b_re", "b_im", "c_re", "c_im",
             "d_skip", "glu_w", "glu_b", "w_out", "final_g"]
    outs = [loss, dx.reshape(nb, seq, D_MODEL)]
    for kind in ("grad", "delta", "m", "v"):
        outs += [res[kind, n] for n in order]
    return tuple(outs)
```
